```python
import math
import jax
import jax.numpy as jnp
from jax import lax
import numpy as np


D_MODEL = 1024
BATCH = 8
SEQ = 8192
DEPTH = 1

GRID_W = 64
CTX_LEN = 256
S5_WIDTH = 512
S5_GROUP = 16
S5_GROUPS = S5_WIDTH // S5_GROUP
S5_STATE = 64
SGU_WIDTH = 512
SGU_GROUPS = 8
SGU_GROUP_DIM = SGU_WIDTH // SGU_GROUPS
CHUNK = 128
FFN_HIDDEN = 2816
CONV_K = 3
N_BRANCH = 2
IN_WIDTH = S5_WIDTH + 2 * SGU_WIDTH + N_BRANCH * D_MODEL
N_MOD = 6
EPS = 1e-6
DT_MIN = 1e-3
DT_MAX = 1e-1

kernel_name = 'hybrid_s5_sgu_convffn_block'


def rms_norm(x, g):
    x32 = x.astype(jnp.float32)
    y = x32 * lax.rsqrt(jnp.mean(x32 * x32, axis=-1, keepdims=True) + EPS)
    return y.astype(x.dtype) * g


def layer_norm(x, g, b):
    x32 = x.astype(jnp.float32)
    xc = x32 - jnp.mean(x32, axis=-1, keepdims=True)
    y = xc * lax.rsqrt(jnp.mean(xc * xc, axis=-1, keepdims=True) + EPS)
    return y.astype(x.dtype) * g + b


def modulate(x, g, shift, scale):
    return rms_norm(x, g) * (1 + scale) + shift


def s5_discretize(a_re, a_im, log_step, b_re, b_im):
    f32 = jnp.float32
    a_re = a_re.astype(f32)
    a_im = a_im.astype(f32)
    dt = jnp.exp(log_step.astype(f32))[:, None]
    mag = jnp.exp(a_re * dt)
    ab_re = mag * jnp.cos(a_im * dt)
    ab_im = mag * jnp.sin(a_im * dt)
    p = ab_re - 1.0
    q = ab_im
    den = a_re * a_re + a_im * a_im
    k_re = ((p * a_re + q * a_im) / den)[..., None]
    k_im = ((q * a_re - p * a_im) / den)[..., None]
    b_re = b_re.astype(f32)
    b_im = b_im.astype(f32)
    bb_re = k_re * b_re - k_im * b_im
    bb_im = k_re * b_im + k_im * b_re
    return ab_re, ab_im, bb_re, bb_im


def _linear_recurrence_combine(e1, e2):
    a1r, a1i, b1r, b1i = e1
    a2r, a2i, b2r, b2i = e2
    return (a2r * a1r - a2i * a1i,
            a2r * a1i + a2i * a1r,
            a2r * b1r - a2i * b1i + b2r,
            a2r * b1i + a2i * b1r + b2i)


def s5_scan(u, ab_re, ab_im, bb_re, bb_im, s0=None):
    bsz, length, _ = u.shape
    ug = u.astype(jnp.float32).reshape(bsz, length, S5_GROUPS, S5_GROUP)
    bu_re = jnp.einsum('blgh,gnh->blgn', ug, bb_re)
    bu_im = jnp.einsum('blgh,gnh->blgn', ug, bb_im)
    if s0 is not None:
        s0_re, s0_im = s0
        bu_re = bu_re.at[:, 0].add(ab_re * s0_re - ab_im * s0_im)
        bu_im = bu_im.at[:, 0].add(ab_re * s0_im + ab_im * s0_re)
    a_re = jnp.broadcast_to(ab_re, (1, length) + ab_re.shape)
    a_im = jnp.broadcast_to(ab_im, (1, length) + ab_im.shape)
    _, _, h_re, h_im = lax.associative_scan(
        _linear_recurrence_combine, (a_re, a_im, bu_re, bu_im), axis=1)
    return h_re, h_im


def s5_readout(h_re, h_im, c_re, c_im):
    y = (jnp.einsum('blgn,ghn->blgh', h_re, c_re.astype(jnp.float32))
         - jnp.einsum('blgn,ghn->blgh', h_im, c_im.astype(jnp.float32)))
    return y.reshape(y.shape[0], y.shape[1], S5_WIDTH)


def s5_glu(y, w, b):
    y = jax.nn.gelu(y)
    return y * jax.nn.sigmoid(y @ w + b)


def s5_mixer(u, u_ctx, a_re, a_im, log_step, b_re, b_im, c_re, c_im, d_skip, w_glu, b_glu, with_ctx_out):
    f32 = jnp.float32
    y = u.astype(f32) * d_skip.astype(f32)
    y_ctx = u_ctx.astype(f32) * d_skip.astype(f32) if with_ctx_out else None
    for direction in range(2):
        ab_re, ab_im, bb_re, bb_im = s5_discretize(
            a_re[direction], a_im[direction], log_step[direction], b_re[direction], b_im[direction])
        orient = (lambda t: jnp.flip(t, axis=1)) if direction == 1 else (lambda t: t)
        hc_re, hc_im = s5_scan(orient(u_ctx), ab_re, ab_im, bb_re, bb_im)
        h_re, h_im = s5_scan(orient(u), ab_re, ab_im, bb_re, bb_im,
                             s0=(hc_re[:, -1], hc_im[:, -1]))
        y = y + orient(s5_readout(h_re, h_im, c_re[direction], c_im[direction]))
        if with_ctx_out:
            y_ctx = y_ctx + orient(s5_readout(hc_re, hc_im, c_re[direction], c_im[direction]))
    y = s5_glu(y.astype(u.dtype), w_glu, b_glu)
    if with_ctx_out:
        y_ctx = s5_glu(y_ctx.astype(u.dtype), w_glu, b_glu)
    return y, y_ctx


def sgu_mixer(z, ln_g, ln_b, w_sp, b_sp):
    u, v = jnp.split(z, 2, axis=-1)
    v = layer_norm(v, ln_g, ln_b)
    bsz, length, _ = v.shape
    v = v.reshape(bsz, length // CHUNK, CHUNK, SGU_GROUPS, SGU_GROUP_DIM)
    s = jnp.einsum('bcpgd,gqp->bcqgd', v, w_sp) + jnp.transpose(b_sp)[None, None, :, :, None]
    return u * s.reshape(bsz, length, SGU_WIDTH)


def merge_branches(y_a, y_b, gate_logits, w_proj_a, w_proj_b, b_gate, w_out):
    g_a, g_b = jnp.split(jax.nn.sigmoid(gate_logits + b_gate), N_BRANCH, axis=-1)
    return (g_a * (y_a @ w_proj_a) + g_b * (y_b @ w_proj_b)) @ w_out


def depthwise_conv_grid(u, w, b, rows, cols):
    bsz, _, ch = u.shape
    y = lax.conv_general_dilated(
        u.reshape(bsz, rows, cols, ch), w[:, :, None, :], (1, 1), 'SAME',
        dimension_numbers=('NHWC', 'HWIO', 'NHWC'), feature_group_count=ch)
    return y.reshape(bsz, rows * cols, ch) + b


def conv_ffn(h, w_up, conv_w, conv_b, w_down, rows, cols):
    up = depthwise_conv_grid(h @ w_up, conv_w, conv_b, rows, cols)
    gate, val = jnp.split(up, 2, axis=-1)
    return (jax.nn.silu(gate) * val) @ w_down


def trunk_layer(x, xc, c_silu, cc_silu, rows, w_ada, b_ada, g_mix, w_in,
                s5_a_re, s5_a_im, s5_log_step, s5_b_re, s5_b_im, s5_c_re, s5_c_im,
                s5_d, s5_w_glu, s5_b_glu, sgu_ln_g, sgu_ln_b, sgu_w, sgu_b,
                w_proj_a, w_proj_b, b_gate, w_out, g_ffn, w_up, conv_w, conv_b, w_down,
                update_ctx):
    mod = c_silu @ w_ada + b_ada
    sh1, sc1, ga1, sh2, sc2, ga2 = jnp.split(mod[:, None, :], N_MOD, axis=-1)
    mod_c = cc_silu @ w_ada + b_ada
    sh1c, sc1c, ga1c, sh2c, sc2c, ga2c = jnp.split(mod_c, N_MOD, axis=-1)

    h = modulate(x, g_mix, sh1, sc1)
    hc = modulate(xc, g_mix, sh1c, sc1c)
    proj = h @ w_in
    u_a = proj[..., :S5_WIDTH]
    z_b = jax.nn.gelu(proj[..., S5_WIDTH:S5_WIDTH + 2 * SGU_WIDTH])
    gate_logits = proj[..., S5_WIDTH + 2 * SGU_WIDTH:]
    proj_c = hc @ (w_in if update_ctx else w_in[:, :S5_WIDTH])
    u_ac = proj_c[..., :S5_WIDTH]
    y_a, y_ac = s5_mixer(u_a, u_ac, s5_a_re, s5_a_im, s5_log_step, s5_b_re, s5_b_im,
                         s5_c_re, s5_c_im, s5_d, s5_w_glu, s5_b_glu, update_ctx)
    y_b = sgu_mixer(z_b, sgu_ln_g, sgu_ln_b, sgu_w, sgu_b)
    x = x + ga1 * merge_branches(y_a, y_b, gate_logits, w_proj_a, w_proj_b, b_gate, w_out)

    h2 = modulate(x, g_ffn, sh2, sc2)
    x = x + ga2 * conv_ffn(h2, w_up, conv_w, conv_b, w_down, rows, GRID_W)

    if update_ctx:
        z_bc = jax.nn.gelu(proj_c[..., S5_WIDTH:S5_WIDTH + 2 * SGU_WIDTH])
        gate_c = proj_c[..., S5_WIDTH + 2 * SGU_WIDTH:]
        y_bc = sgu_mixer(z_bc, sgu_ln_g, sgu_ln_b, sgu_w, sgu_b)
        xc = xc + ga1c * merge_branches(y_ac, y_bc, gate_c, w_proj_a, w_proj_b, b_gate, w_out)
        h2c = modulate(xc, g_ffn, sh2c, sc2c)
        mid = CONV_K // 2
        xc = xc + ga2c * conv_ffn(h2c, w_up, conv_w[mid:mid + 1], conv_b, w_down, 1, xc.shape[1])
    else:
        xc = None
    return x, xc


def _fwd_setup_inputs(seed: int = 0) -> dict:
    key = jax.random.key(seed)
    ks = jax.random.split(key, 40)
    f32 = jnp.float32
    D, L = D_MODEL, DEPTH
    G, N, H = S5_GROUPS, S5_STATE, S5_GROUP
    F2 = 2 * FFN_HIDDEN

    def nrm(k, shape, scale):
        return jax.random.normal(k, shape, f32) * scale

    n_idx = jnp.arange(N, dtype=f32)
    return {
        'x': nrm(ks[0], (BATCH, SEQ, D), 1.0),
        'c': nrm(ks[1], (BATCH, D), 1.0),
        'ctx': nrm(ks[2], (BATCH, CTX_LEN, D), 1.0),
        'c_ctx': nrm(ks[3], (D,), 1.0),
        'w_ada': nrm(ks[4], (L, D, N_MOD * D), 0.5 * D ** -0.5),
        'b_ada': nrm(ks[5], (L, N_MOD * D), 0.02),
        'g_mix': 1.0 + nrm(ks[6], (L, D), 0.02),
        'w_in': nrm(ks[7], (L, D, IN_WIDTH), D ** -0.5),
        's5_a_re': -0.5 + nrm(ks[8], (L, 2, G, N), 0.01),
        's5_a_im': math.pi * n_idx + nrm(ks[9], (L, 2, G, N), 0.01),
        's5_log_step': jax.random.uniform(ks[10], (L, 2, G), f32, math.log(DT_MIN), math.log(DT_MAX)),
        's5_b_re': nrm(ks[11], (L, 2, G, N, H), (2 * H) ** -0.5),
        's5_b_im': nrm(ks[12], (L, 2, G, N, H), (2 * H) ** -0.5),
        's5_c_re': nrm(ks[13], (L, 2, G, H, N), N ** -0.5),
        's5_c_im': nrm(ks[14], (L, 2, G, H, N), N ** -0.5),
        's5_d': nrm(ks[15], (L, S5_WIDTH), 1.0),
        's5_w_glu': nrm(ks[16], (L, S5_WIDTH, S5_WIDTH), S5_WIDTH ** -0.5),
        's5_b_glu': nrm(ks[17], (L, S5_WIDTH), 0.02),
        'sgu_ln_g': 1.0 + nrm(ks[18], (L, SGU_WIDTH), 0.02),
        'sgu_ln_b': nrm(ks[19], (L, SGU_WIDTH), 0.02),
        'sgu_w': nrm(ks[20], (L, SGU_GROUPS, CHUNK, CHUNK), CHUNK ** -0.5),
        'sgu_b': 1.0 + nrm(ks[21], (L, SGU_GROUPS, CHUNK), 0.1),
        'w_proj_a': nrm(ks[22], (L, S5_WIDTH, D), S5_WIDTH ** -0.5),
        'w_proj_b': nrm(ks[23], (L, SGU_WIDTH, D), SGU_WIDTH ** -0.5),
        'b_gate': nrm(ks[24], (L, N_BRANCH * D), 0.02),
        'w_out': nrm(ks[25], (L, D, D), D ** -0.5),
        'g_ffn': 1.0 + nrm(ks[26], (L, D), 0.02),
        'w_up': nrm(ks[27], (L, D, F2), D ** -0.5),
        'conv_w': nrm(ks[28], (L, CONV_K, CONV_K, F2), 1.0 / CONV_K),
        'conv_b': nrm(ks[29], (L, F2), 0.02),
        'w_down': nrm(ks[30], (L, FFN_HIDDEN, D), FFN_HIDDEN ** -0.5),
        'g_final': 1.0 + nrm(ks[31], (D,), 0.02),
    }


def _fwd_reference(x, c, ctx, c_ctx, w_ada, b_ada, g_mix, w_in, s5_a_re, s5_a_im, s5_log_step,
              s5_b_re, s5_b_im, s5_c_re, s5_c_im, s5_d, s5_w_glu, s5_b_glu,
              sgu_ln_g, sgu_ln_b, sgu_w, sgu_b, w_proj_a, w_proj_b, b_gate, w_out,
              g_ffn, w_up, conv_w, conv_b, w_down, g_final):
    rows = x.shape[1] // GRID_W
    c_silu = jax.nn.silu(c)
    cc_silu = jax.nn.silu(c_ctx)
    xc = ctx
    for i in range(DEPTH):
        x, xc = trunk_layer(
            x, xc, c_silu, cc_silu, rows, w_ada[i], b_ada[i], g_mix[i], w_in[i],
            s5_a_re[i], s5_a_im[i], s5_log_step[i], s5_b_re[i], s5_b_im[i], s5_c_re[i], s5_c_im[i],
            s5_d[i], s5_w_glu[i], s5_b_glu[i], sgu_ln_g[i], sgu_ln_b[i], sgu_w[i], sgu_b[i],
            w_proj_a[i], w_proj_b[i], b_gate[i], w_out[i], g_ffn[i], w_up[i], conv_w[i], conv_b[i],
            w_down[i], i + 1 < DEPTH)
    return rms_norm(x, g_final)


import jax as _jax
import jax.numpy as _jnp

TWIN_FORMAT = 'train_step'
FWD_PARAMS = ['x', 'c', 'ctx', 'c_ctx', 'w_ada', 'b_ada', 'g_mix', 'w_in', 's5_a_re', 's5_a_im', 's5_log_step', 's5_b_re', 's5_b_im', 's5_c_re', 's5_c_im', 's5_d', 's5_w_glu', 's5_b_glu', 'sgu_ln_g', 'sgu_ln_b', 'sgu_w', 'sgu_b', 'w_proj_a', 'w_proj_b', 'b_gate', 'w_out', 'g_ffn', 'w_up', 'conv_w', 'conv_b', 'w_down', 'g_final']
TWIN_WEIGHTS = ['c_ctx', 'w_ada', 'b_ada', 'g_mix', 'w_in', 's5_a_re', 's5_a_im', 's5_log_step', 's5_b_re', 's5_b_im', 's5_c_re', 's5_c_im', 's5_d', 's5_w_glu', 's5_b_glu', 'sgu_ln_g', 'sgu_ln_b', 'sgu_w', 'sgu_b', 'w_proj_a', 'w_proj_b', 'b_gate', 'w_out', 'g_ffn', 'w_up', 'conv_w', 'conv_b', 'w_down', 'g_final']
TWIN_DIFF_INPUT = 'x'
TWIN_INPUTS = ['x', 'c', 'ctx', 'c_ctx', 'w_ada', 'b_ada', 'g_mix', 'w_in', 's5_a_re', 's5_a_im', 's5_log_step', 's5_b_re', 's5_b_im', 's5_c_re', 's5_c_im', 's5_d', 's5_w_glu', 's5_b_glu', 'sgu_ln_g', 'sgu_ln_b', 'sgu_w', 'sgu_b', 'w_proj_a', 'w_proj_b', 'b_gate', 'w_out', 'g_ffn', 'w_up', 'conv_w', 'conv_b', 'w_down', 'g_final', 'loss_target', 'm_c_ctx', 'm_w_ada', 'm_b_ada', 'm_g_mix', 'm_w_in', 'm_s5_a_re', 'm_s5_a_im', 'm_s5_log_step', 'm_s5_b_re', 'm_s5_b_im', 'm_s5_c_re', 'm_s5_c_im', 'm_s5_d', 'm_s5_w_glu', 'm_s5_b_glu', 'm_sgu_ln_g', 'm_sgu_ln_b', 'm_sgu_w', 'm_sgu_b', 'm_w_proj_a', 'm_w_proj_b', 'm_b_gate', 'm_w_out', 'm_g_ffn', 'm_w_up', 'm_conv_w', 'm_conv_b', 'm_w_down', 'm_g_final', 'v_c_ctx', 'v_w_ada', 'v_b_ada', 'v_g_mix', 'v_w_in', 'v_s5_a_re', 'v_s5_a_im', 'v_s5_log_step', 'v_s5_b_re', 'v_s5_b_im', 'v_s5_c_re', 'v_s5_c_im', 'v_s5_d', 'v_s5_w_glu', 'v_s5_b_glu', 'v_sgu_ln_g', 'v_sgu_ln_b', 'v_sgu_w', 'v_sgu_b', 'v_w_proj_a', 'v_w_proj_b', 'v_b_gate', 'v_w_out', 'v_g_ffn', 'v_w_up', 'v_conv_w', 'v_conv_b', 'v_w_down', 'v_g_final']
TWIN_OUTPUTS = ['loss', 'grad_x', 'grad_c_ctx', 'grad_w_ada', 'grad_b_ada', 'grad_g_mix', 'grad_w_in', 'grad_s5_a_re', 'grad_s5_a_im', 'grad_s5_log_step', 'grad_s5_b_re', 'grad_s5_b_im', 'grad_s5_c_re', 'grad_s5_c_im', 'grad_s5_d', 'grad_s5_w_glu', 'grad_s5_b_glu', 'grad_sgu_ln_g', 'grad_sgu_ln_b', 'grad_sgu_w', 'grad_sgu_b', 'grad_w_proj_a', 'grad_w_proj_b', 'grad_b_gate', 'grad_w_out', 'grad_g_ffn', 'grad_w_up', 'grad_conv_w', 'grad_conv_b', 'grad_w_down', 'grad_g_final', 'delta_c_ctx', 'delta_w_ada', 'delta_b_ada', 'delta_g_mix', 'delta_w_in', 'delta_s5_a_re', 'delta_s5_a_im', 'delta_s5_log_step', 'delta_s5_b_re', 'delta_s5_b_im', 'delta_s5_c_re', 'delta_s5_c_im', 'delta_s5_d', 'delta_s5_w_glu', 'delta_s5_b_glu', 'delta_sgu_ln_g', 'delta_sgu_ln_b', 'delta_sgu_w', 'delta_sgu_b', 'delta_w_proj_a', 'delta_w_proj_b', 'delta_b_gate', 'delta_w_out', 'delta_g_ffn', 'delta_w_up', 'delta_conv_w', 'delta_conv_b', 'delta_w_down', 'delta_g_final', 'new_m_c_ctx', 'new_m_w_ada', 'new_m_b_ada', 'new_m_g_mix', 'new_m_w_in', 'new_m_s5_a_re', 'new_m_s5_a_im', 'new_m_s5_log_step', 'new_m_s5_b_re', 'new_m_s5_b_im', 'new_m_s5_c_re', 'new_m_s5_c_im', 'new_m_s5_d', 'new_m_s5_w_glu', 'new_m_s5_b_glu', 'new_m_sgu_ln_g', 'new_m_sgu_ln_b', 'new_m_sgu_w', 'new_m_sgu_b', 'new_m_w_proj_a', 'new_m_w_proj_b', 'new_m_b_gate', 'new_m_w_out', 'new_m_g_ffn', 'new_m_w_up', 'new_m_conv_w', 'new_m_conv_b', 'new_m_w_down', 'new_m_g_final', 'new_v_c_ctx', 'new_v_w_ada', 'new_v_b_ada', 'new_v_g_mix', 'new_v_w_in', 'new_v_s5_a_re', 'new_v_s5_a_im', 'new_v_s5_log_step', 'new_v_s5_b_re', 'new_v_s5_b_im', 'new_v_s5_c_re', 'new_v_s5_c_im', 'new_v_s5_d', 'new_v_s5_w_glu', 'new_v_s5_b_glu', 'new_v_sgu_ln_g', 'new_v_sgu_ln_b', 'new_v_sgu_w', 'new_v_sgu_b', 'new_v_w_proj_a', 'new_v_w_proj_b', 'new_v_b_gate', 'new_v_w_out', 'new_v_g_ffn', 'new_v_w_up', 'new_v_conv_w', 'new_v_conv_b', 'new_v_w_down', 'new_v_g_final']
TWIN_LEAF_KINDS = {'loss': 'loss', 'grad_x': 'grad_x', 'grad_c_ctx': 'grad_w', 'grad_w_ada': 'grad_w', 'grad_b_ada': 'grad_w', 'grad_g_mix': 'grad_w', 'grad_w_in': 'grad_w', 'grad_s5_a_re': 'grad_w', 'grad_s5_a_im': 'grad_w', 'grad_s5_log_step': 'grad_w', 'grad_s5_b_re': 'grad_w', 'grad_s5_b_im': 'grad_w', 'grad_s5_c_re': 'grad_w', 'grad_s5_c_im': 'grad_w', 'grad_s5_d': 'grad_w', 'grad_s5_w_glu': 'grad_w', 'grad_s5_b_glu': 'grad_w', 'grad_sgu_ln_g': 'grad_w', 'grad_sgu_ln_b': 'grad_w', 'grad_sgu_w': 'grad_w', 'grad_sgu_b': 'grad_w', 'grad_w_proj_a': 'grad_w', 'grad_w_proj_b': 'grad_w', 'grad_b_gate': 'grad_w', 'grad_w_out': 'grad_w', 'grad_g_ffn': 'grad_w', 'grad_w_up': 'grad_w', 'grad_conv_w': 'grad_w', 'grad_conv_b': 'grad_w', 'grad_w_down': 'grad_w', 'grad_g_final': 'grad_w', 'delta_c_ctx': 'delta_w', 'delta_w_ada': 'delta_w', 'delta_b_ada': 'delta_w', 'delta_g_mix': 'delta_w', 'delta_w_in': 'delta_w', 'delta_s5_a_re': 'delta_w', 'delta_s5_a_im': 'delta_w', 'delta_s5_log_step': 'delta_w', 'delta_s5_b_re': 'delta_w', 'delta_s5_b_im': 'delta_w', 'delta_s5_c_re': 'delta_w', 'delta_s5_c_im': 'delta_w', 'delta_s5_d': 'delta_w', 'delta_s5_w_glu': 'delta_w', 'delta_s5_b_glu': 'delta_w', 'delta_sgu_ln_g': 'delta_w', 'delta_sgu_ln_b': 'delta_w', 'delta_sgu_w': 'delta_w', 'delta_sgu_b': 'delta_w', 'delta_w_proj_a': 'delta_w', 'delta_w_proj_b': 'delta_w', 'delta_b_gate': 'delta_w', 'delta_w_out': 'delta_w', 'delta_g_ffn': 'delta_w', 'delta_w_up': 'delta_w', 'delta_conv_w': 'delta_w', 'delta_conv_b': 'delta_w', 'delta_w_down': 'delta_w', 'delta_g_final': 'delta_w', 'new_m_c_ctx': 'new_m', 'new_m_w_ada': 'new_m', 'new_m_b_ada': 'new_m', 'new_m_g_mix': 'new_m', 'new_m_w_in': 'new_m', 'new_m_s5_a_re': 'new_m', 'new_m_s5_a_im': 'new_m', 'new_m_s5_log_step': 'new_m', 'new_m_s5_b_re': 'new_m', 'new_m_s5_b_im': 'new_m', 'new_m_s5_c_re': 'new_m', 'new_m_s5_c_im': 'new_m', 'new_m_s5_d': 'new_m', 'new_m_s5_w_glu': 'new_m', 'new_m_s5_b_glu': 'new_m', 'new_m_sgu_ln_g': 'new_m', 'new_m_sgu_ln_b': 'new_m', 'new_m_sgu_w': 'new_m', 'new_m_sgu_b': 'new_m', 'new_m_w_proj_a': 'new_m', 'new_m_w_proj_b': 'new_m', 'new_m_b_gate': 'new_m', 'new_m_w_out': 'new_m', 'new_m_g_ffn': 'new_m', 'new_m_w_up': 'new_m', 'new_m_conv_w': 'new_m', 'new_m_conv_b': 'new_m', 'new_m_w_down': 'new_m', 'new_m_g_final': 'new_m', 'new_v_c_ctx': 'new_v', 'new_v_w_ada': 'new_v', 'new_v_b_ada': 'new_v', 'new_v_g_mix': 'new_v', 'new_v_w_in': 'new_v', 'new_v_s5_a_re': 'new_v', 'new_v_s5_a_im': 'new_v', 'new_v_s5_log_step': 'new_v', 'new_v_s5_b_re': 'new_v', 'new_v_s5_b_im': 'new_v', 'new_v_s5_c_re': 'new_v', 'new_v_s5_c_im': 'new_v', 'new_v_s5_d': 'new_v', 'new_v_s5_w_glu': 'new_v', 'new_v_s5_b_glu': 'new_v', 'new_v_sgu_ln_g': 'new_v', 'new_v_sgu_ln_b': 'new_v', 'new_v_sgu_w': 'new_v', 'new_v_sgu_b': 'new_v', 'new_v_w_proj_a': 'new_v', 'new_v_w_proj_b': 'new_v', 'new_v_b_gate': 'new_v', 'new_v_w_out': 'new_v', 'new_v_g_ffn': 'new_v', 'new_v_w_up': 'new_v', 'new_v_conv_w': 'new_v', 'new_v_conv_b': 'new_v', 'new_v_w_down': 'new_v', 'new_v_g_final': 'new_v'}


def _forward(args):
    return _fwd_reference(*[args[k] for k in FWD_PARAMS])


def _output_shape():
    def fwd():
        inp = _fwd_setup_inputs(0)
        return _fwd_reference(*[inp[k] for k in FWD_PARAMS])
    out = _jax.eval_shape(fwd)
    return out.shape, out.dtype

N_MICROBATCH = 1
ADAM_LR = 0.001
ADAM_B1 = 0.9
ADAM_B2 = 0.999
ADAM_EPS = 1e-08
ADAM_WD = 0.01
ADAM_STEP = 10
PER_EXAMPLE_BATCH_AXIS = {'x': 0, 'c': 0, 'ctx': 0, 'loss_target': 0}
SHARED_INPUTS = []
_WEIGHT_DTYPES = {'c_ctx': _jnp.float32, 'w_ada': _jnp.float32, 'b_ada': _jnp.float32, 'g_mix': _jnp.float32, 'w_in': _jnp.float32, 's5_a_re': _jnp.float32, 's5_a_im': _jnp.float32, 's5_log_step': _jnp.float32, 's5_b_re': _jnp.float32, 's5_b_im': _jnp.float32, 's5_c_re': _jnp.float32, 's5_c_im': _jnp.float32, 's5_d': _jnp.float32, 's5_w_glu': _jnp.float32, 's5_b_glu': _jnp.float32, 'sgu_ln_g': _jnp.float32, 'sgu_ln_b': _jnp.float32, 'sgu_w': _jnp.float32, 'sgu_b': _jnp.float32, 'w_proj_a': _jnp.float32, 'w_proj_b': _jnp.float32, 'b_gate': _jnp.float32, 'w_out': _jnp.float32, 'g_ffn': _jnp.float32, 'w_up': _jnp.float32, 'conv_w': _jnp.float32, 'conv_b': _jnp.float32, 'w_down': _jnp.float32, 'g_final': _jnp.float32}
MOMENT_SCALE = {'c_ctx': 5.674621e-04, 'w_ada': 7.205833e-02, 'b_ada': 1.196007e-01, 'g_mix': 6.261159e-02, 'w_in': 3.295438e-02, 's5_a_re': 1.743481e-03, 's5_a_im': 1.734301e-03, 's5_log_step': 7.741217e-01, 's5_b_re': 1.262244e-03, 's5_b_im': 1.314052e-03, 's5_c_re': 1.867717e-03, 's5_c_im': 1.792216e-03, 's5_d': 2.626590e-02, 's5_w_glu': 7.921489e-03, 's5_b_glu': 1.032028e-02, 'sgu_ln_g': 4.676591e-02, 'sgu_ln_b': 4.289308e-02, 'sgu_w': 2.996453e-02, 'sgu_b': 3.001223e-02, 'w_proj_a': 1.704095e-02, 'w_proj_b': 4.275861e-02, 'b_gate': 1.231840e-02, 'w_out': 4.639450e-02, 'g_ffn': 8.637214e-02, 'w_up': 3.702698e-02, 'conv_w': 3.669035e-02, 'conv_b': 2.959801e-02, 'w_down': 6.071642e-02, 'g_final': 6.401183e+01}


def _to_microbatches(a, axis):
    t = _jnp.moveaxis(a, axis, 0)
    t = t.reshape((N_MICROBATCH, t.shape[0] // N_MICROBATCH) + t.shape[1:])
    return _jnp.moveaxis(t, 1, axis + 1)


def setup_inputs(seed: int = 0) -> dict:
    inp = _fwd_setup_inputs(seed)
    key = _jax.random.fold_in(_jax.random.key(seed), 7919)
    shape, _ = _output_shape()
    out = dict(inp)
    out["loss_target"] = _jax.random.normal(_jax.random.fold_in(key, 0), shape, _jnp.float32)
    for i, name in enumerate(TWIN_WEIGHTS):
        w = inp[name].astype(_jnp.float32)
        if MOMENT_SCALE is None:
            s = _jnp.sqrt(_jnp.mean(_jnp.square(w)) + 1e-30)
        else:
            s = MOMENT_SCALE[name]
        km, kv = _jax.random.split(_jax.random.fold_in(key, i + 1))
        out[name] = w
        out["m_" + name] = s * _jax.random.normal(km, w.shape, _jnp.float32)
        out["v_" + name] = (s * s) * _jax.random.uniform(kv, w.shape, _jnp.float32, 0.5, 1.5)
    if N_MICROBATCH > 1:
        for name, axis in PER_EXAMPLE_BATCH_AXIS.items():
            out[name] = _to_microbatches(out[name], axis)
    return {'x': out['x'], 'c': out['c'], 'ctx': out['ctx'], 'c_ctx': out['c_ctx'], 'w_ada': out['w_ada'], 'b_ada': out['b_ada'], 'g_mix': out['g_mix'], 'w_in': out['w_in'], 's5_a_re': out['s5_a_re'], 's5_a_im': out['s5_a_im'], 's5_log_step': out['s5_log_step'], 's5_b_re': out['s5_b_re'], 's5_b_im': out['s5_b_im'], 's5_c_re': out['s5_c_re'], 's5_c_im': out['s5_c_im'], 's5_d': out['s5_d'], 's5_w_glu': out['s5_w_glu'], 's5_b_glu': out['s5_b_glu'], 'sgu_ln_g': out['sgu_ln_g'], 'sgu_ln_b': out['sgu_ln_b'], 'sgu_w': out['sgu_w'], 'sgu_b': out['sgu_b'], 'w_proj_a': out['w_proj_a'], 'w_proj_b': out['w_proj_b'], 'b_gate': out['b_gate'], 'w_out': out['w_out'], 'g_ffn': out['g_ffn'], 'w_up': out['w_up'], 'conv_w': out['conv_w'], 'conv_b': out['conv_b'], 'w_down': out['w_down'], 'g_final': out['g_final'], 'loss_target': out['loss_target'], 'm_c_ctx': out['m_c_ctx'], 'm_w_ada': out['m_w_ada'], 'm_b_ada': out['m_b_ada'], 'm_g_mix': out['m_g_mix'], 'm_w_in': out['m_w_in'], 'm_s5_a_re': out['m_s5_a_re'], 'm_s5_a_im': out['m_s5_a_im'], 'm_s5_log_step': out['m_s5_log_step'], 'm_s5_b_re': out['m_s5_b_re'], 'm_s5_b_im': out['m_s5_b_im'], 'm_s5_c_re': out['m_s5_c_re'], 'm_s5_c_im': out['m_s5_c_im'], 'm_s5_d': out['m_s5_d'], 'm_s5_w_glu': out['m_s5_w_glu'], 'm_s5_b_glu': out['m_s5_b_glu'], 'm_sgu_ln_g': out['m_sgu_ln_g'], 'm_sgu_ln_b': out['m_sgu_ln_b'], 'm_sgu_w': out['m_sgu_w'], 'm_sgu_b': out['m_sgu_b'], 'm_w_proj_a': out['m_w_proj_a'], 'm_w_proj_b': out['m_w_proj_b'], 'm_b_gate': out['m_b_gate'], 'm_w_out': out['m_w_out'], 'm_g_ffn': out['m_g_ffn'], 'm_w_up': out['m_w_up'], 'm_conv_w': out['m_conv_w'], 'm_conv_b': out['m_conv_b'], 'm_w_down': out['m_w_down'], 'm_g_final': out['m_g_final'], 'v_c_ctx': out['v_c_ctx'], 'v_w_ada': out['v_w_ada'], 'v_b_ada': out['v_b_ada'], 'v_g_mix': out['v_g_mix'], 'v_w_in': out['v_w_in'], 'v_s5_a_re': out['v_s5_a_re'], 'v_s5_a_im': out['v_s5_a_im'], 'v_s5_log_step': out['v_s5_log_step'], 'v_s5_b_re': out['v_s5_b_re'], 'v_s5_b_im': out['v_s5_b_im'], 'v_s5_c_re': out['v_s5_c_re'], 'v_s5_c_im': out['v_s5_c_im'], 'v_s5_d': out['v_s5_d'], 'v_s5_w_glu': out['v_s5_w_glu'], 'v_s5_b_glu': out['v_s5_b_glu'], 'v_sgu_ln_g': out['v_sgu_ln_g'], 'v_sgu_ln_b': out['v_sgu_ln_b'], 'v_sgu_w': out['v_sgu_w'], 'v_sgu_b': out['v_sgu_b'], 'v_w_proj_a': out['v_w_proj_a'], 'v_w_proj_b': out['v_w_proj_b'], 'v_b_gate': out['v_b_gate'], 'v_w_out': out['v_w_out'], 'v_g_ffn': out['v_g_ffn'], 'v_w_up': out['v_w_up'], 'v_conv_w': out['v_conv_w'], 'v_conv_b': out['v_conv_b'], 'v_w_down': out['v_w_down'], 'v_g_final': out['v_g_final']}


def _loss(weights, diff, rest, loss_target):
    with _jax.named_scope("forward"):
        args = {**rest, TWIN_DIFF_INPUT: diff, **{k: w.astype(_WEIGHT_DTYPES[k]) for k, w in weights.items()}}
        y = _forward(args)
    with _jax.named_scope("loss_head"):
        err = _jnp.square(y.astype(_jnp.float32) - loss_target)
        return 0.5 * _jnp.sum(_jnp.mean(err, axis=-1)) if err.ndim else 0.5 * err


def _adamw(w, g, m, v):
    m = ADAM_B1 * m + (1.0 - ADAM_B1) * g
    v = ADAM_B2 * v + (1.0 - ADAM_B2) * _jnp.square(g)
    m_hat = m / (1.0 - ADAM_B1 ** ADAM_STEP)
    v_hat = v / (1.0 - ADAM_B2 ** ADAM_STEP)
    delta = -ADAM_LR * (m_hat / (_jnp.sqrt(v_hat) + ADAM_EPS) + ADAM_WD * w)
    return delta, m, v


def reference(x, c, ctx, c_ctx, w_ada, b_ada, g_mix, w_in, s5_a_re, s5_a_im, s5_log_step, s5_b_re, s5_b_im, s5_c_re, s5_c_im, s5_d, s5_w_glu, s5_b_glu, sgu_ln_g, sgu_ln_b, sgu_w, sgu_b, w_proj_a, w_proj_b, b_gate, w_out, g_ffn, w_up, conv_w, conv_b, w_down, g_final, loss_target, m_c_ctx, m_w_ada, m_b_ada, m_g_mix, m_w_in, m_s5_a_re, m_s5_a_im, m_s5_log_step, m_s5_b_re, m_s5_b_im, m_s5_c_re, m_s5_c_im, m_s5_d, m_s5_w_glu, m_s5_b_glu, m_sgu_ln_g, m_sgu_ln_b, m_sgu_w, m_sgu_b, m_w_proj_a, m_w_proj_b, m_b_gate, m_w_out, m_g_ffn, m_w_up, m_conv_w, m_conv_b, m_w_down, m_g_final, v_c_ctx, v_w_ada, v_b_ada, v_g_mix, v_w_in, v_s5_a_re, v_s5_a_im, v_s5_log_step, v_s5_b_re, v_s5_b_im, v_s5_c_re, v_s5_c_im, v_s5_d, v_s5_w_glu, v_s5_b_glu, v_sgu_ln_g, v_sgu_ln_b, v_sgu_w, v_sgu_b, v_w_proj_a, v_w_proj_b, v_b_gate, v_w_out, v_g_ffn, v_w_up, v_conv_w, v_conv_b, v_w_down, v_g_final):
    given = dict(x=x, c=c, ctx=ctx, c_ctx=c_ctx, w_ada=w_ada, b_ada=b_ada, g_mix=g_mix, w_in=w_in, s5_a_re=s5_a_re, s5_a_im=s5_a_im, s5_log_step=s5_log_step, s5_b_re=s5_b_re, s5_b_im=s5_b_im, s5_c_re=s5_c_re, s5_c_im=s5_c_im, s5_d=s5_d, s5_w_glu=s5_w_glu, s5_b_glu=s5_b_glu, sgu_ln_g=sgu_ln_g, sgu_ln_b=sgu_ln_b, sgu_w=sgu_w, sgu_b=sgu_b, w_proj_a=w_proj_a, w_proj_b=w_proj_b, b_gate=b_gate, w_out=w_out, g_ffn=g_ffn, w_up=w_up, conv_w=conv_w, conv_b=conv_b, w_down=w_down, g_final=g_final, loss_target=loss_target, m_c_ctx=m_c_ctx, m_w_ada=m_w_ada, m_b_ada=m_b_ada, m_g_mix=m_g_mix, m_w_in=m_w_in, m_s5_a_re=m_s5_a_re, m_s5_a_im=m_s5_a_im, m_s5_log_step=m_s5_log_step, m_s5_b_re=m_s5_b_re, m_s5_b_im=m_s5_b_im, m_s5_c_re=m_s5_c_re, m_s5_c_im=m_s5_c_im, m_s5_d=m_s5_d, m_s5_w_glu=m_s5_w_glu, m_s5_b_glu=m_s5_b_glu, m_sgu_ln_g=m_sgu_ln_g, m_sgu_ln_b=m_sgu_ln_b, m_sgu_w=m_sgu_w, m_sgu_b=m_sgu_b, m_w_proj_a=m_w_proj_a, m_w_proj_b=m_w_proj_b, m_b_gate=m_b_gate, m_w_out=m_w_out, m_g_ffn=m_g_ffn, m_w_up=m_w_up, m_conv_w=m_conv_w, m_conv_b=m_conv_b, m_w_down=m_w_down, m_g_final=m_g_final, v_c_ctx=v_c_ctx, v_w_ada=v_w_ada, v_b_ada=v_b_ada, v_g_mix=v_g_mix, v_w_in=v_w_in, v_s5_a_re=v_s5_a_re, v_s5_a_im=v_s5_a_im, v_s5_log_step=v_s5_log_step, v_s5_b_re=v_s5_b_re, v_s5_b_im=v_s5_b_im, v_s5_c_re=v_s5_c_re, v_s5_c_im=v_s5_c_im, v_s5_d=v_s5_d, v_s5_w_glu=v_s5_w_glu, v_s5_b_glu=v_s5_b_glu, v_sgu_ln_g=v_sgu_ln_g, v_sgu_ln_b=v_sgu_ln_b, v_sgu_w=v_sgu_w, v_sgu_b=v_sgu_b, v_w_proj_a=v_w_proj_a, v_w_proj_b=v_w_proj_b, v_b_gate=v_b_gate, v_w_out=v_w_out, v_g_ffn=v_g_ffn, v_w_up=v_w_up, v_conv_w=v_conv_w, v_conv_b=v_conv_b, v_w_down=v_w_down, v_g_final=v_g_final)
    weights = {n: given[n] for n in TWIN_WEIGHTS}
    shared = {n: given[n] for n in SHARED_INPUTS}
    per_example = {n: given[n] for n in ['x', 'c', 'ctx']}
    grad_fn = _jax.value_and_grad(_loss, argnums=(0, 1))

    def one_microbatch(ex, loss_target):
        ex = dict(ex)
        diff = ex.pop(TWIN_DIFF_INPUT)
        return grad_fn(weights, diff, {**shared, **ex}, loss_target)

    if N_MICROBATCH == 1:
        loss, (grad_w, grad_x) = one_microbatch(per_example, given["loss_target"])
    else:
        def body(carry, xs):
            loss_sum, grad_sum = carry
            l_k, (gw_k, gx_k) = one_microbatch(xs[0], xs[1])
            with _jax.named_scope("update"):
                return (loss_sum + l_k, _jax.tree.map(_jnp.add, grad_sum, gw_k)), gx_k

        init = (_jnp.zeros((), _jnp.float32), _jax.tree.map(_jnp.zeros_like, weights))
        (loss, grad_w), grad_x = _jax.lax.scan(body, init, (per_example, given["loss_target"]))
    with _jax.named_scope("update"):
        delta_w, new_m, new_v = {}, {}, {}
        for n in TWIN_WEIGHTS:
            delta_w[n], new_m[n], new_v[n] = _adamw(weights[n], grad_w[n], given["m_" + n], given["v_" + n])
    return (loss, grad_x, *[grad_w[n] for n in TWIN_WEIGHTS], *[delta_w[n] for n in TWIN_WEIGHTS],
            *[new_m[n] for n in TWIN_WEIGHTS], *[new_v[n] for n in TWIN_WEIGHTS])
```

```python
import functools
import math

import jax
import jax.numpy as jnp
from jax import lax
from jax.experimental import pallas as pl
from jax.experimental.pallas import tpu as pltpu

F32 = jnp.float32
BF16 = jnp.bfloat16
HI = lax.Precision.HIGHEST

N_DEV = 8
GRID_W = 64
CHUNK = 128
EPS = 1e-6
S5_T = 32
S5_H = 16
LANE = 128
HALO = 128
VMEM_LIMIT = 56 * 1024 * 1024

ADAM_LR = 0.001
ADAM_B1 = 0.9
ADAM_B2 = 0.999
ADAM_EPS = 1e-08
ADAM_WD = 0.01
ADAM_STEP = 10

WEIGHTS = ['c_ctx', 'w_ada', 'b_ada', 'g_mix', 'w_in', 's5_a_re', 's5_a_im', 's5_log_step', 's5_b_re', 's5_b_im',
           's5_c_re', 's5_c_im', 's5_d', 's5_w_glu', 's5_b_glu', 'sgu_ln_g', 'sgu_ln_b', 'sgu_w', 'sgu_b',
           'w_proj_a', 'w_proj_b', 'b_gate', 'w_out', 'g_ffn', 'w_up', 'conv_w', 'conv_b', 'w_down', 'g_final']
COL_SHARDED = ('w_ada', 'w_in', 'w_proj_a', 'w_proj_b', 'w_up', 'conv_w')
ROW_SHARDED = ('s5_w_glu', 'w_out', 'w_down')
SHARDED = COL_SHARDED + ROW_SHARDED
REPLICATED = [n for n in WEIGHTS if n not in SHARDED]


def _call(body, **kw):
    return pl.pallas_call(body, **kw)


def _params(n_grid):
    return pltpu.CompilerParams(dimension_semantics=("arbitrary",) * n_grid, vmem_limit_bytes=VMEM_LIMIT)


def _pick(dim, pref, unit=LANE):
    best = None
    d = unit
    while d <= min(dim, pref):
        if dim % d == 0:
            best = d
        d += unit
    return best if best is not None else dim


def _dg(a, b, ca, cb, prec=None):
    return lax.dot_general(a, b, (((ca,), (cb,)), ((), ())), precision=prec, preferred_element_type=F32)


def _b16(v):
    return v.astype(BF16)


@jax.custom_vjp
def mmb(a, b):
    return _dg(_b16(a), _b16(b), 1, 0)


def _mmb_fwd(a, b):
    return mmb(a, b), (a, b)


def _mmb_bwd(res, g):
    a, b = res
    g = _b16(g)
    return _dg(g, _b16(b), 1, 1).astype(a.dtype), _dg(_b16(a), g, 0, 0).astype(b.dtype)


mmb.defvjp(_mmb_fwd, _mmb_bwd)


@jax.custom_vjp
def mmb_nt(a, b):
    return _dg(_b16(a), _b16(b), 1, 1)


def _mmb_nt_fwd(a, b):
    return mmb_nt(a, b), (a, b)


def _mmb_nt_bwd(res, g):
    a, b = res
    g = _b16(g)
    return _dg(g, _b16(b), 1, 0).astype(a.dtype), _dg(g, _b16(a), 0, 0).astype(b.dtype)


mmb_nt.defvjp(_mmb_nt_fwd, _mmb_nt_bwd)


@jax.custom_vjp
def mmf(a, b):
    return _dg(a, b, 1, 0, HI)


def _mmf_fwd(a, b):
    return mmf(a, b), (a, b)


def _mmf_bwd(res, g):
    a, b = res
    return _dg(g, b, 1, 1, HI), _dg(a, g, 0, 0, HI)


mmf.defvjp(_mmf_fwd, _mmf_bwd)


@jax.custom_vjp
def mmf_nt(a, b):
    return _dg(a, b, 1, 1, HI)


def _mmf_nt_fwd(a, b):
    return mmf_nt(a, b), (a, b)


def _mmf_nt_bwd(res, g):
    a, b = res
    return _dg(g, b, 1, 0, HI), _dg(g, a, 0, 0, HI)


mmf_nt.defvjp(_mmf_nt_fwd, _mmf_nt_bwd)


def _shift_impl(x, k, up):
    n = x.shape[0]
    idx = lax.broadcasted_iota(jnp.int32, (n, 1), 0)
    if up:
        return jnp.where(idx < n - k, pltpu.roll(x, n - k, 0), 0.0)
    return jnp.where(idx >= k, pltpu.roll(x, k, 0), 0.0)


@functools.partial(jax.custom_vjp, nondiff_argnums=(1, 2))
def _shift(x, k, up):
    return _shift_impl(x, k, up)


def _shift_fwd(x, k, up):
    return _shift_impl(x, k, up), None


def _shift_bwd(k, up, _, g):
    return (_shift_impl(g, k, not up),)


_shift.defvjp(_shift_fwd, _shift_bwd)


def _mm(a, b, *, name, ta=False, tb=False, n=None, b_n0=0, b_k0=0, add=None, out_dtype=F32,
        tm_pref=1024, tn_pref=768, tk_pref=1536):
    m, k = (a.shape[1], a.shape[0]) if ta else a.shape
    if n is None:
        n = b.shape[0] if tb else b.shape[1]
    tm, tn, tk = _pick(m, tm_pref), _pick(n, tn_pref), _pick(k, tk_pref)
    while b_n0 % tn:
        tn -= LANE
    while b_k0 % tk:
        tk -= LANE
    assert m % tm == 0 and n % tn == 0 and k % tk == 0
    nk = k // tk
    n_off, k_off = b_n0 // tn, b_k0 // tk
    a_spec = (pl.BlockSpec((tk, tm), lambda i, j, kk: (kk, i)) if ta
              else pl.BlockSpec((tm, tk), lambda i, j, kk: (i, kk)))
    b_spec = (pl.BlockSpec((tn, tk), lambda i, j, kk: (j + n_off, kk + k_off)) if tb
              else pl.BlockSpec((tk, tn), lambda i, j, kk: (kk + k_off, j + n_off)))
    o_spec = pl.BlockSpec((tm, tn), lambda i, j, kk: (i, j))
    ca, cb = (0 if ta else 1), (1 if tb else 0)
    has_add = add is not None

    def body(*refs):
        a_ref, b_ref = refs[0], refs[1]
        o_ref, acc_ref = refs[-2], refs[-1]
        kk = pl.program_id(2)

        @pl.when(kk == 0)
        def _():
            acc_ref[...] = jnp.zeros_like(acc_ref)

        acc_ref[...] += _dg(_b16(a_ref[...]), _b16(b_ref[...]), ca, cb)

        @pl.when(kk == nk - 1)
        def _():
            r = acc_ref[...]
            if has_add:
                r = r + refs[2][...].astype(F32)
            o_ref[...] = r.astype(o_ref.dtype)

    ins = [a, b] + ([add] if has_add else [])
    in_specs = [a_spec, b_spec] + ([o_spec] if has_add else [])
    return _call(body, name=name, grid=(m // tm, n // tn, nk), in_specs=in_specs, out_specs=o_spec,
                 out_shape=jax.ShapeDtypeStruct((m, n), out_dtype),
                 scratch_shapes=[pltpu.VMEM((tm, tn), F32)], compiler_params=_params(3))(*ins)


def _row_spec(blk, width):
    return pl.BlockSpec((blk, width), lambda i: (i, 0))


def _whole_spec(shape):
    return pl.BlockSpec(shape, lambda i: (0,) * len(shape))


def _stage_fwd(fn, rows, params, outs, *, blk, name, n_rows=None):
    n = n_rows or rows[0].shape[0]
    nr, npar = len(rows), len(params)

    def body(*refs):
        vals = [r[...] for r in refs[:nr + npar]]
        res = fn(*vals)
        for o_ref, v in zip(refs[nr + npar:], res):
            o_ref[...] = v.astype(o_ref.dtype)

    return _call(body, name=name, grid=(n // blk,),
                 in_specs=[_row_spec(blk, r.shape[1]) for r in rows] + [_whole_spec(p.shape) for p in params],
                 out_specs=[_row_spec(blk, w) for w, _ in outs],
                 out_shape=[jax.ShapeDtypeStruct((n, w), dt) for w, dt in outs],
                 compiler_params=_params(1))(*rows, *params)


def _stage_bwd(fn, rows, params, cts, *, blk, name, row_grads, n_rows=None):
    n = n_rows or rows[0].shape[0]
    nr, npar, nct = len(rows), len(params), len(cts)

    def body(*refs):
        vals = [r[...].astype(F32) for r in refs[:nr + npar]]
        ct = [r[...] for r in refs[nr + npar:nr + npar + nct]]
        d_rows = refs[nr + npar + nct:nr + npar + nct + len(row_grads)]
        d_par = refs[nr + npar + nct + len(row_grads):]
        res, vjp = jax.vjp(fn, *vals)
        g = vjp(tuple(c.astype(r.dtype) for c, r in zip(ct, res)))
        for o_ref, (j, _) in zip(d_rows, row_grads):
            o_ref[...] = g[j].astype(o_ref.dtype)

        @pl.when(pl.program_id(0) == 0)
        def _():
            for o_ref in d_par:
                o_ref[...] = jnp.zeros_like(o_ref)

        for j, o_ref in enumerate(d_par):
            o_ref[...] += g[nr + j].astype(F32)

    return _call(body, name=name, grid=(n // blk,),
                 in_specs=([_row_spec(blk, r.shape[1]) for r in rows] + [_whole_spec(p.shape) for p in params]
                           + [_row_spec(blk, c.shape[1]) for c in cts]),
                 out_specs=([_row_spec(blk, rows[j].shape[1]) for j, _ in row_grads]
                            + [_whole_spec(p.shape) for p in params]),
                 out_shape=([jax.ShapeDtypeStruct((n, rows[j].shape[1]), dt) for j, dt in row_grads]
                            + [jax.ShapeDtypeStruct(p.shape, F32) for p in params]),
                 compiler_params=_params(1))(*rows, *params, *cts)


def _stage_loss(fn, rows, params, *, blk, name, row_grads):
    n = rows[0].shape[0]
    nr, npar = len(rows), len(params)

    def body(*refs):
        vals = [r[...].astype(F32) for r in refs[:nr + npar]]
        loss_ref = refs[nr + npar]
        d_rows = refs[nr + npar + 1:nr + npar + 1 + len(row_grads)]
        d_par = refs[nr + npar + 1 + len(row_grads):]
        res, vjp = jax.vjp(fn, *vals)
        g = vjp(jnp.ones_like(res))
        for o_ref, (j, _) in zip(d_rows, row_grads):
            o_ref[...] = g[j].astype(o_ref.dtype)

        @pl.when(pl.program_id(0) == 0)
        def _():
            loss_ref[...] = jnp.zeros_like(loss_ref)
            for o_ref in d_par:
                o_ref[...] = jnp.zeros_like(o_ref)

        loss_ref[...] += res
        for j, o_ref in enumerate(d_par):
            o_ref[...] += g[nr + j].astype(F32)

    return _call(body, name=name, grid=(n // blk,),
                 in_specs=[_row_spec(blk, r.shape[1]) for r in rows] + [_whole_spec(p.shape) for p in params],
                 out_specs=([_whole_spec((1, 1))] + [_row_spec(blk, rows[j].shape[1]) for j, _ in row_grads]
                            + [_whole_spec(p.shape) for p in params]),
                 out_shape=([jax.ShapeDtypeStruct((1, 1), F32)]
                            + [jax.ShapeDtypeStruct((n, rows[j].shape[1]), dt) for j, dt in row_grads]
                            + [jax.ShapeDtypeStruct(p.shape, F32) for p in params]),
                 compiler_params=_params(1))(*rows, *params)


def _rms(x, g):
    return x * lax.rsqrt(jnp.mean(x * x, axis=-1, keepdims=True) + EPS) * g


def _modulate(x, g, shift, scale):
    return _rms(x, g) * (1.0 + scale) + shift


def _fn_mod(cs, w_ada):
    return (mmb(jax.nn.silu(cs), w_ada),)


def _fn_a(x, g_mix, sh, sc):
    return (_b16(_modulate(x, g_mix, sh, sc)),)


def _fn_a_res(x, g_mix, sh, sc):
    return _b16(_modulate(x, g_mix, sh, sc)), x


def _sgu_spatial(v, sgu_w, sgu_bt):
    rows, width = v.shape
    gdim = width // (sgu_w.shape[0] // CHUNK)
    groups = width // gdim
    expand = (lax.broadcasted_iota(jnp.int32, (groups, width), 1) // gdim
              == lax.broadcasted_iota(jnp.int32, (groups, width), 0)).astype(F32)
    bias = mmf(sgu_bt, expand)
    lane = lax.broadcasted_iota(jnp.int32, (CHUNK, LANE), 1)
    per_lane_block = LANE // gdim
    chunks = []
    for ci in range(rows // CHUNK):
        vc = v[ci * CHUNK:(ci + 1) * CHUNK]
        blocks = []
        for lb in range(width // LANE):
            vb = vc[:, lb * LANE:(lb + 1) * LANE]
            acc = None
            for s in range(per_lane_block):
                g = lb * per_lane_block + s
                r = mmb(sgu_w[g * CHUNK:(g + 1) * CHUNK], vb)
                sel = (lane // gdim) == s
                acc = jnp.where(sel, r, 0.0) if acc is None else jnp.where(sel, r, acc)
            blocks.append(acc)
        chunks.append(jnp.concatenate(blocks, axis=1) + bias)
    return jnp.concatenate(chunks, axis=0)


def _fn_b(pu, prest, ysc, s5_d, w_glu, b_glu, ln_g, ln_b, sgu_w, sgu_bt, w_pa, w_pb, b_gate):
    sw = ln_g.shape[1]
    y = jax.nn.gelu(pu * s5_d + ysc)
    ya = y * jax.nn.sigmoid(mmb(y, w_glu) + b_glu)
    z = jax.nn.gelu(prest[:, :2 * sw])
    u, v = z[:, :sw], z[:, sw:]
    vc = v - jnp.mean(v, axis=-1, keepdims=True)
    v = vc * lax.rsqrt(jnp.mean(vc * vc, axis=-1, keepdims=True) + EPS) * ln_g + ln_b
    yb = u * _sgu_spatial(v, sgu_w, sgu_bt)
    gates = jax.nn.sigmoid(prest[:, 2 * sw:] + b_gate)
    d = gates.shape[1] // 2
    return (_b16(gates[:, :d] * mmb(ya, w_pa) + gates[:, d:] * mmb(yb, w_pb)),)


def _fn_c(x, mo, ga1, g_ffn, sh2, sc2):
    x1 = x + ga1 * mo
    return x1, _b16(_modulate(x1, g_ffn, sh2, sc2))


def _fn_e(x1, dn, tgt, ga2, g_final):
    y = _rms(x1 + ga2 * dn, g_final)
    err = (y - tgt) ** 2
    return 0.5 * jnp.sum(jnp.mean(err, axis=-1, keepdims=True), axis=0, keepdims=True)


def _s5_direction(u, a_re, a_im, log_step, bt_re, bt_im, c_re, c_im, rev):
    nc, width = u.shape
    t_len = width // S5_H
    n_state = a_re.shape[1]
    dt = jnp.exp(log_step)
    lr, li = a_re * dt, a_im * dt
    mag = jnp.exp(lr)
    ab_re, ab_im = mag * jnp.cos(li), mag * jnp.sin(li)
    p, q = ab_re - 1.0, ab_im
    den = a_re * a_re + a_im * a_im
    k_re, k_im = (p * a_re + q * a_im) / den, (q * a_re - p * a_im) / den
    bb_re = k_re * bt_re - k_im * bt_im
    bb_im = k_re * bt_im + k_im * bt_re
    row = lax.broadcasted_iota(jnp.int32, (width, S5_H), 0)
    tile = ((row % S5_H) == lax.broadcasted_iota(jnp.int32, (width, S5_H), 1)).astype(F32)
    bbr, bbi = mmf(tile, bb_re), mmf(tile, bb_im)
    ccr, cci = mmf(tile, c_re), mmf(tile, c_im)
    pos_i = lax.broadcasted_iota(jnp.int32, (width, 1), 0) // S5_H
    if rev:
        pos_i = (t_len - 1) - pos_i
    pos = pos_i.astype(F32)

    def power(e):
        m = jnp.exp(lr * e)
        return m * jnp.cos(li * e), m * jnp.sin(li * e)

    def cmul(xr, xi, yr, yi):
        return xr * yr - xi * yi, xr * yi + xi * yr

    pr, pi = cmul(*power((t_len - 1.0) - pos), bbr, bbi)
    rr, ri = cmul(*power(pos - (t_len - 1.0)), ccr, cci)
    wr, wi = cmul(*power(pos + 1.0), ccr, cci)
    toep = mmf_nt(pr, rr) - mmf_nt(pi, ri)
    col_pos = lax.broadcasted_iota(jnp.int32, (width, width), 1) // S5_H
    row_pos = lax.broadcasted_iota(jnp.int32, (width, width), 0) // S5_H
    keep = (col_pos <= row_pos) if rev else (col_pos >= row_pos)
    toep = jnp.where(keep, toep, 0.0)

    sr, si = mmb(u, pr), mmb(u, pi)
    k = 1
    while k < nc:
        ar, ai = power(jnp.full((1, 1), float(t_len * k), F32))
        hr, hi = _shift(sr, k, rev), _shift(si, k, rev)
        sr, si = sr + ar * hr - ai * hi, si + ar * hi + ai * hr
        k *= 2
    cr_in, ci_in = _shift(sr, 1, rev), _shift(si, 1, rev)
    del n_state
    return mmb(u, toep) + mmb_nt(cr_in, wr) - mmb_nt(ci_in, wi)


def _fn_s5(u, *prm):
    out = None
    for d in range(2):
        y = _s5_direction(u, *[p[d] for p in prm], rev=(d == 1))
        out = y if out is None else out + y
    return out


def _s5_specs(prm):
    return [pl.BlockSpec((2, 1) + p.shape[2:], lambda g: (0, g, 0, 0)) for p in prm]


def _s5_fwd(u_all, prm):
    g, nc, width = u_all.shape
    u_spec = pl.BlockSpec((1, nc, width), lambda i: (i, 0, 0))

    def body(*refs):
        u = refs[0][0]
        pv = [r[:, 0] for r in refs[1:1 + len(prm)]]
        refs[-1][0] = _fn_s5(u, *pv)

    return _call(body, name="s5_fwd", grid=(g,), in_specs=[u_spec] + _s5_specs(prm), out_specs=u_spec,
                 out_shape=jax.ShapeDtypeStruct(u_all.shape, F32), compiler_params=_params(1))(u_all, *prm)


def _s5_bwd(u_all, prm, dy_all):
    g, nc, width = u_all.shape
    u_spec = pl.BlockSpec((1, nc, width), lambda i: (i, 0, 0))
    npar = len(prm)

    def body(*refs):
        u = refs[0][0]
        pv = [r[:, 0] for r in refs[1:1 + npar]]
        dy = refs[1 + npar][0]
        _, vjp = jax.vjp(_fn_s5, u, *pv)
        grads = vjp(dy)
        refs[2 + npar][0] = grads[0]
        for o_ref, gv in zip(refs[3 + npar:], grads[1:]):
            o_ref[:, 0] = gv

    return _call(body, name="s5_bwd", grid=(g,), in_specs=[u_spec] + _s5_specs(prm) + [u_spec],
                 out_specs=[u_spec] + _s5_specs(prm),
                 out_shape=[jax.ShapeDtypeStruct(u_all.shape, F32)]
                 + [jax.ShapeDtypeStruct(p.shape, F32) for p in prm],
                 compiler_params=_params(1))(u_all, *prm, dy_all)


def _to_groups(tok):
    n, width = tok.shape
    g = width // S5_H
    return jnp.transpose(tok.reshape(n, g, S5_H), (1, 0, 2)).reshape(g, n // S5_T, S5_T * S5_H)


def _from_groups(grp):
    g, nc, _ = grp.shape
    return jnp.transpose(grp.reshape(g, nc * S5_T, S5_H), (1, 0, 2)).reshape(nc * S5_T, g * S5_H)


def _conv_taps(xp, xm, xn, blk_i, rows_total):
    tb = xm.shape[0]
    buf = jnp.concatenate([xp, xm, xn], axis=0).astype(F32)
    n = tb + 2 * HALO
    col = lax.broadcasted_iota(jnp.int32, (n, 1), 0) % GRID_W
    left = jnp.where(col >= 1, pltpu.roll(buf, 1, 0), 0.0)
    right = jnp.where(col <= GRID_W - 2, pltpu.roll(buf, n - 1, 0), 0.0)
    shifted = (left, buf, right)
    grid_row = blk_i * (tb // GRID_W) + lax.broadcasted_iota(jnp.int32, (tb, 1), 0) // GRID_W
    taps = []
    for di in range(3):
        src = grid_row + (di - 1)
        ok = (src >= 0) & (src < rows_total)
        start = HALO + (di - 1) * GRID_W
        for dj in range(3):
            taps.append(jnp.where(ok, shifted[dj][start:start + tb], 0.0))
    return taps


def _conv_sum(taps, w_ref):
    acc = None
    for k, tap in enumerate(taps):
        term = tap * w_ref[k:k + 1, :]
        acc = term if acc is None else acc + term
    return acc


def _conv_geometry(n_tok, width):
    tb = _pick(n_tok, 1024, HALO)
    cb = _pick(width, 256)
    nb = tb // HALO
    last = n_tok // HALO - 1
    main = pl.BlockSpec((tb, cb), lambda j, i: (i, j))
    prev = pl.BlockSpec((HALO, cb), lambda j, i: (jnp.maximum(i * nb - 1, 0), j))
    nxt = pl.BlockSpec((HALO, cb), lambda j, i: (jnp.minimum(i * nb + nb, last), j))
    par = lambda r: pl.BlockSpec((r, cb), lambda j, i: (0, j))
    return tb, cb, main, prev, nxt, par


def _conv_act_fwd(up_g, up_v, w_g, w_v, b_g, b_v):
    n_tok, width = up_g.shape
    rows_total = n_tok // GRID_W
    tb, cb, main, prev, nxt, par = _conv_geometry(n_tok, width)

    def body(gp, gm, gn, vp, vm, vn, wg, wv, bg, bv, o_ref):
        i = pl.program_id(1)
        gate = _conv_sum(_conv_taps(gp[...], gm[...], gn[...], i, rows_total), wg) + bg[...]
        val = _conv_sum(_conv_taps(vp[...], vm[...], vn[...], i, rows_total), wv) + bv[...]
        o_ref[...] = (jax.nn.silu(gate) * val).astype(o_ref.dtype)

    return _call(body, name="conv_act_fwd", grid=(width // cb, n_tok // tb),
                 in_specs=[prev, main, nxt, prev, main, nxt, par(9), par(9), par(1), par(1)], out_specs=main,
                 out_shape=jax.ShapeDtypeStruct((n_tok, width), BF16),
                 compiler_params=_params(2))(up_g, up_g, up_g, up_v, up_v, up_v, w_g, w_v, b_g, b_v)


def _conv_act_bwd(up_g, up_v, w_g, w_v, b_g, b_v, d_act):
    n_tok, width = up_g.shape
    rows_total = n_tok // GRID_W
    tb, cb, main, prev, nxt, par = _conv_geometry(n_tok, width)

    def body(gp, gm, gn, vp, vm, vn, wg, wv, bg, bv, da, dcg, dcv, dwg, dwv, dbg, dbv):
        i = pl.program_id(1)
        taps_g = _conv_taps(gp[...], gm[...], gn[...], i, rows_total)
        taps_v = _conv_taps(vp[...], vm[...], vn[...], i, rows_total)
        gate = _conv_sum(taps_g, wg) + bg[...]
        val = _conv_sum(taps_v, wv) + bv[...]
        sig = jax.nn.sigmoid(gate)
        d = da[...].astype(F32)
        d_gate = d * val * sig * (1.0 + gate * (1.0 - sig))
        d_val = d * gate * sig
        dcg[...] = d_gate
        dcv[...] = d_val

        @pl.when(i == 0)
        def _():
            for r in (dwg, dwv, dbg, dbv):
                r[...] = jnp.zeros_like(r)

        dbg[...] += jnp.sum(d_gate, axis=0, keepdims=True)
        dbv[...] += jnp.sum(d_val, axis=0, keepdims=True)
        for k in range(9):
            dwg[k:k + 1, :] += jnp.sum(taps_g[k] * d_gate, axis=0, keepdims=True)
            dwv[k:k + 1, :] += jnp.sum(taps_v[k] * d_val, axis=0, keepdims=True)

    shp = jax.ShapeDtypeStruct
    return _call(body, name="conv_act_bwd", grid=(width // cb, n_tok // tb),
                 in_specs=[prev, main, nxt, prev, main, nxt, par(9), par(9), par(1), par(1), main],
                 out_specs=[main, main, par(9), par(9), par(1), par(1)],
                 out_shape=[shp((n_tok, width), F32), shp((n_tok, width), F32), shp((9, width), F32),
                            shp((9, width), F32), shp((1, width), F32), shp((1, width), F32)],
                 compiler_params=_params(2))(up_g, up_g, up_g, up_v, up_v, up_v, w_g, w_v, b_g, b_v, d_act)


def _conv_plain(x, w, name):
    n_tok, width = x.shape
    rows_total = n_tok // GRID_W
    tb, cb, main, prev, nxt, par = _conv_geometry(n_tok, width)

    def body(xp, xm, xn, w_ref, o_ref):
        taps = _conv_taps(xp[...], xm[...], xn[...], pl.program_id(1), rows_total)
        o_ref[...] = _conv_sum(taps, w_ref).astype(o_ref.dtype)

    return _call(body, name=name, grid=(width // cb, n_tok // tb), in_specs=[prev, main, nxt, par(9)],
                 out_specs=main, out_shape=jax.ShapeDtypeStruct((n_tok, width), BF16),
                 compiler_params=_params(2))(x, x, x, w)


def _adamw(w, g_parts, m, v, name):
    rows, cols = w.shape
    parts = g_parts.shape[0]
    blk = _pick(rows, 256, 8)
    spec = pl.BlockSpec((blk, cols), lambda i: (i, 0))

    def body(w_ref, g_ref, m_ref, v_ref, g_out, d_out, m_out, v_out):
        g = g_ref[0].astype(F32)
        for p in range(1, parts):
            g = g + g_ref[p].astype(F32)
        m_new = ADAM_B1 * m_ref[...] + (1.0 - ADAM_B1) * g
        v_new = ADAM_B2 * v_ref[...] + (1.0 - ADAM_B2) * (g * g)
        m_hat = m_new / (1.0 - ADAM_B1 ** ADAM_STEP)
        v_hat = v_new / (1.0 - ADAM_B2 ** ADAM_STEP)
        g_out[...] = g
        d_out[...] = -ADAM_LR * (m_hat / (jnp.sqrt(v_hat) + ADAM_EPS) + ADAM_WD * w_ref[...])
        m_out[...] = m_new
        v_out[...] = v_new

    return _call(body, name=name, grid=(rows // blk,),
                 in_specs=[spec, pl.BlockSpec((parts, blk, cols), lambda i: (0, i, 0)), spec, spec],
                 out_specs=[spec] * 4, out_shape=[jax.ShapeDtypeStruct((rows, cols), F32)] * 4,
                 compiler_params=_params(1))(w, g_parts, m, v)


def _exchange(items, name):
    n = len(items)
    hbm = pl.BlockSpec(memory_space=pl.ANY)

    def body(*refs):
        srcs, outs = refs[:n], refs[n:2 * n]
        send_sems, recv_sems, own_sems = refs[2 * n:]
        x, y, c = lax.axis_index("x"), lax.axis_index("y"), lax.axis_index("c")
        me = 4 * x + 2 * y + c
        own = []
        for i, (_, mode) in enumerate(items):
            src = srcs[i] if mode == "gather" else srcs[i].at[me]
            cp = pltpu.make_async_copy(src, outs[i].at[me], own_sems.at[i])
            cp.start()
            own.append(cp)
        sent = []
        for i, (_, mode) in enumerate(items):
            for k in range(1, N_DEV):
                px = 1 - x if k & 4 else x
                py = 1 - y if k & 2 else y
                pc = 1 - c if k & 1 else c
                peer = 4 * px + 2 * py + pc
                src = srcs[i] if mode == "gather" else srcs[i].at[peer]
                cp = pltpu.make_async_remote_copy(
                    src_ref=src, dst_ref=outs[i].at[me], send_sem=send_sems.at[i, k - 1],
                    recv_sem=recv_sems.at[i, k - 1], device_id=(px, py, pc), device_id_type=pl.DeviceIdType.MESH)
                cp.start()
                landing = pltpu.make_async_remote_copy(
                    src_ref=src, dst_ref=outs[i].at[peer], send_sem=send_sems.at[i, k - 1],
                    recv_sem=recv_sems.at[i, k - 1], device_id=(px, py, pc), device_id_type=pl.DeviceIdType.MESH)
                sent.append((cp, landing))
        for cp in own:
            cp.wait()
        for cp, landing in sent:
            cp.wait_send()
            landing.wait_recv()

    out_shape = [jax.ShapeDtypeStruct((N_DEV,) + (a.shape if mode == "gather" else a.shape[1:]), a.dtype)
                 for a, mode in items]
    return _call(body, name=name, in_specs=[hbm] * n, out_specs=[hbm] * n, out_shape=out_shape,
                 scratch_shapes=[pltpu.SemaphoreType.DMA((n, N_DEV - 1)), pltpu.SemaphoreType.DMA((n, N_DEV - 1)),
                                 pltpu.SemaphoreType.DMA((n,))])(*[a for a, _ in items])


def _cols_from_blocks(g):
    return jnp.transpose(g, (1, 0, 2)).reshape(g.shape[1], N_DEV * g.shape[2])


def _blocks_from_cols(w):
    r, c8 = w.shape
    return jnp.transpose(w.reshape(r, N_DEV, c8 // N_DEV), (1, 0, 2))


def _pack(arrs):
    flat = jnp.concatenate([a.reshape(-1).astype(F32) for a in arrs])
    pad = (-flat.shape[0]) % (8 * LANE)
    return jnp.pad(flat, (0, pad)).reshape(-1, LANE)


def _unpack(packed, shapes):
    flat = packed.reshape(-1)
    out, off = [], 0
    for s in shapes:
        size = math.prod(s)
        out.append(flat[off:off + size].reshape(s))
        off += size
    return out


def kernel(x, c, ctx, c_ctx, w_ada, b_ada, g_mix, w_in, s5_a_re, s5_a_im, s5_log_step, s5_b_re, s5_b_im, s5_c_re, s5_c_im, s5_d, s5_w_glu, s5_b_glu, sgu_ln_g, sgu_ln_b, sgu_w, sgu_b, w_proj_a, w_proj_b, b_gate, w_out, g_ffn, w_up, conv_w, conv_b, w_down, g_final, loss_target, m_c_ctx, m_w_ada, m_b_ada, m_g_mix, m_w_in, m_s5_a_re, m_s5_a_im, m_s5_log_step, m_s5_b_re, m_s5_b_im, m_s5_c_re, m_s5_c_im, m_s5_d, m_s5_w_glu, m_s5_b_glu, m_sgu_ln_g, m_sgu_ln_b, m_sgu_w, m_sgu_b, m_w_proj_a, m_w_proj_b, m_b_gate, m_w_out, m_g_ffn, m_w_up, m_conv_w, m_conv_b, m_w_down, m_g_final, v_c_ctx, v_w_ada, v_b_ada, v_g_mix, v_w_in, v_s5_a_re, v_s5_a_im, v_s5_log_step, v_s5_b_re, v_s5_b_im, v_s5_c_re, v_s5_c_im, v_s5_d, v_s5_w_glu, v_s5_b_glu, v_sgu_ln_g, v_sgu_ln_b, v_sgu_w, v_sgu_b, v_w_proj_a, v_w_proj_b, v_b_gate, v_w_out, v_g_ffn, v_w_up, v_conv_w, v_conv_b, v_w_down, v_g_final):
    given = dict(locals())
    wts = {n: given[n] for n in WEIGHTS}
    mom1 = {n: given["m_" + n] for n in WEIGHTS}
    mom2 = {n: given["v_" + n] for n in WEIGHTS}

    me = 4 * lax.axis_index("x") + 2 * lax.axis_index("y") + lax.axis_index("c")
    xs, cx, tgt = x[0], ctx[0], loss_target[0]
    n_tok, d = xs.shape
    n_ctx = cx.shape[0]
    s5w = s5_d.shape[1]
    ffn = w_down.shape[1] * N_DEV
    in_w = w_in.shape[2] * N_DEV
    n_mod = w_ada.shape[2] * N_DEV // d
    mod_cols = w_ada.shape[2]

    def two_d(a):
        return a.reshape(-1, a.shape[-1])

    conv_w9 = conv_w[0].reshape(9, -1)
    gathered = _exchange(
        [(c, "gather"), (_b16(w_in[0]), "gather"), (_b16(s5_w_glu[0]), "gather"), (_b16(w_proj_a[0]), "gather"),
         (_b16(w_proj_b[0]), "gather"), (_b16(w_out[0]), "gather"), (_b16(w_up[0]), "gather"),
         (conv_w9, "gather"), (_b16(w_down[0]), "gather")], "gather_weights")
    c_all = gathered[0].reshape(N_DEV, d)
    w_in_f = _cols_from_blocks(gathered[1])
    w_glu_f = gathered[2].reshape(-1, s5w)
    w_pa_f = _cols_from_blocks(gathered[3])
    w_pb_f = _cols_from_blocks(gathered[4])
    w_out_f = gathered[5].reshape(-1, d)
    w_up_f = _cols_from_blocks(gathered[6])
    conv_w_f = _cols_from_blocks(gathered[7])
    w_down_f = gathered[8].reshape(-1, d)

    cs_in = jnp.concatenate([c_all, jnp.broadcast_to(c_ctx[None, :], (N_DEV, d))], axis=0)
    w_ada_loc = w_ada[0]
    (mod_mine,) = _stage_fwd(_fn_mod, [cs_in], [w_ada_loc], [(mod_cols, F32)], blk=2 * N_DEV, name="mod_fwd")
    (mod_blocks,) = _exchange([(mod_mine, "gather")], "gather_mod")
    mod_all = _cols_from_blocks(mod_blocks) + b_ada
    mod = lax.dynamic_slice_in_dim(mod_all, me, 1, axis=0)
    mod_c = mod_all[N_DEV:N_DEV + 1]
    sh1, sc1, ga1, sh2, sc2, ga2 = [mod[:, i * d:(i + 1) * d] for i in range(n_mod)]
    sh1c, sc1c = mod_c[:, :d], mod_c[:, d:2 * d]

    a_par = [g_mix, sh1, sc1]
    ac_par = [g_mix, sh1c, sc1c]
    (h,) = _stage_fwd(_fn_a, [xs], a_par, [(d, BF16)], blk=_pick(n_tok, 512, 8), name="modulate1_fwd")
    (hc,) = _stage_fwd(_fn_a, [cx], ac_par, [(d, BF16)], blk=_pick(n_ctx, 512, 8), name="modulate1_ctx_fwd")
    pu = _mm(h, w_in_f, n=s5w, name="proj_u")
    prest = _mm(h, w_in_f, n=in_w - s5w, b_n0=s5w, name="proj_rest")
    puc = _mm(hc, w_in_f, n=s5w, name="proj_u_ctx")

    s5_prm = [s5_a_re[0][:, :, None, :], s5_a_im[0][:, :, None, :], s5_log_step[0][:, :, None, None],
              jnp.swapaxes(s5_b_re[0], 2, 3), jnp.swapaxes(s5_b_im[0], 2, 3), s5_c_re[0], s5_c_im[0]]
    n_seq = n_tok + 2 * n_ctx
    n_pad = (-(n_seq // S5_T)) % LANE * S5_T
    u_all = _to_groups(jnp.concatenate([puc, pu, puc, jnp.zeros((n_pad, s5w), F32)], axis=0))
    ysc = _from_groups(_s5_fwd(u_all, s5_prm))[n_ctx:n_ctx + n_tok]

    b_par = [s5_d, w_glu_f, s5_b_glu, sgu_ln_g, sgu_ln_b, two_d(sgu_w[0]), jnp.transpose(sgu_b[0]),
             w_pa_f, w_pb_f, b_gate]
    b_rows = [pu, prest, ysc]
    b_blk = _pick(n_tok, 256, CHUNK)
    (mpre,) = _stage_fwd(_fn_b, b_rows, b_par, [(d, BF16)], blk=b_blk, name="mixers_fwd")
    mo = _mm(mpre, w_out_f, name="out_proj")

    c_par = [ga1, g_ffn, sh2, sc2]
    c_blk = _pick(n_tok, 512, 8)
    x1, h2 = _stage_fwd(_fn_c, [xs, mo], c_par, [(d, F32), (d, BF16)], blk=c_blk, name="modulate2_fwd")
    up_g = _mm(h2, w_up_f, n=ffn, name="up_gate")
    up_v = _mm(h2, w_up_f, n=ffn, b_n0=ffn, name="up_val")
    cw_g, cw_v = conv_w_f[:, :ffn], conv_w_f[:, ffn:]
    cb_g, cb_v = conv_b[:, :ffn], conv_b[:, ffn:]
    act = _conv_act_fwd(up_g, up_v, cw_g, cw_v, cb_g, cb_v)
    dn = _mm(act, w_down_f, name="down_proj")

    loss_part, d_x1a, d_dn, d_ga2, d_g_final = _stage_loss(
        _fn_e, [x1, dn, tgt], [ga2, g_final[None, :]], blk=c_blk, name="loss_head",
        row_grads=[(0, F32), (1, BF16)])

    d_act = _mm(d_dn, w_down_f, tb=True, name="down_proj_dx")
    g_w_down = _mm(act, d_dn, ta=True, name="down_proj_dw")
    dcg, dcv, g_cw_g, g_cw_v, g_cb_g, g_cb_v = _conv_act_bwd(up_g, up_v, cw_g, cw_v, cb_g, cb_v, d_act)
    dug = _conv_plain(dcg, cw_g[::-1], "conv_dx_gate")
    duv = _conv_plain(dcv, cw_v[::-1], "conv_dx_val")
    d_h2 = _mm(dug, w_up_f, tb=True, name="up_gate_dx")
    d_h2 = _mm(duv, w_up_f, tb=True, b_k0=ffn, add=d_h2, name="up_val_dx")
    g_w_up = jnp.concatenate([_mm(h2, dug, ta=True, name="up_gate_dw"), _mm(h2, duv, ta=True, name="up_val_dw")],
                             axis=1)
    (d_xc, d_mo), (d_ga1, g_g_ffn, d_sh2, d_sc2) = _split(_stage_bwd(
        _fn_c, [xs, mo], c_par, [d_x1a, d_h2], blk=c_blk, name="modulate2_bwd",
        row_grads=[(0, F32), (1, BF16)]), 2)

    d_mpre = _mm(d_mo, w_out_f, tb=True, name="out_proj_dx")
    g_w_out = _mm(mpre, d_mo, ta=True, name="out_proj_dw")
    (d_pu, d_prest, d_ysc), b_grads = _split(_stage_bwd(
        _fn_b, b_rows, b_par, [d_mpre], blk=b_blk, name="mixers_bwd",
        row_grads=[(0, F32), (1, BF16), (2, F32)]), 3)
    (g_s5_d, g_w_glu, g_b_glu, g_ln_g, g_ln_b, g_sgu_w, g_sgu_bt, g_w_pa, g_w_pb, g_b_gate) = b_grads

    dy_all = _to_groups(jnp.pad(d_ysc, ((n_ctx, n_ctx + n_pad), (0, 0))))
    s5_out = _s5_bwd(u_all, s5_prm, dy_all)
    du_tok = _from_groups(s5_out[0])
    d_pu = d_pu + du_tok[n_ctx:n_ctx + n_tok]
    d_puc = du_tok[:n_ctx] + du_tok[n_ctx + n_tok:n_seq]
    g_a_re, g_a_im, g_ls, g_bt_re, g_bt_im, g_c_re, g_c_im = s5_out[1:]

    d_h = _mm(d_pu, w_in_f, tb=True, name="proj_u_dx")
    d_h = _mm(d_prest, w_in_f, tb=True, b_k0=s5w, add=d_h, name="proj_rest_dx")
    d_hc = _mm(d_puc, w_in_f, tb=True, name="proj_u_ctx_dx")
    g_w_in_u = _mm(hc, d_puc, ta=True, name="proj_u_ctx_dw")
    g_w_in_u = _mm(h, d_pu, ta=True, add=g_w_in_u, name="proj_u_dw")
    g_w_in = jnp.concatenate([g_w_in_u, _mm(h, d_prest, ta=True, name="proj_rest_dw")], axis=1)

    (grad_x,), (g_g_mix_x, d_sh1, d_sc1) = _split(_stage_bwd(
        _fn_a_res, [xs], a_par, [d_h, d_xc], blk=c_blk, name="modulate1_bwd", row_grads=[(0, F32)]), 1)
    _, (g_g_mix_c, d_sh1c, d_sc1c) = _split(_stage_bwd(
        _fn_a, [cx], ac_par, [d_hc], blk=_pick(n_ctx, 512, 8), name="modulate1_ctx_bwd", row_grads=[]), 0)

    zeros = jnp.zeros((1, (n_mod - 2) * d), F32)
    d_mod = jnp.concatenate([d_sh1, d_sc1, d_ga1, d_sh2, d_sc2, d_ga2], axis=1)
    d_mod_c = jnp.concatenate([d_sh1c, d_sc1c, zeros], axis=1)
    (d_mod_all,) = _exchange([(jnp.concatenate([d_mod, d_mod_c], axis=0), "gather")], "gather_dmod")
    d_mod_rows = jnp.transpose(d_mod_all, (1, 0, 2)).reshape(2 * N_DEV, n_mod * d)
    d_mod_mine = lax.dynamic_slice_in_dim(d_mod_rows, me * mod_cols, mod_cols, axis=1)
    (d_cs,), (g_w_ada,) = _split(_stage_bwd(
        _fn_mod, [cs_in], [w_ada_loc], [d_mod_mine], blk=2 * N_DEV, name="mod_bwd", row_grads=[(0, F32)]), 1)

    part = {
        'c_ctx': jnp.sum(d_cs[N_DEV:], axis=0), 'b_ada': d_mod + d_mod_c, 'g_mix': g_g_mix_x + g_g_mix_c,
        's5_a_re': g_a_re, 's5_a_im': g_a_im, 's5_log_step': g_ls,
        's5_b_re': jnp.swapaxes(g_bt_re, 2, 3), 's5_b_im': jnp.swapaxes(g_bt_im, 2, 3),
        's5_c_re': g_c_re, 's5_c_im': g_c_im, 's5_d': g_s5_d, 's5_b_glu': g_b_glu, 'sgu_ln_g': g_ln_g,
        'sgu_ln_b': g_ln_b, 'sgu_w': g_sgu_w, 'sgu_b': jnp.transpose(g_sgu_bt), 'b_gate': g_b_gate,
        'g_ffn': g_g_ffn, 'conv_b': jnp.concatenate([g_cb_g, g_cb_v], axis=1), 'g_final': d_g_final,
    }
    rep_shapes = [wts[n].shape for n in REPLICATED]
    g_full = {
        'w_in': g_w_in, 'w_proj_a': g_w_pa, 'w_proj_b': g_w_pb, 'w_up': g_w_up,
        'conv_w': jnp.concatenate([g_cw_g, g_cw_v], axis=1), 's5_w_glu': g_w_glu, 'w_out': g_w_out,
        'w_down': g_w_down,
    }
    big = [n for n in SHARDED if n != 'w_ada']
    items = [(_pack([part[n] for n in REPLICATED]), "gather")]
    for n in big:
        g = g_full[n]
        blocks = _blocks_from_cols(g) if n in COL_SHARDED else g.reshape(N_DEV, g.shape[0] // N_DEV, g.shape[1])
        items.append((blocks, "a2a"))
    reduced = _exchange(items, "exchange_grads")

    out = {}
    rep = _adamw(_pack([wts[n] for n in REPLICATED]), reduced[0], _pack([mom1[n] for n in REPLICATED]),
                 _pack([mom2[n] for n in REPLICATED]), "adamw_replicated")
    rep = [_unpack(r, rep_shapes) for r in rep]
    for i, n in enumerate(REPLICATED):
        out[n] = tuple(r[i] for r in rep)
    for n, parts in zip(big, reduced[1:]):
        shape = wts[n].shape
        res = _adamw(two_d(wts[n]), parts, two_d(mom1[n]), two_d(mom2[n]), "adamw_" + n)
        out[n] = tuple(r.reshape(shape) for r in res)
    res = _adamw(w_ada_loc, g_w_ada[None], m_w_ada[0], v_w_ada[0], "adamw_w_ada")
    out['w_ada'] = tuple(r.reshape(w_ada.shape) for r in res)

    loss = lax.psum(loss_part[0, 0], ("x", "y", "c"))
    return (loss, grad_x[None], *[out[n][0] for n in WEIGHTS], *[out[n][1] for n in WEIGHTS],
            *[out[n][2] for n in WEIGHTS], *[out[n][3] for n in WEIGHTS])


def _split(res, n_rows):
    return tuple(res[:n_rows]), tuple(res[n_rows:])
```

```python
import functools
import math

import jax
import jax.numpy as jnp
from jax import lax
from jax.experimental import pallas as pl
from jax.experimental.pallas import tpu as pltpu

F32 = jnp.float32
BF16 = jnp.bfloat16
HI = lax.Precision.HIGHEST

N_DEV = 8
GRID_W = 64
CHUNK = 128
EPS = 1e-6
S5_T = 32
S5_H = 16
LANE = 128
HALO = 128
VMEM_LIMIT = 56 * 1024 * 1024

ADAM_LR = 0.001
ADAM_B1 = 0.9
ADAM_B2 = 0.999
ADAM_EPS = 1e-08
ADAM_WD = 0.01
ADAM_STEP = 10

WEIGHTS = ['c_ctx', 'w_ada', 'b_ada', 'g_mix', 'w_in', 's5_a_re', 's5_a_im', 's5_log_step', 's5_b_re', 's5_b_im',
           's5_c_re', 's5_c_im', 's5_d', 's5_w_glu', 's5_b_glu', 'sgu_ln_g', 'sgu_ln_b', 'sgu_w', 'sgu_b',
           'w_proj_a', 'w_proj_b', 'b_gate', 'w_out', 'g_ffn', 'w_up', 'conv_w', 'conv_b', 'w_down', 'g_final']
COL_SHARDED = ('w_ada', 'w_in', 'w_proj_a', 'w_proj_b', 'w_up', 'conv_w')
ROW_SHARDED = ('s5_w_glu', 'w_out', 'w_down')
SHARDED = COL_SHARDED + ROW_SHARDED
REPLICATED = [n for n in WEIGHTS if n not in SHARDED]


def _call(body, **kw):
    return pl.pallas_call(body, **kw)


def _params(n_grid):
    return pltpu.CompilerParams(dimension_semantics=("arbitrary",) * n_grid, vmem_limit_bytes=VMEM_LIMIT)


def _pick(dim, pref, unit=LANE):
    best = None
    d = unit
    while d <= min(dim, pref):
        if dim % d == 0:
            best = d
        d += unit
    return best if best is not None else dim


def _dg(a, b, ca, cb, prec=None):
    return lax.dot_general(a, b, (((ca,), (cb,)), ((), ())), precision=prec, preferred_element_type=F32)


def _b16(v):
    return v.astype(BF16)


@jax.custom_vjp
def mmb(a, b):
    return _dg(_b16(a), _b16(b), 1, 0)


def _mmb_fwd(a, b):
    return mmb(a, b), (a, b)


def _mmb_bwd(res, g):
    a, b = res
    g = _b16(g)
    return _dg(g, _b16(b), 1, 1).astype(a.dtype), _dg(_b16(a), g, 0, 0).astype(b.dtype)


mmb.defvjp(_mmb_fwd, _mmb_bwd)


@jax.custom_vjp
def mmb_nt(a, b):
    return _dg(_b16(a), _b16(b), 1, 1)


def _mmb_nt_fwd(a, b):
    return mmb_nt(a, b), (a, b)


def _mmb_nt_bwd(res, g):
    a, b = res
    g = _b16(g)
    return _dg(g, _b16(b), 1, 0).astype(a.dtype), _dg(g, _b16(a), 0, 0).astype(b.dtype)


mmb_nt.defvjp(_mmb_nt_fwd, _mmb_nt_bwd)


@jax.custom_vjp
def mmf(a, b):
    return _dg(a, b, 1, 0, HI)


def _mmf_fwd(a, b):
    return mmf(a, b), (a, b)


def _mmf_bwd(res, g):
    a, b = res
    return _dg(g, b, 1, 1, HI), _dg(a, g, 0, 0, HI)


mmf.defvjp(_mmf_fwd, _mmf_bwd)


@jax.custom_vjp
def mmf_nt(a, b):
    return _dg(a, b, 1, 1, HI)


def _mmf_nt_fwd(a, b):
    return mmf_nt(a, b), (a, b)


def _mmf_nt_bwd(res, g):
    a, b = res
    return _dg(g, b, 1, 0, HI), _dg(g, a, 0, 0, HI)


mmf_nt.defvjp(_mmf_nt_fwd, _mmf_nt_bwd)


def _shift_impl(x, k, up):
    n = x.shape[0]
    idx = lax.broadcasted_iota(jnp.int32, (n, 1), 0)
    if up:
        return jnp.where(idx < n - k, pltpu.roll(x, n - k, 0), 0.0)
    return jnp.where(idx >= k, pltpu.roll(x, k, 0), 0.0)


@functools.partial(jax.custom_vjp, nondiff_argnums=(1, 2))
def _shift(x, k, up):
    return _shift_impl(x, k, up)


def _shift_fwd(x, k, up):
    return _shift_impl(x, k, up), None


def _shift_bwd(k, up, _, g):
    return (_shift_impl(g, k, not up),)


_shift.defvjp(_shift_fwd, _shift_bwd)


def _mm(a, b, *, name, ta=False, tb=False, add=None, out_dtype=F32, tm_pref=1408, tn_pref=1408, tk_pref=1408):
    m, k = (a.shape[1], a.shape[0]) if ta else a.shape
    n = b.shape[0] if tb else b.shape[1]
    tm, tn, tk = _pick(m, tm_pref), _pick(n, tn_pref), _pick(k, tk_pref)
    nk = k // tk
    a_spec = (pl.BlockSpec((tk, tm), lambda i, j, kk: (kk, i)) if ta
              else pl.BlockSpec((tm, tk), lambda i, j, kk: (i, kk)))
    b_spec = (pl.BlockSpec((tn, tk), lambda i, j, kk: (j, kk)) if tb
              else pl.BlockSpec((tk, tn), lambda i, j, kk: (kk, j)))
    o_spec = pl.BlockSpec((tm, tn), lambda i, j, kk: (i, j))
    ca, cb = (0 if ta else 1), (1 if tb else 0)
    has_add = add is not None
    in_place = out_dtype == F32 or nk == 1

    def body(*refs):
        a_ref, b_ref = refs[0], refs[1]
        o_ref = refs[3] if has_add else refs[2]
        prod = _dg(_b16(a_ref[...]), _b16(b_ref[...]), ca, cb)
        if nk == 1:
            if has_add:
                prod = prod + refs[2][...].astype(F32)
            o_ref[...] = prod.astype(o_ref.dtype)
            return
        acc_ref = o_ref if in_place else refs[-1]
        kk = pl.program_id(2)

        @pl.when(kk == 0)
        def _():
            acc_ref[...] = prod

        @pl.when(kk > 0)
        def _():
            acc_ref[...] += prod

        if has_add or not in_place:
            @pl.when(kk == nk - 1)
            def _():
                r = acc_ref[...]
                if has_add:
                    r = r + refs[2][...].astype(F32)
                o_ref[...] = r.astype(o_ref.dtype)

    ins = [a, b] + ([add] if has_add else [])
    in_specs = [a_spec, b_spec] + ([o_spec] if has_add else [])
    return _call(body, name=name, grid=(m // tm, n // tn, nk), in_specs=in_specs, out_specs=o_spec,
                 out_shape=jax.ShapeDtypeStruct((m, n), out_dtype),
                 scratch_shapes=[] if in_place else [pltpu.VMEM((tm, tn), F32)],
                 compiler_params=_params(3))(*ins)


def _row_spec(blk, width):
    return pl.BlockSpec((blk, width), lambda i: (i, 0))


def _whole_spec(shape):
    return pl.BlockSpec(shape, lambda i: (0,) * len(shape))


def _stage_fwd(fn, rows, params, outs, *, blk, name, n_rows=None):
    n = n_rows or rows[0].shape[0]
    nr, npar = len(rows), len(params)

    def body(*refs):
        vals = [r[...] for r in refs[:nr + npar]]
        res = fn(*vals)
        for o_ref, v in zip(refs[nr + npar:], res):
            o_ref[...] = v.astype(o_ref.dtype)

    return _call(body, name=name, grid=(n // blk,),
                 in_specs=[_row_spec(blk, r.shape[1]) for r in rows] + [_whole_spec(p.shape) for p in params],
                 out_specs=[_row_spec(blk, w) for w, _ in outs],
                 out_shape=[jax.ShapeDtypeStruct((n, w), dt) for w, dt in outs],
                 compiler_params=_params(1))(*rows, *params)


def _stage_bwd(fn, rows, params, cts, *, blk, name, row_grads, n_rows=None):
    n = n_rows or rows[0].shape[0]
    nr, npar, nct = len(rows), len(params), len(cts)

    def body(*refs):
        vals = [r[...].astype(F32) for r in refs[:nr + npar]]
        ct = [r[...] for r in refs[nr + npar:nr + npar + nct]]
        d_rows = refs[nr + npar + nct:nr + npar + nct + len(row_grads)]
        d_par = refs[nr + npar + nct + len(row_grads):]
        res, vjp = jax.vjp(fn, *vals)
        g = vjp(tuple(c.astype(r.dtype) for c, r in zip(ct, res)))
        for o_ref, (j, _) in zip(d_rows, row_grads):
            o_ref[...] = g[j].astype(o_ref.dtype)

        @pl.when(pl.program_id(0) == 0)
        def _():
            for o_ref in d_par:
                o_ref[...] = jnp.zeros_like(o_ref)

        for j, o_ref in enumerate(d_par):
            o_ref[...] += g[nr + j].astype(F32)

    return _call(body, name=name, grid=(n // blk,),
                 in_specs=([_row_spec(blk, r.shape[1]) for r in rows] + [_whole_spec(p.shape) for p in params]
                           + [_row_spec(blk, c.shape[1]) for c in cts]),
                 out_specs=([_row_spec(blk, rows[j].shape[1]) for j, _ in row_grads]
                            + [_whole_spec(p.shape) for p in params]),
                 out_shape=([jax.ShapeDtypeStruct((n, rows[j].shape[1]), dt) for j, dt in row_grads]
                            + [jax.ShapeDtypeStruct(p.shape, F32) for p in params]),
                 compiler_params=_params(1))(*rows, *params, *cts)


def _stage_loss(fn, rows, params, *, blk, name, row_grads):
    n = rows[0].shape[0]
    nr, npar = len(rows), len(params)

    def body(*refs):
        vals = [r[...].astype(F32) for r in refs[:nr + npar]]
        loss_ref = refs[nr + npar]
        d_rows = refs[nr + npar + 1:nr + npar + 1 + len(row_grads)]
        d_par = refs[nr + npar + 1 + len(row_grads):]
        res, vjp = jax.vjp(fn, *vals)
        g = vjp(jnp.ones_like(res))
        for o_ref, (j, _) in zip(d_rows, row_grads):
            o_ref[...] = g[j].astype(o_ref.dtype)

        @pl.when(pl.program_id(0) == 0)
        def _():
            loss_ref[...] = jnp.zeros_like(loss_ref)
            for o_ref in d_par:
                o_ref[...] = jnp.zeros_like(o_ref)

        loss_ref[...] += res
        for j, o_ref in enumerate(d_par):
            o_ref[...] += g[nr + j].astype(F32)

    return _call(body, name=name, grid=(n // blk,),
                 in_specs=[_row_spec(blk, r.shape[1]) for r in rows] + [_whole_spec(p.shape) for p in params],
                 out_specs=([_whole_spec((1, 1))] + [_row_spec(blk, rows[j].shape[1]) for j, _ in row_grads]
                            + [_whole_spec(p.shape) for p in params]),
                 out_shape=([jax.ShapeDtypeStruct((1, 1), F32)]
                            + [jax.ShapeDtypeStruct((n, rows[j].shape[1]), dt) for j, dt in row_grads]
                            + [jax.ShapeDtypeStruct(p.shape, F32) for p in params]),
                 compiler_params=_params(1))(*rows, *params)


def _rms(x, g):
    return x * lax.rsqrt(jnp.mean(x * x, axis=-1, keepdims=True) + EPS) * g


def _modulate(x, g, shift, scale):
    return _rms(x, g) * (1.0 + scale) + shift


def _fn_mod(cs, w_ada):
    return (mmb(jax.nn.silu(cs), w_ada),)


def _fn_a(x, g_mix, sh, sc):
    return (_b16(_modulate(x, g_mix, sh, sc)),)


def _fn_a_res(x, g_mix, sh, sc):
    return _b16(_modulate(x, g_mix, sh, sc)), x


def _sgu_spatial(v, sgu_w, sgu_bt):
    rows, width = v.shape
    gdim = width // (sgu_w.shape[0] // CHUNK)
    groups = width // gdim
    expand = (lax.broadcasted_iota(jnp.int32, (groups, width), 1) // gdim
              == lax.broadcasted_iota(jnp.int32, (groups, width), 0)).astype(F32)
    bias = mmf(sgu_bt, expand)
    lane = lax.broadcasted_iota(jnp.int32, (CHUNK, LANE), 1)
    per_lane_block = LANE // gdim
    chunks = []
    for ci in range(rows // CHUNK):
        vc = v[ci * CHUNK:(ci + 1) * CHUNK]
        blocks = []
        for lb in range(width // LANE):
            vb = vc[:, lb * LANE:(lb + 1) * LANE]
            acc = None
            for s in range(per_lane_block):
                g = lb * per_lane_block + s
                r = mmb(sgu_w[g * CHUNK:(g + 1) * CHUNK], vb)
                sel = (lane // gdim) == s
                acc = jnp.where(sel, r, 0.0) if acc is None else jnp.where(sel, r, acc)
            blocks.append(acc)
        chunks.append(jnp.concatenate(blocks, axis=1) + bias)
    return jnp.concatenate(chunks, axis=0)


def _fn_b(pu, prest, ysc, s5_d, w_glu, b_glu, ln_g, ln_b, sgu_w, sgu_bt, w_pa, w_pb, b_gate):
    sw = ln_g.shape[1]
    y = jax.nn.gelu(pu * s5_d + ysc)
    ya = y * jax.nn.sigmoid(mmb(y, w_glu) + b_glu)
    z = jax.nn.gelu(prest[:, :2 * sw])
    u, v = z[:, :sw], z[:, sw:]
    vc = v - jnp.mean(v, axis=-1, keepdims=True)
    v = vc * lax.rsqrt(jnp.mean(vc * vc, axis=-1, keepdims=True) + EPS) * ln_g + ln_b
    yb = u * _sgu_spatial(v, sgu_w, sgu_bt)
    gates = jax.nn.sigmoid(prest[:, 2 * sw:] + b_gate)
    d = gates.shape[1] // 2
    return (_b16(gates[:, :d] * mmb(ya, w_pa) + gates[:, d:] * mmb(yb, w_pb)),)


def _fn_c(x, mo, ga1, g_ffn, sh2, sc2):
    x1 = x + ga1 * mo
    return x1, _b16(_modulate(x1, g_ffn, sh2, sc2))


def _fn_e(x1, dn, tgt, ga2, g_final):
    y = _rms(x1 + ga2 * dn, g_final)
    err = (y - tgt) ** 2
    return 0.5 * jnp.sum(jnp.mean(err, axis=-1, keepdims=True), axis=0, keepdims=True)


def _s5_direction(u, a_re, a_im, log_step, bt_re, bt_im, c_re, c_im, rev):
    nc, width = u.shape
    t_len = width // S5_H
    dt = jnp.exp(log_step)
    lr, li = a_re * dt, a_im * dt
    mag = jnp.exp(lr)
    ab_re, ab_im = mag * jnp.cos(li), mag * jnp.sin(li)
    p, q = ab_re - 1.0, ab_im
    den = a_re * a_re + a_im * a_im
    k_re, k_im = (p * a_re + q * a_im) / den, (q * a_re - p * a_im) / den
    bb_re = k_re * bt_re - k_im * bt_im
    bb_im = k_re * bt_im + k_im * bt_re
    row = lax.broadcasted_iota(jnp.int32, (width, S5_H), 0)
    tile = ((row % S5_H) == lax.broadcasted_iota(jnp.int32, (width, S5_H), 1)).astype(F32)
    bbr, bbi = mmf(tile, bb_re), mmf(tile, bb_im)
    ccr, cci = mmf(tile, c_re), mmf(tile, c_im)
    pos_i = lax.broadcasted_iota(jnp.int32, (t_len, 1), 0)
    if rev:
        pos_i = (t_len - 1) - pos_i
    pos = pos_i.astype(F32)
    spread = ((lax.broadcasted_iota(jnp.int32, (width, t_len), 0) // S5_H)
              == lax.broadcasted_iota(jnp.int32, (width, t_len), 1)).astype(F32)

    def power(e):
        m = jnp.exp(lr * e)
        return m * jnp.cos(li * e), m * jnp.sin(li * e)

    def power_rows(e):
        xr, xi = power(e)
        return mmf(spread, xr), mmf(spread, xi)

    def cmul(xr, xi, yr, yi):
        return xr * yr - xi * yi, xr * yi + xi * yr

    pr, pi = cmul(*power_rows((t_len - 1.0) - pos), bbr, bbi)
    rr, ri = cmul(*power_rows(pos - (t_len - 1.0)), ccr, cci)
    wr, wi = cmul(*power_rows(pos + 1.0), ccr, cci)
    toep = mmf_nt(pr, rr) - mmf_nt(pi, ri)
    col_pos = lax.broadcasted_iota(jnp.int32, (width, width), 1) // S5_H
    row_pos = lax.broadcasted_iota(jnp.int32, (width, width), 0) // S5_H
    keep = (col_pos <= row_pos) if rev else (col_pos >= row_pos)
    toep = jnp.where(keep, toep, 0.0)

    sr, si = mmb(u, pr), mmb(u, pi)
    k = 1
    while k < nc:
        ar, ai = power(jnp.full((1, 1), float(t_len * k), F32))
        hr, hi = _shift(sr, k, rev), _shift(si, k, rev)
        sr, si = sr + ar * hr - ai * hi, si + ar * hi + ai * hr
        k *= 2
    cr_in, ci_in = _shift(sr, 1, rev), _shift(si, 1, rev)
    return mmb(u, toep) + mmb_nt(cr_in, wr) - mmb_nt(ci_in, wi)


def _fn_s5(x_chunks, ctx_chunks, *prm):
    nx, nctx = x_chunks.shape[0], ctx_chunks.shape[0]
    pad = (-(nx + 2 * nctx)) % LANE
    u = jnp.concatenate([ctx_chunks, x_chunks, ctx_chunks, jnp.zeros((pad, x_chunks.shape[1]), F32)], axis=0)
    out = None
    for d in range(2):
        y = _s5_direction(u, *[p[d] for p in prm], rev=(d == 1))
        out = y if out is None else out + y
    return out[nctx:nctx + nx]


def _s5_specs(prm):
    return [pl.BlockSpec((2, 1) + p.shape[2:], lambda g: (0, g, 0, 0)) for p in prm]


def _group_spec(a):
    return pl.BlockSpec((1,) + a.shape[1:], lambda i: (i, 0, 0))


def _s5_fwd(x_g, ctx_g, prm):
    def body(*refs):
        pv = [r[:, 0] for r in refs[2:2 + len(prm)]]
        refs[-1][0] = _fn_s5(refs[0][0], refs[1][0], *pv)

    return _call(body, name="s5_fwd", grid=(x_g.shape[0],),
                 in_specs=[_group_spec(x_g), _group_spec(ctx_g)] + _s5_specs(prm), out_specs=_group_spec(x_g),
                 out_shape=jax.ShapeDtypeStruct(x_g.shape, F32), compiler_params=_params(1))(x_g, ctx_g, *prm)


def _s5_bwd(x_g, ctx_g, prm, dy_g, skip_g):
    npar = len(prm)

    def body(*refs):
        pv = [r[:, 0] for r in refs[2:2 + npar]]
        dy = refs[2 + npar][0]
        _, vjp = jax.vjp(_fn_s5, refs[0][0], refs[1][0], *pv)
        grads = vjp(dy)
        outs = refs[4 + npar:]
        outs[0][0] = (grads[0] + dy * refs[3 + npar][0]).astype(outs[0].dtype)
        outs[1][0] = grads[1]
        for o_ref, gv in zip(outs[2:], grads[2:]):
            o_ref[:, 0] = gv

    return _call(body, name="s5_bwd", grid=(x_g.shape[0],),
                 in_specs=([_group_spec(x_g), _group_spec(ctx_g)] + _s5_specs(prm)
                           + [_group_spec(dy_g), _group_spec(skip_g)]),
                 out_specs=[_group_spec(x_g), _group_spec(ctx_g)] + _s5_specs(prm),
                 out_shape=[jax.ShapeDtypeStruct(x_g.shape, F32), jax.ShapeDtypeStruct(ctx_g.shape, F32)]
                 + [jax.ShapeDtypeStruct(p.shape, F32) for p in prm],
                 compiler_params=_params(1))(x_g, ctx_g, *prm, dy_g, skip_g)


def _to_groups(tok):
    n, width = tok.shape
    g = width // S5_H
    return jnp.transpose(tok.reshape(n, g, S5_H), (1, 0, 2)).reshape(g, n // S5_T, S5_T * S5_H)


def _from_groups(grp):
    g, nc, _ = grp.shape
    return jnp.transpose(grp.reshape(g, nc * S5_T, S5_H), (1, 0, 2)).reshape(nc * S5_T, g * S5_H)


def _conv_taps(xp, xm, xn, blk_i, n_blk):
    tb = xm.shape[0]
    xp = jnp.where(blk_i == 0, 0.0, xp.astype(F32))
    xn = jnp.where(blk_i == n_blk - 1, 0.0, xn.astype(F32))
    buf = jnp.concatenate([xp, xm.astype(F32), xn], axis=0)
    n = tb + 2 * HALO
    col = lax.broadcasted_iota(jnp.int32, (n, 1), 0) % GRID_W
    left = jnp.where(col >= 1, pltpu.roll(buf, 1, 0), 0.0)
    right = jnp.where(col <= GRID_W - 2, pltpu.roll(buf, n - 1, 0), 0.0)
    shifted = (left, buf, right)
    taps = []
    for di in range(3):
        start = HALO + (di - 1) * GRID_W
        for dj in range(3):
            taps.append(shifted[dj][start:start + tb])
    return taps


def _conv_sum(taps, w_ref):
    acc = None
    for k, tap in enumerate(taps):
        term = tap * w_ref[k:k + 1, :]
        acc = term if acc is None else acc + term
    return acc


def _conv_geometry(n_tok, width):
    tb = _pick(n_tok, 1024, HALO)
    cb = _pick(width, 256)
    nb = tb // HALO
    last = n_tok // HALO - 1
    main = pl.BlockSpec((tb, cb), lambda j, i: (i, j))
    prev = pl.BlockSpec((HALO, cb), lambda j, i: (jnp.maximum(i * nb - 1, 0), j))
    nxt = pl.BlockSpec((HALO, cb), lambda j, i: (jnp.minimum(i * nb + nb, last), j))
    par = lambda r: pl.BlockSpec((r, cb), lambda j, i: (0, j))
    return tb, cb, main, prev, nxt, par


def _conv_act_fwd(up_g, up_v, w_g, w_v, b_g, b_v):
    n_tok, width = up_g.shape
    tb, cb, main, prev, nxt, par = _conv_geometry(n_tok, width)
    n_blk = n_tok // tb

    def body(gp, gm, gn, vp, vm, vn, wg, wv, bg, bv, o_ref):
        i = pl.program_id(1)
        gate = _conv_sum(_conv_taps(gp[...], gm[...], gn[...], i, n_blk), wg) + bg[...]
        val = _conv_sum(_conv_taps(vp[...], vm[...], vn[...], i, n_blk), wv) + bv[...]
        o_ref[...] = (jax.nn.silu(gate) * val).astype(o_ref.dtype)

    return _call(body, name="conv_act_fwd", grid=(width // cb, n_tok // tb),
                 in_specs=[prev, main, nxt, prev, main, nxt, par(9), par(9), par(1), par(1)], out_specs=main,
                 out_shape=jax.ShapeDtypeStruct((n_tok, width), BF16),
                 compiler_params=_params(2))(up_g, up_g, up_g, up_v, up_v, up_v, w_g, w_v, b_g, b_v)


def _conv_act_bwd(up_g, up_v, w_g, w_v, b_g, b_v, d_act):
    n_tok, width = up_g.shape
    tb, cb, main, prev, nxt, par = _conv_geometry(n_tok, width)
    n_blk = n_tok // tb

    def body(gp, gm, gn, vp, vm, vn, wg, wv, bg, bv, da, dcg, dcv, dwg, dwv, dbg, dbv):
        i = pl.program_id(1)
        taps_g = _conv_taps(gp[...], gm[...], gn[...], i, n_blk)
        taps_v = _conv_taps(vp[...], vm[...], vn[...], i, n_blk)
        gate = _conv_sum(taps_g, wg) + bg[...]
        val = _conv_sum(taps_v, wv) + bv[...]
        sig = jax.nn.sigmoid(gate)
        d = da[...].astype(F32)
        d_gate = d * val * sig * (1.0 + gate * (1.0 - sig))
        d_val = d * gate * sig
        dcg[...] = d_gate
        dcv[...] = d_val

        @pl.when(i == 0)
        def _():
            for r in (dwg, dwv, dbg, dbv):
                r[...] = jnp.zeros_like(r)

        dbg[...] += jnp.sum(d_gate, axis=0, keepdims=True)
        dbv[...] += jnp.sum(d_val, axis=0, keepdims=True)
        for k in range(9):
            dwg[k:k + 1, :] += jnp.sum(taps_g[k] * d_gate, axis=0, keepdims=True)
            dwv[k:k + 1, :] += jnp.sum(taps_v[k] * d_val, axis=0, keepdims=True)

    shp = jax.ShapeDtypeStruct
    return _call(body, name="conv_act_bwd", grid=(width // cb, n_tok // tb),
                 in_specs=[prev, main, nxt, prev, main, nxt, par(9), par(9), par(1), par(1), main],
                 out_specs=[main, main, par(9), par(9), par(1), par(1)],
                 out_shape=[shp((n_tok, width), F32), shp((n_tok, width), F32), shp((9, width), F32),
                            shp((9, width), F32), shp((1, width), F32), shp((1, width), F32)],
                 compiler_params=_params(2))(up_g, up_g, up_g, up_v, up_v, up_v, w_g, w_v, b_g, b_v, d_act)


def _conv_plain(x, w, name):
    n_tok, width = x.shape
    tb, cb, main, prev, nxt, par = _conv_geometry(n_tok, width)
    n_blk = n_tok // tb

    def body(xp, xm, xn, w_ref, o_ref):
        taps = _conv_taps(xp[...], xm[...], xn[...], pl.program_id(1), n_blk)
        o_ref[...] = _conv_sum(taps, w_ref).astype(o_ref.dtype)

    return _call(body, name=name, grid=(width // cb, n_tok // tb), in_specs=[prev, main, nxt, par(9)],
                 out_specs=main, out_shape=jax.ShapeDtypeStruct((n_tok, width), BF16),
                 compiler_params=_params(2))(x, x, x, w)


def _adamw(w, g_parts, m, v, name):
    rows, cols = w.shape
    parts = g_parts.shape[0]
    blk = _pick(rows, 256, 8)
    spec = pl.BlockSpec((blk, cols), lambda i: (i, 0))

    def body(w_ref, g_ref, m_ref, v_ref, g_out, d_out, m_out, v_out):
        g = g_ref[0].astype(F32)
        for p in range(1, parts):
            g = g + g_ref[p].astype(F32)
        m_new = ADAM_B1 * m_ref[...] + (1.0 - ADAM_B1) * g
        v_new = ADAM_B2 * v_ref[...] + (1.0 - ADAM_B2) * (g * g)
        m_hat = m_new / (1.0 - ADAM_B1 ** ADAM_STEP)
        v_hat = v_new / (1.0 - ADAM_B2 ** ADAM_STEP)
        g_out[...] = g
        d_out[...] = -ADAM_LR * (m_hat / (jnp.sqrt(v_hat) + ADAM_EPS) + ADAM_WD * w_ref[...])
        m_out[...] = m_new
        v_out[...] = v_new

    return _call(body, name=name, grid=(rows // blk,),
                 in_specs=[spec, pl.BlockSpec((parts, blk, cols), lambda i: (0, i, 0)), spec, spec],
                 out_specs=[spec] * 4, out_shape=[jax.ShapeDtypeStruct((rows, cols), F32)] * 4,
                 compiler_params=_params(1))(w, g_parts, m, v)


def _exchange(items, name):
    n = len(items)
    hbm = pl.BlockSpec(memory_space=pl.ANY)

    def body(*refs):
        srcs, outs = refs[:n], refs[n:2 * n]
        send_sems, recv_sems, own_sems = refs[2 * n:]
        x, y, c = lax.axis_index("x"), lax.axis_index("y"), lax.axis_index("c")
        me = 4 * x + 2 * y + c
        own = []
        for i, (_, mode) in enumerate(items):
            src = srcs[i] if mode == "gather" else srcs[i].at[me]
            cp = pltpu.make_async_copy(src, outs[i].at[me], own_sems.at[i])
            cp.start()
            own.append(cp)
        sent = []
        for i, (_, mode) in enumerate(items):
            for k in range(1, N_DEV):
                px = 1 - x if k & 4 else x
                py = 1 - y if k & 2 else y
                pc = 1 - c if k & 1 else c
                peer = 4 * px + 2 * py + pc
                src = srcs[i] if mode == "gather" else srcs[i].at[peer]
                cp = pltpu.make_async_remote_copy(
                    src_ref=src, dst_ref=outs[i].at[me], send_sem=send_sems.at[i, k - 1],
                    recv_sem=recv_sems.at[i, k - 1], device_id=(px, py, pc), device_id_type=pl.DeviceIdType.MESH)
                cp.start()
                landing = pltpu.make_async_remote_copy(
                    src_ref=src, dst_ref=outs[i].at[peer], send_sem=send_sems.at[i, k - 1],
                    recv_sem=recv_sems.at[i, k - 1], device_id=(px, py, pc), device_id_type=pl.DeviceIdType.MESH)
                sent.append((cp, landing))
        for cp in own:
            cp.wait()
        for cp, landing in sent:
            cp.wait_send()
            landing.wait_recv()

    out_shape = [jax.ShapeDtypeStruct((N_DEV,) + (a.shape if mode == "gather" else a.shape[1:]), a.dtype)
                 for a, mode in items]
    return _call(body, name=name, in_specs=[hbm] * n, out_specs=[hbm] * n, out_shape=out_shape,
                 scratch_shapes=[pltpu.SemaphoreType.DMA((n, N_DEV - 1)), pltpu.SemaphoreType.DMA((n, N_DEV - 1)),
                                 pltpu.SemaphoreType.DMA((n,))])(*[a for a, _ in items])


def _cols_from_blocks(g):
    return jnp.transpose(g, (1, 0, 2)).reshape(g.shape[1], N_DEV * g.shape[2])


def _blocks_from_cols(w):
    r, c8 = w.shape
    return jnp.transpose(w.reshape(r, N_DEV, c8 // N_DEV), (1, 0, 2))


def _pack(arrs):
    flat = jnp.concatenate([a.reshape(-1).astype(F32) for a in arrs])
    pad = (-flat.shape[0]) % (8 * LANE)
    return jnp.pad(flat, (0, pad)).reshape(-1, LANE)


def _unpack(packed, shapes):
    flat = packed.reshape(-1)
    out, off = [], 0
    for s in shapes:
        size = math.prod(s)
        out.append(flat[off:off + size].reshape(s))
        off += size
    return out


def kernel(x, c, ctx, c_ctx, w_ada, b_ada, g_mix, w_in, s5_a_re, s5_a_im, s5_log_step, s5_b_re, s5_b_im, s5_c_re, s5_c_im, s5_d, s5_w_glu, s5_b_glu, sgu_ln_g, sgu_ln_b, sgu_w, sgu_b, w_proj_a, w_proj_b, b_gate, w_out, g_ffn, w_up, conv_w, conv_b, w_down, g_final, loss_target, m_c_ctx, m_w_ada, m_b_ada, m_g_mix, m_w_in, m_s5_a_re, m_s5_a_im, m_s5_log_step, m_s5_b_re, m_s5_b_im, m_s5_c_re, m_s5_c_im, m_s5_d, m_s5_w_glu, m_s5_b_glu, m_sgu_ln_g, m_sgu_ln_b, m_sgu_w, m_sgu_b, m_w_proj_a, m_w_proj_b, m_b_gate, m_w_out, m_g_ffn, m_w_up, m_conv_w, m_conv_b, m_w_down, m_g_final, v_c_ctx, v_w_ada, v_b_ada, v_g_mix, v_w_in, v_s5_a_re, v_s5_a_im, v_s5_log_step, v_s5_b_re, v_s5_b_im, v_s5_c_re, v_s5_c_im, v_s5_d, v_s5_w_glu, v_s5_b_glu, v_sgu_ln_g, v_sgu_ln_b, v_sgu_w, v_sgu_b, v_w_proj_a, v_w_proj_b, v_b_gate, v_w_out, v_g_ffn, v_w_up, v_conv_w, v_conv_b, v_w_down, v_g_final):
    given = dict(locals())
    wts = {n: given[n] for n in WEIGHTS}
    mom1 = {n: given["m_" + n] for n in WEIGHTS}
    mom2 = {n: given["v_" + n] for n in WEIGHTS}

    me = 4 * lax.axis_index("x") + 2 * lax.axis_index("y") + lax.axis_index("c")
    xs, cx, tgt = x[0], ctx[0], loss_target[0]
    n_tok, d = xs.shape
    n_ctx = cx.shape[0]
    s5w = s5_d.shape[1]
    ffn = w_down.shape[1] * N_DEV
    n_mod = w_ada.shape[2] * N_DEV // d
    mod_cols = w_ada.shape[2]

    def two_d(a):
        return a.reshape(-1, a.shape[-1])

    conv_w9 = conv_w[0].reshape(9, -1)
    gathered = _exchange(
        [(c, "gather"), (_b16(w_in[0]), "gather"), (_b16(s5_w_glu[0]), "gather"), (_b16(w_proj_a[0]), "gather"),
         (_b16(w_proj_b[0]), "gather"), (_b16(w_out[0]), "gather"), (_b16(w_up[0]), "gather"),
         (conv_w9, "gather"), (_b16(w_down[0]), "gather")], "gather_weights")
    c_all = gathered[0].reshape(N_DEV, d)
    w_in_f = _cols_from_blocks(gathered[1])
    w_in_u, w_in_rest = w_in_f[:, :s5w], w_in_f[:, s5w:]
    w_glu_f = gathered[2].reshape(-1, s5w)
    w_pa_f = _cols_from_blocks(gathered[3])
    w_pb_f = _cols_from_blocks(gathered[4])
    w_out_f = gathered[5].reshape(-1, d)
    w_up_f = _cols_from_blocks(gathered[6])
    w_up_g, w_up_v = w_up_f[:, :ffn], w_up_f[:, ffn:]
    conv_w_f = _cols_from_blocks(gathered[7])
    w_down_f = gathered[8].reshape(-1, d)

    cs_in = jnp.concatenate([c_all, jnp.broadcast_to(c_ctx[None, :], (N_DEV, d))], axis=0)
    w_ada_loc = w_ada[0]
    (mod_mine,) = _stage_fwd(_fn_mod, [cs_in], [w_ada_loc], [(mod_cols, F32)], blk=2 * N_DEV, name="mod_fwd")
    (mod_blocks,) = _exchange([(mod_mine, "gather")], "gather_mod")
    mod_all = _cols_from_blocks(mod_blocks) + b_ada
    mod = lax.dynamic_slice_in_dim(mod_all, me, 1, axis=0)
    mod_c = mod_all[N_DEV:N_DEV + 1]
    sh1, sc1, ga1, sh2, sc2, ga2 = [mod[:, i * d:(i + 1) * d] for i in range(n_mod)]
    sh1c, sc1c = mod_c[:, :d], mod_c[:, d:2 * d]

    a_par = [g_mix, sh1, sc1]
    ac_par = [g_mix, sh1c, sc1c]
    (h,) = _stage_fwd(_fn_a, [xs], a_par, [(d, BF16)], blk=_pick(n_tok, 512, 8), name="modulate1_fwd")
    (hc,) = _stage_fwd(_fn_a, [cx], ac_par, [(d, BF16)], blk=_pick(n_ctx, 512, 8), name="modulate1_ctx_fwd")
    pu = _mm(h, w_in_u, name="proj_u")
    prest = _mm(h, w_in_rest, name="proj_rest")
    puc = _mm(hc, w_in_u, name="proj_u_ctx")

    s5_prm = [s5_a_re[0][:, :, None, :], s5_a_im[0][:, :, None, :], s5_log_step[0][:, :, None, None],
              jnp.swapaxes(s5_b_re[0], 2, 3), jnp.swapaxes(s5_b_im[0], 2, 3), s5_c_re[0], s5_c_im[0]]
    pu_g, puc_g = _to_groups(pu), _to_groups(puc)
    ysc = _from_groups(_s5_fwd(pu_g, puc_g, s5_prm))

    b_par = [s5_d, w_glu_f, s5_b_glu, sgu_ln_g, sgu_ln_b, two_d(sgu_w[0]), jnp.transpose(sgu_b[0]),
             w_pa_f, w_pb_f, b_gate]
    b_rows = [pu, prest, ysc]
    b_blk = _pick(n_tok, 256, CHUNK)
    (mpre,) = _stage_fwd(_fn_b, b_rows, b_par, [(d, BF16)], blk=b_blk, name="mixers_fwd")
    mo = _mm(mpre, w_out_f, name="out_proj")

    c_par = [ga1, g_ffn, sh2, sc2]
    c_blk = _pick(n_tok, 512, 8)
    x1, h2 = _stage_fwd(_fn_c, [xs, mo], c_par, [(d, F32), (d, BF16)], blk=c_blk, name="modulate2_fwd")
    up_g = _mm(h2, w_up_g, out_dtype=BF16, name="up_gate")
    up_v = _mm(h2, w_up_v, out_dtype=BF16, name="up_val")
    cw_g, cw_v = conv_w_f[:, :ffn], conv_w_f[:, ffn:]
    cb_g, cb_v = conv_b[:, :ffn], conv_b[:, ffn:]
    act = _conv_act_fwd(up_g, up_v, cw_g, cw_v, cb_g, cb_v)
    dn = _mm(act, w_down_f, name="down_proj")

    loss_part, d_x1a, d_dn, d_ga2, d_g_final = _stage_loss(
        _fn_e, [x1, dn, tgt], [ga2, g_final[None, :]], blk=c_blk, name="loss_head",
        row_grads=[(0, F32), (1, BF16)])

    d_act = _mm(d_dn, w_down_f, tb=True, name="down_proj_dx")
    g_w_down = _mm(act, d_dn, ta=True, name="down_proj_dw")
    dcg, dcv, g_cw_g, g_cw_v, g_cb_g, g_cb_v = _conv_act_bwd(up_g, up_v, cw_g, cw_v, cb_g, cb_v, d_act)
    dug = _conv_plain(dcg, cw_g[::-1], "conv_dx_gate")
    duv = _conv_plain(dcv, cw_v[::-1], "conv_dx_val")
    d_h2 = _mm(dug, w_up_g, tb=True, name="up_gate_dx")
    d_h2 = _mm(duv, w_up_v, tb=True, add=d_h2, out_dtype=BF16, name="up_val_dx")
    g_w_up = jnp.concatenate([_mm(h2, dug, ta=True, name="up_gate_dw"), _mm(h2, duv, ta=True, name="up_val_dw")],
                             axis=1)
    (d_xc, d_mo), (d_ga1, g_g_ffn, d_sh2, d_sc2) = _split(_stage_bwd(
        _fn_c, [xs, mo], c_par, [d_x1a, d_h2], blk=c_blk, name="modulate2_bwd",
        row_grads=[(0, F32), (1, BF16)]), 2)

    d_mpre = _mm(d_mo, w_out_f, tb=True, out_dtype=BF16, name="out_proj_dx")
    g_w_out = _mm(mpre, d_mo, ta=True, name="out_proj_dw")
    (d_prest, d_ysc), b_grads = _split(_stage_bwd(
        _fn_b, b_rows, b_par, [d_mpre], blk=b_blk, name="mixers_bwd", row_grads=[(1, BF16), (2, F32)]), 2)
    (g_s5_d, g_w_glu, g_b_glu, g_ln_g, g_ln_b, g_sgu_w, g_sgu_bt, g_w_pa, g_w_pb, g_b_gate) = b_grads

    skip_g = jnp.tile(s5_d.reshape(-1, 1, S5_H), (1, 1, S5_T))
    s5_out = _s5_bwd(pu_g, puc_g, s5_prm, _to_groups(d_ysc), skip_g)
    d_pu, d_puc = _b16(_from_groups(s5_out[0])), _from_groups(s5_out[1])
    g_a_re, g_a_im, g_ls, g_bt_re, g_bt_im, g_c_re, g_c_im = s5_out[2:]

    d_h = _mm(d_pu, w_in_u, tb=True, name="proj_u_dx")
    d_h = _mm(d_prest, w_in_rest, tb=True, add=d_h, out_dtype=BF16, name="proj_rest_dx")
    d_hc = _mm(d_puc, w_in_u, tb=True, out_dtype=BF16, name="proj_u_ctx_dx")
    g_w_in_u = _mm(hc, d_puc, ta=True, name="proj_u_ctx_dw")
    g_w_in_u = _mm(h, d_pu, ta=True, add=g_w_in_u, name="proj_u_dw")
    g_w_in = jnp.concatenate([g_w_in_u, _mm(h, d_prest, ta=True, name="proj_rest_dw")], axis=1)

    (grad_x,), (g_g_mix_x, d_sh1, d_sc1) = _split(_stage_bwd(
        _fn_a_res, [xs], a_par, [d_h, d_xc], blk=c_blk, name="modulate1_bwd", row_grads=[(0, F32)]), 1)
    _, (g_g_mix_c, d_sh1c, d_sc1c) = _split(_stage_bwd(
        _fn_a, [cx], ac_par, [d_hc], blk=_pick(n_ctx, 512, 8), name="modulate1_ctx_bwd", row_grads=[]), 0)

    zeros = jnp.zeros((1, (n_mod - 2) * d), F32)
    d_mod = jnp.concatenate([d_sh1, d_sc1, d_ga1, d_sh2, d_sc2, d_ga2], axis=1)
    d_mod_c = jnp.concatenate([d_sh1c, d_sc1c, zeros], axis=1)
    (d_mod_all,) = _exchange([(jnp.concatenate([d_mod, d_mod_c], axis=0), "gather")], "gather_dmod")
    d_mod_rows = jnp.transpose(d_mod_all, (1, 0, 2)).reshape(2 * N_DEV, n_mod * d)
    d_mod_mine = lax.dynamic_slice_in_dim(d_mod_rows, me * mod_cols, mod_cols, axis=1)
    (d_cs,), (g_w_ada,) = _split(_stage_bwd(
        _fn_mod, [cs_in], [w_ada_loc], [d_mod_mine], blk=2 * N_DEV, name="mod_bwd", row_grads=[(0, F32)]), 1)

    part = {
        'c_ctx': jnp.sum(d_cs[N_DEV:], axis=0), 'b_ada': d_mod + d_mod_c, 'g_mix': g_g_mix_x + g_g_mix_c,
        's5_a_re': g_a_re, 's5_a_im': g_a_im, 's5_log_step': g_ls,
        's5_b_re': jnp.swapaxes(g_bt_re, 2, 3), 's5_b_im': jnp.swapaxes(g_bt_im, 2, 3),
        's5_c_re': g_c_re, 's5_c_im': g_c_im, 's5_d': g_s5_d, 's5_b_glu': g_b_glu, 'sgu_ln_g': g_ln_g,
        'sgu_ln_b': g_ln_b, 'sgu_w': g_sgu_w, 'sgu_b': jnp.transpose(g_sgu_bt), 'b_gate': g_b_gate,
        'g_ffn': g_g_ffn, 'conv_b': jnp.concatenate([g_cb_g, g_cb_v], axis=1), 'g_final': d_g_final,
    }
    rep_shapes = [wts[n].shape for n in REPLICATED]
    g_full = {
        'w_in': g_w_in, 'w_proj_a': g_w_pa, 'w_proj_b': g_w_pb, 'w_up': g_w_up,
        'conv_w': jnp.concatenate([g_cw_g, g_cw_v], axis=1), 's5_w_glu': g_w_glu, 'w_out': g_w_out,
        'w_down': g_w_down,
    }
    big = [n for n in SHARDED if n != 'w_ada']
    items = [(_pack([part[n] for n in REPLICATED]), "gather")]
    for n in big:
        g = g_full[n]
        blocks = _blocks_from_cols(g) if n in COL_SHARDED else g.reshape(N_DEV, g.shape[0] // N_DEV, g.shape[1])
        items.append((_b16(blocks), "a2a"))
    reduced = _exchange(items, "exchange_grads")

    out = {}
    rep = _adamw(_pack([wts[n] for n in REPLICATED]), reduced[0], _pack([mom1[n] for n in REPLICATED]),
                 _pack([mom2[n] for n in REPLICATED]), "adamw_replicated")
    rep = [_unpack(r, rep_shapes) for r in rep]
    for i, n in enumerate(REPLICATED):
        out[n] = tuple(r[i] for r in rep)
    for n, parts in zip(big, reduced[1:]):
        shape = wts[n].shape
        res = _adamw(two_d(wts[n]), parts, two_d(mom1[n]), two_d(mom2[n]), "adamw_" + n)
        out[n] = tuple(r.reshape(shape) for r in res)
    res = _adamw(w_ada_loc, g_w_ada[None], m_w_ada[0], v_w_ada[0], "adamw_w_ada")
    out['w_ada'] = tuple(r.reshape(w_ada.shape) for r in res)

    loss = lax.psum(loss_part[0, 0], ("x", "y", "c"))
    return (loss, grad_x[None], *[out[n][0] for n in WEIGHTS], *[out[n][1] for n in WEIGHTS],
            *[out[n][2] for n in WEIGHTS], *[out[n][3] for n in WEIGHTS])


def _split(res, n_rows):
    return tuple(res[:n_rows]), tuple(res[n_rows:])
```

```python
import functools
import math

import jax
import jax.numpy as jnp
from jax import lax
from jax.experimental import pallas as pl
from jax.experimental.pallas import tpu as pltpu

F32 = jnp.float32
BF16 = jnp.bfloat16
HI = lax.Precision.HIGHEST

N_DEV = 8
GRID_W = 64
CHUNK = 128
EPS = 1e-6
S5_T = 32
S5_H = 16
LANE = 128
HALO = 128
VMEM_LIMIT = 56 * 1024 * 1024

ADAM_LR = 0.001
ADAM_B1 = 0.9
ADAM_B2 = 0.999
ADAM_EPS = 1e-08
ADAM_WD = 0.01
ADAM_STEP = 10

WEIGHTS = ['c_ctx', 'w_ada', 'b_ada', 'g_mix', 'w_in', 's5_a_re', 's5_a_im', 's5_log_step', 's5_b_re', 's5_b_im',
           's5_c_re', 's5_c_im', 's5_d', 's5_w_glu', 's5_b_glu', 'sgu_ln_g', 'sgu_ln_b', 'sgu_w', 'sgu_b',
           'w_proj_a', 'w_proj_b', 'b_gate', 'w_out', 'g_ffn', 'w_up', 'conv_w', 'conv_b', 'w_down', 'g_final']
COL_SHARDED = ('w_ada', 'w_in', 'w_proj_a', 'w_proj_b', 'w_up', 'conv_w')
ROW_SHARDED = ('s5_w_glu', 'w_out', 'w_down')
SHARDED = COL_SHARDED + ROW_SHARDED
REPLICATED = [n for n in WEIGHTS if n not in SHARDED]


def _call(body, **kw):
    return pl.pallas_call(body, **kw)


def _params(n_grid):
    return pltpu.CompilerParams(dimension_semantics=("arbitrary",) * n_grid, vmem_limit_bytes=VMEM_LIMIT)


def _pick(dim, pref, unit=LANE):
    best = None
    d = unit
    while d <= min(dim, pref):
        if dim % d == 0:
            best = d
        d += unit
    return best if best is not None else dim


def _dg(a, b, ca, cb, prec=None):
    return lax.dot_general(a, b, (((ca,), (cb,)), ((), ())), precision=prec, preferred_element_type=F32)


def _b16(v):
    return v.astype(BF16)


@jax.custom_vjp
def mmb(a, b):
    return _dg(_b16(a), _b16(b), 1, 0)


def _mmb_fwd(a, b):
    return mmb(a, b), (a, b)


def _mmb_bwd(res, g):
    a, b = res
    g = _b16(g)
    return _dg(g, _b16(b), 1, 1).astype(a.dtype), _dg(_b16(a), g, 0, 0).astype(b.dtype)


mmb.defvjp(_mmb_fwd, _mmb_bwd)


@jax.custom_vjp
def mmb_nt(a, b):
    return _dg(_b16(a), _b16(b), 1, 1)


def _mmb_nt_fwd(a, b):
    return mmb_nt(a, b), (a, b)


def _mmb_nt_bwd(res, g):
    a, b = res
    g = _b16(g)
    return _dg(g, _b16(b), 1, 0).astype(a.dtype), _dg(g, _b16(a), 0, 0).astype(b.dtype)


mmb_nt.defvjp(_mmb_nt_fwd, _mmb_nt_bwd)


@jax.custom_vjp
def mmf(a, b):
    return _dg(a, b, 1, 0, HI)


def _mmf_fwd(a, b):
    return mmf(a, b), (a, b)


def _mmf_bwd(res, g):
    a, b = res
    return _dg(g, b, 1, 1, HI), _dg(a, g, 0, 0, HI)


mmf.defvjp(_mmf_fwd, _mmf_bwd)


def _dg3(a, b, ca, cb):
    ah, bh = _b16(a), _b16(b)
    al, bl = _b16(a - ah.astype(F32)), _b16(b - bh.astype(F32))
    return _dg(ah, bh, ca, cb) + _dg(ah, bl, ca, cb) + _dg(al, bh, ca, cb)


@jax.custom_vjp
def mm3_nt(a, b):
    return _dg3(a, b, 1, 1)


def _mm3_nt_fwd(a, b):
    return mm3_nt(a, b), (a, b)


def _mm3_nt_bwd(res, g):
    a, b = res
    return _dg3(g, b, 1, 0), _dg3(g, a, 0, 0)


mm3_nt.defvjp(_mm3_nt_fwd, _mm3_nt_bwd)


def _shift_impl(x, k, up):
    n = x.shape[0]
    idx = lax.broadcasted_iota(jnp.int32, (n, 1), 0)
    if up:
        return jnp.where(idx < n - k, pltpu.roll(x, n - k, 0), 0.0)
    return jnp.where(idx >= k, pltpu.roll(x, k, 0), 0.0)


@functools.partial(jax.custom_vjp, nondiff_argnums=(1, 2))
def _shift(x, k, up):
    return _shift_impl(x, k, up)


def _shift_fwd(x, k, up):
    return _shift_impl(x, k, up), None


def _shift_bwd(k, up, _, g):
    return (_shift_impl(g, k, not up),)


_shift.defvjp(_shift_fwd, _shift_bwd)


def _mm(a, b, *, name, ta=False, tb=False, add=None, out_dtype=F32, tm_pref=1408, tn_pref=1408, tk_pref=1408):
    m, k = (a.shape[1], a.shape[0]) if ta else a.shape
    n = b.shape[0] if tb else b.shape[1]
    tm, tn, tk = _pick(m, tm_pref), _pick(n, tn_pref), _pick(k, tk_pref)
    nk = k // tk
    a_spec = (pl.BlockSpec((tk, tm), lambda i, j, kk: (kk, i)) if ta
              else pl.BlockSpec((tm, tk), lambda i, j, kk: (i, kk)))
    b_spec = (pl.BlockSpec((tn, tk), lambda i, j, kk: (j, kk)) if tb
              else pl.BlockSpec((tk, tn), lambda i, j, kk: (kk, j)))
    o_spec = pl.BlockSpec((tm, tn), lambda i, j, kk: (i, j))
    ca, cb = (0 if ta else 1), (1 if tb else 0)
    has_add = add is not None
    in_place = out_dtype == F32 or nk == 1

    def body(*refs):
        a_ref, b_ref = refs[0], refs[1]
        o_ref = refs[3] if has_add else refs[2]
        prod = _dg(_b16(a_ref[...]), _b16(b_ref[...]), ca, cb)
        if nk == 1:
            if has_add:
                prod = prod + refs[2][...].astype(F32)
            o_ref[...] = prod.astype(o_ref.dtype)
            return
        acc_ref = o_ref if in_place else refs[-1]
        kk = pl.program_id(2)

        @pl.when(kk == 0)
        def _():
            acc_ref[...] = prod

        @pl.when(kk > 0)
        def _():
            acc_ref[...] += prod

        if has_add or not in_place:
            @pl.when(kk == nk - 1)
            def _():
                r = acc_ref[...]
                if has_add:
                    r = r + refs[2][...].astype(F32)
                o_ref[...] = r.astype(o_ref.dtype)

    ins = [a, b] + ([add] if has_add else [])
    in_specs = [a_spec, b_spec] + ([o_spec] if has_add else [])
    return _call(body, name=name, grid=(m // tm, n // tn, nk), in_specs=in_specs, out_specs=o_spec,
                 out_shape=jax.ShapeDtypeStruct((m, n), out_dtype),
                 scratch_shapes=[] if in_place else [pltpu.VMEM((tm, tn), F32)],
                 compiler_params=_params(3))(*ins)


def _row_spec(blk, width):
    return pl.BlockSpec((blk, width), lambda i: (i, 0))


def _whole_spec(shape):
    return pl.BlockSpec(shape, lambda i: (0,) * len(shape))


def _stage_fwd(fn, rows, params, outs, *, blk, name, n_rows=None):
    n = n_rows or rows[0].shape[0]
    nr, npar = len(rows), len(params)

    def body(*refs):
        vals = [r[...] for r in refs[:nr + npar]]
        res = fn(*vals)
        for o_ref, v in zip(refs[nr + npar:], res):
            o_ref[...] = v.astype(o_ref.dtype)

    return _call(body, name=name, grid=(n // blk,),
                 in_specs=[_row_spec(blk, r.shape[1]) for r in rows] + [_whole_spec(p.shape) for p in params],
                 out_specs=[_row_spec(blk, w) for w, _ in outs],
                 out_shape=[jax.ShapeDtypeStruct((n, w), dt) for w, dt in outs],
                 compiler_params=_params(1))(*rows, *params)


def _stage_bwd(fn, rows, params, cts, *, blk, name, row_grads, n_rows=None):
    n = n_rows or rows[0].shape[0]
    nr, npar, nct = len(rows), len(params), len(cts)

    def body(*refs):
        vals = [r[...].astype(F32) for r in refs[:nr + npar]]
        ct = [r[...] for r in refs[nr + npar:nr + npar + nct]]
        d_rows = refs[nr + npar + nct:nr + npar + nct + len(row_grads)]
        d_par = refs[nr + npar + nct + len(row_grads):]
        res, vjp = jax.vjp(fn, *vals)
        g = vjp(tuple(c.astype(r.dtype) for c, r in zip(ct, res)))
        for o_ref, (j, _) in zip(d_rows, row_grads):
            o_ref[...] = g[j].astype(o_ref.dtype)

        @pl.when(pl.program_id(0) == 0)
        def _():
            for o_ref in d_par:
                o_ref[...] = jnp.zeros_like(o_ref)

        for j, o_ref in enumerate(d_par):
            o_ref[...] += g[nr + j].astype(F32)

    return _call(body, name=name, grid=(n // blk,),
                 in_specs=([_row_spec(blk, r.shape[1]) for r in rows] + [_whole_spec(p.shape) for p in params]
                           + [_row_spec(blk, c.shape[1]) for c in cts]),
                 out_specs=([_row_spec(blk, rows[j].shape[1]) for j, _ in row_grads]
                            + [_whole_spec(p.shape) for p in params]),
                 out_shape=([jax.ShapeDtypeStruct((n, rows[j].shape[1]), dt) for j, dt in row_grads]
                            + [jax.ShapeDtypeStruct(p.shape, F32) for p in params]),
                 compiler_params=_params(1))(*rows, *params, *cts)


def _stage_loss(fn, rows, params, *, blk, name, row_grads):
    n = rows[0].shape[0]
    nr, npar = len(rows), len(params)

    def body(*refs):
        vals = [r[...].astype(F32) for r in refs[:nr + npar]]
        loss_ref = refs[nr + npar]
        d_rows = refs[nr + npar + 1:nr + npar + 1 + len(row_grads)]
        d_par = refs[nr + npar + 1 + len(row_grads):]
        res, vjp = jax.vjp(fn, *vals)
        g = vjp(jnp.ones_like(res))
        for o_ref, (j, _) in zip(d_rows, row_grads):
            o_ref[...] = g[j].astype(o_ref.dtype)

        @pl.when(pl.program_id(0) == 0)
        def _():
            loss_ref[...] = jnp.zeros_like(loss_ref)
            for o_ref in d_par:
                o_ref[...] = jnp.zeros_like(o_ref)

        loss_ref[...] += res
        for j, o_ref in enumerate(d_par):
            o_ref[...] += g[nr + j].astype(F32)

    return _call(body, name=name, grid=(n // blk,),
                 in_specs=[_row_spec(blk, r.shape[1]) for r in rows] + [_whole_spec(p.shape) for p in params],
                 out_specs=([_whole_spec((1, 1))] + [_row_spec(blk, rows[j].shape[1]) for j, _ in row_grads]
                            + [_whole_spec(p.shape) for p in params]),
                 out_shape=([jax.ShapeDtypeStruct((1, 1), F32)]
                            + [jax.ShapeDtypeStruct((n, rows[j].shape[1]), dt) for j, dt in row_grads]
                            + [jax.ShapeDtypeStruct(p.shape, F32) for p in params]),
                 compiler_params=_params(1))(*rows, *params)


def _rms(x, g):
    return x * lax.rsqrt(jnp.mean(x * x, axis=-1, keepdims=True) + EPS) * g


def _modulate(x, g, shift, scale):
    return _rms(x, g) * (1.0 + scale) + shift


def _fn_mod(cs, w_ada):
    return (mmb(jax.nn.silu(cs), w_ada),)


def _fn_a(x, g_mix, sh, sc):
    return (_b16(_modulate(x, g_mix, sh, sc)),)


def _fn_a_res(x, g_mix, sh, sc):
    return _b16(_modulate(x, g_mix, sh, sc)), x


def _sgu_spatial(v, sgu_w, sgu_bt):
    rows, width = v.shape
    gdim = width // (sgu_w.shape[0] // CHUNK)
    groups = width // gdim
    expand = (lax.broadcasted_iota(jnp.int32, (groups, width), 1) // gdim
              == lax.broadcasted_iota(jnp.int32, (groups, width), 0)).astype(F32)
    bias = mmf(sgu_bt, expand)
    lane = lax.broadcasted_iota(jnp.int32, (CHUNK, LANE), 1)
    per_lane_block = LANE // gdim
    chunks = []
    for ci in range(rows // CHUNK):
        vc = v[ci * CHUNK:(ci + 1) * CHUNK]
        blocks = []
        for lb in range(width // LANE):
            vb = vc[:, lb * LANE:(lb + 1) * LANE]
            acc = None
            for s in range(per_lane_block):
                g = lb * per_lane_block + s
                r = mmb(sgu_w[g * CHUNK:(g + 1) * CHUNK], vb)
                sel = (lane // gdim) == s
                acc = jnp.where(sel, r, 0.0) if acc is None else jnp.where(sel, r, acc)
            blocks.append(acc)
        chunks.append(jnp.concatenate(blocks, axis=1) + bias)
    return jnp.concatenate(chunks, axis=0)


def _fn_b(pu, prest, ysc, s5_d, w_glu, b_glu, ln_g, ln_b, sgu_w, sgu_bt, w_pa, w_pb, b_gate):
    sw = ln_g.shape[1]
    y = jax.nn.gelu(pu * s5_d + ysc)
    ya = y * jax.nn.sigmoid(mmb(y, w_glu) + b_glu)
    z = jax.nn.gelu(prest[:, :2 * sw])
    u, v = z[:, :sw], z[:, sw:]
    vc = v - jnp.mean(v, axis=-1, keepdims=True)
    v = vc * lax.rsqrt(jnp.mean(vc * vc, axis=-1, keepdims=True) + EPS) * ln_g + ln_b
    yb = u * _sgu_spatial(v, sgu_w, sgu_bt)
    gates = jax.nn.sigmoid(prest[:, 2 * sw:] + b_gate)
    d = gates.shape[1] // 2
    return (_b16(gates[:, :d] * mmb(ya, w_pa) + gates[:, d:] * mmb(yb, w_pb)),)


def _fn_c(x, mo, ga1, g_ffn, sh2, sc2):
    x1 = x + ga1 * mo
    return x1, _b16(_modulate(x1, g_ffn, sh2, sc2))


def _fn_e(x1, dn, tgt, ga2, g_final):
    y = _rms(x1 + ga2 * dn, g_final)
    err = (y - tgt) ** 2
    return 0.5 * jnp.sum(jnp.mean(err, axis=-1, keepdims=True), axis=0, keepdims=True)


def _s5_direction(u, mask, a_re, a_im, log_step, bt_re, bt_im, c_re, c_im, rev):
    nc, width = u.shape
    t_len = width // S5_H
    n = a_re.shape[1]
    dt = jnp.exp(log_step)
    lr, li = a_re * dt, a_im * dt
    mag = jnp.exp(lr)
    ab_re, ab_im = mag * jnp.cos(li), mag * jnp.sin(li)
    p, q = ab_re - 1.0, ab_im
    den = a_re * a_re + a_im * a_im
    k_re, k_im = (p * a_re + q * a_im) / den, (q * a_re - p * a_im) / den
    bb_re = k_re * bt_re - k_im * bt_im
    bb_im = k_re * bt_im + k_im * bt_re

    def power(e):
        m = jnp.exp(lr * e)
        return m * jnp.cos(li * e), m * jnp.sin(li * e)

    def cmul(xr, xi, yr, yi):
        return xr * yr - xi * yi, xr * yi + xi * yr

    order = range(t_len - 1, -1, -1) if rev else range(t_len)
    e1 = jnp.concatenate([jnp.full((1, 1, n), float(t_len - 1 - pos), F32) for pos in order], axis=0)
    lr3, li3 = lr.reshape(1, 1, n), li.reshape(1, 1, n)
    m1, c1, s1 = jnp.exp(lr3 * e1), jnp.cos(li3 * e1), jnp.sin(li3 * e1)
    m2 = jnp.exp(-(lr3 * e1))
    x1 = (m1 * c1, m1 * s1)
    x2 = (m2 * c1, -(m2 * s1))
    x3 = cmul(*[v.reshape(1, 1, n) for v in power(float(t_len))], *x2)

    def rows(xr, xi, yr, yi):
        zr, zi = cmul(xr, xi, yr.reshape(1, S5_H, n), yi.reshape(1, S5_H, n))
        return zr.reshape(width, n), zi.reshape(width, n)

    pr, pi = rows(*x1, bb_re, bb_im)
    rr, ri = rows(*x2, c_re, c_im)
    wr, wi = rows(*x3, c_re, c_im)
    toep = (mm3_nt(pr, rr) - mm3_nt(pi, ri)) * mask

    sr, si = mmb(u, pr), mmb(u, pi)
    k = 1
    while k < nc:
        ar, ai = power(float(t_len * k))
        hr, hi = _shift(sr, k, rev), _shift(si, k, rev)
        sr, si = sr + ar * hr - ai * hi, si + ar * hi + ai * hr
        k *= 2
    cr_in, ci_in = _shift(sr, 1, rev), _shift(si, 1, rev)
    return mmb(u, toep) + mmb_nt(cr_in, wr) - mmb_nt(ci_in, wi)


def _fn_s5(masks, x_chunks, ctx_chunks, *prm):
    nx, nctx = x_chunks.shape[0], ctx_chunks.shape[0]
    pad = (-(nx + 2 * nctx)) % LANE
    u = jnp.concatenate([ctx_chunks, x_chunks, ctx_chunks, jnp.zeros((pad, x_chunks.shape[1]), F32)], axis=0)
    out = None
    for d in range(2):
        y = _s5_direction(u, masks[d], *[p[d] for p in prm], rev=(d == 1))
        out = y if out is None else out + y
    return out[nctx:nctx + nx]


def _s5_specs(prm):
    return [pl.BlockSpec((2, 1) + p.shape[2:], lambda g: (0, g, 0, 0)) for p in prm]


def _group_spec(a):
    return pl.BlockSpec((1,) + a.shape[1:], lambda i: (i, 0, 0))


def _s5_masks(width):
    pos = jnp.arange(width) // S5_H
    causal = (pos[None, :] >= pos[:, None]).astype(F32)
    return jnp.stack([causal, causal.T])


def _s5_fwd(x_g, ctx_g, prm):
    masks = _s5_masks(x_g.shape[2])

    def body(*refs):
        pv = [r[:, 0] for r in refs[3:3 + len(prm)]]
        refs[-1][0] = _fn_s5(refs[0][...], refs[1][0], refs[2][0], *pv)

    return _call(body, name="s5_fwd", grid=(x_g.shape[0],),
                 in_specs=[_whole_spec(masks.shape), _group_spec(x_g), _group_spec(ctx_g)] + _s5_specs(prm),
                 out_specs=_group_spec(x_g), out_shape=jax.ShapeDtypeStruct(x_g.shape, F32),
                 compiler_params=_params(1))(masks, x_g, ctx_g, *prm)


def _s5_bwd(x_g, ctx_g, prm, dy_g, skip_g):
    npar = len(prm)
    masks = _s5_masks(x_g.shape[2])

    def body(*refs):
        pv = [r[:, 0] for r in refs[3:3 + npar]]
        dy = refs[3 + npar][0]
        _, vjp = jax.vjp(functools.partial(_fn_s5, refs[0][...]), refs[1][0], refs[2][0], *pv)
        grads = vjp(dy)
        outs = refs[5 + npar:]
        outs[0][0] = (grads[0] + dy * refs[4 + npar][0]).astype(outs[0].dtype)
        outs[1][0] = grads[1]
        for o_ref, gv in zip(outs[2:], grads[2:]):
            o_ref[:, 0] = gv

    return _call(body, name="s5_bwd", grid=(x_g.shape[0],),
                 in_specs=([_whole_spec(masks.shape), _group_spec(x_g), _group_spec(ctx_g)] + _s5_specs(prm)
                           + [_group_spec(dy_g), _group_spec(skip_g)]),
                 out_specs=[_group_spec(x_g), _group_spec(ctx_g)] + _s5_specs(prm),
                 out_shape=[jax.ShapeDtypeStruct(x_g.shape, F32), jax.ShapeDtypeStruct(ctx_g.shape, F32)]
                 + [jax.ShapeDtypeStruct(p.shape, F32) for p in prm],
                 compiler_params=_params(1))(masks, x_g, ctx_g, *prm, dy_g, skip_g)


def _to_groups(tok):
    n, width = tok.shape
    g = width // S5_H
    return jnp.transpose(tok.reshape(n, g, S5_H), (1, 0, 2)).reshape(g, n // S5_T, S5_T * S5_H)


def _from_groups(grp):
    g, nc, _ = grp.shape
    return jnp.transpose(grp.reshape(g, nc * S5_T, S5_H), (1, 0, 2)).reshape(nc * S5_T, g * S5_H)


def _conv_taps(xp, xm, xn, blk_i, n_blk):
    tb = xm.shape[0]
    xp = jnp.where(blk_i == 0, 0.0, xp.astype(F32))
    xn = jnp.where(blk_i == n_blk - 1, 0.0, xn.astype(F32))
    buf = jnp.concatenate([xp, xm.astype(F32), xn], axis=0)
    n = tb + 2 * HALO
    col = lax.broadcasted_iota(jnp.int32, (n, 1), 0) % GRID_W
    left = jnp.where(col >= 1, pltpu.roll(buf, 1, 0), 0.0)
    right = jnp.where(col <= GRID_W - 2, pltpu.roll(buf, n - 1, 0), 0.0)
    shifted = (left, buf, right)
    taps = []
    for di in range(3):
        start = HALO + (di - 1) * GRID_W
        for dj in range(3):
            taps.append(shifted[dj][start:start + tb])
    return taps


def _conv_sum(taps, w_ref):
    acc = None
    for k, tap in enumerate(taps):
        term = tap * w_ref[k:k + 1, :]
        acc = term if acc is None else acc + term
    return acc


def _conv_geometry(n_tok, width):
    tb = _pick(n_tok, 1024, HALO)
    cb = _pick(width, 256)
    nb = tb // HALO
    last = n_tok // HALO - 1
    main = pl.BlockSpec((tb, cb), lambda j, i: (i, j))
    prev = pl.BlockSpec((HALO, cb), lambda j, i: (jnp.maximum(i * nb - 1, 0), j))
    nxt = pl.BlockSpec((HALO, cb), lambda j, i: (jnp.minimum(i * nb + nb, last), j))
    par = lambda r: pl.BlockSpec((r, cb), lambda j, i: (0, j))
    return tb, cb, main, prev, nxt, par


def _conv_act_fwd(up_g, up_v, w_g, w_v, b_g, b_v):
    n_tok, width = up_g.shape
    tb, cb, main, prev, nxt, par = _conv_geometry(n_tok, width)
    n_blk = n_tok // tb

    def body(gp, gm, gn, vp, vm, vn, wg, wv, bg, bv, o_ref, gate_ref, val_ref):
        i = pl.program_id(1)
        gate = _conv_sum(_conv_taps(gp[...], gm[...], gn[...], i, n_blk), wg) + bg[...]
        val = _conv_sum(_conv_taps(vp[...], vm[...], vn[...], i, n_blk), wv) + bv[...]
        o_ref[...] = (jax.nn.silu(gate) * val).astype(o_ref.dtype)
        gate_ref[...] = gate
        val_ref[...] = val

    shp = jax.ShapeDtypeStruct
    return _call(body, name="conv_act_fwd", grid=(width // cb, n_tok // tb),
                 in_specs=[prev, main, nxt, prev, main, nxt, par(9), par(9), par(1), par(1)],
                 out_specs=[main, main, main],
                 out_shape=[shp((n_tok, width), BF16), shp((n_tok, width), F32), shp((n_tok, width), F32)],
                 compiler_params=_params(2))(up_g, up_g, up_g, up_v, up_v, up_v, w_g, w_v, b_g, b_v)


def _conv_act_bwd(up_g, up_v, gate_c, val_c, d_act):
    n_tok, width = up_g.shape
    tb, cb, main, prev, nxt, par = _conv_geometry(n_tok, width)
    n_blk = n_tok // tb

    def body(gp, gm, gn, vp, vm, vn, gc, vc, da, dcg, dcv, dwg, dwv, dbg, dbv):
        i = pl.program_id(1)
        taps_g = _conv_taps(gp[...], gm[...], gn[...], i, n_blk)
        taps_v = _conv_taps(vp[...], vm[...], vn[...], i, n_blk)
        gate, val = gc[...], vc[...]
        sig = jax.nn.sigmoid(gate)
        d = da[...].astype(F32)
        d_gate = d * val * sig * (1.0 + gate * (1.0 - sig))
        d_val = d * gate * sig
        dcg[...] = d_gate.astype(dcg.dtype)
        dcv[...] = d_val.astype(dcv.dtype)

        @pl.when(i == 0)
        def _():
            for r in (dwg, dwv, dbg, dbv):
                r[...] = jnp.zeros_like(r)

        dbg[...] += jnp.sum(d_gate, axis=0, keepdims=True)
        dbv[...] += jnp.sum(d_val, axis=0, keepdims=True)
        for k in range(9):
            dwg[k:k + 1, :] += jnp.sum(taps_g[k] * d_gate, axis=0, keepdims=True)
            dwv[k:k + 1, :] += jnp.sum(taps_v[k] * d_val, axis=0, keepdims=True)

    shp = jax.ShapeDtypeStruct
    return _call(body, name="conv_act_bwd", grid=(width // cb, n_tok // tb),
                 in_specs=[prev, main, nxt, prev, main, nxt, main, main, main],
                 out_specs=[main, main, par(9), par(9), par(1), par(1)],
                 out_shape=[shp((n_tok, width), BF16), shp((n_tok, width), BF16), shp((9, width), F32),
                            shp((9, width), F32), shp((1, width), F32), shp((1, width), F32)],
                 compiler_params=_params(2))(up_g, up_g, up_g, up_v, up_v, up_v, gate_c, val_c, d_act)


def _conv_plain(x, w, name):
    n_tok, width = x.shape
    tb, cb, main, prev, nxt, par = _conv_geometry(n_tok, width)
    n_blk = n_tok // tb

    def body(xp, xm, xn, w_ref, o_ref):
        taps = _conv_taps(xp[...], xm[...], xn[...], pl.program_id(1), n_blk)
        o_ref[...] = _conv_sum(taps, w_ref).astype(o_ref.dtype)

    return _call(body, name=name, grid=(width // cb, n_tok // tb), in_specs=[prev, main, nxt, par(9)],
                 out_specs=main, out_shape=jax.ShapeDtypeStruct((n_tok, width), BF16),
                 compiler_params=_params(2))(x, x, x, w)


def _adamw(w, g_parts, m, v, name):
    rows, cols = w.shape
    parts = g_parts.shape[0]
    blk = _pick(rows, 256, 8)
    spec = pl.BlockSpec((blk, cols), lambda i: (i, 0))

    def body(w_ref, g_ref, m_ref, v_ref, g_out, d_out, m_out, v_out):
        g = g_ref[0].astype(F32)
        for p in range(1, parts):
            g = g + g_ref[p].astype(F32)
        m_new = ADAM_B1 * m_ref[...] + (1.0 - ADAM_B1) * g
        v_new = ADAM_B2 * v_ref[...] + (1.0 - ADAM_B2) * (g * g)
        m_hat = m_new / (1.0 - ADAM_B1 ** ADAM_STEP)
        v_hat = v_new / (1.0 - ADAM_B2 ** ADAM_STEP)
        g_out[...] = g
        d_out[...] = -ADAM_LR * (m_hat / (jnp.sqrt(v_hat) + ADAM_EPS) + ADAM_WD * w_ref[...])
        m_out[...] = m_new
        v_out[...] = v_new

    return _call(body, name=name, grid=(rows // blk,),
                 in_specs=[spec, pl.BlockSpec((parts, blk, cols), lambda i: (0, i, 0)), spec, spec],
                 out_specs=[spec] * 4, out_shape=[jax.ShapeDtypeStruct((rows, cols), F32)] * 4,
                 compiler_params=_params(1))(w, g_parts, m, v)


def _exchange(items, name):
    n = len(items)
    hbm = pl.BlockSpec(memory_space=pl.ANY)

    def body(*refs):
        srcs, outs = refs[:n], refs[n:2 * n]
        send_sems, recv_sems, own_sems = refs[2 * n:]
        x, y, c = lax.axis_index("x"), lax.axis_index("y"), lax.axis_index("c")
        me = 4 * x + 2 * y + c
        own = []
        for i, (_, mode) in enumerate(items):
            src = srcs[i] if mode == "gather" else srcs[i].at[me]
            cp = pltpu.make_async_copy(src, outs[i].at[me], own_sems.at[i])
            cp.start()
            own.append(cp)
        sent = []
        for i, (_, mode) in enumerate(items):
            for k in range(1, N_DEV):
                px = 1 - x if k & 4 else x
                py = 1 - y if k & 2 else y
                pc = 1 - c if k & 1 else c
                peer = 4 * px + 2 * py + pc
                src = srcs[i] if mode == "gather" else srcs[i].at[peer]
                cp = pltpu.make_async_remote_copy(
                    src_ref=src, dst_ref=outs[i].at[me], send_sem=send_sems.at[i, k - 1],
                    recv_sem=recv_sems.at[i, k - 1], device_id=(px, py, pc), device_id_type=pl.DeviceIdType.MESH)
                cp.start()
                landing = pltpu.make_async_remote_copy(
                    src_ref=src, dst_ref=outs[i].at[peer], send_sem=send_sems.at[i, k - 1],
                    recv_sem=recv_sems.at[i, k - 1], device_id=(px, py, pc), device_id_type=pl.DeviceIdType.MESH)
                sent.append((cp, landing))
        for cp in own:
            cp.wait()
        for cp, landing in sent:
            cp.wait_send()
            landing.wait_recv()

    out_shape = [jax.ShapeDtypeStruct((N_DEV,) + (a.shape if mode == "gather" else a.shape[1:]), a.dtype)
                 for a, mode in items]
    return _call(body, name=name, in_specs=[hbm] * n, out_specs=[hbm] * n, out_shape=out_shape,
                 scratch_shapes=[pltpu.SemaphoreType.DMA((n, N_DEV - 1)), pltpu.SemaphoreType.DMA((n, N_DEV - 1)),
                                 pltpu.SemaphoreType.DMA((n,))])(*[a for a, _ in items])


def _cols_from_blocks(g):
    return jnp.transpose(g, (1, 0, 2)).reshape(g.shape[1], N_DEV * g.shape[2])


def _blocks_from_cols(w):
    r, c8 = w.shape
    return jnp.transpose(w.reshape(r, N_DEV, c8 // N_DEV), (1, 0, 2))


def _pack(arrs):
    flat = jnp.concatenate([a.reshape(-1).astype(F32) for a in arrs])
    pad = (-flat.shape[0]) % (8 * LANE)
    return jnp.pad(flat, (0, pad)).reshape(-1, LANE)


def _unpack(packed, shapes):
    flat = packed.reshape(-1)
    out, off = [], 0
    for s in shapes:
        size = math.prod(s)
        out.append(flat[off:off + size].reshape(s))
        off += size
    return out


def kernel(x, c, ctx, c_ctx, w_ada, b_ada, g_mix, w_in, s5_a_re, s5_a_im, s5_log_step, s5_b_re, s5_b_im, s5_c_re, s5_c_im, s5_d, s5_w_glu, s5_b_glu, sgu_ln_g, sgu_ln_b, sgu_w, sgu_b, w_proj_a, w_proj_b, b_gate, w_out, g_ffn, w_up, conv_w, conv_b, w_down, g_final, loss_target, m_c_ctx, m_w_ada, m_b_ada, m_g_mix, m_w_in, m_s5_a_re, m_s5_a_im, m_s5_log_step, m_s5_b_re, m_s5_b_im, m_s5_c_re, m_s5_c_im, m_s5_d, m_s5_w_glu, m_s5_b_glu, m_sgu_ln_g, m_sgu_ln_b, m_sgu_w, m_sgu_b, m_w_proj_a, m_w_proj_b, m_b_gate, m_w_out, m_g_ffn, m_w_up, m_conv_w, m_conv_b, m_w_down, m_g_final, v_c_ctx, v_w_ada, v_b_ada, v_g_mix, v_w_in, v_s5_a_re, v_s5_a_im, v_s5_log_step, v_s5_b_re, v_s5_b_im, v_s5_c_re, v_s5_c_im, v_s5_d, v_s5_w_glu, v_s5_b_glu, v_sgu_ln_g, v_sgu_ln_b, v_sgu_w, v_sgu_b, v_w_proj_a, v_w_proj_b, v_b_gate, v_w_out, v_g_ffn, v_w_up, v_conv_w, v_conv_b, v_w_down, v_g_final):
    given = dict(locals())
    wts = {n: given[n] for n in WEIGHTS}
    mom1 = {n: given["m_" + n] for n in WEIGHTS}
    mom2 = {n: given["v_" + n] for n in WEIGHTS}

    me = 4 * lax.axis_index("x") + 2 * lax.axis_index("y") + lax.axis_index("c")
    xs, cx, tgt = x[0], ctx[0], loss_target[0]
    n_tok, d = xs.shape
    n_ctx = cx.shape[0]
    s5w = s5_d.shape[1]
    ffn = w_down.shape[1] * N_DEV
    n_mod = w_ada.shape[2] * N_DEV // d
    mod_cols = w_ada.shape[2]

    def two_d(a):
        return a.reshape(-1, a.shape[-1])

    conv_w9 = conv_w[0].reshape(9, -1)
    gathered = _exchange(
        [(c, "gather"), (_b16(w_in[0]), "gather"), (_b16(s5_w_glu[0]), "gather"), (_b16(w_proj_a[0]), "gather"),
         (_b16(w_proj_b[0]), "gather"), (_b16(w_out[0]), "gather"), (_b16(w_up[0]), "gather"),
         (conv_w9, "gather"), (_b16(w_down[0]), "gather")], "gather_weights")
    c_all = gathered[0].reshape(N_DEV, d)
    w_in_f = _cols_from_blocks(gathered[1])
    w_in_u, w_in_rest = w_in_f[:, :s5w], w_in_f[:, s5w:]
    w_glu_f = gathered[2].reshape(-1, s5w)
    w_pa_f = _cols_from_blocks(gathered[3])
    w_pb_f = _cols_from_blocks(gathered[4])
    w_out_f = gathered[5].reshape(-1, d)
    w_up_f = _cols_from_blocks(gathered[6])
    w_up_g, w_up_v = w_up_f[:, :ffn], w_up_f[:, ffn:]
    conv_w_f = _cols_from_blocks(gathered[7])
    w_down_f = gathered[8].reshape(-1, d)

    cs_in = jnp.concatenate([c_all, jnp.broadcast_to(c_ctx[None, :], (N_DEV, d))], axis=0)
    w_ada_loc = w_ada[0]
    (mod_mine,) = _stage_fwd(_fn_mod, [cs_in], [w_ada_loc], [(mod_cols, F32)], blk=2 * N_DEV, name="mod_fwd")
    (mod_blocks,) = _exchange([(mod_mine, "gather")], "gather_mod")
    mod_all = _cols_from_blocks(mod_blocks) + b_ada
    mod = lax.dynamic_slice_in_dim(mod_all, me, 1, axis=0)
    mod_c = mod_all[N_DEV:N_DEV + 1]
    sh1, sc1, ga1, sh2, sc2, ga2 = [mod[:, i * d:(i + 1) * d] for i in range(n_mod)]
    sh1c, sc1c = mod_c[:, :d], mod_c[:, d:2 * d]

    a_par = [g_mix, sh1, sc1]
    ac_par = [g_mix, sh1c, sc1c]
    (h,) = _stage_fwd(_fn_a, [xs], a_par, [(d, BF16)], blk=_pick(n_tok, 512, 8), name="modulate1_fwd")
    (hc,) = _stage_fwd(_fn_a, [cx], ac_par, [(d, BF16)], blk=_pick(n_ctx, 512, 8), name="modulate1_ctx_fwd")
    pu = _mm(h, w_in_u, name="proj_u")
    prest = _mm(h, w_in_rest, name="proj_rest")
    puc = _mm(hc, w_in_u, name="proj_u_ctx")

    s5_prm = [s5_a_re[0][:, :, None, :], s5_a_im[0][:, :, None, :], s5_log_step[0][:, :, None, None],
              jnp.swapaxes(s5_b_re[0], 2, 3), jnp.swapaxes(s5_b_im[0], 2, 3), s5_c_re[0], s5_c_im[0]]
    pu_g, puc_g = _to_groups(pu), _to_groups(puc)
    ysc = _from_groups(_s5_fwd(pu_g, puc_g, s5_prm))

    b_par = [s5_d, w_glu_f, s5_b_glu, sgu_ln_g, sgu_ln_b, two_d(sgu_w[0]), jnp.transpose(sgu_b[0]),
             w_pa_f, w_pb_f, b_gate]
    b_rows = [pu, prest, ysc]
    b_blk = _pick(n_tok, 256, CHUNK)
    (mpre,) = _stage_fwd(_fn_b, b_rows, b_par, [(d, BF16)], blk=b_blk, name="mixers_fwd")
    mo = _mm(mpre, w_out_f, name="out_proj")

    c_par = [ga1, g_ffn, sh2, sc2]
    c_blk = _pick(n_tok, 512, 8)
    x1, h2 = _stage_fwd(_fn_c, [xs, mo], c_par, [(d, F32), (d, BF16)], blk=c_blk, name="modulate2_fwd")
    up_g = _mm(h2, w_up_g, out_dtype=BF16, name="up_gate")
    up_v = _mm(h2, w_up_v, out_dtype=BF16, name="up_val")
    cw_g, cw_v = conv_w_f[:, :ffn], conv_w_f[:, ffn:]
    cb_g, cb_v = conv_b[:, :ffn], conv_b[:, ffn:]
    act, gate_c, val_c = _conv_act_fwd(up_g, up_v, cw_g, cw_v, cb_g, cb_v)
    dn = _mm(act, w_down_f, name="down_proj")

    loss_part, d_x1a, d_dn, d_ga2, d_g_final = _stage_loss(
        _fn_e, [x1, dn, tgt], [ga2, g_final[None, :]], blk=c_blk, name="loss_head",
        row_grads=[(0, F32), (1, BF16)])

    d_act = _mm(d_dn, w_down_f, tb=True, name="down_proj_dx")
    g_w_down = _mm(act, d_dn, ta=True, name="down_proj_dw")
    dcg, dcv, g_cw_g, g_cw_v, g_cb_g, g_cb_v = _conv_act_bwd(up_g, up_v, gate_c, val_c, d_act)
    dug = _conv_plain(dcg, cw_g[::-1], "conv_dx_gate")
    duv = _conv_plain(dcv, cw_v[::-1], "conv_dx_val")
    d_h2 = _mm(dug, w_up_g, tb=True, name="up_gate_dx")
    d_h2 = _mm(duv, w_up_v, tb=True, add=d_h2, out_dtype=BF16, name="up_val_dx")
    g_w_up = jnp.concatenate([_mm(h2, dug, ta=True, name="up_gate_dw"), _mm(h2, duv, ta=True, name="up_val_dw")],
                             axis=1)
    (d_xc, d_mo), (d_ga1, g_g_ffn, d_sh2, d_sc2) = _split(_stage_bwd(
        _fn_c, [xs, mo], c_par, [d_x1a, d_h2], blk=c_blk, name="modulate2_bwd",
        row_grads=[(0, F32), (1, BF16)]), 2)

    d_mpre = _mm(d_mo, w_out_f, tb=True, out_dtype=BF16, name="out_proj_dx")
    g_w_out = _mm(mpre, d_mo, ta=True, name="out_proj_dw")
    (d_prest, d_ysc), b_grads = _split(_stage_bwd(
        _fn_b, b_rows, b_par, [d_mpre], blk=b_blk, name="mixers_bwd", row_grads=[(1, BF16), (2, F32)]), 2)
    (g_s5_d, g_w_glu, g_b_glu, g_ln_g, g_ln_b, g_sgu_w, g_sgu_bt, g_w_pa, g_w_pb, g_b_gate) = b_grads

    skip_g = jnp.tile(s5_d.reshape(-1, 1, S5_H), (1, 1, S5_T))
    s5_out = _s5_bwd(pu_g, puc_g, s5_prm, _to_groups(d_ysc), skip_g)
    d_pu, d_puc = _b16(_from_groups(s5_out[0])), _from_groups(s5_out[1])
    g_a_re, g_a_im, g_ls, g_bt_re, g_bt_im, g_c_re, g_c_im = s5_out[2:]

    d_h = _mm(d_pu, w_in_u, tb=True, name="proj_u_dx")
    d_h = _mm(d_prest, w_in_rest, tb=True, add=d_h, out_dtype=BF16, name="proj_rest_dx")
    d_hc = _mm(d_puc, w_in_u, tb=True, out_dtype=BF16, name="proj_u_ctx_dx")
    g_w_in_u = _mm(hc, d_puc, ta=True, name="proj_u_ctx_dw")
    g_w_in_u = _mm(h, d_pu, ta=True, add=g_w_in_u, name="proj_u_dw")
    g_w_in = jnp.concatenate([g_w_in_u, _mm(h, d_prest, ta=True, name="proj_rest_dw")], axis=1)

    (grad_x,), (g_g_mix_x, d_sh1, d_sc1) = _split(_stage_bwd(
        _fn_a_res, [xs], a_par, [d_h, d_xc], blk=c_blk, name="modulate1_bwd", row_grads=[(0, F32)]), 1)
    _, (g_g_mix_c, d_sh1c, d_sc1c) = _split(_stage_bwd(
        _fn_a, [cx], ac_par, [d_hc], blk=_pick(n_ctx, 512, 8), name="modulate1_ctx_bwd", row_grads=[]), 0)

    zeros = jnp.zeros((1, (n_mod - 2) * d), F32)
    d_mod = jnp.concatenate([d_sh1, d_sc1, d_ga1, d_sh2, d_sc2, d_ga2], axis=1)
    d_mod_c = jnp.concatenate([d_sh1c, d_sc1c, zeros], axis=1)
    (d_mod_all,) = _exchange([(jnp.concatenate([d_mod, d_mod_c], axis=0), "gather")], "gather_dmod")
    d_mod_rows = jnp.transpose(d_mod_all, (1, 0, 2)).reshape(2 * N_DEV, n_mod * d)
    d_mod_mine = lax.dynamic_slice_in_dim(d_mod_rows, me * mod_cols, mod_cols, axis=1)
    (d_cs,), (g_w_ada,) = _split(_stage_bwd(
        _fn_mod, [cs_in], [w_ada_loc], [d_mod_mine], blk=2 * N_DEV, name="mod_bwd", row_grads=[(0, F32)]), 1)

    part = {
        'c_ctx': jnp.sum(d_cs[N_DEV:], axis=0), 'b_ada': d_mod + d_mod_c, 'g_mix': g_g_mix_x + g_g_mix_c,
        's5_a_re': g_a_re, 's5_a_im': g_a_im, 's5_log_step': g_ls,
        's5_b_re': jnp.swapaxes(g_bt_re, 2, 3), 's5_b_im': jnp.swapaxes(g_bt_im, 2, 3),
        's5_c_re': g_c_re, 's5_c_im': g_c_im, 's5_d': g_s5_d, 's5_b_glu': g_b_glu, 'sgu_ln_g': g_ln_g,
        'sgu_ln_b': g_ln_b, 'sgu_w': g_sgu_w, 'sgu_b': jnp.transpose(g_sgu_bt), 'b_gate': g_b_gate,
        'g_ffn': g_g_ffn, 'conv_b': jnp.concatenate([g_cb_g, g_cb_v], axis=1), 'g_final': d_g_final,
    }
    rep_shapes = [wts[n].shape for n in REPLICATED]
    g_full = {
        'w_in': g_w_in, 'w_proj_a': g_w_pa, 'w_proj_b': g_w_pb, 'w_up': g_w_up,
        'conv_w': jnp.concatenate([g_cw_g, g_cw_v], axis=1), 's5_w_glu': g_w_glu, 'w_out': g_w_out,
        'w_down': g_w_down,
    }
    big = [n for n in SHARDED if n != 'w_ada']
    items = [(_pack([part[n] for n in REPLICATED]), "gather")]
    for n in big:
        g = g_full[n]
        blocks = _blocks_from_cols(g) if n in COL_SHARDED else g.reshape(N_DEV, g.shape[0] // N_DEV, g.shape[1])
        items.append((_b16(blocks), "a2a"))
    reduced = _exchange(items, "exchange_grads")

    out = {}
    rep = _adamw(_pack([wts[n] for n in REPLICATED]), reduced[0], _pack([mom1[n] for n in REPLICATED]),
                 _pack([mom2[n] for n in REPLICATED]), "adamw_replicated")
    rep = [_unpack(r, rep_shapes) for r in rep]
    for i, n in enumerate(REPLICATED):
        out[n] = tuple(r[i] for r in rep)
    for n, parts in zip(big, reduced[1:]):
        shape = wts[n].shape
        res = _adamw(two_d(wts[n]), parts, two_d(mom1[n]), two_d(mom2[n]), "adamw_" + n)
        out[n] = tuple(r.reshape(shape) for r in res)
    res = _adamw(w_ada_loc, g_w_ada[None], m_w_ada[0], v_w_ada[0], "adamw_w_ada")
    out['w_ada'] = tuple(r.reshape(w_ada.shape) for r in res)

    loss = lax.psum(loss_part[0, 0], ("x", "y", "c"))
    return (loss, grad_x[None], *[out[n][0] for n in WEIGHTS], *[out[n][1] for n in WEIGHTS],
            *[out[n][2] for n in WEIGHTS], *[out[n][3] for n in WEIGHTS])


def _split(res, n_rows):
    return tuple(res[:n_rows]), tuple(res[n_rows:])
```

```python
import functools
import math

import jax
import jax.numpy as jnp
from jax import lax
from jax.experimental import pallas as pl
from jax.experimental.pallas import tpu as pltpu

F32 = jnp.float32
BF16 = jnp.bfloat16
HI = lax.Precision.HIGHEST

N_DEV = 8
GRID_W = 64
CHUNK = 128
EPS = 1e-6
S5_T = 32
S5_H = 16
LANE = 128
HALO = 128
VMEM_LIMIT = 56 * 1024 * 1024

ADAM_LR = 0.001
ADAM_B1 = 0.9
ADAM_B2 = 0.999
ADAM_EPS = 1e-08
ADAM_WD = 0.01
ADAM_STEP = 10

WEIGHTS = ['c_ctx', 'w_ada', 'b_ada', 'g_mix', 'w_in', 's5_a_re', 's5_a_im', 's5_log_step', 's5_b_re', 's5_b_im',
           's5_c_re', 's5_c_im', 's5_d', 's5_w_glu', 's5_b_glu', 'sgu_ln_g', 'sgu_ln_b', 'sgu_w', 'sgu_b',
           'w_proj_a', 'w_proj_b', 'b_gate', 'w_out', 'g_ffn', 'w_up', 'conv_w', 'conv_b', 'w_down', 'g_final']
COL_SHARDED = ('w_ada', 'w_in', 'w_proj_a', 'w_proj_b', 'w_up', 'conv_w')
ROW_SHARDED = ('s5_w_glu', 'w_out', 'w_down')
SHARDED = COL_SHARDED + ROW_SHARDED
REPLICATED = [n for n in WEIGHTS if n not in SHARDED]


def _call(body, **kw):
    return pl.pallas_call(body, **kw)


def _params(n_grid):
    return pltpu.CompilerParams(dimension_semantics=("arbitrary",) * n_grid, vmem_limit_bytes=VMEM_LIMIT)


def _pick(dim, pref, unit=LANE):
    best = None
    d = unit
    while d <= min(dim, pref):
        if dim % d == 0:
            best = d
        d += unit
    return best if best is not None else dim


def _dg(a, b, ca, cb, prec=None):
    return lax.dot_general(a, b, (((ca,), (cb,)), ((), ())), precision=prec, preferred_element_type=F32)


def _b16(v):
    return v.astype(BF16)


@jax.custom_vjp
def mmb(a, b):
    return _dg(_b16(a), _b16(b), 1, 0)


def _mmb_fwd(a, b):
    return mmb(a, b), (a, b)


def _mmb_bwd(res, g):
    a, b = res
    g = _b16(g)
    return _dg(g, _b16(b), 1, 1).astype(a.dtype), _dg(_b16(a), g, 0, 0).astype(b.dtype)


mmb.defvjp(_mmb_fwd, _mmb_bwd)


@jax.custom_vjp
def mmb_nt(a, b):
    return _dg(_b16(a), _b16(b), 1, 1)


def _mmb_nt_fwd(a, b):
    return mmb_nt(a, b), (a, b)


def _mmb_nt_bwd(res, g):
    a, b = res
    g = _b16(g)
    return _dg(g, _b16(b), 1, 0).astype(a.dtype), _dg(g, _b16(a), 0, 0).astype(b.dtype)


mmb_nt.defvjp(_mmb_nt_fwd, _mmb_nt_bwd)


@jax.custom_vjp
def mmf(a, b):
    return _dg(a, b, 1, 0, HI)


def _mmf_fwd(a, b):
    return mmf(a, b), (a, b)


def _mmf_bwd(res, g):
    a, b = res
    return _dg(g, b, 1, 1, HI), _dg(a, g, 0, 0, HI)


mmf.defvjp(_mmf_fwd, _mmf_bwd)


def _dg3(a, b, ca, cb):
    ah, bh = _b16(a), _b16(b)
    al, bl = _b16(a - ah.astype(F32)), _b16(b - bh.astype(F32))
    return _dg(ah, bh, ca, cb) + _dg(ah, bl, ca, cb) + _dg(al, bh, ca, cb)


@jax.custom_vjp
def mm3_nt(a, b):
    return _dg3(a, b, 1, 1)


def _mm3_nt_fwd(a, b):
    return mm3_nt(a, b), (a, b)


def _mm3_nt_bwd(res, g):
    a, b = res
    return _dg3(g, b, 1, 0), _dg3(g, a, 0, 0)


mm3_nt.defvjp(_mm3_nt_fwd, _mm3_nt_bwd)


def _shift_impl(x, k, up):
    n = x.shape[0]
    idx = lax.broadcasted_iota(jnp.int32, (n, 1), 0)
    if up:
        return jnp.where(idx < n - k, pltpu.roll(x, n - k, 0), 0.0)
    return jnp.where(idx >= k, pltpu.roll(x, k, 0), 0.0)


@functools.partial(jax.custom_vjp, nondiff_argnums=(1, 2))
def _shift(x, k, up):
    return _shift_impl(x, k, up)


def _shift_fwd(x, k, up):
    return _shift_impl(x, k, up), None


def _shift_bwd(k, up, _, g):
    return (_shift_impl(g, k, not up),)


_shift.defvjp(_shift_fwd, _shift_bwd)


def _mm(a, b, *, name, ta=False, tb=False, add=None, out_dtype=F32, tm_pref=1408, tn_pref=1408, tk_pref=1408):
    m, k = (a.shape[1], a.shape[0]) if ta else a.shape
    n = b.shape[0] if tb else b.shape[1]
    tm, tn, tk = _pick(m, tm_pref), _pick(n, tn_pref), _pick(k, tk_pref)
    nk = k // tk
    a_spec = (pl.BlockSpec((tk, tm), lambda i, j, kk: (kk, i)) if ta
              else pl.BlockSpec((tm, tk), lambda i, j, kk: (i, kk)))
    b_spec = (pl.BlockSpec((tn, tk), lambda i, j, kk: (j, kk)) if tb
              else pl.BlockSpec((tk, tn), lambda i, j, kk: (kk, j)))
    o_spec = pl.BlockSpec((tm, tn), lambda i, j, kk: (i, j))
    ca, cb = (0 if ta else 1), (1 if tb else 0)
    has_add = add is not None
    in_place = out_dtype == F32 or nk == 1

    def body(*refs):
        a_ref, b_ref = refs[0], refs[1]
        o_ref = refs[3] if has_add else refs[2]
        prod = _dg(_b16(a_ref[...]), _b16(b_ref[...]), ca, cb)
        if nk == 1:
            if has_add:
                prod = prod + refs[2][...].astype(F32)
            o_ref[...] = prod.astype(o_ref.dtype)
            return
        acc_ref = o_ref if in_place else refs[-1]
        kk = pl.program_id(2)

        @pl.when(kk == 0)
        def _():
            acc_ref[...] = prod

        @pl.when(kk > 0)
        def _():
            acc_ref[...] += prod

        if has_add or not in_place:
            @pl.when(kk == nk - 1)
            def _():
                r = acc_ref[...]
                if has_add:
                    r = r + refs[2][...].astype(F32)
                o_ref[...] = r.astype(o_ref.dtype)

    ins = [a, b] + ([add] if has_add else [])
    in_specs = [a_spec, b_spec] + ([o_spec] if has_add else [])
    return _call(body, name=name, grid=(m // tm, n // tn, nk), in_specs=in_specs, out_specs=o_spec,
                 out_shape=jax.ShapeDtypeStruct((m, n), out_dtype),
                 scratch_shapes=[] if in_place else [pltpu.VMEM((tm, tn), F32)],
                 compiler_params=_params(3))(*ins)


def _row_spec(blk, width):
    return pl.BlockSpec((blk, width), lambda i: (i, 0))


def _whole_spec(shape):
    return pl.BlockSpec(shape, lambda i: (0,) * len(shape))


def _stage_fwd(fn, rows, params, outs, *, blk, name, n_rows=None):
    n = n_rows or rows[0].shape[0]
    nr, npar = len(rows), len(params)

    def body(*refs):
        vals = [r[...] for r in refs[:nr + npar]]
        res = fn(*vals)
        for o_ref, v in zip(refs[nr + npar:], res):
            o_ref[...] = v.astype(o_ref.dtype)

    return _call(body, name=name, grid=(n // blk,),
                 in_specs=[_row_spec(blk, r.shape[1]) for r in rows] + [_whole_spec(p.shape) for p in params],
                 out_specs=[_row_spec(blk, w) for w, _ in outs],
                 out_shape=[jax.ShapeDtypeStruct((n, w), dt) for w, dt in outs],
                 compiler_params=_params(1))(*rows, *params)


def _stage_bwd(fn, rows, params, cts, *, blk, name, row_grads, n_rows=None):
    n = n_rows or rows[0].shape[0]
    nr, npar, nct = len(rows), len(params), len(cts)

    def body(*refs):
        vals = [r[...].astype(F32) for r in refs[:nr + npar]]
        ct = [r[...] for r in refs[nr + npar:nr + npar + nct]]
        d_rows = refs[nr + npar + nct:nr + npar + nct + len(row_grads)]
        d_par = refs[nr + npar + nct + len(row_grads):]
        res, vjp = jax.vjp(fn, *vals)
        g = vjp(tuple(c.astype(r.dtype) for c, r in zip(ct, res)))
        for o_ref, (j, _) in zip(d_rows, row_grads):
            o_ref[...] = g[j].astype(o_ref.dtype)

        @pl.when(pl.program_id(0) == 0)
        def _():
            for o_ref in d_par:
                o_ref[...] = jnp.zeros_like(o_ref)

        for j, o_ref in enumerate(d_par):
            o_ref[...] += g[nr + j].astype(F32)

    return _call(body, name=name, grid=(n // blk,),
                 in_specs=([_row_spec(blk, r.shape[1]) for r in rows] + [_whole_spec(p.shape) for p in params]
                           + [_row_spec(blk, c.shape[1]) for c in cts]),
                 out_specs=([_row_spec(blk, rows[j].shape[1]) for j, _ in row_grads]
                            + [_whole_spec(p.shape) for p in params]),
                 out_shape=([jax.ShapeDtypeStruct((n, rows[j].shape[1]), dt) for j, dt in row_grads]
                            + [jax.ShapeDtypeStruct(p.shape, F32) for p in params]),
                 compiler_params=_params(1))(*rows, *params, *cts)


def _stage_loss(fn, rows, params, *, blk, name, row_grads):
    n = rows[0].shape[0]
    nr, npar = len(rows), len(params)

    def body(*refs):
        vals = [r[...].astype(F32) for r in refs[:nr + npar]]
        loss_ref = refs[nr + npar]
        d_rows = refs[nr + npar + 1:nr + npar + 1 + len(row_grads)]
        d_par = refs[nr + npar + 1 + len(row_grads):]
        res, vjp = jax.vjp(fn, *vals)
        g = vjp(jnp.ones_like(res))
        for o_ref, (j, _) in zip(d_rows, row_grads):
            o_ref[...] = g[j].astype(o_ref.dtype)

        @pl.when(pl.program_id(0) == 0)
        def _():
            loss_ref[...] = jnp.zeros_like(loss_ref)
            for o_ref in d_par:
                o_ref[...] = jnp.zeros_like(o_ref)

        loss_ref[...] += res
        for j, o_ref in enumerate(d_par):
            o_ref[...] += g[nr + j].astype(F32)

    return _call(body, name=name, grid=(n // blk,),
                 in_specs=[_row_spec(blk, r.shape[1]) for r in rows] + [_whole_spec(p.shape) for p in params],
                 out_specs=([_whole_spec((1, 1))] + [_row_spec(blk, rows[j].shape[1]) for j, _ in row_grads]
                            + [_whole_spec(p.shape) for p in params]),
                 out_shape=([jax.ShapeDtypeStruct((1, 1), F32)]
                            + [jax.ShapeDtypeStruct((n, rows[j].shape[1]), dt) for j, dt in row_grads]
                            + [jax.ShapeDtypeStruct(p.shape, F32) for p in params]),
                 compiler_params=_params(1))(*rows, *params)


def _rms(x, g):
    return x * lax.rsqrt(jnp.mean(x * x, axis=-1, keepdims=True) + EPS) * g


def _modulate(x, g, shift, scale):
    return _rms(x, g) * (1.0 + scale) + shift


def _fn_mod(cs, w_ada):
    return (mmb(jax.nn.silu(cs), w_ada),)


def _fn_a(x, g_mix, sh, sc):
    return (_b16(_modulate(x, g_mix, sh, sc)),)


def _fn_a_res(x, g_mix, sh, sc):
    return _b16(_modulate(x, g_mix, sh, sc)), x


def _sgu_spatial(v, sgu_w, sgu_bt):
    rows, width = v.shape
    gdim = width // (sgu_w.shape[0] // CHUNK)
    groups = width // gdim
    expand = (lax.broadcasted_iota(jnp.int32, (groups, width), 1) // gdim
              == lax.broadcasted_iota(jnp.int32, (groups, width), 0)).astype(F32)
    bias = mmf(sgu_bt, expand)
    lane = lax.broadcasted_iota(jnp.int32, (CHUNK, LANE), 1)
    per_lane_block = LANE // gdim
    chunks = []
    for ci in range(rows // CHUNK):
        vc = v[ci * CHUNK:(ci + 1) * CHUNK]
        blocks = []
        for lb in range(width // LANE):
            vb = vc[:, lb * LANE:(lb + 1) * LANE]
            acc = None
            for s in range(per_lane_block):
                g = lb * per_lane_block + s
                r = mmb(sgu_w[g * CHUNK:(g + 1) * CHUNK], vb)
                sel = (lane // gdim) == s
                acc = jnp.where(sel, r, 0.0) if acc is None else jnp.where(sel, r, acc)
            blocks.append(acc)
        chunks.append(jnp.concatenate(blocks, axis=1) + bias)
    return jnp.concatenate(chunks, axis=0)


def _fn_b(pu, prest, ysc, s5_d, w_glu, b_glu, ln_g, ln_b, sgu_w, sgu_bt, w_pa, w_pb, b_gate):
    sw = ln_g.shape[1]
    y = jax.nn.gelu(pu * s5_d + ysc)
    ya = y * jax.nn.sigmoid(mmb(y, w_glu) + b_glu)
    z = jax.nn.gelu(prest[:, :2 * sw])
    u, v = z[:, :sw], z[:, sw:]
    vc = v - jnp.mean(v, axis=-1, keepdims=True)
    v = vc * lax.rsqrt(jnp.mean(vc * vc, axis=-1, keepdims=True) + EPS) * ln_g + ln_b
    yb = u * _sgu_spatial(v, sgu_w, sgu_bt)
    gates = jax.nn.sigmoid(prest[:, 2 * sw:] + b_gate)
    d = gates.shape[1] // 2
    return (_b16(gates[:, :d] * mmb(ya, w_pa) + gates[:, d:] * mmb(yb, w_pb)),)


def _fn_c(x, mo, ga1, g_ffn, sh2, sc2):
    x1 = x + ga1 * mo
    return x1, _b16(_modulate(x1, g_ffn, sh2, sc2))


def _fn_e(x1, dn, tgt, ga2, g_final):
    y = _rms(x1 + ga2 * dn, g_final)
    err = (y - tgt) ** 2
    return 0.5 * jnp.sum(jnp.mean(err, axis=-1, keepdims=True), axis=0, keepdims=True)


def _s5_direction(u, mask, a_re, a_im, log_step, bt_re, bt_im, c_re, c_im, rev):
    nc, width = u.shape
    t_len = width // S5_H
    n = a_re.shape[1]
    dt = jnp.exp(log_step)
    lr, li = a_re * dt, a_im * dt
    mag = jnp.exp(lr)
    ab_re, ab_im = mag * jnp.cos(li), mag * jnp.sin(li)
    p, q = ab_re - 1.0, ab_im
    den = a_re * a_re + a_im * a_im
    k_re, k_im = (p * a_re + q * a_im) / den, (q * a_re - p * a_im) / den
    bb_re = k_re * bt_re - k_im * bt_im
    bb_im = k_re * bt_im + k_im * bt_re

    def power(e):
        m = jnp.exp(lr * e)
        return m * jnp.cos(li * e), m * jnp.sin(li * e)

    def cmul(xr, xi, yr, yi):
        return xr * yr - xi * yi, xr * yi + xi * yr

    order = range(t_len - 1, -1, -1) if rev else range(t_len)
    e1 = jnp.concatenate([jnp.full((1, 1, n), float(t_len - 1 - pos), F32) for pos in order], axis=0)
    lr3, li3 = lr.reshape(1, 1, n), li.reshape(1, 1, n)
    m1, c1, s1 = jnp.exp(lr3 * e1), jnp.cos(li3 * e1), jnp.sin(li3 * e1)
    m2 = jnp.exp(-(lr3 * e1))
    x1 = (m1 * c1, m1 * s1)
    x2 = (m2 * c1, -(m2 * s1))
    x3 = cmul(*[v.reshape(1, 1, n) for v in power(float(t_len))], *x2)

    def rows(xr, xi, yr, yi):
        zr, zi = cmul(xr, xi, yr.reshape(1, S5_H, n), yi.reshape(1, S5_H, n))
        return zr.reshape(width, n), zi.reshape(width, n)

    pr, pi = rows(*x1, bb_re, bb_im)
    rr, ri = rows(*x2, c_re, c_im)
    wr, wi = rows(*x3, c_re, c_im)
    toep = (mm3_nt(pr, rr) - mm3_nt(pi, ri)) * mask

    sr, si = mmb(u, pr), mmb(u, pi)
    k = 1
    while k < nc:
        ar, ai = power(float(t_len * k))
        hr, hi = _shift(sr, k, rev), _shift(si, k, rev)
        sr, si = sr + ar * hr - ai * hi, si + ar * hi + ai * hr
        k *= 2
    cr_in, ci_in = _shift(sr, 1, rev), _shift(si, 1, rev)
    return mmb(u, toep) + mmb_nt(cr_in, wr) - mmb_nt(ci_in, wi)


def _fn_s5(masks, x_chunks, ctx_chunks, *prm):
    nx, nctx = x_chunks.shape[0], ctx_chunks.shape[0]
    pad = (-(nx + 2 * nctx)) % LANE
    u = jnp.concatenate([ctx_chunks, x_chunks, ctx_chunks, jnp.zeros((pad, x_chunks.shape[1]), F32)], axis=0)
    out = None
    for d in range(2):
        y = _s5_direction(u, masks[d], *[p[d] for p in prm], rev=(d == 1))
        out = y if out is None else out + y
    return out[nctx:nctx + nx]


def _s5_specs(prm):
    return [pl.BlockSpec((2, 1) + p.shape[2:], lambda g: (0, g, 0, 0)) for p in prm]


def _group_spec(a):
    return pl.BlockSpec((1,) + a.shape[1:], lambda i: (i, 0, 0))


def _s5_masks(width):
    pos = jnp.arange(width) // S5_H
    causal = (pos[None, :] >= pos[:, None]).astype(F32)
    return jnp.stack([causal, causal.T])


def _s5_fwd(x_g, ctx_g, prm):
    masks = _s5_masks(x_g.shape[2])

    def body(*refs):
        pv = [r[:, 0] for r in refs[3:3 + len(prm)]]
        refs[-1][0] = _fn_s5(refs[0][...], refs[1][0], refs[2][0], *pv)

    return _call(body, name="s5_fwd", grid=(x_g.shape[0],),
                 in_specs=[_whole_spec(masks.shape), _group_spec(x_g), _group_spec(ctx_g)] + _s5_specs(prm),
                 out_specs=_group_spec(x_g), out_shape=jax.ShapeDtypeStruct(x_g.shape, F32),
                 compiler_params=_params(1))(masks, x_g, ctx_g, *prm)


def _s5_bwd(x_g, ctx_g, prm, dy_g, skip_g):
    npar = len(prm)
    masks = _s5_masks(x_g.shape[2])

    def body(*refs):
        pv = [r[:, 0] for r in refs[3:3 + npar]]
        dy = refs[3 + npar][0]
        _, vjp = jax.vjp(functools.partial(_fn_s5, refs[0][...]), refs[1][0], refs[2][0], *pv)
        grads = vjp(dy)
        outs = refs[5 + npar:]
        outs[0][0] = (grads[0] + dy * refs[4 + npar][0]).astype(outs[0].dtype)
        outs[1][0] = grads[1]
        for o_ref, gv in zip(outs[2:], grads[2:]):
            o_ref[:, 0] = gv

    return _call(body, name="s5_bwd", grid=(x_g.shape[0],),
                 in_specs=([_whole_spec(masks.shape), _group_spec(x_g), _group_spec(ctx_g)] + _s5_specs(prm)
                           + [_group_spec(dy_g), _group_spec(skip_g)]),
                 out_specs=[_group_spec(x_g), _group_spec(ctx_g)] + _s5_specs(prm),
                 out_shape=[jax.ShapeDtypeStruct(x_g.shape, F32), jax.ShapeDtypeStruct(ctx_g.shape, F32)]
                 + [jax.ShapeDtypeStruct(p.shape, F32) for p in prm],
                 compiler_params=_params(1))(masks, x_g, ctx_g, *prm, dy_g, skip_g)


def _to_groups(tok):
    n, width = tok.shape
    g = width // S5_H
    return jnp.transpose(tok.reshape(n, g, S5_H), (1, 0, 2)).reshape(g, n // S5_T, S5_T * S5_H)


def _from_groups(grp):
    g, nc, _ = grp.shape
    return jnp.transpose(grp.reshape(g, nc * S5_T, S5_H), (1, 0, 2)).reshape(nc * S5_T, g * S5_H)


def _conv_taps(xp, xm, xn, blk_i, n_blk):
    tb = xm.shape[0]
    xp = jnp.where(blk_i == 0, 0.0, xp.astype(F32))
    xn = jnp.where(blk_i == n_blk - 1, 0.0, xn.astype(F32))
    buf = jnp.concatenate([xp, xm.astype(F32), xn], axis=0)
    n = tb + 2 * HALO
    col = lax.broadcasted_iota(jnp.int32, (n, 1), 0) % GRID_W
    left = jnp.where(col >= 1, pltpu.roll(buf, 1, 0), 0.0)
    right = jnp.where(col <= GRID_W - 2, pltpu.roll(buf, n - 1, 0), 0.0)
    shifted = (left, buf, right)
    taps = []
    for di in range(3):
        start = HALO + (di - 1) * GRID_W
        for dj in range(3):
            taps.append(shifted[dj][start:start + tb])
    return taps


def _conv_sum(taps, w_ref):
    acc = None
    for k, tap in enumerate(taps):
        term = tap * w_ref[k:k + 1, :]
        acc = term if acc is None else acc + term
    return acc


def _conv_geometry(n_tok, width):
    tb = _pick(n_tok, 1024, HALO)
    cb = _pick(width, 256)
    nb = tb // HALO
    last = n_tok // HALO - 1
    main = pl.BlockSpec((tb, cb), lambda j, i: (i, j))
    prev = pl.BlockSpec((HALO, cb), lambda j, i: (jnp.maximum(i * nb - 1, 0), j))
    nxt = pl.BlockSpec((HALO, cb), lambda j, i: (jnp.minimum(i * nb + nb, last), j))
    par = lambda r: pl.BlockSpec((r, cb), lambda j, i: (0, j))
    return tb, cb, main, prev, nxt, par


def _conv_act_fwd(up_g, up_v, w_g, w_v, b_g, b_v):
    n_tok, width = up_g.shape
    tb, cb, main, prev, nxt, par = _conv_geometry(n_tok, width)
    n_blk = n_tok // tb

    def body(gp, gm, gn, vp, vm, vn, wg, wv, bg, bv, o_ref, gate_ref, val_ref):
        i = pl.program_id(1)
        gate = _conv_sum(_conv_taps(gp[...], gm[...], gn[...], i, n_blk), wg) + bg[...]
        val = _conv_sum(_conv_taps(vp[...], vm[...], vn[...], i, n_blk), wv) + bv[...]
        o_ref[...] = (jax.nn.silu(gate) * val).astype(o_ref.dtype)
        gate_ref[...] = gate
        val_ref[...] = val

    shp = jax.ShapeDtypeStruct
    return _call(body, name="conv_act_fwd", grid=(width // cb, n_tok // tb),
                 in_specs=[prev, main, nxt, prev, main, nxt, par(9), par(9), par(1), par(1)],
                 out_specs=[main, main, main],
                 out_shape=[shp((n_tok, width), BF16), shp((n_tok, width), F32), shp((n_tok, width), F32)],
                 compiler_params=_params(2))(up_g, up_g, up_g, up_v, up_v, up_v, w_g, w_v, b_g, b_v)


def _conv_act_bwd(up_g, up_v, gate_c, val_c, d_act):
    n_tok, width = up_g.shape
    tb, cb, main, prev, nxt, par = _conv_geometry(n_tok, width)
    n_blk = n_tok // tb

    def body(gp, gm, gn, vp, vm, vn, gc, vc, da, dcg, dcv, dwg, dwv, dbg, dbv):
        i = pl.program_id(1)
        taps_g = _conv_taps(gp[...], gm[...], gn[...], i, n_blk)
        taps_v = _conv_taps(vp[...], vm[...], vn[...], i, n_blk)
        gate, val = gc[...], vc[...]
        sig = jax.nn.sigmoid(gate)
        d = da[...].astype(F32)
        d_gate = d * val * sig * (1.0 + gate * (1.0 - sig))
        d_val = d * gate * sig
        dcg[...] = d_gate.astype(dcg.dtype)
        dcv[...] = d_val.astype(dcv.dtype)

        @pl.when(i == 0)
        def _():
            for r in (dwg, dwv, dbg, dbv):
                r[...] = jnp.zeros_like(r)

        dbg[...] += jnp.sum(d_gate, axis=0, keepdims=True)
        dbv[...] += jnp.sum(d_val, axis=0, keepdims=True)
        for k in range(9):
            dwg[k:k + 1, :] += jnp.sum(taps_g[k] * d_gate, axis=0, keepdims=True)
            dwv[k:k + 1, :] += jnp.sum(taps_v[k] * d_val, axis=0, keepdims=True)

    shp = jax.ShapeDtypeStruct
    return _call(body, name="conv_act_bwd", grid=(width // cb, n_tok // tb),
                 in_specs=[prev, main, nxt, prev, main, nxt, main, main, main],
                 out_specs=[main, main, par(9), par(9), par(1), par(1)],
                 out_shape=[shp((n_tok, width), BF16), shp((n_tok, width), BF16), shp((9, width), F32),
                            shp((9, width), F32), shp((1, width), F32), shp((1, width), F32)],
                 compiler_params=_params(2))(up_g, up_g, up_g, up_v, up_v, up_v, gate_c, val_c, d_act)


def _conv_plain(x, w, name):
    n_tok, width = x.shape
    tb, cb, main, prev, nxt, par = _conv_geometry(n_tok, width)
    n_blk = n_tok // tb

    def body(xp, xm, xn, w_ref, o_ref):
        taps = _conv_taps(xp[...], xm[...], xn[...], pl.program_id(1), n_blk)
        o_ref[...] = _conv_sum(taps, w_ref).astype(o_ref.dtype)

    return _call(body, name=name, grid=(width // cb, n_tok // tb), in_specs=[prev, main, nxt, par(9)],
                 out_specs=main, out_shape=jax.ShapeDtypeStruct((n_tok, width), BF16),
                 compiler_params=_params(2))(x, x, x, w)


def _adamw(w, g_parts, m, v, name):
    rows, cols = w.shape
    parts = g_parts.shape[0]
    blk = _pick(rows, 256, 8)
    spec = pl.BlockSpec((blk, cols), lambda i: (i, 0))

    def body(w_ref, g_ref, m_ref, v_ref, g_out, d_out, m_out, v_out):
        g = g_ref[0].astype(F32)
        for p in range(1, parts):
            g = g + g_ref[p].astype(F32)
        m_new = ADAM_B1 * m_ref[...] + (1.0 - ADAM_B1) * g
        v_new = ADAM_B2 * v_ref[...] + (1.0 - ADAM_B2) * (g * g)
        m_hat = m_new / (1.0 - ADAM_B1 ** ADAM_STEP)
        v_hat = v_new / (1.0 - ADAM_B2 ** ADAM_STEP)
        g_out[...] = g
        d_out[...] = -ADAM_LR * (m_hat / (jnp.sqrt(v_hat) + ADAM_EPS) + ADAM_WD * w_ref[...])
        m_out[...] = m_new
        v_out[...] = v_new

    return _call(body, name=name, grid=(rows // blk,),
                 in_specs=[spec, pl.BlockSpec((parts, blk, cols), lambda i: (0, i, 0)), spec, spec],
                 out_specs=[spec] * 4, out_shape=[jax.ShapeDtypeStruct((rows, cols), F32)] * 4,
                 compiler_params=_params(1))(w, g_parts, m, v)


def _exchange(items, name):
    n = len(items)
    hbm = pl.BlockSpec(memory_space=pl.ANY)

    def body(*refs):
        srcs, outs = refs[:n], refs[n:2 * n]
        send_sems, recv_sems, own_sems = refs[2 * n:]
        x, y, c = lax.axis_index("x"), lax.axis_index("y"), lax.axis_index("c")
        me = 4 * x + 2 * y + c
        own = []
        for i, (_, mode) in enumerate(items):
            src = srcs[i] if mode == "gather" else srcs[i].at[me]
            cp = pltpu.make_async_copy(src, outs[i].at[me], own_sems.at[i])
            cp.start()
            own.append(cp)
        sent = []
        for i, (_, mode) in enumerate(items):
            for k in range(1, N_DEV):
                px = 1 - x if k & 4 else x
                py = 1 - y if k & 2 else y
                pc = 1 - c if k & 1 else c
                peer = 4 * px + 2 * py + pc
                src = srcs[i] if mode == "gather" else srcs[i].at[peer]
                cp = pltpu.make_async_remote_copy(
                    src_ref=src, dst_ref=outs[i].at[me], send_sem=send_sems.at[i, k - 1],
                    recv_sem=recv_sems.at[i, k - 1], device_id=(px, py, pc), device_id_type=pl.DeviceIdType.MESH)
                cp.start()
                landing = pltpu.make_async_remote_copy(
                    src_ref=src, dst_ref=outs[i].at[peer], send_sem=send_sems.at[i, k - 1],
                    recv_sem=recv_sems.at[i, k - 1], device_id=(px, py, pc), device_id_type=pl.DeviceIdType.MESH)
                sent.append((cp, landing))
        for cp in own:
            cp.wait()
        for cp, landing in sent:
            cp.wait_send()
            landing.wait_recv()

    out_shape = [jax.ShapeDtypeStruct((N_DEV,) + (a.shape if mode == "gather" else a.shape[1:]), a.dtype)
                 for a, mode in items]
    return _call(body, name=name, in_specs=[hbm] * n, out_specs=[hbm] * n, out_shape=out_shape,
                 scratch_shapes=[pltpu.SemaphoreType.DMA((n, N_DEV - 1)), pltpu.SemaphoreType.DMA((n, N_DEV - 1)),
                                 pltpu.SemaphoreType.DMA((n,))])(*[a for a, _ in items])


def _peer(k, x, y, c):
    px = 1 - x if k & 4 else x
    py = 1 - y if k & 2 else y
    pc = 1 - c if k & 1 else c
    return (px, py, pc), 4 * px + 2 * py + pc


_HBM_SPEC = pl.BlockSpec(memory_space=pltpu.HBM)
_SEM_SPEC = pl.BlockSpec(memory_space=pltpu.SEMAPHORE)
_SPLIT_EFFECT = pltpu.SideEffectType.DATAFLOW_SIDE_EFFECTING


def _exchange_begin(items, name):
    n = len(items)
    modes = [mode for _, mode in items]
    srcs = [pltpu.with_memory_space_constraint(a, pltpu.HBM) for a, _ in items]
    lands = [pltpu.with_memory_space_constraint(
        lax.empty((N_DEV,) + (a.shape if mode == "gather" else a.shape[1:]), a.dtype), pltpu.HBM)
        for a, mode in items]

    def body(*refs):
        src_refs, land_refs = refs[:n], refs[n:2 * n]
        send_sems, recv_sems = refs[2 * n], refs[2 * n + 1]
        token = refs[-1]
        x, y, c = lax.axis_index("x"), lax.axis_index("y"), lax.axis_index("c")
        me = 4 * x + 2 * y + c
        for i in range(n):
            for k in range(1, N_DEV):
                coords, peer = _peer(k, x, y, c)
                src = src_refs[i] if modes[i] == "gather" else src_refs[i].at[peer]
                pltpu.make_async_remote_copy(
                    src_ref=src, dst_ref=land_refs[i].at[me], send_sem=send_sems.at[i * (N_DEV - 1) + k - 1],
                    recv_sem=recv_sems.at[i * (N_DEV - 1) + k - 1], device_id=coords,
                    device_id_type=pl.DeviceIdType.MESH).start()
        token[...] = jnp.zeros_like(token)

    sems = pltpu.SemaphoreType.DMA((n * (N_DEV - 1),))
    res = _call(body, name=name,
                out_shape=(sems, sems, *[pltpu.HBM(a.shape, a.dtype) for a in srcs + lands],
                           jax.ShapeDtypeStruct((8, LANE), F32)),
                in_specs=[_HBM_SPEC] * (2 * n),
                out_specs=(_SEM_SPEC, _SEM_SPEC, *[_HBM_SPEC] * (2 * n), pl.BlockSpec(memory_space=pltpu.VMEM)),
                input_output_aliases={i: 2 + i for i in range(2 * n)},
                compiler_params=pltpu.CompilerParams(has_side_effects=_SPLIT_EFFECT))(*srcs, *lands)
    return (modes, res[0], res[1], list(res[2:2 + n]), list(res[2 + n:2 + 2 * n])), res[-1][0, 0]


def _exchange_end(handle, after, name):
    modes, send_sems, recv_sems, srcs, lands = handle
    n = len(modes)

    def wait_body(*refs):
        src_refs, land_refs = refs[:n], refs[n:2 * n]
        send, recv = refs[2 * n], refs[2 * n + 1]
        x, y, c = lax.axis_index("x"), lax.axis_index("y"), lax.axis_index("c")
        for i in range(n):
            for k in range(1, N_DEV):
                coords, peer = _peer(k, x, y, c)
                src = src_refs[i] if modes[i] == "gather" else src_refs[i].at[peer]
                cp = pltpu.make_async_remote_copy(
                    src_ref=src, dst_ref=land_refs[i].at[peer], send_sem=send.at[i * (N_DEV - 1) + k - 1],
                    recv_sem=recv.at[i * (N_DEV - 1) + k - 1], device_id=coords,
                    device_id_type=pl.DeviceIdType.MESH)
                cp.wait_send()
                cp.wait_recv()

    res = _call(wait_body, name=name, out_shape=[pltpu.HBM(a.shape, a.dtype) for a in srcs + lands],
                in_specs=[_HBM_SPEC] * (2 * n) + [_SEM_SPEC, _SEM_SPEC, pl.BlockSpec(memory_space=pl.ANY)],
                out_specs=[_HBM_SPEC] * (2 * n), input_output_aliases={i: i for i in range(2 * n)},
                compiler_params=pltpu.CompilerParams(has_side_effects=_SPLIT_EFFECT))(
                    *srcs, *lands, send_sems, recv_sems, after)
    srcs, lands = res[:n], res[n:]

    def own_body(*refs):
        src_refs, land_refs, sems = refs[:n], refs[2 * n:3 * n], refs[-1]
        me = 4 * lax.axis_index("x") + 2 * lax.axis_index("y") + lax.axis_index("c")
        copies = []
        for i in range(n):
            src = src_refs[i] if modes[i] == "gather" else src_refs[i].at[me]
            copies.append(pltpu.make_async_copy(src, land_refs[i].at[me], sems.at[i]))
            copies[-1].start()
        for cp in copies:
            cp.wait()

    any_spec = pl.BlockSpec(memory_space=pl.ANY)
    return _call(own_body, name=name + "_own", out_shape=[jax.ShapeDtypeStruct(a.shape, a.dtype) for a in lands],
                 in_specs=[any_spec] * (2 * n), out_specs=[any_spec] * n,
                 input_output_aliases={n + i: i for i in range(n)},
                 scratch_shapes=[pltpu.SemaphoreType.DMA((n,))])(*srcs, *lands)


def _cols_from_blocks(g):
    return jnp.transpose(g, (1, 0, 2)).reshape(g.shape[1], N_DEV * g.shape[2])


def _blocks_from_cols(w):
    r, c8 = w.shape
    return jnp.transpose(w.reshape(r, N_DEV, c8 // N_DEV), (1, 0, 2))


def _pack(arrs):
    flat = jnp.concatenate([a.reshape(-1).astype(F32) for a in arrs])
    pad = (-flat.shape[0]) % (8 * LANE)
    return jnp.pad(flat, (0, pad)).reshape(-1, LANE)


def _unpack(packed, shapes):
    flat = packed.reshape(-1)
    out, off = [], 0
    for s in shapes:
        size = math.prod(s)
        out.append(flat[off:off + size].reshape(s))
        off += size
    return out


def kernel(x, c, ctx, c_ctx, w_ada, b_ada, g_mix, w_in, s5_a_re, s5_a_im, s5_log_step, s5_b_re, s5_b_im, s5_c_re, s5_c_im, s5_d, s5_w_glu, s5_b_glu, sgu_ln_g, sgu_ln_b, sgu_w, sgu_b, w_proj_a, w_proj_b, b_gate, w_out, g_ffn, w_up, conv_w, conv_b, w_down, g_final, loss_target, m_c_ctx, m_w_ada, m_b_ada, m_g_mix, m_w_in, m_s5_a_re, m_s5_a_im, m_s5_log_step, m_s5_b_re, m_s5_b_im, m_s5_c_re, m_s5_c_im, m_s5_d, m_s5_w_glu, m_s5_b_glu, m_sgu_ln_g, m_sgu_ln_b, m_sgu_w, m_sgu_b, m_w_proj_a, m_w_proj_b, m_b_gate, m_w_out, m_g_ffn, m_w_up, m_conv_w, m_conv_b, m_w_down, m_g_final, v_c_ctx, v_w_ada, v_b_ada, v_g_mix, v_w_in, v_s5_a_re, v_s5_a_im, v_s5_log_step, v_s5_b_re, v_s5_b_im, v_s5_c_re, v_s5_c_im, v_s5_d, v_s5_w_glu, v_s5_b_glu, v_sgu_ln_g, v_sgu_ln_b, v_sgu_w, v_sgu_b, v_w_proj_a, v_w_proj_b, v_b_gate, v_w_out, v_g_ffn, v_w_up, v_conv_w, v_conv_b, v_w_down, v_g_final):
    given = dict(locals())
    wts = {n: given[n] for n in WEIGHTS}
    mom1 = {n: given["m_" + n] for n in WEIGHTS}
    mom2 = {n: given["v_" + n] for n in WEIGHTS}

    me = 4 * lax.axis_index("x") + 2 * lax.axis_index("y") + lax.axis_index("c")
    xs, cx, tgt = x[0], ctx[0], loss_target[0]
    n_tok, d = xs.shape
    n_ctx = cx.shape[0]
    s5w = s5_d.shape[1]
    ffn = w_down.shape[1] * N_DEV
    n_mod = w_ada.shape[2] * N_DEV // d
    mod_cols = w_ada.shape[2]

    def two_d(a):
        return a.reshape(-1, a.shape[-1])

    conv_w9 = conv_w[0].reshape(9, -1)
    gathered = _exchange(
        [(c, "gather"), (_b16(w_in[0]), "gather"), (_b16(s5_w_glu[0]), "gather"), (_b16(w_proj_a[0]), "gather"),
         (_b16(w_proj_b[0]), "gather")], "gather_weights")
    late_weights, tok = _exchange_begin(
        [(_b16(w_out[0]), "gather"), (_b16(w_up[0]), "gather"), (conv_w9, "gather"), (_b16(w_down[0]), "gather")],
        "gather_late_begin")
    c_all = gathered[0].reshape(N_DEV, d)
    w_in_f = _cols_from_blocks(gathered[1])
    w_in_u, w_in_rest = w_in_f[:, :s5w], w_in_f[:, s5w:]
    w_glu_f = gathered[2].reshape(-1, s5w)
    w_pa_f = _cols_from_blocks(gathered[3])
    w_pb_f = _cols_from_blocks(gathered[4])

    cs_in = jnp.concatenate([c_all, jnp.broadcast_to(c_ctx[None, :], (N_DEV, d))], axis=0) + tok
    w_ada_loc = w_ada[0]
    (mod_mine,) = _stage_fwd(_fn_mod, [cs_in], [w_ada_loc], [(mod_cols, F32)], blk=2 * N_DEV, name="mod_fwd")
    (mod_blocks,) = _exchange([(mod_mine, "gather")], "gather_mod")
    mod_all = _cols_from_blocks(mod_blocks) + b_ada
    mod = lax.dynamic_slice_in_dim(mod_all, me, 1, axis=0)
    mod_c = mod_all[N_DEV:N_DEV + 1]
    sh1, sc1, ga1, sh2, sc2, ga2 = [mod[:, i * d:(i + 1) * d] for i in range(n_mod)]
    sh1c, sc1c = mod_c[:, :d], mod_c[:, d:2 * d]

    a_par = [g_mix, sh1, sc1]
    ac_par = [g_mix, sh1c, sc1c]
    (h,) = _stage_fwd(_fn_a, [xs], a_par, [(d, BF16)], blk=_pick(n_tok, 512, 8), name="modulate1_fwd")
    (hc,) = _stage_fwd(_fn_a, [cx], ac_par, [(d, BF16)], blk=_pick(n_ctx, 512, 8), name="modulate1_ctx_fwd")
    pu = _mm(h, w_in_u, name="proj_u")
    prest = _mm(h, w_in_rest, name="proj_rest")
    puc = _mm(hc, w_in_u, name="proj_u_ctx")

    s5_prm = [s5_a_re[0][:, :, None, :], s5_a_im[0][:, :, None, :], s5_log_step[0][:, :, None, None],
              jnp.swapaxes(s5_b_re[0], 2, 3), jnp.swapaxes(s5_b_im[0], 2, 3), s5_c_re[0], s5_c_im[0]]
    pu_g, puc_g = _to_groups(pu), _to_groups(puc)
    ysc = _from_groups(_s5_fwd(pu_g, puc_g, s5_prm))

    b_par = [s5_d, w_glu_f, s5_b_glu, sgu_ln_g, sgu_ln_b, two_d(sgu_w[0]), jnp.transpose(sgu_b[0]),
             w_pa_f, w_pb_f, b_gate]
    b_rows = [pu, prest, ysc]
    b_blk = _pick(n_tok, 256, CHUNK)
    (mpre,) = _stage_fwd(_fn_b, b_rows, b_par, [(d, BF16)], blk=b_blk, name="mixers_fwd")
    late = _exchange_end(late_weights, mpre, "gather_late_end")
    w_out_f = late[0].reshape(-1, d)
    w_up_f = _cols_from_blocks(late[1])
    w_up_g, w_up_v = w_up_f[:, :ffn], w_up_f[:, ffn:]
    conv_w_f = _cols_from_blocks(late[2])
    w_down_f = late[3].reshape(-1, d)
    mo = _mm(mpre, w_out_f, name="out_proj")

    c_par = [ga1, g_ffn, sh2, sc2]
    c_blk = _pick(n_tok, 512, 8)
    x1, h2 = _stage_fwd(_fn_c, [xs, mo], c_par, [(d, F32), (d, BF16)], blk=c_blk, name="modulate2_fwd")
    up_g = _mm(h2, w_up_g, out_dtype=BF16, name="up_gate")
    up_v = _mm(h2, w_up_v, out_dtype=BF16, name="up_val")
    cw_g, cw_v = conv_w_f[:, :ffn], conv_w_f[:, ffn:]
    cb_g, cb_v = conv_b[:, :ffn], conv_b[:, ffn:]
    act, gate_c, val_c = _conv_act_fwd(up_g, up_v, cw_g, cw_v, cb_g, cb_v)
    dn = _mm(act, w_down_f, name="down_proj")

    loss_part, d_x1a, d_dn, d_ga2, d_g_final = _stage_loss(
        _fn_e, [x1, dn, tgt], [ga2, g_final[None, :]], blk=c_blk, name="loss_head",
        row_grads=[(0, F32), (1, BF16)])

    d_act = _mm(d_dn, w_down_f, tb=True, name="down_proj_dx")
    g_w_down = _mm(act, d_dn, ta=True, name="down_proj_dw")

    def grad_blocks(name, g):
        blocks = (_blocks_from_cols(g) if name in COL_SHARDED
                  else g.reshape(N_DEV, g.shape[0] // N_DEV, g.shape[1]))
        return _b16(blocks), "a2a"

    sent_down, tok = _exchange_begin([grad_blocks('w_down', g_w_down)], "grads_down_begin")
    dcg, dcv, g_cw_g, g_cw_v, g_cb_g, g_cb_v = _conv_act_bwd(up_g, up_v, gate_c, val_c, d_act)
    dug = _conv_plain(dcg, cw_g[::-1] + tok, "conv_dx_gate")
    duv = _conv_plain(dcv, cw_v[::-1], "conv_dx_val")
    d_h2 = _mm(dug, w_up_g, tb=True, name="up_gate_dx")
    d_h2 = _mm(duv, w_up_v, tb=True, add=d_h2, out_dtype=BF16, name="up_val_dx")
    g_w_up = jnp.concatenate([_mm(h2, dug, ta=True, name="up_gate_dw"), _mm(h2, duv, ta=True, name="up_val_dw")],
                             axis=1)
    sent_up, tok = _exchange_begin(
        [grad_blocks('w_up', g_w_up), grad_blocks('conv_w', jnp.concatenate([g_cw_g, g_cw_v], axis=1))],
        "grads_up_begin")
    (d_xc, d_mo), (d_ga1, g_g_ffn, d_sh2, d_sc2) = _split(_stage_bwd(
        _fn_c, [xs, mo], [ga1 + tok] + c_par[1:], [d_x1a, d_h2], blk=c_blk, name="modulate2_bwd",
        row_grads=[(0, F32), (1, BF16)]), 2)

    d_mpre = _mm(d_mo, w_out_f, tb=True, out_dtype=BF16, name="out_proj_dx")
    g_w_out = _mm(mpre, d_mo, ta=True, name="out_proj_dw")
    (d_prest, d_ysc), b_grads = _split(_stage_bwd(
        _fn_b, b_rows, b_par, [d_mpre], blk=b_blk, name="mixers_bwd", row_grads=[(1, BF16), (2, F32)]), 2)
    (g_s5_d, g_w_glu, g_b_glu, g_ln_g, g_ln_b, g_sgu_w, g_sgu_bt, g_w_pa, g_w_pb, g_b_gate) = b_grads

    sent_mix, tok = _exchange_begin(
        [grad_blocks('w_out', g_w_out), grad_blocks('s5_w_glu', g_w_glu), grad_blocks('w_proj_a', g_w_pa),
         grad_blocks('w_proj_b', g_w_pb)], "grads_mixer_begin")
    skip_g = jnp.tile(s5_d.reshape(-1, 1, S5_H), (1, 1, S5_T)) + tok
    s5_out = _s5_bwd(pu_g, puc_g, s5_prm, _to_groups(d_ysc), skip_g)
    d_pu, d_puc = _b16(_from_groups(s5_out[0])), _from_groups(s5_out[1])
    g_a_re, g_a_im, g_ls, g_bt_re, g_bt_im, g_c_re, g_c_im = s5_out[2:]

    d_h = _mm(d_pu, w_in_u, tb=True, name="proj_u_dx")
    d_h = _mm(d_prest, w_in_rest, tb=True, add=d_h, out_dtype=BF16, name="proj_rest_dx")
    d_hc = _mm(d_puc, w_in_u, tb=True, out_dtype=BF16, name="proj_u_ctx_dx")
    g_w_in_u = _mm(hc, d_puc, ta=True, name="proj_u_ctx_dw")
    g_w_in_u = _mm(h, d_pu, ta=True, add=g_w_in_u, name="proj_u_dw")
    g_w_in = jnp.concatenate([g_w_in_u, _mm(h, d_prest, ta=True, name="proj_rest_dw")], axis=1)

    (grad_x,), (g_g_mix_x, d_sh1, d_sc1) = _split(_stage_bwd(
        _fn_a_res, [xs], a_par, [d_h, d_xc], blk=c_blk, name="modulate1_bwd", row_grads=[(0, F32)]), 1)
    _, (g_g_mix_c, d_sh1c, d_sc1c) = _split(_stage_bwd(
        _fn_a, [cx], ac_par, [d_hc], blk=_pick(n_ctx, 512, 8), name="modulate1_ctx_bwd", row_grads=[]), 0)

    zeros = jnp.zeros((1, (n_mod - 2) * d), F32)
    d_mod = jnp.concatenate([d_sh1, d_sc1, d_ga1, d_sh2, d_sc2, d_ga2], axis=1)
    d_mod_c = jnp.concatenate([d_sh1c, d_sc1c, zeros], axis=1)
    (d_mod_all,) = _exchange([(jnp.concatenate([d_mod, d_mod_c], axis=0), "gather")], "gather_dmod")
    d_mod_rows = jnp.transpose(d_mod_all, (1, 0, 2)).reshape(2 * N_DEV, n_mod * d)
    d_mod_mine = lax.dynamic_slice_in_dim(d_mod_rows, me * mod_cols, mod_cols, axis=1)
    (d_cs,), (g_w_ada,) = _split(_stage_bwd(
        _fn_mod, [cs_in], [w_ada_loc], [d_mod_mine], blk=2 * N_DEV, name="mod_bwd", row_grads=[(0, F32)]), 1)

    part = {
        'c_ctx': jnp.sum(d_cs[N_DEV:], axis=0), 'b_ada': d_mod + d_mod_c, 'g_mix': g_g_mix_x + g_g_mix_c,
        's5_a_re': g_a_re, 's5_a_im': g_a_im, 's5_log_step': g_ls,
        's5_b_re': jnp.swapaxes(g_bt_re, 2, 3), 's5_b_im': jnp.swapaxes(g_bt_im, 2, 3),
        's5_c_re': g_c_re, 's5_c_im': g_c_im, 's5_d': g_s5_d, 's5_b_glu': g_b_glu, 'sgu_ln_g': g_ln_g,
        'sgu_ln_b': g_ln_b, 'sgu_w': g_sgu_w, 'sgu_b': jnp.transpose(g_sgu_bt), 'b_gate': g_b_gate,
        'g_ffn': g_g_ffn, 'conv_b': jnp.concatenate([g_cb_g, g_cb_v], axis=1), 'g_final': d_g_final,
    }
    rep_shapes = [wts[n].shape for n in REPLICATED]
    last = _exchange([(_pack([part[n] for n in REPLICATED]), "gather"), grad_blocks('w_in', g_w_in)],
                     "exchange_grads")
    summed = {'w_in': last[1]}
    (summed['w_down'],) = _exchange_end(sent_down, last[0], "grads_down_end")
    summed['w_up'], summed['conv_w'] = _exchange_end(sent_up, last[0], "grads_up_end")
    summed['w_out'], summed['s5_w_glu'], summed['w_proj_a'], summed['w_proj_b'] = _exchange_end(
        sent_mix, last[0], "grads_mixer_end")

    out = {}
    rep = _adamw(_pack([wts[n] for n in REPLICATED]), last[0], _pack([mom1[n] for n in REPLICATED]),
                 _pack([mom2[n] for n in REPLICATED]), "adamw_replicated")
    rep = [_unpack(r, rep_shapes) for r in rep]
    for i, n in enumerate(REPLICATED):
        out[n] = tuple(r[i] for r in rep)
    for n, parts in summed.items():
        shape = wts[n].shape
        res = _adamw(two_d(wts[n]), parts, two_d(mom1[n]), two_d(mom2[n]), "adamw_" + n)
        out[n] = tuple(r.reshape(shape) for r in res)
    res = _adamw(w_ada_loc, g_w_ada[None], m_w_ada[0], v_w_ada[0], "adamw_w_ada")
    out['w_ada'] = tuple(r.reshape(w_ada.shape) for r in res)

    loss = lax.psum(loss_part[0, 0], ("x", "y", "c"))
    return (loss, grad_x[None], *[out[n][0] for n in WEIGHTS], *[out[n][1] for n in WEIGHTS],
            *[out[n][2] for n in WEIGHTS], *[out[n][3] for n in WEIGHTS])


def _split(res, n_rows):
    return tuple(res[:n_rows]), tuple(res[n_rows:])
```

```python
import functools
import math

import jax
import jax.numpy as jnp
from jax import lax
from jax.experimental import pallas as pl
from jax.experimental.pallas import tpu as pltpu

F32 = jnp.float32
BF16 = jnp.bfloat16
HI = lax.Precision.HIGHEST

N_DEV = 8
GRID_W = 64
CHUNK = 128
EPS = 1e-6
S5_T = 32
S5_H = 16
LANE = 128
HALO = 128
VMEM_LIMIT = 56 * 1024 * 1024

ADAM_LR = 0.001
ADAM_B1 = 0.9
ADAM_B2 = 0.999
ADAM_EPS = 1e-08
ADAM_WD = 0.01
ADAM_STEP = 10

WEIGHTS = ['c_ctx', 'w_ada', 'b_ada', 'g_mix', 'w_in', 's5_a_re', 's5_a_im', 's5_log_step', 's5_b_re', 's5_b_im',
           's5_c_re', 's5_c_im', 's5_d', 's5_w_glu', 's5_b_glu', 'sgu_ln_g', 'sgu_ln_b', 'sgu_w', 'sgu_b',
           'w_proj_a', 'w_proj_b', 'b_gate', 'w_out', 'g_ffn', 'w_up', 'conv_w', 'conv_b', 'w_down', 'g_final']
COL_SHARDED = ('w_ada', 'w_in', 'w_proj_a', 'w_proj_b', 'w_up', 'conv_w')
ROW_SHARDED = ('s5_w_glu', 'w_out', 'w_down')
SHARDED = COL_SHARDED + ROW_SHARDED
REPLICATED = [n for n in WEIGHTS if n not in SHARDED]
LATE_REPLICATED = ['c_ctx', 'b_ada', 'g_mix']


def _call(body, **kw):
    return pl.pallas_call(body, **kw)


def _params(n_grid):
    return pltpu.CompilerParams(dimension_semantics=("arbitrary",) * n_grid, vmem_limit_bytes=VMEM_LIMIT)


def _pick(dim, pref, unit=LANE):
    best = None
    d = unit
    while d <= min(dim, pref):
        if dim % d == 0:
            best = d
        d += unit
    return best if best is not None else dim


def _dg(a, b, ca, cb, prec=None):
    return lax.dot_general(a, b, (((ca,), (cb,)), ((), ())), precision=prec, preferred_element_type=F32)


def _b16(v):
    return v.astype(BF16)


@jax.custom_vjp
def mmb(a, b):
    return _dg(_b16(a), _b16(b), 1, 0)


def _mmb_fwd(a, b):
    return mmb(a, b), (a, b)


def _mmb_bwd(res, g):
    a, b = res
    g = _b16(g)
    return _dg(g, _b16(b), 1, 1).astype(a.dtype), _dg(_b16(a), g, 0, 0).astype(b.dtype)


mmb.defvjp(_mmb_fwd, _mmb_bwd)


@jax.custom_vjp
def mmb_nt(a, b):
    return _dg(_b16(a), _b16(b), 1, 1)


def _mmb_nt_fwd(a, b):
    return mmb_nt(a, b), (a, b)


def _mmb_nt_bwd(res, g):
    a, b = res
    g = _b16(g)
    return _dg(g, _b16(b), 1, 0).astype(a.dtype), _dg(g, _b16(a), 0, 0).astype(b.dtype)


mmb_nt.defvjp(_mmb_nt_fwd, _mmb_nt_bwd)


@jax.custom_vjp
def mmf(a, b):
    return _dg(a, b, 1, 0, HI)


def _mmf_fwd(a, b):
    return mmf(a, b), (a, b)


def _mmf_bwd(res, g):
    a, b = res
    return _dg(g, b, 1, 1, HI), _dg(a, g, 0, 0, HI)


mmf.defvjp(_mmf_fwd, _mmf_bwd)


def _dg3(a, b, ca, cb):
    ah, bh = _b16(a), _b16(b)
    al, bl = _b16(a - ah.astype(F32)), _b16(b - bh.astype(F32))
    return _dg(ah, bh, ca, cb) + _dg(ah, bl, ca, cb) + _dg(al, bh, ca, cb)


@jax.custom_vjp
def mm3_nt(a, b):
    return _dg3(a, b, 1, 1)


def _mm3_nt_fwd(a, b):
    return mm3_nt(a, b), (a, b)


def _mm3_nt_bwd(res, g):
    a, b = res
    return _dg3(g, b, 1, 0), _dg3(g, a, 0, 0)


mm3_nt.defvjp(_mm3_nt_fwd, _mm3_nt_bwd)


def _shift_impl(x, k, up):
    n = x.shape[0]
    idx = lax.broadcasted_iota(jnp.int32, (n, 1), 0)
    if up:
        return jnp.where(idx < n - k, pltpu.roll(x, n - k, 0), 0.0)
    return jnp.where(idx >= k, pltpu.roll(x, k, 0), 0.0)


@functools.partial(jax.custom_vjp, nondiff_argnums=(1, 2))
def _shift(x, k, up):
    return _shift_impl(x, k, up)


def _shift_fwd(x, k, up):
    return _shift_impl(x, k, up), None


def _shift_bwd(k, up, _, g):
    return (_shift_impl(g, k, not up),)


_shift.defvjp(_shift_fwd, _shift_bwd)


def _mm(a, b, *, name, ta=False, tb=False, add=None, out_dtype=F32, tm_pref=1408, tn_pref=1408, tk_pref=1408):
    m, k = (a.shape[1], a.shape[0]) if ta else a.shape
    n = b.shape[0] if tb else b.shape[1]
    tm, tn, tk = _pick(m, tm_pref), _pick(n, tn_pref), _pick(k, tk_pref)
    nk = k // tk
    a_spec = (pl.BlockSpec((tk, tm), lambda i, j, kk: (kk, i)) if ta
              else pl.BlockSpec((tm, tk), lambda i, j, kk: (i, kk)))
    b_spec = (pl.BlockSpec((tn, tk), lambda i, j, kk: (j, kk)) if tb
              else pl.BlockSpec((tk, tn), lambda i, j, kk: (kk, j)))
    o_spec = pl.BlockSpec((tm, tn), lambda i, j, kk: (i, j))
    ca, cb = (0 if ta else 1), (1 if tb else 0)
    has_add = add is not None
    in_place = out_dtype == F32 or nk == 1

    def body(*refs):
        a_ref, b_ref = refs[0], refs[1]
        o_ref = refs[3] if has_add else refs[2]
        prod = _dg(_b16(a_ref[...]), _b16(b_ref[...]), ca, cb)
        if nk == 1:
            if has_add:
                prod = prod + refs[2][...].astype(F32)
            o_ref[...] = prod.astype(o_ref.dtype)
            return
        acc_ref = o_ref if in_place else refs[-1]
        kk = pl.program_id(2)

        @pl.when(kk == 0)
        def _():
            acc_ref[...] = prod

        @pl.when(kk > 0)
        def _():
            acc_ref[...] += prod

        if has_add or not in_place:
            @pl.when(kk == nk - 1)
            def _():
                r = acc_ref[...]
                if has_add:
                    r = r + refs[2][...].astype(F32)
                o_ref[...] = r.astype(o_ref.dtype)

    ins = [a, b] + ([add] if has_add else [])
    in_specs = [a_spec, b_spec] + ([o_spec] if has_add else [])
    return _call(body, name=name, grid=(m // tm, n // tn, nk), in_specs=in_specs, out_specs=o_spec,
                 out_shape=jax.ShapeDtypeStruct((m, n), out_dtype),
                 scratch_shapes=[] if in_place else [pltpu.VMEM((tm, tn), F32)],
                 compiler_params=_params(3))(*ins)


def _row_spec(blk, width):
    return pl.BlockSpec((blk, width), lambda i: (i, 0))


def _whole_spec(shape):
    return pl.BlockSpec(shape, lambda i: (0,) * len(shape))


def _stage_fwd(fn, rows, params, outs, *, blk, name, n_rows=None):
    n = n_rows or rows[0].shape[0]
    nr, npar = len(rows), len(params)

    def body(*refs):
        vals = [r[...] for r in refs[:nr + npar]]
        res = fn(*vals)
        for o_ref, v in zip(refs[nr + npar:], res):
            o_ref[...] = v.astype(o_ref.dtype)

    return _call(body, name=name, grid=(n // blk,),
                 in_specs=[_row_spec(blk, r.shape[1]) for r in rows] + [_whole_spec(p.shape) for p in params],
                 out_specs=[_row_spec(blk, w) for w, _ in outs],
                 out_shape=[jax.ShapeDtypeStruct((n, w), dt) for w, dt in outs],
                 compiler_params=_params(1))(*rows, *params)


def _stage_bwd(fn, rows, params, cts, *, blk, name, row_grads, n_rows=None):
    n = n_rows or rows[0].shape[0]
    nr, npar, nct = len(rows), len(params), len(cts)

    def body(*refs):
        vals = [r[...].astype(F32) for r in refs[:nr + npar]]
        ct = [r[...] for r in refs[nr + npar:nr + npar + nct]]
        d_rows = refs[nr + npar + nct:nr + npar + nct + len(row_grads)]
        d_par = refs[nr + npar + nct + len(row_grads):]
        res, vjp = jax.vjp(fn, *vals)
        g = vjp(tuple(c.astype(r.dtype) for c, r in zip(ct, res)))
        for o_ref, (j, _) in zip(d_rows, row_grads):
            o_ref[...] = g[j].astype(o_ref.dtype)

        @pl.when(pl.program_id(0) == 0)
        def _():
            for o_ref in d_par:
                o_ref[...] = jnp.zeros_like(o_ref)

        for j, o_ref in enumerate(d_par):
            o_ref[...] += g[nr + j].astype(F32)

    return _call(body, name=name, grid=(n // blk,),
                 in_specs=([_row_spec(blk, r.shape[1]) for r in rows] + [_whole_spec(p.shape) for p in params]
                           + [_row_spec(blk, c.shape[1]) for c in cts]),
                 out_specs=([_row_spec(blk, rows[j].shape[1]) for j, _ in row_grads]
                            + [_whole_spec(p.shape) for p in params]),
                 out_shape=([jax.ShapeDtypeStruct((n, rows[j].shape[1]), dt) for j, dt in row_grads]
                            + [jax.ShapeDtypeStruct(p.shape, F32) for p in params]),
                 compiler_params=_params(1))(*rows, *params, *cts)


def _stage_loss(fn, rows, params, *, blk, name, row_grads):
    n = rows[0].shape[0]
    nr, npar = len(rows), len(params)

    def body(*refs):
        vals = [r[...].astype(F32) for r in refs[:nr + npar]]
        loss_ref = refs[nr + npar]
        d_rows = refs[nr + npar + 1:nr + npar + 1 + len(row_grads)]
        d_par = refs[nr + npar + 1 + len(row_grads):]
        res, vjp = jax.vjp(fn, *vals)
        g = vjp(jnp.ones_like(res))
        for o_ref, (j, _) in zip(d_rows, row_grads):
            o_ref[...] = g[j].astype(o_ref.dtype)

        @pl.when(pl.program_id(0) == 0)
        def _():
            loss_ref[...] = jnp.zeros_like(loss_ref)
            for o_ref in d_par:
                o_ref[...] = jnp.zeros_like(o_ref)

        loss_ref[...] += res
        for j, o_ref in enumerate(d_par):
            o_ref[...] += g[nr + j].astype(F32)

    return _call(body, name=name, grid=(n // blk,),
                 in_specs=[_row_spec(blk, r.shape[1]) for r in rows] + [_whole_spec(p.shape) for p in params],
                 out_specs=([_whole_spec((1, 1))] + [_row_spec(blk, rows[j].shape[1]) for j, _ in row_grads]
                            + [_whole_spec(p.shape) for p in params]),
                 out_shape=([jax.ShapeDtypeStruct((1, 1), F32)]
                            + [jax.ShapeDtypeStruct((n, rows[j].shape[1]), dt) for j, dt in row_grads]
                            + [jax.ShapeDtypeStruct(p.shape, F32) for p in params]),
                 compiler_params=_params(1))(*rows, *params)


def _rms(x, g):
    return x * lax.rsqrt(jnp.mean(x * x, axis=-1, keepdims=True) + EPS) * g


def _modulate(x, g, shift, scale):
    return _rms(x, g) * (1.0 + scale) + shift


def _fn_mod(cs, w_ada):
    return (mmb(jax.nn.silu(cs), w_ada),)


def _fn_a(x, g_mix, sh, sc):
    return (_b16(_modulate(x, g_mix, sh, sc)),)


def _fn_a_res(x, g_mix, sh, sc):
    return _b16(_modulate(x, g_mix, sh, sc)), x


def _sgu_spatial(v, sgu_w, sgu_bt):
    rows, width = v.shape
    gdim = width // (sgu_w.shape[0] // CHUNK)
    groups = width // gdim
    expand = (lax.broadcasted_iota(jnp.int32, (groups, width), 1) // gdim
              == lax.broadcasted_iota(jnp.int32, (groups, width), 0)).astype(F32)
    bias = mmf(sgu_bt, expand)
    lane = lax.broadcasted_iota(jnp.int32, (CHUNK, LANE), 1)
    per_lane_block = LANE // gdim
    chunks = []
    for ci in range(rows // CHUNK):
        vc = v[ci * CHUNK:(ci + 1) * CHUNK]
        blocks = []
        for lb in range(width // LANE):
            vb = vc[:, lb * LANE:(lb + 1) * LANE]
            acc = None
            for s in range(per_lane_block):
                g = lb * per_lane_block + s
                r = mmb(sgu_w[g * CHUNK:(g + 1) * CHUNK], vb)
                sel = (lane // gdim) == s
                acc = jnp.where(sel, r, 0.0) if acc is None else jnp.where(sel, r, acc)
            blocks.append(acc)
        chunks.append(jnp.concatenate(blocks, axis=1) + bias)
    return jnp.concatenate(chunks, axis=0)


def _fn_b(pu, prest, ysc, s5_d, w_glu, b_glu, ln_g, ln_b, sgu_w, sgu_bt, w_pa, w_pb, b_gate):
    sw = ln_g.shape[1]
    y = jax.nn.gelu(pu * s5_d + ysc)
    ya = y * jax.nn.sigmoid(mmb(y, w_glu) + b_glu)
    z = jax.nn.gelu(prest[:, :2 * sw])
    u, v = z[:, :sw], z[:, sw:]
    vc = v - jnp.mean(v, axis=-1, keepdims=True)
    v = vc * lax.rsqrt(jnp.mean(vc * vc, axis=-1, keepdims=True) + EPS) * ln_g + ln_b
    yb = u * _sgu_spatial(v, sgu_w, sgu_bt)
    gates = jax.nn.sigmoid(prest[:, 2 * sw:] + b_gate)
    d = gates.shape[1] // 2
    return (_b16(gates[:, :d] * mmb(ya, w_pa) + gates[:, d:] * mmb(yb, w_pb)),)


def _fn_c(x, mo, ga1, g_ffn, sh2, sc2):
    x1 = x + ga1 * mo
    return x1, _b16(_modulate(x1, g_ffn, sh2, sc2))


def _fn_e(x1, dn, tgt, ga2, g_final):
    y = _rms(x1 + ga2 * dn, g_final)
    err = (y - tgt) ** 2
    return 0.5 * jnp.sum(jnp.mean(err, axis=-1, keepdims=True), axis=0, keepdims=True)


def _s5_direction(u, mask, a_re, a_im, log_step, bt_re, bt_im, c_re, c_im, rev):
    nc, width = u.shape
    t_len = width // S5_H
    n = a_re.shape[1]
    dt = jnp.exp(log_step)
    lr, li = a_re * dt, a_im * dt
    mag = jnp.exp(lr)
    ab_re, ab_im = mag * jnp.cos(li), mag * jnp.sin(li)
    p, q = ab_re - 1.0, ab_im
    den = a_re * a_re + a_im * a_im
    k_re, k_im = (p * a_re + q * a_im) / den, (q * a_re - p * a_im) / den
    bb_re = k_re * bt_re - k_im * bt_im
    bb_im = k_re * bt_im + k_im * bt_re

    def power(e):
        m = jnp.exp(lr * e)
        return m * jnp.cos(li * e), m * jnp.sin(li * e)

    def cmul(xr, xi, yr, yi):
        return xr * yr - xi * yi, xr * yi + xi * yr

    order = range(t_len - 1, -1, -1) if rev else range(t_len)
    e1 = jnp.concatenate([jnp.full((1, 1, n), float(t_len - 1 - pos), F32) for pos in order], axis=0)
    lr3, li3 = lr.reshape(1, 1, n), li.reshape(1, 1, n)
    m1, c1, s1 = jnp.exp(lr3 * e1), jnp.cos(li3 * e1), jnp.sin(li3 * e1)
    m2 = jnp.exp(-(lr3 * e1))
    x1 = (m1 * c1, m1 * s1)
    x2 = (m2 * c1, -(m2 * s1))
    x3 = cmul(*[v.reshape(1, 1, n) for v in power(float(t_len))], *x2)

    def rows(xr, xi, yr, yi):
        zr, zi = cmul(xr, xi, yr.reshape(1, S5_H, n), yi.reshape(1, S5_H, n))
        return zr.reshape(width, n), zi.reshape(width, n)

    pr, pi = rows(*x1, bb_re, bb_im)
    rr, ri = rows(*x2, c_re, c_im)
    wr, wi = rows(*x3, c_re, c_im)
    toep = (mm3_nt(pr, rr) - mm3_nt(pi, ri)) * mask

    sr, si = mmb(u, pr), mmb(u, pi)
    k = 1
    while k < nc:
        ar, ai = power(float(t_len * k))
        hr, hi = _shift(sr, k, rev), _shift(si, k, rev)
        sr, si = sr + ar * hr - ai * hi, si + ar * hi + ai * hr
        k *= 2
    cr_in, ci_in = _shift(sr, 1, rev), _shift(si, 1, rev)
    return mmb(u, toep) + mmb_nt(cr_in, wr) - mmb_nt(ci_in, wi)


def _fn_s5(masks, x_chunks, ctx_chunks, *prm):
    nx, nctx = x_chunks.shape[0], ctx_chunks.shape[0]
    pad = (-(nx + 2 * nctx)) % LANE
    u = jnp.concatenate([ctx_chunks, x_chunks, ctx_chunks, jnp.zeros((pad, x_chunks.shape[1]), F32)], axis=0)
    out = None
    for d in range(2):
        y = _s5_direction(u, masks[d], *[p[d] for p in prm], rev=(d == 1))
        out = y if out is None else out + y
    return out[nctx:nctx + nx]


def _s5_specs(prm):
    return [pl.BlockSpec((2, 1) + p.shape[2:], lambda g: (0, g, 0, 0)) for p in prm]


def _group_spec(a):
    return pl.BlockSpec((1,) + a.shape[1:], lambda i: (i, 0, 0))


def _s5_masks(width):
    pos = jnp.arange(width) // S5_H
    causal = (pos[None, :] >= pos[:, None]).astype(F32)
    return jnp.stack([causal, causal.T])


def _s5_fwd(x_g, ctx_g, prm):
    masks = _s5_masks(x_g.shape[2])

    def body(*refs):
        pv = [r[:, 0] for r in refs[3:3 + len(prm)]]
        refs[-1][0] = _fn_s5(refs[0][...], refs[1][0], refs[2][0], *pv)

    return _call(body, name="s5_fwd", grid=(x_g.shape[0],),
                 in_specs=[_whole_spec(masks.shape), _group_spec(x_g), _group_spec(ctx_g)] + _s5_specs(prm),
                 out_specs=_group_spec(x_g), out_shape=jax.ShapeDtypeStruct(x_g.shape, F32),
                 compiler_params=_params(1))(masks, x_g, ctx_g, *prm)


def _s5_bwd(x_g, ctx_g, prm, dy_g, skip_g):
    npar = len(prm)
    masks = _s5_masks(x_g.shape[2])

    def body(*refs):
        pv = [r[:, 0] for r in refs[3:3 + npar]]
        dy = refs[3 + npar][0]
        _, vjp = jax.vjp(functools.partial(_fn_s5, refs[0][...]), refs[1][0], refs[2][0], *pv)
        grads = vjp(dy)
        outs = refs[5 + npar:]
        outs[0][0] = (grads[0] + dy * refs[4 + npar][0]).astype(outs[0].dtype)
        outs[1][0] = grads[1]
        for o_ref, gv in zip(outs[2:], grads[2:]):
            o_ref[:, 0] = gv

    return _call(body, name="s5_bwd", grid=(x_g.shape[0],),
                 in_specs=([_whole_spec(masks.shape), _group_spec(x_g), _group_spec(ctx_g)] + _s5_specs(prm)
                           + [_group_spec(dy_g), _group_spec(skip_g)]),
                 out_specs=[_group_spec(x_g), _group_spec(ctx_g)] + _s5_specs(prm),
                 out_shape=[jax.ShapeDtypeStruct(x_g.shape, F32), jax.ShapeDtypeStruct(ctx_g.shape, F32)]
                 + [jax.ShapeDtypeStruct(p.shape, F32) for p in prm],
                 compiler_params=_params(1))(masks, x_g, ctx_g, *prm, dy_g, skip_g)


def _to_groups(tok):
    n, width = tok.shape
    g = width // S5_H
    return jnp.transpose(tok.reshape(n, g, S5_H), (1, 0, 2)).reshape(g, n // S5_T, S5_T * S5_H)


def _from_groups(grp):
    g, nc, _ = grp.shape
    return jnp.transpose(grp.reshape(g, nc * S5_T, S5_H), (1, 0, 2)).reshape(nc * S5_T, g * S5_H)


def _conv_taps(xp, xm, xn, blk_i, n_blk):
    tb = xm.shape[0]
    xp = jnp.where(blk_i == 0, 0.0, xp.astype(F32))
    xn = jnp.where(blk_i == n_blk - 1, 0.0, xn.astype(F32))
    buf = jnp.concatenate([xp, xm.astype(F32), xn], axis=0)
    n = tb + 2 * HALO
    col = lax.broadcasted_iota(jnp.int32, (n, 1), 0) % GRID_W
    left = jnp.where(col >= 1, pltpu.roll(buf, 1, 0), 0.0)
    right = jnp.where(col <= GRID_W - 2, pltpu.roll(buf, n - 1, 0), 0.0)
    shifted = (left, buf, right)
    taps = []
    for di in range(3):
        start = HALO + (di - 1) * GRID_W
        for dj in range(3):
            taps.append(shifted[dj][start:start + tb])
    return taps


def _conv_sum(taps, w_ref):
    acc = None
    for k, tap in enumerate(taps):
        term = tap * w_ref[k:k + 1, :]
        acc = term if acc is None else acc + term
    return acc


def _conv_geometry(n_tok, width):
    tb = _pick(n_tok, 1024, HALO)
    cb = _pick(width, 256)
    nb = tb // HALO
    last = n_tok // HALO - 1
    main = pl.BlockSpec((tb, cb), lambda j, i: (i, j))
    prev = pl.BlockSpec((HALO, cb), lambda j, i: (jnp.maximum(i * nb - 1, 0), j))
    nxt = pl.BlockSpec((HALO, cb), lambda j, i: (jnp.minimum(i * nb + nb, last), j))
    par = lambda r: pl.BlockSpec((r, cb), lambda j, i: (0, j))
    return tb, cb, main, prev, nxt, par


def _conv_act_fwd(up_g, up_v, w_g, w_v, b_g, b_v):
    n_tok, width = up_g.shape
    tb, cb, main, prev, nxt, par = _conv_geometry(n_tok, width)
    n_blk = n_tok // tb

    def body(gp, gm, gn, vp, vm, vn, wg, wv, bg, bv, o_ref, gate_ref, val_ref):
        i = pl.program_id(1)
        gate = _conv_sum(_conv_taps(gp[...], gm[...], gn[...], i, n_blk), wg) + bg[...]
        val = _conv_sum(_conv_taps(vp[...], vm[...], vn[...], i, n_blk), wv) + bv[...]
        o_ref[...] = (jax.nn.silu(gate) * val).astype(o_ref.dtype)
        gate_ref[...] = gate
        val_ref[...] = val

    shp = jax.ShapeDtypeStruct
    return _call(body, name="conv_act_fwd", grid=(width // cb, n_tok // tb),
                 in_specs=[prev, main, nxt, prev, main, nxt, par(9), par(9), par(1), par(1)],
                 out_specs=[main, main, main],
                 out_shape=[shp((n_tok, width), BF16), shp((n_tok, width), F32), shp((n_tok, width), F32)],
                 compiler_params=_params(2))(up_g, up_g, up_g, up_v, up_v, up_v, w_g, w_v, b_g, b_v)


def _conv_act_bwd(up_g, up_v, gate_c, val_c, d_act):
    n_tok, width = up_g.shape
    tb, cb, main, prev, nxt, par = _conv_geometry(n_tok, width)
    n_blk = n_tok // tb

    def body(gp, gm, gn, vp, vm, vn, gc, vc, da, dcg, dcv, dwg, dwv, dbg, dbv):
        i = pl.program_id(1)
        taps_g = _conv_taps(gp[...], gm[...], gn[...], i, n_blk)
        taps_v = _conv_taps(vp[...], vm[...], vn[...], i, n_blk)
        gate, val = gc[...], vc[...]
        sig = jax.nn.sigmoid(gate)
        d = da[...].astype(F32)
        d_gate = d * val * sig * (1.0 + gate * (1.0 - sig))
        d_val = d * gate * sig
        dcg[...] = d_gate.astype(dcg.dtype)
        dcv[...] = d_val.astype(dcv.dtype)

        @pl.when(i == 0)
        def _():
            for r in (dwg, dwv, dbg, dbv):
                r[...] = jnp.zeros_like(r)

        dbg[...] += jnp.sum(d_gate, axis=0, keepdims=True)
        dbv[...] += jnp.sum(d_val, axis=0, keepdims=True)
        for k in range(9):
            dwg[k:k + 1, :] += jnp.sum(taps_g[k] * d_gate, axis=0, keepdims=True)
            dwv[k:k + 1, :] += jnp.sum(taps_v[k] * d_val, axis=0, keepdims=True)

    shp = jax.ShapeDtypeStruct
    return _call(body, name="conv_act_bwd", grid=(width // cb, n_tok // tb),
                 in_specs=[prev, main, nxt, prev, main, nxt, main, main, main],
                 out_specs=[main, main, par(9), par(9), par(1), par(1)],
                 out_shape=[shp((n_tok, width), BF16), shp((n_tok, width), BF16), shp((9, width), F32),
                            shp((9, width), F32), shp((1, width), F32), shp((1, width), F32)],
                 compiler_params=_params(2))(up_g, up_g, up_g, up_v, up_v, up_v, gate_c, val_c, d_act)


def _conv_plain(x, w, name):
    n_tok, width = x.shape
    tb, cb, main, prev, nxt, par = _conv_geometry(n_tok, width)
    n_blk = n_tok // tb

    def body(xp, xm, xn, w_ref, o_ref):
        taps = _conv_taps(xp[...], xm[...], xn[...], pl.program_id(1), n_blk)
        o_ref[...] = _conv_sum(taps, w_ref).astype(o_ref.dtype)

    return _call(body, name=name, grid=(width // cb, n_tok // tb), in_specs=[prev, main, nxt, par(9)],
                 out_specs=main, out_shape=jax.ShapeDtypeStruct((n_tok, width), BF16),
                 compiler_params=_params(2))(x, x, x, w)


def _adamw(w, g_parts, m, v, name, own=None, me=None):
    rows, cols = w.shape
    parts = g_parts.shape[0]
    blk = _pick(rows, 256, 8)
    spec = pl.BlockSpec((blk, cols), lambda i: (i, 0))
    has_own = own is not None

    def body(*refs):
        w_ref, g_ref, m_ref, v_ref = refs[:4]
        g_out, d_out, m_out, v_out = refs[-4:]

        def part(p):
            if has_own:
                return jnp.where(refs[5][...] == p, refs[4][...], g_ref[p]).astype(F32)
            return g_ref[p].astype(F32)

        g = part(0)
        for p in range(1, parts):
            g = g + part(p)
        m_new = ADAM_B1 * m_ref[...] + (1.0 - ADAM_B1) * g
        v_new = ADAM_B2 * v_ref[...] + (1.0 - ADAM_B2) * (g * g)
        m_hat = m_new / (1.0 - ADAM_B1 ** ADAM_STEP)
        v_hat = v_new / (1.0 - ADAM_B2 ** ADAM_STEP)
        g_out[...] = g
        d_out[...] = -ADAM_LR * (m_hat / (jnp.sqrt(v_hat) + ADAM_EPS) + ADAM_WD * w_ref[...])
        m_out[...] = m_new
        v_out[...] = v_new

    extra = [own, me] if has_own else []
    return _call(body, name=name, grid=(rows // blk,),
                 in_specs=([spec, pl.BlockSpec((parts, blk, cols), lambda i: (0, i, 0)), spec, spec]
                           + ([spec, _whole_spec((1, 1))] if has_own else [])),
                 out_specs=[spec] * 4, out_shape=[jax.ShapeDtypeStruct((rows, cols), F32)] * 4,
                 compiler_params=_params(1))(w, g_parts, m, v, *extra)


def _exchange(items, name):
    n = len(items)
    hbm = pl.BlockSpec(memory_space=pl.ANY)

    def body(*refs):
        srcs, outs = refs[:n], refs[n:2 * n]
        send_sems, recv_sems, own_sems = refs[2 * n:]
        x, y, c = lax.axis_index("x"), lax.axis_index("y"), lax.axis_index("c")
        me = 4 * x + 2 * y + c
        own = []
        for i, (_, mode) in enumerate(items):
            src = srcs[i] if mode == "gather" else srcs[i].at[me]
            cp = pltpu.make_async_copy(src, outs[i].at[me], own_sems.at[i])
            cp.start()
            own.append(cp)
        sent = []
        for i, (_, mode) in enumerate(items):
            for k in range(1, N_DEV):
                px = 1 - x if k & 4 else x
                py = 1 - y if k & 2 else y
                pc = 1 - c if k & 1 else c
                peer = 4 * px + 2 * py + pc
                src = srcs[i] if mode == "gather" else srcs[i].at[peer]
                cp = pltpu.make_async_remote_copy(
                    src_ref=src, dst_ref=outs[i].at[me], send_sem=send_sems.at[i, k - 1],
                    recv_sem=recv_sems.at[i, k - 1], device_id=(px, py, pc), device_id_type=pl.DeviceIdType.MESH)
                cp.start()
                landing = pltpu.make_async_remote_copy(
                    src_ref=src, dst_ref=outs[i].at[peer], send_sem=send_sems.at[i, k - 1],
                    recv_sem=recv_sems.at[i, k - 1], device_id=(px, py, pc), device_id_type=pl.DeviceIdType.MESH)
                sent.append((cp, landing))
        for cp in own:
            cp.wait()
        for cp, landing in sent:
            cp.wait_send()
            landing.wait_recv()

    out_shape = [jax.ShapeDtypeStruct((N_DEV,) + (a.shape if mode == "gather" else a.shape[1:]), a.dtype)
                 for a, mode in items]
    return _call(body, name=name, in_specs=[hbm] * n, out_specs=[hbm] * n, out_shape=out_shape,
                 scratch_shapes=[pltpu.SemaphoreType.DMA((n, N_DEV - 1)), pltpu.SemaphoreType.DMA((n, N_DEV - 1)),
                                 pltpu.SemaphoreType.DMA((n,))])(*[a for a, _ in items])


def _peer(k, x, y, c):
    px = 1 - x if k & 4 else x
    py = 1 - y if k & 2 else y
    pc = 1 - c if k & 1 else c
    return (px, py, pc), 4 * px + 2 * py + pc


_HBM_SPEC = pl.BlockSpec(memory_space=pltpu.HBM)
_SEM_SPEC = pl.BlockSpec(memory_space=pltpu.SEMAPHORE)
_SPLIT_EFFECT = pltpu.SideEffectType.DATAFLOW_SIDE_EFFECTING


def _exchange_begin(items, name, after=None):
    n = len(items)
    modes = [mode for _, mode in items]
    srcs = [pltpu.with_memory_space_constraint(a, pltpu.HBM) for a, _ in items]
    lands = [pltpu.with_memory_space_constraint(
        lax.empty((N_DEV,) + (a.shape if mode == "gather" else a.shape[1:]), a.dtype), pltpu.HBM)
        for a, mode in items]

    def body(*refs):
        src_refs, land_refs = refs[:n], refs[n:2 * n]
        token = refs[-1]
        send_sems, recv_sems = refs[-2 * n - 3], refs[-2 * n - 2]
        x, y, c = lax.axis_index("x"), lax.axis_index("y"), lax.axis_index("c")
        me = 4 * x + 2 * y + c
        for i in range(n):
            for k in range(1, N_DEV):
                coords, peer = _peer(k, x, y, c)
                src = src_refs[i] if modes[i] == "gather" else src_refs[i].at[peer]
                pltpu.make_async_remote_copy(
                    src_ref=src, dst_ref=land_refs[i].at[me], send_sem=send_sems.at[i * (N_DEV - 1) + k - 1],
                    recv_sem=recv_sems.at[i * (N_DEV - 1) + k - 1], device_id=coords,
                    device_id_type=pl.DeviceIdType.MESH).start()
        token[...] = jnp.zeros_like(token)

    sems = pltpu.SemaphoreType.DMA((n * (N_DEV - 1),))
    order = [] if after is None else [after]
    res = _call(body, name=name,
                out_shape=(sems, sems, *[pltpu.HBM(a.shape, a.dtype) for a in srcs + lands],
                           jax.ShapeDtypeStruct((8, LANE), F32)),
                in_specs=[_HBM_SPEC] * (2 * n) + [pl.BlockSpec(memory_space=pl.ANY)] * len(order),
                out_specs=(_SEM_SPEC, _SEM_SPEC, *[_HBM_SPEC] * (2 * n), pl.BlockSpec(memory_space=pltpu.VMEM)),
                input_output_aliases={i: 2 + i for i in range(2 * n)},
                compiler_params=pltpu.CompilerParams(has_side_effects=_SPLIT_EFFECT))(*srcs, *lands, *order)
    return (modes, res[0], res[1], list(res[2:2 + n]), list(res[2 + n:2 + 2 * n])), res[-1][0, 0]


def _exchange_end(handle, after, name):
    modes, send_sems, recv_sems, srcs, lands = handle
    n = len(modes)

    def wait_body(*refs):
        src_refs, land_refs = refs[:n], refs[n:2 * n]
        send, recv = refs[2 * n], refs[2 * n + 1]
        x, y, c = lax.axis_index("x"), lax.axis_index("y"), lax.axis_index("c")
        for i in range(n):
            for k in range(1, N_DEV):
                coords, peer = _peer(k, x, y, c)
                src = src_refs[i] if modes[i] == "gather" else src_refs[i].at[peer]
                cp = pltpu.make_async_remote_copy(
                    src_ref=src, dst_ref=land_refs[i].at[peer], send_sem=send.at[i * (N_DEV - 1) + k - 1],
                    recv_sem=recv.at[i * (N_DEV - 1) + k - 1], device_id=coords,
                    device_id_type=pl.DeviceIdType.MESH)
                cp.wait_send()
                cp.wait_recv()

    res = _call(wait_body, name=name, out_shape=[pltpu.HBM(a.shape, a.dtype) for a in srcs + lands],
                in_specs=[_HBM_SPEC] * (2 * n) + [_SEM_SPEC, _SEM_SPEC, pl.BlockSpec(memory_space=pl.ANY)],
                out_specs=[_HBM_SPEC] * (2 * n), input_output_aliases={i: i for i in range(2 * n)},
                compiler_params=pltpu.CompilerParams(has_side_effects=_SPLIT_EFFECT))(
                    *srcs, *lands, send_sems, recv_sems, after)
    return res[n:]


def _with_own(land, own, me):
    slot = lax.broadcasted_iota(jnp.int32, (N_DEV,) + (1,) * (land.ndim - 1), 0)
    return jnp.where(slot == me, own[None], land)


def _cols_from_blocks(g):
    return jnp.transpose(g, (1, 0, 2)).reshape(g.shape[1], N_DEV * g.shape[2])


def _blocks_from_cols(w):
    r, c8 = w.shape
    return jnp.transpose(w.reshape(r, N_DEV, c8 // N_DEV), (1, 0, 2))


def _pack(arrs):
    flat = jnp.concatenate([a.reshape(-1).astype(F32) for a in arrs])
    pad = (-flat.shape[0]) % (8 * LANE)
    return jnp.pad(flat, (0, pad)).reshape(-1, LANE)


def _unpack(packed, shapes):
    flat = packed.reshape(-1)
    out, off = [], 0
    for s in shapes:
        size = math.prod(s)
        out.append(flat[off:off + size].reshape(s))
        off += size
    return out


def kernel(x, c, ctx, c_ctx, w_ada, b_ada, g_mix, w_in, s5_a_re, s5_a_im, s5_log_step, s5_b_re, s5_b_im, s5_c_re, s5_c_im, s5_d, s5_w_glu, s5_b_glu, sgu_ln_g, sgu_ln_b, sgu_w, sgu_b, w_proj_a, w_proj_b, b_gate, w_out, g_ffn, w_up, conv_w, conv_b, w_down, g_final, loss_target, m_c_ctx, m_w_ada, m_b_ada, m_g_mix, m_w_in, m_s5_a_re, m_s5_a_im, m_s5_log_step, m_s5_b_re, m_s5_b_im, m_s5_c_re, m_s5_c_im, m_s5_d, m_s5_w_glu, m_s5_b_glu, m_sgu_ln_g, m_sgu_ln_b, m_sgu_w, m_sgu_b, m_w_proj_a, m_w_proj_b, m_b_gate, m_w_out, m_g_ffn, m_w_up, m_conv_w, m_conv_b, m_w_down, m_g_final, v_c_ctx, v_w_ada, v_b_ada, v_g_mix, v_w_in, v_s5_a_re, v_s5_a_im, v_s5_log_step, v_s5_b_re, v_s5_b_im, v_s5_c_re, v_s5_c_im, v_s5_d, v_s5_w_glu, v_s5_b_glu, v_sgu_ln_g, v_sgu_ln_b, v_sgu_w, v_sgu_b, v_w_proj_a, v_w_proj_b, v_b_gate, v_w_out, v_g_ffn, v_w_up, v_conv_w, v_conv_b, v_w_down, v_g_final):
    given = dict(locals())
    wts = {n: given[n] for n in WEIGHTS}
    mom1 = {n: given["m_" + n] for n in WEIGHTS}
    mom2 = {n: given["v_" + n] for n in WEIGHTS}

    me = 4 * lax.axis_index("x") + 2 * lax.axis_index("y") + lax.axis_index("c")
    xs, cx, tgt = x[0], ctx[0], loss_target[0]
    n_tok, d = xs.shape
    n_ctx = cx.shape[0]
    s5w = s5_d.shape[1]
    ffn = w_down.shape[1] * N_DEV
    n_mod = w_ada.shape[2] * N_DEV // d
    mod_cols = w_ada.shape[2]

    def two_d(a):
        return a.reshape(-1, a.shape[-1])

    conv_w9 = conv_w[0].reshape(9, -1)
    gathered = _exchange(
        [(c, "gather"), (_b16(w_in[0]), "gather"), (_b16(s5_w_glu[0]), "gather"), (_b16(w_proj_a[0]), "gather"),
         (_b16(w_proj_b[0]), "gather")], "gather_weights")
    c_all = gathered[0].reshape(N_DEV, d)
    w_in_f = _cols_from_blocks(gathered[1])
    w_in_u, w_in_rest = w_in_f[:, :s5w], w_in_f[:, s5w:]
    w_glu_f = gathered[2].reshape(-1, s5w)
    w_pa_f = _cols_from_blocks(gathered[3])
    w_pb_f = _cols_from_blocks(gathered[4])

    cs_in = jnp.concatenate([c_all, jnp.broadcast_to(c_ctx[None, :], (N_DEV, d))], axis=0)
    w_ada_loc = w_ada[0]
    (mod_mine,) = _stage_fwd(_fn_mod, [cs_in], [w_ada_loc], [(mod_cols, F32)], blk=2 * N_DEV, name="mod_fwd")
    (mod_blocks,) = _exchange([(mod_mine, "gather")], "gather_mod")
    late_own = [_b16(w_out[0]), _b16(w_up[0]), conv_w9, _b16(w_down[0])]
    late_weights, tok = _exchange_begin([(a, "gather") for a in late_own], "gather_late_begin", after=mod_blocks)
    mod_all = _cols_from_blocks(mod_blocks) + b_ada + tok
    mod = lax.dynamic_slice_in_dim(mod_all, me, 1, axis=0)
    mod_c = mod_all[N_DEV:N_DEV + 1]
    sh1, sc1, ga1, sh2, sc2, ga2 = [mod[:, i * d:(i + 1) * d] for i in range(n_mod)]
    sh1c, sc1c = mod_c[:, :d], mod_c[:, d:2 * d]

    a_par = [g_mix, sh1, sc1]
    ac_par = [g_mix, sh1c, sc1c]
    (h,) = _stage_fwd(_fn_a, [xs], a_par, [(d, BF16)], blk=_pick(n_tok, 512, 8), name="modulate1_fwd")
    (hc,) = _stage_fwd(_fn_a, [cx], ac_par, [(d, BF16)], blk=_pick(n_ctx, 512, 8), name="modulate1_ctx_fwd")
    pu = _mm(h, w_in_u, name="proj_u")
    prest = _mm(h, w_in_rest, name="proj_rest")
    puc = _mm(hc, w_in_u, name="proj_u_ctx")

    s5_prm = [s5_a_re[0][:, :, None, :], s5_a_im[0][:, :, None, :], s5_log_step[0][:, :, None, None],
              jnp.swapaxes(s5_b_re[0], 2, 3), jnp.swapaxes(s5_b_im[0], 2, 3), s5_c_re[0], s5_c_im[0]]
    pu_g, puc_g = _to_groups(pu), _to_groups(puc)
    ysc = _from_groups(_s5_fwd(pu_g, puc_g, s5_prm))

    b_par = [s5_d, w_glu_f, s5_b_glu, sgu_ln_g, sgu_ln_b, two_d(sgu_w[0]), jnp.transpose(sgu_b[0]),
             w_pa_f, w_pb_f, b_gate]
    b_rows = [pu, prest, ysc]
    b_blk = _pick(n_tok, 256, CHUNK)
    (mpre,) = _stage_fwd(_fn_b, b_rows, b_par, [(d, BF16)], blk=b_blk, name="mixers_fwd")
    late = [_with_own(land, own, me) for land, own in zip(_exchange_end(late_weights, mpre, "gather_late_end"),
                                                          late_own)]
    w_out_f = late[0].reshape(-1, d)
    w_up_f = _cols_from_blocks(late[1])
    w_up_g, w_up_v = w_up_f[:, :ffn], w_up_f[:, ffn:]
    conv_w_f = _cols_from_blocks(late[2])
    w_down_f = late[3].reshape(-1, d)
    mo = _mm(mpre, w_out_f, name="out_proj")

    c_par = [ga1, g_ffn, sh2, sc2]
    c_blk = _pick(n_tok, 512, 8)
    x1, h2 = _stage_fwd(_fn_c, [xs, mo], c_par, [(d, F32), (d, BF16)], blk=c_blk, name="modulate2_fwd")
    up_g = _mm(h2, w_up_g, out_dtype=BF16, name="up_gate")
    up_v = _mm(h2, w_up_v, out_dtype=BF16, name="up_val")
    cw_g, cw_v = conv_w_f[:, :ffn], conv_w_f[:, ffn:]
    cb_g, cb_v = conv_b[:, :ffn], conv_b[:, ffn:]
    act, gate_c, val_c = _conv_act_fwd(up_g, up_v, cw_g, cw_v, cb_g, cb_v)
    dn = _mm(act, w_down_f, name="down_proj")

    loss_part, d_x1a, d_dn, d_ga2, d_g_final = _stage_loss(
        _fn_e, [x1, dn, tgt], [ga2, g_final[None, :]], blk=c_blk, name="loss_head",
        row_grads=[(0, F32), (1, BF16)])

    d_act = _mm(d_dn, w_down_f, tb=True, name="down_proj_dx")
    g_w_down = _mm(act, d_dn, ta=True, name="down_proj_dw")

    own_block = {}

    def grad_blocks(name, g):
        blocks = _b16(_blocks_from_cols(g) if name in COL_SHARDED
                      else g.reshape(N_DEV, g.shape[0] // N_DEV, g.shape[1]))
        own_block[name] = lax.dynamic_index_in_dim(blocks, me, 0, keepdims=False)
        return blocks, "a2a"

    sent_down, tok = _exchange_begin([grad_blocks('w_down', g_w_down)], "grads_down_begin")
    dcg, dcv, g_cw_g, g_cw_v, g_cb_g, g_cb_v = _conv_act_bwd(up_g, up_v, gate_c, val_c, d_act)
    dug = _conv_plain(dcg, cw_g[::-1] + tok, "conv_dx_gate")
    duv = _conv_plain(dcv, cw_v[::-1], "conv_dx_val")
    d_h2 = _mm(dug, w_up_g, tb=True, name="up_gate_dx")
    d_h2 = _mm(duv, w_up_v, tb=True, add=d_h2, out_dtype=BF16, name="up_val_dx")
    g_w_up = jnp.concatenate([_mm(h2, dug, ta=True, name="up_gate_dw"), _mm(h2, duv, ta=True, name="up_val_dw")],
                             axis=1)
    sent_up, tok = _exchange_begin(
        [grad_blocks('w_up', g_w_up), grad_blocks('conv_w', jnp.concatenate([g_cw_g, g_cw_v], axis=1))],
        "grads_up_begin")
    (d_xc, d_mo), (d_ga1, g_g_ffn, d_sh2, d_sc2) = _split(_stage_bwd(
        _fn_c, [xs, mo], [ga1 + tok] + c_par[1:], [d_x1a, d_h2], blk=c_blk, name="modulate2_bwd",
        row_grads=[(0, F32), (1, BF16)]), 2)

    d_mpre = _mm(d_mo, w_out_f, tb=True, out_dtype=BF16, name="out_proj_dx")
    g_w_out = _mm(mpre, d_mo, ta=True, name="out_proj_dw")
    (d_prest, d_ysc), b_grads = _split(_stage_bwd(
        _fn_b, b_rows, b_par, [d_mpre], blk=b_blk, name="mixers_bwd", row_grads=[(1, BF16), (2, F32)]), 2)
    (g_s5_d, g_w_glu, g_b_glu, g_ln_g, g_ln_b, g_sgu_w, g_sgu_bt, g_w_pa, g_w_pb, g_b_gate) = b_grads

    sent_mix, tok = _exchange_begin(
        [grad_blocks('w_out', g_w_out), grad_blocks('s5_w_glu', g_w_glu), grad_blocks('w_proj_a', g_w_pa),
         grad_blocks('w_proj_b', g_w_pb)], "grads_mixer_begin")
    skip_g = jnp.tile(s5_d.reshape(-1, 1, S5_H), (1, 1, S5_T)) + tok
    s5_out = _s5_bwd(pu_g, puc_g, s5_prm, _to_groups(d_ysc), skip_g)
    g_a_re, g_a_im, g_ls, g_bt_re, g_bt_im, g_c_re, g_c_im = s5_out[2:]

    part = {
        's5_a_re': g_a_re, 's5_a_im': g_a_im, 's5_log_step': g_ls,
        's5_b_re': jnp.swapaxes(g_bt_re, 2, 3), 's5_b_im': jnp.swapaxes(g_bt_im, 2, 3),
        's5_c_re': g_c_re, 's5_c_im': g_c_im, 's5_d': g_s5_d, 's5_b_glu': g_b_glu, 'sgu_ln_g': g_ln_g,
        'sgu_ln_b': g_ln_b, 'sgu_w': g_sgu_w, 'sgu_b': jnp.transpose(g_sgu_bt), 'b_gate': g_b_gate,
        'g_ffn': g_g_ffn, 'conv_b': jnp.concatenate([g_cb_g, g_cb_v], axis=1), 'g_final': d_g_final,
    }
    early = [n for n in REPLICATED if n not in LATE_REPLICATED]
    early_part = _pack([part[n] for n in early])
    sent_small, tok = _exchange_begin([(early_part, "gather")], "grads_small_begin")
    d_pu, d_puc = _b16(_from_groups(s5_out[0])), _from_groups(s5_out[1]) + tok

    d_h = _mm(d_pu, w_in_u, tb=True, name="proj_u_dx")
    d_h = _mm(d_prest, w_in_rest, tb=True, add=d_h, out_dtype=BF16, name="proj_rest_dx")
    d_hc = _mm(d_puc, w_in_u, tb=True, out_dtype=BF16, name="proj_u_ctx_dx")
    g_w_in_u = _mm(hc, d_puc, ta=True, name="proj_u_ctx_dw")
    g_w_in_u = _mm(h, d_pu, ta=True, add=g_w_in_u, name="proj_u_dw")
    g_w_in = jnp.concatenate([g_w_in_u, _mm(h, d_prest, ta=True, name="proj_rest_dw")], axis=1)
    sent_in, tok = _exchange_begin([grad_blocks('w_in', g_w_in)], "grads_in_begin")

    (grad_x,), (g_g_mix_x, d_sh1, d_sc1) = _split(_stage_bwd(
        _fn_a_res, [xs], [g_mix + tok] + a_par[1:], [d_h, d_xc], blk=c_blk, name="modulate1_bwd",
        row_grads=[(0, F32)]), 1)
    _, (g_g_mix_c, d_sh1c, d_sc1c) = _split(_stage_bwd(
        _fn_a, [cx], ac_par, [d_hc], blk=_pick(n_ctx, 512, 8), name="modulate1_ctx_bwd", row_grads=[]), 0)

    zeros = jnp.zeros((1, (n_mod - 2) * d), F32)
    d_mod = jnp.concatenate([d_sh1, d_sc1, d_ga1, d_sh2, d_sc2, d_ga2], axis=1)
    d_mod_c = jnp.concatenate([d_sh1c, d_sc1c, zeros], axis=1)
    (d_mod_all,) = _exchange([(jnp.concatenate([d_mod, d_mod_c], axis=0), "gather")], "gather_dmod")
    d_mod_rows = jnp.transpose(d_mod_all, (1, 0, 2)).reshape(2 * N_DEV, n_mod * d)
    d_mod_mine = lax.dynamic_slice_in_dim(d_mod_rows, me * mod_cols, mod_cols, axis=1)
    (d_cs,), (g_w_ada,) = _split(_stage_bwd(
        _fn_mod, [cs_in], [w_ada_loc], [d_mod_mine], blk=2 * N_DEV, name="mod_bwd", row_grads=[(0, F32)]), 1)

    part.update({'c_ctx': jnp.sum(d_cs[N_DEV:], axis=0), 'b_ada': d_mod + d_mod_c, 'g_mix': g_g_mix_x + g_g_mix_c})
    (late_parts,) = _exchange([(_pack([part[n] for n in LATE_REPLICATED]), "gather")], "exchange_grads")

    summed = {}
    (early_parts,) = _exchange_end(sent_small, late_parts, "grads_small_end")
    (summed['w_down'],) = _exchange_end(sent_down, late_parts, "grads_down_end")
    summed['w_up'], summed['conv_w'] = _exchange_end(sent_up, late_parts, "grads_up_end")
    summed['w_out'], summed['s5_w_glu'], summed['w_proj_a'], summed['w_proj_b'] = _exchange_end(
        sent_mix, late_parts, "grads_mixer_end")
    (summed['w_in'],) = _exchange_end(sent_in, late_parts, "grads_in_end")

    out = {}
    me_arr = me.reshape(1, 1).astype(jnp.int32)
    for names, parts, own, tag in ((early, early_parts, early_part, "early"),
                                   (LATE_REPLICATED, late_parts, None, "late")):
        res = _adamw(_pack([wts[n] for n in names]), parts, _pack([mom1[n] for n in names]),
                     _pack([mom2[n] for n in names]), "adamw_replicated_" + tag, own=own,
                     me=None if own is None else me_arr)
        res = [_unpack(r, [wts[n].shape for n in names]) for r in res]
        for i, n in enumerate(names):
            out[n] = tuple(r[i] for r in res)
    for n, parts in summed.items():
        shape = wts[n].shape
        res = _adamw(two_d(wts[n]), parts, two_d(mom1[n]), two_d(mom2[n]), "adamw_" + n, own=own_block[n],
                     me=me_arr)
        out[n] = tuple(r.reshape(shape) for r in res)
    res = _adamw(w_ada_loc, g_w_ada[None], m_w_ada[0], v_w_ada[0], "adamw_w_ada")
    out['w_ada'] = tuple(r.reshape(w_ada.shape) for r in res)

    loss = lax.psum(loss_part[0, 0], ("x", "y", "c"))
    return (loss, grad_x[None], *[out[n][0] for n in WEIGHTS], *[out[n][1] for n in WEIGHTS],
            *[out[n][2] for n in WEIGHTS], *[out[n][3] for n in WEIGHTS])


def _split(res, n_rows):
    return tuple(res[:n_rows]), tuple(res[n_rows:])
```

```python
import functools
import math

import jax
import jax.numpy as jnp
from jax import lax
from jax.experimental import pallas as pl
from jax.experimental.pallas import tpu as pltpu

F32 = jnp.float32
BF16 = jnp.bfloat16
HI = lax.Precision.HIGHEST

N_DEV = 8
GRID_W = 64
CHUNK = 128
EPS = 1e-6
S5_T = 32
S5_H = 16
LANE = 128
HALO = 128
VMEM_LIMIT = 56 * 1024 * 1024

ADAM_LR = 0.001
ADAM_B1 = 0.9
ADAM_B2 = 0.999
ADAM_EPS = 1e-08
ADAM_WD = 0.01
ADAM_STEP = 10

WEIGHTS = ['c_ctx', 'w_ada', 'b_ada', 'g_mix', 'w_in', 's5_a_re', 's5_a_im', 's5_log_step', 's5_b_re', 's5_b_im',
           's5_c_re', 's5_c_im', 's5_d', 's5_w_glu', 's5_b_glu', 'sgu_ln_g', 'sgu_ln_b', 'sgu_w', 'sgu_b',
           'w_proj_a', 'w_proj_b', 'b_gate', 'w_out', 'g_ffn', 'w_up', 'conv_w', 'conv_b', 'w_down', 'g_final']
COL_SHARDED = ('w_ada', 'w_in', 'w_proj_a', 'w_proj_b', 'w_up', 'conv_w')
ROW_SHARDED = ('s5_w_glu', 'w_out', 'w_down')
SHARDED = COL_SHARDED + ROW_SHARDED
REPLICATED = [n for n in WEIGHTS if n not in SHARDED]
LATE_REPLICATED = ['c_ctx', 'b_ada', 'g_mix']


def _call(body, **kw):
    return pl.pallas_call(body, **kw)


def _params(n_grid):
    return pltpu.CompilerParams(dimension_semantics=("arbitrary",) * n_grid, vmem_limit_bytes=VMEM_LIMIT)


def _pick(dim, pref, unit=LANE):
    best = None
    d = unit
    while d <= min(dim, pref):
        if dim % d == 0:
            best = d
        d += unit
    return best if best is not None else dim


def _dg(a, b, ca, cb, prec=None):
    return lax.dot_general(a, b, (((ca,), (cb,)), ((), ())), precision=prec, preferred_element_type=F32)


def _b16(v):
    return v.astype(BF16)


@jax.custom_vjp
def mmb(a, b):
    return _dg(_b16(a), _b16(b), 1, 0)


def _mmb_fwd(a, b):
    return mmb(a, b), (a, b)


def _mmb_bwd(res, g):
    a, b = res
    g = _b16(g)
    return _dg(g, _b16(b), 1, 1).astype(a.dtype), _dg(_b16(a), g, 0, 0).astype(b.dtype)


mmb.defvjp(_mmb_fwd, _mmb_bwd)


@jax.custom_vjp
def mmb_nt(a, b):
    return _dg(_b16(a), _b16(b), 1, 1)


def _mmb_nt_fwd(a, b):
    return mmb_nt(a, b), (a, b)


def _mmb_nt_bwd(res, g):
    a, b = res
    g = _b16(g)
    return _dg(g, _b16(b), 1, 0).astype(a.dtype), _dg(g, _b16(a), 0, 0).astype(b.dtype)


mmb_nt.defvjp(_mmb_nt_fwd, _mmb_nt_bwd)


@jax.custom_vjp
def mmf(a, b):
    return _dg(a, b, 1, 0, HI)


def _mmf_fwd(a, b):
    return mmf(a, b), (a, b)


def _mmf_bwd(res, g):
    a, b = res
    return _dg(g, b, 1, 1, HI), _dg(a, g, 0, 0, HI)


mmf.defvjp(_mmf_fwd, _mmf_bwd)


def _dg3(a, b, ca, cb):
    ah, bh = _b16(a), _b16(b)
    al, bl = _b16(a - ah.astype(F32)), _b16(b - bh.astype(F32))
    return _dg(ah, bh, ca, cb) + _dg(ah, bl, ca, cb) + _dg(al, bh, ca, cb)


@jax.custom_vjp
def mm3_nt(a, b):
    return _dg3(a, b, 1, 1)


def _mm3_nt_fwd(a, b):
    return mm3_nt(a, b), (a, b)


def _mm3_nt_bwd(res, g):
    a, b = res
    return _dg3(g, b, 1, 0), _dg3(g, a, 0, 0)


mm3_nt.defvjp(_mm3_nt_fwd, _mm3_nt_bwd)


def _shift_impl(x, k, up):
    n = x.shape[0]
    idx = lax.broadcasted_iota(jnp.int32, (n, 1), 0)
    if up:
        return jnp.where(idx < n - k, pltpu.roll(x, n - k, 0), 0.0)
    return jnp.where(idx >= k, pltpu.roll(x, k, 0), 0.0)


@functools.partial(jax.custom_vjp, nondiff_argnums=(1, 2))
def _shift(x, k, up):
    return _shift_impl(x, k, up)


def _shift_fwd(x, k, up):
    return _shift_impl(x, k, up), None


def _shift_bwd(k, up, _, g):
    return (_shift_impl(g, k, not up),)


_shift.defvjp(_shift_fwd, _shift_bwd)


def _mm(a, b, *, name, ta=False, tb=False, add=None, out_dtype=F32, tm_pref=2048, tn_pref=1408, tk_pref=1408):
    m, k = (a.shape[1], a.shape[0]) if ta else a.shape
    n = b.shape[0] if tb else b.shape[1]
    if add is not None and out_dtype != F32:
        tm_pref = min(tm_pref, 1024)
    tm, tn, tk = _pick(m, tm_pref), _pick(n, tn_pref), _pick(k, tk_pref)
    nk = k // tk
    a_spec = (pl.BlockSpec((tk, tm), lambda i, j, kk: (kk, i)) if ta
              else pl.BlockSpec((tm, tk), lambda i, j, kk: (i, kk)))
    b_spec = (pl.BlockSpec((tn, tk), lambda i, j, kk: (j, kk)) if tb
              else pl.BlockSpec((tk, tn), lambda i, j, kk: (kk, j)))
    o_spec = pl.BlockSpec((tm, tn), lambda i, j, kk: (i, j))
    ca, cb = (0 if ta else 1), (1 if tb else 0)
    has_add = add is not None
    in_place = out_dtype == F32 or nk == 1

    def body(*refs):
        a_ref, b_ref = refs[0], refs[1]
        o_ref = refs[3] if has_add else refs[2]
        prod = _dg(_b16(a_ref[...]), _b16(b_ref[...]), ca, cb)
        if nk == 1:
            if has_add:
                prod = prod + refs[2][...].astype(F32)
            o_ref[...] = prod.astype(o_ref.dtype)
            return
        acc_ref = o_ref if in_place else refs[-1]
        kk = pl.program_id(2)

        @pl.when(kk == 0)
        def _():
            acc_ref[...] = prod

        @pl.when(kk > 0)
        def _():
            acc_ref[...] += prod

        if has_add or not in_place:
            @pl.when(kk == nk - 1)
            def _():
                r = acc_ref[...]
                if has_add:
                    r = r + refs[2][...].astype(F32)
                o_ref[...] = r.astype(o_ref.dtype)

    ins = [a, b] + ([add] if has_add else [])
    in_specs = [a_spec, b_spec] + ([o_spec] if has_add else [])
    return _call(body, name=name, grid=(m // tm, n // tn, nk), in_specs=in_specs, out_specs=o_spec,
                 out_shape=jax.ShapeDtypeStruct((m, n), out_dtype),
                 scratch_shapes=[] if in_place else [pltpu.VMEM((tm, tn), F32)],
                 compiler_params=_params(3))(*ins)


def _row_spec(blk, width):
    return pl.BlockSpec((blk, width), lambda i: (i, 0))


def _whole_spec(shape):
    return pl.BlockSpec(shape, lambda i: (0,) * len(shape))


def _stage_fwd(fn, rows, params, outs, *, blk, name, n_rows=None):
    n = n_rows or rows[0].shape[0]
    nr, npar = len(rows), len(params)

    def body(*refs):
        vals = [r[...] for r in refs[:nr + npar]]
        res = fn(*vals)
        for o_ref, v in zip(refs[nr + npar:], res):
            o_ref[...] = v.astype(o_ref.dtype)

    return _call(body, name=name, grid=(n // blk,),
                 in_specs=[_row_spec(blk, r.shape[1]) for r in rows] + [_whole_spec(p.shape) for p in params],
                 out_specs=[_row_spec(blk, w) for w, _ in outs],
                 out_shape=[jax.ShapeDtypeStruct((n, w), dt) for w, dt in outs],
                 compiler_params=_params(1))(*rows, *params)


def _stage_bwd(fn, rows, params, cts, *, blk, name, row_grads, n_rows=None):
    n = n_rows or rows[0].shape[0]
    nr, npar, nct = len(rows), len(params), len(cts)

    def body(*refs):
        vals = [r[...].astype(F32) for r in refs[:nr + npar]]
        ct = [r[...] for r in refs[nr + npar:nr + npar + nct]]
        d_rows = refs[nr + npar + nct:nr + npar + nct + len(row_grads)]
        d_par = refs[nr + npar + nct + len(row_grads):]
        res, vjp = jax.vjp(fn, *vals)
        g = vjp(tuple(c.astype(r.dtype) for c, r in zip(ct, res)))
        for o_ref, (j, _) in zip(d_rows, row_grads):
            o_ref[...] = g[j].astype(o_ref.dtype)

        @pl.when(pl.program_id(0) == 0)
        def _():
            for o_ref in d_par:
                o_ref[...] = jnp.zeros_like(o_ref)

        for j, o_ref in enumerate(d_par):
            o_ref[...] += g[nr + j].astype(F32)

    return _call(body, name=name, grid=(n // blk,),
                 in_specs=([_row_spec(blk, r.shape[1]) for r in rows] + [_whole_spec(p.shape) for p in params]
                           + [_row_spec(blk, c.shape[1]) for c in cts]),
                 out_specs=([_row_spec(blk, rows[j].shape[1]) for j, _ in row_grads]
                            + [_whole_spec(p.shape) for p in params]),
                 out_shape=([jax.ShapeDtypeStruct((n, rows[j].shape[1]), dt) for j, dt in row_grads]
                            + [jax.ShapeDtypeStruct(p.shape, F32) for p in params]),
                 compiler_params=_params(1))(*rows, *params, *cts)


def _stage_loss(fn, rows, params, *, blk, name, row_grads):
    n = rows[0].shape[0]
    nr, npar = len(rows), len(params)

    def body(*refs):
        vals = [r[...].astype(F32) for r in refs[:nr + npar]]
        loss_ref = refs[nr + npar]
        d_rows = refs[nr + npar + 1:nr + npar + 1 + len(row_grads)]
        d_par = refs[nr + npar + 1 + len(row_grads):]
        res, vjp = jax.vjp(fn, *vals)
        g = vjp(jnp.ones_like(res))
        for o_ref, (j, _) in zip(d_rows, row_grads):
            o_ref[...] = g[j].astype(o_ref.dtype)

        @pl.when(pl.program_id(0) == 0)
        def _():
            loss_ref[...] = jnp.zeros_like(loss_ref)
            for o_ref in d_par:
                o_ref[...] = jnp.zeros_like(o_ref)

        loss_ref[...] += res
        for j, o_ref in enumerate(d_par):
            o_ref[...] += g[nr + j].astype(F32)

    return _call(body, name=name, grid=(n // blk,),
                 in_specs=[_row_spec(blk, r.shape[1]) for r in rows] + [_whole_spec(p.shape) for p in params],
                 out_specs=([_whole_spec((1, 1))] + [_row_spec(blk, rows[j].shape[1]) for j, _ in row_grads]
                            + [_whole_spec(p.shape) for p in params]),
                 out_shape=([jax.ShapeDtypeStruct((1, 1), F32)]
                            + [jax.ShapeDtypeStruct((n, rows[j].shape[1]), dt) for j, dt in row_grads]
                            + [jax.ShapeDtypeStruct(p.shape, F32) for p in params]),
                 compiler_params=_params(1))(*rows, *params)


def _rms(x, g):
    return x * lax.rsqrt(jnp.mean(x * x, axis=-1, keepdims=True) + EPS) * g


def _modulate(x, g, shift, scale):
    return _rms(x, g) * (1.0 + scale) + shift


def _fn_mod(cs, w_ada):
    return (mmb(jax.nn.silu(cs), w_ada),)


def _fn_a(x, g_mix, sh, sc):
    return (_b16(_modulate(x, g_mix, sh, sc)),)


def _fn_a_res(x, g_mix, sh, sc):
    return _b16(_modulate(x, g_mix, sh, sc)), x


def _sgu_spatial(v, sgu_w, sgu_bt):
    rows, width = v.shape
    gdim = width // (sgu_w.shape[0] // CHUNK)
    groups = width // gdim
    expand = (lax.broadcasted_iota(jnp.int32, (groups, width), 1) // gdim
              == lax.broadcasted_iota(jnp.int32, (groups, width), 0)).astype(F32)
    bias = mmf(sgu_bt, expand)
    lane = lax.broadcasted_iota(jnp.int32, (CHUNK, LANE), 1)
    per_lane_block = LANE // gdim
    chunks = []
    for ci in range(rows // CHUNK):
        vc = v[ci * CHUNK:(ci + 1) * CHUNK]
        blocks = []
        for lb in range(width // LANE):
            vb = vc[:, lb * LANE:(lb + 1) * LANE]
            acc = None
            for s in range(per_lane_block):
                g = lb * per_lane_block + s
                r = mmb(sgu_w[g * CHUNK:(g + 1) * CHUNK], vb)
                sel = (lane // gdim) == s
                acc = jnp.where(sel, r, 0.0) if acc is None else jnp.where(sel, r, acc)
            blocks.append(acc)
        chunks.append(jnp.concatenate(blocks, axis=1) + bias)
    return jnp.concatenate(chunks, axis=0)


def _fn_b(pu, prest, ysc, s5_d, w_glu, b_glu, ln_g, ln_b, sgu_w, sgu_bt, w_pa, w_pb, b_gate):
    sw = ln_g.shape[1]
    y = jax.nn.gelu(pu * s5_d + ysc)
    ya = y * jax.nn.sigmoid(mmb(y, w_glu) + b_glu)
    z = jax.nn.gelu(prest[:, :2 * sw])
    u, v = z[:, :sw], z[:, sw:]
    vc = v - jnp.mean(v, axis=-1, keepdims=True)
    v = vc * lax.rsqrt(jnp.mean(vc * vc, axis=-1, keepdims=True) + EPS) * ln_g + ln_b
    yb = u * _sgu_spatial(v, sgu_w, sgu_bt)
    gates = jax.nn.sigmoid(prest[:, 2 * sw:] + b_gate)
    d = gates.shape[1] // 2
    return (_b16(gates[:, :d] * mmb(ya, w_pa) + gates[:, d:] * mmb(yb, w_pb)),)


def _fn_c(x, mo, ga1, g_ffn, sh2, sc2):
    x1 = x + ga1 * mo
    return x1, _b16(_modulate(x1, g_ffn, sh2, sc2))


def _fn_e(x1, dn, tgt, ga2, g_final):
    y = _rms(x1 + ga2 * dn, g_final)
    err = (y - tgt) ** 2
    return 0.5 * jnp.sum(jnp.mean(err, axis=-1, keepdims=True), axis=0, keepdims=True)


def _s5_direction(u, mask, a_re, a_im, log_step, bt_re, bt_im, c_re, c_im, rev):
    nc, width = u.shape
    t_len = width // S5_H
    n = a_re.shape[1]
    dt = jnp.exp(log_step)
    lr, li = a_re * dt, a_im * dt
    mag = jnp.exp(lr)
    ab_re, ab_im = mag * jnp.cos(li), mag * jnp.sin(li)
    p, q = ab_re - 1.0, ab_im
    den = a_re * a_re + a_im * a_im
    k_re, k_im = (p * a_re + q * a_im) / den, (q * a_re - p * a_im) / den
    bb_re = k_re * bt_re - k_im * bt_im
    bb_im = k_re * bt_im + k_im * bt_re

    def power(e):
        m = jnp.exp(lr * e)
        return m * jnp.cos(li * e), m * jnp.sin(li * e)

    def cmul(xr, xi, yr, yi):
        return xr * yr - xi * yi, xr * yi + xi * yr

    order = range(t_len - 1, -1, -1) if rev else range(t_len)
    e1 = jnp.concatenate([jnp.full((1, 1, n), float(t_len - 1 - pos), F32) for pos in order], axis=0)
    lr3, li3 = lr.reshape(1, 1, n), li.reshape(1, 1, n)
    m1, c1, s1 = jnp.exp(lr3 * e1), jnp.cos(li3 * e1), jnp.sin(li3 * e1)
    m2 = jnp.exp(-(lr3 * e1))
    x1 = (m1 * c1, m1 * s1)
    x2 = (m2 * c1, -(m2 * s1))
    x3 = cmul(*[v.reshape(1, 1, n) for v in power(float(t_len))], *x2)

    def rows(xr, xi, yr, yi):
        zr, zi = cmul(xr, xi, yr.reshape(1, S5_H, n), yi.reshape(1, S5_H, n))
        return zr.reshape(width, n), zi.reshape(width, n)

    pr, pi = rows(*x1, bb_re, bb_im)
    rr, ri = rows(*x2, c_re, c_im)
    wr, wi = rows(*x3, c_re, c_im)
    toep = (mm3_nt(pr, rr) - mm3_nt(pi, ri)) * mask

    sr, si = mmb(u, pr), mmb(u, pi)
    k = 1
    while k < nc:
        ar, ai = power(float(t_len * k))
        hr, hi = _shift(sr, k, rev), _shift(si, k, rev)
        sr, si = sr + ar * hr - ai * hi, si + ar * hi + ai * hr
        k *= 2
    cr_in, ci_in = _shift(sr, 1, rev), _shift(si, 1, rev)
    return mmb(u, toep) + mmb_nt(cr_in, wr) - mmb_nt(ci_in, wi)


def _fn_s5(masks, x_chunks, ctx_chunks, *prm):
    nx, nctx = x_chunks.shape[0], ctx_chunks.shape[0]
    pad = (-(nx + 2 * nctx)) % LANE
    u = jnp.concatenate([ctx_chunks, x_chunks, ctx_chunks, jnp.zeros((pad, x_chunks.shape[1]), F32)], axis=0)
    out = None
    for d in range(2):
        y = _s5_direction(u, masks[d], *[p[d] for p in prm], rev=(d == 1))
        out = y if out is None else out + y
    return out[nctx:nctx + nx]


def _s5_specs(prm):
    return [pl.BlockSpec((2, 1) + p.shape[2:], lambda g: (0, g, 0, 0)) for p in prm]


def _group_spec(a):
    return pl.BlockSpec((1,) + a.shape[1:], lambda i: (i, 0, 0))


def _s5_masks(width):
    pos = jnp.arange(width) // S5_H
    causal = (pos[None, :] >= pos[:, None]).astype(F32)
    return jnp.stack([causal, causal.T])


def _s5_fwd(x_g, ctx_g, prm):
    masks = _s5_masks(x_g.shape[2])

    def body(*refs):
        pv = [r[:, 0] for r in refs[3:3 + len(prm)]]
        refs[-1][0] = _fn_s5(refs[0][...], refs[1][0], refs[2][0], *pv)

    return _call(body, name="s5_fwd", grid=(x_g.shape[0],),
                 in_specs=[_whole_spec(masks.shape), _group_spec(x_g), _group_spec(ctx_g)] + _s5_specs(prm),
                 out_specs=_group_spec(x_g), out_shape=jax.ShapeDtypeStruct(x_g.shape, F32),
                 compiler_params=_params(1))(masks, x_g, ctx_g, *prm)


def _s5_bwd(x_g, ctx_g, prm, dy_g, skip_g):
    npar = len(prm)
    masks = _s5_masks(x_g.shape[2])

    def body(*refs):
        pv = [r[:, 0] for r in refs[3:3 + npar]]
        dy = refs[3 + npar][0]
        _, vjp = jax.vjp(functools.partial(_fn_s5, refs[0][...]), refs[1][0], refs[2][0], *pv)
        grads = vjp(dy)
        outs = refs[5 + npar:]
        outs[0][0] = (grads[0] + dy * refs[4 + npar][0]).astype(outs[0].dtype)
        outs[1][0] = grads[1]
        for o_ref, gv in zip(outs[2:], grads[2:]):
            o_ref[:, 0] = gv

    return _call(body, name="s5_bwd", grid=(x_g.shape[0],),
                 in_specs=([_whole_spec(masks.shape), _group_spec(x_g), _group_spec(ctx_g)] + _s5_specs(prm)
                           + [_group_spec(dy_g), _group_spec(skip_g)]),
                 out_specs=[_group_spec(x_g), _group_spec(ctx_g)] + _s5_specs(prm),
                 out_shape=[jax.ShapeDtypeStruct(x_g.shape, F32), jax.ShapeDtypeStruct(ctx_g.shape, F32)]
                 + [jax.ShapeDtypeStruct(p.shape, F32) for p in prm],
                 compiler_params=_params(1))(masks, x_g, ctx_g, *prm, dy_g, skip_g)


def _to_groups(tok):
    n, width = tok.shape
    g = width // S5_H
    return jnp.transpose(tok.reshape(n, g, S5_H), (1, 0, 2)).reshape(g, n // S5_T, S5_T * S5_H)


def _from_groups(grp):
    g, nc, _ = grp.shape
    return jnp.transpose(grp.reshape(g, nc * S5_T, S5_H), (1, 0, 2)).reshape(nc * S5_T, g * S5_H)


def _conv_taps(xp, xm, xn, blk_i, n_blk):
    tb = xm.shape[0]
    xp = jnp.where(blk_i == 0, 0.0, xp.astype(F32))
    xn = jnp.where(blk_i == n_blk - 1, 0.0, xn.astype(F32))
    buf = jnp.concatenate([xp, xm.astype(F32), xn], axis=0)
    n = tb + 2 * HALO
    col = lax.broadcasted_iota(jnp.int32, (n, 1), 0) % GRID_W
    left = jnp.where(col >= 1, pltpu.roll(buf, 1, 0), 0.0)
    right = jnp.where(col <= GRID_W - 2, pltpu.roll(buf, n - 1, 0), 0.0)
    shifted = (left, buf, right)
    taps = []
    for di in range(3):
        start = HALO + (di - 1) * GRID_W
        for dj in range(3):
            taps.append(shifted[dj][start:start + tb])
    return taps


def _conv_sum(taps, w_ref):
    acc = None
    for k, tap in enumerate(taps):
        term = tap * w_ref[k:k + 1, :]
        acc = term if acc is None else acc + term
    return acc


def _conv_geometry(n_tok, width):
    tb = _pick(n_tok, 1024, HALO)
    cb = _pick(width, 256)
    nb = tb // HALO
    last = n_tok // HALO - 1
    main = pl.BlockSpec((tb, cb), lambda j, i: (i, j))
    prev = pl.BlockSpec((HALO, cb), lambda j, i: (jnp.maximum(i * nb - 1, 0), j))
    nxt = pl.BlockSpec((HALO, cb), lambda j, i: (jnp.minimum(i * nb + nb, last), j))
    par = lambda r: pl.BlockSpec((r, cb), lambda j, i: (0, j))
    return tb, cb, main, prev, nxt, par


def _conv_act_fwd(up_g, up_v, w_g, w_v, b_g, b_v):
    n_tok, width = up_g.shape
    tb, cb, main, prev, nxt, par = _conv_geometry(n_tok, width)
    n_blk = n_tok // tb

    def body(gp, gm, gn, vp, vm, vn, wg, wv, bg, bv, o_ref, gate_ref, val_ref):
        i = pl.program_id(1)
        gate = _conv_sum(_conv_taps(gp[...], gm[...], gn[...], i, n_blk), wg) + bg[...]
        val = _conv_sum(_conv_taps(vp[...], vm[...], vn[...], i, n_blk), wv) + bv[...]
        o_ref[...] = (jax.nn.silu(gate) * val).astype(o_ref.dtype)
        gate_ref[...] = gate
        val_ref[...] = val

    shp = jax.ShapeDtypeStruct
    return _call(body, name="conv_act_fwd", grid=(width // cb, n_tok // tb),
                 in_specs=[prev, main, nxt, prev, main, nxt, par(9), par(9), par(1), par(1)],
                 out_specs=[main, main, main],
                 out_shape=[shp((n_tok, width), BF16), shp((n_tok, width), F32), shp((n_tok, width), F32)],
                 compiler_params=_params(2))(up_g, up_g, up_g, up_v, up_v, up_v, w_g, w_v, b_g, b_v)


def _conv_act_bwd(up_g, up_v, gate_c, val_c, d_act):
    n_tok, width = up_g.shape
    tb, cb, main, prev, nxt, par = _conv_geometry(n_tok, width)
    n_blk = n_tok // tb

    def body(gp, gm, gn, vp, vm, vn, gc, vc, da, dcg, dcv, dwg, dwv, dbg, dbv):
        i = pl.program_id(1)
        taps_g = _conv_taps(gp[...], gm[...], gn[...], i, n_blk)
        taps_v = _conv_taps(vp[...], vm[...], vn[...], i, n_blk)
        gate, val = gc[...], vc[...]
        sig = jax.nn.sigmoid(gate)
        d = da[...].astype(F32)
        d_gate = d * val * sig * (1.0 + gate * (1.0 - sig))
        d_val = d * gate * sig
        dcg[...] = d_gate.astype(dcg.dtype)
        dcv[...] = d_val.astype(dcv.dtype)

        @pl.when(i == 0)
        def _():
            for r in (dwg, dwv, dbg, dbv):
                r[...] = jnp.zeros_like(r)

        dbg[...] += jnp.sum(d_gate, axis=0, keepdims=True)
        dbv[...] += jnp.sum(d_val, axis=0, keepdims=True)
        for k in range(9):
            dwg[k:k + 1, :] += jnp.sum(taps_g[k] * d_gate, axis=0, keepdims=True)
            dwv[k:k + 1, :] += jnp.sum(taps_v[k] * d_val, axis=0, keepdims=True)

    shp = jax.ShapeDtypeStruct
    return _call(body, name="conv_act_bwd", grid=(width // cb, n_tok // tb),
                 in_specs=[prev, main, nxt, prev, main, nxt, main, main, main],
                 out_specs=[main, main, par(9), par(9), par(1), par(1)],
                 out_shape=[shp((n_tok, width), BF16), shp((n_tok, width), BF16), shp((9, width), F32),
                            shp((9, width), F32), shp((1, width), F32), shp((1, width), F32)],
                 compiler_params=_params(2))(up_g, up_g, up_g, up_v, up_v, up_v, gate_c, val_c, d_act)


def _conv_plain(x, w, name):
    n_tok, width = x.shape
    tb, cb, main, prev, nxt, par = _conv_geometry(n_tok, width)
    n_blk = n_tok // tb

    def body(xp, xm, xn, w_ref, o_ref):
        taps = _conv_taps(xp[...], xm[...], xn[...], pl.program_id(1), n_blk)
        o_ref[...] = _conv_sum(taps, w_ref).astype(o_ref.dtype)

    return _call(body, name=name, grid=(width // cb, n_tok // tb), in_specs=[prev, main, nxt, par(9)],
                 out_specs=main, out_shape=jax.ShapeDtypeStruct((n_tok, width), BF16),
                 compiler_params=_params(2))(x, x, x, w)


def _adamw(w, g_parts, m, v, name, own=None, me=None):
    rows, cols = w.shape
    parts = g_parts.shape[0]
    blk = _pick(rows, 256, 8)
    spec = pl.BlockSpec((blk, cols), lambda i: (i, 0))
    has_own = own is not None

    def body(*refs):
        w_ref, g_ref, m_ref, v_ref = refs[:4]
        g_out, d_out, m_out, v_out = refs[-4:]

        def part(p):
            if has_own:
                return jnp.where(refs[5][...] == p, refs[4][...], g_ref[p]).astype(F32)
            return g_ref[p].astype(F32)

        g = part(0)
        for p in range(1, parts):
            g = g + part(p)
        m_new = ADAM_B1 * m_ref[...] + (1.0 - ADAM_B1) * g
        v_new = ADAM_B2 * v_ref[...] + (1.0 - ADAM_B2) * (g * g)
        m_hat = m_new / (1.0 - ADAM_B1 ** ADAM_STEP)
        v_hat = v_new / (1.0 - ADAM_B2 ** ADAM_STEP)
        g_out[...] = g
        d_out[...] = -ADAM_LR * (m_hat / (jnp.sqrt(v_hat) + ADAM_EPS) + ADAM_WD * w_ref[...])
        m_out[...] = m_new
        v_out[...] = v_new

    extra = [own, me] if has_own else []
    return _call(body, name=name, grid=(rows // blk,),
                 in_specs=([spec, pl.BlockSpec((parts, blk, cols), lambda i: (0, i, 0)), spec, spec]
                           + ([spec, _whole_spec((1, 1))] if has_own else [])),
                 out_specs=[spec] * 4, out_shape=[jax.ShapeDtypeStruct((rows, cols), F32)] * 4,
                 compiler_params=_params(1))(w, g_parts, m, v, *extra)


def _exchange(items, name):
    n = len(items)
    hbm = pl.BlockSpec(memory_space=pl.ANY)

    def body(*refs):
        srcs, outs = refs[:n], refs[n:2 * n]
        send_sems, recv_sems, own_sems = refs[2 * n:]
        x, y, c = lax.axis_index("x"), lax.axis_index("y"), lax.axis_index("c")
        me = 4 * x + 2 * y + c
        own = []
        for i, (_, mode) in enumerate(items):
            src = srcs[i] if mode == "gather" else srcs[i].at[me]
            cp = pltpu.make_async_copy(src, outs[i].at[me], own_sems.at[i])
            cp.start()
            own.append(cp)
        sent = []
        for i, (_, mode) in enumerate(items):
            for k in range(1, N_DEV):
                px = 1 - x if k & 4 else x
                py = 1 - y if k & 2 else y
                pc = 1 - c if k & 1 else c
                peer = 4 * px + 2 * py + pc
                src = srcs[i] if mode == "gather" else srcs[i].at[peer]
                cp = pltpu.make_async_remote_copy(
                    src_ref=src, dst_ref=outs[i].at[me], send_sem=send_sems.at[i, k - 1],
                    recv_sem=recv_sems.at[i, k - 1], device_id=(px, py, pc), device_id_type=pl.DeviceIdType.MESH)
                cp.start()
                landing = pltpu.make_async_remote_copy(
                    src_ref=src, dst_ref=outs[i].at[peer], send_sem=send_sems.at[i, k - 1],
                    recv_sem=recv_sems.at[i, k - 1], device_id=(px, py, pc), device_id_type=pl.DeviceIdType.MESH)
                sent.append((cp, landing))
        for cp in own:
            cp.wait()
        for cp, landing in sent:
            cp.wait_send()
            landing.wait_recv()

    out_shape = [jax.ShapeDtypeStruct((N_DEV,) + (a.shape if mode == "gather" else a.shape[1:]), a.dtype)
                 for a, mode in items]
    return _call(body, name=name, in_specs=[hbm] * n, out_specs=[hbm] * n, out_shape=out_shape,
                 scratch_shapes=[pltpu.SemaphoreType.DMA((n, N_DEV - 1)), pltpu.SemaphoreType.DMA((n, N_DEV - 1)),
                                 pltpu.SemaphoreType.DMA((n,))])(*[a for a, _ in items])


def _peer(k, x, y, c):
    px = 1 - x if k & 4 else x
    py = 1 - y if k & 2 else y
    pc = 1 - c if k & 1 else c
    return (px, py, pc), 4 * px + 2 * py + pc


_HBM_SPEC = pl.BlockSpec(memory_space=pltpu.HBM)
_SEM_SPEC = pl.BlockSpec(memory_space=pltpu.SEMAPHORE)
_SPLIT_EFFECT = pltpu.SideEffectType.DATAFLOW_SIDE_EFFECTING


def _exchange_begin(items, name, after=None):
    n = len(items)
    modes = [mode for _, mode in items]
    srcs = [pltpu.with_memory_space_constraint(a, pltpu.HBM) for a, _ in items]
    lands = [pltpu.with_memory_space_constraint(
        lax.empty((N_DEV,) + (a.shape if mode == "gather" else a.shape[1:]), a.dtype), pltpu.HBM)
        for a, mode in items]

    def body(*refs):
        src_refs, land_refs = refs[:n], refs[n:2 * n]
        token = refs[-1]
        send_sems, recv_sems = refs[-2 * n - 3], refs[-2 * n - 2]
        x, y, c = lax.axis_index("x"), lax.axis_index("y"), lax.axis_index("c")
        me = 4 * x + 2 * y + c
        for i in range(n):
            for k in range(1, N_DEV):
                coords, peer = _peer(k, x, y, c)
                src = src_refs[i] if modes[i] == "gather" else src_refs[i].at[peer]
                pltpu.make_async_remote_copy(
                    src_ref=src, dst_ref=land_refs[i].at[me], send_sem=send_sems.at[i * (N_DEV - 1) + k - 1],
                    recv_sem=recv_sems.at[i * (N_DEV - 1) + k - 1], device_id=coords,
                    device_id_type=pl.DeviceIdType.MESH).start()
        token[...] = jnp.zeros_like(token)

    sems = pltpu.SemaphoreType.DMA((n * (N_DEV - 1),))
    order = [] if after is None else [after]
    res = _call(body, name=name,
                out_shape=(sems, sems, *[pltpu.HBM(a.shape, a.dtype) for a in srcs + lands],
                           jax.ShapeDtypeStruct((8, LANE), F32)),
                in_specs=[_HBM_SPEC] * (2 * n) + [pl.BlockSpec(memory_space=pl.ANY)] * len(order),
                out_specs=(_SEM_SPEC, _SEM_SPEC, *[_HBM_SPEC] * (2 * n), pl.BlockSpec(memory_space=pltpu.VMEM)),
                input_output_aliases={i: 2 + i for i in range(2 * n)},
                compiler_params=pltpu.CompilerParams(has_side_effects=_SPLIT_EFFECT))(*srcs, *lands, *order)
    return (modes, res[0], res[1], list(res[2:2 + n]), list(res[2 + n:2 + 2 * n])), res[-1][0, 0]


def _exchange_end(handle, after, name):
    modes, send_sems, recv_sems, srcs, lands = handle
    n = len(modes)

    def wait_body(*refs):
        src_refs, land_refs = refs[:n], refs[n:2 * n]
        send, recv = refs[2 * n], refs[2 * n + 1]
        x, y, c = lax.axis_index("x"), lax.axis_index("y"), lax.axis_index("c")
        for i in range(n):
            for k in range(1, N_DEV):
                coords, peer = _peer(k, x, y, c)
                src = src_refs[i] if modes[i] == "gather" else src_refs[i].at[peer]
                cp = pltpu.make_async_remote_copy(
                    src_ref=src, dst_ref=land_refs[i].at[peer], send_sem=send.at[i * (N_DEV - 1) + k - 1],
                    recv_sem=recv.at[i * (N_DEV - 1) + k - 1], device_id=coords,
                    device_id_type=pl.DeviceIdType.MESH)
                cp.wait_send()
                cp.wait_recv()

    res = _call(wait_body, name=name, out_shape=[pltpu.HBM(a.shape, a.dtype) for a in srcs + lands],
                in_specs=[_HBM_SPEC] * (2 * n) + [_SEM_SPEC, _SEM_SPEC, pl.BlockSpec(memory_space=pl.ANY)],
                out_specs=[_HBM_SPEC] * (2 * n), input_output_aliases={i: i for i in range(2 * n)},
                compiler_params=pltpu.CompilerParams(has_side_effects=_SPLIT_EFFECT))(
                    *srcs, *lands, send_sems, recv_sems, after)
    return res[n:]


def _with_own(land, own, me):
    slot = lax.broadcasted_iota(jnp.int32, (N_DEV,) + (1,) * (land.ndim - 1), 0)
    return jnp.where(slot == me, own[None], land)


def _cols_from_blocks(g):
    return jnp.transpose(g, (1, 0, 2)).reshape(g.shape[1], N_DEV * g.shape[2])


def _blocks_from_cols(w):
    r, c8 = w.shape
    return jnp.transpose(w.reshape(r, N_DEV, c8 // N_DEV), (1, 0, 2))


def _pack(arrs):
    flat = jnp.concatenate([a.reshape(-1).astype(F32) for a in arrs])
    pad = (-flat.shape[0]) % (8 * LANE)
    return jnp.pad(flat, (0, pad)).reshape(-1, LANE)


def _unpack(packed, shapes):
    flat = packed.reshape(-1)
    out, off = [], 0
    for s in shapes:
        size = math.prod(s)
        out.append(flat[off:off + size].reshape(s))
        off += size
    return out


def kernel(x, c, ctx, c_ctx, w_ada, b_ada, g_mix, w_in, s5_a_re, s5_a_im, s5_log_step, s5_b_re, s5_b_im, s5_c_re, s5_c_im, s5_d, s5_w_glu, s5_b_glu, sgu_ln_g, sgu_ln_b, sgu_w, sgu_b, w_proj_a, w_proj_b, b_gate, w_out, g_ffn, w_up, conv_w, conv_b, w_down, g_final, loss_target, m_c_ctx, m_w_ada, m_b_ada, m_g_mix, m_w_in, m_s5_a_re, m_s5_a_im, m_s5_log_step, m_s5_b_re, m_s5_b_im, m_s5_c_re, m_s5_c_im, m_s5_d, m_s5_w_glu, m_s5_b_glu, m_sgu_ln_g, m_sgu_ln_b, m_sgu_w, m_sgu_b, m_w_proj_a, m_w_proj_b, m_b_gate, m_w_out, m_g_ffn, m_w_up, m_conv_w, m_conv_b, m_w_down, m_g_final, v_c_ctx, v_w_ada, v_b_ada, v_g_mix, v_w_in, v_s5_a_re, v_s5_a_im, v_s5_log_step, v_s5_b_re, v_s5_b_im, v_s5_c_re, v_s5_c_im, v_s5_d, v_s5_w_glu, v_s5_b_glu, v_sgu_ln_g, v_sgu_ln_b, v_sgu_w, v_sgu_b, v_w_proj_a, v_w_proj_b, v_b_gate, v_w_out, v_g_ffn, v_w_up, v_conv_w, v_conv_b, v_w_down, v_g_final):
    given = dict(locals())
    wts = {n: given[n] for n in WEIGHTS}
    mom1 = {n: given["m_" + n] for n in WEIGHTS}
    mom2 = {n: given["v_" + n] for n in WEIGHTS}

    me = 4 * lax.axis_index("x") + 2 * lax.axis_index("y") + lax.axis_index("c")
    xs, cx, tgt = x[0], ctx[0], loss_target[0]
    n_tok, d = xs.shape
    n_ctx = cx.shape[0]
    s5w = s5_d.shape[1]
    ffn = w_down.shape[1] * N_DEV
    n_mod = w_ada.shape[2] * N_DEV // d
    mod_cols = w_ada.shape[2]

    def two_d(a):
        return a.reshape(-1, a.shape[-1])

    conv_w9 = conv_w[0].reshape(9, -1)
    gathered = _exchange(
        [(c, "gather"), (_b16(w_in[0]), "gather"), (_b16(s5_w_glu[0]), "gather"), (_b16(w_proj_a[0]), "gather"),
         (_b16(w_proj_b[0]), "gather")], "gather_weights")
    c_all = gathered[0].reshape(N_DEV, d)
    w_in_f = _cols_from_blocks(gathered[1])
    w_in_u, w_in_rest = w_in_f[:, :s5w], w_in_f[:, s5w:]
    w_glu_f = gathered[2].reshape(-1, s5w)
    w_pa_f = _cols_from_blocks(gathered[3])
    w_pb_f = _cols_from_blocks(gathered[4])

    cs_in = jnp.concatenate([c_all, jnp.broadcast_to(c_ctx[None, :], (N_DEV, d))], axis=0)
    w_ada_loc = w_ada[0]
    (mod_mine,) = _stage_fwd(_fn_mod, [cs_in], [w_ada_loc], [(mod_cols, F32)], blk=2 * N_DEV, name="mod_fwd")
    (mod_blocks,) = _exchange([(mod_mine, "gather")], "gather_mod")
    late_own = [_b16(w_out[0]), _b16(w_up[0]), conv_w9, _b16(w_down[0])]
    late_weights, tok = _exchange_begin([(a, "gather") for a in late_own], "gather_late_begin", after=mod_blocks)
    mod_all = _cols_from_blocks(mod_blocks) + b_ada + tok
    mod = lax.dynamic_slice_in_dim(mod_all, me, 1, axis=0)
    mod_c = mod_all[N_DEV:N_DEV + 1]
    sh1, sc1, ga1, sh2, sc2, ga2 = [mod[:, i * d:(i + 1) * d] for i in range(n_mod)]
    sh1c, sc1c = mod_c[:, :d], mod_c[:, d:2 * d]

    a_par = [g_mix, sh1, sc1]
    ac_par = [g_mix, sh1c, sc1c]
    (h,) = _stage_fwd(_fn_a, [xs], a_par, [(d, BF16)], blk=_pick(n_tok, 512, 8), name="modulate1_fwd")
    (hc,) = _stage_fwd(_fn_a, [cx], ac_par, [(d, BF16)], blk=_pick(n_ctx, 512, 8), name="modulate1_ctx_fwd")
    pu = _mm(h, w_in_u, name="proj_u")
    prest = _mm(h, w_in_rest, name="proj_rest")
    puc = _mm(hc, w_in_u, name="proj_u_ctx")

    s5_prm = [s5_a_re[0][:, :, None, :], s5_a_im[0][:, :, None, :], s5_log_step[0][:, :, None, None],
              jnp.swapaxes(s5_b_re[0], 2, 3), jnp.swapaxes(s5_b_im[0], 2, 3), s5_c_re[0], s5_c_im[0]]
    pu_g, puc_g = _to_groups(pu), _to_groups(puc)
    ysc = _from_groups(_s5_fwd(pu_g, puc_g, s5_prm))

    b_par = [s5_d, w_glu_f, s5_b_glu, sgu_ln_g, sgu_ln_b, two_d(sgu_w[0]), jnp.transpose(sgu_b[0]),
             w_pa_f, w_pb_f, b_gate]
    b_rows = [pu, prest, ysc]
    b_blk = _pick(n_tok, 512, CHUNK)
    (mpre,) = _stage_fwd(_fn_b, b_rows, b_par, [(d, BF16)], blk=b_blk, name="mixers_fwd")
    late = [_with_own(land, own, me) for land, own in zip(_exchange_end(late_weights, mpre, "gather_late_end"),
                                                          late_own)]
    w_out_f = late[0].reshape(-1, d)
    w_up_f = _cols_from_blocks(late[1])
    w_up_g, w_up_v = w_up_f[:, :ffn], w_up_f[:, ffn:]
    conv_w_f = _cols_from_blocks(late[2])
    w_down_f = late[3].reshape(-1, d)
    mo = _mm(mpre, w_out_f, name="out_proj")

    c_par = [ga1, g_ffn, sh2, sc2]
    c_blk = _pick(n_tok, 512, 8)
    x1, h2 = _stage_fwd(_fn_c, [xs, mo], c_par, [(d, F32), (d, BF16)], blk=c_blk, name="modulate2_fwd")
    up_g = _mm(h2, w_up_g, out_dtype=BF16, name="up_gate")
    up_v = _mm(h2, w_up_v, out_dtype=BF16, name="up_val")
    cw_g, cw_v = conv_w_f[:, :ffn], conv_w_f[:, ffn:]
    cb_g, cb_v = conv_b[:, :ffn], conv_b[:, ffn:]
    act, gate_c, val_c = _conv_act_fwd(up_g, up_v, cw_g, cw_v, cb_g, cb_v)
    dn = _mm(act, w_down_f, name="down_proj")

    loss_part, d_x1a, d_dn, d_ga2, d_g_final = _stage_loss(
        _fn_e, [x1, dn, tgt], [ga2, g_final[None, :]], blk=c_blk, name="loss_head",
        row_grads=[(0, F32), (1, BF16)])

    d_act = _mm(d_dn, w_down_f, tb=True, name="down_proj_dx")
    g_w_down = _mm(act, d_dn, ta=True, name="down_proj_dw")

    own_block = {}

    def grad_blocks(name, g):
        blocks = _b16(_blocks_from_cols(g) if name in COL_SHARDED
                      else g.reshape(N_DEV, g.shape[0] // N_DEV, g.shape[1]))
        own_block[name] = lax.dynamic_index_in_dim(blocks, me, 0, keepdims=False)
        return blocks, "a2a"

    sent_down, tok = _exchange_begin([grad_blocks('w_down', g_w_down)], "grads_down_begin")
    dcg, dcv, g_cw_g, g_cw_v, g_cb_g, g_cb_v = _conv_act_bwd(up_g, up_v, gate_c, val_c, d_act)
    dug = _conv_plain(dcg, cw_g[::-1] + tok, "conv_dx_gate")
    duv = _conv_plain(dcv, cw_v[::-1], "conv_dx_val")
    d_h2 = _mm(dug, w_up_g, tb=True, name="up_gate_dx")
    d_h2 = _mm(duv, w_up_v, tb=True, add=d_h2, out_dtype=BF16, name="up_val_dx")
    g_w_up = jnp.concatenate([_mm(h2, dug, ta=True, name="up_gate_dw"), _mm(h2, duv, ta=True, name="up_val_dw")],
                             axis=1)
    sent_up, tok = _exchange_begin(
        [grad_blocks('w_up', g_w_up), grad_blocks('conv_w', jnp.concatenate([g_cw_g, g_cw_v], axis=1))],
        "grads_up_begin")
    (d_xc, d_mo), (d_ga1, g_g_ffn, d_sh2, d_sc2) = _split(_stage_bwd(
        _fn_c, [xs, mo], [ga1 + tok] + c_par[1:], [d_x1a, d_h2], blk=c_blk, name="modulate2_bwd",
        row_grads=[(0, F32), (1, BF16)]), 2)

    d_mpre = _mm(d_mo, w_out_f, tb=True, out_dtype=BF16, name="out_proj_dx")
    g_w_out = _mm(mpre, d_mo, ta=True, name="out_proj_dw")
    (d_prest, d_ysc), b_grads = _split(_stage_bwd(
        _fn_b, b_rows, b_par, [d_mpre], blk=_pick(n_tok, 256, CHUNK), name="mixers_bwd",
        row_grads=[(1, BF16), (2, F32)]), 2)
    (g_s5_d, g_w_glu, g_b_glu, g_ln_g, g_ln_b, g_sgu_w, g_sgu_bt, g_w_pa, g_w_pb, g_b_gate) = b_grads

    sent_mix, tok = _exchange_begin(
        [grad_blocks('w_out', g_w_out), grad_blocks('s5_w_glu', g_w_glu), grad_blocks('w_proj_a', g_w_pa),
         grad_blocks('w_proj_b', g_w_pb)], "grads_mixer_begin")
    skip_g = jnp.tile(s5_d.reshape(-1, 1, S5_H), (1, 1, S5_T)) + tok
    s5_out = _s5_bwd(pu_g, puc_g, s5_prm, _to_groups(d_ysc), skip_g)
    g_a_re, g_a_im, g_ls, g_bt_re, g_bt_im, g_c_re, g_c_im = s5_out[2:]

    part = {
        's5_a_re': g_a_re, 's5_a_im': g_a_im, 's5_log_step': g_ls,
        's5_b_re': jnp.swapaxes(g_bt_re, 2, 3), 's5_b_im': jnp.swapaxes(g_bt_im, 2, 3),
        's5_c_re': g_c_re, 's5_c_im': g_c_im, 's5_d': g_s5_d, 's5_b_glu': g_b_glu, 'sgu_ln_g': g_ln_g,
        'sgu_ln_b': g_ln_b, 'sgu_w': g_sgu_w, 'sgu_b': jnp.transpose(g_sgu_bt), 'b_gate': g_b_gate,
        'g_ffn': g_g_ffn, 'conv_b': jnp.concatenate([g_cb_g, g_cb_v], axis=1), 'g_final': d_g_final,
    }
    early = [n for n in REPLICATED if n not in LATE_REPLICATED]
    early_part = _pack([part[n] for n in early])
    sent_small, tok = _exchange_begin([(early_part, "gather")], "grads_small_begin")
    d_pu, d_puc = _b16(_from_groups(s5_out[0])), _from_groups(s5_out[1]) + tok

    d_h = _mm(d_pu, w_in_u, tb=True, name="proj_u_dx")
    d_h = _mm(d_prest, w_in_rest, tb=True, add=d_h, out_dtype=BF16, name="proj_rest_dx")
    d_hc = _mm(d_puc, w_in_u, tb=True, out_dtype=BF16, name="proj_u_ctx_dx")
    g_w_in_u = _mm(hc, d_puc, ta=True, name="proj_u_ctx_dw")
    g_w_in_u = _mm(h, d_pu, ta=True, add=g_w_in_u, name="proj_u_dw")
    g_w_in = jnp.concatenate([g_w_in_u, _mm(h, d_prest, ta=True, name="proj_rest_dw")], axis=1)
    sent_in, tok = _exchange_begin([grad_blocks('w_in', g_w_in)], "grads_in_begin")

    (grad_x,), (g_g_mix_x, d_sh1, d_sc1) = _split(_stage_bwd(
        _fn_a_res, [xs], [g_mix + tok] + a_par[1:], [d_h, d_xc], blk=c_blk, name="modulate1_bwd",
        row_grads=[(0, F32)]), 1)
    _, (g_g_mix_c, d_sh1c, d_sc1c) = _split(_stage_bwd(
        _fn_a, [cx], ac_par, [d_hc], blk=_pick(n_ctx, 512, 8), name="modulate1_ctx_bwd", row_grads=[]), 0)

    zeros = jnp.zeros((1, (n_mod - 2) * d), F32)
    d_mod = jnp.concatenate([d_sh1, d_sc1, d_ga1, d_sh2, d_sc2, d_ga2], axis=1)
    d_mod_c = jnp.concatenate([d_sh1c, d_sc1c, zeros], axis=1)
    (d_mod_all,) = _exchange([(jnp.concatenate([d_mod, d_mod_c], axis=0), "gather")], "gather_dmod")
    d_mod_rows = jnp.transpose(d_mod_all, (1, 0, 2)).reshape(2 * N_DEV, n_mod * d)
    d_mod_mine = lax.dynamic_slice_in_dim(d_mod_rows, me * mod_cols, mod_cols, axis=1)
    (d_cs,), (g_w_ada,) = _split(_stage_bwd(
        _fn_mod, [cs_in], [w_ada_loc], [d_mod_mine], blk=2 * N_DEV, name="mod_bwd", row_grads=[(0, F32)]), 1)

    part.update({'c_ctx': jnp.sum(d_cs[N_DEV:], axis=0), 'b_ada': d_mod + d_mod_c, 'g_mix': g_g_mix_x + g_g_mix_c})
    late_parts, loss_parts = _exchange(
        [(_pack([part[n] for n in LATE_REPLICATED]), "gather"), (jnp.broadcast_to(loss_part, (8, LANE)), "gather")],
        "exchange_grads")

    summed = {}
    (early_parts,) = _exchange_end(sent_small, late_parts, "grads_small_end")
    (summed['w_down'],) = _exchange_end(sent_down, late_parts, "grads_down_end")
    summed['w_up'], summed['conv_w'] = _exchange_end(sent_up, late_parts, "grads_up_end")
    summed['w_out'], summed['s5_w_glu'], summed['w_proj_a'], summed['w_proj_b'] = _exchange_end(
        sent_mix, late_parts, "grads_mixer_end")
    (summed['w_in'],) = _exchange_end(sent_in, late_parts, "grads_in_end")

    out = {}
    me_arr = me.reshape(1, 1).astype(jnp.int32)
    for names, parts, own, tag in ((early, early_parts, early_part, "early"),
                                   (LATE_REPLICATED, late_parts, None, "late")):
        res = _adamw(_pack([wts[n] for n in names]), parts, _pack([mom1[n] for n in names]),
                     _pack([mom2[n] for n in names]), "adamw_replicated_" + tag, own=own,
                     me=None if own is None else me_arr)
        res = [_unpack(r, [wts[n].shape for n in names]) for r in res]
        for i, n in enumerate(names):
            out[n] = tuple(r[i] for r in res)
    for n, parts in summed.items():
        shape = wts[n].shape
        res = _adamw(two_d(wts[n]), parts, two_d(mom1[n]), two_d(mom2[n]), "adamw_" + n, own=own_block[n],
                     me=me_arr)
        out[n] = tuple(r.reshape(shape) for r in res)
    res = _adamw(w_ada_loc, g_w_ada[None], m_w_ada[0], v_w_ada[0], "adamw_w_ada")
    out['w_ada'] = tuple(r.reshape(w_ada.shape) for r in res)

    loss = jnp.sum(loss_parts[:, 0, 0])
    return (loss, grad_x[None], *[out[n][0] for n in WEIGHTS], *[out[n][1] for n in WEIGHTS],
            *[out[n][2] for n in WEIGHTS], *[out[n][3] for n in WEIGHTS])


def _split(res, n_rows):
    return tuple(res[:n_rows]), tuple(res[n_rows:])
```

```python
import functools
import math

import jax
import jax.numpy as jnp
from jax import lax
from jax.experimental import pallas as pl
from jax.experimental.pallas import tpu as pltpu

F32 = jnp.float32
BF16 = jnp.bfloat16
HI = lax.Precision.HIGHEST

N_DEV = 8
GRID_W = 64
CHUNK = 128
EPS = 1e-6
S5_T = 32
S5_H = 16
LANE = 128
HALO = 128
VMEM_LIMIT = 56 * 1024 * 1024

ADAM_LR = 0.001
ADAM_B1 = 0.9
ADAM_B2 = 0.999
ADAM_EPS = 1e-08
ADAM_WD = 0.01
ADAM_STEP = 10

WEIGHTS = ['c_ctx', 'w_ada', 'b_ada', 'g_mix', 'w_in', 's5_a_re', 's5_a_im', 's5_log_step', 's5_b_re', 's5_b_im',
           's5_c_re', 's5_c_im', 's5_d', 's5_w_glu', 's5_b_glu', 'sgu_ln_g', 'sgu_ln_b', 'sgu_w', 'sgu_b',
           'w_proj_a', 'w_proj_b', 'b_gate', 'w_out', 'g_ffn', 'w_up', 'conv_w', 'conv_b', 'w_down', 'g_final']
COL_SHARDED = ('w_ada', 'w_in', 'w_proj_a', 'w_proj_b', 'w_up', 'conv_w')
ROW_SHARDED = ('s5_w_glu', 'w_out', 'w_down')
SHARDED = COL_SHARDED + ROW_SHARDED
REPLICATED = [n for n in WEIGHTS if n not in SHARDED]
LATE_REPLICATED = ['c_ctx', 'b_ada', 'g_mix']
UNPACKED_REPLICATED = ['sgu_w', 's5_c_re', 's5_c_im']


def _call(body, **kw):
    return pl.pallas_call(body, **kw)


def _params(n_grid):
    return pltpu.CompilerParams(dimension_semantics=("arbitrary",) * n_grid, vmem_limit_bytes=VMEM_LIMIT)


def _pick(dim, pref, unit=LANE):
    best = None
    d = unit
    while d <= min(dim, pref):
        if dim % d == 0:
            best = d
        d += unit
    return best if best is not None else dim


def _dg(a, b, ca, cb, prec=None):
    return lax.dot_general(a, b, (((ca,), (cb,)), ((), ())), precision=prec, preferred_element_type=F32)


def _b16(v):
    return v.astype(BF16)


@jax.custom_vjp
def mmb(a, b):
    return _dg(_b16(a), _b16(b), 1, 0)


def _mmb_fwd(a, b):
    return mmb(a, b), (a, b)


def _mmb_bwd(res, g):
    a, b = res
    g = _b16(g)
    return _dg(g, _b16(b), 1, 1).astype(a.dtype), _dg(_b16(a), g, 0, 0).astype(b.dtype)


mmb.defvjp(_mmb_fwd, _mmb_bwd)


@jax.custom_vjp
def mmb_nt(a, b):
    return _dg(_b16(a), _b16(b), 1, 1)


def _mmb_nt_fwd(a, b):
    return mmb_nt(a, b), (a, b)


def _mmb_nt_bwd(res, g):
    a, b = res
    g = _b16(g)
    return _dg(g, _b16(b), 1, 0).astype(a.dtype), _dg(g, _b16(a), 0, 0).astype(b.dtype)


mmb_nt.defvjp(_mmb_nt_fwd, _mmb_nt_bwd)


@jax.custom_vjp
def mmf(a, b):
    return _dg(a, b, 1, 0, HI)


def _mmf_fwd(a, b):
    return mmf(a, b), (a, b)


def _mmf_bwd(res, g):
    a, b = res
    return _dg(g, b, 1, 1, HI), _dg(a, g, 0, 0, HI)


mmf.defvjp(_mmf_fwd, _mmf_bwd)


def _dg3(a, b, ca, cb):
    ah, bh = _b16(a), _b16(b)
    al, bl = _b16(a - ah.astype(F32)), _b16(b - bh.astype(F32))
    return _dg(ah, bh, ca, cb) + _dg(ah, bl, ca, cb) + _dg(al, bh, ca, cb)


@jax.custom_vjp
def mm3_nt(a, b):
    return _dg3(a, b, 1, 1)


def _mm3_nt_fwd(a, b):
    return mm3_nt(a, b), (a, b)


def _mm3_nt_bwd(res, g):
    a, b = res
    return _dg3(g, b, 1, 0), _dg3(g, a, 0, 0)


mm3_nt.defvjp(_mm3_nt_fwd, _mm3_nt_bwd)


def _shift_impl(x, k, up):
    n = x.shape[0]
    idx = lax.broadcasted_iota(jnp.int32, (n, 1), 0)
    if up:
        return jnp.where(idx < n - k, pltpu.roll(x, n - k, 0), 0.0)
    return jnp.where(idx >= k, pltpu.roll(x, k, 0), 0.0)


@functools.partial(jax.custom_vjp, nondiff_argnums=(1, 2))
def _shift(x, k, up):
    return _shift_impl(x, k, up)


def _shift_fwd(x, k, up):
    return _shift_impl(x, k, up), None


def _shift_bwd(k, up, _, g):
    return (_shift_impl(g, k, not up),)


_shift.defvjp(_shift_fwd, _shift_bwd)


def _mm(a, b, *, name, ta=False, tb=False, add=None, out_dtype=F32, tm_pref=2048, tn_pref=1408, tk_pref=1408):
    m, k = (a.shape[1], a.shape[0]) if ta else a.shape
    n = b.shape[0] if tb else b.shape[1]
    if add is not None and out_dtype != F32:
        tm_pref = min(tm_pref, 1024)
    tm, tn, tk = _pick(m, tm_pref), _pick(n, tn_pref), _pick(k, tk_pref)
    nk = k // tk
    a_spec = (pl.BlockSpec((tk, tm), lambda i, j, kk: (kk, i)) if ta
              else pl.BlockSpec((tm, tk), lambda i, j, kk: (i, kk)))
    b_spec = (pl.BlockSpec((tn, tk), lambda i, j, kk: (j, kk)) if tb
              else pl.BlockSpec((tk, tn), lambda i, j, kk: (kk, j)))
    o_spec = pl.BlockSpec((tm, tn), lambda i, j, kk: (i, j))
    ca, cb = (0 if ta else 1), (1 if tb else 0)
    has_add = add is not None
    in_place = out_dtype == F32 or nk == 1

    def body(*refs):
        a_ref, b_ref = refs[0], refs[1]
        o_ref = refs[3] if has_add else refs[2]
        prod = _dg(_b16(a_ref[...]), _b16(b_ref[...]), ca, cb)
        if nk == 1:
            if has_add:
                prod = prod + refs[2][...].astype(F32)
            o_ref[...] = prod.astype(o_ref.dtype)
            return
        acc_ref = o_ref if in_place else refs[-1]
        kk = pl.program_id(2)

        @pl.when(kk == 0)
        def _():
            acc_ref[...] = prod

        @pl.when(kk > 0)
        def _():
            acc_ref[...] += prod

        if has_add or not in_place:
            @pl.when(kk == nk - 1)
            def _():
                r = acc_ref[...]
                if has_add:
                    r = r + refs[2][...].astype(F32)
                o_ref[...] = r.astype(o_ref.dtype)

    ins = [a, b] + ([add] if has_add else [])
    in_specs = [a_spec, b_spec] + ([o_spec] if has_add else [])
    return _call(body, name=name, grid=(m // tm, n // tn, nk), in_specs=in_specs, out_specs=o_spec,
                 out_shape=jax.ShapeDtypeStruct((m, n), out_dtype),
                 scratch_shapes=[] if in_place else [pltpu.VMEM((tm, tn), F32)],
                 compiler_params=_params(3))(*ins)


def _row_spec(blk, width):
    return pl.BlockSpec((blk, width), lambda i: (i, 0))


def _whole_spec(shape):
    return pl.BlockSpec(shape, lambda i: (0,) * len(shape))


def _stage_fwd(fn, rows, params, outs, *, blk, name, n_rows=None):
    n = n_rows or rows[0].shape[0]
    nr, npar = len(rows), len(params)

    def body(*refs):
        vals = [r[...] for r in refs[:nr + npar]]
        res = fn(*vals)
        for o_ref, v in zip(refs[nr + npar:], res):
            o_ref[...] = v.astype(o_ref.dtype)

    return _call(body, name=name, grid=(n // blk,),
                 in_specs=[_row_spec(blk, r.shape[1]) for r in rows] + [_whole_spec(p.shape) for p in params],
                 out_specs=[_row_spec(blk, w) for w, _ in outs],
                 out_shape=[jax.ShapeDtypeStruct((n, w), dt) for w, dt in outs],
                 compiler_params=_params(1))(*rows, *params)


def _stage_bwd(fn, rows, params, cts, *, blk, name, row_grads, n_rows=None):
    n = n_rows or rows[0].shape[0]
    nr, npar, nct = len(rows), len(params), len(cts)

    def body(*refs):
        vals = [r[...].astype(F32) for r in refs[:nr + npar]]
        ct = [r[...] for r in refs[nr + npar:nr + npar + nct]]
        d_rows = refs[nr + npar + nct:nr + npar + nct + len(row_grads)]
        d_par = refs[nr + npar + nct + len(row_grads):]
        res, vjp = jax.vjp(fn, *vals)
        g = vjp(tuple(c.astype(r.dtype) for c, r in zip(ct, res)))
        for o_ref, (j, _) in zip(d_rows, row_grads):
            o_ref[...] = g[j].astype(o_ref.dtype)

        @pl.when(pl.program_id(0) == 0)
        def _():
            for o_ref in d_par:
                o_ref[...] = jnp.zeros_like(o_ref)

        for j, o_ref in enumerate(d_par):
            o_ref[...] += g[nr + j].astype(F32)

    return _call(body, name=name, grid=(n // blk,),
                 in_specs=([_row_spec(blk, r.shape[1]) for r in rows] + [_whole_spec(p.shape) for p in params]
                           + [_row_spec(blk, c.shape[1]) for c in cts]),
                 out_specs=([_row_spec(blk, rows[j].shape[1]) for j, _ in row_grads]
                            + [_whole_spec(p.shape) for p in params]),
                 out_shape=([jax.ShapeDtypeStruct((n, rows[j].shape[1]), dt) for j, dt in row_grads]
                            + [jax.ShapeDtypeStruct(p.shape, F32) for p in params]),
                 compiler_params=_params(1))(*rows, *params, *cts)


def _stage_loss(fn, rows, params, *, blk, name, row_grads):
    n = rows[0].shape[0]
    nr, npar = len(rows), len(params)

    def body(*refs):
        vals = [r[...].astype(F32) for r in refs[:nr + npar]]
        loss_ref = refs[nr + npar]
        d_rows = refs[nr + npar + 1:nr + npar + 1 + len(row_grads)]
        d_par = refs[nr + npar + 1 + len(row_grads):]
        res, vjp = jax.vjp(fn, *vals)
        g = vjp(jnp.ones_like(res))
        for o_ref, (j, _) in zip(d_rows, row_grads):
            o_ref[...] = g[j].astype(o_ref.dtype)

        @pl.when(pl.program_id(0) == 0)
        def _():
            loss_ref[...] = jnp.zeros_like(loss_ref)
            for o_ref in d_par:
                o_ref[...] = jnp.zeros_like(o_ref)

        loss_ref[...] += res
        for j, o_ref in enumerate(d_par):
            o_ref[...] += g[nr + j].astype(F32)

    return _call(body, name=name, grid=(n // blk,),
                 in_specs=[_row_spec(blk, r.shape[1]) for r in rows] + [_whole_spec(p.shape) for p in params],
                 out_specs=([_whole_spec((1, 1))] + [_row_spec(blk, rows[j].shape[1]) for j, _ in row_grads]
                            + [_whole_spec(p.shape) for p in params]),
                 out_shape=([jax.ShapeDtypeStruct((1, 1), F32)]
                            + [jax.ShapeDtypeStruct((n, rows[j].shape[1]), dt) for j, dt in row_grads]
                            + [jax.ShapeDtypeStruct(p.shape, F32) for p in params]),
                 compiler_params=_params(1))(*rows, *params)


def _rms(x, g):
    return x * lax.rsqrt(jnp.mean(x * x, axis=-1, keepdims=True) + EPS) * g


def _modulate(x, g, shift, scale):
    return _rms(x, g) * (1.0 + scale) + shift


def _fn_mod(cs, w_ada):
    return (mmb(jax.nn.silu(cs), w_ada),)


def _fn_a(x, g_mix, sh, sc):
    return (_b16(_modulate(x, g_mix, sh, sc)),)


def _fn_a_res(x, g_mix, sh, sc):
    return _b16(_modulate(x, g_mix, sh, sc)), x


def _sgu_spatial(v, sgu_w, sgu_bt):
    rows, width = v.shape
    gdim = width // (sgu_w.shape[0] // CHUNK)
    groups = width // gdim
    expand = (lax.broadcasted_iota(jnp.int32, (groups, width), 1) // gdim
              == lax.broadcasted_iota(jnp.int32, (groups, width), 0)).astype(F32)
    bias = mmf(sgu_bt, expand)
    lane = lax.broadcasted_iota(jnp.int32, (CHUNK, LANE), 1)
    per_lane_block = LANE // gdim
    chunks = []
    for ci in range(rows // CHUNK):
        vc = v[ci * CHUNK:(ci + 1) * CHUNK]
        blocks = []
        for lb in range(width // LANE):
            vb = vc[:, lb * LANE:(lb + 1) * LANE]
            acc = None
            for s in range(per_lane_block):
                g = lb * per_lane_block + s
                r = mmb(sgu_w[g * CHUNK:(g + 1) * CHUNK], vb)
                sel = (lane // gdim) == s
                acc = jnp.where(sel, r, 0.0) if acc is None else jnp.where(sel, r, acc)
            blocks.append(acc)
        chunks.append(jnp.concatenate(blocks, axis=1) + bias)
    return jnp.concatenate(chunks, axis=0)


def _fn_b(pu, prest, ysc, s5_d, w_glu, b_glu, ln_g, ln_b, sgu_w, sgu_bt, w_pa, w_pb, b_gate):
    sw = ln_g.shape[1]
    y = jax.nn.gelu(pu * s5_d + ysc)
    ya = y * jax.nn.sigmoid(mmb(y, w_glu) + b_glu)
    z = jax.nn.gelu(prest[:, :2 * sw])
    u, v = z[:, :sw], z[:, sw:]
    vc = v - jnp.mean(v, axis=-1, keepdims=True)
    v = vc * lax.rsqrt(jnp.mean(vc * vc, axis=-1, keepdims=True) + EPS) * ln_g + ln_b
    yb = u * _sgu_spatial(v, sgu_w, sgu_bt)
    gates = jax.nn.sigmoid(prest[:, 2 * sw:] + b_gate)
    d = gates.shape[1] // 2
    return (_b16(gates[:, :d] * mmb(ya, w_pa) + gates[:, d:] * mmb(yb, w_pb)),)


def _fn_c(x, mo, ga1, g_ffn, sh2, sc2):
    x1 = x + ga1 * mo
    return x1, _b16(_modulate(x1, g_ffn, sh2, sc2))


def _fn_e(x1, dn, tgt, ga2, g_final):
    y = _rms(x1 + ga2 * dn, g_final)
    err = (y - tgt) ** 2
    return 0.5 * jnp.sum(jnp.mean(err, axis=-1, keepdims=True), axis=0, keepdims=True)


def _swap_impl(x):
    return pltpu.roll(x, x.shape[1] // 2, 1)


@jax.custom_vjp
def _swap_halves(x):
    return _swap_impl(x)


_swap_halves.defvjp(lambda x: (_swap_impl(x), None), lambda _, g: (_swap_impl(g),))


def _s5_direction(u, mask, a_re, a_im, log_step, bt, c, rev):
    nc, width = u.shape
    t_len = width // S5_H
    n2 = a_re.shape[1]
    lane = lax.broadcasted_iota(jnp.int32, (1, n2), 1)
    sign = jnp.where(lane < n2 // 2, -1.0, 1.0)
    dt = jnp.exp(log_step)
    lr, li = a_re * dt, a_im * dt
    mag = jnp.exp(lr)
    ab_re, ab_im = mag * jnp.cos(li), mag * jnp.sin(li)
    p, q = ab_re - 1.0, ab_im
    den = a_re * a_re + a_im * a_im
    k_re, k_im = (p * a_re + q * a_im) / den, (q * a_re - p * a_im) / den
    bb = k_re * bt + (k_im * sign) * _swap_halves(bt)

    def power(e):
        m = jnp.exp(lr * e)
        return m * jnp.cos(li * e), m * jnp.sin(li * e)

    order = range(t_len - 1, -1, -1) if rev else range(t_len)
    e1 = jnp.concatenate([jnp.full((1, 1, n2), float(t_len - 1 - pos), F32) for pos in order], axis=0)
    lr3, li3, sign3 = lr.reshape(1, 1, n2), li.reshape(1, 1, n2), sign.reshape(1, 1, n2)
    m1, c1, s1 = jnp.exp(lr3 * e1), jnp.cos(li3 * e1), jnp.sin(li3 * e1)
    m2 = jnp.exp(-(lr3 * e1))
    x1r, x1i = m1 * c1, m1 * s1
    x2r, x2i = m2 * c1, -(m2 * s1)
    at_r, at_i = [v.reshape(1, 1, n2) for v in power(float(t_len))]
    x3r, x3i = at_r * x2r - at_i * x2i, at_r * x2i + at_i * x2r

    def rows(xr, xi, z):
        z3, zs3 = z.reshape(1, S5_H, n2), _swap_halves(z).reshape(1, S5_H, n2)
        return (xr * z3 + (xi * sign3) * zs3).reshape(width, n2)

    p_in = rows(x1r, x1i, bb)
    r_out = rows(x2r, x2i, c)
    w_out = rows(x3r, x3i, c)
    toep = mm3_nt(p_in, r_out * (-sign)) * mask

    state = mmb(u, p_in)
    k = 1
    while k < nc:
        ar, ai = power(float(t_len * k))
        moved = _shift(state, k, rev)
        state = state + ar * moved + (ai * sign) * _swap_halves(moved)
        k *= 2
    entering = _shift(state, 1, rev)
    return mmb(u, toep) + mmb_nt(entering, w_out * (-sign))


def _fn_s5(masks, x_chunks, ctx_chunks, *prm):
    nx, nctx = x_chunks.shape[0], ctx_chunks.shape[0]
    pad = (-(nx + 2 * nctx)) % LANE
    u = jnp.concatenate([ctx_chunks, x_chunks, ctx_chunks, jnp.zeros((pad, x_chunks.shape[1]), F32)], axis=0)
    out = None
    for d in range(2):
        y = _s5_direction(u, masks[d], *[p[d] for p in prm], rev=(d == 1))
        out = y if out is None else out + y
    return out[nctx:nctx + nx]


def _s5_specs(prm):
    return [pl.BlockSpec((2, 1) + p.shape[2:], lambda g: (0, g, 0, 0)) for p in prm]


def _group_spec(a):
    return pl.BlockSpec((1,) + a.shape[1:], lambda i: (i, 0, 0))


def _s5_masks(width):
    pos = jnp.arange(width) // S5_H
    causal = (pos[None, :] >= pos[:, None]).astype(F32)
    return jnp.stack([causal, causal.T])


def _s5_fwd(x_g, ctx_g, prm):
    masks = _s5_masks(x_g.shape[2])

    def body(*refs):
        pv = [r[:, 0] for r in refs[3:3 + len(prm)]]
        refs[-1][0] = _fn_s5(refs[0][...], refs[1][0], refs[2][0], *pv)

    return _call(body, name="s5_fwd", grid=(x_g.shape[0],),
                 in_specs=[_whole_spec(masks.shape), _group_spec(x_g), _group_spec(ctx_g)] + _s5_specs(prm),
                 out_specs=_group_spec(x_g), out_shape=jax.ShapeDtypeStruct(x_g.shape, F32),
                 compiler_params=_params(1))(masks, x_g, ctx_g, *prm)


def _s5_bwd(x_g, ctx_g, prm, dy_g, skip_g):
    npar = len(prm)
    masks = _s5_masks(x_g.shape[2])

    def body(*refs):
        pv = [r[:, 0] for r in refs[3:3 + npar]]
        dy = refs[3 + npar][0]
        _, vjp = jax.vjp(functools.partial(_fn_s5, refs[0][...]), refs[1][0], refs[2][0], *pv)
        grads = vjp(dy)
        outs = refs[5 + npar:]
        outs[0][0] = (grads[0] + dy * refs[4 + npar][0]).astype(outs[0].dtype)
        outs[1][0] = grads[1]
        for o_ref, gv in zip(outs[2:], grads[2:]):
            o_ref[:, 0] = gv

    return _call(body, name="s5_bwd", grid=(x_g.shape[0],),
                 in_specs=([_whole_spec(masks.shape), _group_spec(x_g), _group_spec(ctx_g)] + _s5_specs(prm)
                           + [_group_spec(dy_g), _group_spec(skip_g)]),
                 out_specs=[_group_spec(x_g), _group_spec(ctx_g)] + _s5_specs(prm),
                 out_shape=[jax.ShapeDtypeStruct(x_g.shape, F32), jax.ShapeDtypeStruct(ctx_g.shape, F32)]
                 + [jax.ShapeDtypeStruct(p.shape, F32) for p in prm],
                 compiler_params=_params(1))(masks, x_g, ctx_g, *prm, dy_g, skip_g)


def _to_groups(tok):
    n, width = tok.shape
    g = width // S5_H
    return jnp.transpose(tok.reshape(n, g, S5_H), (1, 0, 2)).reshape(g, n // S5_T, S5_T * S5_H)


def _from_groups(grp):
    g, nc, _ = grp.shape
    return jnp.transpose(grp.reshape(g, nc * S5_T, S5_H), (1, 0, 2)).reshape(nc * S5_T, g * S5_H)


def _conv_taps(xp, xm, xn, blk_i, n_blk):
    tb = xm.shape[0]
    xp = jnp.where(blk_i == 0, 0.0, xp.astype(F32))
    xn = jnp.where(blk_i == n_blk - 1, 0.0, xn.astype(F32))
    buf = jnp.concatenate([xp, xm.astype(F32), xn], axis=0)
    n = tb + 2 * HALO
    col = lax.broadcasted_iota(jnp.int32, (n, 1), 0) % GRID_W
    left = jnp.where(col >= 1, pltpu.roll(buf, 1, 0), 0.0)
    right = jnp.where(col <= GRID_W - 2, pltpu.roll(buf, n - 1, 0), 0.0)
    shifted = (left, buf, right)
    taps = []
    for di in range(3):
        start = HALO + (di - 1) * GRID_W
        for dj in range(3):
            taps.append(shifted[dj][start:start + tb])
    return taps


def _conv_sum(taps, w_ref, flip=False):
    acc = None
    for k, tap in enumerate(taps):
        j = len(taps) - 1 - k if flip else k
        term = tap * w_ref[j:j + 1, :]
        acc = term if acc is None else acc + term
    return acc


def _conv_geometry(n_tok, width):
    tb = _pick(n_tok, 1024, HALO)
    cb = _pick(width, 256)
    nb = tb // HALO
    last = n_tok // HALO - 1
    main = pl.BlockSpec((tb, cb), lambda j, i: (i, j))
    prev = pl.BlockSpec((HALO, cb), lambda j, i: (jnp.maximum(i * nb - 1, 0), j))
    nxt = pl.BlockSpec((HALO, cb), lambda j, i: (jnp.minimum(i * nb + nb, last), j))
    par = lambda r: pl.BlockSpec((r, cb), lambda j, i: (0, j))
    return tb, cb, main, prev, nxt, par


def _conv_act_fwd(up_g, up_v, w_g, w_v, b_g, b_v):
    n_tok, width = up_g.shape
    tb, cb, main, prev, nxt, par = _conv_geometry(n_tok, width)
    n_blk = n_tok // tb

    def body(gp, gm, gn, vp, vm, vn, wg, wv, bg, bv, o_ref, gate_ref, val_ref):
        i = pl.program_id(1)
        gate = _conv_sum(_conv_taps(gp[...], gm[...], gn[...], i, n_blk), wg) + bg[...]
        val = _conv_sum(_conv_taps(vp[...], vm[...], vn[...], i, n_blk), wv) + bv[...]
        o_ref[...] = (jax.nn.silu(gate) * val).astype(o_ref.dtype)
        gate_ref[...] = gate
        val_ref[...] = val

    shp = jax.ShapeDtypeStruct
    return _call(body, name="conv_act_fwd", grid=(width // cb, n_tok // tb),
                 in_specs=[prev, main, nxt, prev, main, nxt, par(9), par(9), par(1), par(1)],
                 out_specs=[main, main, main],
                 out_shape=[shp((n_tok, width), BF16), shp((n_tok, width), F32), shp((n_tok, width), F32)],
                 compiler_params=_params(2))(up_g, up_g, up_g, up_v, up_v, up_v, w_g, w_v, b_g, b_v)


def _conv_act_bwd(up_g, up_v, gate_c, val_c, d_act):
    n_tok, width = up_g.shape
    tb, cb, main, prev, nxt, par = _conv_geometry(n_tok, width)
    n_blk = n_tok // tb

    def body(gp, gm, gn, vp, vm, vn, gc, vc, da, dcg, dcv, dwg, dwv, dbg, dbv):
        i = pl.program_id(1)
        taps_g = _conv_taps(gp[...], gm[...], gn[...], i, n_blk)
        taps_v = _conv_taps(vp[...], vm[...], vn[...], i, n_blk)
        gate, val = gc[...], vc[...]
        sig = jax.nn.sigmoid(gate)
        d = da[...].astype(F32)
        d_gate = d * val * sig * (1.0 + gate * (1.0 - sig))
        d_val = d * gate * sig
        dcg[...] = d_gate.astype(dcg.dtype)
        dcv[...] = d_val.astype(dcv.dtype)

        @pl.when(i == 0)
        def _():
            for r in (dwg, dwv, dbg, dbv):
                r[...] = jnp.zeros_like(r)

        dbg[...] += jnp.sum(d_gate, axis=0, keepdims=True)
        dbv[...] += jnp.sum(d_val, axis=0, keepdims=True)
        for k in range(9):
            dwg[k:k + 1, :] += jnp.sum(taps_g[k] * d_gate, axis=0, keepdims=True)
            dwv[k:k + 1, :] += jnp.sum(taps_v[k] * d_val, axis=0, keepdims=True)

    shp = jax.ShapeDtypeStruct
    return _call(body, name="conv_act_bwd", grid=(width // cb, n_tok // tb),
                 in_specs=[prev, main, nxt, prev, main, nxt, main, main, main],
                 out_specs=[main, main, par(9), par(9), par(1), par(1)],
                 out_shape=[shp((n_tok, width), BF16), shp((n_tok, width), BF16), shp((9, width), F32),
                            shp((9, width), F32), shp((1, width), F32), shp((1, width), F32)],
                 compiler_params=_params(2))(up_g, up_g, up_g, up_v, up_v, up_v, gate_c, val_c, d_act)


def _conv_transposed(x, w, name):
    n_tok, width = x.shape
    tb, cb, main, prev, nxt, par = _conv_geometry(n_tok, width)
    n_blk = n_tok // tb

    def body(xp, xm, xn, w_ref, o_ref):
        taps = _conv_taps(xp[...], xm[...], xn[...], pl.program_id(1), n_blk)
        o_ref[...] = _conv_sum(taps, w_ref, flip=True).astype(o_ref.dtype)

    return _call(body, name=name, grid=(width // cb, n_tok // tb), in_specs=[prev, main, nxt, par(9)],
                 out_specs=main, out_shape=jax.ShapeDtypeStruct((n_tok, width), BF16),
                 compiler_params=_params(2))(x, x, x, w)


def _adamw(w, g_parts, m, v, name, own=None, me=None):
    rows, cols = w.shape
    parts = g_parts.shape[0]
    blk = _pick(rows, 256, 8)
    spec = pl.BlockSpec((blk, cols), lambda i: (i, 0))
    has_own = own is not None

    def body(*refs):
        w_ref, g_ref, m_ref, v_ref = refs[:4]
        g_out, d_out, m_out, v_out = refs[-4:]

        def part(p):
            if has_own:
                return jnp.where(refs[5][...] == p, refs[4][...], g_ref[p]).astype(F32)
            return g_ref[p].astype(F32)

        g = part(0)
        for p in range(1, parts):
            g = g + part(p)
        m_new = ADAM_B1 * m_ref[...] + (1.0 - ADAM_B1) * g
        v_new = ADAM_B2 * v_ref[...] + (1.0 - ADAM_B2) * (g * g)
        m_hat = m_new / (1.0 - ADAM_B1 ** ADAM_STEP)
        v_hat = v_new / (1.0 - ADAM_B2 ** ADAM_STEP)
        g_out[...] = g
        d_out[...] = -ADAM_LR * (m_hat / (jnp.sqrt(v_hat) + ADAM_EPS) + ADAM_WD * w_ref[...])
        m_out[...] = m_new
        v_out[...] = v_new

    extra = [own, me] if has_own else []
    return _call(body, name=name, grid=(rows // blk,),
                 in_specs=([spec, pl.BlockSpec((parts, blk, cols), lambda i: (0, i, 0)), spec, spec]
                           + ([spec, _whole_spec((1, 1))] if has_own else [])),
                 out_specs=[spec] * 4, out_shape=[jax.ShapeDtypeStruct((rows, cols), F32)] * 4,
                 compiler_params=_params(1))(w, g_parts, m, v, *extra)


def _exchange(items, name):
    n = len(items)
    hbm = pl.BlockSpec(memory_space=pl.ANY)

    def body(*refs):
        srcs, outs = refs[:n], refs[n:2 * n]
        send_sems, recv_sems, own_sems = refs[2 * n:]
        x, y, c = lax.axis_index("x"), lax.axis_index("y"), lax.axis_index("c")
        me = 4 * x + 2 * y + c
        own = []
        for i, (_, mode) in enumerate(items):
            src = srcs[i] if mode == "gather" else srcs[i].at[me]
            cp = pltpu.make_async_copy(src, outs[i].at[me], own_sems.at[i])
            cp.start()
            own.append(cp)
        sent = []
        for i, (_, mode) in enumerate(items):
            for k in range(1, N_DEV):
                px = 1 - x if k & 4 else x
                py = 1 - y if k & 2 else y
                pc = 1 - c if k & 1 else c
                peer = 4 * px + 2 * py + pc
                src = srcs[i] if mode == "gather" else srcs[i].at[peer]
                cp = pltpu.make_async_remote_copy(
                    src_ref=src, dst_ref=outs[i].at[me], send_sem=send_sems.at[i, k - 1],
                    recv_sem=recv_sems.at[i, k - 1], device_id=(px, py, pc), device_id_type=pl.DeviceIdType.MESH)
                cp.start()
                landing = pltpu.make_async_remote_copy(
                    src_ref=src, dst_ref=outs[i].at[peer], send_sem=send_sems.at[i, k - 1],
                    recv_sem=recv_sems.at[i, k - 1], device_id=(px, py, pc), device_id_type=pl.DeviceIdType.MESH)
                sent.append((cp, landing))
        for cp in own:
            cp.wait()
        for cp, landing in sent:
            cp.wait_send()
            landing.wait_recv()

    out_shape = [jax.ShapeDtypeStruct((N_DEV,) + (a.shape if mode == "gather" else a.shape[1:]), a.dtype)
                 for a, mode in items]
    return _call(body, name=name, in_specs=[hbm] * n, out_specs=[hbm] * n, out_shape=out_shape,
                 scratch_shapes=[pltpu.SemaphoreType.DMA((n, N_DEV - 1)), pltpu.SemaphoreType.DMA((n, N_DEV - 1)),
                                 pltpu.SemaphoreType.DMA((n,))])(*[a for a, _ in items])


def _peer(k, x, y, c):
    px = 1 - x if k & 4 else x
    py = 1 - y if k & 2 else y
    pc = 1 - c if k & 1 else c
    return (px, py, pc), 4 * px + 2 * py + pc


_HBM_SPEC = pl.BlockSpec(memory_space=pltpu.HBM)
_SEM_SPEC = pl.BlockSpec(memory_space=pltpu.SEMAPHORE)
_SPLIT_EFFECT = pltpu.SideEffectType.DATAFLOW_SIDE_EFFECTING


def _exchange_begin(items, name, after=None):
    n = len(items)
    modes = [mode for _, mode in items]
    srcs = [pltpu.with_memory_space_constraint(a, pltpu.HBM) for a, _ in items]
    lands = [pltpu.with_memory_space_constraint(
        lax.empty((N_DEV,) + (a.shape if mode == "gather" else a.shape[1:]), a.dtype), pltpu.HBM)
        for a, mode in items]

    def body(*refs):
        src_refs, land_refs = refs[:n], refs[n:2 * n]
        token = refs[-1]
        send_sems, recv_sems = refs[-2 * n - 3], refs[-2 * n - 2]
        x, y, c = lax.axis_index("x"), lax.axis_index("y"), lax.axis_index("c")
        me = 4 * x + 2 * y + c
        for i in range(n):
            for k in range(1, N_DEV):
                coords, peer = _peer(k, x, y, c)
                src = src_refs[i] if modes[i] == "gather" else src_refs[i].at[peer]
                pltpu.make_async_remote_copy(
                    src_ref=src, dst_ref=land_refs[i].at[me], send_sem=send_sems.at[i * (N_DEV - 1) + k - 1],
                    recv_sem=recv_sems.at[i * (N_DEV - 1) + k - 1], device_id=coords,
                    device_id_type=pl.DeviceIdType.MESH).start()
        token[...] = jnp.zeros_like(token)

    sems = pltpu.SemaphoreType.DMA((n * (N_DEV - 1),))
    order = [] if after is None else [after]
    res = _call(body, name=name,
                out_shape=(sems, sems, *[pltpu.HBM(a.shape, a.dtype) for a in srcs + lands],
                           jax.ShapeDtypeStruct((8, LANE), F32)),
                in_specs=[_HBM_SPEC] * (2 * n) + [pl.BlockSpec(memory_space=pl.ANY)] * len(order),
                out_specs=(_SEM_SPEC, _SEM_SPEC, *[_HBM_SPEC] * (2 * n), pl.BlockSpec(memory_space=pltpu.VMEM)),
                input_output_aliases={i: 2 + i for i in range(2 * n)},
                compiler_params=pltpu.CompilerParams(has_side_effects=_SPLIT_EFFECT))(*srcs, *lands, *order)
    return (modes, res[0], res[1], list(res[2:2 + n]), list(res[2 + n:2 + 2 * n])), res[-1][0, 0]


def _exchange_end(handle, after, name):
    modes, send_sems, recv_sems, srcs, lands = handle
    n = len(modes)

    def wait_body(*refs):
        src_refs, land_refs = refs[:n], refs[n:2 * n]
        send, recv = refs[2 * n], refs[2 * n + 1]
        x, y, c = lax.axis_index("x"), lax.axis_index("y"), lax.axis_index("c")
        for i in range(n):
            for k in range(1, N_DEV):
                coords, peer = _peer(k, x, y, c)
                src = src_refs[i] if modes[i] == "gather" else src_refs[i].at[peer]
                cp = pltpu.make_async_remote_copy(
                    src_ref=src, dst_ref=land_refs[i].at[peer], send_sem=send.at[i * (N_DEV - 1) + k - 1],
                    recv_sem=recv.at[i * (N_DEV - 1) + k - 1], device_id=coords,
                    device_id_type=pl.DeviceIdType.MESH)
                cp.wait_send()
                cp.wait_recv()

    res = _call(wait_body, name=name, out_shape=[pltpu.HBM(a.shape, a.dtype) for a in srcs + lands],
                in_specs=[_HBM_SPEC] * (2 * n) + [_SEM_SPEC, _SEM_SPEC, pl.BlockSpec(memory_space=pl.ANY)],
                out_specs=[_HBM_SPEC] * (2 * n), input_output_aliases={i: i for i in range(2 * n)},
                compiler_params=pltpu.CompilerParams(has_side_effects=_SPLIT_EFFECT))(
                    *srcs, *lands, send_sems, recv_sems, after)
    return res[n:]


def _with_own(land, own, me):
    slot = lax.broadcasted_iota(jnp.int32, (N_DEV,) + (1,) * (land.ndim - 1), 0)
    return jnp.where(slot == me, own[None], land)


def _cols_from_blocks(g):
    return jnp.transpose(g, (1, 0, 2)).reshape(g.shape[1], N_DEV * g.shape[2])


def _blocks_from_cols(w):
    r, c8 = w.shape
    return jnp.transpose(w.reshape(r, N_DEV, c8 // N_DEV), (1, 0, 2))


def _pack(arrs):
    flat = jnp.concatenate([a.reshape(-1).astype(F32) for a in arrs])
    pad = (-flat.shape[0]) % (8 * LANE)
    return jnp.pad(flat, (0, pad)).reshape(-1, LANE)


def _unpack(packed, shapes):
    flat = packed.reshape(-1)
    out, off = [], 0
    for s in shapes:
        size = math.prod(s)
        out.append(flat[off:off + size].reshape(s))
        off += size
    return out


def kernel(x, c, ctx, c_ctx, w_ada, b_ada, g_mix, w_in, s5_a_re, s5_a_im, s5_log_step, s5_b_re, s5_b_im, s5_c_re, s5_c_im, s5_d, s5_w_glu, s5_b_glu, sgu_ln_g, sgu_ln_b, sgu_w, sgu_b, w_proj_a, w_proj_b, b_gate, w_out, g_ffn, w_up, conv_w, conv_b, w_down, g_final, loss_target, m_c_ctx, m_w_ada, m_b_ada, m_g_mix, m_w_in, m_s5_a_re, m_s5_a_im, m_s5_log_step, m_s5_b_re, m_s5_b_im, m_s5_c_re, m_s5_c_im, m_s5_d, m_s5_w_glu, m_s5_b_glu, m_sgu_ln_g, m_sgu_ln_b, m_sgu_w, m_sgu_b, m_w_proj_a, m_w_proj_b, m_b_gate, m_w_out, m_g_ffn, m_w_up, m_conv_w, m_conv_b, m_w_down, m_g_final, v_c_ctx, v_w_ada, v_b_ada, v_g_mix, v_w_in, v_s5_a_re, v_s5_a_im, v_s5_log_step, v_s5_b_re, v_s5_b_im, v_s5_c_re, v_s5_c_im, v_s5_d, v_s5_w_glu, v_s5_b_glu, v_sgu_ln_g, v_sgu_ln_b, v_sgu_w, v_sgu_b, v_w_proj_a, v_w_proj_b, v_b_gate, v_w_out, v_g_ffn, v_w_up, v_conv_w, v_conv_b, v_w_down, v_g_final):
    given = dict(locals())
    wts = {n: given[n] for n in WEIGHTS}
    mom1 = {n: given["m_" + n] for n in WEIGHTS}
    mom2 = {n: given["v_" + n] for n in WEIGHTS}

    me = 4 * lax.axis_index("x") + 2 * lax.axis_index("y") + lax.axis_index("c")
    xs, cx, tgt = x[0], ctx[0], loss_target[0]
    n_tok, d = xs.shape
    n_ctx = cx.shape[0]
    s5w = s5_d.shape[1]
    ffn = w_down.shape[1] * N_DEV
    n_mod = w_ada.shape[2] * N_DEV // d
    mod_cols = w_ada.shape[2]

    def two_d(a):
        return a.reshape(-1, a.shape[-1])

    conv_w9 = conv_w[0].reshape(9, -1)
    gathered = _exchange(
        [(c, "gather"), (_b16(w_in[0]), "gather"), (_b16(s5_w_glu[0]), "gather"), (_b16(w_proj_a[0]), "gather"),
         (_b16(w_proj_b[0]), "gather")], "gather_weights")
    c_all = gathered[0].reshape(N_DEV, d)
    w_in_f = _cols_from_blocks(gathered[1])
    w_in_u, w_in_rest = w_in_f[:, :s5w], w_in_f[:, s5w:]
    w_glu_f = gathered[2].reshape(-1, s5w)
    w_pa_f = _cols_from_blocks(gathered[3])
    w_pb_f = _cols_from_blocks(gathered[4])

    cs_in = jnp.concatenate([c_all, jnp.broadcast_to(c_ctx[None, :], (N_DEV, d))], axis=0)
    w_ada_loc = w_ada[0]
    (mod_mine,) = _stage_fwd(_fn_mod, [cs_in], [w_ada_loc], [(mod_cols, F32)], blk=2 * N_DEV, name="mod_fwd")
    (mod_blocks,) = _exchange([(mod_mine, "gather")], "gather_mod")
    late_own = [_b16(w_out[0]), _b16(w_up[0]), conv_w9, _b16(w_down[0])]
    late_weights, tok = _exchange_begin([(a, "gather") for a in late_own], "gather_late_begin", after=mod_blocks)
    mod_all = _cols_from_blocks(mod_blocks) + b_ada + tok
    mod = lax.dynamic_slice_in_dim(mod_all, me, 1, axis=0)
    mod_c = mod_all[N_DEV:N_DEV + 1]
    sh1, sc1, ga1, sh2, sc2, ga2 = [mod[:, i * d:(i + 1) * d] for i in range(n_mod)]
    sh1c, sc1c = mod_c[:, :d], mod_c[:, d:2 * d]

    a_par = [g_mix, sh1, sc1]
    ac_par = [g_mix, sh1c, sc1c]
    (h,) = _stage_fwd(_fn_a, [xs], a_par, [(d, BF16)], blk=_pick(n_tok, 512, 8), name="modulate1_fwd")
    (hc,) = _stage_fwd(_fn_a, [cx], ac_par, [(d, BF16)], blk=_pick(n_ctx, 512, 8), name="modulate1_ctx_fwd")
    pu = _mm(h, w_in_u, name="proj_u")
    prest = _mm(h, w_in_rest, name="proj_rest")
    puc = _mm(hc, w_in_u, name="proj_u_ctx")

    n_state = s5_a_re.shape[-1]

    def twice(a):
        return jnp.concatenate([a, a], axis=-1)

    s5_prm = [twice(s5_a_re[0])[:, :, None, :], twice(s5_a_im[0])[:, :, None, :], s5_log_step[0][:, :, None, None],
              jnp.concatenate([jnp.swapaxes(s5_b_re[0], 2, 3), jnp.swapaxes(s5_b_im[0], 2, 3)], axis=-1),
              jnp.concatenate([s5_c_re[0], s5_c_im[0]], axis=-1)]
    pu_g, puc_g = _to_groups(pu), _to_groups(puc)
    ysc = _from_groups(_s5_fwd(pu_g, puc_g, s5_prm))

    b_par = [s5_d, w_glu_f, s5_b_glu, sgu_ln_g, sgu_ln_b, two_d(sgu_w[0]), jnp.transpose(sgu_b[0]),
             w_pa_f, w_pb_f, b_gate]
    b_rows = [pu, prest, ysc]
    b_blk = _pick(n_tok, 512, CHUNK)
    (mpre,) = _stage_fwd(_fn_b, b_rows, b_par, [(d, BF16)], blk=b_blk, name="mixers_fwd")
    late = [_with_own(land, own, me) for land, own in zip(_exchange_end(late_weights, mpre, "gather_late_end"),
                                                          late_own)]
    w_out_f = late[0].reshape(-1, d)
    w_up_f = _cols_from_blocks(late[1])
    w_up_g, w_up_v = w_up_f[:, :ffn], w_up_f[:, ffn:]
    conv_w_f = _cols_from_blocks(late[2])
    w_down_f = late[3].reshape(-1, d)
    mo = _mm(mpre, w_out_f, name="out_proj")

    c_par = [ga1, g_ffn, sh2, sc2]
    c_blk = _pick(n_tok, 512, 8)
    x1, h2 = _stage_fwd(_fn_c, [xs, mo], c_par, [(d, F32), (d, BF16)], blk=c_blk, name="modulate2_fwd")
    up_g = _mm(h2, w_up_g, out_dtype=BF16, name="up_gate")
    up_v = _mm(h2, w_up_v, out_dtype=BF16, name="up_val")
    cw_g, cw_v = conv_w_f[:, :ffn], conv_w_f[:, ffn:]
    cb_g, cb_v = conv_b[:, :ffn], conv_b[:, ffn:]
    act, gate_c, val_c = _conv_act_fwd(up_g, up_v, cw_g, cw_v, cb_g, cb_v)
    dn = _mm(act, w_down_f, name="down_proj")

    loss_part, d_x1a, d_dn, d_ga2, d_g_final = _stage_loss(
        _fn_e, [x1, dn, tgt], [ga2, g_final[None, :]], blk=c_blk, name="loss_head",
        row_grads=[(0, F32), (1, BF16)])

    d_act = _mm(d_dn, w_down_f, tb=True, name="down_proj_dx")
    g_w_down = _mm(act, d_dn, ta=True, out_dtype=BF16, name="down_proj_dw")

    own_block = {}

    def grad_blocks(name, g):
        blocks = _b16(_blocks_from_cols(g) if name in COL_SHARDED
                      else g.reshape(N_DEV, g.shape[0] // N_DEV, g.shape[1]))
        own_block[name] = lax.dynamic_index_in_dim(blocks, me, 0, keepdims=False)
        return blocks, "a2a"

    sent_down, tok = _exchange_begin([grad_blocks('w_down', g_w_down)], "grads_down_begin")
    dcg, dcv, g_cw_g, g_cw_v, g_cb_g, g_cb_v = _conv_act_bwd(up_g, up_v, gate_c, val_c, d_act)
    dug = _conv_transposed(dcg, cw_g + tok, "conv_dx_gate")
    duv = _conv_transposed(dcv, cw_v, "conv_dx_val")
    d_h2 = _mm(dug, w_up_g, tb=True, name="up_gate_dx")
    d_h2 = _mm(duv, w_up_v, tb=True, add=d_h2, out_dtype=BF16, name="up_val_dx")
    g_w_up = jnp.concatenate([_mm(h2, dug, ta=True, out_dtype=BF16, name="up_gate_dw"),
                              _mm(h2, duv, ta=True, out_dtype=BF16, name="up_val_dw")], axis=1)
    sent_up, tok = _exchange_begin(
        [grad_blocks('w_up', g_w_up), grad_blocks('conv_w', jnp.concatenate([g_cw_g, g_cw_v], axis=1))],
        "grads_up_begin")
    (d_xc, d_mo), (d_ga1, g_g_ffn, d_sh2, d_sc2) = _split(_stage_bwd(
        _fn_c, [xs, mo], [ga1 + tok] + c_par[1:], [d_x1a, d_h2], blk=c_blk, name="modulate2_bwd",
        row_grads=[(0, F32), (1, BF16)]), 2)

    d_mpre = _mm(d_mo, w_out_f, tb=True, out_dtype=BF16, name="out_proj_dx")
    g_w_out = _mm(mpre, d_mo, ta=True, out_dtype=BF16, name="out_proj_dw")
    (d_prest, d_ysc), b_grads = _split(_stage_bwd(
        _fn_b, b_rows, b_par, [d_mpre], blk=_pick(n_tok, 256, CHUNK), name="mixers_bwd",
        row_grads=[(1, BF16), (2, F32)]), 2)
    (g_s5_d, g_w_glu, g_b_glu, g_ln_g, g_ln_b, g_sgu_w, g_sgu_bt, g_w_pa, g_w_pb, g_b_gate) = b_grads

    sent_mix, tok = _exchange_begin(
        [grad_blocks('w_out', g_w_out), grad_blocks('s5_w_glu', g_w_glu), grad_blocks('w_proj_a', g_w_pa),
         grad_blocks('w_proj_b', g_w_pb)], "grads_mixer_begin")
    skip_g = jnp.tile(s5_d.reshape(-1, 1, S5_H), (1, 1, S5_T)) + tok
    s5_out = _s5_bwd(pu_g, puc_g, s5_prm, _to_groups(d_ysc), skip_g)
    g_a_re2, g_a_im2, g_ls, g_bt2, g_c2 = s5_out[2:]
    g_a_re = g_a_re2[..., :n_state] + g_a_re2[..., n_state:]
    g_a_im = g_a_im2[..., :n_state] + g_a_im2[..., n_state:]
    g_bt_re, g_bt_im = g_bt2[..., :n_state], g_bt2[..., n_state:]
    g_c_re, g_c_im = g_c2[..., :n_state], g_c2[..., n_state:]

    part = {
        's5_a_re': g_a_re, 's5_a_im': g_a_im, 's5_log_step': g_ls,
        's5_b_re': jnp.swapaxes(g_bt_re, 2, 3), 's5_b_im': jnp.swapaxes(g_bt_im, 2, 3),
        's5_c_re': g_c_re, 's5_c_im': g_c_im, 's5_d': g_s5_d, 's5_b_glu': g_b_glu, 'sgu_ln_g': g_ln_g,
        'sgu_ln_b': g_ln_b, 'sgu_w': g_sgu_w, 'sgu_b': jnp.transpose(g_sgu_bt), 'b_gate': g_b_gate,
        'g_ffn': g_g_ffn, 'conv_b': jnp.concatenate([g_cb_g, g_cb_v], axis=1), 'g_final': d_g_final,
    }
    early = [n for n in REPLICATED if n not in LATE_REPLICATED and n not in UNPACKED_REPLICATED]
    early_part = _pack([part[n] for n in early])
    own_small = {n: two_d(part[n]) for n in UNPACKED_REPLICATED}
    sent_small, tok = _exchange_begin(
        [(early_part, "gather")] + [(own_small[n], "gather") for n in UNPACKED_REPLICATED], "grads_small_begin")
    d_pu, d_puc = _b16(_from_groups(s5_out[0])), _from_groups(s5_out[1]) + tok

    d_h = _mm(d_pu, w_in_u, tb=True, name="proj_u_dx")
    d_h = _mm(d_prest, w_in_rest, tb=True, add=d_h, out_dtype=BF16, name="proj_rest_dx")
    d_hc = _mm(d_puc, w_in_u, tb=True, out_dtype=BF16, name="proj_u_ctx_dx")
    g_w_in_u = _mm(hc, d_puc, ta=True, name="proj_u_ctx_dw")
    g_w_in_u = _mm(h, d_pu, ta=True, add=g_w_in_u, out_dtype=BF16, name="proj_u_dw")
    g_w_in = jnp.concatenate([g_w_in_u, _mm(h, d_prest, ta=True, out_dtype=BF16, name="proj_rest_dw")], axis=1)
    sent_in, tok = _exchange_begin([grad_blocks('w_in', g_w_in)], "grads_in_begin")

    (grad_x,), (g_g_mix_x, d_sh1, d_sc1) = _split(_stage_bwd(
        _fn_a_res, [xs], [g_mix + tok] + a_par[1:], [d_h, d_xc], blk=c_blk, name="modulate1_bwd",
        row_grads=[(0, F32)]), 1)
    _, (g_g_mix_c, d_sh1c, d_sc1c) = _split(_stage_bwd(
        _fn_a, [cx], ac_par, [d_hc], blk=_pick(n_ctx, 512, 8), name="modulate1_ctx_bwd", row_grads=[]), 0)

    zeros = jnp.zeros((1, (n_mod - 2) * d), F32)
    d_mod = jnp.concatenate([d_sh1, d_sc1, d_ga1, d_sh2, d_sc2, d_ga2], axis=1)
    d_mod_c = jnp.concatenate([d_sh1c, d_sc1c, zeros], axis=1)
    (d_mod_all,) = _exchange([(jnp.concatenate([d_mod, d_mod_c], axis=0), "gather")], "gather_dmod")
    d_mod_rows = jnp.transpose(d_mod_all, (1, 0, 2)).reshape(2 * N_DEV, n_mod * d)
    d_mod_mine = lax.dynamic_slice_in_dim(d_mod_rows, me * mod_cols, mod_cols, axis=1)
    (d_cs,), (g_w_ada,) = _split(_stage_bwd(
        _fn_mod, [cs_in], [w_ada_loc], [d_mod_mine], blk=2 * N_DEV, name="mod_bwd", row_grads=[(0, F32)]), 1)

    part.update({'c_ctx': jnp.sum(d_cs[N_DEV:], axis=0), 'b_ada': d_mod + d_mod_c, 'g_mix': g_g_mix_x + g_g_mix_c})
    late_parts, loss_parts = _exchange(
        [(_pack([part[n] for n in LATE_REPLICATED]), "gather"), (jnp.broadcast_to(loss_part, (8, LANE)), "gather")],
        "exchange_grads")

    summed = {}
    small_parts = _exchange_end(sent_small, late_parts, "grads_small_end")
    early_parts = small_parts[0]
    for n, parts in zip(UNPACKED_REPLICATED, small_parts[1:]):
        summed[n], own_block[n] = parts, own_small[n]
    (summed['w_down'],) = _exchange_end(sent_down, late_parts, "grads_down_end")
    summed['w_up'], summed['conv_w'] = _exchange_end(sent_up, late_parts, "grads_up_end")
    summed['w_out'], summed['s5_w_glu'], summed['w_proj_a'], summed['w_proj_b'] = _exchange_end(
        sent_mix, late_parts, "grads_mixer_end")
    (summed['w_in'],) = _exchange_end(sent_in, late_parts, "grads_in_end")

    out = {}
    me_arr = me.reshape(1, 1).astype(jnp.int32)
    for names, parts, own, tag in ((early, early_parts, early_part, "early"),
                                   (LATE_REPLICATED, late_parts, None, "late")):
        res = _adamw(_pack([wts[n] for n in names]), parts, _pack([mom1[n] for n in names]),
                     _pack([mom2[n] for n in names]), "adamw_replicated_" + tag, own=own,
                     me=None if own is None else me_arr)
        res = [_unpack(r, [wts[n].shape for n in names]) for r in res]
        for i, n in enumerate(names):
            out[n] = tuple(r[i] for r in res)
    for n, parts in summed.items():
        shape = wts[n].shape
        res = _adamw(two_d(wts[n]), parts, two_d(mom1[n]), two_d(mom2[n]), "adamw_" + n, own=own_block[n],
                     me=me_arr)
        out[n] = tuple(r.reshape(shape) for r in res)
    res = _adamw(w_ada_loc, g_w_ada[None], m_w_ada[0], v_w_ada[0], "adamw_w_ada")
    out['w_ada'] = tuple(r.reshape(w_ada.shape) for r in res)

    loss = jnp.sum(loss_parts[:, 0, 0])
    return (loss, grad_x[None], *[out[n][0] for n in WEIGHTS], *[out[n][1] for n in WEIGHTS],
            *[out[n][2] for n in WEIGHTS], *[out[n][3] for n in WEIGHTS])


def _split(res, n_rows):
    return tuple(res[:n_rows]), tuple(res[n_rows:])
```

```python
import functools
import math

import jax
import jax.numpy as jnp
from jax import lax
from jax.experimental import pallas as pl
from jax.experimental.pallas import tpu as pltpu

F32 = jnp.float32
BF16 = jnp.bfloat16
HI = lax.Precision.HIGHEST

N_DEV = 8
GRID_W = 64
CHUNK = 128
EPS = 1e-6
S5_T = 32
S5_H = 16
LANE = 128
HALO = 128
VMEM_LIMIT = 56 * 1024 * 1024
PACK_ROWS = 256

ADAM_LR = 0.001
ADAM_B1 = 0.9
ADAM_B2 = 0.999
ADAM_EPS = 1e-08
ADAM_WD = 0.01
ADAM_STEP = 10

WEIGHTS = ['c_ctx', 'w_ada', 'b_ada', 'g_mix', 'w_in', 's5_a_re', 's5_a_im', 's5_log_step', 's5_b_re', 's5_b_im',
           's5_c_re', 's5_c_im', 's5_d', 's5_w_glu', 's5_b_glu', 'sgu_ln_g', 'sgu_ln_b', 'sgu_w', 'sgu_b',
           'w_proj_a', 'w_proj_b', 'b_gate', 'w_out', 'g_ffn', 'w_up', 'conv_w', 'conv_b', 'w_down', 'g_final']
COL_SHARDED = ('w_ada', 'w_in', 'w_proj_a', 'w_proj_b', 'w_up', 'conv_w')
ROW_SHARDED = ('s5_w_glu', 'w_out', 'w_down')
SHARDED = COL_SHARDED + ROW_SHARDED
REPLICATED = [n for n in WEIGHTS if n not in SHARDED]
LATE_REPLICATED = ['c_ctx', 'b_ada', 'g_mix']
UNPACKED_REPLICATED = ['sgu_w', 's5_c_re', 's5_c_im']


def _call(body, **kw):
    return pl.pallas_call(body, **kw)


def _params(n_grid):
    return pltpu.CompilerParams(dimension_semantics=("arbitrary",) * n_grid, vmem_limit_bytes=VMEM_LIMIT)


def _pick(dim, pref, unit=LANE):
    best = None
    d = unit
    while d <= min(dim, pref):
        if dim % d == 0:
            best = d
        d += unit
    return best if best is not None else dim


def _dg(a, b, ca, cb, prec=None):
    return lax.dot_general(a, b, (((ca,), (cb,)), ((), ())), precision=prec, preferred_element_type=F32)


def _b16(v):
    return v.astype(BF16)


@jax.custom_vjp
def mmb(a, b):
    return _dg(_b16(a), _b16(b), 1, 0)


def _mmb_fwd(a, b):
    return mmb(a, b), (a, b)


def _mmb_bwd(res, g):
    a, b = res
    g = _b16(g)
    return _dg(g, _b16(b), 1, 1).astype(a.dtype), _dg(_b16(a), g, 0, 0).astype(b.dtype)


mmb.defvjp(_mmb_fwd, _mmb_bwd)


@jax.custom_vjp
def mmb_nt(a, b):
    return _dg(_b16(a), _b16(b), 1, 1)


def _mmb_nt_fwd(a, b):
    return mmb_nt(a, b), (a, b)


def _mmb_nt_bwd(res, g):
    a, b = res
    g = _b16(g)
    return _dg(g, _b16(b), 1, 0).astype(a.dtype), _dg(g, _b16(a), 0, 0).astype(b.dtype)


mmb_nt.defvjp(_mmb_nt_fwd, _mmb_nt_bwd)


@jax.custom_vjp
def mmf(a, b):
    return _dg(a, b, 1, 0, HI)


def _mmf_fwd(a, b):
    return mmf(a, b), (a, b)


def _mmf_bwd(res, g):
    a, b = res
    return _dg(g, b, 1, 1, HI), _dg(a, g, 0, 0, HI)


mmf.defvjp(_mmf_fwd, _mmf_bwd)


def _dg3(a, b, ca, cb):
    ah, bh = _b16(a), _b16(b)
    al, bl = _b16(a - ah.astype(F32)), _b16(b - bh.astype(F32))
    return _dg(ah, bh, ca, cb) + _dg(ah, bl, ca, cb) + _dg(al, bh, ca, cb)


@jax.custom_vjp
def mm3_nt(a, b):
    return _dg3(a, b, 1, 1)


def _mm3_nt_fwd(a, b):
    return mm3_nt(a, b), (a, b)


def _mm3_nt_bwd(res, g):
    a, b = res
    return _dg3(g, b, 1, 0), _dg3(g, a, 0, 0)


mm3_nt.defvjp(_mm3_nt_fwd, _mm3_nt_bwd)


def _shift_impl(x, k, up):
    n = x.shape[0]
    idx = lax.broadcasted_iota(jnp.int32, (n, 1), 0)
    if up:
        return jnp.where(idx < n - k, pltpu.roll(x, n - k, 0), 0.0)
    return jnp.where(idx >= k, pltpu.roll(x, k, 0), 0.0)


@functools.partial(jax.custom_vjp, nondiff_argnums=(1, 2))
def _shift(x, k, up):
    return _shift_impl(x, k, up)


def _shift_fwd(x, k, up):
    return _shift_impl(x, k, up), None


def _shift_bwd(k, up, _, g):
    return (_shift_impl(g, k, not up),)


_shift.defvjp(_shift_fwd, _shift_bwd)


def _mm(a, b, *, name, ta=False, tb=False, add=None, out_dtype=F32, tm_pref=2048, tn_pref=1408, tk_pref=1408):
    m, k = (a.shape[1], a.shape[0]) if ta else a.shape
    n = b.shape[0] if tb else b.shape[1]
    if add is not None and out_dtype != F32:
        tm_pref = min(tm_pref, 1024)
    tm, tn, tk = _pick(m, tm_pref), _pick(n, tn_pref), _pick(k, tk_pref)
    nk = k // tk
    a_spec = (pl.BlockSpec((tk, tm), lambda i, j, kk: (kk, i)) if ta
              else pl.BlockSpec((tm, tk), lambda i, j, kk: (i, kk)))
    b_spec = (pl.BlockSpec((tn, tk), lambda i, j, kk: (j, kk)) if tb
              else pl.BlockSpec((tk, tn), lambda i, j, kk: (kk, j)))
    o_spec = pl.BlockSpec((tm, tn), lambda i, j, kk: (i, j))
    ca, cb = (0 if ta else 1), (1 if tb else 0)
    has_add = add is not None
    in_place = out_dtype == F32 or nk == 1

    def body(*refs):
        a_ref, b_ref = refs[0], refs[1]
        o_ref = refs[3] if has_add else refs[2]
        prod = _dg(_b16(a_ref[...]), _b16(b_ref[...]), ca, cb)
        if nk == 1:
            if has_add:
                prod = prod + refs[2][...].astype(F32)
            o_ref[...] = prod.astype(o_ref.dtype)
            return
        acc_ref = o_ref if in_place else refs[-1]
        kk = pl.program_id(2)

        @pl.when(kk == 0)
        def _():
            acc_ref[...] = prod

        @pl.when(kk > 0)
        def _():
            acc_ref[...] += prod

        if has_add or not in_place:
            @pl.when(kk == nk - 1)
            def _():
                r = acc_ref[...]
                if has_add:
                    r = r + refs[2][...].astype(F32)
                o_ref[...] = r.astype(o_ref.dtype)

    ins = [a, b] + ([add] if has_add else [])
    in_specs = [a_spec, b_spec] + ([o_spec] if has_add else [])
    return _call(body, name=name, grid=(m // tm, n // tn, nk), in_specs=in_specs, out_specs=o_spec,
                 out_shape=jax.ShapeDtypeStruct((m, n), out_dtype),
                 scratch_shapes=[] if in_place else [pltpu.VMEM((tm, tn), F32)],
                 compiler_params=_params(3))(*ins)


def _row_spec(blk, width):
    return pl.BlockSpec((blk, width), lambda i: (i, 0))


def _whole_spec(shape):
    return pl.BlockSpec(shape, lambda i: (0,) * len(shape))


def _stage_fwd(fn, rows, params, outs, *, blk, name, n_rows=None):
    n = n_rows or rows[0].shape[0]
    nr, npar = len(rows), len(params)

    def body(*refs):
        vals = [r[...] for r in refs[:nr + npar]]
        res = fn(*vals)
        for o_ref, v in zip(refs[nr + npar:], res):
            o_ref[...] = v.astype(o_ref.dtype)

    return _call(body, name=name, grid=(n // blk,),
                 in_specs=[_row_spec(blk, r.shape[1]) for r in rows] + [_whole_spec(p.shape) for p in params],
                 out_specs=[_row_spec(blk, w) for w, _ in outs],
                 out_shape=[jax.ShapeDtypeStruct((n, w), dt) for w, dt in outs],
                 compiler_params=_params(1))(*rows, *params)


def _stage_bwd(fn, rows, params, cts, *, blk, name, row_grads, n_rows=None):
    n = n_rows or rows[0].shape[0]
    nr, npar, nct = len(rows), len(params), len(cts)

    def body(*refs):
        vals = [r[...].astype(F32) for r in refs[:nr + npar]]
        ct = [r[...] for r in refs[nr + npar:nr + npar + nct]]
        d_rows = refs[nr + npar + nct:nr + npar + nct + len(row_grads)]
        d_par = refs[nr + npar + nct + len(row_grads):]
        res, vjp = jax.vjp(fn, *vals)
        g = vjp(tuple(c.astype(r.dtype) for c, r in zip(ct, res)))
        for o_ref, (j, _) in zip(d_rows, row_grads):
            o_ref[...] = g[j].astype(o_ref.dtype)

        @pl.when(pl.program_id(0) == 0)
        def _():
            for o_ref in d_par:
                o_ref[...] = jnp.zeros_like(o_ref)

        for j, o_ref in enumerate(d_par):
            o_ref[...] += g[nr + j].astype(F32)

    return _call(body, name=name, grid=(n // blk,),
                 in_specs=([_row_spec(blk, r.shape[1]) for r in rows] + [_whole_spec(p.shape) for p in params]
                           + [_row_spec(blk, c.shape[1]) for c in cts]),
                 out_specs=([_row_spec(blk, rows[j].shape[1]) for j, _ in row_grads]
                            + [_whole_spec(p.shape) for p in params]),
                 out_shape=([jax.ShapeDtypeStruct((n, rows[j].shape[1]), dt) for j, dt in row_grads]
                            + [jax.ShapeDtypeStruct(p.shape, F32) for p in params]),
                 compiler_params=_params(1))(*rows, *params, *cts)


def _stage_loss(fn, rows, params, *, blk, name, row_grads):
    n = rows[0].shape[0]
    nr, npar = len(rows), len(params)

    def body(*refs):
        vals = [r[...].astype(F32) for r in refs[:nr + npar]]
        loss_ref = refs[nr + npar]
        d_rows = refs[nr + npar + 1:nr + npar + 1 + len(row_grads)]
        d_par = refs[nr + npar + 1 + len(row_grads):]
        res, vjp = jax.vjp(fn, *vals)
        g = vjp(jnp.ones_like(res))
        for o_ref, (j, _) in zip(d_rows, row_grads):
            o_ref[...] = g[j].astype(o_ref.dtype)

        @pl.when(pl.program_id(0) == 0)
        def _():
            loss_ref[...] = jnp.zeros_like(loss_ref)
            for o_ref in d_par:
                o_ref[...] = jnp.zeros_like(o_ref)

        loss_ref[...] += res
        for j, o_ref in enumerate(d_par):
            o_ref[...] += g[nr + j].astype(F32)

    return _call(body, name=name, grid=(n // blk,),
                 in_specs=[_row_spec(blk, r.shape[1]) for r in rows] + [_whole_spec(p.shape) for p in params],
                 out_specs=([_whole_spec((1, 1))] + [_row_spec(blk, rows[j].shape[1]) for j, _ in row_grads]
                            + [_whole_spec(p.shape) for p in params]),
                 out_shape=([jax.ShapeDtypeStruct((1, 1), F32)]
                            + [jax.ShapeDtypeStruct((n, rows[j].shape[1]), dt) for j, dt in row_grads]
                            + [jax.ShapeDtypeStruct(p.shape, F32) for p in params]),
                 compiler_params=_params(1))(*rows, *params)


def _rms(x, g):
    return x * lax.rsqrt(jnp.mean(x * x, axis=-1, keepdims=True) + EPS) * g


def _modulate(x, g, shift, scale):
    return _rms(x, g) * (1.0 + scale) + shift


def _fn_mod(cs, w_ada):
    return (mmb(jax.nn.silu(cs), w_ada),)


def _fn_a(x, g_mix, sh, sc):
    return (_b16(_modulate(x, g_mix, sh, sc)),)


def _fn_a_res(x, g_mix, sh, sc):
    return _b16(_modulate(x, g_mix, sh, sc)), x


def _sgu_spatial(v, sgu_w, sgu_bt):
    rows, width = v.shape
    gdim = width // (sgu_w.shape[0] // CHUNK)
    groups = width // gdim
    expand = (lax.broadcasted_iota(jnp.int32, (groups, width), 1) // gdim
              == lax.broadcasted_iota(jnp.int32, (groups, width), 0)).astype(F32)
    bias = mmf(sgu_bt, expand)
    lane = lax.broadcasted_iota(jnp.int32, (CHUNK, LANE), 1)
    per_lane_block = LANE // gdim
    chunks = []
    for ci in range(rows // CHUNK):
        vc = v[ci * CHUNK:(ci + 1) * CHUNK]
        blocks = []
        for lb in range(width // LANE):
            vb = vc[:, lb * LANE:(lb + 1) * LANE]
            acc = None
            for s in range(per_lane_block):
                g = lb * per_lane_block + s
                r = mmb(sgu_w[g * CHUNK:(g + 1) * CHUNK], vb)
                sel = (lane // gdim) == s
                acc = jnp.where(sel, r, 0.0) if acc is None else jnp.where(sel, r, acc)
            blocks.append(acc)
        chunks.append(jnp.concatenate(blocks, axis=1) + bias)
    return jnp.concatenate(chunks, axis=0)


def _fn_b(pu, prest, ysc, s5_d, w_glu, b_glu, ln_g, ln_b, sgu_w, sgu_bt, w_pa, w_pb, b_gate):
    sw = ln_g.shape[1]
    y = jax.nn.gelu(pu * s5_d + ysc)
    ya = y * jax.nn.sigmoid(mmb(y, w_glu) + b_glu)
    z = jax.nn.gelu(prest[:, :2 * sw])
    u, v = z[:, :sw], z[:, sw:]
    vc = v - jnp.mean(v, axis=-1, keepdims=True)
    v = vc * lax.rsqrt(jnp.mean(vc * vc, axis=-1, keepdims=True) + EPS) * ln_g + ln_b
    yb = u * _sgu_spatial(v, sgu_w, sgu_bt)
    gates = jax.nn.sigmoid(prest[:, 2 * sw:] + b_gate)
    d = gates.shape[1] // 2
    return (_b16(gates[:, :d] * mmb(ya, w_pa) + gates[:, d:] * mmb(yb, w_pb)),)


def _fn_c(x, mo, ga1, g_ffn, sh2, sc2):
    x1 = x + ga1 * mo
    return x1, _b16(_modulate(x1, g_ffn, sh2, sc2))


def _fn_e(x1, dn, tgt, ga2, g_final):
    y = _rms(x1 + ga2 * dn, g_final)
    err = (y - tgt) ** 2
    return 0.5 * jnp.sum(jnp.mean(err, axis=-1, keepdims=True), axis=0, keepdims=True)


def _swap_impl(x):
    return pltpu.roll(x, x.shape[1] // 2, 1)


@jax.custom_vjp
def _swap_halves(x):
    return _swap_impl(x)


_swap_halves.defvjp(lambda x: (_swap_impl(x), None), lambda _, g: (_swap_impl(g),))


def _s5_direction(u, mask, a_re, a_im, log_step, bt, c, rev):
    nc, width = u.shape
    t_len = width // S5_H
    n2 = a_re.shape[1]
    lane = lax.broadcasted_iota(jnp.int32, (1, n2), 1)
    sign = jnp.where(lane < n2 // 2, -1.0, 1.0)
    dt = jnp.exp(log_step)
    lr, li = a_re * dt, a_im * dt
    mag = jnp.exp(lr)
    ab_re, ab_im = mag * jnp.cos(li), mag * jnp.sin(li)
    p, q = ab_re - 1.0, ab_im
    den = a_re * a_re + a_im * a_im
    k_re, k_im = (p * a_re + q * a_im) / den, (q * a_re - p * a_im) / den
    bb = k_re * bt + (k_im * sign) * _swap_halves(bt)

    def power(e):
        m = jnp.exp(lr * e)
        return m * jnp.cos(li * e), m * jnp.sin(li * e)

    order = range(t_len - 1, -1, -1) if rev else range(t_len)
    e1 = jnp.concatenate([jnp.full((1, 1, n2), float(t_len - 1 - pos), F32) for pos in order], axis=0)
    lr3, li3, sign3 = lr.reshape(1, 1, n2), li.reshape(1, 1, n2), sign.reshape(1, 1, n2)
    m1, c1, s1 = jnp.exp(lr3 * e1), jnp.cos(li3 * e1), jnp.sin(li3 * e1)
    m2 = jnp.exp(-(lr3 * e1))
    x1r, x1i = m1 * c1, m1 * s1
    x2r, x2i = m2 * c1, -(m2 * s1)
    at_r, at_i = [v.reshape(1, 1, n2) for v in power(float(t_len))]
    x3r, x3i = at_r * x2r - at_i * x2i, at_r * x2i + at_i * x2r

    def rows(xr, xi, z):
        z3, zs3 = z.reshape(1, S5_H, n2), _swap_halves(z).reshape(1, S5_H, n2)
        return (xr * z3 + (xi * sign3) * zs3).reshape(width, n2)

    p_in = rows(x1r, x1i, bb)
    r_out = rows(x2r, x2i, c)
    w_out = rows(x3r, x3i, c)
    toep = mm3_nt(p_in, r_out * (-sign)) * mask

    state = mmb(u, p_in)
    k = 1
    while k < nc:
        ar, ai = power(float(t_len * k))
        moved = _shift(state, k, rev)
        state = state + ar * moved + (ai * sign) * _swap_halves(moved)
        k *= 2
    entering = _shift(state, 1, rev)
    return mmb(u, toep) + mmb_nt(entering, w_out * (-sign))


def _fn_s5(masks, x_chunks, ctx_chunks, *prm):
    nx, nctx = x_chunks.shape[0], ctx_chunks.shape[0]
    pad = (-(nx + 2 * nctx)) % LANE
    u = jnp.concatenate([ctx_chunks, x_chunks, ctx_chunks, jnp.zeros((pad, x_chunks.shape[1]), F32)], axis=0)
    out = None
    for d in range(2):
        y = _s5_direction(u, masks[d], *[p[d] for p in prm], rev=(d == 1))
        out = y if out is None else out + y
    return out[nctx:nctx + nx]


def _s5_specs(prm):
    return [pl.BlockSpec((2, 1) + p.shape[2:], lambda g: (0, g, 0, 0)) for p in prm]


def _group_spec(a):
    return pl.BlockSpec((1,) + a.shape[1:], lambda i: (i, 0, 0))


def _s5_masks(width):
    pos = jnp.arange(width) // S5_H
    causal = (pos[None, :] >= pos[:, None]).astype(F32)
    return jnp.stack([causal, causal.T])


def _s5_fwd(x_g, ctx_g, prm):
    masks = _s5_masks(x_g.shape[2])

    def body(*refs):
        pv = [r[:, 0] for r in refs[3:3 + len(prm)]]
        refs[-1][0] = _fn_s5(refs[0][...], refs[1][0].astype(F32), refs[2][0], *pv)

    return _call(body, name="s5_fwd", grid=(x_g.shape[0],),
                 in_specs=[_whole_spec(masks.shape), _group_spec(x_g), _group_spec(ctx_g)] + _s5_specs(prm),
                 out_specs=_group_spec(x_g), out_shape=jax.ShapeDtypeStruct(x_g.shape, F32),
                 compiler_params=_params(1))(masks, x_g, ctx_g, *prm)


def _s5_bwd(x_g, ctx_g, prm, dy_g, skip_g):
    npar = len(prm)
    masks = _s5_masks(x_g.shape[2])

    def body(*refs):
        pv = [r[:, 0] for r in refs[3:3 + npar]]
        dy = refs[3 + npar][0]
        _, vjp = jax.vjp(functools.partial(_fn_s5, refs[0][...]), refs[1][0].astype(F32), refs[2][0], *pv)
        grads = vjp(dy)
        outs = refs[5 + npar:]
        outs[0][0] = (grads[0] + dy * refs[4 + npar][0]).astype(outs[0].dtype)
        outs[1][0] = grads[1]
        for o_ref, gv in zip(outs[2:], grads[2:]):
            o_ref[:, 0] = gv

    return _call(body, name="s5_bwd", grid=(x_g.shape[0],),
                 in_specs=([_whole_spec(masks.shape), _group_spec(x_g), _group_spec(ctx_g)] + _s5_specs(prm)
                           + [_group_spec(dy_g), _group_spec(skip_g)]),
                 out_specs=[_group_spec(x_g), _group_spec(ctx_g)] + _s5_specs(prm),
                 out_shape=[jax.ShapeDtypeStruct(x_g.shape, BF16), jax.ShapeDtypeStruct(ctx_g.shape, F32)]
                 + [jax.ShapeDtypeStruct(p.shape, F32) for p in prm],
                 compiler_params=_params(1))(masks, x_g, ctx_g, *prm, dy_g, skip_g)


def _to_groups(tok):
    n, width = tok.shape
    g = width // S5_H
    return jnp.transpose(tok.reshape(n, g, S5_H), (1, 0, 2)).reshape(g, n // S5_T, S5_T * S5_H)


def _from_groups(grp):
    g, nc, _ = grp.shape
    return jnp.transpose(grp.reshape(g, nc * S5_T, S5_H), (1, 0, 2)).reshape(nc * S5_T, g * S5_H)


def _conv_taps(xp, xm, xn, blk_i, n_blk):
    tb = xm.shape[0]
    xp = jnp.where(blk_i == 0, 0.0, xp.astype(F32))
    xn = jnp.where(blk_i == n_blk - 1, 0.0, xn.astype(F32))
    buf = jnp.concatenate([xp, xm.astype(F32), xn], axis=0)
    n = tb + 2 * HALO
    col = lax.broadcasted_iota(jnp.int32, (n, 1), 0) % GRID_W
    left = jnp.where(col >= 1, pltpu.roll(buf, 1, 0), 0.0)
    right = jnp.where(col <= GRID_W - 2, pltpu.roll(buf, n - 1, 0), 0.0)
    shifted = (left, buf, right)
    taps = []
    for di in range(3):
        start = HALO + (di - 1) * GRID_W
        for dj in range(3):
            taps.append(shifted[dj][start:start + tb])
    return taps


def _conv_sum(taps, w_ref, flip=False):
    acc = None
    for k, tap in enumerate(taps):
        j = len(taps) - 1 - k if flip else k
        term = tap * w_ref[j:j + 1, :]
        acc = term if acc is None else acc + term
    return acc


def _conv_geometry(n_tok, width):
    tb = _pick(n_tok, 1024, HALO)
    cb = _pick(width, 256)
    nb = tb // HALO
    last = n_tok // HALO - 1
    main = pl.BlockSpec((tb, cb), lambda j, i: (i, j))
    prev = pl.BlockSpec((HALO, cb), lambda j, i: (jnp.maximum(i * nb - 1, 0), j))
    nxt = pl.BlockSpec((HALO, cb), lambda j, i: (jnp.minimum(i * nb + nb, last), j))
    par = lambda r: pl.BlockSpec((r, cb), lambda j, i: (0, j))
    return tb, cb, main, prev, nxt, par


def _conv_act_fwd(up_g, up_v, w_g, w_v, b_g, b_v):
    n_tok, width = up_g.shape
    tb, cb, main, prev, nxt, par = _conv_geometry(n_tok, width)
    n_blk = n_tok // tb

    def body(gp, gm, gn, vp, vm, vn, wg, wv, bg, bv, o_ref, gate_ref, val_ref):
        i = pl.program_id(1)
        gate = _conv_sum(_conv_taps(gp[...], gm[...], gn[...], i, n_blk), wg) + bg[...]
        val = _conv_sum(_conv_taps(vp[...], vm[...], vn[...], i, n_blk), wv) + bv[...]
        o_ref[...] = (jax.nn.silu(gate) * val).astype(o_ref.dtype)
        gate_ref[...] = gate
        val_ref[...] = val

    shp = jax.ShapeDtypeStruct
    return _call(body, name="conv_act_fwd", grid=(width // cb, n_tok // tb),
                 in_specs=[prev, main, nxt, prev, main, nxt, par(9), par(9), par(1), par(1)],
                 out_specs=[main, main, main],
                 out_shape=[shp((n_tok, width), BF16), shp((n_tok, width), F32), shp((n_tok, width), F32)],
                 compiler_params=_params(2))(up_g, up_g, up_g, up_v, up_v, up_v, w_g, w_v, b_g, b_v)


def _conv_act_bwd(up_g, up_v, gate_c, val_c, d_act):
    n_tok, width = up_g.shape
    tb, cb, main, prev, nxt, par = _conv_geometry(n_tok, width)
    n_blk = n_tok // tb

    def body(gp, gm, gn, vp, vm, vn, gc, vc, da, dcg, dcv, dwg, dwv, dbg, dbv):
        i = pl.program_id(1)
        taps_g = _conv_taps(gp[...], gm[...], gn[...], i, n_blk)
        taps_v = _conv_taps(vp[...], vm[...], vn[...], i, n_blk)
        gate, val = gc[...], vc[...]
        sig = jax.nn.sigmoid(gate)
        d = da[...].astype(F32)
        d_gate = d * val * sig * (1.0 + gate * (1.0 - sig))
        d_val = d * gate * sig
        dcg[...] = d_gate.astype(dcg.dtype)
        dcv[...] = d_val.astype(dcv.dtype)

        @pl.when(i == 0)
        def _():
            for r in (dwg, dwv, dbg, dbv):
                r[...] = jnp.zeros_like(r)

        dbg[...] += jnp.sum(d_gate, axis=0, keepdims=True)
        dbv[...] += jnp.sum(d_val, axis=0, keepdims=True)
        for k in range(9):
            dwg[k:k + 1, :] += jnp.sum(taps_g[k] * d_gate, axis=0, keepdims=True)
            dwv[k:k + 1, :] += jnp.sum(taps_v[k] * d_val, axis=0, keepdims=True)

    shp = jax.ShapeDtypeStruct
    return _call(body, name="conv_act_bwd", grid=(width // cb, n_tok // tb),
                 in_specs=[prev, main, nxt, prev, main, nxt, main, main, main],
                 out_specs=[main, main, par(9), par(9), par(1), par(1)],
                 out_shape=[shp((n_tok, width), BF16), shp((n_tok, width), BF16), shp((9, width), F32),
                            shp((9, width), F32), shp((1, width), F32), shp((1, width), F32)],
                 compiler_params=_params(2))(up_g, up_g, up_g, up_v, up_v, up_v, gate_c, val_c, d_act)


def _conv_transposed(x, w, name):
    n_tok, width = x.shape
    tb, cb, main, prev, nxt, par = _conv_geometry(n_tok, width)
    n_blk = n_tok // tb

    def body(xp, xm, xn, w_ref, o_ref):
        taps = _conv_taps(xp[...], xm[...], xn[...], pl.program_id(1), n_blk)
        o_ref[...] = _conv_sum(taps, w_ref, flip=True).astype(o_ref.dtype)

    return _call(body, name=name, grid=(width // cb, n_tok // tb), in_specs=[prev, main, nxt, par(9)],
                 out_specs=main, out_shape=jax.ShapeDtypeStruct((n_tok, width), BF16),
                 compiler_params=_params(2))(x, x, x, w)


def _adamw(w, g_parts, m, v, name, own=None, me=None):
    rows, cols = w.shape
    parts = g_parts.shape[0]
    blk = _pick(rows, 256, 8)
    spec = pl.BlockSpec((blk, cols), lambda i: (i, 0))
    has_own = own is not None

    def body(*refs):
        w_ref, g_ref, m_ref, v_ref = refs[:4]
        g_out, d_out, m_out, v_out = refs[-4:]

        def part(p):
            if has_own:
                return jnp.where(refs[5][...] == p, refs[4][...], g_ref[p]).astype(F32)
            return g_ref[p].astype(F32)

        g = part(0)
        for p in range(1, parts):
            g = g + part(p)
        m_new = ADAM_B1 * m_ref[...] + (1.0 - ADAM_B1) * g
        v_new = ADAM_B2 * v_ref[...] + (1.0 - ADAM_B2) * (g * g)
        m_hat = m_new / (1.0 - ADAM_B1 ** ADAM_STEP)
        v_hat = v_new / (1.0 - ADAM_B2 ** ADAM_STEP)
        g_out[...] = g
        d_out[...] = -ADAM_LR * (m_hat / (jnp.sqrt(v_hat) + ADAM_EPS) + ADAM_WD * w_ref[...])
        m_out[...] = m_new
        v_out[...] = v_new

    extra = [own, me] if has_own else []
    return _call(body, name=name, grid=(rows // blk,),
                 in_specs=([spec, pl.BlockSpec((parts, blk, cols), lambda i: (0, i, 0)), spec, spec]
                           + ([spec, _whole_spec((1, 1))] if has_own else [])),
                 out_specs=[spec] * 4, out_shape=[jax.ShapeDtypeStruct((rows, cols), F32)] * 4,
                 compiler_params=_params(1))(w, g_parts, m, v, *extra)


def _exchange(items, name):
    n = len(items)
    hbm = pl.BlockSpec(memory_space=pl.ANY)

    def body(*refs):
        srcs, outs = refs[:n], refs[n:2 * n]
        send_sems, recv_sems, own_sems = refs[2 * n:]
        x, y, c = lax.axis_index("x"), lax.axis_index("y"), lax.axis_index("c")
        me = 4 * x + 2 * y + c
        own = []
        for i, (_, mode) in enumerate(items):
            src = srcs[i] if mode == "gather" else srcs[i].at[me]
            cp = pltpu.make_async_copy(src, outs[i].at[me], own_sems.at[i])
            cp.start()
            own.append(cp)
        sent = []
        for i, (_, mode) in enumerate(items):
            for k in range(1, N_DEV):
                px = 1 - x if k & 4 else x
                py = 1 - y if k & 2 else y
                pc = 1 - c if k & 1 else c
                peer = 4 * px + 2 * py + pc
                src = srcs[i] if mode == "gather" else srcs[i].at[peer]
                cp = pltpu.make_async_remote_copy(
                    src_ref=src, dst_ref=outs[i].at[me], send_sem=send_sems.at[i, k - 1],
                    recv_sem=recv_sems.at[i, k - 1], device_id=(px, py, pc), device_id_type=pl.DeviceIdType.MESH)
                cp.start()
                landing = pltpu.make_async_remote_copy(
                    src_ref=src, dst_ref=outs[i].at[peer], send_sem=send_sems.at[i, k - 1],
                    recv_sem=recv_sems.at[i, k - 1], device_id=(px, py, pc), device_id_type=pl.DeviceIdType.MESH)
                sent.append((cp, landing))
        for cp in own:
            cp.wait()
        for cp, landing in sent:
            cp.wait_send()
            landing.wait_recv()

    out_shape = [jax.ShapeDtypeStruct((N_DEV,) + (a.shape if mode == "gather" else a.shape[1:]), a.dtype)
                 for a, mode in items]
    return _call(body, name=name, in_specs=[hbm] * n, out_specs=[hbm] * n, out_shape=out_shape,
                 scratch_shapes=[pltpu.SemaphoreType.DMA((n, N_DEV - 1)), pltpu.SemaphoreType.DMA((n, N_DEV - 1)),
                                 pltpu.SemaphoreType.DMA((n,))])(*[a for a, _ in items])


def _peer(k, x, y, c):
    px = 1 - x if k & 4 else x
    py = 1 - y if k & 2 else y
    pc = 1 - c if k & 1 else c
    return (px, py, pc), 4 * px + 2 * py + pc


_HBM_SPEC = pl.BlockSpec(memory_space=pltpu.HBM)
_SEM_SPEC = pl.BlockSpec(memory_space=pltpu.SEMAPHORE)
_SPLIT_EFFECT = pltpu.SideEffectType.DATAFLOW_SIDE_EFFECTING


def _exchange_begin(items, name, after=None):
    n = len(items)
    modes = [mode for _, mode in items]
    srcs = [pltpu.with_memory_space_constraint(a, pltpu.HBM) for a, _ in items]
    lands = [pltpu.with_memory_space_constraint(
        lax.empty((N_DEV,) + (a.shape if mode == "gather" else a.shape[1:]), a.dtype), pltpu.HBM)
        for a, mode in items]

    def body(*refs):
        src_refs, land_refs = refs[:n], refs[n:2 * n]
        token = refs[-1]
        send_sems, recv_sems = refs[-2 * n - 3], refs[-2 * n - 2]
        x, y, c = lax.axis_index("x"), lax.axis_index("y"), lax.axis_index("c")
        me = 4 * x + 2 * y + c
        for i in range(n):
            for k in range(1, N_DEV):
                coords, peer = _peer(k, x, y, c)
                src = src_refs[i] if modes[i] == "gather" else src_refs[i].at[peer]
                pltpu.make_async_remote_copy(
                    src_ref=src, dst_ref=land_refs[i].at[me], send_sem=send_sems.at[i * (N_DEV - 1) + k - 1],
                    recv_sem=recv_sems.at[i * (N_DEV - 1) + k - 1], device_id=coords,
                    device_id_type=pl.DeviceIdType.MESH).start()
        token[...] = jnp.zeros_like(token)

    sems = pltpu.SemaphoreType.DMA((n * (N_DEV - 1),))
    order = [] if after is None else [after]
    res = _call(body, name=name,
                out_shape=(sems, sems, *[pltpu.HBM(a.shape, a.dtype) for a in srcs + lands],
                           jax.ShapeDtypeStruct((8, LANE), F32)),
                in_specs=[_HBM_SPEC] * (2 * n) + [pl.BlockSpec(memory_space=pl.ANY)] * len(order),
                out_specs=(_SEM_SPEC, _SEM_SPEC, *[_HBM_SPEC] * (2 * n), pl.BlockSpec(memory_space=pltpu.VMEM)),
                input_output_aliases={i: 2 + i for i in range(2 * n)},
                compiler_params=pltpu.CompilerParams(has_side_effects=_SPLIT_EFFECT))(*srcs, *lands, *order)
    return (modes, res[0], res[1], list(res[2:2 + n]), list(res[2 + n:2 + 2 * n])), res[-1][0, 0]


def _exchange_end(handle, after, name):
    modes, send_sems, recv_sems, srcs, lands = handle
    n = len(modes)

    def wait_body(*refs):
        src_refs, land_refs = refs[:n], refs[n:2 * n]
        send, recv = refs[2 * n], refs[2 * n + 1]
        x, y, c = lax.axis_index("x"), lax.axis_index("y"), lax.axis_index("c")
        for i in range(n):
            for k in range(1, N_DEV):
                coords, peer = _peer(k, x, y, c)
                src = src_refs[i] if modes[i] == "gather" else src_refs[i].at[peer]
                cp = pltpu.make_async_remote_copy(
                    src_ref=src, dst_ref=land_refs[i].at[peer], send_sem=send.at[i * (N_DEV - 1) + k - 1],
                    recv_sem=recv.at[i * (N_DEV - 1) + k - 1], device_id=coords,
                    device_id_type=pl.DeviceIdType.MESH)
                cp.wait_send()
                cp.wait_recv()

    res = _call(wait_body, name=name, out_shape=[pltpu.HBM(a.shape, a.dtype) for a in srcs + lands],
                in_specs=[_HBM_SPEC] * (2 * n) + [_SEM_SPEC, _SEM_SPEC, pl.BlockSpec(memory_space=pl.ANY)],
                out_specs=[_HBM_SPEC] * (2 * n), input_output_aliases={i: i for i in range(2 * n)},
                compiler_params=pltpu.CompilerParams(has_side_effects=_SPLIT_EFFECT))(
                    *srcs, *lands, send_sems, recv_sems, after)
    return res[n:]


def _with_own(land, own, me):
    slot = lax.broadcasted_iota(jnp.int32, (N_DEV,) + (1,) * (land.ndim - 1), 0)
    return jnp.where(slot == me, own[None], land)


def _cols_from_blocks(g):
    return jnp.transpose(g, (1, 0, 2)).reshape(g.shape[1], N_DEV * g.shape[2])


def _blocks_from_cols(w):
    r, c8 = w.shape
    return jnp.transpose(w.reshape(r, N_DEV, c8 // N_DEV), (1, 0, 2))


def _pack(arrs):
    flat = jnp.concatenate([a.reshape(-1).astype(F32) for a in arrs])
    rows = -(-flat.shape[0] // LANE)
    rows = -(-rows // PACK_ROWS) * PACK_ROWS if rows > PACK_ROWS else -(-rows // 8) * 8
    return jnp.pad(flat, (0, rows * LANE - flat.shape[0])).reshape(rows, LANE)


def _unpack(packed, shapes):
    flat = packed.reshape(-1)
    out, off = [], 0
    for s in shapes:
        size = math.prod(s)
        out.append(flat[off:off + size].reshape(s))
        off += size
    return out


def kernel(x, c, ctx, c_ctx, w_ada, b_ada, g_mix, w_in, s5_a_re, s5_a_im, s5_log_step, s5_b_re, s5_b_im, s5_c_re, s5_c_im, s5_d, s5_w_glu, s5_b_glu, sgu_ln_g, sgu_ln_b, sgu_w, sgu_b, w_proj_a, w_proj_b, b_gate, w_out, g_ffn, w_up, conv_w, conv_b, w_down, g_final, loss_target, m_c_ctx, m_w_ada, m_b_ada, m_g_mix, m_w_in, m_s5_a_re, m_s5_a_im, m_s5_log_step, m_s5_b_re, m_s5_b_im, m_s5_c_re, m_s5_c_im, m_s5_d, m_s5_w_glu, m_s5_b_glu, m_sgu_ln_g, m_sgu_ln_b, m_sgu_w, m_sgu_b, m_w_proj_a, m_w_proj_b, m_b_gate, m_w_out, m_g_ffn, m_w_up, m_conv_w, m_conv_b, m_w_down, m_g_final, v_c_ctx, v_w_ada, v_b_ada, v_g_mix, v_w_in, v_s5_a_re, v_s5_a_im, v_s5_log_step, v_s5_b_re, v_s5_b_im, v_s5_c_re, v_s5_c_im, v_s5_d, v_s5_w_glu, v_s5_b_glu, v_sgu_ln_g, v_sgu_ln_b, v_sgu_w, v_sgu_b, v_w_proj_a, v_w_proj_b, v_b_gate, v_w_out, v_g_ffn, v_w_up, v_conv_w, v_conv_b, v_w_down, v_g_final):
    given = dict(locals())
    wts = {n: given[n] for n in WEIGHTS}
    mom1 = {n: given["m_" + n] for n in WEIGHTS}
    mom2 = {n: given["v_" + n] for n in WEIGHTS}

    me = 4 * lax.axis_index("x") + 2 * lax.axis_index("y") + lax.axis_index("c")
    xs, cx, tgt = x[0], ctx[0], loss_target[0]
    n_tok, d = xs.shape
    n_ctx = cx.shape[0]
    s5w = s5_d.shape[1]
    ffn = w_down.shape[1] * N_DEV
    n_mod = w_ada.shape[2] * N_DEV // d
    mod_cols = w_ada.shape[2]

    def two_d(a):
        return a.reshape(-1, a.shape[-1])

    conv_w9 = conv_w[0].reshape(9, -1)
    gathered = _exchange([(c, "gather"), (_b16(w_in[0]), "gather")], "gather_weights")
    c_all = gathered[0].reshape(N_DEV, d)
    w_in_f = _cols_from_blocks(gathered[1])
    w_in_u, w_in_rest = w_in_f[:, :s5w], w_in_f[:, s5w:]

    cs_in = jnp.concatenate([c_all, jnp.broadcast_to(c_ctx[None, :], (N_DEV, d))], axis=0)
    w_ada_loc = w_ada[0]
    (mod_mine,) = _stage_fwd(_fn_mod, [cs_in], [w_ada_loc], [(mod_cols, F32)], blk=2 * N_DEV, name="mod_fwd")
    (mod_blocks,) = _exchange([(mod_mine, "gather")], "gather_mod")
    mid_own = [_b16(s5_w_glu[0]), _b16(w_proj_a[0]), _b16(w_proj_b[0])]
    mid_weights, tok_mid = _exchange_begin([(a, "gather") for a in mid_own], "gather_mid_begin", after=mod_blocks)
    late_own = [_b16(w_out[0]), _b16(w_up[0]), conv_w9, _b16(w_down[0])]
    late_weights, tok = _exchange_begin([(a, "gather") for a in late_own], "gather_late_begin", after=mod_blocks)
    mod_all = _cols_from_blocks(mod_blocks) + b_ada + (tok + tok_mid)
    mod = lax.dynamic_slice_in_dim(mod_all, me, 1, axis=0)
    mod_c = mod_all[N_DEV:N_DEV + 1]
    sh1, sc1, ga1, sh2, sc2, ga2 = [mod[:, i * d:(i + 1) * d] for i in range(n_mod)]
    sh1c, sc1c = mod_c[:, :d], mod_c[:, d:2 * d]

    a_par = [g_mix, sh1, sc1]
    ac_par = [g_mix, sh1c, sc1c]
    (h,) = _stage_fwd(_fn_a, [xs], a_par, [(d, BF16)], blk=_pick(n_tok, 512, 8), name="modulate1_fwd")
    (hc,) = _stage_fwd(_fn_a, [cx], ac_par, [(d, BF16)], blk=_pick(n_ctx, 512, 8), name="modulate1_ctx_fwd")
    pu = _mm(h, w_in_u, name="proj_u")
    prest = _mm(h, w_in_rest, name="proj_rest")
    puc = _mm(hc, w_in_u, name="proj_u_ctx")

    n_state = s5_a_re.shape[-1]

    def twice(a):
        return jnp.concatenate([a, a], axis=-1)

    s5_prm = [twice(s5_a_re[0])[:, :, None, :], twice(s5_a_im[0])[:, :, None, :], s5_log_step[0][:, :, None, None],
              jnp.concatenate([jnp.swapaxes(s5_b_re[0], 2, 3), jnp.swapaxes(s5_b_im[0], 2, 3)], axis=-1),
              jnp.concatenate([s5_c_re[0], s5_c_im[0]], axis=-1)]
    pu_g, puc_g = _to_groups(_b16(pu)), _to_groups(puc)
    ysc = _from_groups(_s5_fwd(pu_g, puc_g, s5_prm))

    mid = [_with_own(land, own, me) for land, own in zip(_exchange_end(mid_weights, ysc, "gather_mid_end"), mid_own)]
    w_glu_f, w_pa_f, w_pb_f = mid[0].reshape(-1, s5w), _cols_from_blocks(mid[1]), _cols_from_blocks(mid[2])
    b_par = [s5_d, w_glu_f, s5_b_glu, sgu_ln_g, sgu_ln_b, two_d(sgu_w[0]), jnp.transpose(sgu_b[0]),
             w_pa_f, w_pb_f, b_gate]
    b_rows = [pu, prest, ysc]
    b_blk = _pick(n_tok, 512, CHUNK)
    (mpre,) = _stage_fwd(_fn_b, b_rows, b_par, [(d, BF16)], blk=b_blk, name="mixers_fwd")
    late = [_with_own(land, own, me) for land, own in zip(_exchange_end(late_weights, mpre, "gather_late_end"),
                                                          late_own)]
    w_out_f = late[0].reshape(-1, d)
    w_up_f = _cols_from_blocks(late[1])
    w_up_g, w_up_v = w_up_f[:, :ffn], w_up_f[:, ffn:]
    conv_w_f = _cols_from_blocks(late[2])
    w_down_f = late[3].reshape(-1, d)
    mo = _mm(mpre, w_out_f, name="out_proj")

    c_par = [ga1, g_ffn, sh2, sc2]
    c_blk = _pick(n_tok, 512, 8)
    x1, h2 = _stage_fwd(_fn_c, [xs, mo], c_par, [(d, F32), (d, BF16)], blk=c_blk, name="modulate2_fwd")
    up_g = _mm(h2, w_up_g, out_dtype=BF16, name="up_gate")
    up_v = _mm(h2, w_up_v, out_dtype=BF16, name="up_val")
    cw_g, cw_v = conv_w_f[:, :ffn], conv_w_f[:, ffn:]
    cb_g, cb_v = conv_b[:, :ffn], conv_b[:, ffn:]
    act, gate_c, val_c = _conv_act_fwd(up_g, up_v, cw_g, cw_v, cb_g, cb_v)
    dn = _mm(act, w_down_f, name="down_proj")

    loss_part, d_x1a, d_dn, d_ga2, d_g_final = _stage_loss(
        _fn_e, [x1, dn, tgt], [ga2, g_final[None, :]], blk=c_blk, name="loss_head",
        row_grads=[(0, F32), (1, BF16)])

    d_act = _mm(d_dn, w_down_f, tb=True, name="down_proj_dx")
    g_w_down = _mm(act, d_dn, ta=True, out_dtype=BF16, name="down_proj_dw")

    own_block = {}

    def grad_blocks(name, g):
        blocks = _b16(_blocks_from_cols(g) if name in COL_SHARDED
                      else g.reshape(N_DEV, g.shape[0] // N_DEV, g.shape[1]))
        own_block[name] = lax.dynamic_index_in_dim(blocks, me, 0, keepdims=False)
        return blocks, "a2a"

    sent_down, tok = _exchange_begin([grad_blocks('w_down', g_w_down)], "grads_down_begin")
    dcg, dcv, g_cw_g, g_cw_v, g_cb_g, g_cb_v = _conv_act_bwd(up_g, up_v, gate_c, val_c, d_act)
    dug = _conv_transposed(dcg, cw_g + tok, "conv_dx_gate")
    duv = _conv_transposed(dcv, cw_v, "conv_dx_val")
    d_h2 = _mm(dug, w_up_g, tb=True, name="up_gate_dx")
    d_h2 = _mm(duv, w_up_v, tb=True, add=d_h2, out_dtype=BF16, name="up_val_dx")
    g_w_up = jnp.concatenate([_mm(h2, dug, ta=True, out_dtype=BF16, name="up_gate_dw"),
                              _mm(h2, duv, ta=True, out_dtype=BF16, name="up_val_dw")], axis=1)
    sent_up, tok = _exchange_begin(
        [grad_blocks('w_up', g_w_up), grad_blocks('conv_w', jnp.concatenate([g_cw_g, g_cw_v], axis=1))],
        "grads_up_begin")
    (d_xc, d_mo), (d_ga1, g_g_ffn, d_sh2, d_sc2) = _split(_stage_bwd(
        _fn_c, [xs, mo], [ga1 + tok] + c_par[1:], [d_x1a, d_h2], blk=c_blk, name="modulate2_bwd",
        row_grads=[(0, F32), (1, BF16)]), 2)

    d_mpre = _mm(d_mo, w_out_f, tb=True, out_dtype=BF16, name="out_proj_dx")
    g_w_out = _mm(mpre, d_mo, ta=True, out_dtype=BF16, name="out_proj_dw")
    (d_prest, d_ysc), b_grads = _split(_stage_bwd(
        _fn_b, b_rows, b_par, [d_mpre], blk=_pick(n_tok, 256, CHUNK), name="mixers_bwd",
        row_grads=[(1, BF16), (2, F32)]), 2)
    (g_s5_d, g_w_glu, g_b_glu, g_ln_g, g_ln_b, g_sgu_w, g_sgu_bt, g_w_pa, g_w_pb, g_b_gate) = b_grads

    sent_mix, tok = _exchange_begin(
        [grad_blocks('w_out', g_w_out), grad_blocks('s5_w_glu', g_w_glu), grad_blocks('w_proj_a', g_w_pa),
         grad_blocks('w_proj_b', g_w_pb)], "grads_mixer_begin")
    skip_g = jnp.tile(s5_d.reshape(-1, 1, S5_H), (1, 1, S5_T)) + tok
    s5_out = _s5_bwd(pu_g, puc_g, s5_prm, _to_groups(d_ysc), skip_g)
    g_a_re2, g_a_im2, g_ls, g_bt2, g_c2 = s5_out[2:]
    g_a_re = g_a_re2[..., :n_state] + g_a_re2[..., n_state:]
    g_a_im = g_a_im2[..., :n_state] + g_a_im2[..., n_state:]
    g_bt_re, g_bt_im = g_bt2[..., :n_state], g_bt2[..., n_state:]
    g_c_re, g_c_im = g_c2[..., :n_state], g_c2[..., n_state:]

    part = {
        's5_a_re': g_a_re, 's5_a_im': g_a_im, 's5_log_step': g_ls,
        's5_b_re': jnp.swapaxes(g_bt_re, 2, 3), 's5_b_im': jnp.swapaxes(g_bt_im, 2, 3),
        's5_c_re': g_c_re, 's5_c_im': g_c_im, 's5_d': g_s5_d, 's5_b_glu': g_b_glu, 'sgu_ln_g': g_ln_g,
        'sgu_ln_b': g_ln_b, 'sgu_w': g_sgu_w, 'sgu_b': jnp.transpose(g_sgu_bt), 'b_gate': g_b_gate,
        'g_ffn': g_g_ffn, 'conv_b': jnp.concatenate([g_cb_g, g_cb_v], axis=1), 'g_final': d_g_final,
    }
    early = [n for n in REPLICATED if n not in LATE_REPLICATED and n not in UNPACKED_REPLICATED]
    early_part = _pack([part[n] for n in early])
    own_small = {n: two_d(part[n]) for n in UNPACKED_REPLICATED}
    sent_small, tok = _exchange_begin(
        [(early_part, "gather")] + [(own_small[n], "gather") for n in UNPACKED_REPLICATED], "grads_small_begin")
    d_pu, d_puc = _from_groups(s5_out[0]), _from_groups(s5_out[1]) + tok

    d_h = _mm(d_pu, w_in_u, tb=True, name="proj_u_dx")
    d_h = _mm(d_prest, w_in_rest, tb=True, add=d_h, out_dtype=BF16, name="proj_rest_dx")
    d_hc = _mm(d_puc, w_in_u, tb=True, out_dtype=BF16, name="proj_u_ctx_dx")
    g_w_in_u = _mm(hc, d_puc, ta=True, name="proj_u_ctx_dw")
    g_w_in_u = _mm(h, d_pu, ta=True, add=g_w_in_u, out_dtype=BF16, name="proj_u_dw")
    g_w_in = jnp.concatenate([g_w_in_u, _mm(h, d_prest, ta=True, out_dtype=BF16, name="proj_rest_dw")], axis=1)
    sent_in, tok = _exchange_begin([grad_blocks('w_in', g_w_in)], "grads_in_begin")

    (grad_x,), (g_g_mix_x, d_sh1, d_sc1) = _split(_stage_bwd(
        _fn_a_res, [xs], [g_mix + tok] + a_par[1:], [d_h, d_xc], blk=c_blk, name="modulate1_bwd",
        row_grads=[(0, F32)]), 1)
    _, (g_g_mix_c, d_sh1c, d_sc1c) = _split(_stage_bwd(
        _fn_a, [cx], ac_par, [d_hc], blk=_pick(n_ctx, 512, 8), name="modulate1_ctx_bwd", row_grads=[]), 0)

    zeros = jnp.zeros((1, (n_mod - 2) * d), F32)
    d_mod = jnp.concatenate([d_sh1, d_sc1, d_ga1, d_sh2, d_sc2, d_ga2], axis=1)
    d_mod_c = jnp.concatenate([d_sh1c, d_sc1c, zeros], axis=1)
    (d_mod_all,) = _exchange([(jnp.concatenate([d_mod, d_mod_c], axis=0), "gather")], "gather_dmod")
    d_mod_rows = jnp.transpose(d_mod_all, (1, 0, 2)).reshape(2 * N_DEV, n_mod * d)
    d_mod_mine = lax.dynamic_slice_in_dim(d_mod_rows, me * mod_cols, mod_cols, axis=1)
    (d_cs,), (g_w_ada,) = _split(_stage_bwd(
        _fn_mod, [cs_in], [w_ada_loc], [d_mod_mine], blk=2 * N_DEV, name="mod_bwd", row_grads=[(0, F32)]), 1)

    part.update({'c_ctx': jnp.sum(d_cs[N_DEV:], axis=0), 'b_ada': d_mod + d_mod_c, 'g_mix': g_g_mix_x + g_g_mix_c})
    late_parts, loss_parts = _exchange(
        [(_pack([part[n] for n in LATE_REPLICATED]), "gather"), (jnp.broadcast_to(loss_part, (8, LANE)), "gather")],
        "exchange_grads")

    summed = {}
    small_parts = _exchange_end(sent_small, late_parts, "grads_small_end")
    early_parts = small_parts[0]
    for n, parts in zip(UNPACKED_REPLICATED, small_parts[1:]):
        summed[n], own_block[n] = parts, own_small[n]
    (summed['w_down'],) = _exchange_end(sent_down, late_parts, "grads_down_end")
    summed['w_up'], summed['conv_w'] = _exchange_end(sent_up, late_parts, "grads_up_end")
    summed['w_out'], summed['s5_w_glu'], summed['w_proj_a'], summed['w_proj_b'] = _exchange_end(
        sent_mix, late_parts, "grads_mixer_end")
    (summed['w_in'],) = _exchange_end(sent_in, late_parts, "grads_in_end")

    out = {}
    me_arr = me.reshape(1, 1).astype(jnp.int32)
    for names, parts, own, tag in ((early, early_parts, early_part, "early"),
                                   (LATE_REPLICATED, late_parts, None, "late")):
        res = _adamw(_pack([wts[n] for n in names]), parts, _pack([mom1[n] for n in names]),
                     _pack([mom2[n] for n in names]), "adamw_replicated_" + tag, own=own,
                     me=None if own is None else me_arr)
        res = [_unpack(r, [wts[n].shape for n in names]) for r in res]
        for i, n in enumerate(names):
            out[n] = tuple(r[i] for r in res)
    for n, parts in summed.items():
        shape = wts[n].shape
        res = _adamw(two_d(wts[n]), parts, two_d(mom1[n]), two_d(mom2[n]), "adamw_" + n, own=own_block[n],
                     me=me_arr)
        out[n] = tuple(r.reshape(shape) for r in res)
    res = _adamw(w_ada_loc, g_w_ada[None], m_w_ada[0], v_w_ada[0], "adamw_w_ada")
    out['w_ada'] = tuple(r.reshape(w_ada.shape) for r in res)

    loss = jnp.sum(loss_parts[:, 0, 0])
    return (loss, grad_x[None], *[out[n][0] for n in WEIGHTS], *[out[n][1] for n in WEIGHTS],
            *[out[n][2] for n in WEIGHTS], *[out[n][3] for n in WEIGHTS])


def _split(res, n_rows):
    return tuple(res[:n_rows]), tuple(res[n_rows:])
```

```python
import functools
import math

import jax
import jax.numpy as jnp
from jax import lax
from jax.experimental import pallas as pl
from jax.experimental.pallas import tpu as pltpu

F32 = jnp.float32
BF16 = jnp.bfloat16
HI = lax.Precision.HIGHEST

N_DEV = 8
GRID_W = 64
CHUNK = 128
EPS = 1e-6
S5_T = 32
S5_H = 16
LANE = 128
HALO = 128
VMEM_LIMIT = 56 * 1024 * 1024
PACK_ROWS = 256

ADAM_LR = 0.001
ADAM_B1 = 0.9
ADAM_B2 = 0.999
ADAM_EPS = 1e-08
ADAM_WD = 0.01
ADAM_STEP = 10

WEIGHTS = ['c_ctx', 'w_ada', 'b_ada', 'g_mix', 'w_in', 's5_a_re', 's5_a_im', 's5_log_step', 's5_b_re', 's5_b_im',
           's5_c_re', 's5_c_im', 's5_d', 's5_w_glu', 's5_b_glu', 'sgu_ln_g', 'sgu_ln_b', 'sgu_w', 'sgu_b',
           'w_proj_a', 'w_proj_b', 'b_gate', 'w_out', 'g_ffn', 'w_up', 'conv_w', 'conv_b', 'w_down', 'g_final']
COL_SHARDED = ('w_ada', 'w_in', 'w_proj_a', 'w_proj_b', 'w_up', 'conv_w')
ROW_SHARDED = ('s5_w_glu', 'w_out', 'w_down')
SHARDED = COL_SHARDED + ROW_SHARDED
REPLICATED = [n for n in WEIGHTS if n not in SHARDED]
LATE_REPLICATED = ['c_ctx', 'b_ada', 'g_mix']
UNPACKED_REPLICATED = ['sgu_w', 's5_c_re', 's5_c_im']


def _call(body, **kw):
    return pl.pallas_call(body, **kw)


def _params(n_grid):
    return pltpu.CompilerParams(dimension_semantics=("arbitrary",) * n_grid, vmem_limit_bytes=VMEM_LIMIT)


def _pick(dim, pref, unit=LANE):
    best = None
    d = unit
    while d <= min(dim, pref):
        if dim % d == 0:
            best = d
        d += unit
    return best if best is not None else dim


def _dg(a, b, ca, cb, prec=None):
    return lax.dot_general(a, b, (((ca,), (cb,)), ((), ())), precision=prec, preferred_element_type=F32)


def _b16(v):
    return v.astype(BF16)


@jax.custom_vjp
def mmb(a, b):
    return _dg(_b16(a), _b16(b), 1, 0)


def _mmb_fwd(a, b):
    return mmb(a, b), (a, b)


def _mmb_bwd(res, g):
    a, b = res
    g = _b16(g)
    return _dg(g, _b16(b), 1, 1).astype(a.dtype), _dg(_b16(a), g, 0, 0).astype(b.dtype)


mmb.defvjp(_mmb_fwd, _mmb_bwd)


@jax.custom_vjp
def mmb_nt(a, b):
    return _dg(_b16(a), _b16(b), 1, 1)


def _mmb_nt_fwd(a, b):
    return mmb_nt(a, b), (a, b)


def _mmb_nt_bwd(res, g):
    a, b = res
    g = _b16(g)
    return _dg(g, _b16(b), 1, 0).astype(a.dtype), _dg(g, _b16(a), 0, 0).astype(b.dtype)


mmb_nt.defvjp(_mmb_nt_fwd, _mmb_nt_bwd)


@jax.custom_vjp
def mmf(a, b):
    return _dg(a, b, 1, 0, HI)


def _mmf_fwd(a, b):
    return mmf(a, b), (a, b)


def _mmf_bwd(res, g):
    a, b = res
    return _dg(g, b, 1, 1, HI), _dg(a, g, 0, 0, HI)


mmf.defvjp(_mmf_fwd, _mmf_bwd)


def _dg3(a, b, ca, cb):
    ah, bh = _b16(a), _b16(b)
    al, bl = _b16(a - ah.astype(F32)), _b16(b - bh.astype(F32))
    return _dg(ah, bh, ca, cb) + _dg(ah, bl, ca, cb) + _dg(al, bh, ca, cb)


@jax.custom_vjp
def mm3_nt(a, b):
    return _dg3(a, b, 1, 1)


def _mm3_nt_fwd(a, b):
    return mm3_nt(a, b), (a, b)


def _mm3_nt_bwd(res, g):
    a, b = res
    return _dg3(g, b, 1, 0), _dg3(g, a, 0, 0)


mm3_nt.defvjp(_mm3_nt_fwd, _mm3_nt_bwd)


def _shift_impl(x, k, up):
    n = x.shape[0]
    idx = lax.broadcasted_iota(jnp.int32, (n, 1), 0)
    if up:
        return jnp.where(idx < n - k, pltpu.roll(x, n - k, 0), 0.0)
    return jnp.where(idx >= k, pltpu.roll(x, k, 0), 0.0)


@functools.partial(jax.custom_vjp, nondiff_argnums=(1, 2))
def _shift(x, k, up):
    return _shift_impl(x, k, up)


def _shift_fwd(x, k, up):
    return _shift_impl(x, k, up), None


def _shift_bwd(k, up, _, g):
    return (_shift_impl(g, k, not up),)


_shift.defvjp(_shift_fwd, _shift_bwd)


def _mm(a, b, *, name, ta=False, tb=False, add=None, out_dtype=F32, tm_pref=2048, tn_pref=1408, tk_pref=1408):
    m, k = (a.shape[1], a.shape[0]) if ta else a.shape
    n = b.shape[0] if tb else b.shape[1]
    if add is not None and out_dtype != F32:
        tm_pref = min(tm_pref, 1024)
    if ta:
        tk_pref = max(tk_pref, 2048)
    tm, tn, tk = _pick(m, tm_pref), _pick(n, tn_pref), _pick(k, tk_pref)
    nk = k // tk
    a_spec = (pl.BlockSpec((tk, tm), lambda i, j, kk: (kk, i)) if ta
              else pl.BlockSpec((tm, tk), lambda i, j, kk: (i, kk)))
    b_spec = (pl.BlockSpec((tn, tk), lambda i, j, kk: (j, kk)) if tb
              else pl.BlockSpec((tk, tn), lambda i, j, kk: (kk, j)))
    o_spec = pl.BlockSpec((tm, tn), lambda i, j, kk: (i, j))
    ca, cb = (0 if ta else 1), (1 if tb else 0)
    has_add = add is not None
    in_place = out_dtype == F32 or nk == 1

    def body(*refs):
        a_ref, b_ref = refs[0], refs[1]
        o_ref = refs[3] if has_add else refs[2]
        prod = _dg(_b16(a_ref[...]), _b16(b_ref[...]), ca, cb)
        if nk == 1:
            if has_add:
                prod = prod + refs[2][...].astype(F32)
            o_ref[...] = prod.astype(o_ref.dtype)
            return
        acc_ref = o_ref if in_place else refs[-1]
        kk = pl.program_id(2)

        @pl.when(kk == 0)
        def _():
            acc_ref[...] = prod

        @pl.when(kk > 0)
        def _():
            acc_ref[...] += prod

        if has_add or not in_place:
            @pl.when(kk == nk - 1)
            def _():
                r = acc_ref[...]
                if has_add:
                    r = r + refs[2][...].astype(F32)
                o_ref[...] = r.astype(o_ref.dtype)

    ins = [a, b] + ([add] if has_add else [])
    in_specs = [a_spec, b_spec] + ([o_spec] if has_add else [])
    return _call(body, name=name, grid=(m // tm, n // tn, nk), in_specs=in_specs, out_specs=o_spec,
                 out_shape=jax.ShapeDtypeStruct((m, n), out_dtype),
                 scratch_shapes=[] if in_place else [pltpu.VMEM((tm, tn), F32)],
                 compiler_params=_params(3))(*ins)


def _row_spec(blk, width):
    return pl.BlockSpec((blk, width), lambda i: (i, 0))


def _whole_spec(shape):
    return pl.BlockSpec(shape, lambda i: (0,) * len(shape))


def _stage_fwd(fn, rows, params, outs, *, blk, name, n_rows=None):
    n = n_rows or rows[0].shape[0]
    nr, npar = len(rows), len(params)

    def body(*refs):
        vals = [r[...] for r in refs[:nr + npar]]
        res = fn(*vals)
        for o_ref, v in zip(refs[nr + npar:], res):
            o_ref[...] = v.astype(o_ref.dtype)

    return _call(body, name=name, grid=(n // blk,),
                 in_specs=[_row_spec(blk, r.shape[1]) for r in rows] + [_whole_spec(p.shape) for p in params],
                 out_specs=[_row_spec(blk, w) for w, _ in outs],
                 out_shape=[jax.ShapeDtypeStruct((n, w), dt) for w, dt in outs],
                 compiler_params=_params(1))(*rows, *params)


def _stage_bwd(fn, rows, params, cts, *, blk, name, row_grads, n_rows=None):
    n = n_rows or rows[0].shape[0]
    nr, npar, nct = len(rows), len(params), len(cts)

    def body(*refs):
        vals = [r[...].astype(F32) for r in refs[:nr + npar]]
        ct = [r[...] for r in refs[nr + npar:nr + npar + nct]]
        d_rows = refs[nr + npar + nct:nr + npar + nct + len(row_grads)]
        d_par = refs[nr + npar + nct + len(row_grads):]
        res, vjp = jax.vjp(fn, *vals)
        g = vjp(tuple(c.astype(r.dtype) for c, r in zip(ct, res)))
        for o_ref, (j, _) in zip(d_rows, row_grads):
            o_ref[...] = g[j].astype(o_ref.dtype)

        @pl.when(pl.program_id(0) == 0)
        def _():
            for o_ref in d_par:
                o_ref[...] = jnp.zeros_like(o_ref)

        for j, o_ref in enumerate(d_par):
            o_ref[...] += g[nr + j].astype(F32)

    return _call(body, name=name, grid=(n // blk,),
                 in_specs=([_row_spec(blk, r.shape[1]) for r in rows] + [_whole_spec(p.shape) for p in params]
                           + [_row_spec(blk, c.shape[1]) for c in cts]),
                 out_specs=([_row_spec(blk, rows[j].shape[1]) for j, _ in row_grads]
                            + [_whole_spec(p.shape) for p in params]),
                 out_shape=([jax.ShapeDtypeStruct((n, rows[j].shape[1]), dt) for j, dt in row_grads]
                            + [jax.ShapeDtypeStruct(p.shape, F32) for p in params]),
                 compiler_params=_params(1))(*rows, *params, *cts)


def _stage_loss(fn, rows, params, *, blk, name, row_grads):
    n = rows[0].shape[0]
    nr, npar = len(rows), len(params)

    def body(*refs):
        vals = [r[...].astype(F32) for r in refs[:nr + npar]]
        loss_ref = refs[nr + npar]
        d_rows = refs[nr + npar + 1:nr + npar + 1 + len(row_grads)]
        d_par = refs[nr + npar + 1 + len(row_grads):]
        res, vjp = jax.vjp(fn, *vals)
        g = vjp(jnp.ones_like(res))
        for o_ref, (j, _) in zip(d_rows, row_grads):
            o_ref[...] = g[j].astype(o_ref.dtype)

        @pl.when(pl.program_id(0) == 0)
        def _():
            loss_ref[...] = jnp.zeros_like(loss_ref)
            for o_ref in d_par:
                o_ref[...] = jnp.zeros_like(o_ref)

        loss_ref[...] += res
        for j, o_ref in enumerate(d_par):
            o_ref[...] += g[nr + j].astype(F32)

    return _call(body, name=name, grid=(n // blk,),
                 in_specs=[_row_spec(blk, r.shape[1]) for r in rows] + [_whole_spec(p.shape) for p in params],
                 out_specs=([_whole_spec((1, 1))] + [_row_spec(blk, rows[j].shape[1]) for j, _ in row_grads]
                            + [_whole_spec(p.shape) for p in params]),
                 out_shape=([jax.ShapeDtypeStruct((1, 1), F32)]
                            + [jax.ShapeDtypeStruct((n, rows[j].shape[1]), dt) for j, dt in row_grads]
                            + [jax.ShapeDtypeStruct(p.shape, F32) for p in params]),
                 compiler_params=_params(1))(*rows, *params)


def _rms(x, g):
    return x * lax.rsqrt(jnp.mean(x * x, axis=-1, keepdims=True) + EPS) * g


def _modulate(x, g, shift, scale):
    return _rms(x, g) * (1.0 + scale) + shift


def _fn_mod(cs, w_ada):
    return (mmb(jax.nn.silu(cs), w_ada),)


def _fn_a(x, g_mix, sh, sc):
    return (_b16(_modulate(x, g_mix, sh, sc)),)


def _fn_a_res(x, g_mix, sh, sc):
    return _b16(_modulate(x, g_mix, sh, sc)), x


def _sgu_spatial(v, sgu_w, sgu_bt):
    rows, width = v.shape
    gdim = width // (sgu_w.shape[0] // CHUNK)
    groups = width // gdim
    expand = (lax.broadcasted_iota(jnp.int32, (groups, width), 1) // gdim
              == lax.broadcasted_iota(jnp.int32, (groups, width), 0)).astype(F32)
    bias = mmf(sgu_bt, expand)
    lane = lax.broadcasted_iota(jnp.int32, (CHUNK, LANE), 1)
    per_lane_block = LANE // gdim
    chunks = []
    for ci in range(rows // CHUNK):
        vc = v[ci * CHUNK:(ci + 1) * CHUNK]
        blocks = []
        for lb in range(width // LANE):
            vb = vc[:, lb * LANE:(lb + 1) * LANE]
            acc = None
            for s in range(per_lane_block):
                g = lb * per_lane_block + s
                r = mmb(sgu_w[g * CHUNK:(g + 1) * CHUNK], vb)
                sel = (lane // gdim) == s
                acc = jnp.where(sel, r, 0.0) if acc is None else jnp.where(sel, r, acc)
            blocks.append(acc)
        chunks.append(jnp.concatenate(blocks, axis=1) + bias)
    return jnp.concatenate(chunks, axis=0)


def _fn_b(pu, prest, ysc, s5_d, w_glu, b_glu, ln_g, ln_b, sgu_w, sgu_bt, w_pa, w_pb, b_gate):
    sw = ln_g.shape[1]
    y = jax.nn.gelu(pu * s5_d + ysc)
    ya = y * jax.nn.sigmoid(mmb(y, w_glu) + b_glu)
    z = jax.nn.gelu(prest[:, :2 * sw])
    u, v = z[:, :sw], z[:, sw:]
    vc = v - jnp.mean(v, axis=-1, keepdims=True)
    v = vc * lax.rsqrt(jnp.mean(vc * vc, axis=-1, keepdims=True) + EPS) * ln_g + ln_b
    yb = u * _sgu_spatial(v, sgu_w, sgu_bt)
    gates = jax.nn.sigmoid(prest[:, 2 * sw:] + b_gate)
    d = gates.shape[1] // 2
    return (_b16(gates[:, :d] * mmb(ya, w_pa) + gates[:, d:] * mmb(yb, w_pb)),)


def _fn_c(x, mo, ga1, g_ffn, sh2, sc2):
    x1 = x + ga1 * mo
    return x1, _b16(_modulate(x1, g_ffn, sh2, sc2))


def _fn_e(x1, dn, tgt, ga2, g_final):
    y = _rms(x1 + ga2 * dn, g_final)
    err = (y - tgt) ** 2
    return 0.5 * jnp.sum(jnp.mean(err, axis=-1, keepdims=True), axis=0, keepdims=True)


def _swap_impl(x):
    return pltpu.roll(x, x.shape[1] // 2, 1)


@jax.custom_vjp
def _swap_halves(x):
    return _swap_impl(x)


_swap_halves.defvjp(lambda x: (_swap_impl(x), None), lambda _, g: (_swap_impl(g),))


def _s5_direction(u, mask, a_re, a_im, log_step, bt, c, rev):
    nc, width = u.shape
    t_len = width // S5_H
    n2 = a_re.shape[1]
    lane = lax.broadcasted_iota(jnp.int32, (1, n2), 1)
    sign = jnp.where(lane < n2 // 2, -1.0, 1.0)
    dt = jnp.exp(log_step)
    lr, li = a_re * dt, a_im * dt
    mag = jnp.exp(lr)
    ab_re, ab_im = mag * jnp.cos(li), mag * jnp.sin(li)
    p, q = ab_re - 1.0, ab_im
    den = a_re * a_re + a_im * a_im
    k_re, k_im = (p * a_re + q * a_im) / den, (q * a_re - p * a_im) / den
    bb = k_re * bt + (k_im * sign) * _swap_halves(bt)

    def power(e):
        m = jnp.exp(lr * e)
        return m * jnp.cos(li * e), m * jnp.sin(li * e)

    order = range(t_len - 1, -1, -1) if rev else range(t_len)
    e1 = jnp.concatenate([jnp.full((1, 1, n2), float(t_len - 1 - pos), F32) for pos in order], axis=0)
    lr3, li3, sign3 = lr.reshape(1, 1, n2), li.reshape(1, 1, n2), sign.reshape(1, 1, n2)
    m1, c1, s1 = jnp.exp(lr3 * e1), jnp.cos(li3 * e1), jnp.sin(li3 * e1)
    m2 = jnp.exp(-(lr3 * e1))
    x1r, x1i = m1 * c1, m1 * s1
    x2r, x2i = m2 * c1, -(m2 * s1)
    at_r, at_i = [v.reshape(1, 1, n2) for v in power(float(t_len))]
    x3r, x3i = at_r * x2r - at_i * x2i, at_r * x2i + at_i * x2r

    def rows(xr, xi, z):
        z3, zs3 = z.reshape(1, S5_H, n2), _swap_halves(z).reshape(1, S5_H, n2)
        return (xr * z3 + (xi * sign3) * zs3).reshape(width, n2)

    p_in = rows(x1r, x1i, bb)
    r_out = rows(x2r, x2i, c)
    w_out = rows(x3r, x3i, c)
    toep = mm3_nt(p_in, r_out * (-sign)) * mask

    state = mmb(u, p_in)
    k = 1
    while k < nc:
        ar, ai = power(float(t_len * k))
        moved = _shift(state, k, rev)
        state = state + ar * moved + (ai * sign) * _swap_halves(moved)
        k *= 2
    entering = _shift(state, 1, rev)
    return mmb(u, toep) + mmb_nt(entering, w_out * (-sign))


def _fn_s5(masks, x_chunks, ctx_chunks, *prm):
    nx, nctx = x_chunks.shape[0], ctx_chunks.shape[0]
    pad = (-(nx + 2 * nctx)) % LANE
    u = jnp.concatenate([ctx_chunks, x_chunks, ctx_chunks, jnp.zeros((pad, x_chunks.shape[1]), F32)], axis=0)
    out = None
    for d in range(2):
        y = _s5_direction(u, masks[d], *[p[d] for p in prm], rev=(d == 1))
        out = y if out is None else out + y
    return out[nctx:nctx + nx]


def _s5_specs(prm):
    return [pl.BlockSpec((2, 1) + p.shape[2:], lambda g: (0, g, 0, 0)) for p in prm]


def _group_spec(a):
    return pl.BlockSpec((1,) + a.shape[1:], lambda i: (i, 0, 0))


def _s5_masks(width):
    pos = jnp.arange(width) // S5_H
    causal = (pos[None, :] >= pos[:, None]).astype(F32)
    return jnp.stack([causal, causal.T])


def _s5_fwd(x_g, ctx_g, prm):
    masks = _s5_masks(x_g.shape[2])

    def body(*refs):
        pv = [r[:, 0] for r in refs[3:3 + len(prm)]]
        refs[-1][0] = _fn_s5(refs[0][...], refs[1][0].astype(F32), refs[2][0], *pv)

    return _call(body, name="s5_fwd", grid=(x_g.shape[0],),
                 in_specs=[_whole_spec(masks.shape), _group_spec(x_g), _group_spec(ctx_g)] + _s5_specs(prm),
                 out_specs=_group_spec(x_g), out_shape=jax.ShapeDtypeStruct(x_g.shape, F32),
                 compiler_params=_params(1))(masks, x_g, ctx_g, *prm)


def _s5_bwd(x_g, ctx_g, prm, dy_g, skip_g):
    npar = len(prm)
    masks = _s5_masks(x_g.shape[2])

    def body(*refs):
        pv = [r[:, 0] for r in refs[3:3 + npar]]
        dy = refs[3 + npar][0]
        _, vjp = jax.vjp(functools.partial(_fn_s5, refs[0][...]), refs[1][0].astype(F32), refs[2][0], *pv)
        grads = vjp(dy)
        outs = refs[5 + npar:]
        outs[0][0] = (grads[0] + dy * refs[4 + npar][0]).astype(outs[0].dtype)
        outs[1][0] = grads[1]
        for o_ref, gv in zip(outs[2:], grads[2:]):
            o_ref[:, 0] = gv

    return _call(body, name="s5_bwd", grid=(x_g.shape[0],),
                 in_specs=([_whole_spec(masks.shape), _group_spec(x_g), _group_spec(ctx_g)] + _s5_specs(prm)
                           + [_group_spec(dy_g), _group_spec(skip_g)]),
                 out_specs=[_group_spec(x_g), _group_spec(ctx_g)] + _s5_specs(prm),
                 out_shape=[jax.ShapeDtypeStruct(x_g.shape, BF16), jax.ShapeDtypeStruct(ctx_g.shape, F32)]
                 + [jax.ShapeDtypeStruct(p.shape, F32) for p in prm],
                 compiler_params=_params(1))(masks, x_g, ctx_g, *prm, dy_g, skip_g)


def _to_groups(tok):
    n, width = tok.shape
    g = width // S5_H
    return jnp.transpose(tok.reshape(n, g, S5_H), (1, 0, 2)).reshape(g, n // S5_T, S5_T * S5_H)


def _from_groups(grp):
    g, nc, _ = grp.shape
    return jnp.transpose(grp.reshape(g, nc * S5_T, S5_H), (1, 0, 2)).reshape(nc * S5_T, g * S5_H)


def _conv_taps(xp, xm, xn, blk_i, n_blk):
    tb = xm.shape[0]
    xp = jnp.where(blk_i == 0, 0.0, xp.astype(F32))
    xn = jnp.where(blk_i == n_blk - 1, 0.0, xn.astype(F32))
    buf = jnp.concatenate([xp, xm.astype(F32), xn], axis=0)
    n = tb + 2 * HALO
    col = lax.broadcasted_iota(jnp.int32, (n, 1), 0) % GRID_W
    left = jnp.where(col >= 1, pltpu.roll(buf, 1, 0), 0.0)
    right = jnp.where(col <= GRID_W - 2, pltpu.roll(buf, n - 1, 0), 0.0)
    shifted = (left, buf, right)
    taps = []
    for di in range(3):
        start = HALO + (di - 1) * GRID_W
        for dj in range(3):
            taps.append(shifted[dj][start:start + tb])
    return taps


def _conv_sum(taps, w_ref, flip=False):
    acc = None
    for k, tap in enumerate(taps):
        j = len(taps) - 1 - k if flip else k
        term = tap * w_ref[j:j + 1, :]
        acc = term if acc is None else acc + term
    return acc


def _conv_geometry(n_tok, width, tb_pref=1024):
    tb = _pick(n_tok, tb_pref, HALO)
    cb = _pick(width, 256)
    nb = tb // HALO
    last = n_tok // HALO - 1
    main = pl.BlockSpec((tb, cb), lambda j, i: (i, j))
    prev = pl.BlockSpec((HALO, cb), lambda j, i: (jnp.maximum(i * nb - 1, 0), j))
    nxt = pl.BlockSpec((HALO, cb), lambda j, i: (jnp.minimum(i * nb + nb, last), j))
    par = lambda r: pl.BlockSpec((r, cb), lambda j, i: (0, j))
    return tb, cb, main, prev, nxt, par


def _conv_act_fwd(up_g, up_v, w_g, w_v, b_g, b_v):
    n_tok, width = up_g.shape
    tb, cb, main, prev, nxt, par = _conv_geometry(n_tok, width, 2048)
    n_blk = n_tok // tb

    def body(gp, gm, gn, vp, vm, vn, wg, wv, bg, bv, o_ref, gate_ref, val_ref):
        i = pl.program_id(1)
        gate = _conv_sum(_conv_taps(gp[...], gm[...], gn[...], i, n_blk), wg) + bg[...]
        val = _conv_sum(_conv_taps(vp[...], vm[...], vn[...], i, n_blk), wv) + bv[...]
        o_ref[...] = (jax.nn.silu(gate) * val).astype(o_ref.dtype)
        gate_ref[...] = gate
        val_ref[...] = val

    shp = jax.ShapeDtypeStruct
    return _call(body, name="conv_act_fwd", grid=(width // cb, n_tok // tb),
                 in_specs=[prev, main, nxt, prev, main, nxt, par(9), par(9), par(1), par(1)],
                 out_specs=[main, main, main],
                 out_shape=[shp((n_tok, width), BF16), shp((n_tok, width), F32), shp((n_tok, width), F32)],
                 compiler_params=_params(2))(up_g, up_g, up_g, up_v, up_v, up_v, w_g, w_v, b_g, b_v)


def _conv_act_bwd(up_g, up_v, gate_c, val_c, d_act):
    n_tok, width = up_g.shape
    tb, cb, main, prev, nxt, par = _conv_geometry(n_tok, width)
    n_blk = n_tok // tb

    def body(gp, gm, gn, vp, vm, vn, gc, vc, da, dcg, dcv, dwg, dwv, dbg, dbv):
        i = pl.program_id(1)
        taps_g = _conv_taps(gp[...], gm[...], gn[...], i, n_blk)
        taps_v = _conv_taps(vp[...], vm[...], vn[...], i, n_blk)
        gate, val = gc[...], vc[...]
        sig = jax.nn.sigmoid(gate)
        d = da[...].astype(F32)
        d_gate = d * val * sig * (1.0 + gate * (1.0 - sig))
        d_val = d * gate * sig
        dcg[...] = d_gate.astype(dcg.dtype)
        dcv[...] = d_val.astype(dcv.dtype)

        @pl.when(i == 0)
        def _():
            for r in (dwg, dwv, dbg, dbv):
                r[...] = jnp.zeros_like(r)

        dbg[...] += jnp.sum(d_gate, axis=0, keepdims=True)
        dbv[...] += jnp.sum(d_val, axis=0, keepdims=True)
        for k in range(9):
            dwg[k:k + 1, :] += jnp.sum(taps_g[k] * d_gate, axis=0, keepdims=True)
            dwv[k:k + 1, :] += jnp.sum(taps_v[k] * d_val, axis=0, keepdims=True)

    shp = jax.ShapeDtypeStruct
    return _call(body, name="conv_act_bwd", grid=(width // cb, n_tok // tb),
                 in_specs=[prev, main, nxt, prev, main, nxt, main, main, main],
                 out_specs=[main, main, par(9), par(9), par(1), par(1)],
                 out_shape=[shp((n_tok, width), BF16), shp((n_tok, width), BF16), shp((9, width), F32),
                            shp((9, width), F32), shp((1, width), F32), shp((1, width), F32)],
                 compiler_params=_params(2))(up_g, up_g, up_g, up_v, up_v, up_v, gate_c, val_c, d_act)


def _conv_transposed(x, w, name):
    n_tok, width = x.shape
    tb, cb, main, prev, nxt, par = _conv_geometry(n_tok, width, 2048)
    n_blk = n_tok // tb

    def body(xp, xm, xn, w_ref, o_ref):
        taps = _conv_taps(xp[...], xm[...], xn[...], pl.program_id(1), n_blk)
        o_ref[...] = _conv_sum(taps, w_ref, flip=True).astype(o_ref.dtype)

    return _call(body, name=name, grid=(width // cb, n_tok // tb), in_specs=[prev, main, nxt, par(9)],
                 out_specs=main, out_shape=jax.ShapeDtypeStruct((n_tok, width), BF16),
                 compiler_params=_params(2))(x, x, x, w)


def _adamw(w, g_parts, m, v, name, own=None, me=None):
    rows, cols = w.shape
    parts = g_parts.shape[0]
    blk = _pick(rows, 256, 8)
    spec = pl.BlockSpec((blk, cols), lambda i: (i, 0))
    has_own = own is not None

    def body(*refs):
        w_ref, g_ref, m_ref, v_ref = refs[:4]
        g_out, d_out, m_out, v_out = refs[-4:]

        def part(p):
            if has_own:
                return jnp.where(refs[5][...] == p, refs[4][...], g_ref[p]).astype(F32)
            return g_ref[p].astype(F32)

        g = part(0)
        for p in range(1, parts):
            g = g + part(p)
        m_new = ADAM_B1 * m_ref[...] + (1.0 - ADAM_B1) * g
        v_new = ADAM_B2 * v_ref[...] + (1.0 - ADAM_B2) * (g * g)
        m_hat = m_new / (1.0 - ADAM_B1 ** ADAM_STEP)
        v_hat = v_new / (1.0 - ADAM_B2 ** ADAM_STEP)
        g_out[...] = g
        d_out[...] = -ADAM_LR * (m_hat / (jnp.sqrt(v_hat) + ADAM_EPS) + ADAM_WD * w_ref[...])
        m_out[...] = m_new
        v_out[...] = v_new

    extra = [own, me] if has_own else []
    return _call(body, name=name, grid=(rows // blk,),
                 in_specs=([spec, pl.BlockSpec((parts, blk, cols), lambda i: (0, i, 0)), spec, spec]
                           + ([spec, _whole_spec((1, 1))] if has_own else [])),
                 out_specs=[spec] * 4, out_shape=[jax.ShapeDtypeStruct((rows, cols), F32)] * 4,
                 compiler_params=_params(1))(w, g_parts, m, v, *extra)


def _exchange(items, name):
    n = len(items)
    hbm = pl.BlockSpec(memory_space=pl.ANY)

    def body(*refs):
        srcs, outs = refs[:n], refs[n:2 * n]
        send_sems, recv_sems, own_sems = refs[2 * n:]
        x, y, c = lax.axis_index("x"), lax.axis_index("y"), lax.axis_index("c")
        me = 4 * x + 2 * y + c
        own = []
        for i, (_, mode) in enumerate(items):
            src = srcs[i] if mode == "gather" else srcs[i].at[me]
            cp = pltpu.make_async_copy(src, outs[i].at[me], own_sems.at[i])
            cp.start()
            own.append(cp)
        sent = []
        for i, (_, mode) in enumerate(items):
            for k in range(1, N_DEV):
                px = 1 - x if k & 4 else x
                py = 1 - y if k & 2 else y
                pc = 1 - c if k & 1 else c
                peer = 4 * px + 2 * py + pc
                src = srcs[i] if mode == "gather" else srcs[i].at[peer]
                cp = pltpu.make_async_remote_copy(
                    src_ref=src, dst_ref=outs[i].at[me], send_sem=send_sems.at[i, k - 1],
                    recv_sem=recv_sems.at[i, k - 1], device_id=(px, py, pc), device_id_type=pl.DeviceIdType.MESH)
                cp.start()
                landing = pltpu.make_async_remote_copy(
                    src_ref=src, dst_ref=outs[i].at[peer], send_sem=send_sems.at[i, k - 1],
                    recv_sem=recv_sems.at[i, k - 1], device_id=(px, py, pc), device_id_type=pl.DeviceIdType.MESH)
                sent.append((cp, landing))
        for cp in own:
            cp.wait()
        for cp, landing in sent:
            cp.wait_send()
            landing.wait_recv()

    out_shape = [jax.ShapeDtypeStruct((N_DEV,) + (a.shape if mode == "gather" else a.shape[1:]), a.dtype)
                 for a, mode in items]
    return _call(body, name=name, in_specs=[hbm] * n, out_specs=[hbm] * n, out_shape=out_shape,
                 scratch_shapes=[pltpu.SemaphoreType.DMA((n, N_DEV - 1)), pltpu.SemaphoreType.DMA((n, N_DEV - 1)),
                                 pltpu.SemaphoreType.DMA((n,))])(*[a for a, _ in items])


def _peer(k, x, y, c):
    px = 1 - x if k & 4 else x
    py = 1 - y if k & 2 else y
    pc = 1 - c if k & 1 else c
    return (px, py, pc), 4 * px + 2 * py + pc


_HBM_SPEC = pl.BlockSpec(memory_space=pltpu.HBM)
_SEM_SPEC = pl.BlockSpec(memory_space=pltpu.SEMAPHORE)
_SPLIT_EFFECT = pltpu.SideEffectType.DATAFLOW_SIDE_EFFECTING


def _exchange_begin(items, name, after=None):
    n = len(items)
    modes = [mode for _, mode in items]
    srcs = [pltpu.with_memory_space_constraint(a, pltpu.HBM) for a, _ in items]
    lands = [pltpu.with_memory_space_constraint(
        lax.empty((N_DEV,) + (a.shape if mode == "gather" else a.shape[1:]), a.dtype), pltpu.HBM)
        for a, mode in items]

    def body(*refs):
        src_refs, land_refs = refs[:n], refs[n:2 * n]
        token = refs[-1]
        send_sems, recv_sems = refs[-2 * n - 3], refs[-2 * n - 2]
        x, y, c = lax.axis_index("x"), lax.axis_index("y"), lax.axis_index("c")
        me = 4 * x + 2 * y + c
        for i in range(n):
            for k in range(1, N_DEV):
                coords, peer = _peer(k, x, y, c)
                src = src_refs[i] if modes[i] == "gather" else src_refs[i].at[peer]
                pltpu.make_async_remote_copy(
                    src_ref=src, dst_ref=land_refs[i].at[me], send_sem=send_sems.at[i * (N_DEV - 1) + k - 1],
                    recv_sem=recv_sems.at[i * (N_DEV - 1) + k - 1], device_id=coords,
                    device_id_type=pl.DeviceIdType.MESH).start()
        token[...] = jnp.zeros_like(token)

    sems = pltpu.SemaphoreType.DMA((n * (N_DEV - 1),))
    order = [] if after is None else [after]
    res = _call(body, name=name,
                out_shape=(sems, sems, *[pltpu.HBM(a.shape, a.dtype) for a in srcs + lands],
                           jax.ShapeDtypeStruct((8, LANE), F32)),
                in_specs=[_HBM_SPEC] * (2 * n) + [pl.BlockSpec(memory_space=pl.ANY)] * len(order),
                out_specs=(_SEM_SPEC, _SEM_SPEC, *[_HBM_SPEC] * (2 * n), pl.BlockSpec(memory_space=pltpu.VMEM)),
                input_output_aliases={i: 2 + i for i in range(2 * n)},
                compiler_params=pltpu.CompilerParams(has_side_effects=_SPLIT_EFFECT))(*srcs, *lands, *order)
    return (modes, res[0], res[1], list(res[2:2 + n]), list(res[2 + n:2 + 2 * n])), res[-1][0, 0]


def _exchange_end(handle, after, name):
    modes, send_sems, recv_sems, srcs, lands = handle
    n = len(modes)

    def wait_body(*refs):
        src_refs, land_refs = refs[:n], refs[n:2 * n]
        send, recv = refs[2 * n], refs[2 * n + 1]
        x, y, c = lax.axis_index("x"), lax.axis_index("y"), lax.axis_index("c")
        for i in range(n):
            for k in range(1, N_DEV):
                coords, peer = _peer(k, x, y, c)
                src = src_refs[i] if modes[i] == "gather" else src_refs[i].at[peer]
                cp = pltpu.make_async_remote_copy(
                    src_ref=src, dst_ref=land_refs[i].at[peer], send_sem=send.at[i * (N_DEV - 1) + k - 1],
                    recv_sem=recv.at[i * (N_DEV - 1) + k - 1], device_id=coords,
                    device_id_type=pl.DeviceIdType.MESH)
                cp.wait_send()
                cp.wait_recv()

    res = _call(wait_body, name=name, out_shape=[pltpu.HBM(a.shape, a.dtype) for a in srcs + lands],
                in_specs=[_HBM_SPEC] * (2 * n) + [_SEM_SPEC, _SEM_SPEC, pl.BlockSpec(memory_space=pl.ANY)],
                out_specs=[_HBM_SPEC] * (2 * n), input_output_aliases={i: i for i in range(2 * n)},
                compiler_params=pltpu.CompilerParams(has_side_effects=_SPLIT_EFFECT))(
                    *srcs, *lands, send_sems, recv_sems, after)
    return res[n:]


def _with_own(land, own, me):
    slot = lax.broadcasted_iota(jnp.int32, (N_DEV,) + (1,) * (land.ndim - 1), 0)
    return jnp.where(slot == me, own[None], land)


def _cols_from_blocks(g):
    return jnp.transpose(g, (1, 0, 2)).reshape(g.shape[1], N_DEV * g.shape[2])


def _blocks_from_cols(w):
    r, c8 = w.shape
    return jnp.transpose(w.reshape(r, N_DEV, c8 // N_DEV), (1, 0, 2))


def _pack(arrs):
    flat = jnp.concatenate([a.reshape(-1).astype(F32) for a in arrs])
    rows = -(-flat.shape[0] // LANE)
    rows = -(-rows // PACK_ROWS) * PACK_ROWS if rows > PACK_ROWS else -(-rows // 8) * 8
    return jnp.pad(flat, (0, rows * LANE - flat.shape[0])).reshape(rows, LANE)


def _unpack(packed, shapes):
    flat = packed.reshape(-1)
    out, off = [], 0
    for s in shapes:
        size = math.prod(s)
        out.append(flat[off:off + size].reshape(s))
        off += size
    return out


def kernel(x, c, ctx, c_ctx, w_ada, b_ada, g_mix, w_in, s5_a_re, s5_a_im, s5_log_step, s5_b_re, s5_b_im, s5_c_re, s5_c_im, s5_d, s5_w_glu, s5_b_glu, sgu_ln_g, sgu_ln_b, sgu_w, sgu_b, w_proj_a, w_proj_b, b_gate, w_out, g_ffn, w_up, conv_w, conv_b, w_down, g_final, loss_target, m_c_ctx, m_w_ada, m_b_ada, m_g_mix, m_w_in, m_s5_a_re, m_s5_a_im, m_s5_log_step, m_s5_b_re, m_s5_b_im, m_s5_c_re, m_s5_c_im, m_s5_d, m_s5_w_glu, m_s5_b_glu, m_sgu_ln_g, m_sgu_ln_b, m_sgu_w, m_sgu_b, m_w_proj_a, m_w_proj_b, m_b_gate, m_w_out, m_g_ffn, m_w_up, m_conv_w, m_conv_b, m_w_down, m_g_final, v_c_ctx, v_w_ada, v_b_ada, v_g_mix, v_w_in, v_s5_a_re, v_s5_a_im, v_s5_log_step, v_s5_b_re, v_s5_b_im, v_s5_c_re, v_s5_c_im, v_s5_d, v_s5_w_glu, v_s5_b_glu, v_sgu_ln_g, v_sgu_ln_b, v_sgu_w, v_sgu_b, v_w_proj_a, v_w_proj_b, v_b_gate, v_w_out, v_g_ffn, v_w_up, v_conv_w, v_conv_b, v_w_down, v_g_final):
    given = dict(locals())
    wts = {n: given[n] for n in WEIGHTS}
    mom1 = {n: given["m_" + n] for n in WEIGHTS}
    mom2 = {n: given["v_" + n] for n in WEIGHTS}

    me = 4 * lax.axis_index("x") + 2 * lax.axis_index("y") + lax.axis_index("c")
    xs, cx, tgt = x[0], ctx[0], loss_target[0]
    n_tok, d = xs.shape
    n_ctx = cx.shape[0]
    s5w = s5_d.shape[1]
    ffn = w_down.shape[1] * N_DEV
    n_mod = w_ada.shape[2] * N_DEV // d
    mod_cols = w_ada.shape[2]

    def two_d(a):
        return a.reshape(-1, a.shape[-1])

    conv_w9 = conv_w[0].reshape(9, -1)
    gathered = _exchange([(c, "gather"), (_b16(w_in[0]), "gather")], "gather_weights")
    c_all = gathered[0].reshape(N_DEV, d)
    w_in_f = _cols_from_blocks(gathered[1])
    w_in_u, w_in_rest = w_in_f[:, :s5w], w_in_f[:, s5w:]

    cs_in = jnp.concatenate([c_all, jnp.broadcast_to(c_ctx[None, :], (N_DEV, d))], axis=0)
    w_ada_loc = w_ada[0]
    (mod_mine,) = _stage_fwd(_fn_mod, [cs_in], [w_ada_loc], [(mod_cols, F32)], blk=2 * N_DEV, name="mod_fwd")
    (mod_blocks,) = _exchange([(mod_mine, "gather")], "gather_mod")
    mid_own = [_b16(s5_w_glu[0]), _b16(w_proj_a[0]), _b16(w_proj_b[0])]
    mid_weights, tok_mid = _exchange_begin([(a, "gather") for a in mid_own], "gather_mid_begin", after=mod_blocks)
    late_own = [_b16(w_out[0]), _b16(w_up[0]), conv_w9, _b16(w_down[0])]
    late_weights, tok = _exchange_begin([(a, "gather") for a in late_own], "gather_late_begin", after=mod_blocks)
    mod_all = _cols_from_blocks(mod_blocks) + b_ada + (tok + tok_mid)
    mod = lax.dynamic_slice_in_dim(mod_all, me, 1, axis=0)
    mod_c = mod_all[N_DEV:N_DEV + 1]
    sh1, sc1, ga1, sh2, sc2, ga2 = [mod[:, i * d:(i + 1) * d] for i in range(n_mod)]
    sh1c, sc1c = mod_c[:, :d], mod_c[:, d:2 * d]

    a_par = [g_mix, sh1, sc1]
    ac_par = [g_mix, sh1c, sc1c]
    (h,) = _stage_fwd(_fn_a, [xs], a_par, [(d, BF16)], blk=_pick(n_tok, 512, 8), name="modulate1_fwd")
    (hc,) = _stage_fwd(_fn_a, [cx], ac_par, [(d, BF16)], blk=_pick(n_ctx, 512, 8), name="modulate1_ctx_fwd")
    pu = _mm(h, w_in_u, name="proj_u")
    prest = _mm(h, w_in_rest, name="proj_rest")
    puc = _mm(hc, w_in_u, name="proj_u_ctx")

    n_state = s5_a_re.shape[-1]

    def twice(a):
        return jnp.concatenate([a, a], axis=-1)

    s5_prm = [twice(s5_a_re[0])[:, :, None, :], twice(s5_a_im[0])[:, :, None, :], s5_log_step[0][:, :, None, None],
              jnp.concatenate([jnp.swapaxes(s5_b_re[0], 2, 3), jnp.swapaxes(s5_b_im[0], 2, 3)], axis=-1),
              jnp.concatenate([s5_c_re[0], s5_c_im[0]], axis=-1)]
    pu_g, puc_g = _to_groups(_b16(pu)), _to_groups(puc)
    ysc = _from_groups(_s5_fwd(pu_g, puc_g, s5_prm))

    mid = [_with_own(land, own, me) for land, own in zip(_exchange_end(mid_weights, ysc, "gather_mid_end"), mid_own)]
    w_glu_f, w_pa_f, w_pb_f = mid[0].reshape(-1, s5w), _cols_from_blocks(mid[1]), _cols_from_blocks(mid[2])
    b_par = [s5_d, w_glu_f, s5_b_glu, sgu_ln_g, sgu_ln_b, two_d(sgu_w[0]), jnp.transpose(sgu_b[0]),
             w_pa_f, w_pb_f, b_gate]
    b_rows = [pu, prest, ysc]
    b_blk = _pick(n_tok, 512, CHUNK)
    (mpre,) = _stage_fwd(_fn_b, b_rows, b_par, [(d, BF16)], blk=b_blk, name="mixers_fwd")
    late = [_with_own(land, own, me) for land, own in zip(_exchange_end(late_weights, mpre, "gather_late_end"),
                                                          late_own)]
    w_out_f = late[0].reshape(-1, d)
    w_up_f = _cols_from_blocks(late[1])
    w_up_g, w_up_v = w_up_f[:, :ffn], w_up_f[:, ffn:]
    conv_w_f = _cols_from_blocks(late[2])
    w_down_f = late[3].reshape(-1, d)
    mo = _mm(mpre, w_out_f, name="out_proj")

    c_par = [ga1, g_ffn, sh2, sc2]
    c_blk = _pick(n_tok, 512, 8)
    x1, h2 = _stage_fwd(_fn_c, [xs, mo], c_par, [(d, F32), (d, BF16)], blk=c_blk, name="modulate2_fwd")
    up_g = _mm(h2, w_up_g, out_dtype=BF16, name="up_gate")
    up_v = _mm(h2, w_up_v, out_dtype=BF16, name="up_val")
    cw_g, cw_v = conv_w_f[:, :ffn], conv_w_f[:, ffn:]
    cb_g, cb_v = conv_b[:, :ffn], conv_b[:, ffn:]
    act, gate_c, val_c = _conv_act_fwd(up_g, up_v, cw_g, cw_v, cb_g, cb_v)
    dn = _mm(act, w_down_f, name="down_proj")

    loss_part, d_x1a, d_dn, d_ga2, d_g_final = _stage_loss(
        _fn_e, [x1, dn, tgt], [ga2, g_final[None, :]], blk=c_blk, name="loss_head",
        row_grads=[(0, F32), (1, BF16)])

    d_act = _mm(d_dn, w_down_f, tb=True, name="down_proj_dx")
    g_w_down = _mm(act, d_dn, ta=True, out_dtype=BF16, name="down_proj_dw")

    own_block = {}

    def grad_blocks(name, g):
        blocks = _b16(_blocks_from_cols(g) if name in COL_SHARDED
                      else g.reshape(N_DEV, g.shape[0] // N_DEV, g.shape[1]))
        own_block[name] = lax.dynamic_index_in_dim(blocks, me, 0, keepdims=False)
        return blocks, "a2a"

    sent_down, tok = _exchange_begin([grad_blocks('w_down', g_w_down)], "grads_down_begin")
    dcg, dcv, g_cw_g, g_cw_v, g_cb_g, g_cb_v = _conv_act_bwd(up_g, up_v, gate_c, val_c, d_act)
    dug = _conv_transposed(dcg, cw_g + tok, "conv_dx_gate")
    duv = _conv_transposed(dcv, cw_v, "conv_dx_val")
    d_h2 = _mm(dug, w_up_g, tb=True, name="up_gate_dx")
    d_h2 = _mm(duv, w_up_v, tb=True, add=d_h2, out_dtype=BF16, name="up_val_dx")
    g_w_up = jnp.concatenate([_mm(h2, dug, ta=True, out_dtype=BF16, name="up_gate_dw"),
                              _mm(h2, duv, ta=True, out_dtype=BF16, name="up_val_dw")], axis=1)
    sent_up, tok = _exchange_begin(
        [grad_blocks('w_up', g_w_up), grad_blocks('conv_w', jnp.concatenate([g_cw_g, g_cw_v], axis=1))],
        "grads_up_begin")
    (d_xc, d_mo), (d_ga1, g_g_ffn, d_sh2, d_sc2) = _split(_stage_bwd(
        _fn_c, [xs, mo], [ga1 + tok] + c_par[1:], [d_x1a, d_h2], blk=c_blk, name="modulate2_bwd",
        row_grads=[(0, F32), (1, BF16)]), 2)

    d_mpre = _mm(d_mo, w_out_f, tb=True, out_dtype=BF16, name="out_proj_dx")
    g_w_out = _mm(mpre, d_mo, ta=True, out_dtype=BF16, name="out_proj_dw")
    (d_prest, d_ysc), b_grads = _split(_stage_bwd(
        _fn_b, b_rows, b_par, [d_mpre], blk=_pick(n_tok, 256, CHUNK), name="mixers_bwd",
        row_grads=[(1, BF16), (2, F32)]), 2)
    (g_s5_d, g_w_glu, g_b_glu, g_ln_g, g_ln_b, g_sgu_w, g_sgu_bt, g_w_pa, g_w_pb, g_b_gate) = b_grads

    sent_mix, tok = _exchange_begin(
        [grad_blocks('w_out', g_w_out), grad_blocks('s5_w_glu', g_w_glu), grad_blocks('w_proj_a', g_w_pa),
         grad_blocks('w_proj_b', g_w_pb)], "grads_mixer_begin")
    skip_g = jnp.tile(s5_d.reshape(-1, 1, S5_H), (1, 1, S5_T)) + tok
    s5_out = _s5_bwd(pu_g, puc_g, s5_prm, _to_groups(d_ysc), skip_g)
    g_a_re2, g_a_im2, g_ls, g_bt2, g_c2 = s5_out[2:]
    g_a_re = g_a_re2[..., :n_state] + g_a_re2[..., n_state:]
    g_a_im = g_a_im2[..., :n_state] + g_a_im2[..., n_state:]
    g_bt_re, g_bt_im = g_bt2[..., :n_state], g_bt2[..., n_state:]
    g_c_re, g_c_im = g_c2[..., :n_state], g_c2[..., n_state:]

    part = {
        's5_a_re': g_a_re, 's5_a_im': g_a_im, 's5_log_step': g_ls,
        's5_b_re': jnp.swapaxes(g_bt_re, 2, 3), 's5_b_im': jnp.swapaxes(g_bt_im, 2, 3),
        's5_c_re': g_c_re, 's5_c_im': g_c_im, 's5_d': g_s5_d, 's5_b_glu': g_b_glu, 'sgu_ln_g': g_ln_g,
        'sgu_ln_b': g_ln_b, 'sgu_w': g_sgu_w, 'sgu_b': jnp.transpose(g_sgu_bt), 'b_gate': g_b_gate,
        'g_ffn': g_g_ffn, 'conv_b': jnp.concatenate([g_cb_g, g_cb_v], axis=1), 'g_final': d_g_final,
    }
    early = [n for n in REPLICATED if n not in LATE_REPLICATED and n not in UNPACKED_REPLICATED]
    early_part = _pack([part[n] for n in early])
    own_small = {n: two_d(part[n]) for n in UNPACKED_REPLICATED}
    sent_small, tok = _exchange_begin(
        [(early_part, "gather")] + [(own_small[n], "gather") for n in UNPACKED_REPLICATED], "grads_small_begin")
    d_pu, d_puc = _from_groups(s5_out[0]), _from_groups(s5_out[1]) + tok

    g_w_in_u = _mm(hc, d_puc, ta=True, name="proj_u_ctx_dw")
    g_w_in_u = _mm(h, d_pu, ta=True, add=g_w_in_u, out_dtype=BF16, name="proj_u_dw")
    g_w_in = jnp.concatenate([g_w_in_u, _mm(h, d_prest, ta=True, out_dtype=BF16, name="proj_rest_dw")], axis=1)
    sent_in, tok = _exchange_begin([grad_blocks('w_in', g_w_in)], "grads_in_begin")
    w_in_u_behind = w_in_u + _b16(tok)
    d_h = _mm(d_pu, w_in_u_behind, tb=True, name="proj_u_dx")
    d_h = _mm(d_prest, w_in_rest, tb=True, add=d_h, out_dtype=BF16, name="proj_rest_dx")
    d_hc = _mm(d_puc, w_in_u_behind, tb=True, out_dtype=BF16, name="proj_u_ctx_dx")

    (grad_x,), (g_g_mix_x, d_sh1, d_sc1) = _split(_stage_bwd(
        _fn_a_res, [xs], a_par, [d_h, d_xc], blk=c_blk, name="modulate1_bwd", row_grads=[(0, F32)]), 1)
    _, (g_g_mix_c, d_sh1c, d_sc1c) = _split(_stage_bwd(
        _fn_a, [cx], ac_par, [d_hc], blk=_pick(n_ctx, 512, 8), name="modulate1_ctx_bwd", row_grads=[]), 0)

    zeros = jnp.zeros((1, (n_mod - 2) * d), F32)
    d_mod = jnp.concatenate([d_sh1, d_sc1, d_ga1, d_sh2, d_sc2, d_ga2], axis=1)
    d_mod_c = jnp.concatenate([d_sh1c, d_sc1c, zeros], axis=1)
    (d_mod_all,) = _exchange([(jnp.concatenate([d_mod, d_mod_c], axis=0), "gather")], "gather_dmod")
    d_mod_rows = jnp.transpose(d_mod_all, (1, 0, 2)).reshape(2 * N_DEV, n_mod * d)
    d_mod_mine = lax.dynamic_slice_in_dim(d_mod_rows, me * mod_cols, mod_cols, axis=1)
    (d_cs,), (g_w_ada,) = _split(_stage_bwd(
        _fn_mod, [cs_in], [w_ada_loc], [d_mod_mine], blk=2 * N_DEV, name="mod_bwd", row_grads=[(0, F32)]), 1)

    part.update({'c_ctx': jnp.sum(d_cs[N_DEV:], axis=0), 'b_ada': d_mod + d_mod_c, 'g_mix': g_g_mix_x + g_g_mix_c})
    late_parts, loss_parts = _exchange(
        [(_pack([part[n] for n in LATE_REPLICATED]), "gather"), (jnp.broadcast_to(loss_part, (8, LANE)), "gather")],
        "exchange_grads")

    summed = {}
    small_parts = _exchange_end(sent_small, late_parts, "grads_small_end")
    early_parts = small_parts[0]
    for n, parts in zip(UNPACKED_REPLICATED, small_parts[1:]):
        summed[n], own_block[n] = parts, own_small[n]
    (summed['w_down'],) = _exchange_end(sent_down, late_parts, "grads_down_end")
    summed['w_up'], summed['conv_w'] = _exchange_end(sent_up, late_parts, "grads_up_end")
    summed['w_out'], summed['s5_w_glu'], summed['w_proj_a'], summed['w_proj_b'] = _exchange_end(
        sent_mix, late_parts, "grads_mixer_end")
    (summed['w_in'],) = _exchange_end(sent_in, late_parts, "grads_in_end")

    out = {}
    me_arr = me.reshape(1, 1).astype(jnp.int32)
    for names, parts, own, tag in ((early, early_parts, early_part, "early"),
                                   (LATE_REPLICATED, late_parts, None, "late")):
        res = _adamw(_pack([wts[n] for n in names]), parts, _pack([mom1[n] for n in names]),
                     _pack([mom2[n] for n in names]), "adamw_replicated_" + tag, own=own,
                     me=None if own is None else me_arr)
        res = [_unpack(r, [wts[n].shape for n in names]) for r in res]
        for i, n in enumerate(names):
            out[n] = tuple(r[i] for r in res)
    for n, parts in summed.items():
        shape = wts[n].shape
        res = _adamw(two_d(wts[n]), parts, two_d(mom1[n]), two_d(mom2[n]), "adamw_" + n, own=own_block[n],
                     me=me_arr)
        out[n] = tuple(r.reshape(shape) for r in res)
    res = _adamw(w_ada_loc, g_w_ada[None], m_w_ada[0], v_w_ada[0], "adamw_w_ada")
    out['w_ada'] = tuple(r.reshape(w_ada.shape) for r in res)

    loss = jnp.sum(loss_parts[:, 0, 0])
    return (loss, grad_x[None], *[out[n][0] for n in WEIGHTS], *[out[n][1] for n in WEIGHTS],
            *[out[n][2] for n in WEIGHTS], *[out[n][3] for n in WEIGHTS])


def _split(res, n_rows):
    return tuple(res[:n_rows]), tuple(res[n_rows:])
```

```python
import functools
import math

import jax
import jax.numpy as jnp
from jax import lax
from jax.experimental import pallas as pl
from jax.experimental.pallas import tpu as pltpu

F32 = jnp.float32
BF16 = jnp.bfloat16
HI = lax.Precision.HIGHEST

N_DEV = 8
GRID_W = 64
CHUNK = 128
EPS = 1e-6
S5_T = 32
S5_H = 16
LANE = 128
HALO = 128
VMEM_LIMIT = 56 * 1024 * 1024
PACK_ROWS = 256

ADAM_LR = 0.001
ADAM_B1 = 0.9
ADAM_B2 = 0.999
ADAM_EPS = 1e-08
ADAM_WD = 0.01
ADAM_STEP = 10

WEIGHTS = ['c_ctx', 'w_ada', 'b_ada', 'g_mix', 'w_in', 's5_a_re', 's5_a_im', 's5_log_step', 's5_b_re', 's5_b_im',
           's5_c_re', 's5_c_im', 's5_d', 's5_w_glu', 's5_b_glu', 'sgu_ln_g', 'sgu_ln_b', 'sgu_w', 'sgu_b',
           'w_proj_a', 'w_proj_b', 'b_gate', 'w_out', 'g_ffn', 'w_up', 'conv_w', 'conv_b', 'w_down', 'g_final']
COL_SHARDED = ('w_ada', 'w_in', 'w_proj_a', 'w_proj_b', 'w_up', 'conv_w')
ROW_SHARDED = ('s5_w_glu', 'w_out', 'w_down')
SHARDED = COL_SHARDED + ROW_SHARDED
REPLICATED = [n for n in WEIGHTS if n not in SHARDED]
LATE_REPLICATED = ['c_ctx', 'b_ada', 'g_mix']
UNPACKED_REPLICATED = ['sgu_w', 's5_c_re', 's5_c_im']


def _call(body, **kw):
    return pl.pallas_call(body, **kw)


def _params(n_grid):
    return pltpu.CompilerParams(dimension_semantics=("arbitrary",) * n_grid, vmem_limit_bytes=VMEM_LIMIT)


def _pick(dim, pref, unit=LANE):
    best = None
    d = unit
    while d <= min(dim, pref):
        if dim % d == 0:
            best = d
        d += unit
    return best if best is not None else dim


def _dg(a, b, ca, cb, prec=None):
    return lax.dot_general(a, b, (((ca,), (cb,)), ((), ())), precision=prec, preferred_element_type=F32)


def _b16(v):
    return v.astype(BF16)


@jax.custom_vjp
def mmb(a, b):
    return _dg(_b16(a), _b16(b), 1, 0)


def _mmb_fwd(a, b):
    return mmb(a, b), (a, b)


def _mmb_bwd(res, g):
    a, b = res
    g = _b16(g)
    return _dg(g, _b16(b), 1, 1).astype(a.dtype), _dg(_b16(a), g, 0, 0).astype(b.dtype)


mmb.defvjp(_mmb_fwd, _mmb_bwd)


@jax.custom_vjp
def mmb_nt(a, b):
    return _dg(_b16(a), _b16(b), 1, 1)


def _mmb_nt_fwd(a, b):
    return mmb_nt(a, b), (a, b)


def _mmb_nt_bwd(res, g):
    a, b = res
    g = _b16(g)
    return _dg(g, _b16(b), 1, 0).astype(a.dtype), _dg(g, _b16(a), 0, 0).astype(b.dtype)


mmb_nt.defvjp(_mmb_nt_fwd, _mmb_nt_bwd)


@jax.custom_vjp
def mmf(a, b):
    return _dg(a, b, 1, 0, HI)


def _mmf_fwd(a, b):
    return mmf(a, b), (a, b)


def _mmf_bwd(res, g):
    a, b = res
    return _dg(g, b, 1, 1, HI), _dg(a, g, 0, 0, HI)


mmf.defvjp(_mmf_fwd, _mmf_bwd)


def _dg3(a, b, ca, cb):
    ah, bh = _b16(a), _b16(b)
    al, bl = _b16(a - ah.astype(F32)), _b16(b - bh.astype(F32))
    return _dg(ah, bh, ca, cb) + _dg(ah, bl, ca, cb) + _dg(al, bh, ca, cb)


@jax.custom_vjp
def mm3_nt(a, b):
    return _dg3(a, b, 1, 1)


def _mm3_nt_fwd(a, b):
    return mm3_nt(a, b), (a, b)


def _mm3_nt_bwd(res, g):
    a, b = res
    return _dg3(g, b, 1, 0), _dg3(g, a, 0, 0)


mm3_nt.defvjp(_mm3_nt_fwd, _mm3_nt_bwd)


def _shift_impl(x, k, up):
    n = x.shape[0]
    idx = lax.broadcasted_iota(jnp.int32, (n, 1), 0)
    if up:
        return jnp.where(idx < n - k, pltpu.roll(x, n - k, 0), 0.0)
    return jnp.where(idx >= k, pltpu.roll(x, k, 0), 0.0)


@functools.partial(jax.custom_vjp, nondiff_argnums=(1, 2))
def _shift(x, k, up):
    return _shift_impl(x, k, up)


def _shift_fwd(x, k, up):
    return _shift_impl(x, k, up), None


def _shift_bwd(k, up, _, g):
    return (_shift_impl(g, k, not up),)


_shift.defvjp(_shift_fwd, _shift_bwd)


def _mm(a, b, *, name, ta=False, tb=False, add=None, out_dtype=F32, tm_pref=2048, tn_pref=1408, tk_pref=1408):
    m, k = (a.shape[1], a.shape[0]) if ta else a.shape
    n = b.shape[0] if tb else b.shape[1]
    if add is not None and out_dtype != F32:
        tm_pref = min(tm_pref, 1024)
    if ta:
        tk_pref = max(tk_pref, 2048)
    tm, tn, tk = _pick(m, tm_pref), _pick(n, tn_pref), _pick(k, tk_pref)
    nk = k // tk
    a_spec = (pl.BlockSpec((tk, tm), lambda i, j, kk: (kk, i)) if ta
              else pl.BlockSpec((tm, tk), lambda i, j, kk: (i, kk)))
    b_spec = (pl.BlockSpec((tn, tk), lambda i, j, kk: (j, kk)) if tb
              else pl.BlockSpec((tk, tn), lambda i, j, kk: (kk, j)))
    o_spec = pl.BlockSpec((tm, tn), lambda i, j, kk: (i, j))
    ca, cb = (0 if ta else 1), (1 if tb else 0)
    has_add = add is not None
    in_place = out_dtype == F32 or nk == 1

    def body(*refs):
        a_ref, b_ref = refs[0], refs[1]
        o_ref = refs[3] if has_add else refs[2]
        prod = _dg(_b16(a_ref[...]), _b16(b_ref[...]), ca, cb)
        if nk == 1:
            if has_add:
                prod = prod + refs[2][...].astype(F32)
            o_ref[...] = prod.astype(o_ref.dtype)
            return
        acc_ref = o_ref if in_place else refs[-1]
        kk = pl.program_id(2)

        @pl.when(kk == 0)
        def _():
            acc_ref[...] = prod

        @pl.when(kk > 0)
        def _():
            acc_ref[...] += prod

        if has_add or not in_place:
            @pl.when(kk == nk - 1)
            def _():
                r = acc_ref[...]
                if has_add:
                    r = r + refs[2][...].astype(F32)
                o_ref[...] = r.astype(o_ref.dtype)

    ins = [a, b] + ([add] if has_add else [])
    in_specs = [a_spec, b_spec] + ([o_spec] if has_add else [])
    return _call(body, name=name, grid=(m // tm, n // tn, nk), in_specs=in_specs, out_specs=o_spec,
                 out_shape=jax.ShapeDtypeStruct((m, n), out_dtype),
                 scratch_shapes=[] if in_place else [pltpu.VMEM((tm, tn), F32)],
                 compiler_params=_params(3))(*ins)


def _row_spec(blk, width):
    return pl.BlockSpec((blk, width), lambda i: (i, 0))


def _whole_spec(shape):
    return pl.BlockSpec(shape, lambda i: (0,) * len(shape))


def _stage_fwd(fn, rows, params, outs, *, blk, name, n_rows=None):
    n = n_rows or rows[0].shape[0]
    nr, npar = len(rows), len(params)

    def body(*refs):
        vals = [r[...] for r in refs[:nr + npar]]
        res = fn(*vals)
        for o_ref, v in zip(refs[nr + npar:], res):
            o_ref[...] = v.astype(o_ref.dtype)

    return _call(body, name=name, grid=(n // blk,),
                 in_specs=[_row_spec(blk, r.shape[1]) for r in rows] + [_whole_spec(p.shape) for p in params],
                 out_specs=[_row_spec(blk, w) for w, _ in outs],
                 out_shape=[jax.ShapeDtypeStruct((n, w), dt) for w, dt in outs],
                 compiler_params=_params(1))(*rows, *params)


def _stage_bwd(fn, rows, params, cts, *, blk, name, row_grads, n_rows=None):
    n = n_rows or rows[0].shape[0]
    nr, npar, nct = len(rows), len(params), len(cts)

    def body(*refs):
        vals = [r[...].astype(F32) for r in refs[:nr + npar]]
        ct = [r[...] for r in refs[nr + npar:nr + npar + nct]]
        d_rows = refs[nr + npar + nct:nr + npar + nct + len(row_grads)]
        d_par = refs[nr + npar + nct + len(row_grads):]
        res, vjp = jax.vjp(fn, *vals)
        g = vjp(tuple(c.astype(r.dtype) for c, r in zip(ct, res)))
        for o_ref, (j, _) in zip(d_rows, row_grads):
            o_ref[...] = g[j].astype(o_ref.dtype)

        @pl.when(pl.program_id(0) == 0)
        def _():
            for o_ref in d_par:
                o_ref[...] = jnp.zeros_like(o_ref)

        for j, o_ref in enumerate(d_par):
            o_ref[...] += g[nr + j].astype(F32)

    return _call(body, name=name, grid=(n // blk,),
                 in_specs=([_row_spec(blk, r.shape[1]) for r in rows] + [_whole_spec(p.shape) for p in params]
                           + [_row_spec(blk, c.shape[1]) for c in cts]),
                 out_specs=([_row_spec(blk, rows[j].shape[1]) for j, _ in row_grads]
                            + [_whole_spec(p.shape) for p in params]),
                 out_shape=([jax.ShapeDtypeStruct((n, rows[j].shape[1]), dt) for j, dt in row_grads]
                            + [jax.ShapeDtypeStruct(p.shape, F32) for p in params]),
                 compiler_params=_params(1))(*rows, *params, *cts)


def _stage_loss(fn, rows, params, *, blk, name, row_grads):
    n = rows[0].shape[0]
    nr, npar = len(rows), len(params)

    def body(*refs):
        vals = [r[...].astype(F32) for r in refs[:nr + npar]]
        loss_ref = refs[nr + npar]
        d_rows = refs[nr + npar + 1:nr + npar + 1 + len(row_grads)]
        d_par = refs[nr + npar + 1 + len(row_grads):]
        res, vjp = jax.vjp(fn, *vals)
        g = vjp(jnp.ones_like(res))
        for o_ref, (j, _) in zip(d_rows, row_grads):
            o_ref[...] = g[j].astype(o_ref.dtype)

        @pl.when(pl.program_id(0) == 0)
        def _():
            loss_ref[...] = jnp.zeros_like(loss_ref)
            for o_ref in d_par:
                o_ref[...] = jnp.zeros_like(o_ref)

        loss_ref[...] += res
        for j, o_ref in enumerate(d_par):
            o_ref[...] += g[nr + j].astype(F32)

    return _call(body, name=name, grid=(n // blk,),
                 in_specs=[_row_spec(blk, r.shape[1]) for r in rows] + [_whole_spec(p.shape) for p in params],
                 out_specs=([_whole_spec((1, 1))] + [_row_spec(blk, rows[j].shape[1]) for j, _ in row_grads]
                            + [_whole_spec(p.shape) for p in params]),
                 out_shape=([jax.ShapeDtypeStruct((1, 1), F32)]
                            + [jax.ShapeDtypeStruct((n, rows[j].shape[1]), dt) for j, dt in row_grads]
                            + [jax.ShapeDtypeStruct(p.shape, F32) for p in params]),
                 compiler_params=_params(1))(*rows, *params)


def _rms(x, g):
    return x * lax.rsqrt(jnp.mean(x * x, axis=-1, keepdims=True) + EPS) * g


def _modulate(x, g, shift, scale):
    return _rms(x, g) * (1.0 + scale) + shift


def _fn_mod(cs, w_ada):
    return (mmb(jax.nn.silu(cs), w_ada),)


def _fn_a(x, g_mix, sh, sc):
    return (_b16(_modulate(x, g_mix, sh, sc)),)


def _fn_a_res(x, g_mix, sh, sc):
    return _b16(_modulate(x, g_mix, sh, sc)), x


def _sgu_spatial(v, sgu_w, sgu_bt):
    rows, width = v.shape
    gdim = width // (sgu_w.shape[0] // CHUNK)
    groups = width // gdim
    expand = (lax.broadcasted_iota(jnp.int32, (groups, width), 1) // gdim
              == lax.broadcasted_iota(jnp.int32, (groups, width), 0)).astype(F32)
    bias = mmf(sgu_bt, expand)
    lane = lax.broadcasted_iota(jnp.int32, (CHUNK, LANE), 1)
    per_lane_block = LANE // gdim
    chunks = []
    for ci in range(rows // CHUNK):
        vc = v[ci * CHUNK:(ci + 1) * CHUNK]
        blocks = []
        for lb in range(width // LANE):
            vb = vc[:, lb * LANE:(lb + 1) * LANE]
            acc = None
            for s in range(per_lane_block):
                g = lb * per_lane_block + s
                r = mmb(sgu_w[g * CHUNK:(g + 1) * CHUNK], vb)
                sel = (lane // gdim) == s
                acc = jnp.where(sel, r, 0.0) if acc is None else jnp.where(sel, r, acc)
            blocks.append(acc)
        chunks.append(jnp.concatenate(blocks, axis=1) + bias)
    return jnp.concatenate(chunks, axis=0)


def _fn_b(pu, prest, ysc, s5_d, w_glu, b_glu, ln_g, ln_b, sgu_w, sgu_bt, w_pa, w_pb, b_gate):
    sw = ln_g.shape[1]
    y = jax.nn.gelu(pu * s5_d + ysc)
    ya = y * jax.nn.sigmoid(mmb(y, w_glu) + b_glu)
    z = jax.nn.gelu(prest[:, :2 * sw])
    u, v = z[:, :sw], z[:, sw:]
    vc = v - jnp.mean(v, axis=-1, keepdims=True)
    v = vc * lax.rsqrt(jnp.mean(vc * vc, axis=-1, keepdims=True) + EPS) * ln_g + ln_b
    yb = u * _sgu_spatial(v, sgu_w, sgu_bt)
    gates = jax.nn.sigmoid(prest[:, 2 * sw:] + b_gate)
    d = gates.shape[1] // 2
    return (_b16(gates[:, :d] * mmb(ya, w_pa) + gates[:, d:] * mmb(yb, w_pb)),)


def _fn_c(x, mo, ga1, g_ffn, sh2, sc2):
    x1 = x + ga1 * mo
    return x1, _b16(_modulate(x1, g_ffn, sh2, sc2))


def _fn_e(x1, dn, tgt, ga2, g_final):
    y = _rms(x1 + ga2 * dn, g_final)
    err = (y - tgt) ** 2
    return 0.5 * jnp.sum(jnp.mean(err, axis=-1, keepdims=True), axis=0, keepdims=True)


def _swap_impl(x):
    return pltpu.roll(x, x.shape[1] // 2, 1)


@jax.custom_vjp
def _swap_halves(x):
    return _swap_impl(x)


_swap_halves.defvjp(lambda x: (_swap_impl(x), None), lambda _, g: (_swap_impl(g),))


def _s5_direction(u, mask, a_re, a_im, log_step, bt, c, rev):
    nc, width = u.shape
    t_len = width // S5_H
    n2 = a_re.shape[1]
    lane = lax.broadcasted_iota(jnp.int32, (1, n2), 1)
    sign = jnp.where(lane < n2 // 2, -1.0, 1.0)
    dt = jnp.exp(log_step)
    lr, li = a_re * dt, a_im * dt
    mag = jnp.exp(lr)
    ab_re, ab_im = mag * jnp.cos(li), mag * jnp.sin(li)
    p, q = ab_re - 1.0, ab_im
    den = a_re * a_re + a_im * a_im
    k_re, k_im = (p * a_re + q * a_im) / den, (q * a_re - p * a_im) / den
    bb = k_re * bt + (k_im * sign) * _swap_halves(bt)

    def power(e):
        m = jnp.exp(lr * e)
        return m * jnp.cos(li * e), m * jnp.sin(li * e)

    order = range(t_len - 1, -1, -1) if rev else range(t_len)
    e1 = jnp.concatenate([jnp.full((1, 1, n2), float(t_len - 1 - pos), F32) for pos in order], axis=0)
    lr3, li3, sign3 = lr.reshape(1, 1, n2), li.reshape(1, 1, n2), sign.reshape(1, 1, n2)
    m1, c1, s1 = jnp.exp(lr3 * e1), jnp.cos(li3 * e1), jnp.sin(li3 * e1)
    m2 = jnp.exp(-(lr3 * e1))
    x1r, x1i = m1 * c1, m1 * s1
    x2r, x2i = m2 * c1, -(m2 * s1)
    at_r, at_i = [v.reshape(1, 1, n2) for v in power(float(t_len))]
    x3r, x3i = at_r * x2r - at_i * x2i, at_r * x2i + at_i * x2r

    def rows(xr, xi, z):
        z3, zs3 = z.reshape(1, S5_H, n2), _swap_halves(z).reshape(1, S5_H, n2)
        return (xr * z3 + (xi * sign3) * zs3).reshape(width, n2)

    p_in = rows(x1r, x1i, bb)
    r_out = rows(x2r, x2i, c)
    w_out = rows(x3r, x3i, c)
    toep = mm3_nt(p_in, r_out * (-sign)) * mask

    state = mmb(u, p_in)
    k = 1
    while k < nc:
        ar, ai = power(float(t_len * k))
        moved = _shift(state, k, rev)
        state = state + ar * moved + (ai * sign) * _swap_halves(moved)
        k *= 2
    entering = _shift(state, 1, rev)
    return mmb(u, toep) + mmb_nt(entering, w_out * (-sign))


def _fn_s5(masks, x_chunks, ctx_chunks, *prm):
    nx, nctx = x_chunks.shape[0], ctx_chunks.shape[0]
    pad = (-(nx + 2 * nctx)) % LANE
    u = jnp.concatenate([ctx_chunks, x_chunks, ctx_chunks, jnp.zeros((pad, x_chunks.shape[1]), F32)], axis=0)
    out = None
    for d in range(2):
        y = _s5_direction(u, masks[d], *[p[d] for p in prm], rev=(d == 1))
        out = y if out is None else out + y
    return out[nctx:nctx + nx]


def _s5_specs(prm):
    return [pl.BlockSpec((2, 1) + p.shape[2:], lambda g: (0, g, 0, 0)) for p in prm]


def _group_spec(a):
    return pl.BlockSpec((1,) + a.shape[1:], lambda i: (i, 0, 0))


def _s5_masks(width):
    pos = jnp.arange(width) // S5_H
    causal = (pos[None, :] >= pos[:, None]).astype(F32)
    return jnp.stack([causal, causal.T])


def _s5_fwd(x_g, ctx_g, prm):
    masks = _s5_masks(x_g.shape[2])

    def body(*refs):
        pv = [r[:, 0] for r in refs[3:3 + len(prm)]]
        refs[-1][0] = _fn_s5(refs[0][...], refs[1][0].astype(F32), refs[2][0], *pv)

    return _call(body, name="s5_fwd", grid=(x_g.shape[0],),
                 in_specs=[_whole_spec(masks.shape), _group_spec(x_g), _group_spec(ctx_g)] + _s5_specs(prm),
                 out_specs=_group_spec(x_g), out_shape=jax.ShapeDtypeStruct(x_g.shape, F32),
                 compiler_params=_params(1))(masks, x_g, ctx_g, *prm)


def _s5_bwd(x_g, ctx_g, prm, dy_g, skip_g):
    npar = len(prm)
    masks = _s5_masks(x_g.shape[2])

    def body(*refs):
        pv = [r[:, 0] for r in refs[3:3 + npar]]
        dy = refs[3 + npar][0]
        _, vjp = jax.vjp(functools.partial(_fn_s5, refs[0][...]), refs[1][0].astype(F32), refs[2][0], *pv)
        grads = vjp(dy)
        outs = refs[5 + npar:]
        outs[0][0] = (grads[0] + dy * refs[4 + npar][0]).astype(outs[0].dtype)
        outs[1][0] = grads[1]
        for o_ref, gv in zip(outs[2:], grads[2:]):
            o_ref[:, 0] = gv

    return _call(body, name="s5_bwd", grid=(x_g.shape[0],),
                 in_specs=([_whole_spec(masks.shape), _group_spec(x_g), _group_spec(ctx_g)] + _s5_specs(prm)
                           + [_group_spec(dy_g), _group_spec(skip_g)]),
                 out_specs=[_group_spec(x_g), _group_spec(ctx_g)] + _s5_specs(prm),
                 out_shape=[jax.ShapeDtypeStruct(x_g.shape, BF16), jax.ShapeDtypeStruct(ctx_g.shape, F32)]
                 + [jax.ShapeDtypeStruct(p.shape, F32) for p in prm],
                 compiler_params=_params(1))(masks, x_g, ctx_g, *prm, dy_g, skip_g)


def _to_groups(tok):
    n, width = tok.shape
    g = width // S5_H
    return jnp.transpose(tok.reshape(n, g, S5_H), (1, 0, 2)).reshape(g, n // S5_T, S5_T * S5_H)


def _from_groups(grp):
    g, nc, _ = grp.shape
    return jnp.transpose(grp.reshape(g, nc * S5_T, S5_H), (1, 0, 2)).reshape(nc * S5_T, g * S5_H)


def _conv_shifted(xp, xm, xn, blk_i, n_blk):
    tb = xm.shape[0]
    xp = jnp.where(blk_i == 0, 0.0, xp.astype(F32))
    xn = jnp.where(blk_i == n_blk - 1, 0.0, xn.astype(F32))
    buf = jnp.concatenate([xp, xm.astype(F32), xn], axis=0)
    n = tb + 2 * HALO
    col = lax.broadcasted_iota(jnp.int32, (n, 1), 0) % GRID_W
    left = jnp.where(col >= 1, pltpu.roll(buf, 1, 0), 0.0)
    right = jnp.where(col <= GRID_W - 2, pltpu.roll(buf, n - 1, 0), 0.0)
    return left, buf, right


def _conv_taps(xp, xm, xn, blk_i, n_blk):
    tb = xm.shape[0]
    shifted = _conv_shifted(xp, xm, xn, blk_i, n_blk)
    taps = []
    for di in range(3):
        start = HALO + (di - 1) * GRID_W
        for dj in range(3):
            taps.append(shifted[dj][start:start + tb])
    return taps


def _conv_sum(taps, w_ref, flip=False):
    acc = None
    for k, tap in enumerate(taps):
        j = len(taps) - 1 - k if flip else k
        term = tap * w_ref[j:j + 1, :]
        acc = term if acc is None else acc + term
    return acc


def _conv_geometry(n_tok, width, tb_pref=1024):
    tb = _pick(n_tok, tb_pref, HALO)
    cb = _pick(width, 256)
    nb = tb // HALO
    last = n_tok // HALO - 1
    main = pl.BlockSpec((tb, cb), lambda j, i: (i, j))
    prev = pl.BlockSpec((HALO, cb), lambda j, i: (jnp.maximum(i * nb - 1, 0), j))
    nxt = pl.BlockSpec((HALO, cb), lambda j, i: (jnp.minimum(i * nb + nb, last), j))
    par = lambda r: pl.BlockSpec((r, cb), lambda j, i: (0, j))
    return tb, cb, main, prev, nxt, par


def _conv_act_fwd(up_g, up_v, w_g, w_v, b_g, b_v):
    n_tok, width = up_g.shape
    tb, cb, main, prev, nxt, par = _conv_geometry(n_tok, width, 2048)
    n_blk = n_tok // tb

    def body(gp, gm, gn, vp, vm, vn, wg, wv, bg, bv, o_ref, gate_ref, val_ref):
        i = pl.program_id(1)
        gate = _conv_sum(_conv_taps(gp[...], gm[...], gn[...], i, n_blk), wg) + bg[...]
        val = _conv_sum(_conv_taps(vp[...], vm[...], vn[...], i, n_blk), wv) + bv[...]
        o_ref[...] = (jax.nn.silu(gate) * val).astype(o_ref.dtype)
        gate_ref[...] = gate
        val_ref[...] = val

    shp = jax.ShapeDtypeStruct
    return _call(body, name="conv_act_fwd", grid=(width // cb, n_tok // tb),
                 in_specs=[prev, main, nxt, prev, main, nxt, par(9), par(9), par(1), par(1)],
                 out_specs=[main, main, main],
                 out_shape=[shp((n_tok, width), BF16), shp((n_tok, width), F32), shp((n_tok, width), F32)],
                 compiler_params=_params(2))(up_g, up_g, up_g, up_v, up_v, up_v, w_g, w_v, b_g, b_v)


def _conv_act_bwd(up_g, up_v, gate_c, val_c, d_act):
    n_tok, width = up_g.shape
    tb, cb, main, prev, nxt, par = _conv_geometry(n_tok, width)
    n_blk = n_tok // tb

    rows = 8

    def body(gp, gm, gn, vp, vm, vn, gc, vc, da, dcg, dcv, dwg, dwv, dbg, dbv, shifted_ref, d_ref):
        i = pl.program_id(1)
        for half, (p_ref, m_ref, n_ref) in enumerate(((gp, gm, gn), (vp, vm, vn))):
            for dj, arr in enumerate(_conv_shifted(p_ref[...], m_ref[...], n_ref[...], i, n_blk)):
                shifted_ref[3 * half + dj] = arr
        gate, val = gc[...], vc[...]
        sig = jax.nn.sigmoid(gate)
        d = da[...].astype(F32)
        d_gate = d * val * sig * (1.0 + gate * (1.0 - sig))
        d_val = d * gate * sig
        dcg[...] = d_gate.astype(dcg.dtype)
        dcv[...] = d_val.astype(dcv.dtype)
        d_ref[0] = d_gate
        d_ref[1] = d_val

        @pl.when(i == 0)
        def _():
            for r in (dwg, dwv, dbg, dbv):
                r[...] = jnp.zeros_like(r)

        dbg[...] += jnp.sum(d_gate, axis=0, keepdims=True)
        dbv[...] += jnp.sum(d_val, axis=0, keepdims=True)

        def step(r, accs):
            base = pl.multiple_of(r * rows, rows)
            out = []
            for half in range(2):
                dd = d_ref[half, pl.ds(base, rows), :]
                for di in range(3):
                    for dj in range(3):
                        tap = shifted_ref[3 * half + dj, pl.ds(base + HALO + (di - 1) * GRID_W, rows), :]
                        out.append(accs[len(out)] + tap * dd)
            return tuple(out)

        accs = lax.fori_loop(0, tb // rows, step, tuple(jnp.zeros((rows, cb), F32) for _ in range(18)), unroll=2)
        for half, ref in enumerate((dwg, dwv)):
            for k in range(9):
                ref[k:k + 1, :] += jnp.sum(accs[9 * half + k], axis=0, keepdims=True)

    shp = jax.ShapeDtypeStruct
    return _call(body, name="conv_act_bwd", grid=(width // cb, n_tok // tb),
                 in_specs=[prev, main, nxt, prev, main, nxt, main, main, main],
                 out_specs=[main, main, par(9), par(9), par(1), par(1)],
                 out_shape=[shp((n_tok, width), BF16), shp((n_tok, width), BF16), shp((9, width), F32),
                            shp((9, width), F32), shp((1, width), F32), shp((1, width), F32)],
                 scratch_shapes=[pltpu.VMEM((6, tb + 2 * HALO, cb), F32), pltpu.VMEM((2, tb, cb), F32)],
                 compiler_params=_params(2))(up_g, up_g, up_g, up_v, up_v, up_v, gate_c, val_c, d_act)


def _conv_transposed(x, w, name):
    n_tok, width = x.shape
    tb, cb, main, prev, nxt, par = _conv_geometry(n_tok, width, 2048)
    n_blk = n_tok // tb

    def body(xp, xm, xn, w_ref, o_ref):
        taps = _conv_taps(xp[...], xm[...], xn[...], pl.program_id(1), n_blk)
        o_ref[...] = _conv_sum(taps, w_ref, flip=True).astype(o_ref.dtype)

    return _call(body, name=name, grid=(width // cb, n_tok // tb), in_specs=[prev, main, nxt, par(9)],
                 out_specs=main, out_shape=jax.ShapeDtypeStruct((n_tok, width), BF16),
                 compiler_params=_params(2))(x, x, x, w)


def _adamw(w, g_parts, m, v, name, own=None, me=None):
    rows, cols = w.shape
    parts = g_parts.shape[0]
    blk = _pick(rows, 256, 8)
    spec = pl.BlockSpec((blk, cols), lambda i: (i, 0))
    has_own = own is not None

    def body(*refs):
        w_ref, g_ref, m_ref, v_ref = refs[:4]
        g_out, d_out, m_out, v_out = refs[-4:]

        def part(p):
            if has_own:
                return jnp.where(refs[5][...] == p, refs[4][...], g_ref[p]).astype(F32)
            return g_ref[p].astype(F32)

        g = part(0)
        for p in range(1, parts):
            g = g + part(p)
        m_new = ADAM_B1 * m_ref[...] + (1.0 - ADAM_B1) * g
        v_new = ADAM_B2 * v_ref[...] + (1.0 - ADAM_B2) * (g * g)
        m_hat = m_new / (1.0 - ADAM_B1 ** ADAM_STEP)
        v_hat = v_new / (1.0 - ADAM_B2 ** ADAM_STEP)
        g_out[...] = g
        d_out[...] = -ADAM_LR * (m_hat / (jnp.sqrt(v_hat) + ADAM_EPS) + ADAM_WD * w_ref[...])
        m_out[...] = m_new
        v_out[...] = v_new

    extra = [own, me] if has_own else []
    return _call(body, name=name, grid=(rows // blk,),
                 in_specs=([spec, pl.BlockSpec((parts, blk, cols), lambda i: (0, i, 0)), spec, spec]
                           + ([spec, _whole_spec((1, 1))] if has_own else [])),
                 out_specs=[spec] * 4, out_shape=[jax.ShapeDtypeStruct((rows, cols), F32)] * 4,
                 compiler_params=_params(1))(w, g_parts, m, v, *extra)


def _exchange(items, name):
    n = len(items)
    hbm = pl.BlockSpec(memory_space=pl.ANY)

    def body(*refs):
        srcs, outs = refs[:n], refs[n:2 * n]
        send_sems, recv_sems, own_sems = refs[2 * n:]
        x, y, c = lax.axis_index("x"), lax.axis_index("y"), lax.axis_index("c")
        me = 4 * x + 2 * y + c
        own = []
        for i, (_, mode) in enumerate(items):
            src = srcs[i] if mode == "gather" else srcs[i].at[me]
            cp = pltpu.make_async_copy(src, outs[i].at[me], own_sems.at[i])
            cp.start()
            own.append(cp)
        sent = []
        for i, (_, mode) in enumerate(items):
            for k in range(1, N_DEV):
                px = 1 - x if k & 4 else x
                py = 1 - y if k & 2 else y
                pc = 1 - c if k & 1 else c
                peer = 4 * px + 2 * py + pc
                src = srcs[i] if mode == "gather" else srcs[i].at[peer]
                cp = pltpu.make_async_remote_copy(
                    src_ref=src, dst_ref=outs[i].at[me], send_sem=send_sems.at[i, k - 1],
                    recv_sem=recv_sems.at[i, k - 1], device_id=(px, py, pc), device_id_type=pl.DeviceIdType.MESH)
                cp.start()
                landing = pltpu.make_async_remote_copy(
                    src_ref=src, dst_ref=outs[i].at[peer], send_sem=send_sems.at[i, k - 1],
                    recv_sem=recv_sems.at[i, k - 1], device_id=(px, py, pc), device_id_type=pl.DeviceIdType.MESH)
                sent.append((cp, landing))
        for cp in own:
            cp.wait()
        for cp, landing in sent:
            cp.wait_send()
            landing.wait_recv()

    out_shape = [jax.ShapeDtypeStruct((N_DEV,) + (a.shape if mode == "gather" else a.shape[1:]), a.dtype)
                 for a, mode in items]
    return _call(body, name=name, in_specs=[hbm] * n, out_specs=[hbm] * n, out_shape=out_shape,
                 scratch_shapes=[pltpu.SemaphoreType.DMA((n, N_DEV - 1)), pltpu.SemaphoreType.DMA((n, N_DEV - 1)),
                                 pltpu.SemaphoreType.DMA((n,))])(*[a for a, _ in items])


def _peer(k, x, y, c):
    px = 1 - x if k & 4 else x
    py = 1 - y if k & 2 else y
    pc = 1 - c if k & 1 else c
    return (px, py, pc), 4 * px + 2 * py + pc


_HBM_SPEC = pl.BlockSpec(memory_space=pltpu.HBM)
_SEM_SPEC = pl.BlockSpec(memory_space=pltpu.SEMAPHORE)
_SPLIT_EFFECT = pltpu.SideEffectType.DATAFLOW_SIDE_EFFECTING


def _exchange_begin(items, name, after=None):
    n = len(items)
    modes = [mode for _, mode in items]
    srcs = [pltpu.with_memory_space_constraint(a, pltpu.HBM) for a, _ in items]
    lands = [pltpu.with_memory_space_constraint(
        lax.empty((N_DEV,) + (a.shape if mode == "gather" else a.shape[1:]), a.dtype), pltpu.HBM)
        for a, mode in items]

    def body(*refs):
        src_refs, land_refs = refs[:n], refs[n:2 * n]
        token = refs[-1]
        send_sems, recv_sems = refs[-2 * n - 3], refs[-2 * n - 2]
        x, y, c = lax.axis_index("x"), lax.axis_index("y"), lax.axis_index("c")
        me = 4 * x + 2 * y + c
        for i in range(n):
            for k in range(1, N_DEV):
                coords, peer = _peer(k, x, y, c)
                src = src_refs[i] if modes[i] == "gather" else src_refs[i].at[peer]
                pltpu.make_async_remote_copy(
                    src_ref=src, dst_ref=land_refs[i].at[me], send_sem=send_sems.at[i * (N_DEV - 1) + k - 1],
                    recv_sem=recv_sems.at[i * (N_DEV - 1) + k - 1], device_id=coords,
                    device_id_type=pl.DeviceIdType.MESH).start()
        token[...] = jnp.zeros_like(token)

    sems = pltpu.SemaphoreType.DMA((n * (N_DEV - 1),))
    order = [] if after is None else [after]
    res = _call(body, name=name,
                out_shape=(sems, sems, *[pltpu.HBM(a.shape, a.dtype) for a in srcs + lands],
                           jax.ShapeDtypeStruct((8, LANE), F32)),
                in_specs=[_HBM_SPEC] * (2 * n) + [pl.BlockSpec(memory_space=pl.ANY)] * len(order),
                out_specs=(_SEM_SPEC, _SEM_SPEC, *[_HBM_SPEC] * (2 * n), pl.BlockSpec(memory_space=pltpu.VMEM)),
                input_output_aliases={i: 2 + i for i in range(2 * n)},
                compiler_params=pltpu.CompilerParams(has_side_effects=_SPLIT_EFFECT))(*srcs, *lands, *order)
    return (modes, res[0], res[1], list(res[2:2 + n]), list(res[2 + n:2 + 2 * n])), res[-1][0, 0]


def _exchange_end(handle, after, name):
    modes, send_sems, recv_sems, srcs, lands = handle
    n = len(modes)

    def wait_body(*refs):
        src_refs, land_refs = refs[:n], refs[n:2 * n]
        send, recv = refs[2 * n], refs[2 * n + 1]
        x, y, c = lax.axis_index("x"), lax.axis_index("y"), lax.axis_index("c")
        for i in range(n):
            for k in range(1, N_DEV):
                coords, peer = _peer(k, x, y, c)
                src = src_refs[i] if modes[i] == "gather" else src_refs[i].at[peer]
                cp = pltpu.make_async_remote_copy(
                    src_ref=src, dst_ref=land_refs[i].at[peer], send_sem=send.at[i * (N_DEV - 1) + k - 1],
                    recv_sem=recv.at[i * (N_DEV - 1) + k - 1], device_id=coords,
                    device_id_type=pl.DeviceIdType.MESH)
                cp.wait_send()
                cp.wait_recv()

    res = _call(wait_body, name=name, out_shape=[pltpu.HBM(a.shape, a.dtype) for a in srcs + lands],
                in_specs=[_HBM_SPEC] * (2 * n) + [_SEM_SPEC, _SEM_SPEC, pl.BlockSpec(memory_space=pl.ANY)],
                out_specs=[_HBM_SPEC] * (2 * n), input_output_aliases={i: i for i in range(2 * n)},
                compiler_params=pltpu.CompilerParams(has_side_effects=_SPLIT_EFFECT))(
                    *srcs, *lands, send_sems, recv_sems, after)
    return res[n:]


def _with_own(land, own, me):
    slot = lax.broadcasted_iota(jnp.int32, (N_DEV,) + (1,) * (land.ndim - 1), 0)
    return jnp.where(slot == me, own[None], land)


def _cols_from_blocks(g):
    return jnp.transpose(g, (1, 0, 2)).reshape(g.shape[1], N_DEV * g.shape[2])


def _blocks_from_cols(w):
    r, c8 = w.shape
    return jnp.transpose(w.reshape(r, N_DEV, c8 // N_DEV), (1, 0, 2))


def _pack(arrs):
    flat = jnp.concatenate([a.reshape(-1).astype(F32) for a in arrs])
    rows = -(-flat.shape[0] // LANE)
    rows = -(-rows // PACK_ROWS) * PACK_ROWS if rows > PACK_ROWS else -(-rows // 8) * 8
    return jnp.pad(flat, (0, rows * LANE - flat.shape[0])).reshape(rows, LANE)


def _unpack(packed, shapes):
    flat = packed.reshape(-1)
    out, off = [], 0
    for s in shapes:
        size = math.prod(s)
        out.append(flat[off:off + size].reshape(s))
        off += size
    return out


def kernel(x, c, ctx, c_ctx, w_ada, b_ada, g_mix, w_in, s5_a_re, s5_a_im, s5_log_step, s5_b_re, s5_b_im, s5_c_re, s5_c_im, s5_d, s5_w_glu, s5_b_glu, sgu_ln_g, sgu_ln_b, sgu_w, sgu_b, w_proj_a, w_proj_b, b_gate, w_out, g_ffn, w_up, conv_w, conv_b, w_down, g_final, loss_target, m_c_ctx, m_w_ada, m_b_ada, m_g_mix, m_w_in, m_s5_a_re, m_s5_a_im, m_s5_log_step, m_s5_b_re, m_s5_b_im, m_s5_c_re, m_s5_c_im, m_s5_d, m_s5_w_glu, m_s5_b_glu, m_sgu_ln_g, m_sgu_ln_b, m_sgu_w, m_sgu_b, m_w_proj_a, m_w_proj_b, m_b_gate, m_w_out, m_g_ffn, m_w_up, m_conv_w, m_conv_b, m_w_down, m_g_final, v_c_ctx, v_w_ada, v_b_ada, v_g_mix, v_w_in, v_s5_a_re, v_s5_a_im, v_s5_log_step, v_s5_b_re, v_s5_b_im, v_s5_c_re, v_s5_c_im, v_s5_d, v_s5_w_glu, v_s5_b_glu, v_sgu_ln_g, v_sgu_ln_b, v_sgu_w, v_sgu_b, v_w_proj_a, v_w_proj_b, v_b_gate, v_w_out, v_g_ffn, v_w_up, v_conv_w, v_conv_b, v_w_down, v_g_final):
    given = dict(locals())
    wts = {n: given[n] for n in WEIGHTS}
    mom1 = {n: given["m_" + n] for n in WEIGHTS}
    mom2 = {n: given["v_" + n] for n in WEIGHTS}

    me = 4 * lax.axis_index("x") + 2 * lax.axis_index("y") + lax.axis_index("c")
    xs, cx, tgt = x[0], ctx[0], loss_target[0]
    n_tok, d = xs.shape
    n_ctx = cx.shape[0]
    s5w = s5_d.shape[1]
    ffn = w_down.shape[1] * N_DEV
    n_mod = w_ada.shape[2] * N_DEV // d
    mod_cols = w_ada.shape[2]

    def two_d(a):
        return a.reshape(-1, a.shape[-1])

    conv_w9 = conv_w[0].reshape(9, -1)
    gathered = _exchange([(c, "gather"), (_b16(w_in[0]), "gather")], "gather_weights")
    c_all = gathered[0].reshape(N_DEV, d)
    w_in_f = _cols_from_blocks(gathered[1])
    w_in_u, w_in_rest = w_in_f[:, :s5w], w_in_f[:, s5w:]

    cs_in = jnp.concatenate([c_all, jnp.broadcast_to(c_ctx[None, :], (N_DEV, d))], axis=0)
    w_ada_loc = w_ada[0]
    (mod_mine,) = _stage_fwd(_fn_mod, [cs_in], [w_ada_loc], [(mod_cols, F32)], blk=2 * N_DEV, name="mod_fwd")
    (mod_blocks,) = _exchange([(mod_mine, "gather")], "gather_mod")
    mid_own = [_b16(s5_w_glu[0]), _b16(w_proj_a[0]), _b16(w_proj_b[0])]
    mid_weights, tok_mid = _exchange_begin([(a, "gather") for a in mid_own], "gather_mid_begin", after=mod_blocks)
    late_own = [_b16(w_out[0]), _b16(w_up[0]), conv_w9, _b16(w_down[0])]
    late_weights, tok = _exchange_begin([(a, "gather") for a in late_own], "gather_late_begin", after=mod_blocks)
    mod_all = _cols_from_blocks(mod_blocks) + b_ada + (tok + tok_mid)
    mod = lax.dynamic_slice_in_dim(mod_all, me, 1, axis=0)
    mod_c = mod_all[N_DEV:N_DEV + 1]
    sh1, sc1, ga1, sh2, sc2, ga2 = [mod[:, i * d:(i + 1) * d] for i in range(n_mod)]
    sh1c, sc1c = mod_c[:, :d], mod_c[:, d:2 * d]

    a_par = [g_mix, sh1, sc1]
    ac_par = [g_mix, sh1c, sc1c]
    (h,) = _stage_fwd(_fn_a, [xs], a_par, [(d, BF16)], blk=_pick(n_tok, 512, 8), name="modulate1_fwd")
    (hc,) = _stage_fwd(_fn_a, [cx], ac_par, [(d, BF16)], blk=_pick(n_ctx, 512, 8), name="modulate1_ctx_fwd")
    pu = _mm(h, w_in_u, name="proj_u")
    prest = _mm(h, w_in_rest, name="proj_rest")
    puc = _mm(hc, w_in_u, name="proj_u_ctx")

    n_state = s5_a_re.shape[-1]

    def twice(a):
        return jnp.concatenate([a, a], axis=-1)

    s5_prm = [twice(s5_a_re[0])[:, :, None, :], twice(s5_a_im[0])[:, :, None, :], s5_log_step[0][:, :, None, None],
              jnp.concatenate([jnp.swapaxes(s5_b_re[0], 2, 3), jnp.swapaxes(s5_b_im[0], 2, 3)], axis=-1),
              jnp.concatenate([s5_c_re[0], s5_c_im[0]], axis=-1)]
    pu_g, puc_g = _to_groups(_b16(pu)), _to_groups(puc)
    ysc = _from_groups(_s5_fwd(pu_g, puc_g, s5_prm))

    mid = [_with_own(land, own, me) for land, own in zip(_exchange_end(mid_weights, ysc, "gather_mid_end"), mid_own)]
    w_glu_f, w_pa_f, w_pb_f = mid[0].reshape(-1, s5w), _cols_from_blocks(mid[1]), _cols_from_blocks(mid[2])
    b_par = [s5_d, w_glu_f, s5_b_glu, sgu_ln_g, sgu_ln_b, two_d(sgu_w[0]), jnp.transpose(sgu_b[0]),
             w_pa_f, w_pb_f, b_gate]
    b_rows = [pu, prest, ysc]
    b_blk = _pick(n_tok, 512, CHUNK)
    (mpre,) = _stage_fwd(_fn_b, b_rows, b_par, [(d, BF16)], blk=b_blk, name="mixers_fwd")
    late = [_with_own(land, own, me) for land, own in zip(_exchange_end(late_weights, mpre, "gather_late_end"),
                                                          late_own)]
    w_out_f = late[0].reshape(-1, d)
    w_up_f = _cols_from_blocks(late[1])
    w_up_g, w_up_v = w_up_f[:, :ffn], w_up_f[:, ffn:]
    conv_w_f = _cols_from_blocks(late[2])
    w_down_f = late[3].reshape(-1, d)
    mo = _mm(mpre, w_out_f, name="out_proj")

    c_par = [ga1, g_ffn, sh2, sc2]
    c_blk = _pick(n_tok, 512, 8)
    x1, h2 = _stage_fwd(_fn_c, [xs, mo], c_par, [(d, F32), (d, BF16)], blk=c_blk, name="modulate2_fwd")
    up_g = _mm(h2, w_up_g, out_dtype=BF16, name="up_gate")
    up_v = _mm(h2, w_up_v, out_dtype=BF16, name="up_val")
    cw_g, cw_v = conv_w_f[:, :ffn], conv_w_f[:, ffn:]
    cb_g, cb_v = conv_b[:, :ffn], conv_b[:, ffn:]
    act, gate_c, val_c = _conv_act_fwd(up_g, up_v, cw_g, cw_v, cb_g, cb_v)
    dn = _mm(act, w_down_f, name="down_proj")

    loss_part, d_x1a, d_dn, d_ga2, d_g_final = _stage_loss(
        _fn_e, [x1, dn, tgt], [ga2, g_final[None, :]], blk=c_blk, name="loss_head",
        row_grads=[(0, F32), (1, BF16)])

    d_act = _mm(d_dn, w_down_f, tb=True, name="down_proj_dx")
    g_w_down = _mm(act, d_dn, ta=True, out_dtype=BF16, name="down_proj_dw")

    own_block = {}

    def grad_blocks(name, g):
        blocks = _b16(_blocks_from_cols(g) if name in COL_SHARDED
                      else g.reshape(N_DEV, g.shape[0] // N_DEV, g.shape[1]))
        own_block[name] = lax.dynamic_index_in_dim(blocks, me, 0, keepdims=False)
        return blocks, "a2a"

    sent_down, tok = _exchange_begin([grad_blocks('w_down', g_w_down)], "grads_down_begin")
    dcg, dcv, g_cw_g, g_cw_v, g_cb_g, g_cb_v = _conv_act_bwd(up_g, up_v, gate_c, val_c, d_act)
    dug = _conv_transposed(dcg, cw_g + tok, "conv_dx_gate")
    duv = _conv_transposed(dcv, cw_v, "conv_dx_val")
    d_h2 = _mm(dug, w_up_g, tb=True, name="up_gate_dx")
    d_h2 = _mm(duv, w_up_v, tb=True, add=d_h2, out_dtype=BF16, name="up_val_dx")
    g_w_up = jnp.concatenate([_mm(h2, dug, ta=True, out_dtype=BF16, name="up_gate_dw"),
                              _mm(h2, duv, ta=True, out_dtype=BF16, name="up_val_dw")], axis=1)
    sent_up, tok = _exchange_begin(
        [grad_blocks('w_up', g_w_up), grad_blocks('conv_w', jnp.concatenate([g_cw_g, g_cw_v], axis=1))],
        "grads_up_begin")
    (d_xc, d_mo), (d_ga1, g_g_ffn, d_sh2, d_sc2) = _split(_stage_bwd(
        _fn_c, [xs, mo], [ga1 + tok] + c_par[1:], [d_x1a, d_h2], blk=c_blk, name="modulate2_bwd",
        row_grads=[(0, F32), (1, BF16)]), 2)

    d_mpre = _mm(d_mo, w_out_f, tb=True, out_dtype=BF16, name="out_proj_dx")
    g_w_out = _mm(mpre, d_mo, ta=True, out_dtype=BF16, name="out_proj_dw")
    (d_prest, d_ysc), b_grads = _split(_stage_bwd(
        _fn_b, b_rows, b_par, [d_mpre], blk=_pick(n_tok, 256, CHUNK), name="mixers_bwd",
        row_grads=[(1, BF16), (2, F32)]), 2)
    (g_s5_d, g_w_glu, g_b_glu, g_ln_g, g_ln_b, g_sgu_w, g_sgu_bt, g_w_pa, g_w_pb, g_b_gate) = b_grads

    sent_mix, tok = _exchange_begin(
        [grad_blocks('w_out', g_w_out), grad_blocks('s5_w_glu', g_w_glu), grad_blocks('w_proj_a', g_w_pa),
         grad_blocks('w_proj_b', g_w_pb)], "grads_mixer_begin")
    skip_g = jnp.tile(s5_d.reshape(-1, 1, S5_H), (1, 1, S5_T)) + tok
    s5_out = _s5_bwd(pu_g, puc_g, s5_prm, _to_groups(d_ysc), skip_g)
    g_a_re2, g_a_im2, g_ls, g_bt2, g_c2 = s5_out[2:]
    g_a_re = g_a_re2[..., :n_state] + g_a_re2[..., n_state:]
    g_a_im = g_a_im2[..., :n_state] + g_a_im2[..., n_state:]
    g_bt_re, g_bt_im = g_bt2[..., :n_state], g_bt2[..., n_state:]
    g_c_re, g_c_im = g_c2[..., :n_state], g_c2[..., n_state:]

    part = {
        's5_a_re': g_a_re, 's5_a_im': g_a_im, 's5_log_step': g_ls,
        's5_b_re': jnp.swapaxes(g_bt_re, 2, 3), 's5_b_im': jnp.swapaxes(g_bt_im, 2, 3),
        's5_c_re': g_c_re, 's5_c_im': g_c_im, 's5_d': g_s5_d, 's5_b_glu': g_b_glu, 'sgu_ln_g': g_ln_g,
        'sgu_ln_b': g_ln_b, 'sgu_w': g_sgu_w, 'sgu_b': jnp.transpose(g_sgu_bt), 'b_gate': g_b_gate,
        'g_ffn': g_g_ffn, 'conv_b': jnp.concatenate([g_cb_g, g_cb_v], axis=1), 'g_final': d_g_final,
    }
    early = [n for n in REPLICATED if n not in LATE_REPLICATED and n not in UNPACKED_REPLICATED]
    early_part = _pack([part[n] for n in early])
    own_small = {n: two_d(part[n]) for n in UNPACKED_REPLICATED}
    sent_small, tok = _exchange_begin(
        [(early_part, "gather")] + [(own_small[n], "gather") for n in UNPACKED_REPLICATED], "grads_small_begin")
    d_pu, d_puc = _from_groups(s5_out[0]), _from_groups(s5_out[1]) + tok

    g_w_in_u = _mm(hc, d_puc, ta=True, name="proj_u_ctx_dw")
    g_w_in_u = _mm(h, d_pu, ta=True, add=g_w_in_u, out_dtype=BF16, name="proj_u_dw")
    g_w_in = jnp.concatenate([g_w_in_u, _mm(h, d_prest, ta=True, out_dtype=BF16, name="proj_rest_dw")], axis=1)
    sent_in, tok = _exchange_begin([grad_blocks('w_in', g_w_in)], "grads_in_begin")
    w_in_u_behind = w_in_u + _b16(tok)
    d_h = _mm(d_pu, w_in_u_behind, tb=True, name="proj_u_dx")
    d_h = _mm(d_prest, w_in_rest, tb=True, add=d_h, out_dtype=BF16, name="proj_rest_dx")
    d_hc = _mm(d_puc, w_in_u_behind, tb=True, out_dtype=BF16, name="proj_u_ctx_dx")

    (grad_x,), (g_g_mix_x, d_sh1, d_sc1) = _split(_stage_bwd(
        _fn_a_res, [xs], a_par, [d_h, d_xc], blk=c_blk, name="modulate1_bwd", row_grads=[(0, F32)]), 1)
    _, (g_g_mix_c, d_sh1c, d_sc1c) = _split(_stage_bwd(
        _fn_a, [cx], ac_par, [d_hc], blk=_pick(n_ctx, 512, 8), name="modulate1_ctx_bwd", row_grads=[]), 0)

    zeros = jnp.zeros((1, (n_mod - 2) * d), F32)
    d_mod = jnp.concatenate([d_sh1, d_sc1, d_ga1, d_sh2, d_sc2, d_ga2], axis=1)
    d_mod_c = jnp.concatenate([d_sh1c, d_sc1c, zeros], axis=1)
    (d_mod_all,) = _exchange([(jnp.concatenate([d_mod, d_mod_c], axis=0), "gather")], "gather_dmod")
    d_mod_rows = jnp.transpose(d_mod_all, (1, 0, 2)).reshape(2 * N_DEV, n_mod * d)
    d_mod_mine = lax.dynamic_slice_in_dim(d_mod_rows, me * mod_cols, mod_cols, axis=1)
    (d_cs,), (g_w_ada,) = _split(_stage_bwd(
        _fn_mod, [cs_in], [w_ada_loc], [d_mod_mine], blk=2 * N_DEV, name="mod_bwd", row_grads=[(0, F32)]), 1)

    part.update({'c_ctx': jnp.sum(d_cs[N_DEV:], axis=0), 'b_ada': d_mod + d_mod_c, 'g_mix': g_g_mix_x + g_g_mix_c})
    late_parts, loss_parts = _exchange(
        [(_pack([part[n] for n in LATE_REPLICATED]), "gather"), (jnp.broadcast_to(loss_part, (8, LANE)), "gather")],
        "exchange_grads")

    summed = {}
    small_parts = _exchange_end(sent_small, late_parts, "grads_small_end")
    early_parts = small_parts[0]
    for n, parts in zip(UNPACKED_REPLICATED, small_parts[1:]):
        summed[n], own_block[n] = parts, own_small[n]
    (summed['w_down'],) = _exchange_end(sent_down, late_parts, "grads_down_end")
    summed['w_up'], summed['conv_w'] = _exchange_end(sent_up, late_parts, "grads_up_end")
    summed['w_out'], summed['s5_w_glu'], summed['w_proj_a'], summed['w_proj_b'] = _exchange_end(
        sent_mix, late_parts, "grads_mixer_end")
    (summed['w_in'],) = _exchange_end(sent_in, late_parts, "grads_in_end")

    out = {}
    me_arr = me.reshape(1, 1).astype(jnp.int32)
    for names, parts, own, tag in ((early, early_parts, early_part, "early"),
                                   (LATE_REPLICATED, late_parts, None, "late")):
        res = _adamw(_pack([wts[n] for n in names]), parts, _pack([mom1[n] for n in names]),
                     _pack([mom2[n] for n in names]), "adamw_replicated_" + tag, own=own,
                     me=None if own is None else me_arr)
        res = [_unpack(r, [wts[n].shape for n in names]) for r in res]
        for i, n in enumerate(names):
            out[n] = tuple(r[i] for r in res)
    for n, parts in summed.items():
        shape = wts[n].shape
        res = _adamw(two_d(wts[n]), parts, two_d(mom1[n]), two_d(mom2[n]), "adamw_" + n, own=own_block[n],
                     me=me_arr)
        out[n] = tuple(r.reshape(shape) for r in res)
    res = _adamw(w_ada_loc, g_w_ada[None], m_w_ada[0], v_w_ada[0], "adamw_w_ada")
    out['w_ada'] = tuple(r.reshape(w_ada.shape) for r in res)

    loss = jnp.sum(loss_parts[:, 0, 0])
    return (loss, grad_x[None], *[out[n][0] for n in WEIGHTS], *[out[n][1] for n in WEIGHTS],
            *[out[n][2] for n in WEIGHTS], *[out[n][3] for n in WEIGHTS])


def _split(res, n_rows):
    return tuple(res[:n_rows]), tuple(res[n_rows:])
```

```python
import functools
import math

import jax
import jax.numpy as jnp
from jax import lax
from jax.experimental import pallas as pl
from jax.experimental.pallas import tpu as pltpu

F32 = jnp.float32
BF16 = jnp.bfloat16
HI = lax.Precision.HIGHEST

N_DEV = 8
GRID_W = 64
CHUNK = 128
EPS = 1e-6
S5_T = 32
S5_H = 16
S5_CHUNK_ALIGN = 16
LANE = 128
HALO = 128
VMEM_LIMIT = 56 * 1024 * 1024
PACK_ROWS = 256

ADAM_LR = 0.001
ADAM_B1 = 0.9
ADAM_B2 = 0.999
ADAM_EPS = 1e-08
ADAM_WD = 0.01
ADAM_STEP = 10

WEIGHTS = ['c_ctx', 'w_ada', 'b_ada', 'g_mix', 'w_in', 's5_a_re', 's5_a_im', 's5_log_step', 's5_b_re', 's5_b_im',
           's5_c_re', 's5_c_im', 's5_d', 's5_w_glu', 's5_b_glu', 'sgu_ln_g', 'sgu_ln_b', 'sgu_w', 'sgu_b',
           'w_proj_a', 'w_proj_b', 'b_gate', 'w_out', 'g_ffn', 'w_up', 'conv_w', 'conv_b', 'w_down', 'g_final']
COL_SHARDED = ('w_ada', 'w_in', 'w_proj_a', 'w_proj_b', 'w_up', 'conv_w')
ROW_SHARDED = ('s5_w_glu', 'w_out', 'w_down')
SHARDED = COL_SHARDED + ROW_SHARDED
REPLICATED = [n for n in WEIGHTS if n not in SHARDED]
LATE_REPLICATED = ['c_ctx', 'b_ada', 'g_mix']
UNPACKED_REPLICATED = ['sgu_w', 's5_c_re', 's5_c_im']


def _call(body, **kw):
    return pl.pallas_call(body, **kw)


def _params(n_grid):
    return pltpu.CompilerParams(dimension_semantics=("arbitrary",) * n_grid, vmem_limit_bytes=VMEM_LIMIT)


def _pick(dim, pref, unit=LANE):
    best = None
    d = unit
    while d <= min(dim, pref):
        if dim % d == 0:
            best = d
        d += unit
    return best if best is not None else dim


def _dg(a, b, ca, cb, prec=None):
    return lax.dot_general(a, b, (((ca,), (cb,)), ((), ())), precision=prec, preferred_element_type=F32)


def _b16(v):
    return v.astype(BF16)


@jax.custom_vjp
def mmb(a, b):
    return _dg(_b16(a), _b16(b), 1, 0)


def _mmb_fwd(a, b):
    return mmb(a, b), (a, b)


def _mmb_bwd(res, g):
    a, b = res
    g = _b16(g)
    return _dg(g, _b16(b), 1, 1).astype(a.dtype), _dg(_b16(a), g, 0, 0).astype(b.dtype)


mmb.defvjp(_mmb_fwd, _mmb_bwd)


@jax.custom_vjp
def mmb_nt(a, b):
    return _dg(_b16(a), _b16(b), 1, 1)


def _mmb_nt_fwd(a, b):
    return mmb_nt(a, b), (a, b)


def _mmb_nt_bwd(res, g):
    a, b = res
    g = _b16(g)
    return _dg(g, _b16(b), 1, 0).astype(a.dtype), _dg(g, _b16(a), 0, 0).astype(b.dtype)


mmb_nt.defvjp(_mmb_nt_fwd, _mmb_nt_bwd)


@jax.custom_vjp
def mmf(a, b):
    return _dg(a, b, 1, 0, HI)


def _mmf_fwd(a, b):
    return mmf(a, b), (a, b)


def _mmf_bwd(res, g):
    a, b = res
    return _dg(g, b, 1, 1, HI), _dg(a, g, 0, 0, HI)


mmf.defvjp(_mmf_fwd, _mmf_bwd)


def _dg3(a, b, ca, cb):
    ah, bh = _b16(a), _b16(b)
    al, bl = _b16(a - ah.astype(F32)), _b16(b - bh.astype(F32))
    return _dg(ah, bh, ca, cb) + _dg(ah, bl, ca, cb) + _dg(al, bh, ca, cb)


@jax.custom_vjp
def mm3_nt(a, b):
    return _dg3(a, b, 1, 1)


def _mm3_nt_fwd(a, b):
    return mm3_nt(a, b), (a, b)


def _mm3_nt_bwd(res, g):
    a, b = res
    return _dg3(g, b, 1, 0), _dg3(g, a, 0, 0)


mm3_nt.defvjp(_mm3_nt_fwd, _mm3_nt_bwd)


def _shift_impl(x, k, up):
    n = x.shape[0]
    idx = lax.broadcasted_iota(jnp.int32, (n, 1), 0)
    if up:
        return jnp.where(idx < n - k, pltpu.roll(x, n - k, 0), 0.0)
    return jnp.where(idx >= k, pltpu.roll(x, k, 0), 0.0)


@functools.partial(jax.custom_vjp, nondiff_argnums=(1, 2))
def _shift(x, k, up):
    return _shift_impl(x, k, up)


def _shift_fwd(x, k, up):
    return _shift_impl(x, k, up), None


def _shift_bwd(k, up, _, g):
    return (_shift_impl(g, k, not up),)


_shift.defvjp(_shift_fwd, _shift_bwd)


def _mm(a, b, *, name, ta=False, tb=False, add=None, out_dtype=F32, tm_pref=2048, tn_pref=1408, tk_pref=1408):
    m, k = (a.shape[1], a.shape[0]) if ta else a.shape
    n = b.shape[0] if tb else b.shape[1]
    if add is not None and out_dtype != F32:
        tm_pref = min(tm_pref, 1024)
    if ta:
        tk_pref = max(tk_pref, 2048)
    tm, tn, tk = _pick(m, tm_pref), _pick(n, tn_pref), _pick(k, tk_pref)
    nk = k // tk
    a_spec = (pl.BlockSpec((tk, tm), lambda i, j, kk: (kk, i)) if ta
              else pl.BlockSpec((tm, tk), lambda i, j, kk: (i, kk)))
    b_spec = (pl.BlockSpec((tn, tk), lambda i, j, kk: (j, kk)) if tb
              else pl.BlockSpec((tk, tn), lambda i, j, kk: (kk, j)))
    o_spec = pl.BlockSpec((tm, tn), lambda i, j, kk: (i, j))
    ca, cb = (0 if ta else 1), (1 if tb else 0)
    has_add = add is not None
    in_place = out_dtype == F32 or nk == 1

    def body(*refs):
        a_ref, b_ref = refs[0], refs[1]
        o_ref = refs[3] if has_add else refs[2]
        prod = _dg(_b16(a_ref[...]), _b16(b_ref[...]), ca, cb)
        if nk == 1:
            if has_add:
                prod = prod + refs[2][...].astype(F32)
            o_ref[...] = prod.astype(o_ref.dtype)
            return
        acc_ref = o_ref if in_place else refs[-1]
        kk = pl.program_id(2)

        @pl.when(kk == 0)
        def _():
            acc_ref[...] = prod

        @pl.when(kk > 0)
        def _():
            acc_ref[...] += prod

        if has_add or not in_place:
            @pl.when(kk == nk - 1)
            def _():
                r = acc_ref[...]
                if has_add:
                    r = r + refs[2][...].astype(F32)
                o_ref[...] = r.astype(o_ref.dtype)

    ins = [a, b] + ([add] if has_add else [])
    in_specs = [a_spec, b_spec] + ([o_spec] if has_add else [])
    return _call(body, name=name, grid=(m // tm, n // tn, nk), in_specs=in_specs, out_specs=o_spec,
                 out_shape=jax.ShapeDtypeStruct((m, n), out_dtype),
                 scratch_shapes=[] if in_place else [pltpu.VMEM((tm, tn), F32)],
                 compiler_params=_params(3))(*ins)


def _row_spec(blk, width):
    return pl.BlockSpec((blk, width), lambda i: (i, 0))


def _whole_spec(shape):
    return pl.BlockSpec(shape, lambda i: (0,) * len(shape))


def _stage_fwd(fn, rows, params, outs, *, blk, name, n_rows=None):
    n = n_rows or rows[0].shape[0]
    nr, npar = len(rows), len(params)

    def body(*refs):
        vals = [r[...] for r in refs[:nr + npar]]
        res = fn(*vals)
        for o_ref, v in zip(refs[nr + npar:], res):
            o_ref[...] = v.astype(o_ref.dtype)

    return _call(body, name=name, grid=(n // blk,),
                 in_specs=[_row_spec(blk, r.shape[1]) for r in rows] + [_whole_spec(p.shape) for p in params],
                 out_specs=[_row_spec(blk, w) for w, _ in outs],
                 out_shape=[jax.ShapeDtypeStruct((n, w), dt) for w, dt in outs],
                 compiler_params=_params(1))(*rows, *params)


def _stage_bwd(fn, rows, params, cts, *, blk, name, row_grads, n_rows=None):
    n = n_rows or rows[0].shape[0]
    nr, npar, nct = len(rows), len(params), len(cts)

    def body(*refs):
        vals = [r[...].astype(F32) for r in refs[:nr + npar]]
        ct = [r[...] for r in refs[nr + npar:nr + npar + nct]]
        d_rows = refs[nr + npar + nct:nr + npar + nct + len(row_grads)]
        d_par = refs[nr + npar + nct + len(row_grads):]
        res, vjp = jax.vjp(fn, *vals)
        g = vjp(tuple(c.astype(r.dtype) for c, r in zip(ct, res)))
        for o_ref, (j, _) in zip(d_rows, row_grads):
            o_ref[...] = g[j].astype(o_ref.dtype)

        @pl.when(pl.program_id(0) == 0)
        def _():
            for o_ref in d_par:
                o_ref[...] = jnp.zeros_like(o_ref)

        for j, o_ref in enumerate(d_par):
            o_ref[...] += g[nr + j].astype(F32)

    return _call(body, name=name, grid=(n // blk,),
                 in_specs=([_row_spec(blk, r.shape[1]) for r in rows] + [_whole_spec(p.shape) for p in params]
                           + [_row_spec(blk, c.shape[1]) for c in cts]),
                 out_specs=([_row_spec(blk, rows[j].shape[1]) for j, _ in row_grads]
                            + [_whole_spec(p.shape) for p in params]),
                 out_shape=([jax.ShapeDtypeStruct((n, rows[j].shape[1]), dt) for j, dt in row_grads]
                            + [jax.ShapeDtypeStruct(p.shape, F32) for p in params]),
                 compiler_params=_params(1))(*rows, *params, *cts)


def _stage_loss(fn, rows, params, *, blk, name, row_grads):
    n = rows[0].shape[0]
    nr, npar = len(rows), len(params)

    def body(*refs):
        vals = [r[...].astype(F32) for r in refs[:nr + npar]]
        loss_ref = refs[nr + npar]
        d_rows = refs[nr + npar + 1:nr + npar + 1 + len(row_grads)]
        d_par = refs[nr + npar + 1 + len(row_grads):]
        res, vjp = jax.vjp(fn, *vals)
        g = vjp(jnp.ones_like(res))
        for o_ref, (j, _) in zip(d_rows, row_grads):
            o_ref[...] = g[j].astype(o_ref.dtype)

        @pl.when(pl.program_id(0) == 0)
        def _():
            loss_ref[...] = jnp.zeros_like(loss_ref)
            for o_ref in d_par:
                o_ref[...] = jnp.zeros_like(o_ref)

        loss_ref[...] += res
        for j, o_ref in enumerate(d_par):
            o_ref[...] += g[nr + j].astype(F32)

    return _call(body, name=name, grid=(n // blk,),
                 in_specs=[_row_spec(blk, r.shape[1]) for r in rows] + [_whole_spec(p.shape) for p in params],
                 out_specs=([_whole_spec((1, 1))] + [_row_spec(blk, rows[j].shape[1]) for j, _ in row_grads]
                            + [_whole_spec(p.shape) for p in params]),
                 out_shape=([jax.ShapeDtypeStruct((1, 1), F32)]
                            + [jax.ShapeDtypeStruct((n, rows[j].shape[1]), dt) for j, dt in row_grads]
                            + [jax.ShapeDtypeStruct(p.shape, F32) for p in params]),
                 compiler_params=_params(1))(*rows, *params)


def _rms(x, g):
    return x * lax.rsqrt(jnp.mean(x * x, axis=-1, keepdims=True) + EPS) * g


def _modulate(x, g, shift, scale):
    return _rms(x, g) * (1.0 + scale) + shift


def _fn_mod(cs, w_ada):
    return (mmb(jax.nn.silu(cs), w_ada),)


def _fn_a(x, g_mix, sh, sc):
    return (_b16(_modulate(x, g_mix, sh, sc)),)


def _fn_a_res(x, g_mix, sh, sc):
    return _b16(_modulate(x, g_mix, sh, sc)), x


def _sgu_spatial(v, sgu_w, sgu_bt):
    rows, width = v.shape
    gdim = width // (sgu_w.shape[0] // CHUNK)
    groups = width // gdim
    expand = (lax.broadcasted_iota(jnp.int32, (groups, width), 1) // gdim
              == lax.broadcasted_iota(jnp.int32, (groups, width), 0)).astype(F32)
    bias = mmf(sgu_bt, expand)
    lane = lax.broadcasted_iota(jnp.int32, (CHUNK, LANE), 1)
    per_lane_block = LANE // gdim
    chunks = []
    for ci in range(rows // CHUNK):
        vc = v[ci * CHUNK:(ci + 1) * CHUNK]
        blocks = []
        for lb in range(width // LANE):
            vb = vc[:, lb * LANE:(lb + 1) * LANE]
            acc = None
            for s in range(per_lane_block):
                g = lb * per_lane_block + s
                r = mmb(sgu_w[g * CHUNK:(g + 1) * CHUNK], vb)
                sel = (lane // gdim) == s
                acc = jnp.where(sel, r, 0.0) if acc is None else jnp.where(sel, r, acc)
            blocks.append(acc)
        chunks.append(jnp.concatenate(blocks, axis=1) + bias)
    return jnp.concatenate(chunks, axis=0)


def _fn_b(pu, prest, ysc, s5_d, w_glu, b_glu, ln_g, ln_b, sgu_w, sgu_bt, w_pa, w_pb, b_gate):
    sw = ln_g.shape[1]
    y = jax.nn.gelu(pu * s5_d + ysc)
    ya = y * jax.nn.sigmoid(mmb(y, w_glu) + b_glu)
    z = jax.nn.gelu(prest[:, :2 * sw])
    u, v = z[:, :sw], z[:, sw:]
    vc = v - jnp.mean(v, axis=-1, keepdims=True)
    v = vc * lax.rsqrt(jnp.mean(vc * vc, axis=-1, keepdims=True) + EPS) * ln_g + ln_b
    yb = u * _sgu_spatial(v, sgu_w, sgu_bt)
    gates = jax.nn.sigmoid(prest[:, 2 * sw:] + b_gate)
    d = gates.shape[1] // 2
    return (_b16(gates[:, :d] * mmb(ya, w_pa) + gates[:, d:] * mmb(yb, w_pb)),)


def _fn_c(x, mo, ga1, g_ffn, sh2, sc2):
    x1 = x + ga1 * mo
    return x1, _b16(_modulate(x1, g_ffn, sh2, sc2))


def _fn_e(x1, dn, tgt, ga2, g_final):
    y = _rms(x1 + ga2 * dn, g_final)
    err = (y - tgt) ** 2
    return 0.5 * jnp.sum(jnp.mean(err, axis=-1, keepdims=True), axis=0, keepdims=True)


def _swap_impl(x):
    return pltpu.roll(x, x.shape[1] // 2, 1)


@jax.custom_vjp
def _swap_halves(x):
    return _swap_impl(x)


_swap_halves.defvjp(lambda x: (_swap_impl(x), None), lambda _, g: (_swap_impl(g),))


def _s5_direction(u, mask, a_re, a_im, log_step, bt, c, rev):
    nc, width = u.shape
    t_len = width // S5_H
    n2 = a_re.shape[1]
    lane = lax.broadcasted_iota(jnp.int32, (1, n2), 1)
    sign = jnp.where(lane < n2 // 2, -1.0, 1.0)
    dt = jnp.exp(log_step)
    lr, li = a_re * dt, a_im * dt
    mag = jnp.exp(lr)
    ab_re, ab_im = mag * jnp.cos(li), mag * jnp.sin(li)
    p, q = ab_re - 1.0, ab_im
    den = a_re * a_re + a_im * a_im
    k_re, k_im = (p * a_re + q * a_im) / den, (q * a_re - p * a_im) / den
    bb = k_re * bt + (k_im * sign) * _swap_halves(bt)

    def power(e):
        m = jnp.exp(lr * e)
        return m * jnp.cos(li * e), m * jnp.sin(li * e)

    order = range(t_len - 1, -1, -1) if rev else range(t_len)
    e1 = jnp.concatenate([jnp.full((1, 1, n2), float(t_len - 1 - pos), F32) for pos in order], axis=0)
    lr3, li3, sign3 = lr.reshape(1, 1, n2), li.reshape(1, 1, n2), sign.reshape(1, 1, n2)
    m1, c1, s1 = jnp.exp(lr3 * e1), jnp.cos(li3 * e1), jnp.sin(li3 * e1)
    m2 = jnp.exp(-(lr3 * e1))
    x1r, x1i = m1 * c1, m1 * s1
    x2r, x2i = m2 * c1, -(m2 * s1)
    at_r, at_i = [v.reshape(1, 1, n2) for v in power(float(t_len))]
    x3r, x3i = at_r * x2r - at_i * x2i, at_r * x2i + at_i * x2r

    def rows(xr, xi, z):
        z3, zs3 = z.reshape(1, S5_H, n2), _swap_halves(z).reshape(1, S5_H, n2)
        return (xr * z3 + (xi * sign3) * zs3).reshape(width, n2)

    p_in = rows(x1r, x1i, bb)
    r_out = rows(x2r, x2i, c)
    w_out = rows(x3r, x3i, c)
    toep = mm3_nt(p_in, r_out * (-sign)) * mask

    state = mmb(u, p_in)
    k = 1
    while k < nc:
        ar, ai = power(float(t_len * k))
        moved = _shift(state, k, rev)
        state = state + ar * moved + (ai * sign) * _swap_halves(moved)
        k *= 2
    entering = _shift(state, 1, rev)
    return mmb(u, toep) + mmb_nt(entering, w_out * (-sign))


def _fn_s5(masks, x_chunks, ctx_chunks, *prm):
    nx, nctx = x_chunks.shape[0], ctx_chunks.shape[0]
    pad = (-(nx + 2 * nctx)) % S5_CHUNK_ALIGN
    pieces = [ctx_chunks, x_chunks, ctx_chunks] + ([jnp.zeros((pad, x_chunks.shape[1]), F32)] if pad else [])
    u = jnp.concatenate(pieces, axis=0)
    out = None
    for d in range(2):
        y = _s5_direction(u, masks[d], *[p[d] for p in prm], rev=(d == 1))
        out = y if out is None else out + y
    return out[nctx:nctx + nx]


def _s5_specs(prm):
    return [pl.BlockSpec((2, 1) + p.shape[2:], lambda g: (0, g, 0, 0)) for p in prm]


def _group_spec(a):
    return pl.BlockSpec((1,) + a.shape[1:], lambda i: (i, 0, 0))


def _s5_masks(width):
    pos = jnp.arange(width) // S5_H
    causal = (pos[None, :] >= pos[:, None]).astype(F32)
    return jnp.stack([causal, causal.T])


def _s5_fwd(x_g, ctx_g, prm):
    masks = _s5_masks(x_g.shape[2])

    def body(*refs):
        pv = [r[:, 0] for r in refs[3:3 + len(prm)]]
        refs[-1][0] = _fn_s5(refs[0][...], refs[1][0].astype(F32), refs[2][0], *pv)

    return _call(body, name="s5_fwd", grid=(x_g.shape[0],),
                 in_specs=[_whole_spec(masks.shape), _group_spec(x_g), _group_spec(ctx_g)] + _s5_specs(prm),
                 out_specs=_group_spec(x_g), out_shape=jax.ShapeDtypeStruct(x_g.shape, F32),
                 compiler_params=_params(1))(masks, x_g, ctx_g, *prm)


def _s5_bwd(x_g, ctx_g, prm, dy_g, skip_g):
    npar = len(prm)
    masks = _s5_masks(x_g.shape[2])

    def body(*refs):
        pv = [r[:, 0] for r in refs[3:3 + npar]]
        dy = refs[3 + npar][0]
        _, vjp = jax.vjp(functools.partial(_fn_s5, refs[0][...]), refs[1][0].astype(F32), refs[2][0], *pv)
        grads = vjp(dy)
        outs = refs[5 + npar:]
        outs[0][0] = (grads[0] + dy * refs[4 + npar][0]).astype(outs[0].dtype)
        outs[1][0] = grads[1]
        for o_ref, gv in zip(outs[2:], grads[2:]):
            o_ref[:, 0] = gv

    return _call(body, name="s5_bwd", grid=(x_g.shape[0],),
                 in_specs=([_whole_spec(masks.shape), _group_spec(x_g), _group_spec(ctx_g)] + _s5_specs(prm)
                           + [_group_spec(dy_g), _group_spec(skip_g)]),
                 out_specs=[_group_spec(x_g), _group_spec(ctx_g)] + _s5_specs(prm),
                 out_shape=[jax.ShapeDtypeStruct(x_g.shape, BF16), jax.ShapeDtypeStruct(ctx_g.shape, F32)]
                 + [jax.ShapeDtypeStruct(p.shape, F32) for p in prm],
                 compiler_params=_params(1))(masks, x_g, ctx_g, *prm, dy_g, skip_g)


def _to_groups(tok):
    n, width = tok.shape
    g = width // S5_H
    return jnp.transpose(tok.reshape(n, g, S5_H), (1, 0, 2)).reshape(g, n // S5_T, S5_T * S5_H)


def _from_groups(grp):
    g, nc, _ = grp.shape
    return jnp.transpose(grp.reshape(g, nc * S5_T, S5_H), (1, 0, 2)).reshape(nc * S5_T, g * S5_H)


def _conv_shifted(xp, xm, xn, blk_i, n_blk):
    tb = xm.shape[0]
    xp = jnp.where(blk_i == 0, 0.0, xp.astype(F32))
    xn = jnp.where(blk_i == n_blk - 1, 0.0, xn.astype(F32))
    buf = jnp.concatenate([xp, xm.astype(F32), xn], axis=0)
    n = tb + 2 * HALO
    col = lax.broadcasted_iota(jnp.int32, (n, 1), 0) % GRID_W
    left = jnp.where(col >= 1, pltpu.roll(buf, 1, 0), 0.0)
    right = jnp.where(col <= GRID_W - 2, pltpu.roll(buf, n - 1, 0), 0.0)
    return left, buf, right


def _conv_taps(xp, xm, xn, blk_i, n_blk):
    tb = xm.shape[0]
    shifted = _conv_shifted(xp, xm, xn, blk_i, n_blk)
    taps = []
    for di in range(3):
        start = HALO + (di - 1) * GRID_W
        for dj in range(3):
            taps.append(shifted[dj][start:start + tb])
    return taps


def _conv_sum(taps, w_ref, flip=False):
    acc = None
    for k, tap in enumerate(taps):
        j = len(taps) - 1 - k if flip else k
        term = tap * w_ref[j:j + 1, :]
        acc = term if acc is None else acc + term
    return acc


def _conv_geometry(n_tok, width, tb_pref=1024):
    tb = _pick(n_tok, tb_pref, HALO)
    cb = _pick(width, 256)
    nb = tb // HALO
    last = n_tok // HALO - 1
    main = pl.BlockSpec((tb, cb), lambda j, i: (i, j))
    prev = pl.BlockSpec((HALO, cb), lambda j, i: (jnp.maximum(i * nb - 1, 0), j))
    nxt = pl.BlockSpec((HALO, cb), lambda j, i: (jnp.minimum(i * nb + nb, last), j))
    par = lambda r: pl.BlockSpec((r, cb), lambda j, i: (0, j))
    return tb, cb, main, prev, nxt, par


def _conv_act_fwd(up_g, up_v, w_g, w_v, b_g, b_v):
    n_tok, width = up_g.shape
    tb, cb, main, prev, nxt, par = _conv_geometry(n_tok, width, 2048)
    n_blk = n_tok // tb

    def body(gp, gm, gn, vp, vm, vn, wg, wv, bg, bv, o_ref, gate_ref, val_ref):
        i = pl.program_id(1)
        gate = _conv_sum(_conv_taps(gp[...], gm[...], gn[...], i, n_blk), wg) + bg[...]
        val = _conv_sum(_conv_taps(vp[...], vm[...], vn[...], i, n_blk), wv) + bv[...]
        o_ref[...] = (jax.nn.silu(gate) * val).astype(o_ref.dtype)
        gate_ref[...] = gate
        val_ref[...] = val

    shp = jax.ShapeDtypeStruct
    return _call(body, name="conv_act_fwd", grid=(width // cb, n_tok // tb),
                 in_specs=[prev, main, nxt, prev, main, nxt, par(9), par(9), par(1), par(1)],
                 out_specs=[main, main, main],
                 out_shape=[shp((n_tok, width), BF16), shp((n_tok, width), F32), shp((n_tok, width), F32)],
                 compiler_params=_params(2))(up_g, up_g, up_g, up_v, up_v, up_v, w_g, w_v, b_g, b_v)


def _conv_act_bwd(up_g, up_v, gate_c, val_c, d_act):
    n_tok, width = up_g.shape
    tb, cb, main, prev, nxt, par = _conv_geometry(n_tok, width)
    n_blk = n_tok // tb

    def body(gp, gm, gn, vp, vm, vn, gc, vc, da, dcg, dcv, dwg, dwv, dbg, dbv):
        i = pl.program_id(1)
        taps_g = _conv_taps(gp[...], gm[...], gn[...], i, n_blk)
        taps_v = _conv_taps(vp[...], vm[...], vn[...], i, n_blk)
        gate, val = gc[...], vc[...]
        sig = jax.nn.sigmoid(gate)
        d = da[...].astype(F32)
        d_gate = d * val * sig * (1.0 + gate * (1.0 - sig))
        d_val = d * gate * sig
        dcg[...] = d_gate.astype(dcg.dtype)
        dcv[...] = d_val.astype(dcv.dtype)

        @pl.when(i == 0)
        def _():
            for r in (dwg, dwv, dbg, dbv):
                r[...] = jnp.zeros_like(r)

        dbg[...] += jnp.sum(d_gate, axis=0, keepdims=True)
        dbv[...] += jnp.sum(d_val, axis=0, keepdims=True)
        for k in range(9):
            dwg[k:k + 1, :] += jnp.sum(taps_g[k] * d_gate, axis=0, keepdims=True)
            dwv[k:k + 1, :] += jnp.sum(taps_v[k] * d_val, axis=0, keepdims=True)

    shp = jax.ShapeDtypeStruct
    return _call(body, name="conv_act_bwd", grid=(width // cb, n_tok // tb),
                 in_specs=[prev, main, nxt, prev, main, nxt, main, main, main],
                 out_specs=[main, main, par(9), par(9), par(1), par(1)],
                 out_shape=[shp((n_tok, width), BF16), shp((n_tok, width), BF16), shp((9, width), F32),
                            shp((9, width), F32), shp((1, width), F32), shp((1, width), F32)],
                 compiler_params=_params(2))(up_g, up_g, up_g, up_v, up_v, up_v, gate_c, val_c, d_act)


def _conv_transposed(x, w, name):
    n_tok, width = x.shape
    tb, cb, main, prev, nxt, par = _conv_geometry(n_tok, width, 2048)
    n_blk = n_tok // tb

    def body(xp, xm, xn, w_ref, o_ref):
        taps = _conv_taps(xp[...], xm[...], xn[...], pl.program_id(1), n_blk)
        o_ref[...] = _conv_sum(taps, w_ref, flip=True).astype(o_ref.dtype)

    return _call(body, name=name, grid=(width // cb, n_tok // tb), in_specs=[prev, main, nxt, par(9)],
                 out_specs=main, out_shape=jax.ShapeDtypeStruct((n_tok, width), BF16),
                 compiler_params=_params(2))(x, x, x, w)


def _adamw(w, g_parts, m, v, name, own=None, me=None):
    rows, cols = w.shape
    parts = g_parts.shape[0]
    blk = _pick(rows, 256, 8)
    spec = pl.BlockSpec((blk, cols), lambda i: (i, 0))
    has_own = own is not None

    def body(*refs):
        w_ref, g_ref, m_ref, v_ref = refs[:4]
        g_out, d_out, m_out, v_out = refs[-4:]

        def part(p):
            if has_own:
                return jnp.where(refs[5][...] == p, refs[4][...], g_ref[p]).astype(F32)
            return g_ref[p].astype(F32)

        g = part(0)
        for p in range(1, parts):
            g = g + part(p)
        m_new = ADAM_B1 * m_ref[...] + (1.0 - ADAM_B1) * g
        v_new = ADAM_B2 * v_ref[...] + (1.0 - ADAM_B2) * (g * g)
        m_hat = m_new / (1.0 - ADAM_B1 ** ADAM_STEP)
        v_hat = v_new / (1.0 - ADAM_B2 ** ADAM_STEP)
        g_out[...] = g
        d_out[...] = -ADAM_LR * (m_hat / (jnp.sqrt(v_hat) + ADAM_EPS) + ADAM_WD * w_ref[...])
        m_out[...] = m_new
        v_out[...] = v_new

    extra = [own, me] if has_own else []
    return _call(body, name=name, grid=(rows // blk,),
                 in_specs=([spec, pl.BlockSpec((parts, blk, cols), lambda i: (0, i, 0)), spec, spec]
                           + ([spec, _whole_spec((1, 1))] if has_own else [])),
                 out_specs=[spec] * 4, out_shape=[jax.ShapeDtypeStruct((rows, cols), F32)] * 4,
                 compiler_params=_params(1))(w, g_parts, m, v, *extra)


def _exchange(items, name):
    n = len(items)
    hbm = pl.BlockSpec(memory_space=pl.ANY)

    def body(*refs):
        srcs, outs = refs[:n], refs[n:2 * n]
        send_sems, recv_sems, own_sems = refs[2 * n:]
        x, y, c = lax.axis_index("x"), lax.axis_index("y"), lax.axis_index("c")
        me = 4 * x + 2 * y + c
        own = []
        for i, (_, mode) in enumerate(items):
            src = srcs[i] if mode == "gather" else srcs[i].at[me]
            cp = pltpu.make_async_copy(src, outs[i].at[me], own_sems.at[i])
            cp.start()
            own.append(cp)
        sent = []
        for i, (_, mode) in enumerate(items):
            for k in range(1, N_DEV):
                px = 1 - x if k & 4 else x
                py = 1 - y if k & 2 else y
                pc = 1 - c if k & 1 else c
                peer = 4 * px + 2 * py + pc
                src = srcs[i] if mode == "gather" else srcs[i].at[peer]
                cp = pltpu.make_async_remote_copy(
                    src_ref=src, dst_ref=outs[i].at[me], send_sem=send_sems.at[i, k - 1],
                    recv_sem=recv_sems.at[i, k - 1], device_id=(px, py, pc), device_id_type=pl.DeviceIdType.MESH)
                cp.start()
                landing = pltpu.make_async_remote_copy(
                    src_ref=src, dst_ref=outs[i].at[peer], send_sem=send_sems.at[i, k - 1],
                    recv_sem=recv_sems.at[i, k - 1], device_id=(px, py, pc), device_id_type=pl.DeviceIdType.MESH)
                sent.append((cp, landing))
        for cp in own:
            cp.wait()
        for cp, landing in sent:
            cp.wait_send()
            landing.wait_recv()

    out_shape = [jax.ShapeDtypeStruct((N_DEV,) + (a.shape if mode == "gather" else a.shape[1:]), a.dtype)
                 for a, mode in items]
    return _call(body, name=name, in_specs=[hbm] * n, out_specs=[hbm] * n, out_shape=out_shape,
                 scratch_shapes=[pltpu.SemaphoreType.DMA((n, N_DEV - 1)), pltpu.SemaphoreType.DMA((n, N_DEV - 1)),
                                 pltpu.SemaphoreType.DMA((n,))])(*[a for a, _ in items])


def _peer(k, x, y, c):
    px = 1 - x if k & 4 else x
    py = 1 - y if k & 2 else y
    pc = 1 - c if k & 1 else c
    return (px, py, pc), 4 * px + 2 * py + pc


_HBM_SPEC = pl.BlockSpec(memory_space=pltpu.HBM)
_SEM_SPEC = pl.BlockSpec(memory_space=pltpu.SEMAPHORE)
_SPLIT_EFFECT = pltpu.SideEffectType.DATAFLOW_SIDE_EFFECTING


def _exchange_begin(items, name, after=None):
    n = len(items)
    modes = [mode for _, mode in items]
    srcs = [pltpu.with_memory_space_constraint(a, pltpu.HBM) for a, _ in items]
    lands = [pltpu.with_memory_space_constraint(
        lax.empty((N_DEV,) + (a.shape if mode == "gather" else a.shape[1:]), a.dtype), pltpu.HBM)
        for a, mode in items]

    def body(*refs):
        src_refs, land_refs = refs[:n], refs[n:2 * n]
        token = refs[-1]
        send_sems, recv_sems = refs[-2 * n - 3], refs[-2 * n - 2]
        x, y, c = lax.axis_index("x"), lax.axis_index("y"), lax.axis_index("c")
        me = 4 * x + 2 * y + c
        for i in range(n):
            for k in range(1, N_DEV):
                coords, peer = _peer(k, x, y, c)
                src = src_refs[i] if modes[i] == "gather" else src_refs[i].at[peer]
                pltpu.make_async_remote_copy(
                    src_ref=src, dst_ref=land_refs[i].at[me], send_sem=send_sems.at[i * (N_DEV - 1) + k - 1],
                    recv_sem=recv_sems.at[i * (N_DEV - 1) + k - 1], device_id=coords,
                    device_id_type=pl.DeviceIdType.MESH).start()
        token[...] = jnp.zeros_like(token)

    sems = pltpu.SemaphoreType.DMA((n * (N_DEV - 1),))
    order = [] if after is None else [after]
    res = _call(body, name=name,
                out_shape=(sems, sems, *[pltpu.HBM(a.shape, a.dtype) for a in srcs + lands],
                           jax.ShapeDtypeStruct((8, LANE), F32)),
                in_specs=[_HBM_SPEC] * (2 * n) + [pl.BlockSpec(memory_space=pl.ANY)] * len(order),
                out_specs=(_SEM_SPEC, _SEM_SPEC, *[_HBM_SPEC] * (2 * n), pl.BlockSpec(memory_space=pltpu.VMEM)),
                input_output_aliases={i: 2 + i for i in range(2 * n)},
                compiler_params=pltpu.CompilerParams(has_side_effects=_SPLIT_EFFECT))(*srcs, *lands, *order)
    return (modes, res[0], res[1], list(res[2:2 + n]), list(res[2 + n:2 + 2 * n])), res[-1][0, 0]


def _exchange_end(handle, after, name):
    modes, send_sems, recv_sems, srcs, lands = handle
    n = len(modes)

    def wait_body(*refs):
        src_refs, land_refs = refs[:n], refs[n:2 * n]
        send, recv = refs[2 * n], refs[2 * n + 1]
        x, y, c = lax.axis_index("x"), lax.axis_index("y"), lax.axis_index("c")
        for i in range(n):
            for k in range(1, N_DEV):
                coords, peer = _peer(k, x, y, c)
                src = src_refs[i] if modes[i] == "gather" else src_refs[i].at[peer]
                cp = pltpu.make_async_remote_copy(
                    src_ref=src, dst_ref=land_refs[i].at[peer], send_sem=send.at[i * (N_DEV - 1) + k - 1],
                    recv_sem=recv.at[i * (N_DEV - 1) + k - 1], device_id=coords,
                    device_id_type=pl.DeviceIdType.MESH)
                cp.wait_send()
                cp.wait_recv()

    res = _call(wait_body, name=name, out_shape=[pltpu.HBM(a.shape, a.dtype) for a in srcs + lands],
                in_specs=[_HBM_SPEC] * (2 * n) + [_SEM_SPEC, _SEM_SPEC, pl.BlockSpec(memory_space=pl.ANY)],
                out_specs=[_HBM_SPEC] * (2 * n), input_output_aliases={i: i for i in range(2 * n)},
                compiler_params=pltpu.CompilerParams(has_side_effects=_SPLIT_EFFECT))(
                    *srcs, *lands, send_sems, recv_sems, after)
    return res[n:]


def _with_own(land, own, me):
    slot = lax.broadcasted_iota(jnp.int32, (N_DEV,) + (1,) * (land.ndim - 1), 0)
    return jnp.where(slot == me, own[None], land)


def _cols_from_blocks(g):
    return jnp.transpose(g, (1, 0, 2)).reshape(g.shape[1], N_DEV * g.shape[2])


def _blocks_from_cols(w):
    r, c8 = w.shape
    return jnp.transpose(w.reshape(r, N_DEV, c8 // N_DEV), (1, 0, 2))


def _pack(arrs):
    flat = jnp.concatenate([a.reshape(-1).astype(F32) for a in arrs])
    rows = -(-flat.shape[0] // LANE)
    rows = -(-rows // PACK_ROWS) * PACK_ROWS if rows > PACK_ROWS else -(-rows // 8) * 8
    return jnp.pad(flat, (0, rows * LANE - flat.shape[0])).reshape(rows, LANE)


def _unpack(packed, shapes):
    flat = packed.reshape(-1)
    out, off = [], 0
    for s in shapes:
        size = math.prod(s)
        out.append(flat[off:off + size].reshape(s))
        off += size
    return out


def kernel(x, c, ctx, c_ctx, w_ada, b_ada, g_mix, w_in, s5_a_re, s5_a_im, s5_log_step, s5_b_re, s5_b_im, s5_c_re, s5_c_im, s5_d, s5_w_glu, s5_b_glu, sgu_ln_g, sgu_ln_b, sgu_w, sgu_b, w_proj_a, w_proj_b, b_gate, w_out, g_ffn, w_up, conv_w, conv_b, w_down, g_final, loss_target, m_c_ctx, m_w_ada, m_b_ada, m_g_mix, m_w_in, m_s5_a_re, m_s5_a_im, m_s5_log_step, m_s5_b_re, m_s5_b_im, m_s5_c_re, m_s5_c_im, m_s5_d, m_s5_w_glu, m_s5_b_glu, m_sgu_ln_g, m_sgu_ln_b, m_sgu_w, m_sgu_b, m_w_proj_a, m_w_proj_b, m_b_gate, m_w_out, m_g_ffn, m_w_up, m_conv_w, m_conv_b, m_w_down, m_g_final, v_c_ctx, v_w_ada, v_b_ada, v_g_mix, v_w_in, v_s5_a_re, v_s5_a_im, v_s5_log_step, v_s5_b_re, v_s5_b_im, v_s5_c_re, v_s5_c_im, v_s5_d, v_s5_w_glu, v_s5_b_glu, v_sgu_ln_g, v_sgu_ln_b, v_sgu_w, v_sgu_b, v_w_proj_a, v_w_proj_b, v_b_gate, v_w_out, v_g_ffn, v_w_up, v_conv_w, v_conv_b, v_w_down, v_g_final):
    given = dict(locals())
    wts = {n: given[n] for n in WEIGHTS}
    mom1 = {n: given["m_" + n] for n in WEIGHTS}
    mom2 = {n: given["v_" + n] for n in WEIGHTS}

    me = 4 * lax.axis_index("x") + 2 * lax.axis_index("y") + lax.axis_index("c")
    xs, cx, tgt = x[0], ctx[0], loss_target[0]
    n_tok, d = xs.shape
    n_ctx = cx.shape[0]
    s5w = s5_d.shape[1]
    ffn = w_down.shape[1] * N_DEV
    n_mod = w_ada.shape[2] * N_DEV // d
    mod_cols = w_ada.shape[2]

    def two_d(a):
        return a.reshape(-1, a.shape[-1])

    conv_w9 = conv_w[0].reshape(9, -1)
    gathered = _exchange([(c, "gather"), (_b16(w_in[0]), "gather")], "gather_weights")
    c_all = gathered[0].reshape(N_DEV, d)
    w_in_f = _cols_from_blocks(gathered[1])
    w_in_u, w_in_rest = w_in_f[:, :s5w], w_in_f[:, s5w:]

    cs_in = jnp.concatenate([c_all, jnp.broadcast_to(c_ctx[None, :], (N_DEV, d))], axis=0)
    w_ada_loc = w_ada[0]
    (mod_mine,) = _stage_fwd(_fn_mod, [cs_in], [w_ada_loc], [(mod_cols, F32)], blk=2 * N_DEV, name="mod_fwd")
    (mod_blocks,) = _exchange([(mod_mine, "gather")], "gather_mod")
    mid_own = [_b16(s5_w_glu[0]), _b16(w_proj_a[0]), _b16(w_proj_b[0])]
    mid_weights, tok_mid = _exchange_begin([(a, "gather") for a in mid_own], "gather_mid_begin", after=mod_blocks)
    late_own = [_b16(w_out[0]), _b16(w_up[0]), conv_w9, _b16(w_down[0])]
    late_weights, tok = _exchange_begin([(a, "gather") for a in late_own], "gather_late_begin", after=mod_blocks)
    mod_all = _cols_from_blocks(mod_blocks) + b_ada + (tok + tok_mid)
    mod = lax.dynamic_slice_in_dim(mod_all, me, 1, axis=0)
    mod_c = mod_all[N_DEV:N_DEV + 1]
    sh1, sc1, ga1, sh2, sc2, ga2 = [mod[:, i * d:(i + 1) * d] for i in range(n_mod)]
    sh1c, sc1c = mod_c[:, :d], mod_c[:, d:2 * d]

    a_par = [g_mix, sh1, sc1]
    ac_par = [g_mix, sh1c, sc1c]
    (h,) = _stage_fwd(_fn_a, [xs], a_par, [(d, BF16)], blk=_pick(n_tok, 512, 8), name="modulate1_fwd")
    (hc,) = _stage_fwd(_fn_a, [cx], ac_par, [(d, BF16)], blk=_pick(n_ctx, 512, 8), name="modulate1_ctx_fwd")
    pu = _mm(h, w_in_u, name="proj_u")
    prest = _mm(h, w_in_rest, name="proj_rest")
    puc = _mm(hc, w_in_u, name="proj_u_ctx")

    n_state = s5_a_re.shape[-1]

    def twice(a):
        return jnp.concatenate([a, a], axis=-1)

    s5_prm = [twice(s5_a_re[0])[:, :, None, :], twice(s5_a_im[0])[:, :, None, :], s5_log_step[0][:, :, None, None],
              jnp.concatenate([jnp.swapaxes(s5_b_re[0], 2, 3), jnp.swapaxes(s5_b_im[0], 2, 3)], axis=-1),
              jnp.concatenate([s5_c_re[0], s5_c_im[0]], axis=-1)]
    pu_g, puc_g = _to_groups(_b16(pu)), _to_groups(puc)
    ysc = _from_groups(_s5_fwd(pu_g, puc_g, s5_prm))

    mid = [_with_own(land, own, me) for land, own in zip(_exchange_end(mid_weights, ysc, "gather_mid_end"), mid_own)]
    w_glu_f, w_pa_f, w_pb_f = mid[0].reshape(-1, s5w), _cols_from_blocks(mid[1]), _cols_from_blocks(mid[2])
    b_par = [s5_d, w_glu_f, s5_b_glu, sgu_ln_g, sgu_ln_b, two_d(sgu_w[0]), jnp.transpose(sgu_b[0]),
             w_pa_f, w_pb_f, b_gate]
    b_rows = [pu, prest, ysc]
    b_blk = _pick(n_tok, 512, CHUNK)
    (mpre,) = _stage_fwd(_fn_b, b_rows, b_par, [(d, BF16)], blk=b_blk, name="mixers_fwd")
    late = [_with_own(land, own, me) for land, own in zip(_exchange_end(late_weights, mpre, "gather_late_end"),
                                                          late_own)]
    w_out_f = late[0].reshape(-1, d)
    w_up_f = _cols_from_blocks(late[1])
    w_up_g, w_up_v = w_up_f[:, :ffn], w_up_f[:, ffn:]
    conv_w_f = _cols_from_blocks(late[2])
    w_down_f = late[3].reshape(-1, d)
    mo = _mm(mpre, w_out_f, name="out_proj")

    c_par = [ga1, g_ffn, sh2, sc2]
    c_blk = _pick(n_tok, 512, 8)
    x1, h2 = _stage_fwd(_fn_c, [xs, mo], c_par, [(d, F32), (d, BF16)], blk=c_blk, name="modulate2_fwd")
    up_g = _mm(h2, w_up_g, out_dtype=BF16, name="up_gate")
    up_v = _mm(h2, w_up_v, out_dtype=BF16, name="up_val")
    cw_g, cw_v = conv_w_f[:, :ffn], conv_w_f[:, ffn:]
    cb_g, cb_v = conv_b[:, :ffn], conv_b[:, ffn:]
    act, gate_c, val_c = _conv_act_fwd(up_g, up_v, cw_g, cw_v, cb_g, cb_v)
    dn = _mm(act, w_down_f, name="down_proj")

    loss_part, d_x1a, d_dn, d_ga2, d_g_final = _stage_loss(
        _fn_e, [x1, dn, tgt], [ga2, g_final[None, :]], blk=c_blk, name="loss_head",
        row_grads=[(0, F32), (1, BF16)])

    d_act = _mm(d_dn, w_down_f, tb=True, name="down_proj_dx")
    g_w_down = _mm(act, d_dn, ta=True, out_dtype=BF16, name="down_proj_dw")

    own_block = {}

    def grad_blocks(name, g):
        blocks = _b16(_blocks_from_cols(g) if name in COL_SHARDED
                      else g.reshape(N_DEV, g.shape[0] // N_DEV, g.shape[1]))
        own_block[name] = lax.dynamic_index_in_dim(blocks, me, 0, keepdims=False)
        return blocks, "a2a"

    sent_down, tok = _exchange_begin([grad_blocks('w_down', g_w_down)], "grads_down_begin")
    dcg, dcv, g_cw_g, g_cw_v, g_cb_g, g_cb_v = _conv_act_bwd(up_g, up_v, gate_c, val_c, d_act)
    dug = _conv_transposed(dcg, cw_g + tok, "conv_dx_gate")
    duv = _conv_transposed(dcv, cw_v, "conv_dx_val")
    d_h2 = _mm(dug, w_up_g, tb=True, name="up_gate_dx")
    d_h2 = _mm(duv, w_up_v, tb=True, add=d_h2, out_dtype=BF16, name="up_val_dx")
    g_w_up = jnp.concatenate([_mm(h2, dug, ta=True, out_dtype=BF16, name="up_gate_dw"),
                              _mm(h2, duv, ta=True, out_dtype=BF16, name="up_val_dw")], axis=1)
    sent_up, tok = _exchange_begin(
        [grad_blocks('w_up', g_w_up), grad_blocks('conv_w', jnp.concatenate([g_cw_g, g_cw_v], axis=1))],
        "grads_up_begin")
    (d_xc, d_mo), (d_ga1, g_g_ffn, d_sh2, d_sc2) = _split(_stage_bwd(
        _fn_c, [xs, mo], [ga1 + tok] + c_par[1:], [d_x1a, d_h2], blk=c_blk, name="modulate2_bwd",
        row_grads=[(0, F32), (1, BF16)]), 2)

    d_mpre = _mm(d_mo, w_out_f, tb=True, out_dtype=BF16, name="out_proj_dx")
    g_w_out = _mm(mpre, d_mo, ta=True, out_dtype=BF16, name="out_proj_dw")
    (d_prest, d_ysc), b_grads = _split(_stage_bwd(
        _fn_b, b_rows, b_par, [d_mpre], blk=_pick(n_tok, 256, CHUNK), name="mixers_bwd",
        row_grads=[(1, BF16), (2, F32)]), 2)
    (g_s5_d, g_w_glu, g_b_glu, g_ln_g, g_ln_b, g_sgu_w, g_sgu_bt, g_w_pa, g_w_pb, g_b_gate) = b_grads

    sent_mix, tok = _exchange_begin(
        [grad_blocks('w_out', g_w_out), grad_blocks('s5_w_glu', g_w_glu), grad_blocks('w_proj_a', g_w_pa),
         grad_blocks('w_proj_b', g_w_pb)], "grads_mixer_begin")
    skip_g = jnp.tile(s5_d.reshape(-1, 1, S5_H), (1, 1, S5_T)) + tok
    s5_out = _s5_bwd(pu_g, puc_g, s5_prm, _to_groups(d_ysc), skip_g)
    g_a_re2, g_a_im2, g_ls, g_bt2, g_c2 = s5_out[2:]
    g_a_re = g_a_re2[..., :n_state] + g_a_re2[..., n_state:]
    g_a_im = g_a_im2[..., :n_state] + g_a_im2[..., n_state:]
    g_bt_re, g_bt_im = g_bt2[..., :n_state], g_bt2[..., n_state:]
    g_c_re, g_c_im = g_c2[..., :n_state], g_c2[..., n_state:]

    part = {
        's5_a_re': g_a_re, 's5_a_im': g_a_im, 's5_log_step': g_ls,
        's5_b_re': jnp.swapaxes(g_bt_re, 2, 3), 's5_b_im': jnp.swapaxes(g_bt_im, 2, 3),
        's5_c_re': g_c_re, 's5_c_im': g_c_im, 's5_d': g_s5_d, 's5_b_glu': g_b_glu, 'sgu_ln_g': g_ln_g,
        'sgu_ln_b': g_ln_b, 'sgu_w': g_sgu_w, 'sgu_b': jnp.transpose(g_sgu_bt), 'b_gate': g_b_gate,
        'g_ffn': g_g_ffn, 'conv_b': jnp.concatenate([g_cb_g, g_cb_v], axis=1), 'g_final': d_g_final,
    }
    early = [n for n in REPLICATED if n not in LATE_REPLICATED and n not in UNPACKED_REPLICATED]
    early_part = _pack([part[n] for n in early])
    own_small = {n: two_d(part[n]) for n in UNPACKED_REPLICATED}
    sent_small, tok = _exchange_begin(
        [(early_part, "gather")] + [(own_small[n], "gather") for n in UNPACKED_REPLICATED], "grads_small_begin")
    d_pu, d_puc = _from_groups(s5_out[0]), _from_groups(s5_out[1]) + tok

    g_w_in_u = _mm(hc, d_puc, ta=True, name="proj_u_ctx_dw")
    g_w_in_u = _mm(h, d_pu, ta=True, add=g_w_in_u, out_dtype=BF16, name="proj_u_dw")
    g_w_in = jnp.concatenate([g_w_in_u, _mm(h, d_prest, ta=True, out_dtype=BF16, name="proj_rest_dw")], axis=1)
    sent_in, tok = _exchange_begin([grad_blocks('w_in', g_w_in)], "grads_in_begin")
    w_in_u_behind = w_in_u + _b16(tok)
    d_h = _mm(d_pu, w_in_u_behind, tb=True, name="proj_u_dx")
    d_h = _mm(d_prest, w_in_rest, tb=True, add=d_h, out_dtype=BF16, name="proj_rest_dx")
    d_hc = _mm(d_puc, w_in_u_behind, tb=True, out_dtype=BF16, name="proj_u_ctx_dx")

    (grad_x,), (g_g_mix_x, d_sh1, d_sc1) = _split(_stage_bwd(
        _fn_a_res, [xs], a_par, [d_h, d_xc], blk=c_blk, name="modulate1_bwd", row_grads=[(0, F32)]), 1)
    _, (g_g_mix_c, d_sh1c, d_sc1c) = _split(_stage_bwd(
        _fn_a, [cx], ac_par, [d_hc], blk=_pick(n_ctx, 512, 8), name="modulate1_ctx_bwd", row_grads=[]), 0)

    zeros = jnp.zeros((1, (n_mod - 2) * d), F32)
    d_mod = jnp.concatenate([d_sh1, d_sc1, d_ga1, d_sh2, d_sc2, d_ga2], axis=1)
    d_mod_c = jnp.concatenate([d_sh1c, d_sc1c, zeros], axis=1)
    (d_mod_all,) = _exchange([(jnp.concatenate([d_mod, d_mod_c], axis=0), "gather")], "gather_dmod")
    d_mod_rows = jnp.transpose(d_mod_all, (1, 0, 2)).reshape(2 * N_DEV, n_mod * d)
    d_mod_mine = lax.dynamic_slice_in_dim(d_mod_rows, me * mod_cols, mod_cols, axis=1)
    (d_cs,), (g_w_ada,) = _split(_stage_bwd(
        _fn_mod, [cs_in], [w_ada_loc], [d_mod_mine], blk=2 * N_DEV, name="mod_bwd", row_grads=[(0, F32)]), 1)

    part.update({'c_ctx': jnp.sum(d_cs[N_DEV:], axis=0), 'b_ada': d_mod + d_mod_c, 'g_mix': g_g_mix_x + g_g_mix_c})
    late_parts, loss_parts = _exchange(
        [(_pack([part[n] for n in LATE_REPLICATED]), "gather"), (jnp.broadcast_to(loss_part, (8, LANE)), "gather")],
        "exchange_grads")

    summed = {}
    small_parts = _exchange_end(sent_small, late_parts, "grads_small_end")
    early_parts = small_parts[0]
    for n, parts in zip(UNPACKED_REPLICATED, small_parts[1:]):
        summed[n], own_block[n] = parts, own_small[n]
    (summed['w_down'],) = _exchange_end(sent_down, late_parts, "grads_down_end")
    summed['w_up'], summed['conv_w'] = _exchange_end(sent_up, late_parts, "grads_up_end")
    summed['w_out'], summed['s5_w_glu'], summed['w_proj_a'], summed['w_proj_b'] = _exchange_end(
        sent_mix, late_parts, "grads_mixer_end")
    (summed['w_in'],) = _exchange_end(sent_in, late_parts, "grads_in_end")

    out = {}
    me_arr = me.reshape(1, 1).astype(jnp.int32)
    for names, parts, own, tag in ((early, early_parts, early_part, "early"),
                                   (LATE_REPLICATED, late_parts, None, "late")):
        res = _adamw(_pack([wts[n] for n in names]), parts, _pack([mom1[n] for n in names]),
                     _pack([mom2[n] for n in names]), "adamw_replicated_" + tag, own=own,
                     me=None if own is None else me_arr)
        res = [_unpack(r, [wts[n].shape for n in names]) for r in res]
        for i, n in enumerate(names):
            out[n] = tuple(r[i] for r in res)
    for n, parts in summed.items():
        shape = wts[n].shape
        res = _adamw(two_d(wts[n]), parts, two_d(mom1[n]), two_d(mom2[n]), "adamw_" + n, own=own_block[n],
                     me=me_arr)
        out[n] = tuple(r.reshape(shape) for r in res)
    res = _adamw(w_ada_loc, g_w_ada[None], m_w_ada[0], v_w_ada[0], "adamw_w_ada")
    out['w_ada'] = tuple(r.reshape(w_ada.shape) for r in res)

    loss = jnp.sum(loss_parts[:, 0, 0])
    return (loss, grad_x[None], *[out[n][0] for n in WEIGHTS], *[out[n][1] for n in WEIGHTS],
            *[out[n][2] for n in WEIGHTS], *[out[n][3] for n in WEIGHTS])


def _split(res, n_rows):
    return tuple(res[:n_rows]), tuple(res[n_rows:])
```

```python
import functools
import math

import jax
import jax.numpy as jnp
from jax import lax
from jax.experimental import pallas as pl
from jax.experimental.pallas import tpu as pltpu

F32 = jnp.float32
BF16 = jnp.bfloat16
HI = lax.Precision.HIGHEST

N_DEV = 8
GRID_W = 64
CHUNK = 128
EPS = 1e-6
S5_T = 32
S5_H = 16
S5_CHUNK_ALIGN = 16
LANE = 128
HALO = 128
VMEM_LIMIT = 56 * 1024 * 1024
PACK_ROWS = 256

ADAM_LR = 0.001
ADAM_B1 = 0.9
ADAM_B2 = 0.999
ADAM_EPS = 1e-08
ADAM_WD = 0.01
ADAM_STEP = 10

WEIGHTS = ['c_ctx', 'w_ada', 'b_ada', 'g_mix', 'w_in', 's5_a_re', 's5_a_im', 's5_log_step', 's5_b_re', 's5_b_im',
           's5_c_re', 's5_c_im', 's5_d', 's5_w_glu', 's5_b_glu', 'sgu_ln_g', 'sgu_ln_b', 'sgu_w', 'sgu_b',
           'w_proj_a', 'w_proj_b', 'b_gate', 'w_out', 'g_ffn', 'w_up', 'conv_w', 'conv_b', 'w_down', 'g_final']
COL_SHARDED = ('w_ada', 'w_in', 'w_proj_a', 'w_proj_b', 'w_up', 'conv_w')
ROW_SHARDED = ('s5_w_glu', 'w_out', 'w_down')
SHARDED = COL_SHARDED + ROW_SHARDED
REPLICATED = [n for n in WEIGHTS if n not in SHARDED]
LATE_REPLICATED = ['c_ctx', 'b_ada', 'g_mix']
UNPACKED_REPLICATED = ['sgu_w', 's5_c_re', 's5_c_im']


def _call(body, **kw):
    return pl.pallas_call(body, **kw)


def _params(n_grid):
    return pltpu.CompilerParams(dimension_semantics=("arbitrary",) * n_grid, vmem_limit_bytes=VMEM_LIMIT)


def _pick(dim, pref, unit=LANE):
    best = None
    d = unit
    while d <= min(dim, pref):
        if dim % d == 0:
            best = d
        d += unit
    return best if best is not None else dim


def _dg(a, b, ca, cb, prec=None):
    return lax.dot_general(a, b, (((ca,), (cb,)), ((), ())), precision=prec, preferred_element_type=F32)


def _b16(v):
    return v.astype(BF16)


@jax.custom_vjp
def mmb(a, b):
    return _dg(_b16(a), _b16(b), 1, 0)


def _mmb_fwd(a, b):
    return mmb(a, b), (a, b)


def _mmb_bwd(res, g):
    a, b = res
    g = _b16(g)
    return _dg(g, _b16(b), 1, 1).astype(a.dtype), _dg(_b16(a), g, 0, 0).astype(b.dtype)


mmb.defvjp(_mmb_fwd, _mmb_bwd)


@jax.custom_vjp
def mmb_nt(a, b):
    return _dg(_b16(a), _b16(b), 1, 1)


def _mmb_nt_fwd(a, b):
    return mmb_nt(a, b), (a, b)


def _mmb_nt_bwd(res, g):
    a, b = res
    g = _b16(g)
    return _dg(g, _b16(b), 1, 0).astype(a.dtype), _dg(g, _b16(a), 0, 0).astype(b.dtype)


mmb_nt.defvjp(_mmb_nt_fwd, _mmb_nt_bwd)


@jax.custom_vjp
def mmf(a, b):
    return _dg(a, b, 1, 0, HI)


def _mmf_fwd(a, b):
    return mmf(a, b), (a, b)


def _mmf_bwd(res, g):
    a, b = res
    return _dg(g, b, 1, 1, HI), _dg(a, g, 0, 0, HI)


mmf.defvjp(_mmf_fwd, _mmf_bwd)


def _dg3(a, b, ca, cb):
    ah, bh = _b16(a), _b16(b)
    al, bl = _b16(a - ah.astype(F32)), _b16(b - bh.astype(F32))
    return _dg(ah, bh, ca, cb) + _dg(ah, bl, ca, cb) + _dg(al, bh, ca, cb)


@jax.custom_vjp
def mm3_nt(a, b):
    return _dg3(a, b, 1, 1)


def _mm3_nt_fwd(a, b):
    return mm3_nt(a, b), (a, b)


def _mm3_nt_bwd(res, g):
    a, b = res
    return _dg3(g, b, 1, 0), _dg3(g, a, 0, 0)


mm3_nt.defvjp(_mm3_nt_fwd, _mm3_nt_bwd)


def _shift_impl(x, k, up):
    n = x.shape[0]
    idx = lax.broadcasted_iota(jnp.int32, (n, 1), 0)
    if up:
        return jnp.where(idx < n - k, pltpu.roll(x, n - k, 0), 0.0)
    return jnp.where(idx >= k, pltpu.roll(x, k, 0), 0.0)


@functools.partial(jax.custom_vjp, nondiff_argnums=(1, 2))
def _shift(x, k, up):
    return _shift_impl(x, k, up)


def _shift_fwd(x, k, up):
    return _shift_impl(x, k, up), None


def _shift_bwd(k, up, _, g):
    return (_shift_impl(g, k, not up),)


_shift.defvjp(_shift_fwd, _shift_bwd)


def _mm(a, b, *, name, ta=False, tb=False, add=None, out_dtype=F32, tm_pref=2048, tn_pref=1408, tk_pref=1408):
    m, k = (a.shape[1], a.shape[0]) if ta else a.shape
    n = b.shape[0] if tb else b.shape[1]
    if add is not None and out_dtype != F32:
        tm_pref = min(tm_pref, 1024)
    if ta:
        tk_pref = max(tk_pref, 2048)
    tm, tn, tk = _pick(m, tm_pref), _pick(n, tn_pref), _pick(k, tk_pref)
    nk = k // tk
    a_spec = (pl.BlockSpec((tk, tm), lambda i, j, kk: (kk, i)) if ta
              else pl.BlockSpec((tm, tk), lambda i, j, kk: (i, kk)))
    b_spec = (pl.BlockSpec((tn, tk), lambda i, j, kk: (j, kk)) if tb
              else pl.BlockSpec((tk, tn), lambda i, j, kk: (kk, j)))
    o_spec = pl.BlockSpec((tm, tn), lambda i, j, kk: (i, j))
    ca, cb = (0 if ta else 1), (1 if tb else 0)
    has_add = add is not None
    in_place = out_dtype == F32 or nk == 1

    def body(*refs):
        a_ref, b_ref = refs[0], refs[1]
        o_ref = refs[3] if has_add else refs[2]
        prod = _dg(_b16(a_ref[...]), _b16(b_ref[...]), ca, cb)
        if nk == 1:
            if has_add:
                prod = prod + refs[2][...].astype(F32)
            o_ref[...] = prod.astype(o_ref.dtype)
            return
        acc_ref = o_ref if in_place else refs[-1]
        kk = pl.program_id(2)

        @pl.when(kk == 0)
        def _():
            acc_ref[...] = prod

        @pl.when(kk > 0)
        def _():
            acc_ref[...] += prod

        if has_add or not in_place:
            @pl.when(kk == nk - 1)
            def _():
                r = acc_ref[...]
                if has_add:
                    r = r + refs[2][...].astype(F32)
                o_ref[...] = r.astype(o_ref.dtype)

    ins = [a, b] + ([add] if has_add else [])
    in_specs = [a_spec, b_spec] + ([o_spec] if has_add else [])
    return _call(body, name=name, grid=(m // tm, n // tn, nk), in_specs=in_specs, out_specs=o_spec,
                 out_shape=jax.ShapeDtypeStruct((m, n), out_dtype),
                 scratch_shapes=[] if in_place else [pltpu.VMEM((tm, tn), F32)],
                 compiler_params=_params(3))(*ins)


def _row_spec(blk, width):
    return pl.BlockSpec((blk, width), lambda i: (i, 0))


def _whole_spec(shape):
    return pl.BlockSpec(shape, lambda i: (0,) * len(shape))


def _stage_fwd(fn, rows, params, outs, *, blk, name, n_rows=None):
    n = n_rows or rows[0].shape[0]
    nr, npar = len(rows), len(params)

    def body(*refs):
        vals = [r[...] for r in refs[:nr + npar]]
        res = fn(*vals)
        for o_ref, v in zip(refs[nr + npar:], res):
            o_ref[...] = v.astype(o_ref.dtype)

    return _call(body, name=name, grid=(n // blk,),
                 in_specs=[_row_spec(blk, r.shape[1]) for r in rows] + [_whole_spec(p.shape) for p in params],
                 out_specs=[_row_spec(blk, w) for w, _ in outs],
                 out_shape=[jax.ShapeDtypeStruct((n, w), dt) for w, dt in outs],
                 compiler_params=_params(1))(*rows, *params)


def _stage_bwd(fn, rows, params, cts, *, blk, name, row_grads, n_rows=None):
    n = n_rows or rows[0].shape[0]
    nr, npar, nct = len(rows), len(params), len(cts)

    def body(*refs):
        vals = [r[...].astype(F32) for r in refs[:nr + npar]]
        ct = [r[...] for r in refs[nr + npar:nr + npar + nct]]
        d_rows = refs[nr + npar + nct:nr + npar + nct + len(row_grads)]
        d_par = refs[nr + npar + nct + len(row_grads):]
        res, vjp = jax.vjp(fn, *vals)
        g = vjp(tuple(c.astype(r.dtype) for c, r in zip(ct, res)))
        for o_ref, (j, _) in zip(d_rows, row_grads):
            o_ref[...] = g[j].astype(o_ref.dtype)

        @pl.when(pl.program_id(0) == 0)
        def _():
            for o_ref in d_par:
                o_ref[...] = jnp.zeros_like(o_ref)

        for j, o_ref in enumerate(d_par):
            o_ref[...] += g[nr + j].astype(F32)

    return _call(body, name=name, grid=(n // blk,),
                 in_specs=([_row_spec(blk, r.shape[1]) for r in rows] + [_whole_spec(p.shape) for p in params]
                           + [_row_spec(blk, c.shape[1]) for c in cts]),
                 out_specs=([_row_spec(blk, rows[j].shape[1]) for j, _ in row_grads]
                            + [_whole_spec(p.shape) for p in params]),
                 out_shape=([jax.ShapeDtypeStruct((n, rows[j].shape[1]), dt) for j, dt in row_grads]
                            + [jax.ShapeDtypeStruct(p.shape, F32) for p in params]),
                 compiler_params=_params(1))(*rows, *params, *cts)


def _stage_loss(fn, rows, params, *, blk, name, row_grads):
    n = rows[0].shape[0]
    nr, npar = len(rows), len(params)

    def body(*refs):
        vals = [r[...].astype(F32) for r in refs[:nr + npar]]
        loss_ref = refs[nr + npar]
        d_rows = refs[nr + npar + 1:nr + npar + 1 + len(row_grads)]
        d_par = refs[nr + npar + 1 + len(row_grads):]
        res, vjp = jax.vjp(fn, *vals)
        g = vjp(jnp.ones_like(res))
        for o_ref, (j, _) in zip(d_rows, row_grads):
            o_ref[...] = g[j].astype(o_ref.dtype)

        @pl.when(pl.program_id(0) == 0)
        def _():
            loss_ref[...] = jnp.zeros_like(loss_ref)
            for o_ref in d_par:
                o_ref[...] = jnp.zeros_like(o_ref)

        loss_ref[...] += res
        for j, o_ref in enumerate(d_par):
            o_ref[...] += g[nr + j].astype(F32)

    return _call(body, name=name, grid=(n // blk,),
                 in_specs=[_row_spec(blk, r.shape[1]) for r in rows] + [_whole_spec(p.shape) for p in params],
                 out_specs=([_whole_spec((1, 1))] + [_row_spec(blk, rows[j].shape[1]) for j, _ in row_grads]
                            + [_whole_spec(p.shape) for p in params]),
                 out_shape=([jax.ShapeDtypeStruct((1, 1), F32)]
                            + [jax.ShapeDtypeStruct((n, rows[j].shape[1]), dt) for j, dt in row_grads]
                            + [jax.ShapeDtypeStruct(p.shape, F32) for p in params]),
                 compiler_params=_params(1))(*rows, *params)


def _rms(x, g):
    return x * lax.rsqrt(jnp.mean(x * x, axis=-1, keepdims=True) + EPS) * g


def _modulate(x, g, shift, scale):
    return _rms(x, g) * (1.0 + scale) + shift


def _fn_mod(cs, w_ada):
    return (mmb(jax.nn.silu(cs), w_ada),)


def _fn_a(x, g_mix, sh, sc):
    return (_b16(_modulate(x, g_mix, sh, sc)),)


def _fn_a_res(x, g_mix, sh, sc):
    return _b16(_modulate(x, g_mix, sh, sc)), x


def _sgu_spatial(v, sgu_w, sgu_bt):
    rows, width = v.shape
    gdim = width // (sgu_w.shape[0] // CHUNK)
    groups = width // gdim
    expand = (lax.broadcasted_iota(jnp.int32, (groups, width), 1) // gdim
              == lax.broadcasted_iota(jnp.int32, (groups, width), 0)).astype(F32)
    bias = mmf(sgu_bt, expand)
    lane = lax.broadcasted_iota(jnp.int32, (CHUNK, LANE), 1)
    per_lane_block = LANE // gdim
    chunks = []
    for ci in range(rows // CHUNK):
        vc = v[ci * CHUNK:(ci + 1) * CHUNK]
        blocks = []
        for lb in range(width // LANE):
            vb = vc[:, lb * LANE:(lb + 1) * LANE]
            acc = None
            for s in range(per_lane_block):
                g = lb * per_lane_block + s
                r = mmb(sgu_w[g * CHUNK:(g + 1) * CHUNK], vb)
                sel = (lane // gdim) == s
                acc = jnp.where(sel, r, 0.0) if acc is None else jnp.where(sel, r, acc)
            blocks.append(acc)
        chunks.append(jnp.concatenate(blocks, axis=1) + bias)
    return jnp.concatenate(chunks, axis=0)


def _fn_b(pu, prest, ysc, s5_d, w_glu, b_glu, ln_g, ln_b, sgu_w, sgu_bt, w_pa, w_pb, b_gate):
    sw = ln_g.shape[1]
    y = jax.nn.gelu(pu * s5_d + ysc)
    ya = y * jax.nn.sigmoid(mmb(y, w_glu) + b_glu)
    z = jax.nn.gelu(prest[:, :2 * sw])
    u, v = z[:, :sw], z[:, sw:]
    vc = v - jnp.mean(v, axis=-1, keepdims=True)
    v = vc * lax.rsqrt(jnp.mean(vc * vc, axis=-1, keepdims=True) + EPS) * ln_g + ln_b
    yb = u * _sgu_spatial(v, sgu_w, sgu_bt)
    gates = jax.nn.sigmoid(prest[:, 2 * sw:] + b_gate)
    d = gates.shape[1] // 2
    return (_b16(gates[:, :d] * mmb(ya, w_pa) + gates[:, d:] * mmb(yb, w_pb)),)


def _fn_c(x, mo, ga1, g_ffn, sh2, sc2):
    x1 = x + ga1 * mo
    return x1, _b16(_modulate(x1, g_ffn, sh2, sc2))


def _fn_e(x1, dn, tgt, ga2, g_final):
    y = _rms(x1 + ga2 * dn, g_final)
    err = (y - tgt) ** 2
    return 0.5 * jnp.sum(jnp.mean(err, axis=-1, keepdims=True), axis=0, keepdims=True)


def _swap_impl(x):
    return pltpu.roll(x, x.shape[1] // 2, 1)


@jax.custom_vjp
def _swap_halves(x):
    return _swap_impl(x)


_swap_halves.defvjp(lambda x: (_swap_impl(x), None), lambda _, g: (_swap_impl(g),))


def _s5_direction(u, mask, a_re, a_im, log_step, bt, c, rev):
    nc, width = u.shape
    t_len = width // S5_H
    n2 = a_re.shape[1]
    lane = lax.broadcasted_iota(jnp.int32, (1, n2), 1)
    sign = jnp.where(lane < n2 // 2, -1.0, 1.0)
    dt = jnp.exp(log_step)
    lr, li = a_re * dt, a_im * dt
    mag = jnp.exp(lr)
    ab_re, ab_im = mag * jnp.cos(li), mag * jnp.sin(li)
    p, q = ab_re - 1.0, ab_im
    den = a_re * a_re + a_im * a_im
    k_re, k_im = (p * a_re + q * a_im) / den, (q * a_re - p * a_im) / den
    bb = k_re * bt + (k_im * sign) * _swap_halves(bt)

    def power(e):
        m = jnp.exp(lr * e)
        return m * jnp.cos(li * e), m * jnp.sin(li * e)

    order = range(t_len - 1, -1, -1) if rev else range(t_len)
    e1 = jnp.concatenate([jnp.full((1, 1, n2), float(t_len - 1 - pos), F32) for pos in order], axis=0)
    lr3, li3, sign3 = lr.reshape(1, 1, n2), li.reshape(1, 1, n2), sign.reshape(1, 1, n2)
    m1, c1, s1 = jnp.exp(lr3 * e1), jnp.cos(li3 * e1), jnp.sin(li3 * e1)
    m2 = jnp.exp(-(lr3 * e1))
    x1r, x1i = m1 * c1, m1 * s1
    x2r, x2i = m2 * c1, -(m2 * s1)
    at_r, at_i = [v.reshape(1, 1, n2) for v in power(float(t_len))]
    x3r, x3i = at_r * x2r - at_i * x2i, at_r * x2i + at_i * x2r

    def rows(xr, xi, z):
        z3, zs3 = z.reshape(1, S5_H, n2), _swap_halves(z).reshape(1, S5_H, n2)
        return (xr * z3 + (xi * sign3) * zs3).reshape(width, n2)

    p_in = rows(x1r, x1i, bb)
    r_out = rows(x2r, x2i, c)
    w_out = rows(x3r, x3i, c)
    toep = mm3_nt(p_in, r_out * (-sign)) * mask

    state = mmb(u, p_in)
    k = 1
    while k < nc:
        ar, ai = power(float(t_len * k))
        moved = _shift(state, k, rev)
        state = state + ar * moved + (ai * sign) * _swap_halves(moved)
        k *= 2
    entering = _shift(state, 1, rev)
    return mmb(u, toep) + mmb_nt(entering, w_out * (-sign))


def _fn_s5(masks, x_chunks, ctx_chunks, *prm):
    nx, nctx = x_chunks.shape[0], ctx_chunks.shape[0]
    pad = (-(nx + 2 * nctx)) % S5_CHUNK_ALIGN
    pieces = [ctx_chunks, x_chunks, ctx_chunks] + ([jnp.zeros((pad, x_chunks.shape[1]), F32)] if pad else [])
    u = jnp.concatenate(pieces, axis=0)
    out = None
    for d in range(2):
        y = _s5_direction(u, masks[d], *[p[d] for p in prm], rev=(d == 1))
        out = y if out is None else out + y
    return out[nctx:nctx + nx]


S5_GROUPS_PER_STEP = 2


def _s5_specs(prm):
    return [pl.BlockSpec((2, S5_GROUPS_PER_STEP) + p.shape[2:], lambda g: (0, g, 0, 0)) for p in prm]


def _group_spec(a):
    return pl.BlockSpec((S5_GROUPS_PER_STEP,) + a.shape[1:], lambda i: (i, 0, 0))


def _s5_masks(width):
    pos = jnp.arange(width) // S5_H
    causal = (pos[None, :] >= pos[:, None]).astype(F32)
    return jnp.stack([causal, causal.T])


def _s5_fwd(x_g, ctx_g, prm):
    masks = _s5_masks(x_g.shape[2])

    def body(*refs):
        for j in range(S5_GROUPS_PER_STEP):
            pv = [r[:, j] for r in refs[3:3 + len(prm)]]
            refs[-1][j] = _fn_s5(refs[0][...], refs[1][j].astype(F32), refs[2][j], *pv)

    return _call(body, name="s5_fwd", grid=(x_g.shape[0] // S5_GROUPS_PER_STEP,),
                 in_specs=[_whole_spec(masks.shape), _group_spec(x_g), _group_spec(ctx_g)] + _s5_specs(prm),
                 out_specs=_group_spec(x_g), out_shape=jax.ShapeDtypeStruct(x_g.shape, F32),
                 compiler_params=_params(1))(masks, x_g, ctx_g, *prm)


def _s5_bwd(x_g, ctx_g, prm, dy_g, skip_g):
    npar = len(prm)
    masks = _s5_masks(x_g.shape[2])

    def body(*refs):
        outs = refs[5 + npar:]
        for j in range(S5_GROUPS_PER_STEP):
            pv = [r[:, j] for r in refs[3:3 + npar]]
            dy = refs[3 + npar][j]
            _, vjp = jax.vjp(functools.partial(_fn_s5, refs[0][...]), refs[1][j].astype(F32), refs[2][j], *pv)
            grads = vjp(dy)
            outs[0][j] = (grads[0] + dy * refs[4 + npar][j]).astype(outs[0].dtype)
            outs[1][j] = grads[1]
            for o_ref, gv in zip(outs[2:], grads[2:]):
                o_ref[:, j] = gv

    return _call(body, name="s5_bwd", grid=(x_g.shape[0] // S5_GROUPS_PER_STEP,),
                 in_specs=([_whole_spec(masks.shape), _group_spec(x_g), _group_spec(ctx_g)] + _s5_specs(prm)
                           + [_group_spec(dy_g), _group_spec(skip_g)]),
                 out_specs=[_group_spec(x_g), _group_spec(ctx_g)] + _s5_specs(prm),
                 out_shape=[jax.ShapeDtypeStruct(x_g.shape, BF16), jax.ShapeDtypeStruct(ctx_g.shape, F32)]
                 + [jax.ShapeDtypeStruct(p.shape, F32) for p in prm],
                 compiler_params=_params(1))(masks, x_g, ctx_g, *prm, dy_g, skip_g)


def _to_groups(tok):
    n, width = tok.shape
    g = width // S5_H
    return jnp.transpose(tok.reshape(n, g, S5_H), (1, 0, 2)).reshape(g, n // S5_T, S5_T * S5_H)


def _from_groups(grp):
    g, nc, _ = grp.shape
    return jnp.transpose(grp.reshape(g, nc * S5_T, S5_H), (1, 0, 2)).reshape(nc * S5_T, g * S5_H)


def _conv_shifted(xp, xm, xn, blk_i, n_blk):
    tb = xm.shape[0]
    xp = jnp.where(blk_i == 0, 0.0, xp.astype(F32))
    xn = jnp.where(blk_i == n_blk - 1, 0.0, xn.astype(F32))
    buf = jnp.concatenate([xp, xm.astype(F32), xn], axis=0)
    n = tb + 2 * HALO
    col = lax.broadcasted_iota(jnp.int32, (n, 1), 0) % GRID_W
    left = jnp.where(col >= 1, pltpu.roll(buf, 1, 0), 0.0)
    right = jnp.where(col <= GRID_W - 2, pltpu.roll(buf, n - 1, 0), 0.0)
    return left, buf, right


def _conv_taps(xp, xm, xn, blk_i, n_blk):
    tb = xm.shape[0]
    shifted = _conv_shifted(xp, xm, xn, blk_i, n_blk)
    taps = []
    for di in range(3):
        start = HALO + (di - 1) * GRID_W
        for dj in range(3):
            taps.append(shifted[dj][start:start + tb])
    return taps


def _conv_sum(taps, w_ref, flip=False):
    acc = None
    for k, tap in enumerate(taps):
        j = len(taps) - 1 - k if flip else k
        term = tap * w_ref[j:j + 1, :]
        acc = term if acc is None else acc + term
    return acc


def _conv_geometry(n_tok, width, tb_pref=1024):
    tb = _pick(n_tok, tb_pref, HALO)
    cb = _pick(width, 256)
    nb = tb // HALO
    last = n_tok // HALO - 1
    main = pl.BlockSpec((tb, cb), lambda j, i: (i, j))
    prev = pl.BlockSpec((HALO, cb), lambda j, i: (jnp.maximum(i * nb - 1, 0), j))
    nxt = pl.BlockSpec((HALO, cb), lambda j, i: (jnp.minimum(i * nb + nb, last), j))
    par = lambda r: pl.BlockSpec((r, cb), lambda j, i: (0, j))
    return tb, cb, main, prev, nxt, par


def _conv_act_fwd(up_g, up_v, w_g, w_v, b_g, b_v):
    n_tok, width = up_g.shape
    tb, cb, main, prev, nxt, par = _conv_geometry(n_tok, width, 2048)
    n_blk = n_tok // tb

    def body(gp, gm, gn, vp, vm, vn, wg, wv, bg, bv, o_ref, gate_ref, val_ref):
        i = pl.program_id(1)
        gate = _conv_sum(_conv_taps(gp[...], gm[...], gn[...], i, n_blk), wg) + bg[...]
        val = _conv_sum(_conv_taps(vp[...], vm[...], vn[...], i, n_blk), wv) + bv[...]
        o_ref[...] = (jax.nn.silu(gate) * val).astype(o_ref.dtype)
        gate_ref[...] = gate
        val_ref[...] = val

    shp = jax.ShapeDtypeStruct
    return _call(body, name="conv_act_fwd", grid=(width // cb, n_tok // tb),
                 in_specs=[prev, main, nxt, prev, main, nxt, par(9), par(9), par(1), par(1)],
                 out_specs=[main, main, main],
                 out_shape=[shp((n_tok, width), BF16), shp((n_tok, width), F32), shp((n_tok, width), F32)],
                 compiler_params=_params(2))(up_g, up_g, up_g, up_v, up_v, up_v, w_g, w_v, b_g, b_v)


def _conv_act_bwd(up_g, up_v, gate_c, val_c, d_act):
    n_tok, width = up_g.shape
    tb, cb, main, prev, nxt, par = _conv_geometry(n_tok, width)
    n_blk = n_tok // tb

    def body(gp, gm, gn, vp, vm, vn, gc, vc, da, dcg, dcv, dwg, dwv, dbg, dbv):
        i = pl.program_id(1)
        taps_g = _conv_taps(gp[...], gm[...], gn[...], i, n_blk)
        taps_v = _conv_taps(vp[...], vm[...], vn[...], i, n_blk)
        gate, val = gc[...], vc[...]
        sig = jax.nn.sigmoid(gate)
        d = da[...].astype(F32)
        d_gate = d * val * sig * (1.0 + gate * (1.0 - sig))
        d_val = d * gate * sig
        dcg[...] = d_gate.astype(dcg.dtype)
        dcv[...] = d_val.astype(dcv.dtype)

        @pl.when(i == 0)
        def _():
            for r in (dwg, dwv, dbg, dbv):
                r[...] = jnp.zeros_like(r)

        dbg[...] += jnp.sum(d_gate, axis=0, keepdims=True)
        dbv[...] += jnp.sum(d_val, axis=0, keepdims=True)
        for k in range(9):
            dwg[k:k + 1, :] += jnp.sum(taps_g[k] * d_gate, axis=0, keepdims=True)
            dwv[k:k + 1, :] += jnp.sum(taps_v[k] * d_val, axis=0, keepdims=True)

    shp = jax.ShapeDtypeStruct
    return _call(body, name="conv_act_bwd", grid=(width // cb, n_tok // tb),
                 in_specs=[prev, main, nxt, prev, main, nxt, main, main, main],
                 out_specs=[main, main, par(9), par(9), par(1), par(1)],
                 out_shape=[shp((n_tok, width), BF16), shp((n_tok, width), BF16), shp((9, width), F32),
                            shp((9, width), F32), shp((1, width), F32), shp((1, width), F32)],
                 compiler_params=_params(2))(up_g, up_g, up_g, up_v, up_v, up_v, gate_c, val_c, d_act)


def _conv_transposed(x, w, name):
    n_tok, width = x.shape
    tb, cb, main, prev, nxt, par = _conv_geometry(n_tok, width, 2048)
    n_blk = n_tok // tb

    def body(xp, xm, xn, w_ref, o_ref):
        taps = _conv_taps(xp[...], xm[...], xn[...], pl.program_id(1), n_blk)
        o_ref[...] = _conv_sum(taps, w_ref, flip=True).astype(o_ref.dtype)

    return _call(body, name=name, grid=(width // cb, n_tok // tb), in_specs=[prev, main, nxt, par(9)],
                 out_specs=main, out_shape=jax.ShapeDtypeStruct((n_tok, width), BF16),
                 compiler_params=_params(2))(x, x, x, w)


def _adamw(w, g_parts, m, v, name, own=None, me=None):
    rows, cols = w.shape
    parts = g_parts.shape[0]
    blk = _pick(rows, 256, 8)
    spec = pl.BlockSpec((blk, cols), lambda i: (i, 0))
    has_own = own is not None

    def body(*refs):
        w_ref, g_ref, m_ref, v_ref = refs[:4]
        g_out, d_out, m_out, v_out = refs[-4:]

        def part(p):
            if has_own:
                return jnp.where(refs[5][...] == p, refs[4][...], g_ref[p]).astype(F32)
            return g_ref[p].astype(F32)

        g = part(0)
        for p in range(1, parts):
            g = g + part(p)
        m_new = ADAM_B1 * m_ref[...] + (1.0 - ADAM_B1) * g
        v_new = ADAM_B2 * v_ref[...] + (1.0 - ADAM_B2) * (g * g)
        m_hat = m_new / (1.0 - ADAM_B1 ** ADAM_STEP)
        v_hat = v_new / (1.0 - ADAM_B2 ** ADAM_STEP)
        g_out[...] = g
        d_out[...] = -ADAM_LR * (m_hat / (jnp.sqrt(v_hat) + ADAM_EPS) + ADAM_WD * w_ref[...])
        m_out[...] = m_new
        v_out[...] = v_new

    extra = [own, me] if has_own else []
    return _call(body, name=name, grid=(rows // blk,),
                 in_specs=([spec, pl.BlockSpec((parts, blk, cols), lambda i: (0, i, 0)), spec, spec]
                           + ([spec, _whole_spec((1, 1))] if has_own else [])),
                 out_specs=[spec] * 4, out_shape=[jax.ShapeDtypeStruct((rows, cols), F32)] * 4,
                 compiler_params=_params(1))(w, g_parts, m, v, *extra)


def _exchange(items, name):
    n = len(items)
    hbm = pl.BlockSpec(memory_space=pl.ANY)

    def body(*refs):
        srcs, outs = refs[:n], refs[n:2 * n]
        send_sems, recv_sems, own_sems = refs[2 * n:]
        x, y, c = lax.axis_index("x"), lax.axis_index("y"), lax.axis_index("c")
        me = 4 * x + 2 * y + c
        own = []
        for i, (_, mode) in enumerate(items):
            src = srcs[i] if mode == "gather" else srcs[i].at[me]
            cp = pltpu.make_async_copy(src, outs[i].at[me], own_sems.at[i])
            cp.start()
            own.append(cp)
        sent = []
        for i, (_, mode) in enumerate(items):
            for k in range(1, N_DEV):
                px = 1 - x if k & 4 else x
                py = 1 - y if k & 2 else y
                pc = 1 - c if k & 1 else c
                peer = 4 * px + 2 * py + pc
                src = srcs[i] if mode == "gather" else srcs[i].at[peer]
                cp = pltpu.make_async_remote_copy(
                    src_ref=src, dst_ref=outs[i].at[me], send_sem=send_sems.at[i, k - 1],
                    recv_sem=recv_sems.at[i, k - 1], device_id=(px, py, pc), device_id_type=pl.DeviceIdType.MESH)
                cp.start()
                landing = pltpu.make_async_remote_copy(
                    src_ref=src, dst_ref=outs[i].at[peer], send_sem=send_sems.at[i, k - 1],
                    recv_sem=recv_sems.at[i, k - 1], device_id=(px, py, pc), device_id_type=pl.DeviceIdType.MESH)
                sent.append((cp, landing))
        for cp in own:
            cp.wait()
        for cp, landing in sent:
            cp.wait_send()
            landing.wait_recv()

    out_shape = [jax.ShapeDtypeStruct((N_DEV,) + (a.shape if mode == "gather" else a.shape[1:]), a.dtype)
                 for a, mode in items]
    return _call(body, name=name, in_specs=[hbm] * n, out_specs=[hbm] * n, out_shape=out_shape,
                 scratch_shapes=[pltpu.SemaphoreType.DMA((n, N_DEV - 1)), pltpu.SemaphoreType.DMA((n, N_DEV - 1)),
                                 pltpu.SemaphoreType.DMA((n,))])(*[a for a, _ in items])


def _peer(k, x, y, c):
    px = 1 - x if k & 4 else x
    py = 1 - y if k & 2 else y
    pc = 1 - c if k & 1 else c
    return (px, py, pc), 4 * px + 2 * py + pc


_HBM_SPEC = pl.BlockSpec(memory_space=pltpu.HBM)
_SEM_SPEC = pl.BlockSpec(memory_space=pltpu.SEMAPHORE)
_SPLIT_EFFECT = pltpu.SideEffectType.DATAFLOW_SIDE_EFFECTING


def _exchange_begin(items, name, after=None):
    n = len(items)
    modes = [mode for _, mode in items]
    srcs = [pltpu.with_memory_space_constraint(a, pltpu.HBM) for a, _ in items]
    lands = [pltpu.with_memory_space_constraint(
        lax.empty((N_DEV,) + (a.shape if mode == "gather" else a.shape[1:]), a.dtype), pltpu.HBM)
        for a, mode in items]

    def body(*refs):
        src_refs, land_refs = refs[:n], refs[n:2 * n]
        token = refs[-1]
        send_sems, recv_sems = refs[-2 * n - 3], refs[-2 * n - 2]
        x, y, c = lax.axis_index("x"), lax.axis_index("y"), lax.axis_index("c")
        me = 4 * x + 2 * y + c
        for i in range(n):
            for k in range(1, N_DEV):
                coords, peer = _peer(k, x, y, c)
                src = src_refs[i] if modes[i] == "gather" else src_refs[i].at[peer]
                pltpu.make_async_remote_copy(
                    src_ref=src, dst_ref=land_refs[i].at[me], send_sem=send_sems.at[i * (N_DEV - 1) + k - 1],
                    recv_sem=recv_sems.at[i * (N_DEV - 1) + k - 1], device_id=coords,
                    device_id_type=pl.DeviceIdType.MESH).start()
        token[...] = jnp.zeros_like(token)

    sems = pltpu.SemaphoreType.DMA((n * (N_DEV - 1),))
    order = [] if after is None else [after]
    res = _call(body, name=name,
                out_shape=(sems, sems, *[pltpu.HBM(a.shape, a.dtype) for a in srcs + lands],
                           jax.ShapeDtypeStruct((8, LANE), F32)),
                in_specs=[_HBM_SPEC] * (2 * n) + [pl.BlockSpec(memory_space=pl.ANY)] * len(order),
                out_specs=(_SEM_SPEC, _SEM_SPEC, *[_HBM_SPEC] * (2 * n), pl.BlockSpec(memory_space=pltpu.VMEM)),
                input_output_aliases={i: 2 + i for i in range(2 * n)},
                compiler_params=pltpu.CompilerParams(has_side_effects=_SPLIT_EFFECT))(*srcs, *lands, *order)
    return (modes, res[0], res[1], list(res[2:2 + n]), list(res[2 + n:2 + 2 * n])), res[-1][0, 0]


def _exchange_end(handle, after, name):
    modes, send_sems, recv_sems, srcs, lands = handle
    n = len(modes)

    def wait_body(*refs):
        src_refs, land_refs = refs[:n], refs[n:2 * n]
        send, recv = refs[2 * n], refs[2 * n + 1]
        x, y, c = lax.axis_index("x"), lax.axis_index("y"), lax.axis_index("c")
        for i in range(n):
            for k in range(1, N_DEV):
                coords, peer = _peer(k, x, y, c)
                src = src_refs[i] if modes[i] == "gather" else src_refs[i].at[peer]
                cp = pltpu.make_async_remote_copy(
                    src_ref=src, dst_ref=land_refs[i].at[peer], send_sem=send.at[i * (N_DEV - 1) + k - 1],
                    recv_sem=recv.at[i * (N_DEV - 1) + k - 1], device_id=coords,
                    device_id_type=pl.DeviceIdType.MESH)
                cp.wait_send()
                cp.wait_recv()

    res = _call(wait_body, name=name, out_shape=[pltpu.HBM(a.shape, a.dtype) for a in srcs + lands],
                in_specs=[_HBM_SPEC] * (2 * n) + [_SEM_SPEC, _SEM_SPEC, pl.BlockSpec(memory_space=pl.ANY)],
                out_specs=[_HBM_SPEC] * (2 * n), input_output_aliases={i: i for i in range(2 * n)},
                compiler_params=pltpu.CompilerParams(has_side_effects=_SPLIT_EFFECT))(
                    *srcs, *lands, send_sems, recv_sems, after)
    return res[n:]


def _with_own(land, own, me):
    slot = lax.broadcasted_iota(jnp.int32, (N_DEV,) + (1,) * (land.ndim - 1), 0)
    return jnp.where(slot == me, own[None], land)


def _cols_from_blocks(g):
    return jnp.transpose(g, (1, 0, 2)).reshape(g.shape[1], N_DEV * g.shape[2])


def _blocks_from_cols(w):
    r, c8 = w.shape
    return jnp.transpose(w.reshape(r, N_DEV, c8 // N_DEV), (1, 0, 2))


def _pack(arrs):
    flat = jnp.concatenate([a.reshape(-1).astype(F32) for a in arrs])
    rows = -(-flat.shape[0] // LANE)
    rows = -(-rows // PACK_ROWS) * PACK_ROWS if rows > PACK_ROWS else -(-rows // 8) * 8
    return jnp.pad(flat, (0, rows * LANE - flat.shape[0])).reshape(rows, LANE)


def _unpack(packed, shapes):
    flat = packed.reshape(-1)
    out, off = [], 0
    for s in shapes:
        size = math.prod(s)
        out.append(flat[off:off + size].reshape(s))
        off += size
    return out


def kernel(x, c, ctx, c_ctx, w_ada, b_ada, g_mix, w_in, s5_a_re, s5_a_im, s5_log_step, s5_b_re, s5_b_im, s5_c_re, s5_c_im, s5_d, s5_w_glu, s5_b_glu, sgu_ln_g, sgu_ln_b, sgu_w, sgu_b, w_proj_a, w_proj_b, b_gate, w_out, g_ffn, w_up, conv_w, conv_b, w_down, g_final, loss_target, m_c_ctx, m_w_ada, m_b_ada, m_g_mix, m_w_in, m_s5_a_re, m_s5_a_im, m_s5_log_step, m_s5_b_re, m_s5_b_im, m_s5_c_re, m_s5_c_im, m_s5_d, m_s5_w_glu, m_s5_b_glu, m_sgu_ln_g, m_sgu_ln_b, m_sgu_w, m_sgu_b, m_w_proj_a, m_w_proj_b, m_b_gate, m_w_out, m_g_ffn, m_w_up, m_conv_w, m_conv_b, m_w_down, m_g_final, v_c_ctx, v_w_ada, v_b_ada, v_g_mix, v_w_in, v_s5_a_re, v_s5_a_im, v_s5_log_step, v_s5_b_re, v_s5_b_im, v_s5_c_re, v_s5_c_im, v_s5_d, v_s5_w_glu, v_s5_b_glu, v_sgu_ln_g, v_sgu_ln_b, v_sgu_w, v_sgu_b, v_w_proj_a, v_w_proj_b, v_b_gate, v_w_out, v_g_ffn, v_w_up, v_conv_w, v_conv_b, v_w_down, v_g_final):
    given = dict(locals())
    wts = {n: given[n] for n in WEIGHTS}
    mom1 = {n: given["m_" + n] for n in WEIGHTS}
    mom2 = {n: given["v_" + n] for n in WEIGHTS}

    me = 4 * lax.axis_index("x") + 2 * lax.axis_index("y") + lax.axis_index("c")
    xs, cx, tgt = x[0], ctx[0], loss_target[0]
    n_tok, d = xs.shape
    n_ctx = cx.shape[0]
    s5w = s5_d.shape[1]
    ffn = w_down.shape[1] * N_DEV
    n_mod = w_ada.shape[2] * N_DEV // d
    mod_cols = w_ada.shape[2]

    def two_d(a):
        return a.reshape(-1, a.shape[-1])

    conv_w9 = conv_w[0].reshape(9, -1)
    gathered = _exchange([(c, "gather"), (_b16(w_in[0]), "gather")], "gather_weights")
    c_all = gathered[0].reshape(N_DEV, d)
    w_in_f = _cols_from_blocks(gathered[1])
    w_in_u, w_in_rest = w_in_f[:, :s5w], w_in_f[:, s5w:]

    cs_in = jnp.concatenate([c_all, jnp.broadcast_to(c_ctx[None, :], (N_DEV, d))], axis=0)
    w_ada_loc = w_ada[0]
    (mod_mine,) = _stage_fwd(_fn_mod, [cs_in], [w_ada_loc], [(mod_cols, F32)], blk=2 * N_DEV, name="mod_fwd")
    (mod_blocks,) = _exchange([(mod_mine, "gather")], "gather_mod")
    mid_own = [_b16(s5_w_glu[0]), _b16(w_proj_a[0]), _b16(w_proj_b[0])]
    mid_weights, tok_mid = _exchange_begin([(a, "gather") for a in mid_own], "gather_mid_begin", after=mod_blocks)
    late_own = [_b16(w_out[0]), _b16(w_up[0]), conv_w9, _b16(w_down[0])]
    late_weights, tok = _exchange_begin([(a, "gather") for a in late_own], "gather_late_begin", after=mod_blocks)
    mod_all = _cols_from_blocks(mod_blocks) + b_ada + (tok + tok_mid)
    mod = lax.dynamic_slice_in_dim(mod_all, me, 1, axis=0)
    mod_c = mod_all[N_DEV:N_DEV + 1]
    sh1, sc1, ga1, sh2, sc2, ga2 = [mod[:, i * d:(i + 1) * d] for i in range(n_mod)]
    sh1c, sc1c = mod_c[:, :d], mod_c[:, d:2 * d]

    a_par = [g_mix, sh1, sc1]
    ac_par = [g_mix, sh1c, sc1c]
    (h,) = _stage_fwd(_fn_a, [xs], a_par, [(d, BF16)], blk=_pick(n_tok, 512, 8), name="modulate1_fwd")
    (hc,) = _stage_fwd(_fn_a, [cx], ac_par, [(d, BF16)], blk=_pick(n_ctx, 512, 8), name="modulate1_ctx_fwd")
    pu = _mm(h, w_in_u, name="proj_u")
    prest = _mm(h, w_in_rest, name="proj_rest")
    puc = _mm(hc, w_in_u, name="proj_u_ctx")

    n_state = s5_a_re.shape[-1]

    def twice(a):
        return jnp.concatenate([a, a], axis=-1)

    s5_prm = [twice(s5_a_re[0])[:, :, None, :], twice(s5_a_im[0])[:, :, None, :], s5_log_step[0][:, :, None, None],
              jnp.concatenate([jnp.swapaxes(s5_b_re[0], 2, 3), jnp.swapaxes(s5_b_im[0], 2, 3)], axis=-1),
              jnp.concatenate([s5_c_re[0], s5_c_im[0]], axis=-1)]
    pu_g, puc_g = _to_groups(_b16(pu)), _to_groups(puc)
    ysc = _from_groups(_s5_fwd(pu_g, puc_g, s5_prm))

    mid = [_with_own(land, own, me) for land, own in zip(_exchange_end(mid_weights, ysc, "gather_mid_end"), mid_own)]
    w_glu_f, w_pa_f, w_pb_f = mid[0].reshape(-1, s5w), _cols_from_blocks(mid[1]), _cols_from_blocks(mid[2])
    b_par = [s5_d, w_glu_f, s5_b_glu, sgu_ln_g, sgu_ln_b, two_d(sgu_w[0]), jnp.transpose(sgu_b[0]),
             w_pa_f, w_pb_f, b_gate]
    b_rows = [pu, prest, ysc]
    b_blk = _pick(n_tok, 512, CHUNK)
    (mpre,) = _stage_fwd(_fn_b, b_rows, b_par, [(d, BF16)], blk=b_blk, name="mixers_fwd")
    late = [_with_own(land, own, me) for land, own in zip(_exchange_end(late_weights, mpre, "gather_late_end"),
                                                          late_own)]
    w_out_f = late[0].reshape(-1, d)
    w_up_f = _cols_from_blocks(late[1])
    w_up_g, w_up_v = w_up_f[:, :ffn], w_up_f[:, ffn:]
    conv_w_f = _cols_from_blocks(late[2])
    w_down_f = late[3].reshape(-1, d)
    mo = _mm(mpre, w_out_f, name="out_proj")

    c_par = [ga1, g_ffn, sh2, sc2]
    c_blk = _pick(n_tok, 512, 8)
    x1, h2 = _stage_fwd(_fn_c, [xs, mo], c_par, [(d, F32), (d, BF16)], blk=c_blk, name="modulate2_fwd")
    up_g = _mm(h2, w_up_g, out_dtype=BF16, name="up_gate")
    up_v = _mm(h2, w_up_v, out_dtype=BF16, name="up_val")
    cw_g, cw_v = conv_w_f[:, :ffn], conv_w_f[:, ffn:]
    cb_g, cb_v = conv_b[:, :ffn], conv_b[:, ffn:]
    act, gate_c, val_c = _conv_act_fwd(up_g, up_v, cw_g, cw_v, cb_g, cb_v)
    dn = _mm(act, w_down_f, name="down_proj")

    loss_part, d_x1a, d_dn, d_ga2, d_g_final = _stage_loss(
        _fn_e, [x1, dn, tgt], [ga2, g_final[None, :]], blk=c_blk, name="loss_head",
        row_grads=[(0, F32), (1, BF16)])

    d_act = _mm(d_dn, w_down_f, tb=True, out_dtype=BF16, name="down_proj_dx")
    g_w_down = _mm(act, d_dn, ta=True, out_dtype=BF16, name="down_proj_dw")

    own_block = {}

    def grad_blocks(name, g):
        blocks = _b16(_blocks_from_cols(g) if name in COL_SHARDED
                      else g.reshape(N_DEV, g.shape[0] // N_DEV, g.shape[1]))
        own_block[name] = lax.dynamic_index_in_dim(blocks, me, 0, keepdims=False)
        return blocks, "a2a"

    sent_down, tok = _exchange_begin([grad_blocks('w_down', g_w_down)], "grads_down_begin")
    dcg, dcv, g_cw_g, g_cw_v, g_cb_g, g_cb_v = _conv_act_bwd(up_g, up_v, gate_c, val_c, d_act)
    dug = _conv_transposed(dcg, cw_g + tok, "conv_dx_gate")
    duv = _conv_transposed(dcv, cw_v, "conv_dx_val")
    d_h2 = _mm(dug, w_up_g, tb=True, name="up_gate_dx")
    d_h2 = _mm(duv, w_up_v, tb=True, add=d_h2, out_dtype=BF16, name="up_val_dx")
    g_w_up = jnp.concatenate([_mm(h2, dug, ta=True, out_dtype=BF16, name="up_gate_dw"),
                              _mm(h2, duv, ta=True, out_dtype=BF16, name="up_val_dw")], axis=1)
    sent_up, tok = _exchange_begin(
        [grad_blocks('w_up', g_w_up), grad_blocks('conv_w', jnp.concatenate([g_cw_g, g_cw_v], axis=1))],
        "grads_up_begin")
    (d_xc, d_mo), (d_ga1, g_g_ffn, d_sh2, d_sc2) = _split(_stage_bwd(
        _fn_c, [xs, mo], [ga1 + tok] + c_par[1:], [d_x1a, d_h2], blk=c_blk, name="modulate2_bwd",
        row_grads=[(0, F32), (1, BF16)]), 2)

    d_mpre = _mm(d_mo, w_out_f, tb=True, out_dtype=BF16, name="out_proj_dx")
    g_w_out = _mm(mpre, d_mo, ta=True, out_dtype=BF16, name="out_proj_dw")
    (d_prest, d_ysc), b_grads = _split(_stage_bwd(
        _fn_b, b_rows, b_par, [d_mpre], blk=_pick(n_tok, 256, CHUNK), name="mixers_bwd",
        row_grads=[(1, BF16), (2, F32)]), 2)
    (g_s5_d, g_w_glu, g_b_glu, g_ln_g, g_ln_b, g_sgu_w, g_sgu_bt, g_w_pa, g_w_pb, g_b_gate) = b_grads

    sent_mix, tok = _exchange_begin(
        [grad_blocks('w_out', g_w_out), grad_blocks('s5_w_glu', g_w_glu), grad_blocks('w_proj_a', g_w_pa),
         grad_blocks('w_proj_b', g_w_pb)], "grads_mixer_begin")
    skip_g = jnp.tile(s5_d.reshape(-1, 1, S5_H), (1, 1, S5_T)) + tok
    s5_out = _s5_bwd(pu_g, puc_g, s5_prm, _to_groups(d_ysc), skip_g)
    g_a_re2, g_a_im2, g_ls, g_bt2, g_c2 = s5_out[2:]
    g_a_re = g_a_re2[..., :n_state] + g_a_re2[..., n_state:]
    g_a_im = g_a_im2[..., :n_state] + g_a_im2[..., n_state:]
    g_bt_re, g_bt_im = g_bt2[..., :n_state], g_bt2[..., n_state:]
    g_c_re, g_c_im = g_c2[..., :n_state], g_c2[..., n_state:]

    part = {
        's5_a_re': g_a_re, 's5_a_im': g_a_im, 's5_log_step': g_ls,
        's5_b_re': jnp.swapaxes(g_bt_re, 2, 3), 's5_b_im': jnp.swapaxes(g_bt_im, 2, 3),
        's5_c_re': g_c_re, 's5_c_im': g_c_im, 's5_d': g_s5_d, 's5_b_glu': g_b_glu, 'sgu_ln_g': g_ln_g,
        'sgu_ln_b': g_ln_b, 'sgu_w': g_sgu_w, 'sgu_b': jnp.transpose(g_sgu_bt), 'b_gate': g_b_gate,
        'g_ffn': g_g_ffn, 'conv_b': jnp.concatenate([g_cb_g, g_cb_v], axis=1), 'g_final': d_g_final,
    }
    early = [n for n in REPLICATED if n not in LATE_REPLICATED and n not in UNPACKED_REPLICATED]
    early_part = _pack([part[n] for n in early])
    own_small = {n: two_d(part[n]) for n in UNPACKED_REPLICATED}
    sent_small, tok = _exchange_begin(
        [(early_part, "gather")] + [(own_small[n], "gather") for n in UNPACKED_REPLICATED], "grads_small_begin")
    d_pu, d_puc = _from_groups(s5_out[0]), _from_groups(s5_out[1]) + tok

    g_w_in_u = _mm(hc, d_puc, ta=True, name="proj_u_ctx_dw")
    g_w_in_u = _mm(h, d_pu, ta=True, add=g_w_in_u, out_dtype=BF16, name="proj_u_dw")
    g_w_in = jnp.concatenate([g_w_in_u, _mm(h, d_prest, ta=True, out_dtype=BF16, name="proj_rest_dw")], axis=1)
    sent_in, tok = _exchange_begin([grad_blocks('w_in', g_w_in)], "grads_in_begin")
    w_in_u_behind = w_in_u + _b16(tok)
    d_h = _mm(d_pu, w_in_u_behind, tb=True, name="proj_u_dx")
    d_h = _mm(d_prest, w_in_rest, tb=True, add=d_h, out_dtype=BF16, name="proj_rest_dx")
    d_hc = _mm(d_puc, w_in_u_behind, tb=True, out_dtype=BF16, name="proj_u_ctx_dx")

    (grad_x,), (g_g_mix_x, d_sh1, d_sc1) = _split(_stage_bwd(
        _fn_a_res, [xs], a_par, [d_h, d_xc], blk=c_blk, name="modulate1_bwd", row_grads=[(0, F32)]), 1)
    _, (g_g_mix_c, d_sh1c, d_sc1c) = _split(_stage_bwd(
        _fn_a, [cx], ac_par, [d_hc], blk=_pick(n_ctx, 512, 8), name="modulate1_ctx_bwd", row_grads=[]), 0)

    zeros = jnp.zeros((1, (n_mod - 2) * d), F32)
    d_mod = jnp.concatenate([d_sh1, d_sc1, d_ga1, d_sh2, d_sc2, d_ga2], axis=1)
    d_mod_c = jnp.concatenate([d_sh1c, d_sc1c, zeros], axis=1)
    (d_mod_all,) = _exchange([(jnp.concatenate([d_mod, d_mod_c], axis=0), "gather")], "gather_dmod")
    d_mod_rows = jnp.transpose(d_mod_all, (1, 0, 2)).reshape(2 * N_DEV, n_mod * d)
    d_mod_mine = lax.dynamic_slice_in_dim(d_mod_rows, me * mod_cols, mod_cols, axis=1)
    (d_cs,), (g_w_ada,) = _split(_stage_bwd(
        _fn_mod, [cs_in], [w_ada_loc], [d_mod_mine], blk=2 * N_DEV, name="mod_bwd", row_grads=[(0, F32)]), 1)

    part.update({'c_ctx': jnp.sum(d_cs[N_DEV:], axis=0), 'b_ada': d_mod + d_mod_c, 'g_mix': g_g_mix_x + g_g_mix_c})
    late_parts, loss_parts = _exchange(
        [(_pack([part[n] for n in LATE_REPLICATED]), "gather"), (jnp.broadcast_to(loss_part, (8, LANE)), "gather")],
        "exchange_grads")

    summed = {}
    small_parts = _exchange_end(sent_small, late_parts, "grads_small_end")
    early_parts = small_parts[0]
    for n, parts in zip(UNPACKED_REPLICATED, small_parts[1:]):
        summed[n], own_block[n] = parts, own_small[n]
    (summed['w_down'],) = _exchange_end(sent_down, late_parts, "grads_down_end")
    summed['w_up'], summed['conv_w'] = _exchange_end(sent_up, late_parts, "grads_up_end")
    summed['w_out'], summed['s5_w_glu'], summed['w_proj_a'], summed['w_proj_b'] = _exchange_end(
        sent_mix, late_parts, "grads_mixer_end")
    (summed['w_in'],) = _exchange_end(sent_in, late_parts, "grads_in_end")

    out = {}
    me_arr = me.reshape(1, 1).astype(jnp.int32)
    for names, parts, own, tag in ((early, early_parts, early_part, "early"),
                                   (LATE_REPLICATED, late_parts, None, "late")):
        res = _adamw(_pack([wts[n] for n in names]), parts, _pack([mom1[n] for n in names]),
                     _pack([mom2[n] for n in names]), "adamw_replicated_" + tag, own=own,
                     me=None if own is None else me_arr)
        res = [_unpack(r, [wts[n].shape for n in names]) for r in res]
        for i, n in enumerate(names):
            out[n] = tuple(r[i] for r in res)
    for n, parts in summed.items():
        shape = wts[n].shape
        res = _adamw(two_d(wts[n]), parts, two_d(mom1[n]), two_d(mom2[n]), "adamw_" + n, own=own_block[n],
                     me=me_arr)
        out[n] = tuple(r.reshape(shape) for r in res)
    res = _adamw(w_ada_loc, g_w_ada[None], m_w_ada[0], v_w_ada[0], "adamw_w_ada")
    out['w_ada'] = tuple(r.reshape(w_ada.shape) for r in res)

    loss = jnp.sum(loss_parts[:, 0, 0])
    return (loss, grad_x[None], *[out[n][0] for n in WEIGHTS], *[out[n][1] for n in WEIGHTS],
            *[out[n][2] for n in WEIGHTS], *[out[n][3] for n in WEIGHTS])


def _split(res, n_rows):
    return tuple(res[:n_rows]), tuple(res[n_rows:])
```

```python
import functools
import math

import jax
import jax.numpy as jnp
from jax import lax
from jax.experimental import pallas as pl
from jax.experimental.pallas import tpu as pltpu

F32 = jnp.float32
BF16 = jnp.bfloat16
HI = lax.Precision.HIGHEST

N_DEV = 8
GRID_W = 64
CHUNK = 128
EPS = 1e-6
S5_T = 32
S5_H = 16
S5_CHUNK_ALIGN = 16
LANE = 128
HALO = 128
VMEM_LIMIT = 56 * 1024 * 1024
PACK_ROWS = 256

ADAM_LR = 0.001
ADAM_B1 = 0.9
ADAM_B2 = 0.999
ADAM_EPS = 1e-08
ADAM_WD = 0.01
ADAM_STEP = 10

WEIGHTS = ['c_ctx', 'w_ada', 'b_ada', 'g_mix', 'w_in', 's5_a_re', 's5_a_im', 's5_log_step', 's5_b_re', 's5_b_im',
           's5_c_re', 's5_c_im', 's5_d', 's5_w_glu', 's5_b_glu', 'sgu_ln_g', 'sgu_ln_b', 'sgu_w', 'sgu_b',
           'w_proj_a', 'w_proj_b', 'b_gate', 'w_out', 'g_ffn', 'w_up', 'conv_w', 'conv_b', 'w_down', 'g_final']
COL_SHARDED = ('w_ada', 'w_in', 'w_proj_a', 'w_proj_b', 'w_up', 'conv_w')
ROW_SHARDED = ('s5_w_glu', 'w_out', 'w_down')
SHARDED = COL_SHARDED + ROW_SHARDED
REPLICATED = [n for n in WEIGHTS if n not in SHARDED]
LATE_REPLICATED = ['c_ctx', 'b_ada', 'g_mix']
UNPACKED_REPLICATED = ['sgu_w', 's5_c_re', 's5_c_im']


def _call(body, **kw):
    return pl.pallas_call(body, **kw)


def _params(n_grid):
    return pltpu.CompilerParams(dimension_semantics=("arbitrary",) * n_grid, vmem_limit_bytes=VMEM_LIMIT)


def _pick(dim, pref, unit=LANE):
    best = None
    d = unit
    while d <= min(dim, pref):
        if dim % d == 0:
            best = d
        d += unit
    return best if best is not None else dim


def _dg(a, b, ca, cb, prec=None):
    return lax.dot_general(a, b, (((ca,), (cb,)), ((), ())), precision=prec, preferred_element_type=F32)


def _b16(v):
    return v.astype(BF16)


@jax.custom_vjp
def mmb(a, b):
    return _dg(_b16(a), _b16(b), 1, 0)


def _mmb_fwd(a, b):
    return mmb(a, b), (a, b)


def _mmb_bwd(res, g):
    a, b = res
    g = _b16(g)
    return _dg(g, _b16(b), 1, 1).astype(a.dtype), _dg(_b16(a), g, 0, 0).astype(b.dtype)


mmb.defvjp(_mmb_fwd, _mmb_bwd)


@jax.custom_vjp
def mmb_nt(a, b):
    return _dg(_b16(a), _b16(b), 1, 1)


def _mmb_nt_fwd(a, b):
    return mmb_nt(a, b), (a, b)


def _mmb_nt_bwd(res, g):
    a, b = res
    g = _b16(g)
    return _dg(g, _b16(b), 1, 0).astype(a.dtype), _dg(g, _b16(a), 0, 0).astype(b.dtype)


mmb_nt.defvjp(_mmb_nt_fwd, _mmb_nt_bwd)


@jax.custom_vjp
def mmf(a, b):
    return _dg(a, b, 1, 0, HI)


def _mmf_fwd(a, b):
    return mmf(a, b), (a, b)


def _mmf_bwd(res, g):
    a, b = res
    return _dg(g, b, 1, 1, HI), _dg(a, g, 0, 0, HI)


mmf.defvjp(_mmf_fwd, _mmf_bwd)


def _dg3(a, b, ca, cb):
    ah, bh = _b16(a), _b16(b)
    al, bl = _b16(a - ah.astype(F32)), _b16(b - bh.astype(F32))
    return _dg(ah, bh, ca, cb) + _dg(ah, bl, ca, cb) + _dg(al, bh, ca, cb)


@jax.custom_vjp
def mm3_nt(a, b):
    return _dg3(a, b, 1, 1)


def _mm3_nt_fwd(a, b):
    return mm3_nt(a, b), (a, b)


def _mm3_nt_bwd(res, g):
    a, b = res
    return _dg3(g, b, 1, 0), _dg3(g, a, 0, 0)


mm3_nt.defvjp(_mm3_nt_fwd, _mm3_nt_bwd)


def _shift_impl(x, k, up):
    n = x.shape[0]
    idx = lax.broadcasted_iota(jnp.int32, (n, 1), 0)
    if up:
        return jnp.where(idx < n - k, pltpu.roll(x, n - k, 0), 0.0)
    return jnp.where(idx >= k, pltpu.roll(x, k, 0), 0.0)


@functools.partial(jax.custom_vjp, nondiff_argnums=(1, 2))
def _shift(x, k, up):
    return _shift_impl(x, k, up)


def _shift_fwd(x, k, up):
    return _shift_impl(x, k, up), None


def _shift_bwd(k, up, _, g):
    return (_shift_impl(g, k, not up),)


_shift.defvjp(_shift_fwd, _shift_bwd)


def _mm(a, b, *, name, ta=False, tb=False, add=None, out_dtype=F32, tm_pref=2048, tn_pref=1408, tk_pref=1408):
    m, k = (a.shape[1], a.shape[0]) if ta else a.shape
    n = b.shape[0] if tb else b.shape[1]
    if add is not None and out_dtype != F32:
        tm_pref = min(tm_pref, 1024)
    if ta:
        tk_pref = max(tk_pref, 2048)
    tm, tn, tk = _pick(m, tm_pref), _pick(n, tn_pref), _pick(k, tk_pref)
    nk = k // tk
    a_spec = (pl.BlockSpec((tk, tm), lambda i, j, kk: (kk, i)) if ta
              else pl.BlockSpec((tm, tk), lambda i, j, kk: (i, kk)))
    b_spec = (pl.BlockSpec((tn, tk), lambda i, j, kk: (j, kk)) if tb
              else pl.BlockSpec((tk, tn), lambda i, j, kk: (kk, j)))
    o_spec = pl.BlockSpec((tm, tn), lambda i, j, kk: (i, j))
    ca, cb = (0 if ta else 1), (1 if tb else 0)
    has_add = add is not None
    in_place = out_dtype == F32 or nk == 1

    def body(*refs):
        a_ref, b_ref = refs[0], refs[1]
        o_ref = refs[3] if has_add else refs[2]
        prod = _dg(_b16(a_ref[...]), _b16(b_ref[...]), ca, cb)
        if nk == 1:
            if has_add:
                prod = prod + refs[2][...].astype(F32)
            o_ref[...] = prod.astype(o_ref.dtype)
            return
        acc_ref = o_ref if in_place else refs[-1]
        kk = pl.program_id(2)

        @pl.when(kk == 0)
        def _():
            acc_ref[...] = prod

        @pl.when(kk > 0)
        def _():
            acc_ref[...] += prod

        if has_add or not in_place:
            @pl.when(kk == nk - 1)
            def _():
                r = acc_ref[...]
                if has_add:
                    r = r + refs[2][...].astype(F32)
                o_ref[...] = r.astype(o_ref.dtype)

    ins = [a, b] + ([add] if has_add else [])
    in_specs = [a_spec, b_spec] + ([o_spec] if has_add else [])
    return _call(body, name=name, grid=(m // tm, n // tn, nk), in_specs=in_specs, out_specs=o_spec,
                 out_shape=jax.ShapeDtypeStruct((m, n), out_dtype),
                 scratch_shapes=[] if in_place else [pltpu.VMEM((tm, tn), F32)],
                 compiler_params=_params(3))(*ins)


def _row_spec(blk, width):
    return pl.BlockSpec((blk, width), lambda i: (i, 0))


def _whole_spec(shape):
    return pl.BlockSpec(shape, lambda i: (0,) * len(shape))


def _stage_fwd(fn, rows, params, outs, *, blk, name, n_rows=None):
    n = n_rows or rows[0].shape[0]
    nr, npar = len(rows), len(params)

    def body(*refs):
        vals = [r[...] for r in refs[:nr + npar]]
        res = fn(*vals)
        for o_ref, v in zip(refs[nr + npar:], res):
            o_ref[...] = v.astype(o_ref.dtype)

    return _call(body, name=name, grid=(n // blk,),
                 in_specs=[_row_spec(blk, r.shape[1]) for r in rows] + [_whole_spec(p.shape) for p in params],
                 out_specs=[_row_spec(blk, w) for w, _ in outs],
                 out_shape=[jax.ShapeDtypeStruct((n, w), dt) for w, dt in outs],
                 compiler_params=_params(1))(*rows, *params)


def _stage_bwd(fn, rows, params, cts, *, blk, name, row_grads, n_rows=None):
    n = n_rows or rows[0].shape[0]
    nr, npar, nct = len(rows), len(params), len(cts)

    def body(*refs):
        vals = [r[...].astype(F32) for r in refs[:nr + npar]]
        ct = [r[...] for r in refs[nr + npar:nr + npar + nct]]
        d_rows = refs[nr + npar + nct:nr + npar + nct + len(row_grads)]
        d_par = refs[nr + npar + nct + len(row_grads):]
        res, vjp = jax.vjp(fn, *vals)
        g = vjp(tuple(c.astype(r.dtype) for c, r in zip(ct, res)))
        for o_ref, (j, _) in zip(d_rows, row_grads):
            o_ref[...] = g[j].astype(o_ref.dtype)

        @pl.when(pl.program_id(0) == 0)
        def _():
            for o_ref in d_par:
                o_ref[...] = jnp.zeros_like(o_ref)

        for j, o_ref in enumerate(d_par):
            o_ref[...] += g[nr + j].astype(F32)

    return _call(body, name=name, grid=(n // blk,),
                 in_specs=([_row_spec(blk, r.shape[1]) for r in rows] + [_whole_spec(p.shape) for p in params]
                           + [_row_spec(blk, c.shape[1]) for c in cts]),
                 out_specs=([_row_spec(blk, rows[j].shape[1]) for j, _ in row_grads]
                            + [_whole_spec(p.shape) for p in params]),
                 out_shape=([jax.ShapeDtypeStruct((n, rows[j].shape[1]), dt) for j, dt in row_grads]
                            + [jax.ShapeDtypeStruct(p.shape, F32) for p in params]),
                 compiler_params=_params(1))(*rows, *params, *cts)


def _stage_loss(fn, rows, params, *, blk, name, row_grads):
    n = rows[0].shape[0]
    nr, npar = len(rows), len(params)

    def body(*refs):
        vals = [r[...].astype(F32) for r in refs[:nr + npar]]
        loss_ref = refs[nr + npar]
        d_rows = refs[nr + npar + 1:nr + npar + 1 + len(row_grads)]
        d_par = refs[nr + npar + 1 + len(row_grads):]
        res, vjp = jax.vjp(fn, *vals)
        g = vjp(jnp.ones_like(res))
        for o_ref, (j, _) in zip(d_rows, row_grads):
            o_ref[...] = g[j].astype(o_ref.dtype)

        @pl.when(pl.program_id(0) == 0)
        def _():
            loss_ref[...] = jnp.zeros_like(loss_ref)
            for o_ref in d_par:
                o_ref[...] = jnp.zeros_like(o_ref)

        loss_ref[...] += res
        for j, o_ref in enumerate(d_par):
            o_ref[...] += g[nr + j].astype(F32)

    return _call(body, name=name, grid=(n // blk,),
                 in_specs=[_row_spec(blk, r.shape[1]) for r in rows] + [_whole_spec(p.shape) for p in params],
                 out_specs=([_whole_spec((1, 1))] + [_row_spec(blk, rows[j].shape[1]) for j, _ in row_grads]
                            + [_whole_spec(p.shape) for p in params]),
                 out_shape=([jax.ShapeDtypeStruct((1, 1), F32)]
                            + [jax.ShapeDtypeStruct((n, rows[j].shape[1]), dt) for j, dt in row_grads]
                            + [jax.ShapeDtypeStruct(p.shape, F32) for p in params]),
                 compiler_params=_params(1))(*rows, *params)


def _rms(x, g):
    return x * lax.rsqrt(jnp.mean(x * x, axis=-1, keepdims=True) + EPS) * g


def _modulate(x, g, shift, scale):
    return _rms(x, g) * (1.0 + scale) + shift


def _fn_mod(cs, w_ada):
    return (mmb(jax.nn.silu(cs), w_ada),)


def _fn_a(x, g_mix, sh, sc):
    return (_b16(_modulate(x, g_mix, sh, sc)),)


def _fn_a_res(x, g_mix, sh, sc):
    return _b16(_modulate(x, g_mix, sh, sc)), x


def _sgu_spatial(v, sgu_w, sgu_bt):
    rows, width = v.shape
    gdim = width // (sgu_w.shape[0] // CHUNK)
    groups = width // gdim
    expand = (lax.broadcasted_iota(jnp.int32, (groups, width), 1) // gdim
              == lax.broadcasted_iota(jnp.int32, (groups, width), 0)).astype(F32)
    bias = mmf(sgu_bt, expand)
    lane = lax.broadcasted_iota(jnp.int32, (CHUNK, LANE), 1)
    per_lane_block = LANE // gdim
    chunks = []
    for ci in range(rows // CHUNK):
        vc = v[ci * CHUNK:(ci + 1) * CHUNK]
        blocks = []
        for lb in range(width // LANE):
            vb = vc[:, lb * LANE:(lb + 1) * LANE]
            acc = None
            for s in range(per_lane_block):
                g = lb * per_lane_block + s
                r = mmb(sgu_w[g * CHUNK:(g + 1) * CHUNK], vb)
                sel = (lane // gdim) == s
                acc = jnp.where(sel, r, 0.0) if acc is None else jnp.where(sel, r, acc)
            blocks.append(acc)
        chunks.append(jnp.concatenate(blocks, axis=1) + bias)
    return jnp.concatenate(chunks, axis=0)


def _fn_b(pu, prest, ysc, s5_d, w_glu, b_glu, ln_g, ln_b, sgu_w, sgu_bt, w_pa, w_pb, b_gate):
    sw = ln_g.shape[1]
    y = jax.nn.gelu(pu * s5_d + ysc)
    ya = y * jax.nn.sigmoid(mmb(y, w_glu) + b_glu)
    z = jax.nn.gelu(prest[:, :2 * sw])
    u, v = z[:, :sw], z[:, sw:]
    vc = v - jnp.mean(v, axis=-1, keepdims=True)
    v = vc * lax.rsqrt(jnp.mean(vc * vc, axis=-1, keepdims=True) + EPS) * ln_g + ln_b
    yb = u * _sgu_spatial(v, sgu_w, sgu_bt)
    gates = jax.nn.sigmoid(prest[:, 2 * sw:] + b_gate)
    d = gates.shape[1] // 2
    return (_b16(gates[:, :d] * mmb(ya, w_pa) + gates[:, d:] * mmb(yb, w_pb)),)


def _fn_c(x, mo, ga1, g_ffn, sh2, sc2):
    x1 = x + ga1 * mo
    return x1, _b16(_modulate(x1, g_ffn, sh2, sc2))


def _fn_e(x1, dn, tgt, ga2, g_final):
    y = _rms(x1 + ga2 * dn, g_final)
    err = (y - tgt) ** 2
    return 0.5 * jnp.sum(jnp.mean(err, axis=-1, keepdims=True), axis=0, keepdims=True)


def _swap_impl(x):
    return pltpu.roll(x, x.shape[1] // 2, 1)


@jax.custom_vjp
def _swap_halves(x):
    return _swap_impl(x)


_swap_halves.defvjp(lambda x: (_swap_impl(x), None), lambda _, g: (_swap_impl(g),))


def _s5_direction(u, mask, a_re, a_im, log_step, bt, c, rev):
    nc, width = u.shape
    t_len = width // S5_H
    n2 = a_re.shape[1]
    lane = lax.broadcasted_iota(jnp.int32, (1, n2), 1)
    sign = jnp.where(lane < n2 // 2, -1.0, 1.0)
    dt = jnp.exp(log_step)
    lr, li = a_re * dt, a_im * dt
    mag = jnp.exp(lr)
    ab_re, ab_im = mag * jnp.cos(li), mag * jnp.sin(li)
    p, q = ab_re - 1.0, ab_im
    den = a_re * a_re + a_im * a_im
    k_re, k_im = (p * a_re + q * a_im) / den, (q * a_re - p * a_im) / den
    bb = k_re * bt + (k_im * sign) * _swap_halves(bt)

    def power(e):
        m = jnp.exp(lr * e)
        return m * jnp.cos(li * e), m * jnp.sin(li * e)

    order = range(t_len - 1, -1, -1) if rev else range(t_len)
    e1 = jnp.concatenate([jnp.full((1, 1, n2), float(t_len - 1 - pos), F32) for pos in order], axis=0)
    lr3, li3, sign3 = lr.reshape(1, 1, n2), li.reshape(1, 1, n2), sign.reshape(1, 1, n2)
    m1, c1, s1 = jnp.exp(lr3 * e1), jnp.cos(li3 * e1), jnp.sin(li3 * e1)
    m2 = jnp.exp(-(lr3 * e1))
    x1r, x1i = m1 * c1, m1 * s1
    x2r, x2i = m2 * c1, -(m2 * s1)
    at_r, at_i = [v.reshape(1, 1, n2) for v in power(float(t_len))]
    x3r, x3i = at_r * x2r - at_i * x2i, at_r * x2i + at_i * x2r

    def rows(xr, xi, z):
        z3, zs3 = z.reshape(1, S5_H, n2), _swap_halves(z).reshape(1, S5_H, n2)
        return (xr * z3 + (xi * sign3) * zs3).reshape(width, n2)

    p_in = rows(x1r, x1i, bb)
    r_out = rows(x2r, x2i, c)
    w_out = rows(x3r, x3i, c)
    toep = mm3_nt(p_in, r_out * (-sign)) * mask

    state = mmb(u, p_in)
    k = 1
    while k < nc:
        ar, ai = power(float(t_len * k))
        moved = _shift(state, k, rev)
        state = state + ar * moved + (ai * sign) * _swap_halves(moved)
        k *= 2
    entering = _shift(state, 1, rev)
    return mmb(u, toep) + mmb_nt(entering, w_out * (-sign))


def _fn_s5(masks, x_chunks, ctx_chunks, *prm):
    nx, nctx = x_chunks.shape[0], ctx_chunks.shape[0]
    pad = (-(nx + 2 * nctx)) % S5_CHUNK_ALIGN
    pieces = [ctx_chunks, x_chunks, ctx_chunks] + ([jnp.zeros((pad, x_chunks.shape[1]), F32)] if pad else [])
    u = jnp.concatenate(pieces, axis=0)
    out = None
    for d in range(2):
        y = _s5_direction(u, masks[d], *[p[d] for p in prm], rev=(d == 1))
        out = y if out is None else out + y
    return out[nctx:nctx + nx]


S5_GROUPS_PER_STEP = 2


def _s5_specs(prm):
    return [pl.BlockSpec((2, S5_GROUPS_PER_STEP) + p.shape[2:], lambda g: (0, g, 0, 0)) for p in prm]


def _group_spec(a):
    return pl.BlockSpec((S5_GROUPS_PER_STEP,) + a.shape[1:], lambda i: (i, 0, 0))


def _s5_masks(width):
    pos = jnp.arange(width) // S5_H
    causal = (pos[None, :] >= pos[:, None]).astype(F32)
    return jnp.stack([causal, causal.T])


def _s5_fwd(x_g, ctx_g, prm):
    masks = _s5_masks(x_g.shape[2])

    def body(*refs):
        for j in range(S5_GROUPS_PER_STEP):
            pv = [r[:, j] for r in refs[3:3 + len(prm)]]
            refs[-1][j] = _fn_s5(refs[0][...], refs[1][j].astype(F32), refs[2][j], *pv)

    return _call(body, name="s5_fwd", grid=(x_g.shape[0] // S5_GROUPS_PER_STEP,),
                 in_specs=[_whole_spec(masks.shape), _group_spec(x_g), _group_spec(ctx_g)] + _s5_specs(prm),
                 out_specs=_group_spec(x_g), out_shape=jax.ShapeDtypeStruct(x_g.shape, F32),
                 compiler_params=_params(1))(masks, x_g, ctx_g, *prm)


def _s5_bwd(x_g, ctx_g, prm, dy_g, skip_g):
    npar = len(prm)
    masks = _s5_masks(x_g.shape[2])

    def body(*refs):
        outs = refs[5 + npar:]
        for j in range(S5_GROUPS_PER_STEP):
            pv = [r[:, j] for r in refs[3:3 + npar]]
            dy = refs[3 + npar][j]
            _, vjp = jax.vjp(functools.partial(_fn_s5, refs[0][...]), refs[1][j].astype(F32), refs[2][j], *pv)
            grads = vjp(dy)
            outs[0][j] = (grads[0] + dy * refs[4 + npar][j]).astype(outs[0].dtype)
            outs[1][j] = grads[1]
            for o_ref, gv in zip(outs[2:], grads[2:]):
                o_ref[:, j] = gv

    return _call(body, name="s5_bwd", grid=(x_g.shape[0] // S5_GROUPS_PER_STEP,),
                 in_specs=([_whole_spec(masks.shape), _group_spec(x_g), _group_spec(ctx_g)] + _s5_specs(prm)
                           + [_group_spec(dy_g), _group_spec(skip_g)]),
                 out_specs=[_group_spec(x_g), _group_spec(ctx_g)] + _s5_specs(prm),
                 out_shape=[jax.ShapeDtypeStruct(x_g.shape, BF16), jax.ShapeDtypeStruct(ctx_g.shape, F32)]
                 + [jax.ShapeDtypeStruct(p.shape, F32) for p in prm],
                 compiler_params=_params(1))(masks, x_g, ctx_g, *prm, dy_g, skip_g)


def _to_groups(tok):
    n, width = tok.shape
    g = width // S5_H
    return jnp.transpose(tok.reshape(n, g, S5_H), (1, 0, 2)).reshape(g, n // S5_T, S5_T * S5_H)


def _from_groups(grp):
    g, nc, _ = grp.shape
    return jnp.transpose(grp.reshape(g, nc * S5_T, S5_H), (1, 0, 2)).reshape(nc * S5_T, g * S5_H)


def _conv_shifted(xp, xm, xn, blk_i, n_blk):
    tb = xm.shape[0]
    xp = jnp.where(blk_i == 0, 0.0, xp.astype(F32))
    xn = jnp.where(blk_i == n_blk - 1, 0.0, xn.astype(F32))
    buf = jnp.concatenate([xp, xm.astype(F32), xn], axis=0)
    n = tb + 2 * HALO
    col = lax.broadcasted_iota(jnp.int32, (n, 1), 0) % GRID_W
    left = jnp.where(col >= 1, pltpu.roll(buf, 1, 0), 0.0)
    right = jnp.where(col <= GRID_W - 2, pltpu.roll(buf, n - 1, 0), 0.0)
    return left, buf, right


def _conv_taps(xp, xm, xn, blk_i, n_blk):
    tb = xm.shape[0]
    shifted = _conv_shifted(xp, xm, xn, blk_i, n_blk)
    taps = []
    for di in range(3):
        start = HALO + (di - 1) * GRID_W
        for dj in range(3):
            taps.append(shifted[dj][start:start + tb])
    return taps


def _conv_sum(taps, w_ref, flip=False):
    acc = None
    for k, tap in enumerate(taps):
        j = len(taps) - 1 - k if flip else k
        term = tap * w_ref[j:j + 1, :]
        acc = term if acc is None else acc + term
    return acc


def _conv_geometry(n_tok, width, tb_pref=1024):
    tb = _pick(n_tok, tb_pref, HALO)
    cb = _pick(width, 256)
    nb = tb // HALO
    last = n_tok // HALO - 1
    main = pl.BlockSpec((tb, cb), lambda j, i: (i, j))
    prev = pl.BlockSpec((HALO, cb), lambda j, i: (jnp.maximum(i * nb - 1, 0), j))
    nxt = pl.BlockSpec((HALO, cb), lambda j, i: (jnp.minimum(i * nb + nb, last), j))
    par = lambda r: pl.BlockSpec((r, cb), lambda j, i: (0, j))
    return tb, cb, main, prev, nxt, par


def _conv_act_fwd(up_g, up_v, w_g, w_v, b_g, b_v):
    n_tok, width = up_g.shape
    tb, cb, main, prev, nxt, par = _conv_geometry(n_tok, width, 2048)
    n_blk = n_tok // tb

    def body(gp, gm, gn, vp, vm, vn, wg, wv, bg, bv, o_ref, gate_ref, val_ref):
        i = pl.program_id(1)
        gate = _conv_sum(_conv_taps(gp[...], gm[...], gn[...], i, n_blk), wg) + bg[...]
        val = _conv_sum(_conv_taps(vp[...], vm[...], vn[...], i, n_blk), wv) + bv[...]
        o_ref[...] = (jax.nn.silu(gate) * val).astype(o_ref.dtype)
        gate_ref[...] = gate
        val_ref[...] = val

    shp = jax.ShapeDtypeStruct
    return _call(body, name="conv_act_fwd", grid=(width // cb, n_tok // tb),
                 in_specs=[prev, main, nxt, prev, main, nxt, par(9), par(9), par(1), par(1)],
                 out_specs=[main, main, main],
                 out_shape=[shp((n_tok, width), BF16), shp((n_tok, width), F32), shp((n_tok, width), F32)],
                 compiler_params=_params(2))(up_g, up_g, up_g, up_v, up_v, up_v, w_g, w_v, b_g, b_v)


def _conv_act_bwd(up_g, up_v, gate_c, val_c, d_act):
    n_tok, width = up_g.shape
    tb, cb, main, prev, nxt, par = _conv_geometry(n_tok, width)
    n_blk = n_tok // tb

    def body(gp, gm, gn, vp, vm, vn, gc, vc, da, dcg, dcv, dwg, dwv, dbg, dbv):
        i = pl.program_id(1)
        taps_g = _conv_taps(gp[...], gm[...], gn[...], i, n_blk)
        taps_v = _conv_taps(vp[...], vm[...], vn[...], i, n_blk)
        gate, val = gc[...], vc[...]
        sig = jax.nn.sigmoid(gate)
        d = da[...].astype(F32)
        d_gate = d * val * sig * (1.0 + gate * (1.0 - sig))
        d_val = d * gate * sig
        dcg[...] = d_gate.astype(dcg.dtype)
        dcv[...] = d_val.astype(dcv.dtype)

        @pl.when(i == 0)
        def _():
            for r in (dwg, dwv, dbg, dbv):
                r[...] = jnp.zeros_like(r)

        dbg[...] += jnp.sum(d_gate, axis=0, keepdims=True)
        dbv[...] += jnp.sum(d_val, axis=0, keepdims=True)
        for k in range(9):
            dwg[k:k + 1, :] += jnp.sum(taps_g[k] * d_gate, axis=0, keepdims=True)
            dwv[k:k + 1, :] += jnp.sum(taps_v[k] * d_val, axis=0, keepdims=True)

    shp = jax.ShapeDtypeStruct
    return _call(body, name="conv_act_bwd", grid=(width // cb, n_tok // tb),
                 in_specs=[prev, main, nxt, prev, main, nxt, main, main, main],
                 out_specs=[main, main, par(9), par(9), par(1), par(1)],
                 out_shape=[shp((n_tok, width), BF16), shp((n_tok, width), BF16), shp((9, width), F32),
                            shp((9, width), F32), shp((1, width), F32), shp((1, width), F32)],
                 compiler_params=_params(2))(up_g, up_g, up_g, up_v, up_v, up_v, gate_c, val_c, d_act)


def _conv_transposed(x, w, name):
    n_tok, width = x.shape
    tb, cb, main, prev, nxt, par = _conv_geometry(n_tok, width, 2048)
    n_blk = n_tok // tb

    def body(xp, xm, xn, w_ref, o_ref):
        taps = _conv_taps(xp[...], xm[...], xn[...], pl.program_id(1), n_blk)
        o_ref[...] = _conv_sum(taps, w_ref, flip=True).astype(o_ref.dtype)

    return _call(body, name=name, grid=(width // cb, n_tok // tb), in_specs=[prev, main, nxt, par(9)],
                 out_specs=main, out_shape=jax.ShapeDtypeStruct((n_tok, width), BF16),
                 compiler_params=_params(2))(x, x, x, w)


def _adamw(w, g_parts, m, v, name, own=None, me=None):
    rows, cols = w.shape
    parts = g_parts.shape[0]
    blk = _pick(rows, 256, 8)
    spec = pl.BlockSpec((blk, cols), lambda i: (i, 0))
    has_own = own is not None

    def body(*refs):
        w_ref, g_ref, m_ref, v_ref = refs[:4]
        g_out, d_out, m_out, v_out = refs[-4:]

        def part(p):
            if has_own:
                return jnp.where(refs[5][...] == p, refs[4][...], g_ref[p]).astype(F32)
            return g_ref[p].astype(F32)

        g = part(0)
        for p in range(1, parts):
            g = g + part(p)
        m_new = ADAM_B1 * m_ref[...] + (1.0 - ADAM_B1) * g
        v_new = ADAM_B2 * v_ref[...] + (1.0 - ADAM_B2) * (g * g)
        m_hat = m_new / (1.0 - ADAM_B1 ** ADAM_STEP)
        v_hat = v_new / (1.0 - ADAM_B2 ** ADAM_STEP)
        g_out[...] = g
        d_out[...] = -ADAM_LR * (m_hat / (jnp.sqrt(v_hat) + ADAM_EPS) + ADAM_WD * w_ref[...])
        m_out[...] = m_new
        v_out[...] = v_new

    extra = [own, me] if has_own else []
    return _call(body, name=name, grid=(rows // blk,),
                 in_specs=([spec, pl.BlockSpec((parts, blk, cols), lambda i: (0, i, 0)), spec, spec]
                           + ([spec, _whole_spec((1, 1))] if has_own else [])),
                 out_specs=[spec] * 4, out_shape=[jax.ShapeDtypeStruct((rows, cols), F32)] * 4,
                 compiler_params=_params(1))(w, g_parts, m, v, *extra)


def _exchange(items, name):
    n = len(items)
    hbm = pl.BlockSpec(memory_space=pl.ANY)

    def body(*refs):
        srcs, outs = refs[:n], refs[n:2 * n]
        send_sems, recv_sems, own_sems = refs[2 * n:]
        x, y, c = lax.axis_index("x"), lax.axis_index("y"), lax.axis_index("c")
        me = 4 * x + 2 * y + c
        own = []
        for i, (_, mode) in enumerate(items):
            src = srcs[i] if mode == "gather" else srcs[i].at[me]
            cp = pltpu.make_async_copy(src, outs[i].at[me], own_sems.at[i])
            cp.start()
            own.append(cp)
        sent = []
        for i, (_, mode) in enumerate(items):
            for k in range(1, N_DEV):
                px = 1 - x if k & 4 else x
                py = 1 - y if k & 2 else y
                pc = 1 - c if k & 1 else c
                peer = 4 * px + 2 * py + pc
                src = srcs[i] if mode == "gather" else srcs[i].at[peer]
                cp = pltpu.make_async_remote_copy(
                    src_ref=src, dst_ref=outs[i].at[me], send_sem=send_sems.at[i, k - 1],
                    recv_sem=recv_sems.at[i, k - 1], device_id=(px, py, pc), device_id_type=pl.DeviceIdType.MESH)
                cp.start()
                landing = pltpu.make_async_remote_copy(
                    src_ref=src, dst_ref=outs[i].at[peer], send_sem=send_sems.at[i, k - 1],
                    recv_sem=recv_sems.at[i, k - 1], device_id=(px, py, pc), device_id_type=pl.DeviceIdType.MESH)
                sent.append((cp, landing))
        for cp in own:
            cp.wait()
        for cp, landing in sent:
            cp.wait_send()
            landing.wait_recv()

    out_shape = [jax.ShapeDtypeStruct((N_DEV,) + (a.shape if mode == "gather" else a.shape[1:]), a.dtype)
                 for a, mode in items]
    return _call(body, name=name, in_specs=[hbm] * n, out_specs=[hbm] * n, out_shape=out_shape,
                 scratch_shapes=[pltpu.SemaphoreType.DMA((n, N_DEV - 1)), pltpu.SemaphoreType.DMA((n, N_DEV - 1)),
                                 pltpu.SemaphoreType.DMA((n,))])(*[a for a, _ in items])


def _peer(k, x, y, c):
    px = 1 - x if k & 4 else x
    py = 1 - y if k & 2 else y
    pc = 1 - c if k & 1 else c
    return (px, py, pc), 4 * px + 2 * py + pc


_HBM_SPEC = pl.BlockSpec(memory_space=pltpu.HBM)
_SEM_SPEC = pl.BlockSpec(memory_space=pltpu.SEMAPHORE)
_SPLIT_EFFECT = pltpu.SideEffectType.DATAFLOW_SIDE_EFFECTING


def _exchange_begin(items, name, after=None):
    n = len(items)
    modes = [mode for _, mode in items]
    srcs = [pltpu.with_memory_space_constraint(a, pltpu.HBM) for a, _ in items]
    lands = [pltpu.with_memory_space_constraint(
        lax.empty((N_DEV,) + (a.shape if mode == "gather" else a.shape[1:]), a.dtype), pltpu.HBM)
        for a, mode in items]

    def body(*refs):
        src_refs, land_refs = refs[:n], refs[n:2 * n]
        token = refs[-1]
        send_sems, recv_sems = refs[-2 * n - 3], refs[-2 * n - 2]
        x, y, c = lax.axis_index("x"), lax.axis_index("y"), lax.axis_index("c")
        me = 4 * x + 2 * y + c
        for i in range(n):
            for k in range(1, N_DEV):
                coords, peer = _peer(k, x, y, c)
                src = src_refs[i] if modes[i] == "gather" else src_refs[i].at[peer]
                pltpu.make_async_remote_copy(
                    src_ref=src, dst_ref=land_refs[i].at[me], send_sem=send_sems.at[i * (N_DEV - 1) + k - 1],
                    recv_sem=recv_sems.at[i * (N_DEV - 1) + k - 1], device_id=coords,
                    device_id_type=pl.DeviceIdType.MESH).start()
        token[...] = jnp.zeros_like(token)

    sems = pltpu.SemaphoreType.DMA((n * (N_DEV - 1),))
    order = [] if after is None else [after]
    res = _call(body, name=name,
                out_shape=(sems, sems, *[pltpu.HBM(a.shape, a.dtype) for a in srcs + lands],
                           jax.ShapeDtypeStruct((8, LANE), F32)),
                in_specs=[_HBM_SPEC] * (2 * n) + [pl.BlockSpec(memory_space=pl.ANY)] * len(order),
                out_specs=(_SEM_SPEC, _SEM_SPEC, *[_HBM_SPEC] * (2 * n), pl.BlockSpec(memory_space=pltpu.VMEM)),
                input_output_aliases={i: 2 + i for i in range(2 * n)},
                compiler_params=pltpu.CompilerParams(has_side_effects=_SPLIT_EFFECT))(*srcs, *lands, *order)
    return (modes, res[0], res[1], list(res[2:2 + n]), list(res[2 + n:2 + 2 * n])), res[-1][0, 0]


def _exchange_end(handle, after, name):
    modes, send_sems, recv_sems, srcs, lands = handle
    n = len(modes)

    def wait_body(*refs):
        src_refs, land_refs = refs[:n], refs[n:2 * n]
        send, recv = refs[2 * n], refs[2 * n + 1]
        x, y, c = lax.axis_index("x"), lax.axis_index("y"), lax.axis_index("c")
        for i in range(n):
            for k in range(1, N_DEV):
                coords, peer = _peer(k, x, y, c)
                src = src_refs[i] if modes[i] == "gather" else src_refs[i].at[peer]
                cp = pltpu.make_async_remote_copy(
                    src_ref=src, dst_ref=land_refs[i].at[peer], send_sem=send.at[i * (N_DEV - 1) + k - 1],
                    recv_sem=recv.at[i * (N_DEV - 1) + k - 1], device_id=coords,
                    device_id_type=pl.DeviceIdType.MESH)
                cp.wait_send()
                cp.wait_recv()

    res = _call(wait_body, name=name, out_shape=[pltpu.HBM(a.shape, a.dtype) for a in srcs + lands],
                in_specs=[_HBM_SPEC] * (2 * n) + [_SEM_SPEC, _SEM_SPEC, pl.BlockSpec(memory_space=pl.ANY)],
                out_specs=[_HBM_SPEC] * (2 * n), input_output_aliases={i: i for i in range(2 * n)},
                compiler_params=pltpu.CompilerParams(has_side_effects=_SPLIT_EFFECT))(
                    *srcs, *lands, send_sems, recv_sems, after)
    return res[n:]


def _with_own(land, own, me):
    slot = lax.broadcasted_iota(jnp.int32, (N_DEV,) + (1,) * (land.ndim - 1), 0)
    return jnp.where(slot == me, own[None], land)


def _cols_from_blocks(g):
    return jnp.transpose(g, (1, 0, 2)).reshape(g.shape[1], N_DEV * g.shape[2])


def _blocks_from_cols(w):
    r, c8 = w.shape
    return jnp.transpose(w.reshape(r, N_DEV, c8 // N_DEV), (1, 0, 2))


def _pack(arrs):
    flat = jnp.concatenate([a.reshape(-1).astype(F32) for a in arrs])
    rows = -(-flat.shape[0] // LANE)
    rows = -(-rows // PACK_ROWS) * PACK_ROWS if rows > PACK_ROWS else -(-rows // 8) * 8
    return jnp.pad(flat, (0, rows * LANE - flat.shape[0])).reshape(rows, LANE)


def _unpack(packed, shapes):
    flat = packed.reshape(-1)
    out, off = [], 0
    for s in shapes:
        size = math.prod(s)
        out.append(flat[off:off + size].reshape(s))
        off += size
    return out


def kernel(x, c, ctx, c_ctx, w_ada, b_ada, g_mix, w_in, s5_a_re, s5_a_im, s5_log_step, s5_b_re, s5_b_im, s5_c_re, s5_c_im, s5_d, s5_w_glu, s5_b_glu, sgu_ln_g, sgu_ln_b, sgu_w, sgu_b, w_proj_a, w_proj_b, b_gate, w_out, g_ffn, w_up, conv_w, conv_b, w_down, g_final, loss_target, m_c_ctx, m_w_ada, m_b_ada, m_g_mix, m_w_in, m_s5_a_re, m_s5_a_im, m_s5_log_step, m_s5_b_re, m_s5_b_im, m_s5_c_re, m_s5_c_im, m_s5_d, m_s5_w_glu, m_s5_b_glu, m_sgu_ln_g, m_sgu_ln_b, m_sgu_w, m_sgu_b, m_w_proj_a, m_w_proj_b, m_b_gate, m_w_out, m_g_ffn, m_w_up, m_conv_w, m_conv_b, m_w_down, m_g_final, v_c_ctx, v_w_ada, v_b_ada, v_g_mix, v_w_in, v_s5_a_re, v_s5_a_im, v_s5_log_step, v_s5_b_re, v_s5_b_im, v_s5_c_re, v_s5_c_im, v_s5_d, v_s5_w_glu, v_s5_b_glu, v_sgu_ln_g, v_sgu_ln_b, v_sgu_w, v_sgu_b, v_w_proj_a, v_w_proj_b, v_b_gate, v_w_out, v_g_ffn, v_w_up, v_conv_w, v_conv_b, v_w_down, v_g_final):
    given = dict(locals())
    wts = {n: given[n] for n in WEIGHTS}
    mom1 = {n: given["m_" + n] for n in WEIGHTS}
    mom2 = {n: given["v_" + n] for n in WEIGHTS}

    me = 4 * lax.axis_index("x") + 2 * lax.axis_index("y") + lax.axis_index("c")
    xs, cx, tgt = x[0], ctx[0], loss_target[0]
    n_tok, d = xs.shape
    n_ctx = cx.shape[0]
    s5w = s5_d.shape[1]
    ffn = w_down.shape[1] * N_DEV
    n_mod = w_ada.shape[2] * N_DEV // d
    mod_cols = w_ada.shape[2]

    def two_d(a):
        return a.reshape(-1, a.shape[-1])

    conv_w9 = conv_w[0].reshape(9, -1)
    (c_blocks,) = _exchange([(c, "gather")], "gather_c")
    c_all = c_blocks.reshape(N_DEV, d)

    cs_in = jnp.concatenate([c_all, jnp.broadcast_to(c_ctx[None, :], (N_DEV, d))], axis=0)
    w_ada_loc = w_ada[0]
    (mod_mine,) = _stage_fwd(_fn_mod, [cs_in], [w_ada_loc], [(mod_cols, F32)], blk=2 * N_DEV, name="mod_fwd")
    (mod_blocks,) = _exchange([(mod_mine, "gather")], "gather_mod")
    w_in_own = _b16(w_in[0])
    in_weights, tok_in = _exchange_begin([(w_in_own, "gather")], "gather_in_begin", after=mod_blocks)
    mid_own = [_b16(s5_w_glu[0]), _b16(w_proj_a[0]), _b16(w_proj_b[0])]
    mid_weights, tok_mid = _exchange_begin([(a, "gather") for a in mid_own], "gather_mid_begin",
                                           after=mod_blocks + tok_in)
    late_own = [_b16(w_out[0]), _b16(w_up[0]), conv_w9, _b16(w_down[0])]
    late_weights, tok = _exchange_begin([(a, "gather") for a in late_own], "gather_late_begin",
                                        after=mod_blocks + tok_mid)
    mod_all = _cols_from_blocks(mod_blocks) + b_ada + tok
    mod = lax.dynamic_slice_in_dim(mod_all, me, 1, axis=0)
    mod_c = mod_all[N_DEV:N_DEV + 1]
    sh1, sc1, ga1, sh2, sc2, ga2 = [mod[:, i * d:(i + 1) * d] for i in range(n_mod)]
    sh1c, sc1c = mod_c[:, :d], mod_c[:, d:2 * d]

    a_par = [g_mix, sh1, sc1]
    ac_par = [g_mix, sh1c, sc1c]
    (h,) = _stage_fwd(_fn_a, [xs], a_par, [(d, BF16)], blk=_pick(n_tok, 512, 8), name="modulate1_fwd")
    (hc,) = _stage_fwd(_fn_a, [cx], ac_par, [(d, BF16)], blk=_pick(n_ctx, 512, 8), name="modulate1_ctx_fwd")
    (w_in_land,) = _exchange_end(in_weights, h, "gather_in_end")
    w_in_f = _cols_from_blocks(_with_own(w_in_land, w_in_own, me))
    w_in_u, w_in_rest = w_in_f[:, :s5w], w_in_f[:, s5w:]
    pu = _mm(h, w_in_u, name="proj_u")
    prest = _mm(h, w_in_rest, name="proj_rest")
    puc = _mm(hc, w_in_u, name="proj_u_ctx")

    n_state = s5_a_re.shape[-1]

    def twice(a):
        return jnp.concatenate([a, a], axis=-1)

    s5_prm = [twice(s5_a_re[0])[:, :, None, :], twice(s5_a_im[0])[:, :, None, :], s5_log_step[0][:, :, None, None],
              jnp.concatenate([jnp.swapaxes(s5_b_re[0], 2, 3), jnp.swapaxes(s5_b_im[0], 2, 3)], axis=-1),
              jnp.concatenate([s5_c_re[0], s5_c_im[0]], axis=-1)]
    pu_g, puc_g = _to_groups(_b16(pu)), _to_groups(puc)
    ysc = _from_groups(_s5_fwd(pu_g, puc_g, s5_prm))

    mid = [_with_own(land, own, me) for land, own in zip(_exchange_end(mid_weights, ysc, "gather_mid_end"), mid_own)]
    w_glu_f, w_pa_f, w_pb_f = mid[0].reshape(-1, s5w), _cols_from_blocks(mid[1]), _cols_from_blocks(mid[2])
    b_par = [s5_d, w_glu_f, s5_b_glu, sgu_ln_g, sgu_ln_b, two_d(sgu_w[0]), jnp.transpose(sgu_b[0]),
             w_pa_f, w_pb_f, b_gate]
    b_rows = [pu, prest, ysc]
    b_blk = _pick(n_tok, 512, CHUNK)
    (mpre,) = _stage_fwd(_fn_b, b_rows, b_par, [(d, BF16)], blk=b_blk, name="mixers_fwd")
    late = [_with_own(land, own, me) for land, own in zip(_exchange_end(late_weights, mpre, "gather_late_end"),
                                                          late_own)]
    w_out_f = late[0].reshape(-1, d)
    w_up_f = _cols_from_blocks(late[1])
    w_up_g, w_up_v = w_up_f[:, :ffn], w_up_f[:, ffn:]
    conv_w_f = _cols_from_blocks(late[2])
    w_down_f = late[3].reshape(-1, d)
    mo = _mm(mpre, w_out_f, name="out_proj")

    c_par = [ga1, g_ffn, sh2, sc2]
    c_blk = _pick(n_tok, 512, 8)
    x1, h2 = _stage_fwd(_fn_c, [xs, mo], c_par, [(d, F32), (d, BF16)], blk=c_blk, name="modulate2_fwd")
    up_g = _mm(h2, w_up_g, out_dtype=BF16, name="up_gate")
    up_v = _mm(h2, w_up_v, out_dtype=BF16, name="up_val")
    cw_g, cw_v = conv_w_f[:, :ffn], conv_w_f[:, ffn:]
    cb_g, cb_v = conv_b[:, :ffn], conv_b[:, ffn:]
    act, gate_c, val_c = _conv_act_fwd(up_g, up_v, cw_g, cw_v, cb_g, cb_v)
    dn = _mm(act, w_down_f, name="down_proj")

    loss_part, d_x1a, d_dn, d_ga2, d_g_final = _stage_loss(
        _fn_e, [x1, dn, tgt], [ga2, g_final[None, :]], blk=c_blk, name="loss_head",
        row_grads=[(0, F32), (1, BF16)])

    d_act = _mm(d_dn, w_down_f, tb=True, out_dtype=BF16, name="down_proj_dx")
    g_w_down = _mm(act, d_dn, ta=True, out_dtype=BF16, name="down_proj_dw")

    own_block = {}

    def grad_blocks(name, g):
        blocks = _b16(_blocks_from_cols(g) if name in COL_SHARDED
                      else g.reshape(N_DEV, g.shape[0] // N_DEV, g.shape[1]))
        own_block[name] = lax.dynamic_index_in_dim(blocks, me, 0, keepdims=False)
        return blocks, "a2a"

    sent_down, tok = _exchange_begin([grad_blocks('w_down', g_w_down)], "grads_down_begin")
    dcg, dcv, g_cw_g, g_cw_v, g_cb_g, g_cb_v = _conv_act_bwd(up_g, up_v, gate_c, val_c, d_act)
    dug = _conv_transposed(dcg, cw_g + tok, "conv_dx_gate")
    duv = _conv_transposed(dcv, cw_v, "conv_dx_val")
    d_h2 = _mm(dug, w_up_g, tb=True, name="up_gate_dx")
    d_h2 = _mm(duv, w_up_v, tb=True, add=d_h2, out_dtype=BF16, name="up_val_dx")
    g_w_up = jnp.concatenate([_mm(h2, dug, ta=True, out_dtype=BF16, name="up_gate_dw"),
                              _mm(h2, duv, ta=True, out_dtype=BF16, name="up_val_dw")], axis=1)
    sent_up, tok = _exchange_begin(
        [grad_blocks('w_up', g_w_up), grad_blocks('conv_w', jnp.concatenate([g_cw_g, g_cw_v], axis=1))],
        "grads_up_begin")
    (d_xc, d_mo), (d_ga1, g_g_ffn, d_sh2, d_sc2) = _split(_stage_bwd(
        _fn_c, [xs, mo], [ga1 + tok] + c_par[1:], [d_x1a, d_h2], blk=c_blk, name="modulate2_bwd",
        row_grads=[(0, F32), (1, BF16)]), 2)

    d_mpre = _mm(d_mo, w_out_f, tb=True, out_dtype=BF16, name="out_proj_dx")
    g_w_out = _mm(mpre, d_mo, ta=True, out_dtype=BF16, name="out_proj_dw")
    (d_prest, d_ysc), b_grads = _split(_stage_bwd(
        _fn_b, b_rows, b_par, [d_mpre], blk=_pick(n_tok, 256, CHUNK), name="mixers_bwd",
        row_grads=[(1, BF16), (2, F32)]), 2)
    (g_s5_d, g_w_glu, g_b_glu, g_ln_g, g_ln_b, g_sgu_w, g_sgu_bt, g_w_pa, g_w_pb, g_b_gate) = b_grads

    sent_mix, tok = _exchange_begin(
        [grad_blocks('w_out', g_w_out), grad_blocks('s5_w_glu', g_w_glu), grad_blocks('w_proj_a', g_w_pa),
         grad_blocks('w_proj_b', g_w_pb)], "grads_mixer_begin")
    skip_g = jnp.tile(s5_d.reshape(-1, 1, S5_H), (1, 1, S5_T)) + tok
    s5_out = _s5_bwd(pu_g, puc_g, s5_prm, _to_groups(d_ysc), skip_g)
    g_a_re2, g_a_im2, g_ls, g_bt2, g_c2 = s5_out[2:]
    g_a_re = g_a_re2[..., :n_state] + g_a_re2[..., n_state:]
    g_a_im = g_a_im2[..., :n_state] + g_a_im2[..., n_state:]
    g_bt_re, g_bt_im = g_bt2[..., :n_state], g_bt2[..., n_state:]
    g_c_re, g_c_im = g_c2[..., :n_state], g_c2[..., n_state:]

    part = {
        's5_a_re': g_a_re, 's5_a_im': g_a_im, 's5_log_step': g_ls,
        's5_b_re': jnp.swapaxes(g_bt_re, 2, 3), 's5_b_im': jnp.swapaxes(g_bt_im, 2, 3),
        's5_c_re': g_c_re, 's5_c_im': g_c_im, 's5_d': g_s5_d, 's5_b_glu': g_b_glu, 'sgu_ln_g': g_ln_g,
        'sgu_ln_b': g_ln_b, 'sgu_w': g_sgu_w, 'sgu_b': jnp.transpose(g_sgu_bt), 'b_gate': g_b_gate,
        'g_ffn': g_g_ffn, 'conv_b': jnp.concatenate([g_cb_g, g_cb_v], axis=1), 'g_final': d_g_final,
    }
    early = [n for n in REPLICATED if n not in LATE_REPLICATED and n not in UNPACKED_REPLICATED]
    early_part = _pack([part[n] for n in early])
    own_small = {n: two_d(part[n]) for n in UNPACKED_REPLICATED}
    sent_small, tok = _exchange_begin(
        [(early_part, "gather")] + [(own_small[n], "gather") for n in UNPACKED_REPLICATED], "grads_small_begin")
    d_pu, d_puc = _from_groups(s5_out[0]), _from_groups(s5_out[1]) + tok

    g_w_in_u = _mm(hc, d_puc, ta=True, name="proj_u_ctx_dw")
    g_w_in_u = _mm(h, d_pu, ta=True, add=g_w_in_u, out_dtype=BF16, name="proj_u_dw")
    g_w_in = jnp.concatenate([g_w_in_u, _mm(h, d_prest, ta=True, out_dtype=BF16, name="proj_rest_dw")], axis=1)
    sent_in, tok = _exchange_begin([grad_blocks('w_in', g_w_in)], "grads_in_begin")
    w_in_u_behind = w_in_u + _b16(tok)
    d_h = _mm(d_pu, w_in_u_behind, tb=True, name="proj_u_dx")
    d_h = _mm(d_prest, w_in_rest, tb=True, add=d_h, out_dtype=BF16, name="proj_rest_dx")
    d_hc = _mm(d_puc, w_in_u_behind, tb=True, out_dtype=BF16, name="proj_u_ctx_dx")

    (grad_x,), (g_g_mix_x, d_sh1, d_sc1) = _split(_stage_bwd(
        _fn_a_res, [xs], a_par, [d_h, d_xc], blk=c_blk, name="modulate1_bwd", row_grads=[(0, F32)]), 1)
    _, (g_g_mix_c, d_sh1c, d_sc1c) = _split(_stage_bwd(
        _fn_a, [cx], ac_par, [d_hc], blk=_pick(n_ctx, 512, 8), name="modulate1_ctx_bwd", row_grads=[]), 0)

    zeros = jnp.zeros((1, (n_mod - 2) * d), F32)
    d_mod = jnp.concatenate([d_sh1, d_sc1, d_ga1, d_sh2, d_sc2, d_ga2], axis=1)
    d_mod_c = jnp.concatenate([d_sh1c, d_sc1c, zeros], axis=1)
    (d_mod_all,) = _exchange([(jnp.concatenate([d_mod, d_mod_c], axis=0), "gather")], "gather_dmod")
    d_mod_rows = jnp.transpose(d_mod_all, (1, 0, 2)).reshape(2 * N_DEV, n_mod * d)
    d_mod_mine = lax.dynamic_slice_in_dim(d_mod_rows, me * mod_cols, mod_cols, axis=1)
    (d_cs,), (g_w_ada,) = _split(_stage_bwd(
        _fn_mod, [cs_in], [w_ada_loc], [d_mod_mine], blk=2 * N_DEV, name="mod_bwd", row_grads=[(0, F32)]), 1)

    part.update({'c_ctx': jnp.sum(d_cs[N_DEV:], axis=0), 'b_ada': d_mod + d_mod_c, 'g_mix': g_g_mix_x + g_g_mix_c})
    late_parts, loss_parts = _exchange(
        [(_pack([part[n] for n in LATE_REPLICATED]), "gather"), (jnp.broadcast_to(loss_part, (8, LANE)), "gather")],
        "exchange_grads")

    summed = {}
    small_parts = _exchange_end(sent_small, late_parts, "grads_small_end")
    early_parts = small_parts[0]
    for n, parts in zip(UNPACKED_REPLICATED, small_parts[1:]):
        summed[n], own_block[n] = parts, own_small[n]
    (summed['w_down'],) = _exchange_end(sent_down, late_parts, "grads_down_end")
    summed['w_up'], summed['conv_w'] = _exchange_end(sent_up, late_parts, "grads_up_end")
    summed['w_out'], summed['s5_w_glu'], summed['w_proj_a'], summed['w_proj_b'] = _exchange_end(
        sent_mix, late_parts, "grads_mixer_end")
    (summed['w_in'],) = _exchange_end(sent_in, late_parts, "grads_in_end")

    out = {}
    me_arr = me.reshape(1, 1).astype(jnp.int32)
    for names, parts, own, tag in ((early, early_parts, early_part, "early"),
                                   (LATE_REPLICATED, late_parts, None, "late")):
        res = _adamw(_pack([wts[n] for n in names]), parts, _pack([mom1[n] for n in names]),
                     _pack([mom2[n] for n in names]), "adamw_replicated_" + tag, own=own,
                     me=None if own is None else me_arr)
        res = [_unpack(r, [wts[n].shape for n in names]) for r in res]
        for i, n in enumerate(names):
            out[n] = tuple(r[i] for r in res)
    for n, parts in summed.items():
        shape = wts[n].shape
        res = _adamw(two_d(wts[n]), parts, two_d(mom1[n]), two_d(mom2[n]), "adamw_" + n, own=own_block[n],
                     me=me_arr)
        out[n] = tuple(r.reshape(shape) for r in res)
    res = _adamw(w_ada_loc, g_w_ada[None], m_w_ada[0], v_w_ada[0], "adamw_w_ada")
    out['w_ada'] = tuple(r.reshape(w_ada.shape) for r in res)

    loss = jnp.sum(loss_parts[:, 0, 0])
    return (loss, grad_x[None], *[out[n][0] for n in WEIGHTS], *[out[n][1] for n in WEIGHTS],
            *[out[n][2] for n in WEIGHTS], *[out[n][3] for n in WEIGHTS])


def _split(res, n_rows):
    return tuple(res[:n_rows]), tuple(res[n_rows:])
```

```python
import functools
import math

import jax
import jax.numpy as jnp
from jax import lax
from jax.experimental import pallas as pl
from jax.experimental.pallas import tpu as pltpu

F32 = jnp.float32
BF16 = jnp.bfloat16
HI = lax.Precision.HIGHEST

N_DEV = 8
GRID_W = 64
CHUNK = 128
EPS = 1e-6
S5_T = 32
S5_H = 16
S5_CHUNK_ALIGN = 16
LANE = 128
HALO = 128
VMEM_LIMIT = 56 * 1024 * 1024
PACK_ROWS = 256

ADAM_LR = 0.001
ADAM_B1 = 0.9
ADAM_B2 = 0.999
ADAM_EPS = 1e-08
ADAM_WD = 0.01
ADAM_STEP = 10

WEIGHTS = ['c_ctx', 'w_ada', 'b_ada', 'g_mix', 'w_in', 's5_a_re', 's5_a_im', 's5_log_step', 's5_b_re', 's5_b_im',
           's5_c_re', 's5_c_im', 's5_d', 's5_w_glu', 's5_b_glu', 'sgu_ln_g', 'sgu_ln_b', 'sgu_w', 'sgu_b',
           'w_proj_a', 'w_proj_b', 'b_gate', 'w_out', 'g_ffn', 'w_up', 'conv_w', 'conv_b', 'w_down', 'g_final']
COL_SHARDED = ('w_ada', 'w_in', 'w_proj_a', 'w_proj_b', 'w_up', 'conv_w')
ROW_SHARDED = ('s5_w_glu', 'w_out', 'w_down')
SHARDED = COL_SHARDED + ROW_SHARDED
REPLICATED = [n for n in WEIGHTS if n not in SHARDED]
LATE_REPLICATED = ['c_ctx', 'b_ada', 'g_mix']
UNPACKED_REPLICATED = ['sgu_w', 's5_c_re', 's5_c_im']


def _call(body, **kw):
    return pl.pallas_call(body, **kw)


def _params(n_grid):
    return pltpu.CompilerParams(dimension_semantics=("arbitrary",) * n_grid, vmem_limit_bytes=VMEM_LIMIT)


def _pick(dim, pref, unit=LANE):
    best = None
    d = unit
    while d <= min(dim, pref):
        if dim % d == 0:
            best = d
        d += unit
    return best if best is not None else dim


def _dg(a, b, ca, cb, prec=None):
    return lax.dot_general(a, b, (((ca,), (cb,)), ((), ())), precision=prec, preferred_element_type=F32)


def _b16(v):
    return v.astype(BF16)


@jax.custom_vjp
def mmb(a, b):
    return _dg(_b16(a), _b16(b), 1, 0)


def _mmb_fwd(a, b):
    return mmb(a, b), (a, b)


def _mmb_bwd(res, g):
    a, b = res
    g = _b16(g)
    return _dg(g, _b16(b), 1, 1).astype(a.dtype), _dg(_b16(a), g, 0, 0).astype(b.dtype)


mmb.defvjp(_mmb_fwd, _mmb_bwd)


@jax.custom_vjp
def mmb_nt(a, b):
    return _dg(_b16(a), _b16(b), 1, 1)


def _mmb_nt_fwd(a, b):
    return mmb_nt(a, b), (a, b)


def _mmb_nt_bwd(res, g):
    a, b = res
    g = _b16(g)
    return _dg(g, _b16(b), 1, 0).astype(a.dtype), _dg(g, _b16(a), 0, 0).astype(b.dtype)


mmb_nt.defvjp(_mmb_nt_fwd, _mmb_nt_bwd)


@jax.custom_vjp
def mmf(a, b):
    return _dg(a, b, 1, 0, HI)


def _mmf_fwd(a, b):
    return mmf(a, b), (a, b)


def _mmf_bwd(res, g):
    a, b = res
    return _dg(g, b, 1, 1, HI), _dg(a, g, 0, 0, HI)


mmf.defvjp(_mmf_fwd, _mmf_bwd)


def _dg3(a, b, ca, cb):
    ah, bh = _b16(a), _b16(b)
    al, bl = _b16(a - ah.astype(F32)), _b16(b - bh.astype(F32))
    return _dg(ah, bh, ca, cb) + _dg(ah, bl, ca, cb) + _dg(al, bh, ca, cb)


@jax.custom_vjp
def mm3_nt(a, b):
    return _dg3(a, b, 1, 1)


def _mm3_nt_fwd(a, b):
    return mm3_nt(a, b), (a, b)


def _mm3_nt_bwd(res, g):
    a, b = res
    return _dg3(g, b, 1, 0), _dg3(g, a, 0, 0)


mm3_nt.defvjp(_mm3_nt_fwd, _mm3_nt_bwd)


def _shift_impl(x, k, up):
    n = x.shape[0]
    idx = lax.broadcasted_iota(jnp.int32, (n, 1), 0)
    if up:
        return jnp.where(idx < n - k, pltpu.roll(x, n - k, 0), 0.0)
    return jnp.where(idx >= k, pltpu.roll(x, k, 0), 0.0)


@functools.partial(jax.custom_vjp, nondiff_argnums=(1, 2))
def _shift(x, k, up):
    return _shift_impl(x, k, up)


def _shift_fwd(x, k, up):
    return _shift_impl(x, k, up), None


def _shift_bwd(k, up, _, g):
    return (_shift_impl(g, k, not up),)


_shift.defvjp(_shift_fwd, _shift_bwd)


def _mm(a, b, *, name, ta=False, tb=False, add=None, out_dtype=F32, tm_pref=2048, tn_pref=1408, tk_pref=1408):
    m, k = (a.shape[1], a.shape[0]) if ta else a.shape
    n = b.shape[0] if tb else b.shape[1]
    if add is not None and out_dtype != F32:
        tm_pref = min(tm_pref, 1024)
    if ta:
        tk_pref = max(tk_pref, 2048)
    tm, tn, tk = _pick(m, tm_pref), _pick(n, tn_pref), _pick(k, tk_pref)
    nk = k // tk
    a_spec = (pl.BlockSpec((tk, tm), lambda i, j, kk: (kk, i)) if ta
              else pl.BlockSpec((tm, tk), lambda i, j, kk: (i, kk)))
    b_spec = (pl.BlockSpec((tn, tk), lambda i, j, kk: (j, kk)) if tb
              else pl.BlockSpec((tk, tn), lambda i, j, kk: (kk, j)))
    o_spec = pl.BlockSpec((tm, tn), lambda i, j, kk: (i, j))
    ca, cb = (0 if ta else 1), (1 if tb else 0)
    has_add = add is not None
    in_place = out_dtype == F32 or nk == 1

    def body(*refs):
        a_ref, b_ref = refs[0], refs[1]
        o_ref = refs[3] if has_add else refs[2]
        prod = _dg(_b16(a_ref[...]), _b16(b_ref[...]), ca, cb)
        if nk == 1:
            if has_add:
                prod = prod + refs[2][...].astype(F32)
            o_ref[...] = prod.astype(o_ref.dtype)
            return
        acc_ref = o_ref if in_place else refs[-1]
        kk = pl.program_id(2)

        @pl.when(kk == 0)
        def _():
            acc_ref[...] = prod

        @pl.when(kk > 0)
        def _():
            acc_ref[...] += prod

        if has_add or not in_place:
            @pl.when(kk == nk - 1)
            def _():
                r = acc_ref[...]
                if has_add:
                    r = r + refs[2][...].astype(F32)
                o_ref[...] = r.astype(o_ref.dtype)

    ins = [a, b] + ([add] if has_add else [])
    in_specs = [a_spec, b_spec] + ([o_spec] if has_add else [])
    return _call(body, name=name, grid=(m // tm, n // tn, nk), in_specs=in_specs, out_specs=o_spec,
                 out_shape=jax.ShapeDtypeStruct((m, n), out_dtype),
                 scratch_shapes=[] if in_place else [pltpu.VMEM((tm, tn), F32)],
                 compiler_params=_params(3))(*ins)


def _row_spec(blk, width):
    return pl.BlockSpec((blk, width), lambda i: (i, 0))


def _whole_spec(shape):
    return pl.BlockSpec(shape, lambda i: (0,) * len(shape))


def _stage_fwd(fn, rows, params, outs, *, blk, name, n_rows=None):
    n = n_rows or rows[0].shape[0]
    nr, npar = len(rows), len(params)

    def body(*refs):
        vals = [r[...] for r in refs[:nr + npar]]
        res = fn(*vals)
        for o_ref, v in zip(refs[nr + npar:], res):
            o_ref[...] = v.astype(o_ref.dtype)

    return _call(body, name=name, grid=(n // blk,),
                 in_specs=[_row_spec(blk, r.shape[1]) for r in rows] + [_whole_spec(p.shape) for p in params],
                 out_specs=[_row_spec(blk, w) for w, _ in outs],
                 out_shape=[jax.ShapeDtypeStruct((n, w), dt) for w, dt in outs],
                 compiler_params=_params(1))(*rows, *params)


def _stage_bwd(fn, rows, params, cts, *, blk, name, row_grads, n_rows=None):
    n = n_rows or rows[0].shape[0]
    nr, npar, nct = len(rows), len(params), len(cts)

    def body(*refs):
        vals = [r[...].astype(F32) for r in refs[:nr + npar]]
        ct = [r[...] for r in refs[nr + npar:nr + npar + nct]]
        d_rows = refs[nr + npar + nct:nr + npar + nct + len(row_grads)]
        d_par = refs[nr + npar + nct + len(row_grads):]
        res, vjp = jax.vjp(fn, *vals)
        g = vjp(tuple(c.astype(r.dtype) for c, r in zip(ct, res)))
        for o_ref, (j, _) in zip(d_rows, row_grads):
            o_ref[...] = g[j].astype(o_ref.dtype)

        @pl.when(pl.program_id(0) == 0)
        def _():
            for o_ref in d_par:
                o_ref[...] = jnp.zeros_like(o_ref)

        for j, o_ref in enumerate(d_par):
            o_ref[...] += g[nr + j].astype(F32)

    return _call(body, name=name, grid=(n // blk,),
                 in_specs=([_row_spec(blk, r.shape[1]) for r in rows] + [_whole_spec(p.shape) for p in params]
                           + [_row_spec(blk, c.shape[1]) for c in cts]),
                 out_specs=([_row_spec(blk, rows[j].shape[1]) for j, _ in row_grads]
                            + [_whole_spec(p.shape) for p in params]),
                 out_shape=([jax.ShapeDtypeStruct((n, rows[j].shape[1]), dt) for j, dt in row_grads]
                            + [jax.ShapeDtypeStruct(p.shape, F32) for p in params]),
                 compiler_params=_params(1))(*rows, *params, *cts)


def _stage_loss(fn, rows, params, *, blk, name, row_grads):
    n = rows[0].shape[0]
    nr, npar = len(rows), len(params)

    def body(*refs):
        vals = [r[...].astype(F32) for r in refs[:nr + npar]]
        loss_ref = refs[nr + npar]
        d_rows = refs[nr + npar + 1:nr + npar + 1 + len(row_grads)]
        d_par = refs[nr + npar + 1 + len(row_grads):]
        res, vjp = jax.vjp(fn, *vals)
        g = vjp(jnp.ones_like(res))
        for o_ref, (j, _) in zip(d_rows, row_grads):
            o_ref[...] = g[j].astype(o_ref.dtype)

        @pl.when(pl.program_id(0) == 0)
        def _():
            loss_ref[...] = jnp.zeros_like(loss_ref)
            for o_ref in d_par:
                o_ref[...] = jnp.zeros_like(o_ref)

        loss_ref[...] += res
        for j, o_ref in enumerate(d_par):
            o_ref[...] += g[nr + j].astype(F32)

    return _call(body, name=name, grid=(n // blk,),
                 in_specs=[_row_spec(blk, r.shape[1]) for r in rows] + [_whole_spec(p.shape) for p in params],
                 out_specs=([_whole_spec((1, 1))] + [_row_spec(blk, rows[j].shape[1]) for j, _ in row_grads]
                            + [_whole_spec(p.shape) for p in params]),
                 out_shape=([jax.ShapeDtypeStruct((1, 1), F32)]
                            + [jax.ShapeDtypeStruct((n, rows[j].shape[1]), dt) for j, dt in row_grads]
                            + [jax.ShapeDtypeStruct(p.shape, F32) for p in params]),
                 compiler_params=_params(1))(*rows, *params)


def _rms(x, g):
    return x * lax.rsqrt(jnp.mean(x * x, axis=-1, keepdims=True) + EPS) * g


def _modulate(x, g, shift, scale):
    return _rms(x, g) * (1.0 + scale) + shift


def _fn_mod(cs, w_ada):
    return (mmb(jax.nn.silu(cs), w_ada),)


def _fn_a(x, g_mix, sh, sc):
    return (_b16(_modulate(x, g_mix, sh, sc)),)


def _fn_a_res(x, g_mix, sh, sc):
    return _b16(_modulate(x, g_mix, sh, sc)), x


def _sgu_spatial(v, sgu_w, sgu_bt):
    rows, width = v.shape
    gdim = width // (sgu_w.shape[0] // CHUNK)
    groups = width // gdim
    expand = (lax.broadcasted_iota(jnp.int32, (groups, width), 1) // gdim
              == lax.broadcasted_iota(jnp.int32, (groups, width), 0)).astype(F32)
    bias = mmf(sgu_bt, expand)
    lane = lax.broadcasted_iota(jnp.int32, (CHUNK, LANE), 1)
    per_lane_block = LANE // gdim
    chunks = []
    for ci in range(rows // CHUNK):
        vc = v[ci * CHUNK:(ci + 1) * CHUNK]
        blocks = []
        for lb in range(width // LANE):
            vb = vc[:, lb * LANE:(lb + 1) * LANE]
            acc = None
            for s in range(per_lane_block):
                g = lb * per_lane_block + s
                r = mmb(sgu_w[g * CHUNK:(g + 1) * CHUNK], vb)
                sel = (lane // gdim) == s
                acc = jnp.where(sel, r, 0.0) if acc is None else jnp.where(sel, r, acc)
            blocks.append(acc)
        chunks.append(jnp.concatenate(blocks, axis=1) + bias)
    return jnp.concatenate(chunks, axis=0)


def _fn_b(pu, prest, ysc, s5_d, w_glu, b_glu, ln_g, ln_b, sgu_w, sgu_bt, w_pa, w_pb, b_gate):
    sw = ln_g.shape[1]
    y = jax.nn.gelu(pu * s5_d + ysc)
    ya = y * jax.nn.sigmoid(mmb(y, w_glu) + b_glu)
    z = jax.nn.gelu(prest[:, :2 * sw])
    u, v = z[:, :sw], z[:, sw:]
    vc = v - jnp.mean(v, axis=-1, keepdims=True)
    v = vc * lax.rsqrt(jnp.mean(vc * vc, axis=-1, keepdims=True) + EPS) * ln_g + ln_b
    yb = u * _sgu_spatial(v, sgu_w, sgu_bt)
    gates = jax.nn.sigmoid(prest[:, 2 * sw:] + b_gate)
    d = gates.shape[1] // 2
    return (_b16(gates[:, :d] * mmb(ya, w_pa) + gates[:, d:] * mmb(yb, w_pb)),)


def _fn_c(x, mo, ga1, g_ffn, sh2, sc2):
    x1 = x + ga1 * mo
    return x1, _b16(_modulate(x1, g_ffn, sh2, sc2))


def _fn_e(x1, dn, tgt, ga2, g_final):
    y = _rms(x1 + ga2 * dn, g_final)
    err = (y - tgt) ** 2
    return 0.5 * jnp.sum(jnp.mean(err, axis=-1, keepdims=True), axis=0, keepdims=True)


def _swap_impl(x):
    return pltpu.roll(x, x.shape[1] // 2, 1)


@jax.custom_vjp
def _swap_halves(x):
    return _swap_impl(x)


_swap_halves.defvjp(lambda x: (_swap_impl(x), None), lambda _, g: (_swap_impl(g),))


def _s5_direction(u, mask, a_re, a_im, log_step, bt, c, rev):
    nc, width = u.shape
    t_len = width // S5_H
    n2 = a_re.shape[1]
    lane = lax.broadcasted_iota(jnp.int32, (1, n2), 1)
    sign = jnp.where(lane < n2 // 2, -1.0, 1.0)
    dt = jnp.exp(log_step)
    lr, li = a_re * dt, a_im * dt
    mag = jnp.exp(lr)
    ab_re, ab_im = mag * jnp.cos(li), mag * jnp.sin(li)
    p, q = ab_re - 1.0, ab_im
    den = a_re * a_re + a_im * a_im
    k_re, k_im = (p * a_re + q * a_im) / den, (q * a_re - p * a_im) / den
    bb = k_re * bt + (k_im * sign) * _swap_halves(bt)

    def power(e):
        m = jnp.exp(lr * e)
        return m * jnp.cos(li * e), m * jnp.sin(li * e)

    order = range(t_len - 1, -1, -1) if rev else range(t_len)
    e1 = jnp.concatenate([jnp.full((1, 1, n2), float(t_len - 1 - pos), F32) for pos in order], axis=0)
    lr3, li3, sign3 = lr.reshape(1, 1, n2), li.reshape(1, 1, n2), sign.reshape(1, 1, n2)
    m1, c1, s1 = jnp.exp(lr3 * e1), jnp.cos(li3 * e1), jnp.sin(li3 * e1)
    m2 = jnp.exp(-(lr3 * e1))
    x1r, x1i = m1 * c1, m1 * s1
    x2r, x2i = m2 * c1, -(m2 * s1)
    at_r, at_i = [v.reshape(1, 1, n2) for v in power(float(t_len))]
    x3r, x3i = at_r * x2r - at_i * x2i, at_r * x2i + at_i * x2r

    def rows(xr, xi, z):
        z3, zs3 = z.reshape(1, S5_H, n2), _swap_halves(z).reshape(1, S5_H, n2)
        return (xr * z3 + (xi * sign3) * zs3).reshape(width, n2)

    p_in = rows(x1r, x1i, bb)
    r_out = rows(x2r, x2i, c)
    w_out = rows(x3r, x3i, c)
    toep = mm3_nt(p_in, r_out * (-sign)) * mask

    state = mmb(u, p_in)
    k = 1
    while k < nc:
        ar, ai = power(float(t_len * k))
        moved = _shift(state, k, rev)
        state = state + ar * moved + (ai * sign) * _swap_halves(moved)
        k *= 2
    entering = _shift(state, 1, rev)
    return mmb(u, toep) + mmb_nt(entering, w_out * (-sign))


def _fn_s5(masks, x_chunks, ctx_chunks, *prm):
    nx, nctx = x_chunks.shape[0], ctx_chunks.shape[0]
    pad = (-(nx + 2 * nctx)) % S5_CHUNK_ALIGN
    pieces = [ctx_chunks, x_chunks, ctx_chunks] + ([jnp.zeros((pad, x_chunks.shape[1]), F32)] if pad else [])
    u = jnp.concatenate(pieces, axis=0)
    out = None
    for d in range(2):
        y = _s5_direction(u, masks[d], *[p[d] for p in prm], rev=(d == 1))
        out = y if out is None else out + y
    return out[nctx:nctx + nx]


S5_GROUPS_PER_STEP = 2


def _s5_specs(prm):
    return [pl.BlockSpec((2, S5_GROUPS_PER_STEP) + p.shape[2:], lambda g: (0, g, 0, 0)) for p in prm]


def _group_spec(a):
    return pl.BlockSpec((S5_GROUPS_PER_STEP,) + a.shape[1:], lambda i: (i, 0, 0))


def _s5_masks(width):
    pos = jnp.arange(width) // S5_H
    causal = (pos[None, :] >= pos[:, None]).astype(F32)
    return jnp.stack([causal, causal.T])


def _s5_fwd(x_g, ctx_g, prm):
    masks = _s5_masks(x_g.shape[2])

    def body(*refs):
        for j in range(S5_GROUPS_PER_STEP):
            pv = [r[:, j] for r in refs[3:3 + len(prm)]]
            refs[-1][j] = _fn_s5(refs[0][...], refs[1][j].astype(F32), refs[2][j], *pv)

    return _call(body, name="s5_fwd", grid=(x_g.shape[0] // S5_GROUPS_PER_STEP,),
                 in_specs=[_whole_spec(masks.shape), _group_spec(x_g), _group_spec(ctx_g)] + _s5_specs(prm),
                 out_specs=_group_spec(x_g), out_shape=jax.ShapeDtypeStruct(x_g.shape, F32),
                 compiler_params=_params(1))(masks, x_g, ctx_g, *prm)


def _s5_bwd(x_g, ctx_g, prm, dy_g, skip_g):
    npar = len(prm)
    masks = _s5_masks(x_g.shape[2])

    def body(*refs):
        outs = refs[5 + npar:]
        for j in range(S5_GROUPS_PER_STEP):
            pv = [r[:, j] for r in refs[3:3 + npar]]
            dy = refs[3 + npar][j]
            _, vjp = jax.vjp(functools.partial(_fn_s5, refs[0][...]), refs[1][j].astype(F32), refs[2][j], *pv)
            grads = vjp(dy)
            outs[0][j] = (grads[0] + dy * refs[4 + npar][j]).astype(outs[0].dtype)
            outs[1][j] = grads[1]
            for o_ref, gv in zip(outs[2:], grads[2:]):
                o_ref[:, j] = gv

    return _call(body, name="s5_bwd", grid=(x_g.shape[0] // S5_GROUPS_PER_STEP,),
                 in_specs=([_whole_spec(masks.shape), _group_spec(x_g), _group_spec(ctx_g)] + _s5_specs(prm)
                           + [_group_spec(dy_g), _group_spec(skip_g)]),
                 out_specs=[_group_spec(x_g), _group_spec(ctx_g)] + _s5_specs(prm),
                 out_shape=[jax.ShapeDtypeStruct(x_g.shape, BF16), jax.ShapeDtypeStruct(ctx_g.shape, F32)]
                 + [jax.ShapeDtypeStruct(p.shape, F32) for p in prm],
                 compiler_params=_params(1))(masks, x_g, ctx_g, *prm, dy_g, skip_g)


def _to_groups(tok):
    n, width = tok.shape
    g = width // S5_H
    return jnp.transpose(tok.reshape(n, g, S5_H), (1, 0, 2)).reshape(g, n // S5_T, S5_T * S5_H)


def _from_groups(grp):
    g, nc, _ = grp.shape
    return jnp.transpose(grp.reshape(g, nc * S5_T, S5_H), (1, 0, 2)).reshape(nc * S5_T, g * S5_H)


def _conv_shifted(xp, xm, xn, blk_i, n_blk):
    tb = xm.shape[0]
    xp = jnp.where(blk_i == 0, 0.0, xp.astype(F32))
    xn = jnp.where(blk_i == n_blk - 1, 0.0, xn.astype(F32))
    buf = jnp.concatenate([xp, xm.astype(F32), xn], axis=0)
    n = tb + 2 * HALO
    col = lax.broadcasted_iota(jnp.int32, (n, 1), 0) % GRID_W
    left = jnp.where(col >= 1, pltpu.roll(buf, 1, 0), 0.0)
    right = jnp.where(col <= GRID_W - 2, pltpu.roll(buf, n - 1, 0), 0.0)
    return left, buf, right


def _conv_taps(xp, xm, xn, blk_i, n_blk):
    tb = xm.shape[0]
    shifted = _conv_shifted(xp, xm, xn, blk_i, n_blk)
    taps = []
    for di in range(3):
        start = HALO + (di - 1) * GRID_W
        for dj in range(3):
            taps.append(shifted[dj][start:start + tb])
    return taps


def _conv_sum(taps, w_ref, flip=False):
    acc = None
    for k, tap in enumerate(taps):
        j = len(taps) - 1 - k if flip else k
        term = tap * w_ref[j:j + 1, :]
        acc = term if acc is None else acc + term
    return acc


def _conv_geometry(n_tok, width, tb_pref=1024):
    tb = _pick(n_tok, tb_pref, HALO)
    cb = _pick(width, 256)
    nb = tb // HALO
    last = n_tok // HALO - 1
    main = pl.BlockSpec((tb, cb), lambda j, i: (i, j))
    prev = pl.BlockSpec((HALO, cb), lambda j, i: (jnp.maximum(i * nb - 1, 0), j))
    nxt = pl.BlockSpec((HALO, cb), lambda j, i: (jnp.minimum(i * nb + nb, last), j))
    par = lambda r: pl.BlockSpec((r, cb), lambda j, i: (0, j))
    return tb, cb, main, prev, nxt, par


def _conv_act_fwd(up_g, up_v, w_g, w_v, b_g, b_v):
    n_tok, width = up_g.shape
    tb, cb, main, prev, nxt, par = _conv_geometry(n_tok, width, 2048)
    n_blk = n_tok // tb

    def body(gp, gm, gn, vp, vm, vn, wg, wv, bg, bv, o_ref, gate_ref, val_ref):
        i = pl.program_id(1)
        gate = _conv_sum(_conv_taps(gp[...], gm[...], gn[...], i, n_blk), wg) + bg[...]
        val = _conv_sum(_conv_taps(vp[...], vm[...], vn[...], i, n_blk), wv) + bv[...]
        o_ref[...] = (jax.nn.silu(gate) * val).astype(o_ref.dtype)
        gate_ref[...] = gate
        val_ref[...] = val

    shp = jax.ShapeDtypeStruct
    return _call(body, name="conv_act_fwd", grid=(width // cb, n_tok // tb),
                 in_specs=[prev, main, nxt, prev, main, nxt, par(9), par(9), par(1), par(1)],
                 out_specs=[main, main, main],
                 out_shape=[shp((n_tok, width), BF16), shp((n_tok, width), F32), shp((n_tok, width), F32)],
                 compiler_params=_params(2))(up_g, up_g, up_g, up_v, up_v, up_v, w_g, w_v, b_g, b_v)


def _conv_act_bwd(up_g, up_v, gate_c, val_c, d_act):
    n_tok, width = up_g.shape
    tb, cb, main, prev, nxt, par = _conv_geometry(n_tok, width)
    n_blk = n_tok // tb

    def body(gp, gm, gn, vp, vm, vn, gc, vc, da, dcg, dcv, dwg, dwv, dbg, dbv):
        i = pl.program_id(1)
        taps_g = _conv_taps(gp[...], gm[...], gn[...], i, n_blk)
        taps_v = _conv_taps(vp[...], vm[...], vn[...], i, n_blk)
        gate, val = gc[...], vc[...]
        sig = jax.nn.sigmoid(gate)
        d = da[...].astype(F32)
        d_gate = d * val * sig * (1.0 + gate * (1.0 - sig))
        d_val = d * gate * sig
        dcg[...] = d_gate.astype(dcg.dtype)
        dcv[...] = d_val.astype(dcv.dtype)

        @pl.when(i == 0)
        def _():
            for r in (dwg, dwv, dbg, dbv):
                r[...] = jnp.zeros_like(r)

        dbg[...] += jnp.sum(d_gate, axis=0, keepdims=True)
        dbv[...] += jnp.sum(d_val, axis=0, keepdims=True)
        for k in range(9):
            dwg[k:k + 1, :] += jnp.sum(taps_g[k] * d_gate, axis=0, keepdims=True)
            dwv[k:k + 1, :] += jnp.sum(taps_v[k] * d_val, axis=0, keepdims=True)

    shp = jax.ShapeDtypeStruct
    return _call(body, name="conv_act_bwd", grid=(width // cb, n_tok // tb),
                 in_specs=[prev, main, nxt, prev, main, nxt, main, main, main],
                 out_specs=[main, main, par(9), par(9), par(1), par(1)],
                 out_shape=[shp((n_tok, width), BF16), shp((n_tok, width), BF16), shp((9, width), F32),
                            shp((9, width), F32), shp((1, width), F32), shp((1, width), F32)],
                 compiler_params=_params(2))(up_g, up_g, up_g, up_v, up_v, up_v, gate_c, val_c, d_act)


def _conv_transposed(x, w, name):
    n_tok, width = x.shape
    tb, cb, main, prev, nxt, par = _conv_geometry(n_tok, width, 2048)
    n_blk = n_tok // tb

    def body(xp, xm, xn, w_ref, o_ref):
        taps = _conv_taps(xp[...], xm[...], xn[...], pl.program_id(1), n_blk)
        o_ref[...] = _conv_sum(taps, w_ref, flip=True).astype(o_ref.dtype)

    return _call(body, name=name, grid=(width // cb, n_tok // tb), in_specs=[prev, main, nxt, par(9)],
                 out_specs=main, out_shape=jax.ShapeDtypeStruct((n_tok, width), BF16),
                 compiler_params=_params(2))(x, x, x, w)


def _adamw(w, g_parts, m, v, name, own=None, me=None):
    rows, cols = w.shape
    parts = g_parts.shape[0]
    blk = _pick(rows, 256, 8)
    spec = pl.BlockSpec((blk, cols), lambda i: (i, 0))
    has_own = own is not None

    def body(*refs):
        w_ref, g_ref, m_ref, v_ref = refs[:4]
        g_out, d_out, m_out, v_out = refs[-4:]

        def part(p):
            if has_own:
                return jnp.where(refs[5][...] == p, refs[4][...], g_ref[p]).astype(F32)
            return g_ref[p].astype(F32)

        g = part(0)
        for p in range(1, parts):
            g = g + part(p)
        m_new = ADAM_B1 * m_ref[...] + (1.0 - ADAM_B1) * g
        v_new = ADAM_B2 * v_ref[...] + (1.0 - ADAM_B2) * (g * g)
        m_hat = m_new / (1.0 - ADAM_B1 ** ADAM_STEP)
        v_hat = v_new / (1.0 - ADAM_B2 ** ADAM_STEP)
        g_out[...] = g
        d_out[...] = -ADAM_LR * (m_hat / (jnp.sqrt(v_hat) + ADAM_EPS) + ADAM_WD * w_ref[...])
        m_out[...] = m_new
        v_out[...] = v_new

    extra = [own, me] if has_own else []
    return _call(body, name=name, grid=(rows // blk,),
                 in_specs=([spec, pl.BlockSpec((parts, blk, cols), lambda i: (0, i, 0)), spec, spec]
                           + ([spec, _whole_spec((1, 1))] if has_own else [])),
                 out_specs=[spec] * 4, out_shape=[jax.ShapeDtypeStruct((rows, cols), F32)] * 4,
                 compiler_params=_params(1))(w, g_parts, m, v, *extra)


def _exchange(items, name, after=()):
    n = len(items)
    n_after = len(after)
    hbm = pl.BlockSpec(memory_space=pl.ANY)

    def body(*refs):
        srcs, outs = refs[:n], refs[n + n_after:2 * n + n_after]
        send_sems, recv_sems, own_sems = refs[2 * n + n_after:]
        x, y, c = lax.axis_index("x"), lax.axis_index("y"), lax.axis_index("c")
        me = 4 * x + 2 * y + c
        own = []
        for i, (_, mode) in enumerate(items):
            src = srcs[i] if mode == "gather" else srcs[i].at[me]
            cp = pltpu.make_async_copy(src, outs[i].at[me], own_sems.at[i])
            cp.start()
            own.append(cp)
        sent = []
        for i, (_, mode) in enumerate(items):
            for k in range(1, N_DEV):
                px = 1 - x if k & 4 else x
                py = 1 - y if k & 2 else y
                pc = 1 - c if k & 1 else c
                peer = 4 * px + 2 * py + pc
                src = srcs[i] if mode == "gather" else srcs[i].at[peer]
                cp = pltpu.make_async_remote_copy(
                    src_ref=src, dst_ref=outs[i].at[me], send_sem=send_sems.at[i, k - 1],
                    recv_sem=recv_sems.at[i, k - 1], device_id=(px, py, pc), device_id_type=pl.DeviceIdType.MESH)
                cp.start()
                landing = pltpu.make_async_remote_copy(
                    src_ref=src, dst_ref=outs[i].at[peer], send_sem=send_sems.at[i, k - 1],
                    recv_sem=recv_sems.at[i, k - 1], device_id=(px, py, pc), device_id_type=pl.DeviceIdType.MESH)
                sent.append((cp, landing))
        for cp in own:
            cp.wait()
        for cp, landing in sent:
            cp.wait_send()
            landing.wait_recv()

    out_shape = [jax.ShapeDtypeStruct((N_DEV,) + (a.shape if mode == "gather" else a.shape[1:]), a.dtype)
                 for a, mode in items]
    return _call(body, name=name, in_specs=[hbm] * (n + n_after), out_specs=[hbm] * n, out_shape=out_shape,
                 scratch_shapes=[pltpu.SemaphoreType.DMA((n, N_DEV - 1)), pltpu.SemaphoreType.DMA((n, N_DEV - 1)),
                                 pltpu.SemaphoreType.DMA((n,))])(*[a for a, _ in items], *after)


def _peer(k, x, y, c):
    px = 1 - x if k & 4 else x
    py = 1 - y if k & 2 else y
    pc = 1 - c if k & 1 else c
    return (px, py, pc), 4 * px + 2 * py + pc


_HBM_SPEC = pl.BlockSpec(memory_space=pltpu.HBM)
_SEM_SPEC = pl.BlockSpec(memory_space=pltpu.SEMAPHORE)
_SPLIT_EFFECT = pltpu.SideEffectType.DATAFLOW_SIDE_EFFECTING


def _exchange_begin(items, name, after=None):
    n = len(items)
    modes = [mode for _, mode in items]
    srcs = [pltpu.with_memory_space_constraint(a, pltpu.HBM) for a, _ in items]
    lands = [pltpu.with_memory_space_constraint(
        lax.empty((N_DEV,) + (a.shape if mode == "gather" else a.shape[1:]), a.dtype), pltpu.HBM)
        for a, mode in items]

    def body(*refs):
        src_refs, land_refs = refs[:n], refs[n:2 * n]
        token = refs[-1]
        send_sems, recv_sems = refs[-2 * n - 3], refs[-2 * n - 2]
        x, y, c = lax.axis_index("x"), lax.axis_index("y"), lax.axis_index("c")
        me = 4 * x + 2 * y + c
        for i in range(n):
            for k in range(1, N_DEV):
                coords, peer = _peer(k, x, y, c)
                src = src_refs[i] if modes[i] == "gather" else src_refs[i].at[peer]
                pltpu.make_async_remote_copy(
                    src_ref=src, dst_ref=land_refs[i].at[me], send_sem=send_sems.at[i * (N_DEV - 1) + k - 1],
                    recv_sem=recv_sems.at[i * (N_DEV - 1) + k - 1], device_id=coords,
                    device_id_type=pl.DeviceIdType.MESH).start()
        token[...] = jnp.zeros_like(token)

    sems = pltpu.SemaphoreType.DMA((n * (N_DEV - 1),))
    order = [] if after is None else [after]
    res = _call(body, name=name,
                out_shape=(sems, sems, *[pltpu.HBM(a.shape, a.dtype) for a in srcs + lands],
                           jax.ShapeDtypeStruct((8, LANE), F32)),
                in_specs=[_HBM_SPEC] * (2 * n) + [pl.BlockSpec(memory_space=pl.ANY)] * len(order),
                out_specs=(_SEM_SPEC, _SEM_SPEC, *[_HBM_SPEC] * (2 * n), pl.BlockSpec(memory_space=pltpu.VMEM)),
                input_output_aliases={i: 2 + i for i in range(2 * n)},
                compiler_params=pltpu.CompilerParams(has_side_effects=_SPLIT_EFFECT))(*srcs, *lands, *order)
    return (modes, res[0], res[1], list(res[2:2 + n]), list(res[2 + n:2 + 2 * n])), res[-1][0, 0]


def _exchange_end(handle, after, name):
    modes, send_sems, recv_sems, srcs, lands = handle
    n = len(modes)

    def wait_body(*refs):
        src_refs, land_refs = refs[:n], refs[n:2 * n]
        send, recv = refs[2 * n], refs[2 * n + 1]
        x, y, c = lax.axis_index("x"), lax.axis_index("y"), lax.axis_index("c")
        for i in range(n):
            for k in range(1, N_DEV):
                coords, peer = _peer(k, x, y, c)
                src = src_refs[i] if modes[i] == "gather" else src_refs[i].at[peer]
                cp = pltpu.make_async_remote_copy(
                    src_ref=src, dst_ref=land_refs[i].at[peer], send_sem=send.at[i * (N_DEV - 1) + k - 1],
                    recv_sem=recv.at[i * (N_DEV - 1) + k - 1], device_id=coords,
                    device_id_type=pl.DeviceIdType.MESH)
                cp.wait_send()
                cp.wait_recv()

    res = _call(wait_body, name=name, out_shape=[pltpu.HBM(a.shape, a.dtype) for a in srcs + lands],
                in_specs=[_HBM_SPEC] * (2 * n) + [_SEM_SPEC, _SEM_SPEC, pl.BlockSpec(memory_space=pl.ANY)],
                out_specs=[_HBM_SPEC] * (2 * n), input_output_aliases={i: i for i in range(2 * n)},
                compiler_params=pltpu.CompilerParams(has_side_effects=_SPLIT_EFFECT))(
                    *srcs, *lands, send_sems, recv_sems, after)
    return res[n:]


def _with_own(land, own, me):
    slot = lax.broadcasted_iota(jnp.int32, (N_DEV,) + (1,) * (land.ndim - 1), 0)
    return jnp.where(slot == me, own[None], land)


def _cols_from_blocks(g):
    return jnp.transpose(g, (1, 0, 2)).reshape(g.shape[1], N_DEV * g.shape[2])


def _blocks_from_cols(w):
    r, c8 = w.shape
    return jnp.transpose(w.reshape(r, N_DEV, c8 // N_DEV), (1, 0, 2))


def _pack(arrs):
    flat = jnp.concatenate([a.reshape(-1).astype(F32) for a in arrs])
    rows = -(-flat.shape[0] // LANE)
    rows = -(-rows // PACK_ROWS) * PACK_ROWS if rows > PACK_ROWS else -(-rows // 8) * 8
    return jnp.pad(flat, (0, rows * LANE - flat.shape[0])).reshape(rows, LANE)


def _unpack(packed, shapes):
    flat = packed.reshape(-1)
    out, off = [], 0
    for s in shapes:
        size = math.prod(s)
        out.append(flat[off:off + size].reshape(s))
        off += size
    return out


def kernel(x, c, ctx, c_ctx, w_ada, b_ada, g_mix, w_in, s5_a_re, s5_a_im, s5_log_step, s5_b_re, s5_b_im, s5_c_re, s5_c_im, s5_d, s5_w_glu, s5_b_glu, sgu_ln_g, sgu_ln_b, sgu_w, sgu_b, w_proj_a, w_proj_b, b_gate, w_out, g_ffn, w_up, conv_w, conv_b, w_down, g_final, loss_target, m_c_ctx, m_w_ada, m_b_ada, m_g_mix, m_w_in, m_s5_a_re, m_s5_a_im, m_s5_log_step, m_s5_b_re, m_s5_b_im, m_s5_c_re, m_s5_c_im, m_s5_d, m_s5_w_glu, m_s5_b_glu, m_sgu_ln_g, m_sgu_ln_b, m_sgu_w, m_sgu_b, m_w_proj_a, m_w_proj_b, m_b_gate, m_w_out, m_g_ffn, m_w_up, m_conv_w, m_conv_b, m_w_down, m_g_final, v_c_ctx, v_w_ada, v_b_ada, v_g_mix, v_w_in, v_s5_a_re, v_s5_a_im, v_s5_log_step, v_s5_b_re, v_s5_b_im, v_s5_c_re, v_s5_c_im, v_s5_d, v_s5_w_glu, v_s5_b_glu, v_sgu_ln_g, v_sgu_ln_b, v_sgu_w, v_sgu_b, v_w_proj_a, v_w_proj_b, v_b_gate, v_w_out, v_g_ffn, v_w_up, v_conv_w, v_conv_b, v_w_down, v_g_final):
    given = dict(locals())
    wts = {n: given[n] for n in WEIGHTS}
    mom1 = {n: given["m_" + n] for n in WEIGHTS}
    mom2 = {n: given["v_" + n] for n in WEIGHTS}

    me = 4 * lax.axis_index("x") + 2 * lax.axis_index("y") + lax.axis_index("c")
    xs, cx, tgt = x[0], ctx[0], loss_target[0]
    n_tok, d = xs.shape
    n_ctx = cx.shape[0]
    s5w = s5_d.shape[1]
    ffn = w_down.shape[1] * N_DEV
    n_mod = w_ada.shape[2] * N_DEV // d
    mod_cols = w_ada.shape[2]

    def two_d(a):
        return a.reshape(-1, a.shape[-1])

    conv_w9 = conv_w[0].reshape(9, -1)
    (c_blocks,) = _exchange([(c, "gather")], "gather_c")
    c_all = c_blocks.reshape(N_DEV, d)

    cs_in = jnp.concatenate([c_all, jnp.broadcast_to(c_ctx[None, :], (N_DEV, d))], axis=0)
    w_ada_loc = w_ada[0]
    (mod_mine,) = _stage_fwd(_fn_mod, [cs_in], [w_ada_loc], [(mod_cols, F32)], blk=2 * N_DEV, name="mod_fwd")
    (mod_blocks,) = _exchange([(mod_mine, "gather")], "gather_mod")
    w_in_own = _b16(w_in[0])
    in_weights, tok_in = _exchange_begin([(w_in_own, "gather")], "gather_in_begin", after=mod_blocks)
    mid_own = [_b16(s5_w_glu[0]), _b16(w_proj_a[0]), _b16(w_proj_b[0])]
    mid_weights, tok_mid = _exchange_begin([(a, "gather") for a in mid_own], "gather_mid_begin",
                                           after=mod_blocks + tok_in)
    late_own = [_b16(w_out[0]), _b16(w_up[0]), conv_w9, _b16(w_down[0])]
    late_weights, tok = _exchange_begin([(a, "gather") for a in late_own], "gather_late_begin",
                                        after=mod_blocks + tok_mid)
    mod_all = _cols_from_blocks(mod_blocks) + b_ada + tok
    mod = lax.dynamic_slice_in_dim(mod_all, me, 1, axis=0)
    mod_c = mod_all[N_DEV:N_DEV + 1]
    sh1, sc1, ga1, sh2, sc2, ga2 = [mod[:, i * d:(i + 1) * d] for i in range(n_mod)]
    sh1c, sc1c = mod_c[:, :d], mod_c[:, d:2 * d]

    a_par = [g_mix, sh1, sc1]
    ac_par = [g_mix, sh1c, sc1c]
    (h,) = _stage_fwd(_fn_a, [xs], a_par, [(d, BF16)], blk=_pick(n_tok, 512, 8), name="modulate1_fwd")
    (hc,) = _stage_fwd(_fn_a, [cx], ac_par, [(d, BF16)], blk=_pick(n_ctx, 512, 8), name="modulate1_ctx_fwd")
    (w_in_land,) = _exchange_end(in_weights, h, "gather_in_end")
    w_in_f = _cols_from_blocks(_with_own(w_in_land, w_in_own, me))
    w_in_u, w_in_rest = w_in_f[:, :s5w], w_in_f[:, s5w:]
    pu = _mm(h, w_in_u, name="proj_u")
    prest = _mm(h, w_in_rest, name="proj_rest")
    puc = _mm(hc, w_in_u, name="proj_u_ctx")

    n_state = s5_a_re.shape[-1]

    def twice(a):
        return jnp.concatenate([a, a], axis=-1)

    s5_prm = [twice(s5_a_re[0])[:, :, None, :], twice(s5_a_im[0])[:, :, None, :], s5_log_step[0][:, :, None, None],
              jnp.concatenate([jnp.swapaxes(s5_b_re[0], 2, 3), jnp.swapaxes(s5_b_im[0], 2, 3)], axis=-1),
              jnp.concatenate([s5_c_re[0], s5_c_im[0]], axis=-1)]
    pu_g, puc_g = _to_groups(_b16(pu)), _to_groups(puc)
    ysc = _from_groups(_s5_fwd(pu_g, puc_g, s5_prm))

    mid = [_with_own(land, own, me) for land, own in zip(_exchange_end(mid_weights, ysc, "gather_mid_end"), mid_own)]
    w_glu_f, w_pa_f, w_pb_f = mid[0].reshape(-1, s5w), _cols_from_blocks(mid[1]), _cols_from_blocks(mid[2])
    b_par = [s5_d, w_glu_f, s5_b_glu, sgu_ln_g, sgu_ln_b, two_d(sgu_w[0]), jnp.transpose(sgu_b[0]),
             w_pa_f, w_pb_f, b_gate]
    b_rows = [pu, prest, ysc]
    b_blk = _pick(n_tok, 512, CHUNK)
    (mpre,) = _stage_fwd(_fn_b, b_rows, b_par, [(d, BF16)], blk=b_blk, name="mixers_fwd")
    late = [_with_own(land, own, me) for land, own in zip(_exchange_end(late_weights, mpre, "gather_late_end"),
                                                          late_own)]
    w_out_f = late[0].reshape(-1, d)
    w_up_f = _cols_from_blocks(late[1])
    w_up_g, w_up_v = w_up_f[:, :ffn], w_up_f[:, ffn:]
    conv_w_f = _cols_from_blocks(late[2])
    w_down_f = late[3].reshape(-1, d)
    mo = _mm(mpre, w_out_f, name="out_proj")

    c_par = [ga1, g_ffn, sh2, sc2]
    c_blk = _pick(n_tok, 512, 8)
    x1, h2 = _stage_fwd(_fn_c, [xs, mo], c_par, [(d, F32), (d, BF16)], blk=c_blk, name="modulate2_fwd")
    up_g = _mm(h2, w_up_g, out_dtype=BF16, name="up_gate")
    up_v = _mm(h2, w_up_v, out_dtype=BF16, name="up_val")
    cw_g, cw_v = conv_w_f[:, :ffn], conv_w_f[:, ffn:]
    cb_g, cb_v = conv_b[:, :ffn], conv_b[:, ffn:]
    act, gate_c, val_c = _conv_act_fwd(up_g, up_v, cw_g, cw_v, cb_g, cb_v)
    dn = _mm(act, w_down_f, name="down_proj")

    loss_part, d_x1a, d_dn, d_ga2, d_g_final = _stage_loss(
        _fn_e, [x1, dn, tgt], [ga2, g_final[None, :]], blk=c_blk, name="loss_head",
        row_grads=[(0, F32), (1, BF16)])

    d_act = _mm(d_dn, w_down_f, tb=True, out_dtype=BF16, name="down_proj_dx")
    g_w_down = _mm(act, d_dn, ta=True, out_dtype=BF16, name="down_proj_dw")

    own_block = {}

    def grad_blocks(name, g):
        blocks = _b16(_blocks_from_cols(g) if name in COL_SHARDED
                      else g.reshape(N_DEV, g.shape[0] // N_DEV, g.shape[1]))
        own_block[name] = lax.dynamic_index_in_dim(blocks, me, 0, keepdims=False)
        return blocks, "a2a"

    sent_down, tok = _exchange_begin([grad_blocks('w_down', g_w_down)], "grads_down_begin")
    dcg, dcv, g_cw_g, g_cw_v, g_cb_g, g_cb_v = _conv_act_bwd(up_g, up_v, gate_c, val_c, d_act)
    dug = _conv_transposed(dcg, cw_g + tok, "conv_dx_gate")
    duv = _conv_transposed(dcv, cw_v, "conv_dx_val")
    d_h2 = _mm(dug, w_up_g, tb=True, name="up_gate_dx")
    d_h2 = _mm(duv, w_up_v, tb=True, add=d_h2, out_dtype=BF16, name="up_val_dx")
    g_w_up = jnp.concatenate([_mm(h2, dug, ta=True, out_dtype=BF16, name="up_gate_dw"),
                              _mm(h2, duv, ta=True, out_dtype=BF16, name="up_val_dw")], axis=1)
    sent_up, tok = _exchange_begin(
        [grad_blocks('w_up', g_w_up), grad_blocks('conv_w', jnp.concatenate([g_cw_g, g_cw_v], axis=1))],
        "grads_up_begin")
    (d_xc, d_mo), (d_ga1, g_g_ffn, d_sh2, d_sc2) = _split(_stage_bwd(
        _fn_c, [xs, mo], [ga1 + tok] + c_par[1:], [d_x1a, d_h2], blk=c_blk, name="modulate2_bwd",
        row_grads=[(0, F32), (1, BF16)]), 2)

    d_mpre = _mm(d_mo, w_out_f, tb=True, out_dtype=BF16, name="out_proj_dx")
    g_w_out = _mm(mpre, d_mo, ta=True, out_dtype=BF16, name="out_proj_dw")
    (d_prest, d_ysc), b_grads = _split(_stage_bwd(
        _fn_b, b_rows, b_par, [d_mpre], blk=_pick(n_tok, 256, CHUNK), name="mixers_bwd",
        row_grads=[(1, BF16), (2, F32)]), 2)
    (g_s5_d, g_w_glu, g_b_glu, g_ln_g, g_ln_b, g_sgu_w, g_sgu_bt, g_w_pa, g_w_pb, g_b_gate) = b_grads

    sent_mix, tok = _exchange_begin(
        [grad_blocks('w_out', g_w_out), grad_blocks('s5_w_glu', g_w_glu), grad_blocks('w_proj_a', g_w_pa),
         grad_blocks('w_proj_b', g_w_pb)], "grads_mixer_begin")
    skip_g = jnp.tile(s5_d.reshape(-1, 1, S5_H), (1, 1, S5_T)) + tok
    s5_out = _s5_bwd(pu_g, puc_g, s5_prm, _to_groups(d_ysc), skip_g)
    g_a_re2, g_a_im2, g_ls, g_bt2, g_c2 = s5_out[2:]
    g_a_re = g_a_re2[..., :n_state] + g_a_re2[..., n_state:]
    g_a_im = g_a_im2[..., :n_state] + g_a_im2[..., n_state:]
    g_bt_re, g_bt_im = g_bt2[..., :n_state], g_bt2[..., n_state:]
    g_c_re, g_c_im = g_c2[..., :n_state], g_c2[..., n_state:]

    part = {
        's5_a_re': g_a_re, 's5_a_im': g_a_im, 's5_log_step': g_ls,
        's5_b_re': jnp.swapaxes(g_bt_re, 2, 3), 's5_b_im': jnp.swapaxes(g_bt_im, 2, 3),
        's5_c_re': g_c_re, 's5_c_im': g_c_im, 's5_d': g_s5_d, 's5_b_glu': g_b_glu, 'sgu_ln_g': g_ln_g,
        'sgu_ln_b': g_ln_b, 'sgu_w': g_sgu_w, 'sgu_b': jnp.transpose(g_sgu_bt), 'b_gate': g_b_gate,
        'g_ffn': g_g_ffn, 'conv_b': jnp.concatenate([g_cb_g, g_cb_v], axis=1), 'g_final': d_g_final,
    }
    early = [n for n in REPLICATED if n not in LATE_REPLICATED and n not in UNPACKED_REPLICATED]
    early_part = _pack([part[n] for n in early])
    own_small = {n: two_d(part[n]) for n in UNPACKED_REPLICATED}
    sent_small, tok = _exchange_begin(
        [(early_part, "gather")] + [(own_small[n], "gather") for n in UNPACKED_REPLICATED], "grads_small_begin")
    d_pu, d_puc = _from_groups(s5_out[0]), _from_groups(s5_out[1]) + tok

    g_w_in_u = _mm(hc, d_puc, ta=True, name="proj_u_ctx_dw")
    g_w_in_u = _mm(h, d_pu, ta=True, add=g_w_in_u, out_dtype=BF16, name="proj_u_dw")
    g_w_in = jnp.concatenate([g_w_in_u, _mm(h, d_prest, ta=True, out_dtype=BF16, name="proj_rest_dw")], axis=1)
    sent_in, tok = _exchange_begin([grad_blocks('w_in', g_w_in)], "grads_in_begin")
    w_in_u_behind = w_in_u + _b16(tok)
    d_h = _mm(d_pu, w_in_u_behind, tb=True, name="proj_u_dx")
    d_h = _mm(d_prest, w_in_rest, tb=True, add=d_h, out_dtype=BF16, name="proj_rest_dx")
    d_hc = _mm(d_puc, w_in_u_behind, tb=True, out_dtype=BF16, name="proj_u_ctx_dx")

    (grad_x,), (g_g_mix_x, d_sh1, d_sc1) = _split(_stage_bwd(
        _fn_a_res, [xs], a_par, [d_h, d_xc], blk=c_blk, name="modulate1_bwd", row_grads=[(0, F32)]), 1)
    _, (g_g_mix_c, d_sh1c, d_sc1c) = _split(_stage_bwd(
        _fn_a, [cx], ac_par, [d_hc], blk=_pick(n_ctx, 512, 8), name="modulate1_ctx_bwd", row_grads=[]), 0)

    out, summed = {}, {}
    me_arr = me.reshape(1, 1).astype(jnp.int32)

    def adamw_sharded(names):
        for n in names:
            res = _adamw(two_d(wts[n]), summed[n], two_d(mom1[n]), two_d(mom2[n]), "adamw_" + n, own=own_block[n],
                         me=me_arr)
            out[n] = tuple(r.reshape(wts[n].shape) for r in res)

    (summed['w_down'],) = _exchange_end(sent_down, grad_x, "grads_down_end")
    summed['w_up'], summed['conv_w'] = _exchange_end(sent_up, grad_x, "grads_up_end")
    summed['w_out'], summed['s5_w_glu'], summed['w_proj_a'], summed['w_proj_b'] = _exchange_end(
        sent_mix, grad_x, "grads_mixer_end")
    first = list(summed)
    adamw_sharded(first)

    zeros = jnp.zeros((1, (n_mod - 2) * d), F32)
    d_mod = jnp.concatenate([d_sh1, d_sc1, d_ga1, d_sh2, d_sc2, d_ga2], axis=1)
    d_mod_c = jnp.concatenate([d_sh1c, d_sc1c, zeros], axis=1)
    (d_mod_all,) = _exchange([(jnp.concatenate([d_mod, d_mod_c], axis=0), "gather")], "gather_dmod",
                             after=[out[n][1] for n in first])
    d_mod_rows = jnp.transpose(d_mod_all, (1, 0, 2)).reshape(2 * N_DEV, n_mod * d)
    d_mod_mine = lax.dynamic_slice_in_dim(d_mod_rows, me * mod_cols, mod_cols, axis=1)
    (d_cs,), (g_w_ada,) = _split(_stage_bwd(
        _fn_mod, [cs_in], [w_ada_loc], [d_mod_mine], blk=2 * N_DEV, name="mod_bwd", row_grads=[(0, F32)]), 1)

    part.update({'c_ctx': jnp.sum(d_cs[N_DEV:], axis=0), 'b_ada': d_mod + d_mod_c, 'g_mix': g_g_mix_x + g_g_mix_c})
    late_parts, loss_parts = _exchange(
        [(_pack([part[n] for n in LATE_REPLICATED]), "gather"), (jnp.broadcast_to(loss_part, (8, LANE)), "gather")],
        "exchange_grads")

    small_parts = _exchange_end(sent_small, late_parts, "grads_small_end")
    early_parts = small_parts[0]
    for n, parts in zip(UNPACKED_REPLICATED, small_parts[1:]):
        summed[n], own_block[n] = parts, own_small[n]
    (summed['w_in'],) = _exchange_end(sent_in, late_parts, "grads_in_end")
    adamw_sharded([n for n in summed if n not in first])

    for names, parts, own, tag in ((early, early_parts, early_part, "early"),
                                   (LATE_REPLICATED, late_parts, None, "late")):
        res = _adamw(_pack([wts[n] for n in names]), parts, _pack([mom1[n] for n in names]),
                     _pack([mom2[n] for n in names]), "adamw_replicated_" + tag, own=own,
                     me=None if own is None else me_arr)
        res = [_unpack(r, [wts[n].shape for n in names]) for r in res]
        for i, n in enumerate(names):
            out[n] = tuple(r[i] for r in res)
    res = _adamw(w_ada_loc, g_w_ada[None], m_w_ada[0], v_w_ada[0], "adamw_w_ada")
    out['w_ada'] = tuple(r.reshape(w_ada.shape) for r in res)

    loss = jnp.sum(loss_parts[:, 0, 0])
    return (loss, grad_x[None], *[out[n][0] for n in WEIGHTS], *[out[n][1] for n in WEIGHTS],
            *[out[n][2] for n in WEIGHTS], *[out[n][3] for n in WEIGHTS])


def _split(res, n_rows):
    return tuple(res[:n_rows]), tuple(res[n_rows:])
```

```python
import functools
import math

import jax
import jax.numpy as jnp
from jax import lax
from jax.experimental import pallas as pl
from jax.experimental.pallas import tpu as pltpu

F32 = jnp.float32
BF16 = jnp.bfloat16
HI = lax.Precision.HIGHEST

N_DEV = 8
GRID_W = 64
CHUNK = 128
EPS = 1e-6
S5_T = 32
S5_H = 16
S5_CHUNK_ALIGN = 16
LANE = 128
HALO = 128
VMEM_LIMIT = 56 * 1024 * 1024
PACK_ROWS = 256

ADAM_LR = 0.001
ADAM_B1 = 0.9
ADAM_B2 = 0.999
ADAM_EPS = 1e-08
ADAM_WD = 0.01
ADAM_STEP = 10

WEIGHTS = ['c_ctx', 'w_ada', 'b_ada', 'g_mix', 'w_in', 's5_a_re', 's5_a_im', 's5_log_step', 's5_b_re', 's5_b_im',
           's5_c_re', 's5_c_im', 's5_d', 's5_w_glu', 's5_b_glu', 'sgu_ln_g', 'sgu_ln_b', 'sgu_w', 'sgu_b',
           'w_proj_a', 'w_proj_b', 'b_gate', 'w_out', 'g_ffn', 'w_up', 'conv_w', 'conv_b', 'w_down', 'g_final']
COL_SHARDED = ('w_ada', 'w_in', 'w_proj_a', 'w_proj_b', 'w_up', 'conv_w')
ROW_SHARDED = ('s5_w_glu', 'w_out', 'w_down')
SHARDED = COL_SHARDED + ROW_SHARDED
REPLICATED = [n for n in WEIGHTS if n not in SHARDED]
LATE_REPLICATED = ['c_ctx', 'b_ada', 'g_mix']
UNPACKED_REPLICATED = ['sgu_w', 's5_c_re', 's5_c_im']


def _call(body, **kw):
    return pl.pallas_call(body, **kw)


def _params(n_grid):
    return pltpu.CompilerParams(dimension_semantics=("arbitrary",) * n_grid, vmem_limit_bytes=VMEM_LIMIT)


def _pick(dim, pref, unit=LANE):
    best = None
    d = unit
    while d <= min(dim, pref):
        if dim % d == 0:
            best = d
        d += unit
    return best if best is not None else dim


def _dg(a, b, ca, cb, prec=None):
    return lax.dot_general(a, b, (((ca,), (cb,)), ((), ())), precision=prec, preferred_element_type=F32)


def _b16(v):
    return v.astype(BF16)


@jax.custom_vjp
def mmb(a, b):
    return _dg(_b16(a), _b16(b), 1, 0)


def _mmb_fwd(a, b):
    return mmb(a, b), (a, b)


def _mmb_bwd(res, g):
    a, b = res
    g = _b16(g)
    return _dg(g, _b16(b), 1, 1).astype(a.dtype), _dg(_b16(a), g, 0, 0).astype(b.dtype)


mmb.defvjp(_mmb_fwd, _mmb_bwd)


@jax.custom_vjp
def mmb_nt(a, b):
    return _dg(_b16(a), _b16(b), 1, 1)


def _mmb_nt_fwd(a, b):
    return mmb_nt(a, b), (a, b)


def _mmb_nt_bwd(res, g):
    a, b = res
    g = _b16(g)
    return _dg(g, _b16(b), 1, 0).astype(a.dtype), _dg(g, _b16(a), 0, 0).astype(b.dtype)


mmb_nt.defvjp(_mmb_nt_fwd, _mmb_nt_bwd)


@jax.custom_vjp
def mmf(a, b):
    return _dg(a, b, 1, 0, HI)


def _mmf_fwd(a, b):
    return mmf(a, b), (a, b)


def _mmf_bwd(res, g):
    a, b = res
    return _dg(g, b, 1, 1, HI), _dg(a, g, 0, 0, HI)


mmf.defvjp(_mmf_fwd, _mmf_bwd)


def _dg3(a, b, ca, cb):
    ah, bh = _b16(a), _b16(b)
    al, bl = _b16(a - ah.astype(F32)), _b16(b - bh.astype(F32))
    return _dg(ah, bh, ca, cb) + _dg(ah, bl, ca, cb) + _dg(al, bh, ca, cb)


@jax.custom_vjp
def mm3_nt(a, b):
    return _dg3(a, b, 1, 1)


def _mm3_nt_fwd(a, b):
    return mm3_nt(a, b), (a, b)


def _mm3_nt_bwd(res, g):
    a, b = res
    return _dg3(g, b, 1, 0), _dg3(g, a, 0, 0)


mm3_nt.defvjp(_mm3_nt_fwd, _mm3_nt_bwd)


def _shift_impl(x, k, up):
    n = x.shape[0]
    idx = lax.broadcasted_iota(jnp.int32, (n, 1), 0)
    if up:
        return jnp.where(idx < n - k, pltpu.roll(x, n - k, 0), 0.0)
    return jnp.where(idx >= k, pltpu.roll(x, k, 0), 0.0)


@functools.partial(jax.custom_vjp, nondiff_argnums=(1, 2))
def _shift(x, k, up):
    return _shift_impl(x, k, up)


def _shift_fwd(x, k, up):
    return _shift_impl(x, k, up), None


def _shift_bwd(k, up, _, g):
    return (_shift_impl(g, k, not up),)


_shift.defvjp(_shift_fwd, _shift_bwd)


def _mm(a, b, *, name, ta=False, tb=False, add=None, out_dtype=F32, tm_pref=2048, tn_pref=1408, tk_pref=1408):
    m, k = (a.shape[1], a.shape[0]) if ta else a.shape
    n = b.shape[0] if tb else b.shape[1]
    if add is not None and out_dtype != F32:
        tm_pref = min(tm_pref, 1024)
    if ta:
        tk_pref = max(tk_pref, 2048)
    tm, tn, tk = _pick(m, tm_pref), _pick(n, tn_pref), _pick(k, tk_pref)
    nk = k // tk
    a_spec = (pl.BlockSpec((tk, tm), lambda i, j, kk: (kk, i)) if ta
              else pl.BlockSpec((tm, tk), lambda i, j, kk: (i, kk)))
    b_spec = (pl.BlockSpec((tn, tk), lambda i, j, kk: (j, kk)) if tb
              else pl.BlockSpec((tk, tn), lambda i, j, kk: (kk, j)))
    o_spec = pl.BlockSpec((tm, tn), lambda i, j, kk: (i, j))
    ca, cb = (0 if ta else 1), (1 if tb else 0)
    has_add = add is not None
    in_place = out_dtype == F32 or nk == 1

    def body(*refs):
        a_ref, b_ref = refs[0], refs[1]
        o_ref = refs[3] if has_add else refs[2]
        prod = _dg(_b16(a_ref[...]), _b16(b_ref[...]), ca, cb)
        if nk == 1:
            if has_add:
                prod = prod + refs[2][...].astype(F32)
            o_ref[...] = prod.astype(o_ref.dtype)
            return
        acc_ref = o_ref if in_place else refs[-1]
        kk = pl.program_id(2)

        @pl.when(kk == 0)
        def _():
            acc_ref[...] = prod

        @pl.when(kk > 0)
        def _():
            acc_ref[...] += prod

        if has_add or not in_place:
            @pl.when(kk == nk - 1)
            def _():
                r = acc_ref[...]
                if has_add:
                    r = r + refs[2][...].astype(F32)
                o_ref[...] = r.astype(o_ref.dtype)

    ins = [a, b] + ([add] if has_add else [])
    in_specs = [a_spec, b_spec] + ([o_spec] if has_add else [])
    return _call(body, name=name, grid=(m // tm, n // tn, nk), in_specs=in_specs, out_specs=o_spec,
                 out_shape=jax.ShapeDtypeStruct((m, n), out_dtype),
                 scratch_shapes=[] if in_place else [pltpu.VMEM((tm, tn), F32)],
                 compiler_params=_params(3))(*ins)


def _row_spec(blk, width):
    return pl.BlockSpec((blk, width), lambda i: (i, 0))


def _whole_spec(shape):
    return pl.BlockSpec(shape, lambda i: (0,) * len(shape))


def _stage_fwd(fn, rows, params, outs, *, blk, name, n_rows=None):
    n = n_rows or rows[0].shape[0]
    nr, npar = len(rows), len(params)

    def body(*refs):
        vals = [r[...] for r in refs[:nr + npar]]
        res = fn(*vals)
        for o_ref, v in zip(refs[nr + npar:], res):
            o_ref[...] = v.astype(o_ref.dtype)

    return _call(body, name=name, grid=(n // blk,),
                 in_specs=[_row_spec(blk, r.shape[1]) for r in rows] + [_whole_spec(p.shape) for p in params],
                 out_specs=[_row_spec(blk, w) for w, _ in outs],
                 out_shape=[jax.ShapeDtypeStruct((n, w), dt) for w, dt in outs],
                 compiler_params=_params(1))(*rows, *params)


def _stage_bwd(fn, rows, params, cts, *, blk, name, row_grads, n_rows=None):
    n = n_rows or rows[0].shape[0]
    nr, npar, nct = len(rows), len(params), len(cts)

    def body(*refs):
        vals = [r[...].astype(F32) for r in refs[:nr + npar]]
        ct = [r[...] for r in refs[nr + npar:nr + npar + nct]]
        d_rows = refs[nr + npar + nct:nr + npar + nct + len(row_grads)]
        d_par = refs[nr + npar + nct + len(row_grads):]
        res, vjp = jax.vjp(fn, *vals)
        g = vjp(tuple(c.astype(r.dtype) for c, r in zip(ct, res)))
        for o_ref, (j, _) in zip(d_rows, row_grads):
            o_ref[...] = g[j].astype(o_ref.dtype)

        @pl.when(pl.program_id(0) == 0)
        def _():
            for o_ref in d_par:
                o_ref[...] = jnp.zeros_like(o_ref)

        for j, o_ref in enumerate(d_par):
            o_ref[...] += g[nr + j].astype(F32)

    return _call(body, name=name, grid=(n // blk,),
                 in_specs=([_row_spec(blk, r.shape[1]) for r in rows] + [_whole_spec(p.shape) for p in params]
                           + [_row_spec(blk, c.shape[1]) for c in cts]),
                 out_specs=([_row_spec(blk, rows[j].shape[1]) for j, _ in row_grads]
                            + [_whole_spec(p.shape) for p in params]),
                 out_shape=([jax.ShapeDtypeStruct((n, rows[j].shape[1]), dt) for j, dt in row_grads]
                            + [jax.ShapeDtypeStruct(p.shape, F32) for p in params]),
                 compiler_params=_params(1))(*rows, *params, *cts)


def _stage_loss(fn, rows, params, *, blk, name, row_grads):
    n = rows[0].shape[0]
    nr, npar = len(rows), len(params)

    def body(*refs):
        vals = [r[...].astype(F32) for r in refs[:nr + npar]]
        loss_ref = refs[nr + npar]
        d_rows = refs[nr + npar + 1:nr + npar + 1 + len(row_grads)]
        d_par = refs[nr + npar + 1 + len(row_grads):]
        res, vjp = jax.vjp(fn, *vals)
        g = vjp(jnp.ones_like(res))
        for o_ref, (j, _) in zip(d_rows, row_grads):
            o_ref[...] = g[j].astype(o_ref.dtype)

        @pl.when(pl.program_id(0) == 0)
        def _():
            loss_ref[...] = jnp.zeros_like(loss_ref)
            for o_ref in d_par:
                o_ref[...] = jnp.zeros_like(o_ref)

        loss_ref[...] += res
        for j, o_ref in enumerate(d_par):
            o_ref[...] += g[nr + j].astype(F32)

    return _call(body, name=name, grid=(n // blk,),
                 in_specs=[_row_spec(blk, r.shape[1]) for r in rows] + [_whole_spec(p.shape) for p in params],
                 out_specs=([_whole_spec((1, 1))] + [_row_spec(blk, rows[j].shape[1]) for j, _ in row_grads]
                            + [_whole_spec(p.shape) for p in params]),
                 out_shape=([jax.ShapeDtypeStruct((1, 1), F32)]
                            + [jax.ShapeDtypeStruct((n, rows[j].shape[1]), dt) for j, dt in row_grads]
                            + [jax.ShapeDtypeStruct(p.shape, F32) for p in params]),
                 compiler_params=_params(1))(*rows, *params)


def _rms(x, g):
    return x * lax.rsqrt(jnp.mean(x * x, axis=-1, keepdims=True) + EPS) * g


def _modulate(x, g, shift, scale):
    return _rms(x, g) * (1.0 + scale) + shift


def _fn_mod(cs, w_ada):
    return (mmb(jax.nn.silu(cs), w_ada),)


def _fn_a(x, g_mix, sh, sc):
    return (_b16(_modulate(x, g_mix, sh, sc)),)


def _fn_a_res(x, g_mix, sh, sc):
    return _b16(_modulate(x, g_mix, sh, sc)), x


def _sgu_spatial(v, sgu_w, sgu_bt):
    rows, width = v.shape
    gdim = width // (sgu_w.shape[0] // CHUNK)
    groups = width // gdim
    expand = (lax.broadcasted_iota(jnp.int32, (groups, width), 1) // gdim
              == lax.broadcasted_iota(jnp.int32, (groups, width), 0)).astype(F32)
    bias = mmf(sgu_bt, expand)
    lane = lax.broadcasted_iota(jnp.int32, (CHUNK, LANE), 1)
    per_lane_block = LANE // gdim
    chunks = []
    for ci in range(rows // CHUNK):
        vc = v[ci * CHUNK:(ci + 1) * CHUNK]
        blocks = []
        for lb in range(width // LANE):
            vb = vc[:, lb * LANE:(lb + 1) * LANE]
            acc = None
            for s in range(per_lane_block):
                g = lb * per_lane_block + s
                r = mmb(sgu_w[g * CHUNK:(g + 1) * CHUNK], vb)
                sel = (lane // gdim) == s
                acc = jnp.where(sel, r, 0.0) if acc is None else jnp.where(sel, r, acc)
            blocks.append(acc)
        chunks.append(jnp.concatenate(blocks, axis=1) + bias)
    return jnp.concatenate(chunks, axis=0)


def _fn_b(pu, prest, ysc, s5_d, w_glu, b_glu, ln_g, ln_b, sgu_w, sgu_bt, w_pa, w_pb, b_gate):
    sw = ln_g.shape[1]
    y = jax.nn.gelu(pu * s5_d + ysc)
    ya = y * jax.nn.sigmoid(mmb(y, w_glu) + b_glu)
    z = jax.nn.gelu(prest[:, :2 * sw])
    u, v = z[:, :sw], z[:, sw:]
    vc = v - jnp.mean(v, axis=-1, keepdims=True)
    v = vc * lax.rsqrt(jnp.mean(vc * vc, axis=-1, keepdims=True) + EPS) * ln_g + ln_b
    yb = u * _sgu_spatial(v, sgu_w, sgu_bt)
    gates = jax.nn.sigmoid(prest[:, 2 * sw:] + b_gate)
    d = gates.shape[1] // 2
    return (_b16(gates[:, :d] * mmb(ya, w_pa) + gates[:, d:] * mmb(yb, w_pb)),)


def _fn_c(x, mo, ga1, g_ffn, sh2, sc2):
    x1 = x + ga1 * mo
    return x1, _b16(_modulate(x1, g_ffn, sh2, sc2))


def _fn_e(x1, dn, tgt, ga2, g_final):
    y = _rms(x1 + ga2 * dn, g_final)
    err = (y - tgt) ** 2
    return 0.5 * jnp.sum(jnp.mean(err, axis=-1, keepdims=True), axis=0, keepdims=True)


def _swap_impl(x):
    return pltpu.roll(x, x.shape[1] // 2, 1)


@jax.custom_vjp
def _swap_halves(x):
    return _swap_impl(x)


_swap_halves.defvjp(lambda x: (_swap_impl(x), None), lambda _, g: (_swap_impl(g),))


def _s5_direction(u, mask, a_re, a_im, log_step, bt, c, rev):
    nc, width = u.shape
    t_len = width // S5_H
    n2 = a_re.shape[1]
    lane = lax.broadcasted_iota(jnp.int32, (1, n2), 1)
    sign = jnp.where(lane < n2 // 2, -1.0, 1.0)
    dt = jnp.exp(log_step)
    lr, li = a_re * dt, a_im * dt
    mag = jnp.exp(lr)
    ab_re, ab_im = mag * jnp.cos(li), mag * jnp.sin(li)
    p, q = ab_re - 1.0, ab_im
    den = a_re * a_re + a_im * a_im
    k_re, k_im = (p * a_re + q * a_im) / den, (q * a_re - p * a_im) / den
    bb = k_re * bt + (k_im * sign) * _swap_halves(bt)

    def power(e):
        m = jnp.exp(lr * e)
        return m * jnp.cos(li * e), m * jnp.sin(li * e)

    order = range(t_len - 1, -1, -1) if rev else range(t_len)
    e1 = jnp.concatenate([jnp.full((1, 1, n2), float(t_len - 1 - pos), F32) for pos in order], axis=0)
    lr3, li3, sign3 = lr.reshape(1, 1, n2), li.reshape(1, 1, n2), sign.reshape(1, 1, n2)
    m1, c1, s1 = jnp.exp(lr3 * e1), jnp.cos(li3 * e1), jnp.sin(li3 * e1)
    m2 = jnp.exp(-(lr3 * e1))
    x1r, x1i = m1 * c1, m1 * s1
    x2r, x2i = m2 * c1, -(m2 * s1)
    at_r, at_i = [v.reshape(1, 1, n2) for v in power(float(t_len))]
    x3r, x3i = at_r * x2r - at_i * x2i, at_r * x2i + at_i * x2r

    def rows(xr, xi, z):
        z3, zs3 = z.reshape(1, S5_H, n2), _swap_halves(z).reshape(1, S5_H, n2)
        return (xr * z3 + (xi * sign3) * zs3).reshape(width, n2)

    p_in = rows(x1r, x1i, bb)
    r_out = rows(x2r, x2i, c)
    w_out = rows(x3r, x3i, c)
    toep = mm3_nt(p_in, r_out * (-sign)) * mask

    state = mmb(u, p_in)
    k = 1
    while k < nc:
        ar, ai = power(float(t_len * k))
        moved = _shift(state, k, rev)
        state = state + ar * moved + (ai * sign) * _swap_halves(moved)
        k *= 2
    entering = _shift(state, 1, rev)
    return mmb(u, toep) + mmb_nt(entering, w_out * (-sign))


def _fn_s5(masks, x_chunks, ctx_chunks, *prm):
    nx, nctx = x_chunks.shape[0], ctx_chunks.shape[0]
    pad = (-(nx + 2 * nctx)) % S5_CHUNK_ALIGN
    pieces = [ctx_chunks, x_chunks, ctx_chunks] + ([jnp.zeros((pad, x_chunks.shape[1]), F32)] if pad else [])
    u = jnp.concatenate(pieces, axis=0)
    out = None
    for d in range(2):
        y = _s5_direction(u, masks[d], *[p[d] for p in prm], rev=(d == 1))
        out = y if out is None else out + y
    return out[nctx:nctx + nx]


S5_GROUPS_PER_STEP = 2


def _s5_specs(prm):
    return [pl.BlockSpec((2, S5_GROUPS_PER_STEP) + p.shape[2:], lambda g: (0, g, 0, 0)) for p in prm]


def _group_spec(a):
    return pl.BlockSpec((S5_GROUPS_PER_STEP,) + a.shape[1:], lambda i: (i, 0, 0))


def _s5_masks(width):
    pos = jnp.arange(width) // S5_H
    causal = (pos[None, :] >= pos[:, None]).astype(F32)
    return jnp.stack([causal, causal.T])


def _s5_fwd(x_g, ctx_g, prm):
    masks = _s5_masks(x_g.shape[2])

    def body(*refs):
        for j in range(S5_GROUPS_PER_STEP):
            pv = [r[:, j] for r in refs[3:3 + len(prm)]]
            refs[-1][j] = _fn_s5(refs[0][...], refs[1][j].astype(F32), refs[2][j], *pv)

    return _call(body, name="s5_fwd", grid=(x_g.shape[0] // S5_GROUPS_PER_STEP,),
                 in_specs=[_whole_spec(masks.shape), _group_spec(x_g), _group_spec(ctx_g)] + _s5_specs(prm),
                 out_specs=_group_spec(x_g), out_shape=jax.ShapeDtypeStruct(x_g.shape, F32),
                 compiler_params=_params(1))(masks, x_g, ctx_g, *prm)


def _s5_bwd(x_g, ctx_g, prm, dy_g, skip_g):
    npar = len(prm)
    masks = _s5_masks(x_g.shape[2])

    def body(*refs):
        outs = refs[5 + npar:]
        for j in range(S5_GROUPS_PER_STEP):
            pv = [r[:, j] for r in refs[3:3 + npar]]
            dy = refs[3 + npar][j]
            _, vjp = jax.vjp(functools.partial(_fn_s5, refs[0][...]), refs[1][j].astype(F32), refs[2][j], *pv)
            grads = vjp(dy)
            outs[0][j] = (grads[0] + dy * refs[4 + npar][j]).astype(outs[0].dtype)
            outs[1][j] = grads[1]
            for o_ref, gv in zip(outs[2:], grads[2:]):
                o_ref[:, j] = gv

    return _call(body, name="s5_bwd", grid=(x_g.shape[0] // S5_GROUPS_PER_STEP,),
                 in_specs=([_whole_spec(masks.shape), _group_spec(x_g), _group_spec(ctx_g)] + _s5_specs(prm)
                           + [_group_spec(dy_g), _group_spec(skip_g)]),
                 out_specs=[_group_spec(x_g), _group_spec(ctx_g)] + _s5_specs(prm),
                 out_shape=[jax.ShapeDtypeStruct(x_g.shape, BF16), jax.ShapeDtypeStruct(ctx_g.shape, F32)]
                 + [jax.ShapeDtypeStruct(p.shape, F32) for p in prm],
                 compiler_params=_params(1))(masks, x_g, ctx_g, *prm, dy_g, skip_g)


def _to_groups(tok):
    n, width = tok.shape
    g = width // S5_H
    return jnp.transpose(tok.reshape(n, g, S5_H), (1, 0, 2)).reshape(g, n // S5_T, S5_T * S5_H)


def _from_groups(grp):
    g, nc, _ = grp.shape
    return jnp.transpose(grp.reshape(g, nc * S5_T, S5_H), (1, 0, 2)).reshape(nc * S5_T, g * S5_H)


def _conv_shifted(xp, xm, xn, blk_i, n_blk):
    tb = xm.shape[0]
    xp = jnp.where(blk_i == 0, 0.0, xp.astype(F32))
    xn = jnp.where(blk_i == n_blk - 1, 0.0, xn.astype(F32))
    buf = jnp.concatenate([xp, xm.astype(F32), xn], axis=0)
    n = tb + 2 * HALO
    col = lax.broadcasted_iota(jnp.int32, (n, 1), 0) % GRID_W
    left = jnp.where(col >= 1, pltpu.roll(buf, 1, 0), 0.0)
    right = jnp.where(col <= GRID_W - 2, pltpu.roll(buf, n - 1, 0), 0.0)
    return left, buf, right


def _conv_taps(xp, xm, xn, blk_i, n_blk):
    tb = xm.shape[0]
    shifted = _conv_shifted(xp, xm, xn, blk_i, n_blk)
    taps = []
    for di in range(3):
        start = HALO + (di - 1) * GRID_W
        for dj in range(3):
            taps.append(shifted[dj][start:start + tb])
    return taps


def _conv_sum(taps, w_ref, flip=False):
    acc = None
    for k, tap in enumerate(taps):
        j = len(taps) - 1 - k if flip else k
        term = tap * w_ref[j:j + 1, :]
        acc = term if acc is None else acc + term
    return acc


def _conv_geometry(n_tok, width, tb_pref=1024):
    tb = _pick(n_tok, tb_pref, HALO)
    cb = _pick(width, 256)
    nb = tb // HALO
    last = n_tok // HALO - 1
    main = pl.BlockSpec((tb, cb), lambda j, i: (i, j))
    prev = pl.BlockSpec((HALO, cb), lambda j, i: (jnp.maximum(i * nb - 1, 0), j))
    nxt = pl.BlockSpec((HALO, cb), lambda j, i: (jnp.minimum(i * nb + nb, last), j))
    par = lambda r: pl.BlockSpec((r, cb), lambda j, i: (0, j))
    return tb, cb, main, prev, nxt, par


def _conv_act_fwd(up_g, up_v, w_g, w_v, b_g, b_v):
    n_tok, width = up_g.shape
    tb, cb, main, prev, nxt, par = _conv_geometry(n_tok, width, 2048)
    n_blk = n_tok // tb

    def body(gp, gm, gn, vp, vm, vn, wg, wv, bg, bv, o_ref, gate_ref, val_ref):
        i = pl.program_id(1)
        gate = _conv_sum(_conv_taps(gp[...], gm[...], gn[...], i, n_blk), wg) + bg[...]
        val = _conv_sum(_conv_taps(vp[...], vm[...], vn[...], i, n_blk), wv) + bv[...]
        o_ref[...] = (jax.nn.silu(gate) * val).astype(o_ref.dtype)
        gate_ref[...] = gate
        val_ref[...] = val

    shp = jax.ShapeDtypeStruct
    return _call(body, name="conv_act_fwd", grid=(width // cb, n_tok // tb),
                 in_specs=[prev, main, nxt, prev, main, nxt, par(9), par(9), par(1), par(1)],
                 out_specs=[main, main, main],
                 out_shape=[shp((n_tok, width), BF16), shp((n_tok, width), F32), shp((n_tok, width), F32)],
                 compiler_params=_params(2))(up_g, up_g, up_g, up_v, up_v, up_v, w_g, w_v, b_g, b_v)


def _conv_act_bwd(up_g, up_v, gate_c, val_c, d_act):
    n_tok, width = up_g.shape
    tb, cb, main, prev, nxt, par = _conv_geometry(n_tok, width)
    n_blk = n_tok // tb

    def body(gp, gm, gn, vp, vm, vn, gc, vc, da, dcg, dcv, dwg, dwv, dbg, dbv):
        i = pl.program_id(1)
        taps_g = _conv_taps(gp[...], gm[...], gn[...], i, n_blk)
        taps_v = _conv_taps(vp[...], vm[...], vn[...], i, n_blk)
        gate, val = gc[...], vc[...]
        sig = jax.nn.sigmoid(gate)
        d = da[...].astype(F32)
        d_gate = d * val * sig * (1.0 + gate * (1.0 - sig))
        d_val = d * gate * sig
        dcg[...] = d_gate.astype(dcg.dtype)
        dcv[...] = d_val.astype(dcv.dtype)

        @pl.when(i == 0)
        def _():
            for r in (dwg, dwv, dbg, dbv):
                r[...] = jnp.zeros_like(r)

        dbg[...] += jnp.sum(d_gate, axis=0, keepdims=True)
        dbv[...] += jnp.sum(d_val, axis=0, keepdims=True)
        for k in range(9):
            dwg[k:k + 1, :] += jnp.sum(taps_g[k] * d_gate, axis=0, keepdims=True)
            dwv[k:k + 1, :] += jnp.sum(taps_v[k] * d_val, axis=0, keepdims=True)

    shp = jax.ShapeDtypeStruct
    return _call(body, name="conv_act_bwd", grid=(width // cb, n_tok // tb),
                 in_specs=[prev, main, nxt, prev, main, nxt, main, main, main],
                 out_specs=[main, main, par(9), par(9), par(1), par(1)],
                 out_shape=[shp((n_tok, width), BF16), shp((n_tok, width), BF16), shp((9, width), F32),
                            shp((9, width), F32), shp((1, width), F32), shp((1, width), F32)],
                 compiler_params=_params(2))(up_g, up_g, up_g, up_v, up_v, up_v, gate_c, val_c, d_act)


def _conv_transposed(x, w, name):
    n_tok, width = x.shape
    tb, cb, main, prev, nxt, par = _conv_geometry(n_tok, width, 2048)
    n_blk = n_tok // tb

    def body(xp, xm, xn, w_ref, o_ref):
        taps = _conv_taps(xp[...], xm[...], xn[...], pl.program_id(1), n_blk)
        o_ref[...] = _conv_sum(taps, w_ref, flip=True).astype(o_ref.dtype)

    return _call(body, name=name, grid=(width // cb, n_tok // tb), in_specs=[prev, main, nxt, par(9)],
                 out_specs=main, out_shape=jax.ShapeDtypeStruct((n_tok, width), BF16),
                 compiler_params=_params(2))(x, x, x, w)


def _adamw(w, g_parts, m, v, name, own=None, me=None):
    rows, cols = w.shape
    parts = g_parts.shape[0]
    blk = _pick(rows, 256, 8)
    spec = pl.BlockSpec((blk, cols), lambda i: (i, 0))
    has_own = own is not None

    def body(*refs):
        w_ref, g_ref, m_ref, v_ref = refs[:4]
        g_out, d_out, m_out, v_out = refs[-4:]

        def part(p):
            if has_own:
                return jnp.where(refs[5][...] == p, refs[4][...], g_ref[p]).astype(F32)
            return g_ref[p].astype(F32)

        g = part(0)
        for p in range(1, parts):
            g = g + part(p)
        m_new = ADAM_B1 * m_ref[...] + (1.0 - ADAM_B1) * g
        v_new = ADAM_B2 * v_ref[...] + (1.0 - ADAM_B2) * (g * g)
        m_hat = m_new / (1.0 - ADAM_B1 ** ADAM_STEP)
        v_hat = v_new / (1.0 - ADAM_B2 ** ADAM_STEP)
        g_out[...] = g
        d_out[...] = -ADAM_LR * (m_hat / (jnp.sqrt(v_hat) + ADAM_EPS) + ADAM_WD * w_ref[...])
        m_out[...] = m_new
        v_out[...] = v_new

    extra = [own, me] if has_own else []
    return _call(body, name=name, grid=(rows // blk,),
                 in_specs=([spec, pl.BlockSpec((parts, blk, cols), lambda i: (0, i, 0)), spec, spec]
                           + ([spec, _whole_spec((1, 1))] if has_own else [])),
                 out_specs=[spec] * 4, out_shape=[jax.ShapeDtypeStruct((rows, cols), F32)] * 4,
                 compiler_params=_params(1))(w, g_parts, m, v, *extra)


def _exchange(items, name, after=()):
    n = len(items)
    n_after = len(after)
    hbm = pl.BlockSpec(memory_space=pl.ANY)

    def body(*refs):
        srcs, outs = refs[:n], refs[n + n_after:2 * n + n_after]
        send_sems, recv_sems, own_sems = refs[2 * n + n_after:]
        x, y, c = lax.axis_index("x"), lax.axis_index("y"), lax.axis_index("c")
        me = 4 * x + 2 * y + c
        own = []
        for i, (_, mode) in enumerate(items):
            src = srcs[i] if mode == "gather" else srcs[i].at[me]
            cp = pltpu.make_async_copy(src, outs[i].at[me], own_sems.at[i])
            cp.start()
            own.append(cp)
        sent = []
        for i, (_, mode) in enumerate(items):
            for k in range(1, N_DEV):
                px = 1 - x if k & 4 else x
                py = 1 - y if k & 2 else y
                pc = 1 - c if k & 1 else c
                peer = 4 * px + 2 * py + pc
                src = srcs[i] if mode == "gather" else srcs[i].at[peer]
                cp = pltpu.make_async_remote_copy(
                    src_ref=src, dst_ref=outs[i].at[me], send_sem=send_sems.at[i, k - 1],
                    recv_sem=recv_sems.at[i, k - 1], device_id=(px, py, pc), device_id_type=pl.DeviceIdType.MESH)
                cp.start()
                landing = pltpu.make_async_remote_copy(
                    src_ref=src, dst_ref=outs[i].at[peer], send_sem=send_sems.at[i, k - 1],
                    recv_sem=recv_sems.at[i, k - 1], device_id=(px, py, pc), device_id_type=pl.DeviceIdType.MESH)
                sent.append((cp, landing))
        for cp in own:
            cp.wait()
        for cp, landing in sent:
            cp.wait_send()
            landing.wait_recv()

    out_shape = [jax.ShapeDtypeStruct((N_DEV,) + (a.shape if mode == "gather" else a.shape[1:]), a.dtype)
                 for a, mode in items]
    return _call(body, name=name, in_specs=[hbm] * (n + n_after), out_specs=[hbm] * n, out_shape=out_shape,
                 scratch_shapes=[pltpu.SemaphoreType.DMA((n, N_DEV - 1)), pltpu.SemaphoreType.DMA((n, N_DEV - 1)),
                                 pltpu.SemaphoreType.DMA((n,))])(*[a for a, _ in items], *after)


def _peer(k, x, y, c):
    px = 1 - x if k & 4 else x
    py = 1 - y if k & 2 else y
    pc = 1 - c if k & 1 else c
    return (px, py, pc), 4 * px + 2 * py + pc


_HBM_SPEC = pl.BlockSpec(memory_space=pltpu.HBM)
_SEM_SPEC = pl.BlockSpec(memory_space=pltpu.SEMAPHORE)
_SPLIT_EFFECT = pltpu.SideEffectType.DATAFLOW_SIDE_EFFECTING


def _exchange_begin(items, name, after=None):
    n = len(items)
    modes = [mode for _, mode in items]
    srcs = [pltpu.with_memory_space_constraint(a, pltpu.HBM) for a, _ in items]
    lands = [pltpu.with_memory_space_constraint(
        lax.empty((N_DEV,) + (a.shape if mode == "gather" else a.shape[1:]), a.dtype), pltpu.HBM)
        for a, mode in items]

    def body(*refs):
        src_refs, land_refs = refs[:n], refs[n:2 * n]
        token = refs[-1]
        send_sems, recv_sems = refs[-2 * n - 3], refs[-2 * n - 2]
        x, y, c = lax.axis_index("x"), lax.axis_index("y"), lax.axis_index("c")
        me = 4 * x + 2 * y + c
        for i in range(n):
            for k in range(1, N_DEV):
                coords, peer = _peer(k, x, y, c)
                src = src_refs[i] if modes[i] == "gather" else src_refs[i].at[peer]
                pltpu.make_async_remote_copy(
                    src_ref=src, dst_ref=land_refs[i].at[me], send_sem=send_sems.at[i * (N_DEV - 1) + k - 1],
                    recv_sem=recv_sems.at[i * (N_DEV - 1) + k - 1], device_id=coords,
                    device_id_type=pl.DeviceIdType.MESH).start()
        token[...] = jnp.zeros_like(token)

    sems = pltpu.SemaphoreType.DMA((n * (N_DEV - 1),))
    order = [] if after is None else [after]
    res = _call(body, name=name,
                out_shape=(sems, sems, *[pltpu.HBM(a.shape, a.dtype) for a in srcs + lands],
                           jax.ShapeDtypeStruct((8, LANE), F32)),
                in_specs=[_HBM_SPEC] * (2 * n) + [pl.BlockSpec(memory_space=pl.ANY)] * len(order),
                out_specs=(_SEM_SPEC, _SEM_SPEC, *[_HBM_SPEC] * (2 * n), pl.BlockSpec(memory_space=pltpu.VMEM)),
                input_output_aliases={i: 2 + i for i in range(2 * n)},
                compiler_params=pltpu.CompilerParams(has_side_effects=_SPLIT_EFFECT))(*srcs, *lands, *order)
    return (modes, res[0], res[1], list(res[2:2 + n]), list(res[2 + n:2 + 2 * n])), res[-1][0, 0]


def _exchange_end(handle, after, name):
    modes, send_sems, recv_sems, srcs, lands = handle
    n = len(modes)

    def wait_body(*refs):
        src_refs, land_refs = refs[:n], refs[n:2 * n]
        send, recv = refs[2 * n], refs[2 * n + 1]
        x, y, c = lax.axis_index("x"), lax.axis_index("y"), lax.axis_index("c")
        for i in range(n):
            for k in range(1, N_DEV):
                coords, peer = _peer(k, x, y, c)
                src = src_refs[i] if modes[i] == "gather" else src_refs[i].at[peer]
                cp = pltpu.make_async_remote_copy(
                    src_ref=src, dst_ref=land_refs[i].at[peer], send_sem=send.at[i * (N_DEV - 1) + k - 1],
                    recv_sem=recv.at[i * (N_DEV - 1) + k - 1], device_id=coords,
                    device_id_type=pl.DeviceIdType.MESH)
                cp.wait_send()
                cp.wait_recv()

    res = _call(wait_body, name=name, out_shape=[pltpu.HBM(a.shape, a.dtype) for a in srcs + lands],
                in_specs=[_HBM_SPEC] * (2 * n) + [_SEM_SPEC, _SEM_SPEC, pl.BlockSpec(memory_space=pl.ANY)],
                out_specs=[_HBM_SPEC] * (2 * n), input_output_aliases={i: i for i in range(2 * n)},
                compiler_params=pltpu.CompilerParams(has_side_effects=_SPLIT_EFFECT))(
                    *srcs, *lands, send_sems, recv_sems, after)
    return res[n:]


def _with_own(land, own, me):
    slot = lax.broadcasted_iota(jnp.int32, (N_DEV,) + (1,) * (land.ndim - 1), 0)
    return jnp.where(slot == me, own[None], land)


def _cols_from_blocks(g):
    return jnp.transpose(g, (1, 0, 2)).reshape(g.shape[1], N_DEV * g.shape[2])


def _blocks_from_cols(w):
    r, c8 = w.shape
    return jnp.transpose(w.reshape(r, N_DEV, c8 // N_DEV), (1, 0, 2))


def _pack(arrs):
    flat = jnp.concatenate([a.reshape(-1).astype(F32) for a in arrs])
    rows = -(-flat.shape[0] // LANE)
    rows = -(-rows // PACK_ROWS) * PACK_ROWS if rows > PACK_ROWS else -(-rows // 8) * 8
    return jnp.pad(flat, (0, rows * LANE - flat.shape[0])).reshape(rows, LANE)


def _unpack(packed, shapes):
    flat = packed.reshape(-1)
    out, off = [], 0
    for s in shapes:
        size = math.prod(s)
        out.append(flat[off:off + size].reshape(s))
        off += size
    return out


def kernel(x, c, ctx, c_ctx, w_ada, b_ada, g_mix, w_in, s5_a_re, s5_a_im, s5_log_step, s5_b_re, s5_b_im, s5_c_re, s5_c_im, s5_d, s5_w_glu, s5_b_glu, sgu_ln_g, sgu_ln_b, sgu_w, sgu_b, w_proj_a, w_proj_b, b_gate, w_out, g_ffn, w_up, conv_w, conv_b, w_down, g_final, loss_target, m_c_ctx, m_w_ada, m_b_ada, m_g_mix, m_w_in, m_s5_a_re, m_s5_a_im, m_s5_log_step, m_s5_b_re, m_s5_b_im, m_s5_c_re, m_s5_c_im, m_s5_d, m_s5_w_glu, m_s5_b_glu, m_sgu_ln_g, m_sgu_ln_b, m_sgu_w, m_sgu_b, m_w_proj_a, m_w_proj_b, m_b_gate, m_w_out, m_g_ffn, m_w_up, m_conv_w, m_conv_b, m_w_down, m_g_final, v_c_ctx, v_w_ada, v_b_ada, v_g_mix, v_w_in, v_s5_a_re, v_s5_a_im, v_s5_log_step, v_s5_b_re, v_s5_b_im, v_s5_c_re, v_s5_c_im, v_s5_d, v_s5_w_glu, v_s5_b_glu, v_sgu_ln_g, v_sgu_ln_b, v_sgu_w, v_sgu_b, v_w_proj_a, v_w_proj_b, v_b_gate, v_w_out, v_g_ffn, v_w_up, v_conv_w, v_conv_b, v_w_down, v_g_final):
    given = dict(locals())
    wts = {n: given[n] for n in WEIGHTS}
    mom1 = {n: given["m_" + n] for n in WEIGHTS}
    mom2 = {n: given["v_" + n] for n in WEIGHTS}

    me = 4 * lax.axis_index("x") + 2 * lax.axis_index("y") + lax.axis_index("c")
    xs, cx, tgt = x[0], ctx[0], loss_target[0]
    n_tok, d = xs.shape
    n_ctx = cx.shape[0]
    s5w = s5_d.shape[1]
    ffn = w_down.shape[1] * N_DEV
    n_mod = w_ada.shape[2] * N_DEV // d
    mod_cols = w_ada.shape[2]

    def two_d(a):
        return a.reshape(-1, a.shape[-1])

    conv_w9 = conv_w[0].reshape(9, -1)
    (c_blocks,) = _exchange([(c, "gather")], "gather_c")
    c_all = c_blocks.reshape(N_DEV, d)

    cs_in = jnp.concatenate([c_all, jnp.broadcast_to(c_ctx[None, :], (N_DEV, d))], axis=0)
    w_ada_loc = w_ada[0]
    (mod_mine,) = _stage_fwd(_fn_mod, [cs_in], [w_ada_loc], [(mod_cols, F32)], blk=2 * N_DEV, name="mod_fwd")
    (mod_blocks,) = _exchange([(mod_mine, "gather")], "gather_mod")
    w_in_own = _b16(w_in[0])
    in_weights, tok_in = _exchange_begin([(w_in_own, "gather")], "gather_in_begin", after=mod_blocks)
    mid_own = [_b16(s5_w_glu[0]), _b16(w_proj_a[0]), _b16(w_proj_b[0])]
    mid_weights, tok_mid = _exchange_begin([(a, "gather") for a in mid_own], "gather_mid_begin",
                                           after=mod_blocks + tok_in)
    late_own = [_b16(w_out[0]), _b16(w_up[0]), conv_w9, _b16(w_down[0])]
    late_weights, tok = _exchange_begin([(a, "gather") for a in late_own], "gather_late_begin",
                                        after=mod_blocks + tok_mid)
    mod_all = _cols_from_blocks(mod_blocks) + b_ada + tok
    mod = lax.dynamic_slice_in_dim(mod_all, me, 1, axis=0)
    mod_c = mod_all[N_DEV:N_DEV + 1]
    sh1, sc1, ga1, sh2, sc2, ga2 = [mod[:, i * d:(i + 1) * d] for i in range(n_mod)]
    sh1c, sc1c = mod_c[:, :d], mod_c[:, d:2 * d]

    a_par = [g_mix, sh1, sc1]
    ac_par = [g_mix, sh1c, sc1c]
    (h,) = _stage_fwd(_fn_a, [xs], a_par, [(d, BF16)], blk=_pick(n_tok, 512, 8), name="modulate1_fwd")
    (hc,) = _stage_fwd(_fn_a, [cx], ac_par, [(d, BF16)], blk=_pick(n_ctx, 512, 8), name="modulate1_ctx_fwd")
    (w_in_land,) = _exchange_end(in_weights, h, "gather_in_end")
    w_in_f = _cols_from_blocks(_with_own(w_in_land, w_in_own, me))
    w_in_u, w_in_rest = w_in_f[:, :s5w], w_in_f[:, s5w:]
    pu = _mm(h, w_in_u, name="proj_u")
    prest = _mm(h, w_in_rest, name="proj_rest")
    puc = _mm(hc, w_in_u, name="proj_u_ctx")

    n_state = s5_a_re.shape[-1]

    def twice(a):
        return jnp.concatenate([a, a], axis=-1)

    s5_prm = [twice(s5_a_re[0])[:, :, None, :], twice(s5_a_im[0])[:, :, None, :], s5_log_step[0][:, :, None, None],
              jnp.concatenate([jnp.swapaxes(s5_b_re[0], 2, 3), jnp.swapaxes(s5_b_im[0], 2, 3)], axis=-1),
              jnp.concatenate([s5_c_re[0], s5_c_im[0]], axis=-1)]
    pu_g, puc_g = _to_groups(_b16(pu)), _to_groups(puc)
    ysc = _from_groups(_s5_fwd(pu_g, puc_g, s5_prm))

    mid = [_with_own(land, own, me) for land, own in zip(_exchange_end(mid_weights, ysc, "gather_mid_end"), mid_own)]
    w_glu_f, w_pa_f, w_pb_f = mid[0].reshape(-1, s5w), _cols_from_blocks(mid[1]), _cols_from_blocks(mid[2])
    b_par = [s5_d, w_glu_f, s5_b_glu, sgu_ln_g, sgu_ln_b, two_d(sgu_w[0]), jnp.transpose(sgu_b[0]),
             w_pa_f, w_pb_f, b_gate]
    b_rows = [pu, prest, ysc]
    b_blk = _pick(n_tok, 512, CHUNK)
    (mpre,) = _stage_fwd(_fn_b, b_rows, b_par, [(d, BF16)], blk=b_blk, name="mixers_fwd")
    late = [_with_own(land, own, me) for land, own in zip(_exchange_end(late_weights, mpre, "gather_late_end"),
                                                          late_own)]
    w_out_f = late[0].reshape(-1, d)
    w_up_f = _cols_from_blocks(late[1])
    w_up_g, w_up_v = w_up_f[:, :ffn], w_up_f[:, ffn:]
    conv_w_f = _cols_from_blocks(late[2])
    w_down_f = late[3].reshape(-1, d)
    mo = _mm(mpre, w_out_f, name="out_proj")

    c_par = [ga1, g_ffn, sh2, sc2]
    c_blk = _pick(n_tok, 512, 8)
    x1, h2 = _stage_fwd(_fn_c, [xs, mo], c_par, [(d, F32), (d, BF16)], blk=c_blk, name="modulate2_fwd")
    up_g = _mm(h2, w_up_g, out_dtype=BF16, name="up_gate")
    up_v = _mm(h2, w_up_v, out_dtype=BF16, name="up_val")
    cw_g, cw_v = conv_w_f[:, :ffn], conv_w_f[:, ffn:]
    cb_g, cb_v = conv_b[:, :ffn], conv_b[:, ffn:]
    act, gate_c, val_c = _conv_act_fwd(up_g, up_v, cw_g, cw_v, cb_g, cb_v)
    dn = _mm(act, w_down_f, name="down_proj")

    loss_part, d_x1a, d_dn, d_ga2, d_g_final = _stage_loss(
        _fn_e, [x1, dn, tgt], [ga2, g_final[None, :]], blk=c_blk, name="loss_head",
        row_grads=[(0, F32), (1, BF16)])

    d_act = _mm(d_dn, w_down_f, tb=True, out_dtype=BF16, name="down_proj_dx")
    g_w_down = _mm(act, d_dn, ta=True, out_dtype=BF16, name="down_proj_dw")

    own_block = {}

    def grad_blocks(name, g):
        blocks = _b16(_blocks_from_cols(g) if name in COL_SHARDED
                      else g.reshape(N_DEV, g.shape[0] // N_DEV, g.shape[1]))
        own_block[name] = lax.dynamic_index_in_dim(blocks, me, 0, keepdims=False)
        return blocks, "a2a"

    sent_down, tok = _exchange_begin([grad_blocks('w_down', g_w_down)], "grads_down_begin")
    dcg, dcv, g_cw_g, g_cw_v, g_cb_g, g_cb_v = _conv_act_bwd(up_g, up_v, gate_c, val_c, d_act)
    dug = _conv_transposed(dcg, cw_g + tok, "conv_dx_gate")
    duv = _conv_transposed(dcv, cw_v, "conv_dx_val")
    d_h2 = _mm(dug, w_up_g, tb=True, name="up_gate_dx")
    d_h2 = _mm(duv, w_up_v, tb=True, add=d_h2, out_dtype=BF16, name="up_val_dx")
    g_w_up = jnp.concatenate([_mm(h2, dug, ta=True, out_dtype=BF16, name="up_gate_dw"),
                              _mm(h2, duv, ta=True, out_dtype=BF16, name="up_val_dw")], axis=1)
    sent_up, tok = _exchange_begin(
        [grad_blocks('w_up', g_w_up), grad_blocks('conv_w', jnp.concatenate([g_cw_g, g_cw_v], axis=1))],
        "grads_up_begin")
    (d_xc, d_mo), (d_ga1, g_g_ffn, d_sh2, d_sc2) = _split(_stage_bwd(
        _fn_c, [xs, mo], [ga1 + tok] + c_par[1:], [d_x1a, d_h2], blk=c_blk, name="modulate2_bwd",
        row_grads=[(0, F32), (1, BF16)]), 2)

    d_mpre = _mm(d_mo, w_out_f, tb=True, out_dtype=BF16, name="out_proj_dx")
    g_w_out = _mm(mpre, d_mo, ta=True, out_dtype=BF16, name="out_proj_dw")
    (d_prest, d_ysc), b_grads = _split(_stage_bwd(
        _fn_b, b_rows, b_par, [d_mpre], blk=_pick(n_tok, 256, CHUNK), name="mixers_bwd",
        row_grads=[(1, BF16), (2, F32)]), 2)
    (g_s5_d, g_w_glu, g_b_glu, g_ln_g, g_ln_b, g_sgu_w, g_sgu_bt, g_w_pa, g_w_pb, g_b_gate) = b_grads

    sent_mix, tok = _exchange_begin(
        [grad_blocks('w_out', g_w_out), grad_blocks('s5_w_glu', g_w_glu), grad_blocks('w_proj_a', g_w_pa),
         grad_blocks('w_proj_b', g_w_pb)], "grads_mixer_begin")
    skip_g = jnp.tile(s5_d.reshape(-1, 1, S5_H), (1, 1, S5_T)) + tok
    s5_out = _s5_bwd(pu_g, puc_g, s5_prm, _to_groups(d_ysc), skip_g)
    g_a_re2, g_a_im2, g_ls, g_bt2, g_c2 = s5_out[2:]
    g_a_re = g_a_re2[..., :n_state] + g_a_re2[..., n_state:]
    g_a_im = g_a_im2[..., :n_state] + g_a_im2[..., n_state:]
    g_bt_re, g_bt_im = g_bt2[..., :n_state], g_bt2[..., n_state:]
    g_c_re, g_c_im = g_c2[..., :n_state], g_c2[..., n_state:]

    part = {
        's5_a_re': g_a_re, 's5_a_im': g_a_im, 's5_log_step': g_ls,
        's5_b_re': jnp.swapaxes(g_bt_re, 2, 3), 's5_b_im': jnp.swapaxes(g_bt_im, 2, 3),
        's5_c_re': g_c_re, 's5_c_im': g_c_im, 's5_d': g_s5_d, 's5_b_glu': g_b_glu, 'sgu_ln_g': g_ln_g,
        'sgu_ln_b': g_ln_b, 'sgu_w': g_sgu_w, 'sgu_b': jnp.transpose(g_sgu_bt), 'b_gate': g_b_gate,
        'g_ffn': g_g_ffn, 'conv_b': jnp.concatenate([g_cb_g, g_cb_v], axis=1), 'g_final': d_g_final,
    }
    early = [n for n in REPLICATED if n not in LATE_REPLICATED and n not in UNPACKED_REPLICATED]
    early_part = _pack([part[n] for n in early])
    own_small = {n: two_d(part[n]) for n in UNPACKED_REPLICATED}
    sent_small, tok = _exchange_begin(
        [(early_part, "gather")] + [(own_small[n], "gather") for n in UNPACKED_REPLICATED], "grads_small_begin")
    d_pu, d_puc = _from_groups(s5_out[0]), _from_groups(s5_out[1]) + tok

    g_w_in_u = _mm(hc, d_puc, ta=True, name="proj_u_ctx_dw")
    g_w_in_u = _mm(h, d_pu, ta=True, add=g_w_in_u, out_dtype=BF16, name="proj_u_dw")
    g_w_in = jnp.concatenate([g_w_in_u, _mm(h, d_prest, ta=True, out_dtype=BF16, name="proj_rest_dw")], axis=1)
    sent_in, tok = _exchange_begin([grad_blocks('w_in', g_w_in)], "grads_in_begin")
    w_in_u_behind = w_in_u + _b16(tok)
    d_h = _mm(d_pu, w_in_u_behind, tb=True, name="proj_u_dx")
    d_h = _mm(d_prest, w_in_rest, tb=True, add=d_h, out_dtype=BF16, name="proj_rest_dx")
    d_hc = _mm(d_puc, w_in_u_behind, tb=True, out_dtype=BF16, name="proj_u_ctx_dx")

    (grad_x,), (g_g_mix_x, d_sh1, d_sc1) = _split(_stage_bwd(
        _fn_a_res, [xs], a_par, [d_h, d_xc], blk=c_blk, name="modulate1_bwd", row_grads=[(0, F32)]), 1)
    _, (g_g_mix_c, d_sh1c, d_sc1c) = _split(_stage_bwd(
        _fn_a, [cx], ac_par, [d_hc], blk=_pick(n_ctx, 512, 8), name="modulate1_ctx_bwd", row_grads=[]), 0)

    out, summed = {}, {}
    me_arr = me.reshape(1, 1).astype(jnp.int32)

    def adamw_sharded(names):
        for n in names:
            res = _adamw(two_d(wts[n]), summed[n], two_d(mom1[n]), two_d(mom2[n]), "adamw_" + n, own=own_block[n],
                         me=me_arr)
            out[n] = tuple(r.reshape(wts[n].shape) for r in res)

    (summed['w_down'],) = _exchange_end(sent_down, grad_x, "grads_down_end")
    summed['w_up'], summed['conv_w'] = _exchange_end(sent_up, grad_x, "grads_up_end")
    summed['w_out'], summed['s5_w_glu'], summed['w_proj_a'], summed['w_proj_b'] = _exchange_end(
        sent_mix, grad_x, "grads_mixer_end")
    small_parts = _exchange_end(sent_small, grad_x, "grads_small_end")
    early_parts = small_parts[0]
    for n, parts in zip(UNPACKED_REPLICATED, small_parts[1:]):
        summed[n], own_block[n] = parts, own_small[n]
    first = list(summed)
    adamw_sharded(first)

    def adamw_packed(names, parts, own, tag):
        res = _adamw(_pack([wts[n] for n in names]), parts, _pack([mom1[n] for n in names]),
                     _pack([mom2[n] for n in names]), "adamw_replicated_" + tag, own=own,
                     me=None if own is None else me_arr)
        res = [_unpack(r, [wts[n].shape for n in names]) for r in res]
        for i, n in enumerate(names):
            out[n] = tuple(r[i] for r in res)
        return res[1][0]

    early_done = adamw_packed(early, early_parts, early_part, "early")

    zeros = jnp.zeros((1, (n_mod - 2) * d), F32)
    d_mod = jnp.concatenate([d_sh1, d_sc1, d_ga1, d_sh2, d_sc2, d_ga2], axis=1)
    d_mod_c = jnp.concatenate([d_sh1c, d_sc1c, zeros], axis=1)
    (d_mod_all,) = _exchange([(jnp.concatenate([d_mod, d_mod_c], axis=0), "gather")], "gather_dmod",
                             after=[out[n][1] for n in first] + [early_done])
    d_mod_rows = jnp.transpose(d_mod_all, (1, 0, 2)).reshape(2 * N_DEV, n_mod * d)
    d_mod_mine = lax.dynamic_slice_in_dim(d_mod_rows, me * mod_cols, mod_cols, axis=1)
    (d_cs,), (g_w_ada,) = _split(_stage_bwd(
        _fn_mod, [cs_in], [w_ada_loc], [d_mod_mine], blk=2 * N_DEV, name="mod_bwd", row_grads=[(0, F32)]), 1)

    part.update({'c_ctx': jnp.sum(d_cs[N_DEV:], axis=0), 'b_ada': d_mod + d_mod_c, 'g_mix': g_g_mix_x + g_g_mix_c})
    late_parts, loss_parts = _exchange(
        [(_pack([part[n] for n in LATE_REPLICATED]), "gather"), (jnp.broadcast_to(loss_part, (8, LANE)), "gather")],
        "exchange_grads")

    (summed['w_in'],) = _exchange_end(sent_in, late_parts, "grads_in_end")
    adamw_sharded([n for n in summed if n not in first])

    adamw_packed(LATE_REPLICATED, late_parts, None, "late")
    res = _adamw(w_ada_loc, g_w_ada[None], m_w_ada[0], v_w_ada[0], "adamw_w_ada")
    out['w_ada'] = tuple(r.reshape(w_ada.shape) for r in res)

    loss = jnp.sum(loss_parts[:, 0, 0])
    return (loss, grad_x[None], *[out[n][0] for n in WEIGHTS], *[out[n][1] for n in WEIGHTS],
            *[out[n][2] for n in WEIGHTS], *[out[n][3] for n in WEIGHTS])


def _split(res, n_rows):
    return tuple(res[:n_rows]), tuple(res[n_rows:])
```

```python
import functools
import math

import jax
import jax.numpy as jnp
from jax import lax
from jax.experimental import pallas as pl
from jax.experimental.pallas import tpu as pltpu

F32 = jnp.float32
BF16 = jnp.bfloat16
HI = lax.Precision.HIGHEST

N_DEV = 8
GRID_W = 64
CHUNK = 128
EPS = 1e-6
S5_T = 32
S5_H = 16
S5_CHUNK_ALIGN = 16
LANE = 128
HALO = 128
VMEM_LIMIT = 56 * 1024 * 1024
PACK_ROWS = 256

ADAM_LR = 0.001
ADAM_B1 = 0.9
ADAM_B2 = 0.999
ADAM_EPS = 1e-08
ADAM_WD = 0.01
ADAM_STEP = 10

WEIGHTS = ['c_ctx', 'w_ada', 'b_ada', 'g_mix', 'w_in', 's5_a_re', 's5_a_im', 's5_log_step', 's5_b_re', 's5_b_im',
           's5_c_re', 's5_c_im', 's5_d', 's5_w_glu', 's5_b_glu', 'sgu_ln_g', 'sgu_ln_b', 'sgu_w', 'sgu_b',
           'w_proj_a', 'w_proj_b', 'b_gate', 'w_out', 'g_ffn', 'w_up', 'conv_w', 'conv_b', 'w_down', 'g_final']
COL_SHARDED = ('w_ada', 'w_in', 'w_proj_a', 'w_proj_b', 'w_up', 'conv_w')
ROW_SHARDED = ('s5_w_glu', 'w_out', 'w_down')
SHARDED = COL_SHARDED + ROW_SHARDED
REPLICATED = [n for n in WEIGHTS if n not in SHARDED]
LATE_REPLICATED = ['c_ctx', 'b_ada', 'g_mix']
UNPACKED_REPLICATED = ['sgu_w', 's5_c_re', 's5_c_im']


def _call(body, **kw):
    return pl.pallas_call(body, **kw)


def _params(n_grid):
    return pltpu.CompilerParams(dimension_semantics=("arbitrary",) * n_grid, vmem_limit_bytes=VMEM_LIMIT)


def _pick(dim, pref, unit=LANE):
    best = None
    d = unit
    while d <= min(dim, pref):
        if dim % d == 0:
            best = d
        d += unit
    return best if best is not None else dim


def _dg(a, b, ca, cb, prec=None):
    return lax.dot_general(a, b, (((ca,), (cb,)), ((), ())), precision=prec, preferred_element_type=F32)


def _b16(v):
    return v.astype(BF16)


@jax.custom_vjp
def mmb(a, b):
    return _dg(_b16(a), _b16(b), 1, 0)


def _mmb_fwd(a, b):
    return mmb(a, b), (a, b)


def _mmb_bwd(res, g):
    a, b = res
    g = _b16(g)
    return _dg(g, _b16(b), 1, 1).astype(a.dtype), _dg(_b16(a), g, 0, 0).astype(b.dtype)


mmb.defvjp(_mmb_fwd, _mmb_bwd)


@jax.custom_vjp
def mmb_nt(a, b):
    return _dg(_b16(a), _b16(b), 1, 1)


def _mmb_nt_fwd(a, b):
    return mmb_nt(a, b), (a, b)


def _mmb_nt_bwd(res, g):
    a, b = res
    g = _b16(g)
    return _dg(g, _b16(b), 1, 0).astype(a.dtype), _dg(g, _b16(a), 0, 0).astype(b.dtype)


mmb_nt.defvjp(_mmb_nt_fwd, _mmb_nt_bwd)


@jax.custom_vjp
def mmf(a, b):
    return _dg(a, b, 1, 0, HI)


def _mmf_fwd(a, b):
    return mmf(a, b), (a, b)


def _mmf_bwd(res, g):
    a, b = res
    return _dg(g, b, 1, 1, HI), _dg(a, g, 0, 0, HI)


mmf.defvjp(_mmf_fwd, _mmf_bwd)


def _dg3(a, b, ca, cb):
    ah, bh = _b16(a), _b16(b)
    al, bl = _b16(a - ah.astype(F32)), _b16(b - bh.astype(F32))
    return _dg(ah, bh, ca, cb) + _dg(ah, bl, ca, cb) + _dg(al, bh, ca, cb)


@jax.custom_vjp
def mm3_nt(a, b):
    return _dg3(a, b, 1, 1)


def _mm3_nt_fwd(a, b):
    return mm3_nt(a, b), (a, b)


def _mm3_nt_bwd(res, g):
    a, b = res
    return _dg3(g, b, 1, 0), _dg3(g, a, 0, 0)


mm3_nt.defvjp(_mm3_nt_fwd, _mm3_nt_bwd)


def _shift_impl(x, k, up):
    n = x.shape[0]
    idx = lax.broadcasted_iota(jnp.int32, (n, 1), 0)
    if up:
        return jnp.where(idx < n - k, pltpu.roll(x, n - k, 0), 0.0)
    return jnp.where(idx >= k, pltpu.roll(x, k, 0), 0.0)


@functools.partial(jax.custom_vjp, nondiff_argnums=(1, 2))
def _shift(x, k, up):
    return _shift_impl(x, k, up)


def _shift_fwd(x, k, up):
    return _shift_impl(x, k, up), None


def _shift_bwd(k, up, _, g):
    return (_shift_impl(g, k, not up),)


_shift.defvjp(_shift_fwd, _shift_bwd)


def _mm(a, b, *, name, ta=False, tb=False, add=None, out_dtype=F32, tm_pref=2048, tn_pref=1408, tk_pref=1408):
    m, k = (a.shape[1], a.shape[0]) if ta else a.shape
    n = b.shape[0] if tb else b.shape[1]
    if add is not None and out_dtype != F32:
        tm_pref = min(tm_pref, 1024)
    if ta:
        tk_pref = max(tk_pref, 2048)
    tm, tn, tk = _pick(m, tm_pref), _pick(n, tn_pref), _pick(k, tk_pref)
    nk = k // tk
    a_spec = (pl.BlockSpec((tk, tm), lambda i, j, kk: (kk, i)) if ta
              else pl.BlockSpec((tm, tk), lambda i, j, kk: (i, kk)))
    b_spec = (pl.BlockSpec((tn, tk), lambda i, j, kk: (j, kk)) if tb
              else pl.BlockSpec((tk, tn), lambda i, j, kk: (kk, j)))
    o_spec = pl.BlockSpec((tm, tn), lambda i, j, kk: (i, j))
    ca, cb = (0 if ta else 1), (1 if tb else 0)
    has_add = add is not None
    in_place = out_dtype == F32 or nk == 1

    def body(*refs):
        a_ref, b_ref = refs[0], refs[1]
        o_ref = refs[3] if has_add else refs[2]
        prod = _dg(_b16(a_ref[...]), _b16(b_ref[...]), ca, cb)
        if nk == 1:
            if has_add:
                prod = prod + refs[2][...].astype(F32)
            o_ref[...] = prod.astype(o_ref.dtype)
            return
        acc_ref = o_ref if in_place else refs[-1]
        kk = pl.program_id(2)

        @pl.when(kk == 0)
        def _():
            acc_ref[...] = prod

        @pl.when(kk > 0)
        def _():
            acc_ref[...] += prod

        if has_add or not in_place:
            @pl.when(kk == nk - 1)
            def _():
                r = acc_ref[...]
                if has_add:
                    r = r + refs[2][...].astype(F32)
                o_ref[...] = r.astype(o_ref.dtype)

    ins = [a, b] + ([add] if has_add else [])
    in_specs = [a_spec, b_spec] + ([o_spec] if has_add else [])
    return _call(body, name=name, grid=(m // tm, n // tn, nk), in_specs=in_specs, out_specs=o_spec,
                 out_shape=jax.ShapeDtypeStruct((m, n), out_dtype),
                 scratch_shapes=[] if in_place else [pltpu.VMEM((tm, tn), F32)],
                 compiler_params=_params(3))(*ins)


def _row_spec(blk, width):
    return pl.BlockSpec((blk, width), lambda i: (i, 0))


def _whole_spec(shape):
    return pl.BlockSpec(shape, lambda i: (0,) * len(shape))


def _stage_fwd(fn, rows, params, outs, *, blk, name, n_rows=None):
    n = n_rows or rows[0].shape[0]
    nr, npar = len(rows), len(params)

    def body(*refs):
        vals = [r[...] for r in refs[:nr + npar]]
        res = fn(*vals)
        for o_ref, v in zip(refs[nr + npar:], res):
            o_ref[...] = v.astype(o_ref.dtype)

    return _call(body, name=name, grid=(n // blk,),
                 in_specs=[_row_spec(blk, r.shape[1]) for r in rows] + [_whole_spec(p.shape) for p in params],
                 out_specs=[_row_spec(blk, w) for w, _ in outs],
                 out_shape=[jax.ShapeDtypeStruct((n, w), dt) for w, dt in outs],
                 compiler_params=_params(1))(*rows, *params)


def _stage_bwd(fn, rows, params, cts, *, blk, name, row_grads, n_rows=None):
    n = n_rows or rows[0].shape[0]
    nr, npar, nct = len(rows), len(params), len(cts)

    def body(*refs):
        vals = [r[...].astype(F32) for r in refs[:nr + npar]]
        ct = [r[...] for r in refs[nr + npar:nr + npar + nct]]
        d_rows = refs[nr + npar + nct:nr + npar + nct + len(row_grads)]
        d_par = refs[nr + npar + nct + len(row_grads):]
        res, vjp = jax.vjp(fn, *vals)
        g = vjp(tuple(c.astype(r.dtype) for c, r in zip(ct, res)))
        for o_ref, (j, _) in zip(d_rows, row_grads):
            o_ref[...] = g[j].astype(o_ref.dtype)

        @pl.when(pl.program_id(0) == 0)
        def _():
            for o_ref in d_par:
                o_ref[...] = jnp.zeros_like(o_ref)

        for j, o_ref in enumerate(d_par):
            o_ref[...] += g[nr + j].astype(F32)

    return _call(body, name=name, grid=(n // blk,),
                 in_specs=([_row_spec(blk, r.shape[1]) for r in rows] + [_whole_spec(p.shape) for p in params]
                           + [_row_spec(blk, c.shape[1]) for c in cts]),
                 out_specs=([_row_spec(blk, rows[j].shape[1]) for j, _ in row_grads]
                            + [_whole_spec(p.shape) for p in params]),
                 out_shape=([jax.ShapeDtypeStruct((n, rows[j].shape[1]), dt) for j, dt in row_grads]
                            + [jax.ShapeDtypeStruct(p.shape, F32) for p in params]),
                 compiler_params=_params(1))(*rows, *params, *cts)


def _stage_loss(fn, rows, params, *, blk, name, row_grads):
    n = rows[0].shape[0]
    nr, npar = len(rows), len(params)

    def body(*refs):
        vals = [r[...].astype(F32) for r in refs[:nr + npar]]
        loss_ref = refs[nr + npar]
        d_rows = refs[nr + npar + 1:nr + npar + 1 + len(row_grads)]
        d_par = refs[nr + npar + 1 + len(row_grads):]
        res, vjp = jax.vjp(fn, *vals)
        g = vjp(jnp.ones_like(res))
        for o_ref, (j, _) in zip(d_rows, row_grads):
            o_ref[...] = g[j].astype(o_ref.dtype)

        @pl.when(pl.program_id(0) == 0)
        def _():
            loss_ref[...] = jnp.zeros_like(loss_ref)
            for o_ref in d_par:
                o_ref[...] = jnp.zeros_like(o_ref)

        loss_ref[...] += res
        for j, o_ref in enumerate(d_par):
            o_ref[...] += g[nr + j].astype(F32)

    return _call(body, name=name, grid=(n // blk,),
                 in_specs=[_row_spec(blk, r.shape[1]) for r in rows] + [_whole_spec(p.shape) for p in params],
                 out_specs=([_whole_spec((1, 1))] + [_row_spec(blk, rows[j].shape[1]) for j, _ in row_grads]
                            + [_whole_spec(p.shape) for p in params]),
                 out_shape=([jax.ShapeDtypeStruct((1, 1), F32)]
                            + [jax.ShapeDtypeStruct((n, rows[j].shape[1]), dt) for j, dt in row_grads]
                            + [jax.ShapeDtypeStruct(p.shape, F32) for p in params]),
                 compiler_params=_params(1))(*rows, *params)


def _rms(x, g):
    return x * lax.rsqrt(jnp.mean(x * x, axis=-1, keepdims=True) + EPS) * g


def _modulate(x, g, shift, scale):
    return _rms(x, g) * (1.0 + scale) + shift


def _fn_mod(cs, w_ada):
    return (mmb(jax.nn.silu(cs), w_ada),)


def _fn_a(x, g_mix, sh, sc):
    return (_b16(_modulate(x, g_mix, sh, sc)),)


def _fn_a_res(x, g_mix, sh, sc):
    return _b16(_modulate(x, g_mix, sh, sc)), x


def _sgu_spatial(v, sgu_w, sgu_bt):
    rows, width = v.shape
    gdim = width // (sgu_w.shape[0] // CHUNK)
    groups = width // gdim
    expand = (lax.broadcasted_iota(jnp.int32, (groups, width), 1) // gdim
              == lax.broadcasted_iota(jnp.int32, (groups, width), 0)).astype(F32)
    bias = mmf(sgu_bt, expand)
    lane = lax.broadcasted_iota(jnp.int32, (CHUNK, LANE), 1)
    per_lane_block = LANE // gdim
    chunks = []
    for ci in range(rows // CHUNK):
        vc = v[ci * CHUNK:(ci + 1) * CHUNK]
        blocks = []
        for lb in range(width // LANE):
            vb = vc[:, lb * LANE:(lb + 1) * LANE]
            acc = None
            for s in range(per_lane_block):
                g = lb * per_lane_block + s
                r = mmb(sgu_w[g * CHUNK:(g + 1) * CHUNK], vb)
                sel = (lane // gdim) == s
                acc = jnp.where(sel, r, 0.0) if acc is None else jnp.where(sel, r, acc)
            blocks.append(acc)
        chunks.append(jnp.concatenate(blocks, axis=1) + bias)
    return jnp.concatenate(chunks, axis=0)


def _fn_b(pu, prest, ysc, s5_d, w_glu, b_glu, ln_g, ln_b, sgu_w, sgu_bt, w_pa, w_pb, b_gate):
    sw = ln_g.shape[1]
    y = jax.nn.gelu(pu * s5_d + ysc)
    ya = y * jax.nn.sigmoid(mmb(y, w_glu) + b_glu)
    z = jax.nn.gelu(prest[:, :2 * sw])
    u, v = z[:, :sw], z[:, sw:]
    vc = v - jnp.mean(v, axis=-1, keepdims=True)
    v = vc * lax.rsqrt(jnp.mean(vc * vc, axis=-1, keepdims=True) + EPS) * ln_g + ln_b
    yb = u * _sgu_spatial(v, sgu_w, sgu_bt)
    gates = jax.nn.sigmoid(prest[:, 2 * sw:] + b_gate)
    d = gates.shape[1] // 2
    return (_b16(gates[:, :d] * mmb(ya, w_pa) + gates[:, d:] * mmb(yb, w_pb)),)


def _fn_c(x, mo, ga1, g_ffn, sh2, sc2):
    x1 = x + ga1 * mo
    return x1, _b16(_modulate(x1, g_ffn, sh2, sc2))


def _fn_e(x1, dn, tgt, ga2, g_final):
    y = _rms(x1 + ga2 * dn, g_final)
    err = (y - tgt) ** 2
    return 0.5 * jnp.sum(jnp.mean(err, axis=-1, keepdims=True), axis=0, keepdims=True)


def _swap_impl(x):
    return pltpu.roll(x, x.shape[1] // 2, 1)


@jax.custom_vjp
def _swap_halves(x):
    return _swap_impl(x)


_swap_halves.defvjp(lambda x: (_swap_impl(x), None), lambda _, g: (_swap_impl(g),))


def _s5_direction(u, mask, a_re, a_im, log_step, bt, c, rev):
    nc, width = u.shape
    t_len = width // S5_H
    n2 = a_re.shape[1]
    lane = lax.broadcasted_iota(jnp.int32, (1, n2), 1)
    sign = jnp.where(lane < n2 // 2, -1.0, 1.0)
    dt = jnp.exp(log_step)
    lr, li = a_re * dt, a_im * dt
    mag = jnp.exp(lr)
    ab_re, ab_im = mag * jnp.cos(li), mag * jnp.sin(li)
    p, q = ab_re - 1.0, ab_im
    den = a_re * a_re + a_im * a_im
    k_re, k_im = (p * a_re + q * a_im) / den, (q * a_re - p * a_im) / den
    bb = k_re * bt + (k_im * sign) * _swap_halves(bt)

    def power(e):
        m = jnp.exp(lr * e)
        return m * jnp.cos(li * e), m * jnp.sin(li * e)

    order = range(t_len - 1, -1, -1) if rev else range(t_len)
    e1 = jnp.concatenate([jnp.full((1, 1, n2), float(t_len - 1 - pos), F32) for pos in order], axis=0)
    lr3, li3, sign3 = lr.reshape(1, 1, n2), li.reshape(1, 1, n2), sign.reshape(1, 1, n2)
    m1, c1, s1 = jnp.exp(lr3 * e1), jnp.cos(li3 * e1), jnp.sin(li3 * e1)
    m2 = jnp.exp(-(lr3 * e1))
    x1r, x1i = m1 * c1, m1 * s1
    x2r, x2i = m2 * c1, -(m2 * s1)
    at_r, at_i = [v.reshape(1, 1, n2) for v in power(float(t_len))]
    x3r, x3i = at_r * x2r - at_i * x2i, at_r * x2i + at_i * x2r

    def rows(xr, xi, z):
        z3, zs3 = z.reshape(1, S5_H, n2), _swap_halves(z).reshape(1, S5_H, n2)
        return (xr * z3 + (xi * sign3) * zs3).reshape(width, n2)

    p_in = rows(x1r, x1i, bb)
    r_out = rows(x2r, x2i, c)
    w_out = rows(x3r, x3i, c)
    toep = mm3_nt(p_in, r_out * (-sign)) * mask

    state = mmb(u, p_in)
    k = 1
    while k < nc:
        ar, ai = power(float(t_len * k))
        moved = _shift(state, k, rev)
        state = state + ar * moved + (ai * sign) * _swap_halves(moved)
        k *= 2
    entering = _shift(state, 1, rev)
    return mmb(u, toep) + mmb_nt(entering, w_out * (-sign))


def _fn_s5(masks, x_chunks, ctx_chunks, *prm):
    nx, nctx = x_chunks.shape[0], ctx_chunks.shape[0]
    pad = (-(nx + 2 * nctx)) % S5_CHUNK_ALIGN
    pieces = [ctx_chunks, x_chunks, ctx_chunks] + ([jnp.zeros((pad, x_chunks.shape[1]), F32)] if pad else [])
    u = jnp.concatenate(pieces, axis=0)
    out = None
    for d in range(2):
        y = _s5_direction(u, masks[d], *[p[d] for p in prm], rev=(d == 1))
        out = y if out is None else out + y
    return out[nctx:nctx + nx]


S5_GROUPS_PER_STEP = 2


S5_GROUPS_PER_STEP_FWD = 4


def _s5_specs(prm, gps=S5_GROUPS_PER_STEP):
    return [pl.BlockSpec((2, gps) + p.shape[2:], lambda g: (0, g, 0, 0)) for p in prm]


def _group_spec(a, gps=S5_GROUPS_PER_STEP):
    return pl.BlockSpec((gps,) + a.shape[1:], lambda i: (i, 0, 0))


def _s5_masks(width):
    pos = jnp.arange(width) // S5_H
    causal = (pos[None, :] >= pos[:, None]).astype(F32)
    return jnp.stack([causal, causal.T])


def _s5_fwd(x_g, ctx_g, prm):
    masks = _s5_masks(x_g.shape[2])

    gps = S5_GROUPS_PER_STEP_FWD

    def body(*refs):
        for j in range(gps):
            pv = [r[:, j] for r in refs[3:3 + len(prm)]]
            refs[-1][j] = _fn_s5(refs[0][...], refs[1][j].astype(F32), refs[2][j], *pv)

    return _call(body, name="s5_fwd", grid=(x_g.shape[0] // gps,),
                 in_specs=([_whole_spec(masks.shape), _group_spec(x_g, gps), _group_spec(ctx_g, gps)]
                           + _s5_specs(prm, gps)),
                 out_specs=_group_spec(x_g, gps), out_shape=jax.ShapeDtypeStruct(x_g.shape, F32),
                 compiler_params=_params(1))(masks, x_g, ctx_g, *prm)


def _s5_bwd(x_g, ctx_g, prm, dy_g, skip_g):
    npar = len(prm)
    masks = _s5_masks(x_g.shape[2])

    def body(*refs):
        outs = refs[5 + npar:]
        for j in range(S5_GROUPS_PER_STEP):
            pv = [r[:, j] for r in refs[3:3 + npar]]
            dy = refs[3 + npar][j]
            _, vjp = jax.vjp(functools.partial(_fn_s5, refs[0][...]), refs[1][j].astype(F32), refs[2][j], *pv)
            grads = vjp(dy)
            outs[0][j] = (grads[0] + dy * refs[4 + npar][j]).astype(outs[0].dtype)
            outs[1][j] = grads[1]
            for o_ref, gv in zip(outs[2:], grads[2:]):
                o_ref[:, j] = gv

    return _call(body, name="s5_bwd", grid=(x_g.shape[0] // S5_GROUPS_PER_STEP,),
                 in_specs=([_whole_spec(masks.shape), _group_spec(x_g), _group_spec(ctx_g)] + _s5_specs(prm)
                           + [_group_spec(dy_g), _group_spec(skip_g)]),
                 out_specs=[_group_spec(x_g), _group_spec(ctx_g)] + _s5_specs(prm),
                 out_shape=[jax.ShapeDtypeStruct(x_g.shape, BF16), jax.ShapeDtypeStruct(ctx_g.shape, F32)]
                 + [jax.ShapeDtypeStruct(p.shape, F32) for p in prm],
                 compiler_params=_params(1))(masks, x_g, ctx_g, *prm, dy_g, skip_g)


def _to_groups(tok):
    n, width = tok.shape
    g = width // S5_H
    return jnp.transpose(tok.reshape(n, g, S5_H), (1, 0, 2)).reshape(g, n // S5_T, S5_T * S5_H)


def _from_groups(grp):
    g, nc, _ = grp.shape
    return jnp.transpose(grp.reshape(g, nc * S5_T, S5_H), (1, 0, 2)).reshape(nc * S5_T, g * S5_H)


def _conv_shifted(xp, xm, xn, blk_i, n_blk):
    tb = xm.shape[0]
    xp = jnp.where(blk_i == 0, 0.0, xp.astype(F32))
    xn = jnp.where(blk_i == n_blk - 1, 0.0, xn.astype(F32))
    buf = jnp.concatenate([xp, xm.astype(F32), xn], axis=0)
    n = tb + 2 * HALO
    col = lax.broadcasted_iota(jnp.int32, (n, 1), 0) % GRID_W
    left = jnp.where(col >= 1, pltpu.roll(buf, 1, 0), 0.0)
    right = jnp.where(col <= GRID_W - 2, pltpu.roll(buf, n - 1, 0), 0.0)
    return left, buf, right


def _conv_taps(xp, xm, xn, blk_i, n_blk):
    tb = xm.shape[0]
    shifted = _conv_shifted(xp, xm, xn, blk_i, n_blk)
    taps = []
    for di in range(3):
        start = HALO + (di - 1) * GRID_W
        for dj in range(3):
            taps.append(shifted[dj][start:start + tb])
    return taps


def _conv_sum(taps, w_ref, flip=False):
    acc = None
    for k, tap in enumerate(taps):
        j = len(taps) - 1 - k if flip else k
        term = tap * w_ref[j:j + 1, :]
        acc = term if acc is None else acc + term
    return acc


def _conv_geometry(n_tok, width, tb_pref=1024):
    tb = _pick(n_tok, tb_pref, HALO)
    cb = _pick(width, 256)
    nb = tb // HALO
    last = n_tok // HALO - 1
    main = pl.BlockSpec((tb, cb), lambda j, i: (i, j))
    prev = pl.BlockSpec((HALO, cb), lambda j, i: (jnp.maximum(i * nb - 1, 0), j))
    nxt = pl.BlockSpec((HALO, cb), lambda j, i: (jnp.minimum(i * nb + nb, last), j))
    par = lambda r: pl.BlockSpec((r, cb), lambda j, i: (0, j))
    return tb, cb, main, prev, nxt, par


def _conv_act_fwd(up_g, up_v, w_g, w_v, b_g, b_v):
    n_tok, width = up_g.shape
    tb, cb, main, prev, nxt, par = _conv_geometry(n_tok, width, 2048)
    n_blk = n_tok // tb

    def body(gp, gm, gn, vp, vm, vn, wg, wv, bg, bv, o_ref, gate_ref, val_ref):
        i = pl.program_id(1)
        gate = _conv_sum(_conv_taps(gp[...], gm[...], gn[...], i, n_blk), wg) + bg[...]
        val = _conv_sum(_conv_taps(vp[...], vm[...], vn[...], i, n_blk), wv) + bv[...]
        o_ref[...] = (jax.nn.silu(gate) * val).astype(o_ref.dtype)
        gate_ref[...] = gate
        val_ref[...] = val

    shp = jax.ShapeDtypeStruct
    return _call(body, name="conv_act_fwd", grid=(width // cb, n_tok // tb),
                 in_specs=[prev, main, nxt, prev, main, nxt, par(9), par(9), par(1), par(1)],
                 out_specs=[main, main, main],
                 out_shape=[shp((n_tok, width), BF16), shp((n_tok, width), F32), shp((n_tok, width), F32)],
                 compiler_params=_params(2))(up_g, up_g, up_g, up_v, up_v, up_v, w_g, w_v, b_g, b_v)


def _conv_act_bwd(up_g, up_v, gate_c, val_c, d_act):
    n_tok, width = up_g.shape
    tb, cb, main, prev, nxt, par = _conv_geometry(n_tok, width)
    n_blk = n_tok // tb

    def body(gp, gm, gn, vp, vm, vn, gc, vc, da, dcg, dcv, dwg, dwv, dbg, dbv):
        i = pl.program_id(1)
        taps_g = _conv_taps(gp[...], gm[...], gn[...], i, n_blk)
        taps_v = _conv_taps(vp[...], vm[...], vn[...], i, n_blk)
        gate, val = gc[...], vc[...]
        sig = jax.nn.sigmoid(gate)
        d = da[...].astype(F32)
        d_gate = d * val * sig * (1.0 + gate * (1.0 - sig))
        d_val = d * gate * sig
        dcg[...] = d_gate.astype(dcg.dtype)
        dcv[...] = d_val.astype(dcv.dtype)

        @pl.when(i == 0)
        def _():
            for r in (dwg, dwv, dbg, dbv):
                r[...] = jnp.zeros_like(r)

        dbg[...] += jnp.sum(d_gate, axis=0, keepdims=True)
        dbv[...] += jnp.sum(d_val, axis=0, keepdims=True)
        for k in range(9):
            dwg[k:k + 1, :] += jnp.sum(taps_g[k] * d_gate, axis=0, keepdims=True)
            dwv[k:k + 1, :] += jnp.sum(taps_v[k] * d_val, axis=0, keepdims=True)

    shp = jax.ShapeDtypeStruct
    return _call(body, name="conv_act_bwd", grid=(width // cb, n_tok // tb),
                 in_specs=[prev, main, nxt, prev, main, nxt, main, main, main],
                 out_specs=[main, main, par(9), par(9), par(1), par(1)],
                 out_shape=[shp((n_tok, width), BF16), shp((n_tok, width), BF16), shp((9, width), F32),
                            shp((9, width), F32), shp((1, width), F32), shp((1, width), F32)],
                 compiler_params=_params(2))(up_g, up_g, up_g, up_v, up_v, up_v, gate_c, val_c, d_act)


def _conv_transposed(x, w, name):
    n_tok, width = x.shape
    tb, cb, main, prev, nxt, par = _conv_geometry(n_tok, width, 2048)
    n_blk = n_tok // tb

    def body(xp, xm, xn, w_ref, o_ref):
        taps = _conv_taps(xp[...], xm[...], xn[...], pl.program_id(1), n_blk)
        o_ref[...] = _conv_sum(taps, w_ref, flip=True).astype(o_ref.dtype)

    return _call(body, name=name, grid=(width // cb, n_tok // tb), in_specs=[prev, main, nxt, par(9)],
                 out_specs=main, out_shape=jax.ShapeDtypeStruct((n_tok, width), BF16),
                 compiler_params=_params(2))(x, x, x, w)


def _adamw(w, g_parts, m, v, name, own=None, me=None):
    rows, cols = w.shape
    parts = g_parts.shape[0]
    blk = _pick(rows, 256, 8)
    spec = pl.BlockSpec((blk, cols), lambda i: (i, 0))
    has_own = own is not None

    def body(*refs):
        w_ref, g_ref, m_ref, v_ref = refs[:4]
        g_out, d_out, m_out, v_out = refs[-4:]

        def part(p):
            if has_own:
                return jnp.where(refs[5][...] == p, refs[4][...], g_ref[p]).astype(F32)
            return g_ref[p].astype(F32)

        g = part(0)
        for p in range(1, parts):
            g = g + part(p)
        m_new = ADAM_B1 * m_ref[...] + (1.0 - ADAM_B1) * g
        v_new = ADAM_B2 * v_ref[...] + (1.0 - ADAM_B2) * (g * g)
        m_hat = m_new / (1.0 - ADAM_B1 ** ADAM_STEP)
        v_hat = v_new / (1.0 - ADAM_B2 ** ADAM_STEP)
        g_out[...] = g
        d_out[...] = -ADAM_LR * (m_hat / (jnp.sqrt(v_hat) + ADAM_EPS) + ADAM_WD * w_ref[...])
        m_out[...] = m_new
        v_out[...] = v_new

    extra = [own, me] if has_own else []
    return _call(body, name=name, grid=(rows // blk,),
                 in_specs=([spec, pl.BlockSpec((parts, blk, cols), lambda i: (0, i, 0)), spec, spec]
                           + ([spec, _whole_spec((1, 1))] if has_own else [])),
                 out_specs=[spec] * 4, out_shape=[jax.ShapeDtypeStruct((rows, cols), F32)] * 4,
                 compiler_params=_params(1))(w, g_parts, m, v, *extra)


def _exchange(items, name, after=()):
    n = len(items)
    n_after = len(after)
    hbm = pl.BlockSpec(memory_space=pl.ANY)

    def body(*refs):
        srcs, outs = refs[:n], refs[n + n_after:2 * n + n_after]
        send_sems, recv_sems, own_sems = refs[2 * n + n_after:]
        x, y, c = lax.axis_index("x"), lax.axis_index("y"), lax.axis_index("c")
        me = 4 * x + 2 * y + c
        own = []
        for i, (_, mode) in enumerate(items):
            src = srcs[i] if mode == "gather" else srcs[i].at[me]
            cp = pltpu.make_async_copy(src, outs[i].at[me], own_sems.at[i])
            cp.start()
            own.append(cp)
        sent = []
        for i, (_, mode) in enumerate(items):
            for k in range(1, N_DEV):
                px = 1 - x if k & 4 else x
                py = 1 - y if k & 2 else y
                pc = 1 - c if k & 1 else c
                peer = 4 * px + 2 * py + pc
                src = srcs[i] if mode == "gather" else srcs[i].at[peer]
                cp = pltpu.make_async_remote_copy(
                    src_ref=src, dst_ref=outs[i].at[me], send_sem=send_sems.at[i, k - 1],
                    recv_sem=recv_sems.at[i, k - 1], device_id=(px, py, pc), device_id_type=pl.DeviceIdType.MESH)
                cp.start()
                landing = pltpu.make_async_remote_copy(
                    src_ref=src, dst_ref=outs[i].at[peer], send_sem=send_sems.at[i, k - 1],
                    recv_sem=recv_sems.at[i, k - 1], device_id=(px, py, pc), device_id_type=pl.DeviceIdType.MESH)
                sent.append((cp, landing))
        for cp in own:
            cp.wait()
        for cp, landing in sent:
            cp.wait_send()
            landing.wait_recv()

    out_shape = [jax.ShapeDtypeStruct((N_DEV,) + (a.shape if mode == "gather" else a.shape[1:]), a.dtype)
                 for a, mode in items]
    return _call(body, name=name, in_specs=[hbm] * (n + n_after), out_specs=[hbm] * n, out_shape=out_shape,
                 scratch_shapes=[pltpu.SemaphoreType.DMA((n, N_DEV - 1)), pltpu.SemaphoreType.DMA((n, N_DEV - 1)),
                                 pltpu.SemaphoreType.DMA((n,))])(*[a for a, _ in items], *after)


def _peer(k, x, y, c):
    px = 1 - x if k & 4 else x
    py = 1 - y if k & 2 else y
    pc = 1 - c if k & 1 else c
    return (px, py, pc), 4 * px + 2 * py + pc


_HBM_SPEC = pl.BlockSpec(memory_space=pltpu.HBM)
_SEM_SPEC = pl.BlockSpec(memory_space=pltpu.SEMAPHORE)
_SPLIT_EFFECT = pltpu.SideEffectType.DATAFLOW_SIDE_EFFECTING


def _exchange_begin(items, name, after=None):
    n = len(items)
    modes = [mode for _, mode in items]
    srcs = [pltpu.with_memory_space_constraint(a, pltpu.HBM) for a, _ in items]
    lands = [pltpu.with_memory_space_constraint(
        lax.empty((N_DEV,) + (a.shape if mode == "gather" else a.shape[1:]), a.dtype), pltpu.HBM)
        for a, mode in items]

    def body(*refs):
        src_refs, land_refs = refs[:n], refs[n:2 * n]
        token = refs[-1]
        send_sems, recv_sems = refs[-2 * n - 3], refs[-2 * n - 2]
        x, y, c = lax.axis_index("x"), lax.axis_index("y"), lax.axis_index("c")
        me = 4 * x + 2 * y + c
        for i in range(n):
            for k in range(1, N_DEV):
                coords, peer = _peer(k, x, y, c)
                src = src_refs[i] if modes[i] == "gather" else src_refs[i].at[peer]
                pltpu.make_async_remote_copy(
                    src_ref=src, dst_ref=land_refs[i].at[me], send_sem=send_sems.at[i * (N_DEV - 1) + k - 1],
                    recv_sem=recv_sems.at[i * (N_DEV - 1) + k - 1], device_id=coords,
                    device_id_type=pl.DeviceIdType.MESH).start()
        token[...] = jnp.zeros_like(token)

    sems = pltpu.SemaphoreType.DMA((n * (N_DEV - 1),))
    order = [] if after is None else [after]
    res = _call(body, name=name,
                out_shape=(sems, sems, *[pltpu.HBM(a.shape, a.dtype) for a in srcs + lands],
                           jax.ShapeDtypeStruct((8, LANE), F32)),
                in_specs=[_HBM_SPEC] * (2 * n) + [pl.BlockSpec(memory_space=pl.ANY)] * len(order),
                out_specs=(_SEM_SPEC, _SEM_SPEC, *[_HBM_SPEC] * (2 * n), pl.BlockSpec(memory_space=pltpu.VMEM)),
                input_output_aliases={i: 2 + i for i in range(2 * n)},
                compiler_params=pltpu.CompilerParams(has_side_effects=_SPLIT_EFFECT))(*srcs, *lands, *order)
    return (modes, res[0], res[1], list(res[2:2 + n]), list(res[2 + n:2 + 2 * n])), res[-1][0, 0]


def _exchange_end(handle, after, name):
    modes, send_sems, recv_sems, srcs, lands = handle
    n = len(modes)

    def wait_body(*refs):
        src_refs, land_refs = refs[:n], refs[n:2 * n]
        send, recv = refs[2 * n], refs[2 * n + 1]
        x, y, c = lax.axis_index("x"), lax.axis_index("y"), lax.axis_index("c")
        for i in range(n):
            for k in range(1, N_DEV):
                coords, peer = _peer(k, x, y, c)
                src = src_refs[i] if modes[i] == "gather" else src_refs[i].at[peer]
                cp = pltpu.make_async_remote_copy(
                    src_ref=src, dst_ref=land_refs[i].at[peer], send_sem=send.at[i * (N_DEV - 1) + k - 1],
                    recv_sem=recv.at[i * (N_DEV - 1) + k - 1], device_id=coords,
                    device_id_type=pl.DeviceIdType.MESH)
                cp.wait_send()
                cp.wait_recv()

    res = _call(wait_body, name=name, out_shape=[pltpu.HBM(a.shape, a.dtype) for a in srcs + lands],
                in_specs=[_HBM_SPEC] * (2 * n) + [_SEM_SPEC, _SEM_SPEC, pl.BlockSpec(memory_space=pl.ANY)],
                out_specs=[_HBM_SPEC] * (2 * n), input_output_aliases={i: i for i in range(2 * n)},
                compiler_params=pltpu.CompilerParams(has_side_effects=_SPLIT_EFFECT))(
                    *srcs, *lands, send_sems, recv_sems, after)
    return res[n:]


def _with_own(land, own, me):
    slot = lax.broadcasted_iota(jnp.int32, (N_DEV,) + (1,) * (land.ndim - 1), 0)
    return jnp.where(slot == me, own[None], land)


def _cols_from_blocks(g):
    return jnp.transpose(g, (1, 0, 2)).reshape(g.shape[1], N_DEV * g.shape[2])


def _blocks_from_cols(w):
    r, c8 = w.shape
    return jnp.transpose(w.reshape(r, N_DEV, c8 // N_DEV), (1, 0, 2))


def _pack(arrs):
    flat = jnp.concatenate([a.reshape(-1).astype(F32) for a in arrs])
    rows = -(-flat.shape[0] // LANE)
    rows = -(-rows // PACK_ROWS) * PACK_ROWS if rows > PACK_ROWS else -(-rows // 8) * 8
    return jnp.pad(flat, (0, rows * LANE - flat.shape[0])).reshape(rows, LANE)


def _unpack(packed, shapes):
    flat = packed.reshape(-1)
    out, off = [], 0
    for s in shapes:
        size = math.prod(s)
        out.append(flat[off:off + size].reshape(s))
        off += size
    return out


def kernel(x, c, ctx, c_ctx, w_ada, b_ada, g_mix, w_in, s5_a_re, s5_a_im, s5_log_step, s5_b_re, s5_b_im, s5_c_re, s5_c_im, s5_d, s5_w_glu, s5_b_glu, sgu_ln_g, sgu_ln_b, sgu_w, sgu_b, w_proj_a, w_proj_b, b_gate, w_out, g_ffn, w_up, conv_w, conv_b, w_down, g_final, loss_target, m_c_ctx, m_w_ada, m_b_ada, m_g_mix, m_w_in, m_s5_a_re, m_s5_a_im, m_s5_log_step, m_s5_b_re, m_s5_b_im, m_s5_c_re, m_s5_c_im, m_s5_d, m_s5_w_glu, m_s5_b_glu, m_sgu_ln_g, m_sgu_ln_b, m_sgu_w, m_sgu_b, m_w_proj_a, m_w_proj_b, m_b_gate, m_w_out, m_g_ffn, m_w_up, m_conv_w, m_conv_b, m_w_down, m_g_final, v_c_ctx, v_w_ada, v_b_ada, v_g_mix, v_w_in, v_s5_a_re, v_s5_a_im, v_s5_log_step, v_s5_b_re, v_s5_b_im, v_s5_c_re, v_s5_c_im, v_s5_d, v_s5_w_glu, v_s5_b_glu, v_sgu_ln_g, v_sgu_ln_b, v_sgu_w, v_sgu_b, v_w_proj_a, v_w_proj_b, v_b_gate, v_w_out, v_g_ffn, v_w_up, v_conv_w, v_conv_b, v_w_down, v_g_final):
    given = dict(locals())
    wts = {n: given[n] for n in WEIGHTS}
    mom1 = {n: given["m_" + n] for n in WEIGHTS}
    mom2 = {n: given["v_" + n] for n in WEIGHTS}

    me = 4 * lax.axis_index("x") + 2 * lax.axis_index("y") + lax.axis_index("c")
    xs, cx, tgt = x[0], ctx[0], loss_target[0]
    n_tok, d = xs.shape
    n_ctx = cx.shape[0]
    s5w = s5_d.shape[1]
    ffn = w_down.shape[1] * N_DEV
    n_mod = w_ada.shape[2] * N_DEV // d
    mod_cols = w_ada.shape[2]

    def two_d(a):
        return a.reshape(-1, a.shape[-1])

    conv_w9 = conv_w[0].reshape(9, -1)
    (c_blocks,) = _exchange([(c, "gather")], "gather_c")
    c_all = c_blocks.reshape(N_DEV, d)

    cs_in = jnp.concatenate([c_all, jnp.broadcast_to(c_ctx[None, :], (N_DEV, d))], axis=0)
    w_ada_loc = w_ada[0]
    (mod_mine,) = _stage_fwd(_fn_mod, [cs_in], [w_ada_loc], [(mod_cols, F32)], blk=2 * N_DEV, name="mod_fwd")
    (mod_blocks,) = _exchange([(mod_mine, "gather")], "gather_mod")
    w_in_own = _b16(w_in[0])
    in_weights, tok_in = _exchange_begin([(w_in_own, "gather")], "gather_in_begin", after=mod_blocks)
    mid_own = [_b16(s5_w_glu[0]), _b16(w_proj_a[0]), _b16(w_proj_b[0])]
    mid_weights, tok_mid = _exchange_begin([(a, "gather") for a in mid_own], "gather_mid_begin",
                                           after=mod_blocks + tok_in)
    late_own = [_b16(w_out[0]), _b16(w_up[0]), conv_w9, _b16(w_down[0])]
    late_weights, tok = _exchange_begin([(a, "gather") for a in late_own], "gather_late_begin",
                                        after=mod_blocks + tok_mid)
    mod_all = _cols_from_blocks(mod_blocks) + b_ada + tok
    mod = lax.dynamic_slice_in_dim(mod_all, me, 1, axis=0)
    mod_c = mod_all[N_DEV:N_DEV + 1]
    sh1, sc1, ga1, sh2, sc2, ga2 = [mod[:, i * d:(i + 1) * d] for i in range(n_mod)]
    sh1c, sc1c = mod_c[:, :d], mod_c[:, d:2 * d]

    a_par = [g_mix, sh1, sc1]
    ac_par = [g_mix, sh1c, sc1c]
    (h,) = _stage_fwd(_fn_a, [xs], a_par, [(d, BF16)], blk=_pick(n_tok, 512, 8), name="modulate1_fwd")
    (hc,) = _stage_fwd(_fn_a, [cx], ac_par, [(d, BF16)], blk=_pick(n_ctx, 512, 8), name="modulate1_ctx_fwd")
    (w_in_land,) = _exchange_end(in_weights, h, "gather_in_end")
    w_in_f = _cols_from_blocks(_with_own(w_in_land, w_in_own, me))
    w_in_u, w_in_rest = w_in_f[:, :s5w], w_in_f[:, s5w:]
    pu = _mm(h, w_in_u, name="proj_u")
    prest = _mm(h, w_in_rest, name="proj_rest")
    puc = _mm(hc, w_in_u, name="proj_u_ctx")

    n_state = s5_a_re.shape[-1]

    def twice(a):
        return jnp.concatenate([a, a], axis=-1)

    s5_prm = [twice(s5_a_re[0])[:, :, None, :], twice(s5_a_im[0])[:, :, None, :], s5_log_step[0][:, :, None, None],
              jnp.concatenate([jnp.swapaxes(s5_b_re[0], 2, 3), jnp.swapaxes(s5_b_im[0], 2, 3)], axis=-1),
              jnp.concatenate([s5_c_re[0], s5_c_im[0]], axis=-1)]
    pu_g, puc_g = _to_groups(_b16(pu)), _to_groups(puc)
    ysc = _from_groups(_s5_fwd(pu_g, puc_g, s5_prm))

    mid = [_with_own(land, own, me) for land, own in zip(_exchange_end(mid_weights, ysc, "gather_mid_end"), mid_own)]
    w_glu_f, w_pa_f, w_pb_f = mid[0].reshape(-1, s5w), _cols_from_blocks(mid[1]), _cols_from_blocks(mid[2])
    b_par = [s5_d, w_glu_f, s5_b_glu, sgu_ln_g, sgu_ln_b, two_d(sgu_w[0]), jnp.transpose(sgu_b[0]),
             w_pa_f, w_pb_f, b_gate]
    b_rows = [pu, prest, ysc]
    b_blk = _pick(n_tok, 512, CHUNK)
    (mpre,) = _stage_fwd(_fn_b, b_rows, b_par, [(d, BF16)], blk=b_blk, name="mixers_fwd")
    late = [_with_own(land, own, me) for land, own in zip(_exchange_end(late_weights, mpre, "gather_late_end"),
                                                          late_own)]
    w_out_f = late[0].reshape(-1, d)
    w_up_f = _cols_from_blocks(late[1])
    w_up_g, w_up_v = w_up_f[:, :ffn], w_up_f[:, ffn:]
    conv_w_f = _cols_from_blocks(late[2])
    w_down_f = late[3].reshape(-1, d)
    mo = _mm(mpre, w_out_f, name="out_proj")

    c_par = [ga1, g_ffn, sh2, sc2]
    c_blk = _pick(n_tok, 512, 8)
    x1, h2 = _stage_fwd(_fn_c, [xs, mo], c_par, [(d, F32), (d, BF16)], blk=c_blk, name="modulate2_fwd")
    up_g = _mm(h2, w_up_g, out_dtype=BF16, name="up_gate")
    up_v = _mm(h2, w_up_v, out_dtype=BF16, name="up_val")
    cw_g, cw_v = conv_w_f[:, :ffn], conv_w_f[:, ffn:]
    cb_g, cb_v = conv_b[:, :ffn], conv_b[:, ffn:]
    act, gate_c, val_c = _conv_act_fwd(up_g, up_v, cw_g, cw_v, cb_g, cb_v)
    dn = _mm(act, w_down_f, name="down_proj")

    loss_part, d_x1a, d_dn, d_ga2, d_g_final = _stage_loss(
        _fn_e, [x1, dn, tgt], [ga2, g_final[None, :]], blk=c_blk, name="loss_head",
        row_grads=[(0, F32), (1, BF16)])

    d_act = _mm(d_dn, w_down_f, tb=True, out_dtype=BF16, name="down_proj_dx")
    g_w_down = _mm(act, d_dn, ta=True, out_dtype=BF16, name="down_proj_dw")

    own_block = {}

    def grad_blocks(name, g):
        blocks = _b16(_blocks_from_cols(g) if name in COL_SHARDED
                      else g.reshape(N_DEV, g.shape[0] // N_DEV, g.shape[1]))
        own_block[name] = lax.dynamic_index_in_dim(blocks, me, 0, keepdims=False)
        return blocks, "a2a"

    sent_down, tok = _exchange_begin([grad_blocks('w_down', g_w_down)], "grads_down_begin")
    dcg, dcv, g_cw_g, g_cw_v, g_cb_g, g_cb_v = _conv_act_bwd(up_g, up_v, gate_c, val_c, d_act)
    dug = _conv_transposed(dcg, cw_g + tok, "conv_dx_gate")
    duv = _conv_transposed(dcv, cw_v, "conv_dx_val")
    d_h2 = _mm(dug, w_up_g, tb=True, name="up_gate_dx")
    d_h2 = _mm(duv, w_up_v, tb=True, add=d_h2, out_dtype=BF16, name="up_val_dx")
    g_w_up = jnp.concatenate([_mm(h2, dug, ta=True, out_dtype=BF16, name="up_gate_dw"),
                              _mm(h2, duv, ta=True, out_dtype=BF16, name="up_val_dw")], axis=1)
    sent_up, tok = _exchange_begin(
        [grad_blocks('w_up', g_w_up), grad_blocks('conv_w', jnp.concatenate([g_cw_g, g_cw_v], axis=1))],
        "grads_up_begin")
    (d_xc, d_mo), (d_ga1, g_g_ffn, d_sh2, d_sc2) = _split(_stage_bwd(
        _fn_c, [xs, mo], [ga1 + tok] + c_par[1:], [d_x1a, d_h2], blk=c_blk, name="modulate2_bwd",
        row_grads=[(0, F32), (1, BF16)]), 2)

    d_mpre = _mm(d_mo, w_out_f, tb=True, out_dtype=BF16, name="out_proj_dx")
    g_w_out = _mm(mpre, d_mo, ta=True, out_dtype=BF16, name="out_proj_dw")
    (d_prest, d_ysc), b_grads = _split(_stage_bwd(
        _fn_b, b_rows, b_par, [d_mpre], blk=_pick(n_tok, 256, CHUNK), name="mixers_bwd",
        row_grads=[(1, BF16), (2, F32)]), 2)
    (g_s5_d, g_w_glu, g_b_glu, g_ln_g, g_ln_b, g_sgu_w, g_sgu_bt, g_w_pa, g_w_pb, g_b_gate) = b_grads

    sent_mix, tok = _exchange_begin(
        [grad_blocks('w_out', g_w_out), grad_blocks('s5_w_glu', g_w_glu), grad_blocks('w_proj_a', g_w_pa),
         grad_blocks('w_proj_b', g_w_pb)], "grads_mixer_begin")
    skip_g = jnp.tile(s5_d.reshape(-1, 1, S5_H), (1, 1, S5_T)) + tok
    s5_out = _s5_bwd(pu_g, puc_g, s5_prm, _to_groups(d_ysc), skip_g)
    g_a_re2, g_a_im2, g_ls, g_bt2, g_c2 = s5_out[2:]
    g_a_re = g_a_re2[..., :n_state] + g_a_re2[..., n_state:]
    g_a_im = g_a_im2[..., :n_state] + g_a_im2[..., n_state:]
    g_bt_re, g_bt_im = g_bt2[..., :n_state], g_bt2[..., n_state:]
    g_c_re, g_c_im = g_c2[..., :n_state], g_c2[..., n_state:]

    part = {
        's5_a_re': g_a_re, 's5_a_im': g_a_im, 's5_log_step': g_ls,
        's5_b_re': jnp.swapaxes(g_bt_re, 2, 3), 's5_b_im': jnp.swapaxes(g_bt_im, 2, 3),
        's5_c_re': g_c_re, 's5_c_im': g_c_im, 's5_d': g_s5_d, 's5_b_glu': g_b_glu, 'sgu_ln_g': g_ln_g,
        'sgu_ln_b': g_ln_b, 'sgu_w': g_sgu_w, 'sgu_b': jnp.transpose(g_sgu_bt), 'b_gate': g_b_gate,
        'g_ffn': g_g_ffn, 'conv_b': jnp.concatenate([g_cb_g, g_cb_v], axis=1), 'g_final': d_g_final,
    }
    early = [n for n in REPLICATED if n not in LATE_REPLICATED and n not in UNPACKED_REPLICATED]
    early_part = _pack([part[n] for n in early])
    own_small = {n: two_d(part[n]) for n in UNPACKED_REPLICATED}
    sent_small, tok = _exchange_begin(
        [(early_part, "gather")] + [(own_small[n], "gather") for n in UNPACKED_REPLICATED], "grads_small_begin")
    d_pu, d_puc = _from_groups(s5_out[0]), _from_groups(s5_out[1]) + tok

    g_w_in_u = _mm(hc, d_puc, ta=True, name="proj_u_ctx_dw")
    g_w_in_u = _mm(h, d_pu, ta=True, add=g_w_in_u, out_dtype=BF16, name="proj_u_dw")
    g_w_in = jnp.concatenate([g_w_in_u, _mm(h, d_prest, ta=True, out_dtype=BF16, name="proj_rest_dw")], axis=1)
    sent_in, tok = _exchange_begin([grad_blocks('w_in', g_w_in)], "grads_in_begin")
    w_in_u_behind = w_in_u + _b16(tok)
    d_h = _mm(d_pu, w_in_u_behind, tb=True, name="proj_u_dx")
    d_h = _mm(d_prest, w_in_rest, tb=True, add=d_h, out_dtype=BF16, name="proj_rest_dx")
    d_hc = _mm(d_puc, w_in_u_behind, tb=True, out_dtype=BF16, name="proj_u_ctx_dx")

    (grad_x,), (g_g_mix_x, d_sh1, d_sc1) = _split(_stage_bwd(
        _fn_a_res, [xs], a_par, [d_h, d_xc], blk=c_blk, name="modulate1_bwd", row_grads=[(0, F32)]), 1)
    _, (g_g_mix_c, d_sh1c, d_sc1c) = _split(_stage_bwd(
        _fn_a, [cx], ac_par, [d_hc], blk=_pick(n_ctx, 512, 8), name="modulate1_ctx_bwd", row_grads=[]), 0)

    out, summed = {}, {}
    me_arr = me.reshape(1, 1).astype(jnp.int32)

    def adamw_sharded(names):
        for n in names:
            res = _adamw(two_d(wts[n]), summed[n], two_d(mom1[n]), two_d(mom2[n]), "adamw_" + n, own=own_block[n],
                         me=me_arr)
            out[n] = tuple(r.reshape(wts[n].shape) for r in res)

    (summed['w_down'],) = _exchange_end(sent_down, grad_x, "grads_down_end")
    summed['w_up'], summed['conv_w'] = _exchange_end(sent_up, grad_x, "grads_up_end")
    summed['w_out'], summed['s5_w_glu'], summed['w_proj_a'], summed['w_proj_b'] = _exchange_end(
        sent_mix, grad_x, "grads_mixer_end")
    small_parts = _exchange_end(sent_small, grad_x, "grads_small_end")
    early_parts = small_parts[0]
    for n, parts in zip(UNPACKED_REPLICATED, small_parts[1:]):
        summed[n], own_block[n] = parts, own_small[n]
    first = list(summed)
    adamw_sharded(first)

    def adamw_packed(names, parts, own, tag):
        res = _adamw(_pack([wts[n] for n in names]), parts, _pack([mom1[n] for n in names]),
                     _pack([mom2[n] for n in names]), "adamw_replicated_" + tag, own=own,
                     me=None if own is None else me_arr)
        res = [_unpack(r, [wts[n].shape for n in names]) for r in res]
        for i, n in enumerate(names):
            out[n] = tuple(r[i] for r in res)
        return res[1][0]

    early_done = adamw_packed(early, early_parts, early_part, "early")

    zeros = jnp.zeros((1, (n_mod - 2) * d), F32)
    d_mod = jnp.concatenate([d_sh1, d_sc1, d_ga1, d_sh2, d_sc2, d_ga2], axis=1)
    d_mod_c = jnp.concatenate([d_sh1c, d_sc1c, zeros], axis=1)
    (d_mod_all,) = _exchange([(jnp.concatenate([d_mod, d_mod_c], axis=0), "gather")], "gather_dmod",
                             after=[out[n][1] for n in first] + [early_done])
    d_mod_rows = jnp.transpose(d_mod_all, (1, 0, 2)).reshape(2 * N_DEV, n_mod * d)
    d_mod_mine = lax.dynamic_slice_in_dim(d_mod_rows, me * mod_cols, mod_cols, axis=1)
    (d_cs,), (g_w_ada,) = _split(_stage_bwd(
        _fn_mod, [cs_in], [w_ada_loc], [d_mod_mine], blk=2 * N_DEV, name="mod_bwd", row_grads=[(0, F32)]), 1)

    part.update({'c_ctx': jnp.sum(d_cs[N_DEV:], axis=0), 'b_ada': d_mod + d_mod_c, 'g_mix': g_g_mix_x + g_g_mix_c})
    late_parts, loss_parts = _exchange(
        [(_pack([part[n] for n in LATE_REPLICATED]), "gather"), (jnp.broadcast_to(loss_part, (8, LANE)), "gather")],
        "exchange_grads")

    (summed['w_in'],) = _exchange_end(sent_in, late_parts, "grads_in_end")
    adamw_sharded([n for n in summed if n not in first])

    adamw_packed(LATE_REPLICATED, late_parts, None, "late")
    res = _adamw(w_ada_loc, g_w_ada[None], m_w_ada[0], v_w_ada[0], "adamw_w_ada")
    out['w_ada'] = tuple(r.reshape(w_ada.shape) for r in res)

    loss = jnp.sum(loss_parts[:, 0, 0])
    return (loss, grad_x[None], *[out[n][0] for n in WEIGHTS], *[out[n][1] for n in WEIGHTS],
            *[out[n][2] for n in WEIGHTS], *[out[n][3] for n in WEIGHTS])


def _split(res, n_rows):
    return tuple(res[:n_rows]), tuple(res[n_rows:])
```
